```python
import math
import jax, jax.numpy as jnp
from jax import lax
import numpy as np

D_MODEL = 1024
BATCH = 16
SEQ = 2048
DEPTH = 1

HEAD_DIM = 64
ROPE_THETA = 500000.0
ROPE_FRACTION = 4
A_HEADS = 8
A_KV_HEADS = 2
A_REP = A_HEADS // A_KV_HEADS
A_WIDTH = A_HEADS * HEAD_DIM
A_KV_WIDTH = A_KV_HEADS * HEAD_DIM
IDX_HEADS = 8
IDX_DIM = 32
DSA_TOPK_MAX = 256
A_QBLOCK = 64
B_HEADS = 8
B_KV_HEADS = 2
B_REP = B_HEADS // B_KV_HEADS
B_WIDTH = B_HEADS * HEAD_DIM
B_KV_WIDTH = B_KV_HEADS * HEAD_DIM
CMP_BLOCK = 32
CMP_STRIDE = 16
CMP_HIDDEN = 256
SEL_BLOCK = 64
SEL_COUNT = 16
SEL_LOCAL = 2
WINDOW = 512
SEL_QBLOCK = 16
WIN_QBLOCK = 128
N_EXPERTS = 256
TOP_K = 8
N_GROUPS = 8
TOPK_GROUPS = 4
EXPERT_DIM = 256
SHARED_DIM = 256
ROUTED_SCALE = 2.5
MOE_BLOCK = 128
DN_ALPHA = (2 * DEPTH) ** 0.25
DN_BETA = (8 * DEPTH) ** -0.25
LN_EPS = 1e-5
NEG = -1e30
FORCE = 1e9

IN_WIDTHS = (A_WIDTH, A_KV_WIDTH, A_KV_WIDTH, IDX_HEADS * IDX_DIM, IDX_DIM, IDX_HEADS,
             B_WIDTH, B_KV_WIDTH, B_KV_WIDTH, B_KV_WIDTH, B_KV_WIDTH, B_KV_WIDTH, B_KV_WIDTH, 3 * B_HEADS,
             D_MODEL, D_MODEL)
VALUE_SLOTS = (2, 8, 10, 12)
N_IN = sum(IN_WIDTHS)

kernel_name = 'hybrid_dsa_nsa_moe_block'


def _layer_norm(x, g, b):
    xf = x.astype(jnp.float32)
    mu = xf.mean(-1, keepdims=True)
    var = jnp.square(xf - mu).mean(-1, keepdims=True)
    return ((xf - mu) * lax.rsqrt(var + LN_EPS) * g.astype(jnp.float32) + b.astype(jnp.float32)).astype(x.dtype)


def _masked_softmax(s, mask):
    s = jnp.where(mask, s.astype(jnp.float32), NEG)
    e = jnp.exp(s - s.max(-1, keepdims=True)) * mask
    return e / jnp.maximum(e.sum(-1, keepdims=True), 1e-30)


def _rope_partial(x, positions):
    d = x.shape[-1]
    rot = d // ROPE_FRACTION
    half = rot // 2
    inv = ROPE_THETA ** (-(jnp.arange(half, dtype=jnp.float32) * 2.0) / rot)
    ang = positions.astype(jnp.float32)[..., None] * inv
    ang = ang.reshape(ang.shape[:2] + (1,) * (x.ndim - 3) + (half,))
    cos, sin = jnp.cos(ang), jnp.sin(ang)
    x1 = x[..., :half].astype(jnp.float32)
    x2 = x[..., half:rot].astype(jnp.float32)
    return jnp.concatenate([(x1 * cos - x2 * sin).astype(x.dtype), (x2 * cos + x1 * sin).astype(x.dtype), x[..., rot:]], axis=-1)


def _blocks_to_seq(out):
    out = jnp.swapaxes(out, 0, 1)
    return out.reshape((out.shape[0], out.shape[1] * out.shape[2]) + out.shape[3:])


def _dsa_attention(q, k, v, q_idx, k_idx, w_idx, positions):
    bsz, seq, _ = q.shape
    q = _rope_partial(q.reshape(bsz, seq, A_HEADS, HEAD_DIM), positions).reshape(bsz, seq, A_KV_HEADS, A_REP, HEAD_DIM)
    k = _rope_partial(k.reshape(bsz, seq, A_KV_HEADS, HEAD_DIM), positions)
    v = v.reshape(bsz, seq, A_KV_HEADS, HEAD_DIM)
    qi = _rope_partial(q_idx.reshape(bsz, seq, IDX_HEADS, IDX_DIM), positions)
    ki = _rope_partial(k_idx, positions)
    wi = w_idx * IDX_HEADS ** -0.5
    n_keep = min(DSA_TOPK_MAX, seq // 4)
    key_pos = jnp.arange(seq)
    take = jax.vmap(lambda arr, idx: arr[idx])

    def block(i):
        start = i * A_QBLOCK
        qb = lax.dynamic_slice_in_dim(q, start, A_QBLOCK, axis=1)
        qib = lax.dynamic_slice_in_dim(qi, start, A_QBLOCK, axis=1)
        wib = lax.dynamic_slice_in_dim(wi, start, A_QBLOCK, axis=1)
        t_q = start + jnp.arange(A_QBLOCK)
        rel = jax.nn.relu(jnp.einsum('bqhd,bsd->bqhs', qib, ki) * IDX_DIM ** -0.5)
        score = jnp.einsum('bqhs,bqh->bqs', rel, wib).astype(jnp.float32)
        score = jnp.where(key_pos[None, None, :] <= t_q[None, :, None], score, -jnp.inf)
        _, idx = lax.top_k(score, n_keep)
        ks = take(k, idx)
        vs = take(v, idx)
        s = jnp.einsum('bqgrd,bqkgd->bqgrk', qb, ks) * HEAD_DIM ** -0.5
        valid = (idx <= t_q[None, :, None])[:, :, None, None, :]
        p = _masked_softmax(s, valid).astype(v.dtype)
        o = jnp.einsum('bqgrk,bqkgd->bqgrd', p, vs)
        return o.reshape(bsz, A_QBLOCK, A_WIDTH)

    return _blocks_to_seq(lax.map(block, jnp.arange(seq // A_QBLOCK)))


def _nsa_compress(a, pos_emb, w1, w2, tok):
    bsz = a.shape[0]
    blocks = a[:, tok] + pos_emb[None, None, :, None, :]
    flat = jnp.swapaxes(blocks, 2, 3).reshape(bsz, tok.shape[0], B_KV_HEADS, CMP_BLOCK * HEAD_DIM)
    return jax.nn.gelu(flat @ w1) @ w2


def _nsa_attention(q, k_cmp, v_cmp, k_sel, v_sel, k_win, v_win, g, positions,
                   cmp_pos_k, cmp_pos_v, cmp_k_w1, cmp_k_w2, cmp_v_w1, cmp_v_w2):
    bsz, seq, _ = q.shape
    dt = q.dtype
    scale = HEAD_DIM ** -0.5
    kv = lambda a: a.reshape(bsz, seq, B_KV_HEADS, HEAD_DIM)
    q = q.reshape(bsz, seq, B_HEADS, HEAD_DIM)
    q_raw = q.reshape(bsz, seq, B_KV_HEADS, B_REP, HEAD_DIM)
    q_rot = _rope_partial(q, positions).reshape(bsz, seq, B_KV_HEADS, B_REP, HEAD_DIM)
    t = np.arange(seq)

    n_c = (seq - CMP_BLOCK) // CMP_STRIDE + 1
    c_start = np.arange(n_c) * CMP_STRIDE
    tok = c_start[:, None] + np.arange(CMP_BLOCK)
    kc = _nsa_compress(kv(k_cmp), cmp_pos_k, cmp_k_w1, cmp_k_w2, tok)
    vc = _nsa_compress(kv(v_cmp), cmp_pos_v, cmp_v_w1, cmp_v_w2, tok)
    valid_c = (c_start[None, :] + CMP_BLOCK - 1) <= t[:, None]
    p_c = _masked_softmax(jnp.einsum('bsgrd,bcgd->bsgrc', q_raw, kc) * scale, valid_c[None, :, None, None, :])
    o_cmp = jnp.einsum('bsgrc,bcgd->bsgrd', p_c.astype(dt), vc)

    n_s = seq // SEL_BLOCK
    s_start = np.arange(n_s) * SEL_BLOCK
    overlap = ((c_start[:, None] <= s_start[None, :] + SEL_BLOCK - 1)
               & (c_start[:, None] + CMP_BLOCK - 1 >= s_start[None, :])).astype(np.float32)
    imp = jnp.einsum('bsgrc,cj->bsgj', p_c, overlap)
    blk = np.arange(n_s)
    cur = t // SEL_BLOCK
    forced = (blk[None, :] == 0) | ((cur[:, None] - blk[None, :] >= 0) & (cur[:, None] - blk[None, :] < SEL_LOCAL))
    causal = blk[None, :] * SEL_BLOCK <= t[:, None]
    imp = jnp.where(forced[None, :, None, :], FORCE, jnp.where(causal[None, :, None, :], imp, NEG))
    n_pick = min(SEL_COUNT, n_s)
    _, sel = lax.top_k(imp, n_pick)

    to_blocks = lambda a: a.reshape(bsz, n_s, SEL_BLOCK, B_KV_HEADS, HEAD_DIM).transpose(0, 3, 1, 2, 4)
    ksb = to_blocks(_rope_partial(kv(k_sel), positions))
    vsb = to_blocks(kv(v_sel))
    take2 = jax.vmap(jax.vmap(lambda arr, idx: arr[idx]))

    def sel_block(i):
        start = i * SEL_QBLOCK
        qb = lax.dynamic_slice_in_dim(q_rot, start, SEL_QBLOCK, axis=1)
        sb = lax.dynamic_slice_in_dim(sel, start, SEL_QBLOCK, axis=1)
        t_q = start + jnp.arange(SEL_QBLOCK)
        idx = sb.transpose(0, 2, 1, 3)
        kg = take2(ksb, idx)
        vg = take2(vsb, idx)
        s = jnp.einsum('bqgrd,bgqnld->bqgrnl', qb, kg) * scale
        tok_pos = sb[..., None] * SEL_BLOCK + jnp.arange(SEL_BLOCK)
        valid = (tok_pos <= t_q[None, :, None, None, None])[:, :, :, None]
        sh = s.shape
        p = _masked_softmax(s.reshape(sh[:4] + (-1,)), valid.reshape(valid.shape[:4] + (-1,))).reshape(sh)
        return jnp.einsum('bqgrnl,bgqnld->bqgrd', p.astype(dt), vg)

    o_sel = _blocks_to_seq(lax.map(sel_block, jnp.arange(seq // SEL_QBLOCK)))

    span = WINDOW + WIN_QBLOCK
    pad = ((0, 0), (WINDOW, 0), (0, 0), (0, 0))
    kwp = jnp.pad(_rope_partial(kv(k_win), positions), pad)
    vwp = jnp.pad(kv(v_win), pad)

    def win_block(i):
        start = i * WIN_QBLOCK
        qb = lax.dynamic_slice_in_dim(q_rot, start, WIN_QBLOCK, axis=1)
        kb = lax.dynamic_slice_in_dim(kwp, start, span, axis=1)
        vb = lax.dynamic_slice_in_dim(vwp, start, span, axis=1)
        t_q = start + jnp.arange(WIN_QBLOCK)
        s_pos = start - WINDOW + jnp.arange(span)
        diff = t_q[:, None] - s_pos[None, :]
        valid = (s_pos[None, :] >= 0) & (diff >= 0) & (diff < WINDOW)
        s = jnp.einsum('bqgrd,bkgd->bqgrk', qb, kb) * scale
        p = _masked_softmax(s, valid[None, :, None, None, :])
        return jnp.einsum('bqgrk,bkgd->bqgrd', p.astype(dt), vb)

    o_win = _blocks_to_seq(lax.map(win_block, jnp.arange(seq // WIN_QBLOCK)))

    gates = jax.nn.sigmoid(g.reshape(bsz, seq, B_KV_HEADS, B_REP, 3))
    o = gates[..., 0:1] * o_cmp + gates[..., 1:2] * o_sel + gates[..., 2:3] * o_win
    return o.reshape(bsz, seq, B_WIDTH)


def _token_mixer(u, positions, w_in, w_br_a, w_br_b, w_out,
                 cmp_pos_k, cmp_pos_v, cmp_k_w1, cmp_k_w2, cmp_v_w1, cmp_v_w2):
    z = u @ w_in
    offsets = np.cumsum(IN_WIDTHS)[:-1].tolist()
    (q_a, k_a, v_a, q_idx, k_idx, w_idx, q_b, k_cmp, v_cmp, k_sel, v_sel, k_win, v_win,
     g_nsa, gate_a, gate_b) = jnp.split(z, offsets, axis=-1)
    o_a = _dsa_attention(q_a, k_a, v_a, q_idx, k_idx, w_idx, positions)
    o_b = _nsa_attention(q_b, k_cmp, v_cmp, k_sel, v_sel, k_win, v_win, g_nsa, positions,
                         cmp_pos_k, cmp_pos_v, cmp_k_w1, cmp_k_w2, cmp_v_w1, cmp_v_w2)
    merged = jax.nn.sigmoid(gate_a) * (o_a @ w_br_a) + jax.nn.sigmoid(gate_b) * (o_b @ w_br_b)
    return merged @ w_out


def _routed_experts(h, idx, wts, w_gate, w_up, w_down):
    n_tok, _ = h.shape
    n_asg = n_tok * TOP_K
    e_flat = idx.reshape(-1)
    order = jnp.argsort(e_flat)
    e_sorted = e_flat[order]
    t_sorted = (order // TOP_K).astype(jnp.int32)
    w_sorted = wts.reshape(-1)[order].astype(h.dtype)
    counts = jnp.bincount(e_flat, length=N_EXPERTS)
    starts = jnp.cumsum(counts) - counts
    padded = (counts + MOE_BLOCK - 1) // MOE_BLOCK * MOE_BLOCK
    p_ends = jnp.cumsum(padded)
    dest = (p_ends - padded)[e_sorted] + jnp.arange(n_asg) - starts[e_sorted]
    n_blocks = -(-n_asg // MOE_BLOCK) + N_EXPERTS
    cap = n_blocks * MOE_BLOCK
    tok_buf = jnp.zeros((cap,), jnp.int32).at[dest].set(t_sorted)
    w_buf = jnp.zeros((cap,), h.dtype).at[dest].set(w_sorted)
    blk_e = jnp.minimum(jnp.searchsorted(p_ends, jnp.arange(n_blocks) * MOE_BLOCK, side='right'), N_EXPERTS - 1)

    def step(acc, xs):
        tb, wb, e = xs
        hb = h[tb]
        y = (jax.nn.silu(hb @ w_gate[e]) * (hb @ w_up[e])) @ w_down[e]
        return acc.at[tb].add(y * wb[:, None]), None

    out, _ = lax.scan(step, jnp.zeros_like(h),
                      (tok_buf.reshape(n_blocks, MOE_BLOCK), w_buf.reshape(n_blocks, MOE_BLOCK), blk_e))
    return out


def _moe_ffn(u, w_router, router_bias, w_exp_gate, w_exp_up, w_exp_down, w_sh_gate, w_sh_up, w_sh_down):
    bsz, seq, d = u.shape
    h = u.reshape(bsz * seq, d)
    n_tok = h.shape[0]
    scores = jax.nn.sigmoid((h @ w_router).astype(jnp.float32))
    choice = scores + router_bias.astype(jnp.float32)
    grp = choice.reshape(n_tok, N_GROUPS, N_EXPERTS // N_GROUPS)
    grp_score = lax.top_k(grp, 2)[0].sum(-1)
    _, top_g = lax.top_k(grp_score, TOPK_GROUPS)
    keep = jax.nn.one_hot(top_g, N_GROUPS, dtype=jnp.float32).sum(1) > 0
    masked = jnp.where(keep[:, :, None], grp, NEG).reshape(n_tok, N_EXPERTS)
    _, idx = lax.top_k(masked, TOP_K)
    w = jnp.take_along_axis(scores, idx, axis=-1)
    w = w / w.sum(-1, keepdims=True) * ROUTED_SCALE
    routed = _routed_experts(h, idx, w, w_exp_gate, w_exp_up, w_exp_down)
    shared = (jax.nn.silu(h @ w_sh_gate) * (h @ w_sh_up)) @ w_sh_down
    return (routed + shared).reshape(bsz, seq, d)


def setup_inputs(seed: int = 0) -> dict:
    key = jax.random.key(seed)
    ks = jax.random.split(key, 27)
    nrm = lambda k, shape, s: jax.random.normal(k, shape, jnp.float32) * s
    D = D_MODEL
    col_scale = np.concatenate([np.full((w,), DN_BETA if i in VALUE_SLOTS else 1.0, np.float32)
                                for i, w in enumerate(IN_WIDTHS)])
    offset = jax.random.randint(ks[2], (BATCH, 1), 0, 4096, dtype=jnp.int32)
    positions = offset + jnp.arange(SEQ, dtype=jnp.int32)[None, :]
    return {
        'x': nrm(ks[0], (BATCH, SEQ, D), 1.0),
        'c': nrm(ks[1], (BATCH, D), 1.0),
        'positions': positions,
        'w_ada': nrm(ks[3], (DEPTH, D, 6 * D), 0.5 * D ** -0.5),
        'b_ada': nrm(ks[4], (DEPTH, 6 * D), 0.02),
        'w_in': nrm(ks[5], (DEPTH, D, N_IN), D ** -0.5) * jnp.asarray(col_scale),
        'w_br_a': nrm(ks[6], (DEPTH, A_WIDTH, D), A_WIDTH ** -0.5),
        'w_br_b': nrm(ks[7], (DEPTH, B_WIDTH, D), B_WIDTH ** -0.5),
        'w_out': nrm(ks[8], (DEPTH, D, D), DN_BETA * D ** -0.5),
        'cmp_pos_k': nrm(ks[9], (DEPTH, CMP_BLOCK, HEAD_DIM), 0.1),
        'cmp_pos_v': nrm(ks[10], (DEPTH, CMP_BLOCK, HEAD_DIM), 0.1),
        'cmp_k_w1': nrm(ks[11], (DEPTH, CMP_BLOCK * HEAD_DIM, CMP_HIDDEN), (CMP_BLOCK * HEAD_DIM) ** -0.5),
        'cmp_k_w2': nrm(ks[12], (DEPTH, CMP_HIDDEN, HEAD_DIM), CMP_HIDDEN ** -0.5),
        'cmp_v_w1': nrm(ks[13], (DEPTH, CMP_BLOCK * HEAD_DIM, CMP_HIDDEN), (CMP_BLOCK * HEAD_DIM) ** -0.5),
        'cmp_v_w2': nrm(ks[14], (DEPTH, CMP_HIDDEN, HEAD_DIM), CMP_HIDDEN ** -0.5),
        'ln1_g': 1.0 + nrm(ks[15], (DEPTH, D), 0.02),
        'ln1_b': nrm(ks[16], (DEPTH, D), 0.02),
        'w_router': nrm(ks[17], (DEPTH, D, N_EXPERTS), D ** -0.5),
        'router_bias': nrm(ks[18], (DEPTH, N_EXPERTS), 0.01),
        'w_exp_gate': nrm(ks[19], (DEPTH, N_EXPERTS, D, EXPERT_DIM), D ** -0.5),
        'w_exp_up': nrm(ks[20], (DEPTH, N_EXPERTS, D, EXPERT_DIM), D ** -0.5),
        'w_exp_down': nrm(ks[21], (DEPTH, N_EXPERTS, EXPERT_DIM, D), DN_BETA * EXPERT_DIM ** -0.5),
        'w_sh_gate': nrm(ks[22], (DEPTH, D, SHARED_DIM), D ** -0.5),
        'w_sh_up': nrm(ks[23], (DEPTH, D, SHARED_DIM), D ** -0.5),
        'w_sh_down': nrm(ks[24], (DEPTH, SHARED_DIM, D), DN_BETA * SHARED_DIM ** -0.5),
        'ln2_g': 1.0 + nrm(ks[25], (DEPTH, D), 0.02),
        'ln2_b': nrm(ks[26], (DEPTH, D), 0.02),
    }


def reference(x, c, positions, w_ada, b_ada, w_in, w_br_a, w_br_b, w_out,
              cmp_pos_k, cmp_pos_v, cmp_k_w1, cmp_k_w2, cmp_v_w1, cmp_v_w2, ln1_g, ln1_b,
              w_router, router_bias, w_exp_gate, w_exp_up, w_exp_down,
              w_sh_gate, w_sh_up, w_sh_down, ln2_g, ln2_b):
    cond = jax.nn.silu(c)
    for l in range(DEPTH):
        mod = cond @ w_ada[l] + b_ada[l]
        shift1, scale1, gate1, shift2, scale2, gate2 = jnp.split(mod[:, None, :], 6, axis=-1)
        u = x * (1.0 + scale1) + shift1
        mix = _token_mixer(u, positions, w_in[l], w_br_a[l], w_br_b[l], w_out[l],
                           cmp_pos_k[l], cmp_pos_v[l], cmp_k_w1[l], cmp_k_w2[l], cmp_v_w1[l], cmp_v_w2[l])
        x = _layer_norm(DN_ALPHA * x + gate1 * mix, ln1_g[l], ln1_b[l])
        u = x * (1.0 + scale2) + shift2
        ffn = _moe_ffn(u, w_router[l], router_bias[l], w_exp_gate[l], w_exp_up[l], w_exp_down[l],
                       w_sh_gate[l], w_sh_up[l], w_sh_down[l])
        x = _layer_norm(DN_ALPHA * x + gate2 * ffn, ln2_g[l], ln2_b[l])
    return x
```

```python
import functools
import math

import jax
import jax.numpy as jnp
import numpy as np
from jax import lax
from jax.experimental import pallas as pl
from jax.experimental.pallas import tpu as pltpu
from jax.experimental.pallas import tpu_sc as plsc

F32 = jnp.float32
BF16 = jnp.bfloat16
I32 = jnp.int32

D_MODEL = 1024
HEAD_DIM = 64
ROPE_THETA = 500000.0
ROPE_FRACTION = 4
A_HEADS = 8
A_KV_HEADS = 2
IDX_HEADS = 8
IDX_DIM = 32
DSA_TOPK_MAX = 256
B_HEADS = 8
B_KV_HEADS = 2
REP = 4
CMP_BLOCK = 32
CMP_STRIDE = 16
CMP_HIDDEN = 256
SEL_BLOCK = 64
SEL_COUNT = 16
SEL_LOCAL = 2
WINDOW = 512
N_EXPERTS = 256
TOP_K = 8
N_GROUPS = 8
TOPK_GROUPS = 4
EXPERT_DIM = 256
SHARED_DIM = 256
ROUTED_SCALE = 2.5
DEPTH = 1
DN_ALPHA = (2 * DEPTH) ** 0.25
LN_EPS = 1e-5
NEG = -1e30
FORCE = 1e9
INT_MIN = -2147483648

LANES = 128
VMEM_LIMIT = 56 * 1024 * 1024
SC_WINDOW = 128
SC_ROW = 128
PIECES = (D_MODEL // 2) // SC_ROW

_IN_WIDTHS = (512, 128, 128, 256, 32, 8, 512, 128, 128, 128, 128, 128, 128, 24, 1024, 1024)
_IN_OFFS = np.concatenate([[0], np.cumsum(_IN_WIDTHS)]).tolist()

NT_DIMS = (((1,), (1,)), ((), ()))


def _cparams(sem):
    return pltpu.CompilerParams(dimension_semantics=sem, vmem_limit_bytes=VMEM_LIMIT)


def _sigmoid(x):
    return 1.0 / (1.0 + jnp.exp(-x))


def _dot(a, b):
    return jnp.dot(a, b, preferred_element_type=F32)


def _dot_nt(a, b):
    return lax.dot_general(a, b, NT_DIMS, preferred_element_type=F32)


def _sort_key(x):
    x = jnp.where(x == 0.0, 0.0, x)
    bits = pltpu.bitcast(x, I32)
    return jnp.where(bits < 0, bits ^ 0x7FFFFFFF, bits)


def _kth_largest_key(keys_fn, shape_rows, k):
    kf = float(k)

    def count_ge(cand):
        return jnp.sum(jnp.where(keys_fn() >= cand, 1.0, 0.0), axis=1, keepdims=True)

    t0 = jnp.where(count_ge(jnp.zeros((shape_rows, 1), I32)) >= kf, 0, INT_MIN).astype(I32)

    def body(it, t):
        cand = t + jnp.left_shift(jnp.int32(1), 30 - it)
        return jnp.where(count_ge(cand) >= kf, cand, t)

    return lax.fori_loop(0, 31, body, t0)


def _mod_kernel(c_ref, w_ref, b_ref, o_ref):
    c = c_ref[...]
    cond = (c * _sigmoid(c)).astype(BF16)
    o_ref[...] = _dot(cond, w_ref[...].astype(BF16)) + b_ref[...]


def _mod(c, w_ada, b_ada):
    bsz, d = c.shape
    n = w_ada.shape[1]
    tn = 1024
    return pl.pallas_call(
        _mod_kernel,
        out_shape=jax.ShapeDtypeStruct((bsz, n), F32),
        grid=(n // tn,),
        in_specs=[pl.BlockSpec((bsz, d), lambda j: (0, 0)),
                  pl.BlockSpec((d, tn), lambda j: (0, j)),
                  pl.BlockSpec((1, tn), lambda j: (0, j))],
        out_specs=pl.BlockSpec((bsz, tn), lambda j: (0, j)),
        compiler_params=_cparams(("parallel",)),
        name="mod",
    )(c, w_ada, b_ada.reshape(1, n))


def _rope(z, c_tab, s_tab, period, half):
    w = z.shape[1]
    reps = w // LANES
    c = jnp.concatenate([c_tab] * reps, axis=1) if reps > 1 else c_tab
    s = jnp.concatenate([s_tab] * reps, axis=1) if reps > 1 else s_tab
    lane = lax.broadcasted_iota(I32, z.shape, 1)
    first = (lane & (period - 1)) < half
    partner = jnp.where(first, pltpu.roll(z, w - half, axis=1), pltpu.roll(z, half, axis=1))
    return z * c + partner * s


def _in_proj_kernel(x_ref, mod_ref, w_ref, c64_ref, s64_ref, c32_ref, s32_ref,
                    qa_ref, ka_ref, va_ref, qi_ref, kiw_ref, qbraw_ref, qbrot_ref,
                    kcmp_ref, vcmp_ref, ksel_ref, vsel_ref, kwin_ref, vwin_ref,
                    gnsa_ref, ga_ref, gb_ref):
    mod = mod_ref[0]
    u = (x_ref[...] * (1.0 + mod[1:2, :]) + mod[0:1, :]).astype(BF16)
    c64, s64, c32, s32 = c64_ref[...], s64_ref[...], c32_ref[...], s32_ref[...]
    scale = HEAD_DIM ** -0.5

    def proj(a, b):
        return _dot(u, w_ref[:, a:b])

    rope64 = lambda z: _rope(z, c64, s64, HEAD_DIM, HEAD_DIM // ROPE_FRACTION // 2)
    rope32 = lambda z: _rope(z, c32, s32, IDX_DIM, IDX_DIM // ROPE_FRACTION // 2)

    qa_ref[...] = (rope64(proj(0, 512)) * scale).astype(BF16)
    ka_ref[...] = rope64(proj(512, 640)).astype(BF16)
    va_ref[...] = proj(640, 768).astype(BF16)
    qi_ref[...] = rope32(proj(768, 1024)).astype(BF16)
    kiw_ref[...] = rope32(proj(1024, 1152))
    qb = proj(1152, 1664)
    qbraw_ref[...] = (qb * scale).astype(BF16)
    qbrot_ref[...] = (rope64(qb) * scale).astype(BF16)
    kcmp_ref[...] = proj(1664, 1792)
    vcmp_ref[...] = proj(1792, 1920)
    ksel_ref[...] = rope64(proj(1920, 2048)).astype(BF16)
    vsel_ref[...] = proj(2048, 2176).astype(BF16)
    kwin_ref[...] = rope64(proj(2176, 2304)).astype(BF16)
    vwin_ref[...] = proj(2304, 2432).astype(BF16)
    gnsa_ref[...] = _sigmoid(proj(2432, 2560))
    ga_ref[...] = _sigmoid(proj(2560, 3584)).astype(BF16)
    gb_ref[...] = _sigmoid(proj(3584, 4608)).astype(BF16)


_IN_OUT = (("qa", 512, BF16), ("ka", 128, BF16), ("va", 128, BF16), ("qi", 256, BF16),
           ("kiw", 128, F32), ("qbraw", 512, BF16), ("qbrot", 512, BF16),
           ("kcmp", 128, F32), ("vcmp", 128, F32), ("ksel", 128, BF16), ("vsel", 128, BF16),
           ("kwin", 128, BF16), ("vwin", 128, BF16), ("gnsa", 128, F32),
           ("ga", 1024, BF16), ("gb", 1024, BF16))


def _pack_w_in(w_in):
    d = w_in.shape[0]
    col = lambda i: w_in[:, _IN_OFFS[i]:_IN_OFFS[i + 1]]
    z = lambda n: jnp.zeros((d, n), w_in.dtype)
    kiw = jnp.concatenate([col(4), z(8), col(5), z(80)], axis=1)
    gnsa = jnp.concatenate([col(13), z(104)], axis=1)
    parts = [col(0), col(1), col(2), col(3), kiw, col(6), col(7), col(8), col(9), col(10),
             col(11), col(12), gnsa, col(14), col(15)]
    return jnp.concatenate(parts, axis=1).astype(BF16)


def _rope_tables(positions):
    pos = positions.astype(F32).reshape(-1, 1)

    def tab(dim):
        rot = dim // ROPE_FRACTION
        half = rot // 2
        inv = ROPE_THETA ** (-(jnp.arange(half, dtype=F32) * 2.0) / rot)
        ang = pos * inv
        cos, sin = jnp.cos(ang), jnp.sin(ang)
        ones = jnp.ones((pos.shape[0], dim - rot), F32)
        c = jnp.concatenate([cos, cos, ones], axis=1)
        s = jnp.concatenate([-sin, sin, 0.0 * ones], axis=1)
        return jnp.tile(c, (1, LANES // dim)), jnp.tile(s, (1, LANES // dim))

    return tab(HEAD_DIM) + tab(IDX_DIM)


def _in_proj(x2, mod3, w_pack, tabs, seq, tm):
    t, d = x2.shape
    n = w_pack.shape[1]
    per_b = seq // tm
    row = lambda w: pl.BlockSpec((tm, w), lambda i: (i, 0))
    return pl.pallas_call(
        _in_proj_kernel,
        out_shape=tuple(jax.ShapeDtypeStruct((t, w), dt) for _, w, dt in _IN_OUT),
        grid=(t // tm,),
        in_specs=[row(d),
                  pl.BlockSpec((1, 6, d), lambda i: (i // per_b, 0, 0)),
                  pl.BlockSpec((d, n), lambda i: (0, 0)),
                  row(LANES), row(LANES), row(LANES), row(LANES)],
        out_specs=tuple(row(w) for _, w, _ in _IN_OUT),
        compiler_params=_cparams(("parallel",)),
        name="in_proj",
    )(x2, mod3, w_pack, *tabs)


def _tri_strict_upper(n, dtype):
    r = lax.broadcasted_iota(I32, (n, n), 0)
    c = lax.broadcasted_iota(I32, (n, n), 1)
    return jnp.where(r < c, 1.0, 0.0).astype(dtype)


def _attend(q, k, v, bias):
    s = _dot_nt(q, k) + bias
    m = jnp.max(s, axis=1, keepdims=True)
    e = jnp.exp(s - m)
    l = jnp.sum(e, axis=1, keepdims=True)
    o = _dot(e.astype(BF16), v)
    return o / jnp.maximum(l, 1e-30)


def _dsa_kernel(qi_ref, wq_ref, ki_ref, qa_ref, ka_ref, va_ref, o_ref, keys_ref, bias_ref,
                *, tq, seq, n_keep):
    t0 = pl.program_id(1) * tq
    ki = ki_ref[0][:, 0:IDX_DIM].astype(BF16)
    wq = wq_ref[0] * (IDX_HEADS ** -0.5 * IDX_DIM ** -0.5)
    qi = qi_ref[0]
    score = jnp.zeros((tq, seq), F32)
    for h in range(IDX_HEADS):
        rel = jnp.maximum(_dot_nt(qi[:, h * IDX_DIM:(h + 1) * IDX_DIM], ki), 0.0)
        score = score + wq[:, 40 + h:41 + h] * rel
    kpos = lax.broadcasted_iota(I32, (tq, seq), 1)
    tpos = t0 + lax.broadcasted_iota(I32, (tq, seq), 0)
    causal = kpos <= tpos
    keys_ref[...] = jnp.where(causal, _sort_key(score), INT_MIN)

    thr = _kth_largest_key(lambda: keys_ref[...], tq, n_keep)

    keys = keys_ref[...]
    gt = jnp.where(keys > thr, 1.0, 0.0)
    eq = jnp.where(keys == thr, 1.0, 0.0)
    need = float(n_keep) - jnp.sum(gt, axis=1, keepdims=True)
    tri = _tri_strict_upper(LANES, BF16)
    offset = jnp.zeros((tq, 1), F32)
    for cidx in range(seq // LANES):
        sl = slice(cidx * LANES, (cidx + 1) * LANES)
        eqc = eq[:, sl]
        prefix = _dot(eqc.astype(BF16), tri) + offset
        selc = gt[:, sl] + jnp.where(prefix < need, eqc, 0.0)
        bias_ref[:, sl] = jnp.where(causal[:, sl], (selc - 1.0) * 1e30, NEG)
        offset = offset + jnp.sum(eqc, axis=1, keepdims=True)

    bias = bias_ref[...]
    ka = ka_ref[0]
    va = va_ref[0]
    qa = qa_ref[0]
    for h in range(A_HEADS):
        g = h // (A_HEADS // A_KV_HEADS)
        gs = slice(g * HEAD_DIM, (g + 1) * HEAD_DIM)
        o = _attend(qa[:, h * HEAD_DIM:(h + 1) * HEAD_DIM], ka[:, gs], va[:, gs], bias)
        o_ref[0, :, h * HEAD_DIM:(h + 1) * HEAD_DIM] = o.astype(o_ref.dtype)


def _dsa(qi, kiw, qa, ka, va, tq):
    bsz, seq, _ = qa.shape
    n_keep = min(DSA_TOPK_MAX, seq // 4)
    qblk = lambda w: pl.BlockSpec((1, tq, w), lambda b, i: (b, i, 0))
    full = lambda w: pl.BlockSpec((1, seq, w), lambda b, i: (b, 0, 0))
    return pl.pallas_call(
        functools.partial(_dsa_kernel, tq=tq, seq=seq, n_keep=n_keep),
        out_shape=jax.ShapeDtypeStruct((bsz, seq, A_HEADS * HEAD_DIM), BF16),
        grid=(bsz, seq // tq),
        in_specs=[qblk(256), qblk(LANES), full(LANES), qblk(512), full(LANES), full(LANES)],
        out_specs=qblk(512),
        scratch_shapes=[pltpu.VMEM((tq, seq), I32), pltpu.VMEM((tq, seq), F32)],
        compiler_params=_cparams(("parallel", "parallel")),
        name="dsa",
    )(qi, kiw, kiw, qa, ka, va)


def _gelu_tanh(x):
    return 0.5 * x * (1.0 + jnp.tanh(math.sqrt(2.0 / math.pi) * (x + 0.044715 * (x * x * x))))


def _compress_kernel(k_ref, v_ref, pk_ref, pv_ref, w1k_ref, w2k_ref, w1v_ref, w2v_ref,
                     kc_ref, vc_ref, *, n_rows):
    half = CMP_BLOCK // 2

    def one(x_ref, p_ref, w1_ref, w2_ref, o_ref):
        for g in range(B_KV_HEADS):
            lo = jnp.zeros((n_rows, CMP_HIDDEN), F32)
            hi = jnp.zeros((n_rows, CMP_HIDDEN), F32)
            for l in range(half):
                xl = x_ref[0, :, l * LANES + g * HEAD_DIM:l * LANES + (g + 1) * HEAD_DIM]
                a = (xl + p_ref[l:l + 1, :]).astype(BF16)
                b = (xl + p_ref[half + l:half + l + 1, :]).astype(BF16)
                lo = lo + _dot(a, w1_ref[l * HEAD_DIM:(l + 1) * HEAD_DIM, :].astype(BF16))
                hi = hi + _dot(b, w1_ref[(half + l) * HEAD_DIM:(half + l + 1) * HEAD_DIM, :].astype(BF16))
            hid = lo + pltpu.roll(hi, n_rows - 1, axis=0)
            out = _dot(_gelu_tanh(hid).astype(BF16), w2_ref[...].astype(BF16))
            o_ref[0, g] = out

    one(k_ref, pk_ref, w1k_ref, w2k_ref, kc_ref)
    one(v_ref, pv_ref, w1v_ref, w2v_ref, vc_ref)


def _compress(kcmp, vcmp, pos_k, pos_v, w1k, w2k, w1v, w2v):
    bsz, n_rows, width = kcmp.shape
    xspec = pl.BlockSpec((1, n_rows, width), lambda b: (b, 0, 0))
    cst = lambda a: pl.BlockSpec(a.shape, lambda b: (0,) * a.ndim)
    ospec = pl.BlockSpec((1, B_KV_HEADS, n_rows, HEAD_DIM), lambda b: (b, 0, 0, 0))
    oshape = jax.ShapeDtypeStruct((bsz, B_KV_HEADS, n_rows, HEAD_DIM), F32)
    return pl.pallas_call(
        functools.partial(_compress_kernel, n_rows=n_rows),
        out_shape=(oshape, oshape),
        grid=(bsz,),
        in_specs=[xspec, xspec, cst(pos_k), cst(pos_v), cst(w1k), cst(w2k), cst(w1v), cst(w2v)],
        out_specs=(ospec, ospec),
        compiler_params=_cparams(("parallel",)),
        name="compress",
    )(kcmp, vcmp, pos_k, pos_v, w1k, w2k, w1v, w2v)


def _split3(x):
    a = x.astype(BF16)
    r = x - a.astype(F32)
    b = r.astype(BF16)
    c = (r - b.astype(F32)).astype(BF16)
    return a, b, c


def _nsa_kernel(qraw_ref, qrot_ref, kc_ref, vc_ref, ksel_ref, vsel_ref, kwin_ref, vwin_ref,
                g_ref, o_ref, *, tq, seq, n_c):
    i = pl.program_id(1)
    t0 = i * tq
    n_s = seq // SEL_BLOCK
    n_pick = min(SEL_COUNT, n_s)
    span = WINDOW + tq
    gates = g_ref[0]
    qraw = qraw_ref[0]
    qrot = qrot_ref[0]

    cl = lax.broadcasted_iota(I32, (tq, LANES), 1)
    tl = t0 + lax.broadcasted_iota(I32, (tq, LANES), 0)
    valid_c = (cl * CMP_STRIDE + CMP_BLOCK - 1 <= tl) & (cl < n_c)
    cr = lax.broadcasted_iota(I32, (LANES, LANES), 0) * CMP_STRIDE
    js = lax.broadcasted_iota(I32, (LANES, LANES), 1) * SEL_BLOCK
    overlap = jnp.where((cr <= js + SEL_BLOCK - 1) & (cr + CMP_BLOCK - 1 >= js), 1.0, 0.0).astype(BF16)
    cur = tl // SEL_BLOCK
    forced = (cl == 0) | ((cur - cl >= 0) & (cur - cl < SEL_LOCAL))
    blk_causal = cl * SEL_BLOCK <= tl
    tri = _tri_strict_upper(LANES, BF16)
    ej = lax.broadcasted_iota(I32, (LANES, seq), 0)
    es = lax.broadcasted_iota(I32, (LANES, seq), 1) // SEL_BLOCK
    expand = jnp.where(ej == es, 1.0, 0.0).astype(BF16)
    kpos = lax.broadcasted_iota(I32, (tq, seq), 1)
    tpos = t0 + lax.broadcasted_iota(I32, (tq, seq), 0)
    causal = kpos <= tpos
    w0 = pl.multiple_of(jnp.clip(t0 - WINDOW, 0, seq - span), tq)
    wpos = w0 + lax.broadcasted_iota(I32, (tq, span), 1)
    wt = t0 + lax.broadcasted_iota(I32, (tq, span), 0)
    wdiff = wt - wpos
    wbias = jnp.where((wdiff >= 0) & (wdiff < WINDOW), 0.0, NEG)

    for g in range(B_KV_HEADS):
        gs = slice(g * HEAD_DIM, (g + 1) * HEAD_DIM)
        kc = kc_ref[0, g].astype(BF16)
        vc = vc_ref[0, g].astype(BF16)
        o_cmp = []
        p_sum = jnp.zeros((tq, LANES), F32)
        for r in range(REP):
            h = g * REP + r
            s = jnp.where(valid_c, _dot_nt(qraw[:, h * HEAD_DIM:(h + 1) * HEAD_DIM], kc), NEG)
            e = jnp.where(valid_c, jnp.exp(s - jnp.max(s, axis=1, keepdims=True)), 0.0)
            p = e / jnp.maximum(jnp.sum(e, axis=1, keepdims=True), 1e-30)
            o_cmp.append(_dot(p.astype(BF16), vc))
            p_sum = p_sum + p
        pa, pb, pc = _split3(p_sum)
        imp = _dot(pa, overlap) + _dot(pb, overlap) + _dot(pc, overlap)
        imp = jnp.where(forced, FORCE, jnp.where(blk_causal, imp, NEG))
        keys = jnp.where(cl < n_s, _sort_key(imp), INT_MIN)
        thr = _kth_largest_key(lambda: keys, tq, n_pick)
        gt = jnp.where(keys > thr, 1.0, 0.0)
        eq = jnp.where(keys == thr, 1.0, 0.0)
        need = float(n_pick) - jnp.sum(gt, axis=1, keepdims=True)
        prefix = _dot(eq.astype(BF16), tri)
        blk_sel = gt + jnp.where(prefix < need, eq, 0.0)
        tok_sel = _dot(blk_sel.astype(BF16), expand)
        sbias = jnp.where(causal, (tok_sel - 1.0) * 1e30, NEG)

        ksel = ksel_ref[0][:, gs]
        vsel = vsel_ref[0][:, gs]
        kwin = kwin_ref[0, pl.ds(w0, span), :][:, gs]
        vwin = vwin_ref[0, pl.ds(w0, span), :][:, gs]
        for r in range(REP):
            h = g * REP + r
            hs = slice(h * HEAD_DIM, (h + 1) * HEAD_DIM)
            o_sel = _attend(qrot[:, hs], ksel, vsel, sbias)
            o_win = _attend(qrot[:, hs], kwin, vwin, wbias)
            o = (gates[:, 3 * h:3 * h + 1] * o_cmp[r] + gates[:, 3 * h + 1:3 * h + 2] * o_sel
                 + gates[:, 3 * h + 2:3 * h + 3] * o_win)
            o_ref[0, :, hs] = o.astype(o_ref.dtype)


def _nsa(qraw, qrot, kc, vc, ksel, vsel, kwin, vwin, gnsa, tq):
    bsz, seq, _ = qraw.shape
    n_c = (seq - CMP_BLOCK) // CMP_STRIDE + 1
    qblk = lambda w: pl.BlockSpec((1, tq, w), lambda b, i: (b, i, 0))
    full = lambda w: pl.BlockSpec((1, seq, w), lambda b, i: (b, 0, 0))
    cspec = pl.BlockSpec((1, B_KV_HEADS, kc.shape[2], HEAD_DIM), lambda b, i: (b, 0, 0, 0))
    return pl.pallas_call(
        functools.partial(_nsa_kernel, tq=tq, seq=seq, n_c=n_c),
        out_shape=jax.ShapeDtypeStruct((bsz, seq, B_HEADS * HEAD_DIM), BF16),
        grid=(bsz, seq // tq),
        in_specs=[qblk(512), qblk(512), cspec, cspec, full(LANES), full(LANES), full(LANES),
                  full(LANES), qblk(LANES)],
        out_specs=qblk(512),
        compiler_params=_cparams(("parallel", "parallel")),
        name="nsa",
    )(qraw, qrot, kc, vc, ksel, vsel, kwin, vwin, gnsa)


def _pack_pairs(x):
    n = x.shape[1] // 2
    lo = pltpu.bitcast(x[:, :n].astype(BF16).astype(F32), I32)
    hi = pltpu.bitcast(x[:, n:].astype(BF16).astype(F32), I32)
    return lax.shift_right_logical(lo, 16) | (hi & jnp.int32(-65536))


def _unpack_pairs(p):
    lo = pltpu.bitcast(lax.shift_left(p, 16), F32)
    hi = pltpu.bitcast(p & jnp.int32(-65536), F32)
    return jnp.concatenate([lo, hi], axis=1)


def _layer_norm(y, g, b):
    mu = jnp.mean(y, axis=1, keepdims=True)
    yc = y - mu
    var = jnp.mean(yc * yc, axis=1, keepdims=True)
    return yc * lax.rsqrt(var + LN_EPS) * g + b


def _out_proj_kernel(oa_ref, ob_ref, ga_ref, gb_ref, x_ref, mod_ref, wa_ref, wb_ref, wo_ref,
                     g1_ref, b1_ref, wrh_ref, wrl_ref, x1_ref, u2_ref, lg_ref):
    mod = mod_ref[0]
    merged = (ga_ref[...].astype(F32) * _dot(oa_ref[...], wa_ref[...])
              + gb_ref[...].astype(F32) * _dot(ob_ref[...], wb_ref[...]))
    mix = _dot(merged.astype(BF16), wo_ref[...])
    x1 = _layer_norm(DN_ALPHA * x_ref[...] + mod[2:3, :] * mix, g1_ref[...], b1_ref[...])
    x1_ref[...] = x1
    u2 = x1 * (1.0 + mod[4:5, :]) + mod[3:4, :]
    packed = _pack_pairs(u2)
    for j in range(PIECES):
        u2_ref[j] = packed[:, j * SC_ROW:(j + 1) * SC_ROW]
    uh = u2.astype(BF16)
    ul = (u2 - uh.astype(F32)).astype(BF16)
    lg_ref[...] = _dot_nt(wrh_ref[...], uh) + _dot_nt(wrh_ref[...], ul) + _dot_nt(wrl_ref[...], uh)


def _out_proj(oa, ob, ga, gb, x2, mod3, wa, wb, wo, g1, b1, wrh, wrl, seq, tm):
    t, d = x2.shape
    per_b = seq // tm
    row = lambda w: pl.BlockSpec((tm, w), lambda i: (i, 0))
    cst = lambda a: pl.BlockSpec(a.shape, lambda i: (0,) * a.ndim)
    return pl.pallas_call(
        _out_proj_kernel,
        out_shape=(jax.ShapeDtypeStruct((t, d), F32), jax.ShapeDtypeStruct((PIECES, t, SC_ROW), I32),
                   jax.ShapeDtypeStruct((N_EXPERTS, t), F32)),
        grid=(t // tm,),
        in_specs=[row(512), row(512), row(d), row(d), row(d),
                  pl.BlockSpec((1, 6, d), lambda i: (i // per_b, 0, 0)),
                  cst(wa), cst(wb), cst(wo), cst(g1), cst(b1), cst(wrh), cst(wrl)],
        out_specs=(row(d), pl.BlockSpec((PIECES, tm, SC_ROW), lambda i: (0, i, 0)),
                   pl.BlockSpec((N_EXPERTS, tm), lambda i: (0, i))),
        compiler_params=_cparams(("parallel",)),
        name="out_proj",
    )(oa, ob, ga, gb, x2, mod3, wa, wb, wo, g1, b1, wrh, wrl)


def _first_max(x, rows):
    m = jnp.max(x, axis=0, keepdims=True)
    idx = jnp.min(jnp.where(x == m, rows, 1e9), axis=0, keepdims=True)
    return m, idx


def _router_kernel(lg_ref, rb_ref, idx_ref, w_ref, pos_ref, cnt_ref, carry_ref, *, tm):
    @pl.when(pl.program_id(0) == 0)
    def _():
        carry_ref[...] = jnp.zeros_like(carry_ref)

    per_g = N_EXPERTS // N_GROUPS
    scores = _sigmoid(lg_ref[...])
    choice = scores + rb_ref[...][:, 0:1]
    rows = lax.broadcasted_iota(I32, (N_EXPERTS, tm), 0).astype(F32)
    rows_g = lax.broadcasted_iota(I32, (per_g, tm), 0).astype(F32)
    ninf = -jnp.inf

    gs = []
    for g in range(N_GROUPS):
        x = choice[g * per_g:(g + 1) * per_g, :]
        m1, i1 = _first_max(x, rows_g)
        m2 = jnp.max(jnp.where(rows_g == i1, ninf, x), axis=0, keepdims=True)
        gs.append(m1 + m2)
    gscore = jnp.concatenate(gs, axis=0)
    rows8 = lax.broadcasted_iota(I32, (N_GROUPS, tm), 0).astype(F32)
    keep = jnp.zeros((N_GROUPS, tm), F32)
    for _ in range(TOPK_GROUPS):
        _, gi = _first_max(gscore, rows8)
        hit = rows8 == gi
        keep = jnp.where(hit, 1.0, keep)
        gscore = jnp.where(hit, ninf, gscore)
    keep_full = jnp.concatenate(
        [jnp.broadcast_to(keep[g:g + 1, :], (per_g, tm)) for g in range(N_GROUPS)], axis=0)
    masked = jnp.where(keep_full > 0.0, choice, NEG)

    idxs, ws = [], []
    onehot = jnp.zeros((N_EXPERTS, tm), F32)
    for _ in range(TOP_K):
        _, ei = _first_max(masked, rows)
        hit = rows == ei
        idxs.append(ei)
        ws.append(jnp.sum(jnp.where(hit, scores, 0.0), axis=0, keepdims=True))
        masked = jnp.where(hit, ninf, masked)
        onehot = jnp.where(hit, 1.0, onehot)
    idx = jnp.concatenate(idxs, axis=0)
    w = jnp.concatenate(ws, axis=0)
    idx_ref[...] = idx.astype(I32)
    w_ref[...] = w / jnp.sum(w, axis=0, keepdims=True) * ROUTED_SCALE

    tri = _tri_strict_upper(tm, BF16)
    base = _dot(onehot.astype(BF16), tri) + carry_ref[...][:, 0:1]
    pos = [jnp.sum(jnp.where(rows == idxs[k], base, 0.0), axis=0, keepdims=True) for k in range(TOP_K)]
    pos_ref[...] = jnp.concatenate(pos, axis=0).astype(I32)
    carry = carry_ref[...] + jnp.sum(onehot, axis=1, keepdims=True)
    carry_ref[...] = carry
    cnt_ref[...] = carry


def _router(lg, router_bias, tm):
    e, t = lg.shape
    rb = jnp.broadcast_to(router_bias.reshape(e, 1).astype(F32), (e, LANES))
    tok = lambda r: pl.BlockSpec((r, tm), lambda i: (0, i))
    return pl.pallas_call(
        functools.partial(_router_kernel, tm=tm),
        out_shape=(jax.ShapeDtypeStruct((TOP_K, t), I32), jax.ShapeDtypeStruct((TOP_K, t), F32),
                   jax.ShapeDtypeStruct((TOP_K, t), I32), jax.ShapeDtypeStruct((e, LANES), F32)),
        grid=(t // tm,),
        in_specs=[tok(e), pl.BlockSpec((e, LANES), lambda i: (0, 0))],
        out_specs=(tok(TOP_K), tok(TOP_K), tok(TOP_K), pl.BlockSpec((e, LANES), lambda i: (0, 0))),
        scratch_shapes=[pltpu.VMEM((e, LANES), F32)],
        compiler_params=_cparams(("arbitrary",)),
        name="router",
    )(lg, rb)


def _dest_kernel(idx_ref, pos_ref, st_ref, dest_ref, *, tm):
    rows = lax.broadcasted_iota(I32, (N_EXPERTS, tm), 0)
    starts = st_ref[...][:, 0:1]
    idx = idx_ref[...]
    out = []
    for k in range(TOP_K):
        out.append(jnp.sum(jnp.where(rows == idx[k:k + 1, :], starts, 0.0), axis=0, keepdims=True))
    dest_ref[...] = jnp.concatenate(out, axis=0).astype(I32) + pos_ref[...]


def _dest(idx, pos, starts, tm):
    k, t = idx.shape
    st = jnp.broadcast_to(starts.reshape(N_EXPERTS, 1).astype(F32), (N_EXPERTS, LANES))
    tok = pl.BlockSpec((k, tm), lambda i: (0, i))
    return pl.pallas_call(
        functools.partial(_dest_kernel, tm=tm),
        out_shape=jax.ShapeDtypeStruct((k, t), I32),
        grid=(t // tm,),
        in_specs=[tok, tok, pl.BlockSpec((N_EXPERTS, LANES), lambda i: (0, 0))],
        out_specs=tok,
        compiler_params=_cparams(("parallel",)),
        name="dest",
    )(idx, pos, st)


def _experts_kernel(blk_ref, used_ref, x_ref, wg_ref, wu_ref, wd_ref, y_ref):
    @pl.when(pl.program_id(0) < used_ref[0])
    def _():
        x = _unpack_pairs(x_ref[...]).astype(BF16)
        a = _dot(x, wg_ref[0].astype(BF16))
        b = _dot(x, wu_ref[0].astype(BF16))
        h = (a * _sigmoid(a) * b).astype(BF16)
        y_ref[...] = _pack_pairs(_dot(h, wd_ref[0].astype(BF16)))


def _experts(xs, blk_e, n_used, wg, wu, wd, bm):
    cap, half = xs.shape
    n_blocks = cap // bm
    d, f = wg.shape[1], wg.shape[2]
    rows = pl.BlockSpec((bm, half), lambda b, blk, used: (jnp.minimum(b, used[0] - 1), 0))
    return pl.pallas_call(
        _experts_kernel,
        out_shape=jax.ShapeDtypeStruct((cap, half), I32),
        grid_spec=pltpu.PrefetchScalarGridSpec(
            num_scalar_prefetch=2,
            grid=(n_blocks,),
            in_specs=[rows,
                      pl.BlockSpec((1, d, f), lambda b, blk, used: (blk[b], 0, 0)),
                      pl.BlockSpec((1, d, f), lambda b, blk, used: (blk[b], 0, 0)),
                      pl.BlockSpec((1, f, d), lambda b, blk, used: (blk[b], 0, 0))],
            out_specs=rows),
        compiler_params=_cparams(("arbitrary",)),
        name="experts",
    )(blk_e, n_used, xs, wg, wu, wd)


def _final_kernel(x1_ref, u2_ref, yg_ref, w_ref, mod_ref, sg_ref, su_ref, sd_ref, g2_ref, b2_ref, o_ref):
    mod = mod_ref[0]
    w = w_ref[...]
    rows = lambda ref, *lead: jnp.concatenate([ref[(j,) + lead] for j in range(PIECES)], axis=1)
    routed = w[:, 0:1] * _unpack_pairs(rows(yg_ref, 0))
    for k in range(1, TOP_K):
        routed = routed + w[:, k:k + 1] * _unpack_pairs(rows(yg_ref, k))
    u = _unpack_pairs(rows(u2_ref)).astype(BF16)
    a = _dot(u, sg_ref[...])
    b = _dot(u, su_ref[...])
    shared = _dot((a * _sigmoid(a) * b).astype(BF16), sd_ref[...])
    y = DN_ALPHA * x1_ref[...] + mod[5:6, :] * (routed + shared)
    o_ref[...] = _layer_norm(y, g2_ref[...], b2_ref[...])


def _final(x1, u2p, yg, wtok, mod3, sg, su, sd, g2, b2, seq, tm):
    t, d = x1.shape
    per_b = seq // tm
    row = lambda w: pl.BlockSpec((tm, w), lambda i: (i, 0))
    cst = lambda a: pl.BlockSpec(a.shape, lambda i: (0,) * a.ndim)
    return pl.pallas_call(
        _final_kernel,
        out_shape=jax.ShapeDtypeStruct((t, d), F32),
        grid=(t // tm,),
        in_specs=[row(d), pl.BlockSpec((PIECES, tm, SC_ROW), lambda i: (0, i, 0)),
                  pl.BlockSpec((PIECES, TOP_K, tm, SC_ROW), lambda i: (0, 0, i, 0)),
                  row(LANES), pl.BlockSpec((1, 6, d), lambda i: (i // per_b, 0, 0)),
                  cst(sg), cst(su), cst(sd), cst(g2), cst(b2)],
        out_specs=row(d),
        compiler_params=_cparams(("parallel",)),
        name="final",
    )(x1, u2p, yg, wtok, mod3, sg, su, sd, g2, b2)


def _sc_mesh():
    return plsc.VectorSubcoreMesh(core_axis_name="core", subcore_axis_name="subcore")


def _sc_scatter_rows(src, idx, n_out, src_block):
    n_idx = idx.shape[0]

    @functools.partial(pl.kernel, out_type=jax.ShapeDtypeStruct((n_out, SC_ROW), src.dtype),
                       mesh=_sc_mesh(), scratch_types=[])
    def k(x_hbm, i_hbm, o_hbm):
        def body(x_vmem, i_vmem):
            pltpu.sync_copy(x_vmem, o_hbm.at[i_vmem.at[0]])

        pltpu.emit_pipeline(
            body, grid=(n_idx // SC_WINDOW,),
            in_specs=[pl.BlockSpec((SC_WINDOW, SC_ROW), lambda i: (src_block(i), 0)),
                      pl.BlockSpec((1, SC_WINDOW), lambda i: (0, i))],
            out_specs=[],
            core_axis_name=("core", "subcore"),
            dimension_semantics=(pltpu.PARALLEL,),
        )(x_hbm, i_hbm)

    return k(src, idx.reshape(1, n_idx))


def _sc_gather_rows(src, idx):
    n_idx = idx.shape[0]

    @functools.partial(pl.kernel, out_type=jax.ShapeDtypeStruct((n_idx, SC_ROW), src.dtype),
                       mesh=_sc_mesh(), scratch_types=[])
    def k(x_hbm, i_hbm, o_hbm):
        def body(i_vmem, o_vmem):
            pltpu.sync_copy(x_hbm.at[i_vmem.at[0]], o_vmem)

        pltpu.emit_pipeline(
            body, grid=(n_idx // SC_WINDOW,),
            in_specs=[pl.BlockSpec((1, SC_WINDOW), lambda i: (0, i))],
            out_specs=[pl.BlockSpec((SC_WINDOW, SC_ROW), lambda i: (i, 0))],
            core_axis_name=("core", "subcore"),
            dimension_semantics=(pltpu.PARALLEL,),
        )(i_hbm, o_hbm)

    return k(src, idx.reshape(1, n_idx))


TM_PROJ = 512
TQ_ATTN = 128
TM_ROUTE = 512
BM_EXPERT = 256


def _moe_plan(counts, n_tok):
    bm = BM_EXPERT
    padded = (counts + bm - 1) // bm * bm
    p_ends = jnp.cumsum(padded)
    starts = p_ends - padded
    n_blocks = n_tok * TOP_K // bm + N_EXPERTS
    blk_e = jnp.minimum(jnp.searchsorted(p_ends, jnp.arange(n_blocks, dtype=I32) * bm, side="right"),
                        N_EXPERTS - 1).astype(I32)
    n_used = (p_ends[-1] // bm).astype(I32).reshape(1)
    return starts, blk_e, n_used, n_blocks


def _layer(x, mod, positions, w_in, w_br_a, w_br_b, w_out, cmp_pos_k, cmp_pos_v, cmp_k_w1, cmp_k_w2,
           cmp_v_w1, cmp_v_w2, ln1_g, ln1_b, w_router, router_bias, w_exp_gate, w_exp_up, w_exp_down,
           w_sh_gate, w_sh_up, w_sh_down, ln2_g, ln2_b):
    bsz, seq, d = x.shape
    t = bsz * seq
    assert seq // CMP_STRIDE == LANES and seq % TQ_ATTN == 0 and seq % TM_PROJ == 0
    x2 = x.reshape(t, d)
    mod3 = mod.reshape(bsz, 6, d)

    outs = _in_proj(x2, mod3, _pack_w_in(w_in), _rope_tables(positions), seq, TM_PROJ)
    z = {name: o.reshape(bsz, seq, o.shape[1]) for (name, _, _), o in zip(_IN_OUT, outs)}

    o_a = _dsa(z["qi"], z["kiw"], z["qa"], z["ka"], z["va"], TQ_ATTN)

    n_rows = seq // CMP_STRIDE
    kc, vc = _compress(z["kcmp"].reshape(bsz, n_rows, CMP_STRIDE * LANES),
                       z["vcmp"].reshape(bsz, n_rows, CMP_STRIDE * LANES),
                       cmp_pos_k, cmp_pos_v, cmp_k_w1, cmp_k_w2, cmp_v_w1, cmp_v_w2)
    o_b = _nsa(z["qbraw"], z["qbrot"], kc, vc, z["ksel"], z["vsel"], z["kwin"], z["vwin"],
               z["gnsa"], TQ_ATTN)

    wr_hi = w_router.T.astype(BF16)
    wr_lo = (w_router.T - wr_hi.astype(F32)).astype(BF16)
    x1, u2p, logits = _out_proj(
        o_a.reshape(t, -1), o_b.reshape(t, -1), outs[14], outs[15], x2, mod3,
        w_br_a.astype(BF16), w_br_b.astype(BF16), w_out.astype(BF16),
        ln1_g.reshape(1, d), ln1_b.reshape(1, d), wr_hi, wr_lo, seq, TM_PROJ)

    idx, wts, pos, counts = _router(logits, router_bias, TM_ROUTE)
    starts, blk_e, n_used, n_blocks = _moe_plan(counts[:, 0].astype(I32), t)
    dest = _dest(idx, pos, starts, TM_ROUTE)

    dest_p = (dest[None] * PIECES + jnp.arange(PIECES, dtype=I32).reshape(PIECES, 1, 1)).reshape(-1)
    cap = n_blocks * BM_EXPERT
    tb = t // SC_WINDOW
    xs = _sc_scatter_rows(u2p.reshape(PIECES * t, SC_ROW), dest_p, cap * PIECES,
                          lambda i: (i // (TOP_K * tb)) * tb + i % tb)
    ys = _experts(xs.reshape(cap, d // 2), blk_e, n_used, w_exp_gate, w_exp_up, w_exp_down, BM_EXPERT)
    yg = _sc_gather_rows(ys.reshape(cap * PIECES, SC_ROW), dest_p).reshape(PIECES, TOP_K, t, SC_ROW)

    wtok = jnp.pad(wts.T, ((0, 0), (0, LANES - TOP_K)))
    return _final(x1, u2p, yg, wtok, mod3, w_sh_gate.astype(BF16), w_sh_up.astype(BF16),
                  w_sh_down.astype(BF16), ln2_g.reshape(1, d), ln2_b.reshape(1, d), seq, TM_PROJ
                  ).reshape(bsz, seq, d)


def kernel(x, c, positions, w_ada, b_ada, w_in, w_br_a, w_br_b, w_out, cmp_pos_k, cmp_pos_v, cmp_k_w1,
           cmp_k_w2, cmp_v_w1, cmp_v_w2, ln1_g, ln1_b, w_router, router_bias, w_exp_gate, w_exp_up,
           w_exp_down, w_sh_gate, w_sh_up, w_sh_down, ln2_g, ln2_b):
    for l in range(w_ada.shape[0]):
        mod = _mod(c, w_ada[l], b_ada[l])
        x = _layer(x, mod, positions, w_in[l], w_br_a[l], w_br_b[l], w_out[l], cmp_pos_k[l], cmp_pos_v[l],
                   cmp_k_w1[l], cmp_k_w2[l], cmp_v_w1[l], cmp_v_w2[l], ln1_g[l], ln1_b[l], w_router[l],
                   router_bias[l], w_exp_gate[l], w_exp_up[l], w_exp_down[l], w_sh_gate[l], w_sh_up[l],
                   w_sh_down[l], ln2_g[l], ln2_b[l])
    return x
```

```python
import functools
import math

import jax
import jax.numpy as jnp
import numpy as np
from jax import lax
from jax.experimental import pallas as pl
from jax.experimental.pallas import tpu as pltpu
from jax.experimental.pallas import tpu_sc as plsc

F32 = jnp.float32
BF16 = jnp.bfloat16
I32 = jnp.int32

D_MODEL = 1024
HEAD_DIM = 64
ROPE_THETA = 500000.0
ROPE_FRACTION = 4
A_HEADS = 8
A_KV_HEADS = 2
IDX_HEADS = 8
IDX_DIM = 32
DSA_TOPK_MAX = 256
B_HEADS = 8
B_KV_HEADS = 2
REP = 4
CMP_BLOCK = 32
CMP_STRIDE = 16
CMP_HIDDEN = 256
SEL_BLOCK = 64
SEL_COUNT = 16
SEL_LOCAL = 2
WINDOW = 512
N_EXPERTS = 256
TOP_K = 8
N_GROUPS = 8
TOPK_GROUPS = 4
EXPERT_DIM = 256
SHARED_DIM = 256
ROUTED_SCALE = 2.5
DEPTH = 1
DN_ALPHA = (2 * DEPTH) ** 0.25
LN_EPS = 1e-5
NEG = -1e30
FORCE = 1e9
INT_MIN = -2147483648

LANES = 128
VMEM_LIMIT = 56 * 1024 * 1024
SC_WINDOW = 128
SC_ROW = 128
PIECES = (D_MODEL // 2) // SC_ROW

_IN_WIDTHS = (512, 128, 128, 256, 32, 8, 512, 128, 128, 128, 128, 128, 128, 24, 1024, 1024)
_IN_OFFS = np.concatenate([[0], np.cumsum(_IN_WIDTHS)]).tolist()

NT_DIMS = (((1,), (1,)), ((), ()))


def _cparams(sem):
    return pltpu.CompilerParams(dimension_semantics=sem, vmem_limit_bytes=VMEM_LIMIT)


def _sigmoid(x):
    return 1.0 / (1.0 + jnp.exp(-x))


def _dot(a, b):
    return jnp.dot(a, b, preferred_element_type=F32)


def _dot_nt(a, b):
    return lax.dot_general(a, b, NT_DIMS, preferred_element_type=F32)


def _sort_key(x):
    x = jnp.where(x == 0.0, 0.0, x)
    bits = pltpu.bitcast(x, I32)
    return jnp.where(bits < 0, bits ^ 0x7FFFFFFF, bits)


def _kth_largest_key(count_ge, shape_rows, k):
    kf = float(k)
    t0 = jnp.where(count_ge(jnp.zeros((shape_rows, 1), I32)) >= kf, 0, INT_MIN).astype(I32)

    def body(it, t):
        cand = t + jnp.left_shift(jnp.int32(1), 30 - it)
        return jnp.where(count_ge(cand) >= kf, cand, t)

    return lax.fori_loop(0, 31, body, t0)


def _mod_kernel(c_ref, w_ref, b_ref, o_ref):
    c = c_ref[...]
    cond = (c * _sigmoid(c)).astype(BF16)
    o_ref[...] = _dot(cond, w_ref[...].astype(BF16)) + b_ref[...]


def _mod(c, w_ada, b_ada):
    bsz, d = c.shape
    n = w_ada.shape[1]
    tn = 1024
    return pl.pallas_call(
        _mod_kernel,
        out_shape=jax.ShapeDtypeStruct((bsz, n), F32),
        grid=(n // tn,),
        in_specs=[pl.BlockSpec((bsz, d), lambda j: (0, 0)),
                  pl.BlockSpec((d, tn), lambda j: (0, j)),
                  pl.BlockSpec((1, tn), lambda j: (0, j))],
        out_specs=pl.BlockSpec((bsz, tn), lambda j: (0, j)),
        compiler_params=_cparams(("parallel",)),
        name="mod",
    )(c, w_ada, b_ada.reshape(1, n))


def _rope(z, c_tab, s_tab, period, half):
    w = z.shape[1]
    reps = w // LANES
    c = jnp.concatenate([c_tab] * reps, axis=1) if reps > 1 else c_tab
    s = jnp.concatenate([s_tab] * reps, axis=1) if reps > 1 else s_tab
    lane = lax.broadcasted_iota(I32, z.shape, 1)
    first = (lane & (period - 1)) < half
    partner = jnp.where(first, pltpu.roll(z, w - half, axis=1), pltpu.roll(z, half, axis=1))
    return z * c + partner * s


def _in_proj_kernel(x_ref, mod_ref, w_ref, c64_ref, s64_ref, c32_ref, s32_ref,
                    qa_ref, ka_ref, va_ref, qi_ref, kiw_ref, kib_ref, qbraw_ref, qbrot_ref,
                    kcmp_ref, vcmp_ref, ksel_ref, vsel_ref, kwin_ref, vwin_ref,
                    gnsa_ref, ga_ref, gb_ref):
    mod = mod_ref[0]
    u = (x_ref[...] * (1.0 + mod[1:2, :]) + mod[0:1, :]).astype(BF16)
    c64, s64, c32, s32 = c64_ref[...], s64_ref[...], c32_ref[...], s32_ref[...]
    scale = HEAD_DIM ** -0.5

    def proj(a, b):
        return _dot(u, w_ref[:, a:b])

    rope64 = lambda z: _rope(z, c64, s64, HEAD_DIM, HEAD_DIM // ROPE_FRACTION // 2)
    rope32 = lambda z: _rope(z, c32, s32, IDX_DIM, IDX_DIM // ROPE_FRACTION // 2)

    low = lax.broadcasted_iota(I32, (x_ref.shape[0], LANES), 1) < HEAD_DIM

    def k_ext(z):
        zr = pltpu.roll(z, HEAD_DIM, axis=1)
        return jnp.concatenate([jnp.where(low, z, 0.0), jnp.where(low, 0.0, zr),
                                jnp.where(low, zr, 0.0), jnp.where(low, 0.0, z)], axis=1).astype(BF16)

    def v_ext(z):
        zr = pltpu.roll(z, HEAD_DIM, axis=1)
        return jnp.concatenate([jnp.where(low, z, 1.0), jnp.where(low, zr, 1.0)], axis=1).astype(BF16)

    qa_ref[...] = (rope64(proj(0, 512)) * scale).astype(BF16)
    ka_ref[...] = k_ext(rope64(proj(512, 640)))
    va_ref[...] = v_ext(proj(640, 768))
    qi_ref[...] = rope32(proj(768, 1024)).astype(BF16)
    kiw = rope32(proj(1024, 1152))
    kiw_ref[...] = kiw
    kib_ref[...] = kiw.astype(BF16)
    qb = proj(1152, 1664)
    qbraw_ref[...] = (qb * scale).astype(BF16)
    qbrot_ref[...] = (rope64(qb) * scale).astype(BF16)
    kcmp_ref[...] = proj(1664, 1792)
    vcmp_ref[...] = proj(1792, 1920)
    ksel_ref[...] = k_ext(rope64(proj(1920, 2048)))
    vsel_ref[...] = v_ext(proj(2048, 2176))
    kwin_ref[...] = k_ext(rope64(proj(2176, 2304)))
    vwin_ref[...] = v_ext(proj(2304, 2432))
    gnsa_ref[...] = _sigmoid(proj(2432, 2560))
    ga_ref[...] = _sigmoid(proj(2560, 3584)).astype(BF16)
    gb_ref[...] = _sigmoid(proj(3584, 4608)).astype(BF16)


_IN_OUT = (("qa", 512, BF16), ("ka", 512, BF16), ("va", 256, BF16), ("qi", 256, BF16),
           ("kiw", 128, F32), ("kib", 128, BF16), ("qbraw", 512, BF16), ("qbrot", 512, BF16),
           ("kcmp", 128, F32), ("vcmp", 128, F32), ("ksel", 512, BF16), ("vsel", 256, BF16),
           ("kwin", 512, BF16), ("vwin", 256, BF16), ("gnsa", 128, F32),
           ("ga", 1024, BF16), ("gb", 1024, BF16))


def _pack_w_in(w_in):
    d = w_in.shape[0]
    col = lambda i: w_in[:, _IN_OFFS[i]:_IN_OFFS[i + 1]]
    z = lambda n: jnp.zeros((d, n), w_in.dtype)
    kiw = jnp.concatenate([col(4), z(8), col(5), z(80)], axis=1)
    gnsa = jnp.concatenate([col(13), z(104)], axis=1)
    parts = [col(0), col(1), col(2), col(3), kiw, col(6), col(7), col(8), col(9), col(10),
             col(11), col(12), gnsa, col(14), col(15)]
    return jnp.concatenate(parts, axis=1).astype(BF16)


def _rope_tables(positions):
    pos = positions.astype(F32).reshape(-1, 1)

    def tab(dim):
        rot = dim // ROPE_FRACTION
        half = rot // 2
        inv = ROPE_THETA ** (-(jnp.arange(half, dtype=F32) * 2.0) / rot)
        ang = pos * inv
        cos, sin = jnp.cos(ang), jnp.sin(ang)
        ones = jnp.ones((pos.shape[0], dim - rot), F32)
        c = jnp.concatenate([cos, cos, ones], axis=1)
        s = jnp.concatenate([-sin, sin, 0.0 * ones], axis=1)
        return jnp.tile(c, (1, LANES // dim)), jnp.tile(s, (1, LANES // dim))

    return tab(HEAD_DIM) + tab(IDX_DIM)


def _in_proj(x2, mod3, w_pack, tabs, seq, tm):
    t, d = x2.shape
    n = w_pack.shape[1]
    per_b = seq // tm
    row = lambda w: pl.BlockSpec((tm, w), lambda i: (i, 0))
    return pl.pallas_call(
        _in_proj_kernel,
        out_shape=tuple(jax.ShapeDtypeStruct((t, w), dt) for _, w, dt in _IN_OUT),
        grid=(t // tm,),
        in_specs=[row(d),
                  pl.BlockSpec((1, 6, d), lambda i: (i // per_b, 0, 0)),
                  pl.BlockSpec((d, n), lambda i: (0, 0)),
                  row(LANES), row(LANES), row(LANES), row(LANES)],
        out_specs=tuple(row(w) for _, w, _ in _IN_OUT),
        compiler_params=_cparams(("parallel",)),
        name="in_proj",
    )(x2, mod3, w_pack, *tabs)


def _tri_strict_upper(n, dtype):
    r = lax.broadcasted_iota(I32, (n, n), 0)
    c = lax.broadcasted_iota(I32, (n, n), 1)
    return jnp.where(r < c, 1.0, 0.0).astype(dtype)


def _lane_fold(m):
    part = m[:, 0:LANES]
    for j in range(1, m.shape[1] // LANES):
        part = part + m[:, j * LANES:(j + 1) * LANES]
    return part


def _flash_step(q, k, v_ext, bias, m, acc):
    s = _dot_nt(q, k) + bias
    m_new = jnp.maximum(m, jnp.max(s, axis=1, keepdims=True))
    e = jnp.exp(s - m_new).astype(BF16)
    return m_new, acc * jnp.exp(m - m_new) + _dot(e, v_ext)


def _flash_finish(acc):
    return acc[:, 0:HEAD_DIM] / jnp.maximum(acc[:, HEAD_DIM:HEAD_DIM + 1], 1e-30)


def _head_cols(h):
    g = h // REP
    pair = slice((h // 2) * LANES, (h // 2 + 1) * LANES)
    kcol = slice((2 * g + h % 2) * LANES, (2 * g + h % 2 + 1) * LANES)
    vcol = slice(g * LANES, (g + 1) * LANES)
    return pair, kcol, vcol


def _select_bias_chunk(keys, causal, thr, need, offset, tri, bias_ref, c):
    for j in range(keys.shape[1] // LANES):
        sl = slice(j * LANES, (j + 1) * LANES)
        kj = keys[:, sl]
        gt = jnp.where(kj > thr, 1.0, 0.0)
        eq = jnp.where(kj == thr, 1.0, 0.0)
        prefix = _dot(eq.astype(BF16), tri) + offset
        sel = gt + jnp.where(prefix < need, eq, 0.0)
        bias_ref[c, :, sl] = jnp.where(causal[:, sl], (sel - 1.0) * 1e30, NEG)
        offset = offset + jnp.sum(eq, axis=1, keepdims=True)
    return offset


def _dsa_kernel(qi_ref, wq_ref, ki_ref, qa_ref, ka_ref, va_ref, o_ref, keys_ref, bias_ref,
                *, tq, kc, n_keep):
    t0 = pl.program_id(1) * tq
    n_chunks = (t0 + tq + kc - 1) // kc
    wq = wq_ref[0] * (IDX_HEADS ** -0.5 * IDX_DIM ** -0.5)
    qi = qi_ref[0]
    q_heads = [qi[:, h * IDX_DIM:(h + 1) * IDX_DIM] for h in range(IDX_HEADS)]
    w_heads = [wq[:, 40 + h:41 + h] for h in range(IDX_HEADS)]
    rowpos = t0 + lax.broadcasted_iota(I32, (tq, kc), 0)
    lanepos = lax.broadcasted_iota(I32, (tq, kc), 1)

    def score_chunk(c, carry):
        k0 = pl.multiple_of(c * kc, kc)
        ki = ki_ref[0, pl.ds(k0, kc), :][:, 0:IDX_DIM]
        score = jnp.zeros((tq, kc), F32)
        for h in range(IDX_HEADS):
            score = score + w_heads[h] * jnp.maximum(_dot_nt(q_heads[h], ki), 0.0)
        keys_ref[c] = jnp.where(k0 + lanepos <= rowpos, _sort_key(score), INT_MIN)
        return carry

    lax.fori_loop(0, n_chunks, score_chunk, 0)

    def count_ge(cand):
        def body(c, acc):
            return acc + _lane_fold(jnp.where(keys_ref[c] >= cand, 1.0, 0.0))
        acc = lax.fori_loop(0, n_chunks, body, jnp.zeros((tq, LANES), F32))
        return jnp.sum(acc, axis=1, keepdims=True)

    thr = _kth_largest_key(count_ge, tq, n_keep)
    need = float(n_keep) - count_ge(thr + 1)
    tri = _tri_strict_upper(LANES, BF16)

    def bias_chunk(c, offset):
        causal = c * kc + lanepos <= rowpos
        return _select_bias_chunk(keys_ref[c], causal, thr, need, offset, tri, bias_ref, c)

    lax.fori_loop(0, n_chunks, bias_chunk, jnp.zeros((tq, 1), F32))

    for h in range(A_HEADS):
        pair, kcol, vcol = _head_cols(h)
        q = qa_ref[0, :, pair]

        def body(c, carry, q=q, kcol=kcol, vcol=vcol):
            k0 = pl.multiple_of(c * kc, kc)
            return _flash_step(q, ka_ref[0, pl.ds(k0, kc), kcol], va_ref[0, pl.ds(k0, kc), vcol],
                               bias_ref[c], *carry)

        _, acc = lax.fori_loop(0, n_chunks, body,
                               (jnp.full((tq, 1), NEG, F32), jnp.zeros((tq, LANES), F32)))
        o_ref[0, :, h * HEAD_DIM:(h + 1) * HEAD_DIM] = _flash_finish(acc).astype(o_ref.dtype)


def _dsa(qi, kiw, kib, qa, ka, va, tq, kc):
    bsz, seq, _ = qa.shape
    n_keep = min(DSA_TOPK_MAX, seq // 4)
    qblk = lambda w: pl.BlockSpec((1, tq, w), lambda b, i: (b, i, 0))
    full = lambda w: pl.BlockSpec((1, seq, w), lambda b, i: (b, 0, 0))
    return pl.pallas_call(
        functools.partial(_dsa_kernel, tq=tq, kc=kc, n_keep=n_keep),
        out_shape=jax.ShapeDtypeStruct((bsz, seq, A_HEADS * HEAD_DIM), BF16),
        grid=(bsz, seq // tq),
        in_specs=[qblk(256), qblk(LANES), full(LANES), qblk(512), full(512), full(256)],
        out_specs=qblk(512),
        scratch_shapes=[pltpu.VMEM((seq // kc, tq, kc), I32), pltpu.VMEM((seq // kc, tq, kc), F32)],
        compiler_params=_cparams(("parallel", "parallel")),
        name="dsa",
    )(qi, kiw, kib, qa, ka, va)


def _gelu_tanh(x):
    return 0.5 * x * (1.0 + jnp.tanh(math.sqrt(2.0 / math.pi) * (x + 0.044715 * (x * x * x))))


def _compress_kernel(k_ref, v_ref, pk_ref, pv_ref, w1k_ref, w2k_ref, w1v_ref, w2v_ref,
                     kc_ref, vc_ref, *, n_rows):
    half = CMP_BLOCK // 2

    def one(x_ref, p_ref, w1_ref, w2_ref, o_ref, both_halves):
        for g in range(B_KV_HEADS):
            lo = jnp.zeros((n_rows, CMP_HIDDEN), F32)
            hi = jnp.zeros((n_rows, CMP_HIDDEN), F32)
            for l in range(half):
                xl = x_ref[0, :, l * LANES + g * HEAD_DIM:l * LANES + (g + 1) * HEAD_DIM]
                a = (xl + p_ref[l:l + 1, :]).astype(BF16)
                b = (xl + p_ref[half + l:half + l + 1, :]).astype(BF16)
                lo = lo + _dot(a, w1_ref[l * HEAD_DIM:(l + 1) * HEAD_DIM, :].astype(BF16))
                hi = hi + _dot(b, w1_ref[(half + l) * HEAD_DIM:(half + l + 1) * HEAD_DIM, :].astype(BF16))
            hid = lo + pltpu.roll(hi, n_rows - 1, axis=0)
            out = _dot(_gelu_tanh(hid).astype(BF16), w2_ref[...].astype(BF16))
            zero = jnp.zeros_like(out)
            if both_halves:
                o_ref[0, 2 * g] = jnp.concatenate([out, zero], axis=1).astype(o_ref.dtype)
                o_ref[0, 2 * g + 1] = jnp.concatenate([zero, out], axis=1).astype(o_ref.dtype)
            else:
                o_ref[0, g] = jnp.concatenate([out, zero], axis=1).astype(o_ref.dtype)

    one(k_ref, pk_ref, w1k_ref, w2k_ref, kc_ref, True)
    one(v_ref, pv_ref, w1v_ref, w2v_ref, vc_ref, False)


def _compress(kcmp, vcmp, pos_k, pos_v, w1k, w2k, w1v, w2v):
    bsz, n_rows, width = kcmp.shape
    xspec = pl.BlockSpec((1, n_rows, width), lambda b: (b, 0, 0))
    cst = lambda a: pl.BlockSpec(a.shape, lambda b: (0,) * a.ndim)
    kspec = pl.BlockSpec((1, 2 * B_KV_HEADS, n_rows, LANES), lambda b: (b, 0, 0, 0))
    vspec = pl.BlockSpec((1, B_KV_HEADS, n_rows, LANES), lambda b: (b, 0, 0, 0))
    return pl.pallas_call(
        functools.partial(_compress_kernel, n_rows=n_rows),
        out_shape=(jax.ShapeDtypeStruct((bsz, 2 * B_KV_HEADS, n_rows, LANES), BF16),
                   jax.ShapeDtypeStruct((bsz, B_KV_HEADS, n_rows, LANES), BF16)),
        grid=(bsz,),
        in_specs=[xspec, xspec, cst(pos_k), cst(pos_v), cst(w1k), cst(w2k), cst(w1v), cst(w2v)],
        out_specs=(kspec, vspec),
        compiler_params=_cparams(("parallel",)),
        name="compress",
    )(kcmp, vcmp, pos_k, pos_v, w1k, w2k, w1v, w2v)


def _split3(x):
    a = x.astype(BF16)
    r = x - a.astype(F32)
    b = r.astype(BF16)
    c = (r - b.astype(F32)).astype(BF16)
    return a, b, c


def _nsa_kernel(qraw_ref, qrot_ref, kc_ref, vc_ref, ksel_ref, vsel_ref, kwin_ref, vwin_ref,
                g_ref, exp_ref, o_ref, bias_ref, *, tq, kc, seq, n_c):
    t0 = pl.program_id(1) * tq
    n_chunks = (t0 + tq + kc - 1) // kc
    n_s = seq // SEL_BLOCK
    n_pick = min(SEL_COUNT, n_s)
    span = WINDOW + tq
    gates = g_ref[0]

    cl = lax.broadcasted_iota(I32, (tq, LANES), 1)
    tl = t0 + lax.broadcasted_iota(I32, (tq, LANES), 0)
    valid_c = (cl * CMP_STRIDE + CMP_BLOCK - 1 <= tl) & (cl < n_c)
    cr = lax.broadcasted_iota(I32, (LANES, LANES), 0) * CMP_STRIDE
    js = lax.broadcasted_iota(I32, (LANES, LANES), 1) * SEL_BLOCK
    overlap = jnp.where((cr <= js + SEL_BLOCK - 1) & (cr + CMP_BLOCK - 1 >= js), 1.0, 0.0).astype(BF16)
    cur = tl // SEL_BLOCK
    forced = (cl == 0) | ((cur - cl >= 0) & (cur - cl < SEL_LOCAL))
    blk_causal = cl * SEL_BLOCK <= tl
    tri = _tri_strict_upper(LANES, BF16)
    rowpos = t0 + lax.broadcasted_iota(I32, (tq, kc), 0)
    lanepos = lax.broadcasted_iota(I32, (tq, kc), 1)
    w0 = pl.multiple_of(jnp.clip(t0 - WINDOW, 0, seq - span), tq)
    wdiff = (t0 + lax.broadcasted_iota(I32, (tq, span), 0)) - (w0 + lax.broadcasted_iota(I32, (tq, span), 1))
    wbias = jnp.where((wdiff >= 0) & (wdiff < WINDOW), 0.0, NEG)
    m_init = jnp.full((tq, 1), NEG, F32)
    acc_init = jnp.zeros((tq, LANES), F32)

    for g in range(B_KV_HEADS):
        o_cmp = []
        p_sum = jnp.zeros((tq, LANES), F32)
        for r in range(REP):
            h = g * REP + r
            pair, _, _ = _head_cols(h)
            s = jnp.where(valid_c, _dot_nt(qraw_ref[0, :, pair], kc_ref[0, 2 * g + h % 2]), NEG)
            e = jnp.where(valid_c, jnp.exp(s - jnp.max(s, axis=1, keepdims=True)), 0.0)
            p = e / jnp.maximum(jnp.sum(e, axis=1, keepdims=True), 1e-30)
            o_cmp.append(_dot(p.astype(BF16), vc_ref[0, g])[:, 0:HEAD_DIM])
            p_sum = p_sum + p
        pa, pb, pc = _split3(p_sum)
        imp = _dot(pa, overlap) + _dot(pb, overlap) + _dot(pc, overlap)
        imp = jnp.where(forced, FORCE, jnp.where(blk_causal, imp, NEG))
        keys = jnp.where(cl < n_s, _sort_key(imp), INT_MIN)
        count_ge = lambda cand, keys=keys: jnp.sum(jnp.where(keys >= cand, 1.0, 0.0), axis=1, keepdims=True)
        thr = _kth_largest_key(count_ge, tq, n_pick)
        gt = jnp.where(keys > thr, 1.0, 0.0)
        eq = jnp.where(keys == thr, 1.0, 0.0)
        need = float(n_pick) - jnp.sum(gt, axis=1, keepdims=True)
        prefix = _dot(eq.astype(BF16), tri)
        blk_sel = (gt + jnp.where(prefix < need, eq, 0.0)).astype(BF16)

        def bias_chunk(c, carry, blk_sel=blk_sel):
            tok_sel = _dot(blk_sel, exp_ref[c])
            bias_ref[c] = jnp.where(c * kc + lanepos <= rowpos, (tok_sel - 1.0) * 1e30, NEG)
            return carry

        lax.fori_loop(0, n_chunks, bias_chunk, 0)

        for r in range(REP):
            h = g * REP + r
            pair, kcol, vcol = _head_cols(h)
            q = qrot_ref[0, :, pair]

            def body(c, carry, q=q, kcol=kcol, vcol=vcol):
                k0 = pl.multiple_of(c * kc, kc)
                return _flash_step(q, ksel_ref[0, pl.ds(k0, kc), kcol], vsel_ref[0, pl.ds(k0, kc), vcol],
                                   bias_ref[c], *carry)

            _, acc_sel = lax.fori_loop(0, n_chunks, body, (m_init, acc_init))
            _, acc_win = _flash_step(q, kwin_ref[0, pl.ds(w0, span), kcol], vwin_ref[0, pl.ds(w0, span), vcol],
                                     wbias, m_init, acc_init)
            o = (gates[:, 3 * h:3 * h + 1] * o_cmp[r] + gates[:, 3 * h + 1:3 * h + 2] * _flash_finish(acc_sel)
                 + gates[:, 3 * h + 2:3 * h + 3] * _flash_finish(acc_win))
            o_ref[0, :, h * HEAD_DIM:(h + 1) * HEAD_DIM] = o.astype(o_ref.dtype)


def _nsa(qraw, qrot, kc_x, vc_x, ksel, vsel, kwin, vwin, gnsa, tq, kc):
    bsz, seq, _ = qraw.shape
    n_c = (seq - CMP_BLOCK) // CMP_STRIDE + 1
    key_blk = (jnp.arange(seq, dtype=I32) // SEL_BLOCK).reshape(seq // kc, 1, kc)
    expand = (key_blk == jnp.arange(LANES, dtype=I32).reshape(1, LANES, 1)).astype(BF16)
    qblk = lambda w: pl.BlockSpec((1, tq, w), lambda b, i: (b, i, 0))
    full = lambda w: pl.BlockSpec((1, seq, w), lambda b, i: (b, 0, 0))
    cspec = lambda a: pl.BlockSpec((1,) + a.shape[1:], lambda b, i: (b, 0, 0, 0))
    return pl.pallas_call(
        functools.partial(_nsa_kernel, tq=tq, kc=kc, seq=seq, n_c=n_c),
        out_shape=jax.ShapeDtypeStruct((bsz, seq, B_HEADS * HEAD_DIM), BF16),
        grid=(bsz, seq // tq),
        in_specs=[qblk(512), qblk(512), cspec(kc_x), cspec(vc_x), full(512), full(256), full(512),
                  full(256), qblk(LANES), pl.BlockSpec(expand.shape, lambda b, i: (0, 0, 0))],
        out_specs=qblk(512),
        scratch_shapes=[pltpu.VMEM((seq // kc, tq, kc), F32)],
        compiler_params=_cparams(("parallel", "parallel")),
        name="nsa",
    )(qraw, qrot, kc_x, vc_x, ksel, vsel, kwin, vwin, gnsa, expand)


def _pack_pairs(x):
    n = x.shape[1] // 2
    lo = pltpu.bitcast(x[:, :n].astype(BF16).astype(F32), I32)
    hi = pltpu.bitcast(x[:, n:].astype(BF16).astype(F32), I32)
    return lax.shift_right_logical(lo, 16) | (hi & jnp.int32(-65536))


def _unpack_pairs(p):
    lo = pltpu.bitcast(lax.shift_left(p, 16), F32)
    hi = pltpu.bitcast(p & jnp.int32(-65536), F32)
    return jnp.concatenate([lo, hi], axis=1)


def _layer_norm(y, g, b):
    mu = jnp.mean(y, axis=1, keepdims=True)
    yc = y - mu
    var = jnp.mean(yc * yc, axis=1, keepdims=True)
    return yc * lax.rsqrt(var + LN_EPS) * g + b


def _out_proj_kernel(oa_ref, ob_ref, ga_ref, gb_ref, x_ref, mod_ref, wa_ref, wb_ref, wo_ref,
                     g1_ref, b1_ref, wrh_ref, wrl_ref, x1_ref, u2_ref, lg_ref):
    mod = mod_ref[0]
    merged = (ga_ref[...].astype(F32) * _dot(oa_ref[...], wa_ref[...])
              + gb_ref[...].astype(F32) * _dot(ob_ref[...], wb_ref[...]))
    mix = _dot(merged.astype(BF16), wo_ref[...])
    x1 = _layer_norm(DN_ALPHA * x_ref[...] + mod[2:3, :] * mix, g1_ref[...], b1_ref[...])
    x1_ref[...] = x1
    u2 = x1 * (1.0 + mod[4:5, :]) + mod[3:4, :]
    packed = _pack_pairs(u2)
    for j in range(PIECES):
        u2_ref[j] = packed[:, j * SC_ROW:(j + 1) * SC_ROW]
    uh = u2.astype(BF16)
    ul = (u2 - uh.astype(F32)).astype(BF16)
    lg_ref[...] = _dot_nt(wrh_ref[...], uh) + _dot_nt(wrh_ref[...], ul) + _dot_nt(wrl_ref[...], uh)


def _out_proj(oa, ob, ga, gb, x2, mod3, wa, wb, wo, g1, b1, wrh, wrl, seq, tm):
    t, d = x2.shape
    per_b = seq // tm
    row = lambda w: pl.BlockSpec((tm, w), lambda i: (i, 0))
    cst = lambda a: pl.BlockSpec(a.shape, lambda i: (0,) * a.ndim)
    return pl.pallas_call(
        _out_proj_kernel,
        out_shape=(jax.ShapeDtypeStruct((t, d), F32), jax.ShapeDtypeStruct((PIECES, t, SC_ROW), I32),
                   jax.ShapeDtypeStruct((N_EXPERTS, t), F32)),
        grid=(t // tm,),
        in_specs=[row(512), row(512), row(d), row(d), row(d),
                  pl.BlockSpec((1, 6, d), lambda i: (i // per_b, 0, 0)),
                  cst(wa), cst(wb), cst(wo), cst(g1), cst(b1), cst(wrh), cst(wrl)],
        out_specs=(row(d), pl.BlockSpec((PIECES, tm, SC_ROW), lambda i: (0, i, 0)),
                   pl.BlockSpec((N_EXPERTS, tm), lambda i: (0, i))),
        compiler_params=_cparams(("parallel",)),
        name="out_proj",
    )(oa, ob, ga, gb, x2, mod3, wa, wb, wo, g1, b1, wrh, wrl)


def _first_max(x, rows):
    m = jnp.max(x, axis=0, keepdims=True)
    idx = jnp.min(jnp.where(x == m, rows, 1e9), axis=0, keepdims=True)
    return m, idx


def _router_kernel(lg_ref, rb_ref, idx_ref, w_ref, pos_ref, cnt_ref, carry_ref, *, tm):
    @pl.when(pl.program_id(0) == 0)
    def _():
        carry_ref[...] = jnp.zeros_like(carry_ref)

    per_g = N_EXPERTS // N_GROUPS
    scores = _sigmoid(lg_ref[...])
    choice = scores + rb_ref[...][:, 0:1]
    rows = lax.broadcasted_iota(I32, (N_EXPERTS, tm), 0).astype(F32)
    rows_g = lax.broadcasted_iota(I32, (per_g, tm), 0).astype(F32)
    ninf = -jnp.inf

    gs = []
    for g in range(N_GROUPS):
        x = choice[g * per_g:(g + 1) * per_g, :]
        m1, i1 = _first_max(x, rows_g)
        m2 = jnp.max(jnp.where(rows_g == i1, ninf, x), axis=0, keepdims=True)
        gs.append(m1 + m2)
    gscore = jnp.concatenate(gs, axis=0)
    rows8 = lax.broadcasted_iota(I32, (N_GROUPS, tm), 0).astype(F32)
    keep = jnp.zeros((N_GROUPS, tm), F32)
    for _ in range(TOPK_GROUPS):
        _, gi = _first_max(gscore, rows8)
        hit = rows8 == gi
        keep = jnp.where(hit, 1.0, keep)
        gscore = jnp.where(hit, ninf, gscore)
    keep_full = jnp.concatenate(
        [jnp.broadcast_to(keep[g:g + 1, :], (per_g, tm)) for g in range(N_GROUPS)], axis=0)
    masked = jnp.where(keep_full > 0.0, choice, NEG)

    idxs, ws = [], []
    onehot = jnp.zeros((N_EXPERTS, tm), F32)
    for _ in range(TOP_K):
        _, ei = _first_max(masked, rows)
        hit = rows == ei
        idxs.append(ei)
        ws.append(jnp.sum(jnp.where(hit, scores, 0.0), axis=0, keepdims=True))
        masked = jnp.where(hit, ninf, masked)
        onehot = jnp.where(hit, 1.0, onehot)
    idx = jnp.concatenate(idxs, axis=0)
    w = jnp.concatenate(ws, axis=0)
    idx_ref[...] = idx.astype(I32)
    w_ref[...] = w / jnp.sum(w, axis=0, keepdims=True) * ROUTED_SCALE

    tri = _tri_strict_upper(tm, BF16)
    base = _dot(onehot.astype(BF16), tri) + carry_ref[...][:, 0:1]
    pos = [jnp.sum(jnp.where(rows == idxs[k], base, 0.0), axis=0, keepdims=True) for k in range(TOP_K)]
    pos_ref[...] = jnp.concatenate(pos, axis=0).astype(I32)
    carry = carry_ref[...] + jnp.sum(onehot, axis=1, keepdims=True)
    carry_ref[...] = carry
    cnt_ref[...] = carry


def _router(lg, router_bias, tm):
    e, t = lg.shape
    rb = jnp.broadcast_to(router_bias.reshape(e, 1).astype(F32), (e, LANES))
    tok = lambda r: pl.BlockSpec((r, tm), lambda i: (0, i))
    return pl.pallas_call(
        functools.partial(_router_kernel, tm=tm),
        out_shape=(jax.ShapeDtypeStruct((TOP_K, t), I32), jax.ShapeDtypeStruct((TOP_K, t), F32),
                   jax.ShapeDtypeStruct((TOP_K, t), I32), jax.ShapeDtypeStruct((e, LANES), F32)),
        grid=(t // tm,),
        in_specs=[tok(e), pl.BlockSpec((e, LANES), lambda i: (0, 0))],
        out_specs=(tok(TOP_K), tok(TOP_K), tok(TOP_K), pl.BlockSpec((e, LANES), lambda i: (0, 0))),
        scratch_shapes=[pltpu.VMEM((e, LANES), F32)],
        compiler_params=_cparams(("arbitrary",)),
        name="router",
    )(lg, rb)


def _dest_kernel(idx_ref, pos_ref, st_ref, dest_ref, *, tm):
    rows = lax.broadcasted_iota(I32, (N_EXPERTS, tm), 0)
    starts = st_ref[...][:, 0:1]
    idx = idx_ref[...]
    out = []
    for k in range(TOP_K):
        out.append(jnp.sum(jnp.where(rows == idx[k:k + 1, :], starts, 0.0), axis=0, keepdims=True))
    dest_ref[...] = jnp.concatenate(out, axis=0).astype(I32) + pos_ref[...]


def _dest(idx, pos, starts, tm):
    k, t = idx.shape
    st = jnp.broadcast_to(starts.reshape(N_EXPERTS, 1).astype(F32), (N_EXPERTS, LANES))
    tok = pl.BlockSpec((k, tm), lambda i: (0, i))
    return pl.pallas_call(
        functools.partial(_dest_kernel, tm=tm),
        out_shape=jax.ShapeDtypeStruct((k, t), I32),
        grid=(t // tm,),
        in_specs=[tok, tok, pl.BlockSpec((N_EXPERTS, LANES), lambda i: (0, 0))],
        out_specs=tok,
        compiler_params=_cparams(("parallel",)),
        name="dest",
    )(idx, pos, st)


def _experts_kernel(blk_ref, used_ref, x_ref, wg_ref, wu_ref, wd_ref, y_ref):
    @pl.when(pl.program_id(0) < used_ref[0])
    def _():
        x = jnp.concatenate([x_ref[j] for j in range(PIECES)], axis=1)
        x = _unpack_pairs(x).astype(BF16)
        a = _dot(x, wg_ref[0].astype(BF16))
        b = _dot(x, wu_ref[0].astype(BF16))
        h = (a * _sigmoid(a) * b).astype(BF16)
        y = _pack_pairs(_dot(h, wd_ref[0].astype(BF16)))
        for j in range(PIECES):
            y_ref[j] = y[:, j * SC_ROW:(j + 1) * SC_ROW]


def _experts(xs, blk_e, n_used, wg, wu, wd, bm):
    _, cap, _ = xs.shape
    n_blocks = cap // bm
    d, f = wg.shape[1], wg.shape[2]
    rows = pl.BlockSpec((PIECES, bm, SC_ROW), lambda b, blk, used: (0, jnp.minimum(b, used[0] - 1), 0))
    return pl.pallas_call(
        _experts_kernel,
        out_shape=jax.ShapeDtypeStruct(xs.shape, I32),
        grid_spec=pltpu.PrefetchScalarGridSpec(
            num_scalar_prefetch=2,
            grid=(n_blocks,),
            in_specs=[rows,
                      pl.BlockSpec((1, d, f), lambda b, blk, used: (blk[b], 0, 0)),
                      pl.BlockSpec((1, d, f), lambda b, blk, used: (blk[b], 0, 0)),
                      pl.BlockSpec((1, f, d), lambda b, blk, used: (blk[b], 0, 0))],
            out_specs=rows),
        compiler_params=_cparams(("arbitrary",)),
        name="experts",
    )(blk_e, n_used, xs, wg, wu, wd)


def _final_kernel(x1_ref, u2_ref, yg_ref, w_ref, mod_ref, sg_ref, su_ref, sd_ref, g2_ref, b2_ref, o_ref):
    mod = mod_ref[0]
    w = w_ref[...]
    rows = lambda ref, *lead: jnp.concatenate([ref[(j,) + lead] for j in range(PIECES)], axis=1)
    routed = w[:, 0:1] * _unpack_pairs(rows(yg_ref, 0))
    for k in range(1, TOP_K):
        routed = routed + w[:, k:k + 1] * _unpack_pairs(rows(yg_ref, k))
    u = _unpack_pairs(rows(u2_ref)).astype(BF16)
    a = _dot(u, sg_ref[...])
    b = _dot(u, su_ref[...])
    shared = _dot((a * _sigmoid(a) * b).astype(BF16), sd_ref[...])
    y = DN_ALPHA * x1_ref[...] + mod[5:6, :] * (routed + shared)
    o_ref[...] = _layer_norm(y, g2_ref[...], b2_ref[...])


def _final(x1, u2p, yg, wtok, mod3, sg, su, sd, g2, b2, seq, tm):
    t, d = x1.shape
    per_b = seq // tm
    row = lambda w: pl.BlockSpec((tm, w), lambda i: (i, 0))
    cst = lambda a: pl.BlockSpec(a.shape, lambda i: (0,) * a.ndim)
    return pl.pallas_call(
        _final_kernel,
        out_shape=jax.ShapeDtypeStruct((t, d), F32),
        grid=(t // tm,),
        in_specs=[row(d), pl.BlockSpec((PIECES, tm, SC_ROW), lambda i: (0, i, 0)),
                  pl.BlockSpec((PIECES, TOP_K, tm, SC_ROW), lambda i: (0, 0, i, 0)),
                  row(LANES), pl.BlockSpec((1, 6, d), lambda i: (i // per_b, 0, 0)),
                  cst(sg), cst(su), cst(sd), cst(g2), cst(b2)],
        out_specs=row(d),
        compiler_params=_cparams(("parallel",)),
        name="final",
    )(x1, u2p, yg, wtok, mod3, sg, su, sd, g2, b2)


def _sc_mesh():
    return plsc.VectorSubcoreMesh(core_axis_name="core", subcore_axis_name="subcore")


def _sc_scatter_rows(src, idx, n_out, src_block):
    n_idx = idx.shape[0]

    @functools.partial(pl.kernel, out_type=jax.ShapeDtypeStruct((n_out, SC_ROW), src.dtype),
                       mesh=_sc_mesh(), scratch_types=[])
    def k(x_hbm, i_hbm, o_hbm):
        def body(x_vmem, i_vmem):
            pltpu.sync_copy(x_vmem, o_hbm.at[i_vmem.at[0]])

        pltpu.emit_pipeline(
            body, grid=(n_idx // SC_WINDOW,),
            in_specs=[pl.BlockSpec((SC_WINDOW, SC_ROW), lambda i: (src_block(i), 0)),
                      pl.BlockSpec((1, SC_WINDOW), lambda i: (0, i))],
            out_specs=[],
            core_axis_name=("core", "subcore"),
            dimension_semantics=(pltpu.PARALLEL,),
        )(x_hbm, i_hbm)

    return k(src, idx.reshape(1, n_idx))


def _sc_gather_rows(src, idx):
    n_idx = idx.shape[0]

    @functools.partial(pl.kernel, out_type=jax.ShapeDtypeStruct((n_idx, SC_ROW), src.dtype),
                       mesh=_sc_mesh(), scratch_types=[])
    def k(x_hbm, i_hbm, o_hbm):
        def body(i_vmem, o_vmem):
            pltpu.sync_copy(x_hbm.at[i_vmem.at[0]], o_vmem)

        pltpu.emit_pipeline(
            body, grid=(n_idx // SC_WINDOW,),
            in_specs=[pl.BlockSpec((1, SC_WINDOW), lambda i: (0, i))],
            out_specs=[pl.BlockSpec((SC_WINDOW, SC_ROW), lambda i: (i, 0))],
            core_axis_name=("core", "subcore"),
            dimension_semantics=(pltpu.PARALLEL,),
        )(i_hbm, o_hbm)

    return k(src, idx.reshape(1, n_idx))


TM_PROJ = 512
TQ_ATTN = 128
KC_ATTN = 512
TM_ROUTE = 512
BM_EXPERT = 256


def _moe_plan(counts, n_tok):
    bm = BM_EXPERT
    padded = (counts + bm - 1) // bm * bm
    p_ends = jnp.cumsum(padded)
    starts = p_ends - padded
    n_blocks = n_tok * TOP_K // bm + N_EXPERTS
    blk_e = jnp.minimum(jnp.searchsorted(p_ends, jnp.arange(n_blocks, dtype=I32) * bm, side="right"),
                        N_EXPERTS - 1).astype(I32)
    n_used = (p_ends[-1] // bm).astype(I32).reshape(1)
    return starts, blk_e, n_used, n_blocks


def _layer(x, mod, positions, w_in, w_br_a, w_br_b, w_out, cmp_pos_k, cmp_pos_v, cmp_k_w1, cmp_k_w2,
           cmp_v_w1, cmp_v_w2, ln1_g, ln1_b, w_router, router_bias, w_exp_gate, w_exp_up, w_exp_down,
           w_sh_gate, w_sh_up, w_sh_down, ln2_g, ln2_b):
    bsz, seq, d = x.shape
    t = bsz * seq
    assert seq // CMP_STRIDE == LANES and seq % TQ_ATTN == 0 and seq % TM_PROJ == 0
    x2 = x.reshape(t, d)
    mod3 = mod.reshape(bsz, 6, d)

    outs = _in_proj(x2, mod3, _pack_w_in(w_in), _rope_tables(positions), seq, TM_PROJ)
    flat = {name: o for (name, _, _), o in zip(_IN_OUT, outs)}
    z = {name: o.reshape(bsz, seq, o.shape[1]) for name, o in flat.items()}

    o_a = _dsa(z["qi"], z["kiw"], z["kib"], z["qa"], z["ka"], z["va"], TQ_ATTN, KC_ATTN)

    n_rows = seq // CMP_STRIDE
    kc_x, vc_x = _compress(z["kcmp"].reshape(bsz, n_rows, CMP_STRIDE * LANES),
                           z["vcmp"].reshape(bsz, n_rows, CMP_STRIDE * LANES),
                           cmp_pos_k, cmp_pos_v, cmp_k_w1, cmp_k_w2, cmp_v_w1, cmp_v_w2)
    o_b = _nsa(z["qbraw"], z["qbrot"], kc_x, vc_x, z["ksel"], z["vsel"], z["kwin"], z["vwin"],
               z["gnsa"], TQ_ATTN, KC_ATTN)

    wr_hi = w_router.T.astype(BF16)
    wr_lo = (w_router.T - wr_hi.astype(F32)).astype(BF16)
    x1, u2p, logits = _out_proj(
        o_a.reshape(t, -1), o_b.reshape(t, -1), flat["ga"], flat["gb"], x2, mod3,
        w_br_a.astype(BF16), w_br_b.astype(BF16), w_out.astype(BF16),
        ln1_g.reshape(1, d), ln1_b.reshape(1, d), wr_hi, wr_lo, seq, TM_PROJ)

    idx, wts, pos, counts = _router(logits, router_bias, TM_ROUTE)
    starts, blk_e, n_used, n_blocks = _moe_plan(counts[:, 0].astype(I32), t)
    dest = _dest(idx, pos, starts, TM_ROUTE)

    cap = n_blocks * BM_EXPERT
    dest_p = (dest[None] + (jnp.arange(PIECES, dtype=I32) * cap).reshape(PIECES, 1, 1)).reshape(-1)
    tb = t // SC_WINDOW
    xs = _sc_scatter_rows(u2p.reshape(PIECES * t, SC_ROW), dest_p, cap * PIECES,
                          lambda i: (i // (TOP_K * tb)) * tb + i % tb)
    ys = _experts(xs.reshape(PIECES, cap, SC_ROW), blk_e, n_used, w_exp_gate, w_exp_up, w_exp_down,
                  BM_EXPERT)
    yg = _sc_gather_rows(ys.reshape(cap * PIECES, SC_ROW), dest_p).reshape(PIECES, TOP_K, t, SC_ROW)

    wtok = jnp.pad(wts.T, ((0, 0), (0, LANES - TOP_K)))
    return _final(x1, u2p, yg, wtok, mod3, w_sh_gate.astype(BF16), w_sh_up.astype(BF16),
                  w_sh_down.astype(BF16), ln2_g.reshape(1, d), ln2_b.reshape(1, d), seq, TM_PROJ
                  ).reshape(bsz, seq, d)


def kernel(x, c, positions, w_ada, b_ada, w_in, w_br_a, w_br_b, w_out, cmp_pos_k, cmp_pos_v, cmp_k_w1,
           cmp_k_w2, cmp_v_w1, cmp_v_w2, ln1_g, ln1_b, w_router, router_bias, w_exp_gate, w_exp_up,
           w_exp_down, w_sh_gate, w_sh_up, w_sh_down, ln2_g, ln2_b):
    for l in range(w_ada.shape[0]):
        mod = _mod(c, w_ada[l], b_ada[l])
        x = _layer(x, mod, positions, w_in[l], w_br_a[l], w_br_b[l], w_out[l], cmp_pos_k[l], cmp_pos_v[l],
                   cmp_k_w1[l], cmp_k_w2[l], cmp_v_w1[l], cmp_v_w2[l], ln1_g[l], ln1_b[l], w_router[l],
                   router_bias[l], w_exp_gate[l], w_exp_up[l], w_exp_down[l], w_sh_gate[l], w_sh_up[l],
                   w_sh_down[l], ln2_g[l], ln2_b[l])
    return x
```

```python
import functools
import math

import jax
import jax.numpy as jnp
import numpy as np
from jax import lax
from jax.experimental import pallas as pl
from jax.experimental.pallas import tpu as pltpu
from jax.experimental.pallas import tpu_sc as plsc

F32 = jnp.float32
BF16 = jnp.bfloat16
I32 = jnp.int32

D_MODEL = 1024
HEAD_DIM = 64
ROPE_THETA = 500000.0
ROPE_FRACTION = 4
A_HEADS = 8
A_KV_HEADS = 2
IDX_HEADS = 8
IDX_DIM = 32
DSA_TOPK_MAX = 256
B_HEADS = 8
B_KV_HEADS = 2
REP = 4
CMP_BLOCK = 32
CMP_STRIDE = 16
CMP_HIDDEN = 256
SEL_BLOCK = 64
SEL_COUNT = 16
SEL_LOCAL = 2
WINDOW = 512
N_EXPERTS = 256
TOP_K = 8
N_GROUPS = 8
TOPK_GROUPS = 4
ROUTED_SCALE = 2.5
DEPTH = 1
DN_ALPHA = (2 * DEPTH) ** 0.25
LN_EPS = 1e-5
NEG = -1e30
FORCE = 1e9
INT_MIN = -2147483648
N_GATES = 3 * B_HEADS

LANES = 128
SUBLANES = 8
VMEM_LIMIT = 56 * 1024 * 1024
SC_WINDOW = 128
SC_ROW = 128
PIECES = (D_MODEL // 2) // SC_ROW

TM_PROJ = 512
TQ_ATTN = 128
KC_ATTN = 512
TM_ROUTE = 512
BM_EXPERT = 256

_IN_WIDTHS = (512, 128, 128, 256, 32, 8, 512, 128, 128, 128, 128, 128, 128, 24, 1024, 1024)
_IN_OFFS = np.concatenate([[0], np.cumsum(_IN_WIDTHS)]).tolist()

NT_DIMS = (((1,), (1,)), ((), ()))


def _cparams(sem):
    return pltpu.CompilerParams(dimension_semantics=sem, vmem_limit_bytes=VMEM_LIMIT)


def _sigmoid(x):
    return 1.0 / (1.0 + jnp.exp(-x))


def _dot(a, b):
    return jnp.dot(a, b, preferred_element_type=F32)


def _dot_nt(a, b):
    return lax.dot_general(a, b, NT_DIMS, preferred_element_type=F32)


def _sort_key(x):
    x = jnp.where(x == 0.0, 0.0, x)
    bits = pltpu.bitcast(x, I32)
    return jnp.where(bits < 0, bits ^ 0x7FFFFFFF, bits)


def _kth_largest_key(count_ge, shape, k):
    kf = float(k)
    t0 = jnp.where(count_ge(jnp.zeros(shape, I32)) >= kf, 0, INT_MIN).astype(I32)

    def body(it, t):
        cand = t + jnp.left_shift(jnp.int32(1), 30 - it)
        return jnp.where(count_ge(cand) >= kf, cand, t)

    return lax.fori_loop(0, 31, body, t0)


def _tri_strict_lower(n, dtype):
    r = lax.broadcasted_iota(I32, (n, n), 0)
    c = lax.broadcasted_iota(I32, (n, n), 1)
    return jnp.where(c < r, 1.0, 0.0).astype(dtype)


def _tri_strict_upper(n, dtype):
    r = lax.broadcasted_iota(I32, (n, n), 0)
    c = lax.broadcasted_iota(I32, (n, n), 1)
    return jnp.where(r < c, 1.0, 0.0).astype(dtype)


def _mod_kernel(c_ref, w_ref, b_ref, o_ref):
    c = c_ref[...]
    cond = (c * _sigmoid(c)).astype(BF16)
    o_ref[...] = _dot(cond, w_ref[...].astype(BF16)) + b_ref[...]


def _mod(c, w_ada, b_ada):
    bsz, d = c.shape
    n = w_ada.shape[1]
    tn = 1024
    return pl.pallas_call(
        _mod_kernel,
        out_shape=jax.ShapeDtypeStruct((bsz, n), F32),
        grid=(n // tn,),
        in_specs=[pl.BlockSpec((bsz, d), lambda j: (0, 0)),
                  pl.BlockSpec((d, tn), lambda j: (0, j)),
                  pl.BlockSpec((1, tn), lambda j: (0, j))],
        out_specs=pl.BlockSpec((bsz, tn), lambda j: (0, j)),
        compiler_params=_cparams(("parallel",)),
        name="mod",
    )(c, w_ada, b_ada.reshape(1, n))


def _rope(z, c_tab, s_tab, period, half):
    w = z.shape[1]
    reps = w // LANES
    c = jnp.concatenate([c_tab] * reps, axis=1) if reps > 1 else c_tab
    s = jnp.concatenate([s_tab] * reps, axis=1) if reps > 1 else s_tab
    lane = lax.broadcasted_iota(I32, z.shape, 1)
    first = (lane & (period - 1)) < half
    partner = jnp.where(first, pltpu.roll(z, w - half, axis=1), pltpu.roll(z, half, axis=1))
    return z * c + partner * s


def _in_proj_kernel(x_ref, mod_ref, w_ref, wsm_ref, c64_ref, s64_ref, c32_ref, s32_ref,
                    qa_ref, ka_ref, vaT_ref, qi_ref, ki_ref, wT_ref, qbraw_ref, qbrot_ref,
                    kcmp_ref, vcmp_ref, ksel_ref, vselT_ref, kwin_ref, vwinT_ref, gT_ref, ga_ref, gb_ref):
    mod = mod_ref[0]
    u = (x_ref[...] * (1.0 + mod[1:2, :]) + mod[0:1, :]).astype(BF16)
    tm = u.shape[0]
    c64, s64, c32, s32 = c64_ref[...], s64_ref[...], c32_ref[...], s32_ref[...]
    scale = HEAD_DIM ** -0.5
    lane = lax.broadcasted_iota(I32, (tm, LANES), 1)
    low = lane < HEAD_DIM

    def proj(a, b):
        return _dot(u, w_ref[:, a:b])

    rope64 = lambda z: _rope(z, c64, s64, HEAD_DIM, HEAD_DIM // ROPE_FRACTION // 2)
    rope32 = lambda z: _rope(z, c32, s32, IDX_DIM, IDX_DIM // ROPE_FRACTION // 2)

    def head_slots64(z):
        out = []
        for h in range(A_HEADS):
            pair = z[:, (h // 2) * LANES:(h // 2 + 1) * LANES]
            g = h // REP
            src = pair if h % 2 == g else pltpu.roll(pair, HEAD_DIM, axis=1)
            out.append(jnp.where(low, src, 0.0) if g == 0 else jnp.where(low, 0.0, src))
        return jnp.concatenate(out, axis=1).astype(BF16)

    def head_slots32(z):
        per = LANES // IDX_DIM
        out = []
        for h in range(IDX_HEADS):
            col = z[:, (h // per) * LANES:(h // per + 1) * LANES]
            shift = IDX_DIM * (h % per)
            src = col if shift == 0 else pltpu.roll(col, LANES - shift, axis=1)
            out.append(jnp.where(lane < IDX_DIM, src, 0.0))
        return jnp.concatenate(out, axis=1).astype(BF16)

    def store_vt(ref, z, chunk):
        zt = z.T
        ones = jnp.ones((HEAD_DIM, chunk), F32)
        for g in range(A_KV_HEADS):
            for j in range(tm // chunk):
                blk = zt[g * HEAD_DIM:(g + 1) * HEAD_DIM, j * chunk:(j + 1) * chunk]
                ref[g, j] = jnp.concatenate([blk, ones], axis=0).astype(BF16)

    qa_ref[...] = head_slots64(rope64(proj(0, 512)) * scale)
    ka_ref[...] = rope64(proj(512, 640)).astype(BF16)
    store_vt(vaT_ref, proj(640, 768), tm)
    qi_ref[...] = head_slots32(rope32(proj(768, 1024)))
    ki_ref[...] = rope32(proj(1024, 1152)).astype(BF16)
    qb = proj(1152, 1664)
    qbraw_ref[...] = head_slots64(qb * scale)
    qbrot_ref[...] = head_slots64(rope64(qb) * scale)
    kcmp_ref[...] = proj(1664, 1792)
    vcmp_ref[...] = proj(1792, 1920)
    ksel_ref[...] = rope64(proj(1920, 2048)).astype(BF16)
    store_vt(vselT_ref, proj(2048, 2176), tm)
    kwin_ref[...] = rope64(proj(2176, 2304)).astype(BF16)
    store_vt(vwinT_ref, proj(2304, 2432), LANES)
    ga_ref[...] = _sigmoid(proj(2432, 3456)).astype(BF16)
    gb_ref[...] = _sigmoid(proj(3456, 4480)).astype(BF16)
    small = _dot_nt(wsm_ref[...], u)
    wT_ref[...] = small[0:IDX_HEADS, :]
    gT_ref[...] = _sigmoid(small[IDX_HEADS:IDX_HEADS + N_GATES, :])


def _pack_w_in(w_in):
    d = w_in.shape[0]
    col = lambda i: w_in[:, _IN_OFFS[i]:_IN_OFFS[i + 1]]
    ki = jnp.concatenate([col(4), jnp.zeros((d, LANES - IDX_DIM), w_in.dtype)], axis=1)
    parts = [col(0), col(1), col(2), col(3), ki, col(6), col(7), col(8), col(9), col(10),
             col(11), col(12), col(14), col(15)]
    w_small = jnp.concatenate([col(5), col(13)], axis=1).T
    return jnp.concatenate(parts, axis=1).astype(BF16), w_small.astype(BF16)


def _rope_tables(positions):
    pos = positions.astype(F32).reshape(-1, 1)

    def tab(dim):
        rot = dim // ROPE_FRACTION
        half = rot // 2
        inv = ROPE_THETA ** (-(jnp.arange(half, dtype=F32) * 2.0) / rot)
        ang = pos * inv
        cos, sin = jnp.cos(ang), jnp.sin(ang)
        ones = jnp.ones((pos.shape[0], dim - rot), F32)
        c = jnp.concatenate([cos, cos, ones], axis=1)
        s = jnp.concatenate([-sin, sin, 0.0 * ones], axis=1)
        return jnp.tile(c, (1, LANES // dim)), jnp.tile(s, (1, LANES // dim))

    return tab(HEAD_DIM) + tab(IDX_DIM)


def _in_proj(x2, mod3, w_pack, w_small, tabs, seq, tm):
    t, d = x2.shape
    n = w_pack.shape[1]
    per_b = seq // tm
    g = A_KV_HEADS
    row = lambda w: pl.BlockSpec((tm, w), lambda i: (i, 0))
    tok = lambda r: pl.BlockSpec((r, tm), lambda i: (0, i))
    vt_chunk = pl.BlockSpec((g, 1, LANES, tm), lambda i: (0, i, 0, 0))
    vt_lane = pl.BlockSpec((g, tm // LANES, LANES, LANES), lambda i: (0, i, 0, 0))
    sds = jax.ShapeDtypeStruct
    vt_chunk_shape = sds((g, t // tm, LANES, tm), BF16)
    outs = (("qa", sds((t, 1024), BF16), row(1024)), ("ka", sds((t, LANES), BF16), row(LANES)),
            ("vaT", vt_chunk_shape, vt_chunk), ("qi", sds((t, 1024), BF16), row(1024)),
            ("ki", sds((t, LANES), BF16), row(LANES)), ("wT", sds((IDX_HEADS, t), F32), tok(IDX_HEADS)),
            ("qbraw", sds((t, 1024), BF16), row(1024)), ("qbrot", sds((t, 1024), BF16), row(1024)),
            ("kcmp", sds((t, LANES), F32), row(LANES)), ("vcmp", sds((t, LANES), F32), row(LANES)),
            ("ksel", sds((t, LANES), BF16), row(LANES)), ("vselT", vt_chunk_shape, vt_chunk),
            ("kwin", sds((t, LANES), BF16), row(LANES)),
            ("vwinT", sds((g, t // LANES, LANES, LANES), BF16), vt_lane),
            ("gT", sds((N_GATES, t), F32), tok(N_GATES)),
            ("ga", sds((t, d), BF16), row(d)), ("gb", sds((t, d), BF16), row(d)))
    res = pl.pallas_call(
        _in_proj_kernel,
        out_shape=tuple(o[1] for o in outs),
        grid=(t // tm,),
        in_specs=[row(d),
                  pl.BlockSpec((1, 6, d), lambda i: (i // per_b, 0, 0)),
                  pl.BlockSpec((d, n), lambda i: (0, 0)),
                  pl.BlockSpec(w_small.shape, lambda i: (0, 0)),
                  row(LANES), row(LANES), row(LANES), row(LANES)],
        out_specs=tuple(o[2] for o in outs),
        compiler_params=_cparams(("parallel",)),
        name="in_proj",
    )(x2, mod3, w_pack, w_small, *tabs)
    return {o[0]: r for o, r in zip(outs, res)}


def _fold_rows(x, op):
    n = x.shape[0]
    while n % (2 * SUBLANES) == 0:
        n //= 2
        x = op(x[:n], x[n:])
    slabs = [x[i * SUBLANES:(i + 1) * SUBLANES] for i in range(n // SUBLANES)]
    while len(slabs) > 1:
        nxt = [op(slabs[i], slabs[i + 1]) for i in range(0, len(slabs) - 1, 2)]
        slabs = nxt + ([slabs[-1]] if len(slabs) % 2 else [])
    return slabs[0]


def _col_max(x):
    return jnp.max(_fold_rows(x, jnp.maximum), axis=0, keepdims=True)


def _col_sum(x):
    return jnp.sum(_fold_rows(x, jnp.add), axis=0, keepdims=True)


def _count_rows(mask01):
    return _fold_rows(mask01, jnp.add)


def _stack_heads(q_ref, heads):
    return jnp.concatenate([q_ref[0, :, h * LANES:(h + 1) * LANES] for h in heads], axis=0)


def _flash_step(k, q_stack, v_t, bias4, m, acc):
    s = _dot_nt(k, q_stack) + bias4
    m_new = jnp.maximum(m, _col_max(s))
    e = jnp.exp(s - m_new).astype(BF16)
    return m_new, acc * jnp.exp(m - m_new) + _dot(v_t, e)


HEADS_PER_CHAIN = REP


def _head_chains():
    return [(h // REP, tuple(range(h, h + HEADS_PER_CHAIN))) for h in range(0, A_HEADS, HEADS_PER_CHAIN)]


def _flash_loop(q_ref, k_ref, vT_ref, bias_of, n_chunks, tq, kc):
    chains = _head_chains()
    q_stacks = [_stack_heads(q_ref, heads) for _, heads in chains]
    width = HEADS_PER_CHAIN * tq

    def body(c, carry):
        k = k_ref[0, pl.ds(pl.multiple_of(c * kc, kc), kc), :]
        out = []
        for (g, _), q_stack, (m, acc) in zip(chains, q_stacks, carry):
            bias = jnp.concatenate([bias_of(g, c)] * HEADS_PER_CHAIN, axis=1)
            out.append(_flash_step(k, q_stack, vT_ref[g, c], bias, m, acc))
        return tuple(out)

    init = tuple((jnp.full((1, width), NEG, F32), jnp.zeros((LANES, width), F32)) for _ in chains)
    res = lax.fori_loop(0, n_chunks, body, init)
    return [(heads, acc) for (_, heads), (_, acc) in zip(chains, res)]


def _normalise(acc):
    return acc / jnp.maximum(acc[HEAD_DIM:HEAD_DIM + 1, :], 1e-30)


def _store_heads(o_ref, o_t, heads, tq):
    o = o_t.T
    for r, h in enumerate(heads):
        o_ref[0, :, h * HEAD_DIM:(h + 1) * HEAD_DIM] = o[r * tq:(r + 1) * tq, 0:HEAD_DIM].astype(o_ref.dtype)


def _select_mask(keys, thr, need, offset, tri):
    gt = jnp.where(keys > thr, 1.0, 0.0)
    eq = jnp.where(keys == thr, 1.0, 0.0)
    prefix = _dot(tri, eq.astype(BF16)) + offset
    return gt + jnp.where(prefix < need, eq, 0.0), offset + _col_sum(eq)


def _dsa_kernel(qi_ref, wT_ref, ki_ref, qa_ref, ka_ref, vT_ref, o_ref, keys_ref, bias_ref,
                *, tq, kc, n_keep):
    t0 = pl.program_id(1) * tq
    n_chunks = (t0 + tq + kc - 1) // kc
    wT = wT_ref[...] * (IDX_HEADS ** -0.5 * IDX_DIM ** -0.5)
    qi_stack = _stack_heads(qi_ref, range(IDX_HEADS))
    keypos = lax.broadcasted_iota(I32, (kc, tq), 0)
    qpos = t0 + lax.broadcasted_iota(I32, (kc, tq), 1)

    def score_chunk(c, carry):
        k0 = pl.multiple_of(c * kc, kc)
        sc = _dot_nt(ki_ref[0, pl.ds(k0, kc), :], qi_stack)
        score = jnp.zeros((kc, tq), F32)
        for h in range(IDX_HEADS):
            score = score + wT[h:h + 1, :] * jnp.maximum(sc[:, h * tq:(h + 1) * tq], 0.0)
        keys_ref[c] = jnp.where(k0 + keypos <= qpos, _sort_key(score), INT_MIN)
        return carry

    lax.fori_loop(0, n_chunks, score_chunk, 0)

    def count_ge(cand):
        def body(c, acc):
            return acc + _count_rows(jnp.where(keys_ref[c] >= cand, 1.0, 0.0))
        acc = lax.fori_loop(0, n_chunks, body, jnp.zeros((SUBLANES, tq), F32))
        return jnp.sum(acc, axis=0, keepdims=True)

    thr = _kth_largest_key(count_ge, (1, tq), n_keep)
    need = float(n_keep) - count_ge(thr + 1)
    tri = _tri_strict_lower(LANES, BF16)
    sub = lax.broadcasted_iota(I32, (LANES, tq), 0)
    qsub = t0 + lax.broadcasted_iota(I32, (LANES, tq), 1)

    def bias_chunk(c, offset):
        for j in range(kc // LANES):
            rows = slice(j * LANES, (j + 1) * LANES)
            sel, offset = _select_mask(keys_ref[c, rows, :], thr, need, offset, tri)
            causal = c * kc + j * LANES + sub <= qsub
            bias_ref[c, rows, :] = jnp.where(causal, (sel - 1.0) * 1e30, NEG)
        return offset

    lax.fori_loop(0, n_chunks, bias_chunk, jnp.zeros((1, tq), F32))

    accs = _flash_loop(qa_ref, ka_ref, vT_ref, lambda g, c: bias_ref[c], n_chunks, tq, kc)
    for heads, acc in accs:
        _store_heads(o_ref, _normalise(acc), heads, tq)


def _dsa(qi, wT, ki, qa, ka, vaT, bsz, seq, tq, kc):
    n_keep = min(DSA_TOPK_MAX, seq // 4)
    nq = seq // tq
    qblk = lambda w: pl.BlockSpec((1, tq, w), lambda b, i: (b, i, 0))
    full = lambda w: pl.BlockSpec((1, seq, w), lambda b, i: (b, 0, 0))
    return pl.pallas_call(
        functools.partial(_dsa_kernel, tq=tq, kc=kc, n_keep=n_keep),
        out_shape=jax.ShapeDtypeStruct((bsz, seq, A_HEADS * HEAD_DIM), BF16),
        grid=(bsz, nq),
        in_specs=[qblk(1024), pl.BlockSpec((IDX_HEADS, tq), lambda b, i: (0, b * nq + i)), full(LANES),
                  qblk(1024), full(LANES),
                  pl.BlockSpec((A_KV_HEADS, seq // kc, LANES, kc), lambda b, i: (0, b, 0, 0))],
        out_specs=qblk(512),
        scratch_shapes=[pltpu.VMEM((seq // kc, kc, tq), I32), pltpu.VMEM((seq // kc, kc, tq), F32)],
        compiler_params=_cparams(("parallel", "parallel")),
        name="dsa",
    )(qi, wT, ki, qa, ka, vaT)


def _gelu_tanh(x):
    return 0.5 * x * (1.0 + jnp.tanh(math.sqrt(2.0 / math.pi) * (x + 0.044715 * (x * x * x))))


def _compress_kernel(k_ref, v_ref, pk_ref, pv_ref, w1k_ref, w2k_ref, w1v_ref, w2v_ref,
                     kc_ref, vcT_ref, *, n_rows):
    half = CMP_BLOCK // 2

    def one(x_ref, p_ref, w1_ref, w2_ref):
        outs = []
        for g in range(B_KV_HEADS):
            lo = jnp.zeros((n_rows, CMP_HIDDEN), F32)
            hi = jnp.zeros((n_rows, CMP_HIDDEN), F32)
            for l in range(half):
                xl = x_ref[0, :, l * LANES + g * HEAD_DIM:l * LANES + (g + 1) * HEAD_DIM]
                a = (xl + p_ref[l:l + 1, :]).astype(BF16)
                b = (xl + p_ref[half + l:half + l + 1, :]).astype(BF16)
                lo = lo + _dot(a, w1_ref[l * HEAD_DIM:(l + 1) * HEAD_DIM, :].astype(BF16))
                hi = hi + _dot(b, w1_ref[(half + l) * HEAD_DIM:(half + l + 1) * HEAD_DIM, :].astype(BF16))
            hid = lo + pltpu.roll(hi, n_rows - 1, axis=0)
            outs.append(_dot(_gelu_tanh(hid).astype(BF16), w2_ref[...].astype(BF16)))
        return outs

    k0, k1 = one(k_ref, pk_ref, w1k_ref, w2k_ref)
    kc_ref[0] = jnp.concatenate([k0, k1], axis=1).astype(kc_ref.dtype)
    for g, v in enumerate(one(v_ref, pv_ref, w1v_ref, w2v_ref)):
        vcT_ref[0, g] = jnp.concatenate([v, jnp.zeros_like(v)], axis=1).T.astype(vcT_ref.dtype)


def _compress(kcmp, vcmp, pos_k, pos_v, w1k, w2k, w1v, w2v):
    bsz, n_rows, width = kcmp.shape
    xspec = pl.BlockSpec((1, n_rows, width), lambda b: (b, 0, 0))
    cst = lambda a: pl.BlockSpec(a.shape, lambda b: (0,) * a.ndim)
    return pl.pallas_call(
        functools.partial(_compress_kernel, n_rows=n_rows),
        out_shape=(jax.ShapeDtypeStruct((bsz, n_rows, LANES), BF16),
                   jax.ShapeDtypeStruct((bsz, B_KV_HEADS, LANES, n_rows), BF16)),
        grid=(bsz,),
        in_specs=[xspec, xspec, cst(pos_k), cst(pos_v), cst(w1k), cst(w2k), cst(w1v), cst(w2v)],
        out_specs=(pl.BlockSpec((1, n_rows, LANES), lambda b: (b, 0, 0)),
                   pl.BlockSpec((1, B_KV_HEADS, LANES, n_rows), lambda b: (b, 0, 0, 0))),
        compiler_params=_cparams(("parallel",)),
        name="compress",
    )(kcmp, vcmp, pos_k, pos_v, w1k, w2k, w1v, w2v)


def _split3(x):
    a = x.astype(BF16)
    r = x - a.astype(F32)
    b = r.astype(BF16)
    c = (r - b.astype(F32)).astype(BF16)
    return a, b, c


def _nsa_kernel(qraw_ref, qrot_ref, kc_ref, vcT_ref, ksel_ref, vselT_ref, kwin_ref, vwinT_ref,
                gT_ref, exp_ref, o_ref, bias_ref, *, tq, kc, seq, n_c):
    t0 = pl.program_id(1) * tq
    n_chunks = (t0 + tq + kc - 1) // kc
    n_s = seq // SEL_BLOCK
    n_pick = min(SEL_COUNT, n_s)
    span = WINDOW + tq
    gT = gT_ref[...]

    ridx = lax.broadcasted_iota(I32, (LANES, tq), 0)
    tl = t0 + lax.broadcasted_iota(I32, (LANES, tq), 1)
    valid_c = jnp.where((ridx * CMP_STRIDE + CMP_BLOCK - 1 <= tl) & (ridx < n_c), 1.0, 0.0)
    valid4 = jnp.concatenate([valid_c] * REP, axis=1) > 0.0
    js = lax.broadcasted_iota(I32, (LANES, LANES), 0) * SEL_BLOCK
    cs = lax.broadcasted_iota(I32, (LANES, LANES), 1) * CMP_STRIDE
    overlap_t = jnp.where((cs <= js + SEL_BLOCK - 1) & (cs + CMP_BLOCK - 1 >= js), 1.0, 0.0).astype(BF16)
    cur = tl // SEL_BLOCK
    forced = (ridx == 0) | ((cur - ridx >= 0) & (cur - ridx < SEL_LOCAL))
    blk_causal = ridx * SEL_BLOCK <= tl
    tri = _tri_strict_lower(LANES, BF16)
    keypos = lax.broadcasted_iota(I32, (kc, tq), 0)
    qpos = t0 + lax.broadcasted_iota(I32, (kc, tq), 1)
    w0 = pl.multiple_of(jnp.clip(t0 - WINDOW, 0, seq - span), tq)
    wdiff = (t0 + lax.broadcasted_iota(I32, (span, tq), 1)) - (w0 + lax.broadcasted_iota(I32, (span, tq), 0))
    wbias = jnp.where((wdiff >= 0) & (wdiff < WINDOW), 0.0, NEG)
    wbias_c = jnp.concatenate([wbias] * HEADS_PER_CHAIN, axis=1)
    wblk = w0 // LANES

    o_cmp_g = []
    for g in range(B_KV_HEADS):
        heads = range(g * REP, (g + 1) * REP)
        s = jnp.where(valid4, _dot_nt(kc_ref[0], _stack_heads(qraw_ref, heads)), NEG)
        e = jnp.where(valid4, jnp.exp(s - _col_max(s)), 0.0)
        p = e / jnp.maximum(_col_sum(e), 1e-30)
        o_cmp_g.append(_dot(vcT_ref[0, g], p.astype(BF16)))
        p_sum = p[:, 0:tq]
        for r in range(1, REP):
            p_sum = p_sum + p[:, r * tq:(r + 1) * tq]
        pa, pb, pc = _split3(p_sum)
        imp = _dot(overlap_t, pa) + _dot(overlap_t, pb) + _dot(overlap_t, pc)
        imp = jnp.where(forced, FORCE, jnp.where(blk_causal, imp, NEG))
        keys = jnp.where(ridx < n_s, _sort_key(imp), INT_MIN)
        count_ge = lambda cand, keys=keys: _col_sum(jnp.where(keys >= cand, 1.0, 0.0))
        thr = _kth_largest_key(count_ge, (1, tq), n_pick)
        need = float(n_pick) - _col_sum(jnp.where(keys > thr, 1.0, 0.0))
        blk_sel, _ = _select_mask(keys, thr, need, jnp.zeros((1, tq), F32), tri)
        blk_sel = blk_sel.astype(BF16)

        def bias_chunk(c, carry, blk_sel=blk_sel, g=g):
            tok_sel = _dot(exp_ref[c], blk_sel)
            bias_ref[g, c] = jnp.where(c * kc + keypos <= qpos, (tok_sel - 1.0) * 1e30, NEG)
            return carry

        lax.fori_loop(0, n_chunks, bias_chunk, 0)

    sel = _flash_loop(qrot_ref, ksel_ref, vselT_ref, lambda g, c: bias_ref[g, c], n_chunks, tq, kc)

    kwin = kwin_ref[0, pl.ds(w0, span), :]
    for (g, heads), (_, acc_sel) in zip(_head_chains(), sel):
        sw = _dot_nt(kwin, _stack_heads(qrot_ref, heads)) + wbias_c
        ew = jnp.exp(sw - _col_max(sw)).astype(BF16)
        acc_win = _dot(vwinT_ref[g, wblk], ew[0:LANES, :])
        for j in range(1, span // LANES):
            acc_win = acc_win + _dot(vwinT_ref[g, wblk + j], ew[j * LANES:(j + 1) * LANES, :])

        o_sel = _normalise(acc_sel)
        o_win = _normalise(acc_win)
        cols = []
        for r, h in enumerate(heads):
            col = slice(r * tq, (r + 1) * tq)
            cmp_col = slice((h % REP) * tq, (h % REP + 1) * tq)
            cols.append(gT[3 * h:3 * h + 1, :] * o_cmp_g[g][:, cmp_col]
                        + gT[3 * h + 1:3 * h + 2, :] * o_sel[:, col]
                        + gT[3 * h + 2:3 * h + 3, :] * o_win[:, col])
        _store_heads(o_ref, jnp.concatenate(cols, axis=1), heads, tq)


def _nsa(qraw, qrot, kc_x, vcT, ksel, vselT, kwin, vwinT, gT, bsz, seq, tq, kc):
    n_c = (seq - CMP_BLOCK) // CMP_STRIDE + 1
    nq = seq // tq
    key_blk = (jnp.arange(seq, dtype=I32) // SEL_BLOCK).reshape(seq // kc, kc, 1)
    expand = (key_blk == jnp.arange(LANES, dtype=I32).reshape(1, 1, LANES)).astype(BF16)
    qblk = lambda w: pl.BlockSpec((1, tq, w), lambda b, i: (b, i, 0))
    full = lambda w: pl.BlockSpec((1, seq, w), lambda b, i: (b, 0, 0))
    g = B_KV_HEADS
    return pl.pallas_call(
        functools.partial(_nsa_kernel, tq=tq, kc=kc, seq=seq, n_c=n_c),
        out_shape=jax.ShapeDtypeStruct((bsz, seq, B_HEADS * HEAD_DIM), BF16),
        grid=(bsz, nq),
        in_specs=[qblk(1024), qblk(1024),
                  pl.BlockSpec((1,) + kc_x.shape[1:], lambda b, i: (b, 0, 0)),
                  pl.BlockSpec((1,) + vcT.shape[1:], lambda b, i: (b, 0, 0, 0)),
                  full(LANES), pl.BlockSpec((g, seq // kc, LANES, kc), lambda b, i: (0, b, 0, 0)),
                  full(LANES), pl.BlockSpec((g, seq // LANES, LANES, LANES), lambda b, i: (0, b, 0, 0)),
                  pl.BlockSpec((N_GATES, tq), lambda b, i: (0, b * nq + i)),
                  pl.BlockSpec(expand.shape, lambda b, i: (0, 0, 0))],
        out_specs=qblk(512),
        scratch_shapes=[pltpu.VMEM((g, seq // kc, kc, tq), F32)],
        compiler_params=_cparams(("parallel", "parallel")),
        name="nsa",
    )(qraw, qrot, kc_x, vcT, ksel, vselT, kwin, vwinT, gT, expand)


def _pack_pairs(x):
    n = x.shape[1] // 2
    lo = pltpu.bitcast(x[:, :n].astype(BF16).astype(F32), I32)
    hi = pltpu.bitcast(x[:, n:].astype(BF16).astype(F32), I32)
    return lax.shift_right_logical(lo, 16) | (hi & jnp.int32(-65536))


def _unpack_pairs(p):
    lo = pltpu.bitcast(lax.shift_left(p, 16), F32)
    hi = pltpu.bitcast(p & jnp.int32(-65536), F32)
    return jnp.concatenate([lo, hi], axis=1)


def _layer_norm(y, g, b):
    mu = jnp.mean(y, axis=1, keepdims=True)
    yc = y - mu
    var = jnp.mean(yc * yc, axis=1, keepdims=True)
    return yc * lax.rsqrt(var + LN_EPS) * g + b


def _out_proj_kernel(oa_ref, ob_ref, ga_ref, gb_ref, x_ref, mod_ref, wa_ref, wb_ref, wo_ref,
                     g1_ref, b1_ref, wrh_ref, wrl_ref, x1_ref, u2_ref, lg_ref):
    mod = mod_ref[0]
    merged = (ga_ref[...].astype(F32) * _dot(oa_ref[...], wa_ref[...])
              + gb_ref[...].astype(F32) * _dot(ob_ref[...], wb_ref[...]))
    mix = _dot(merged.astype(BF16), wo_ref[...])
    x1 = _layer_norm(DN_ALPHA * x_ref[...] + mod[2:3, :] * mix, g1_ref[...], b1_ref[...])
    x1_ref[...] = x1
    u2 = x1 * (1.0 + mod[4:5, :]) + mod[3:4, :]
    packed = _pack_pairs(u2)
    for j in range(PIECES):
        u2_ref[j] = packed[:, j * SC_ROW:(j + 1) * SC_ROW]
    uh = u2.astype(BF16)
    ul = (u2 - uh.astype(F32)).astype(BF16)
    lg_ref[...] = _dot_nt(wrh_ref[...], uh) + _dot_nt(wrh_ref[...], ul) + _dot_nt(wrl_ref[...], uh)


def _out_proj(oa, ob, ga, gb, x2, mod3, wa, wb, wo, g1, b1, wrh, wrl, seq, tm):
    t, d = x2.shape
    per_b = seq // tm
    row = lambda w: pl.BlockSpec((tm, w), lambda i: (i, 0))
    cst = lambda a: pl.BlockSpec(a.shape, lambda i: (0,) * a.ndim)
    return pl.pallas_call(
        _out_proj_kernel,
        out_shape=(jax.ShapeDtypeStruct((t, d), F32), jax.ShapeDtypeStruct((PIECES, t, SC_ROW), I32),
                   jax.ShapeDtypeStruct((N_EXPERTS, t), F32)),
        grid=(t // tm,),
        in_specs=[row(512), row(512), row(d), row(d), row(d),
                  pl.BlockSpec((1, 6, d), lambda i: (i // per_b, 0, 0)),
                  cst(wa), cst(wb), cst(wo), cst(g1), cst(b1), cst(wrh), cst(wrl)],
        out_specs=(row(d), pl.BlockSpec((PIECES, tm, SC_ROW), lambda i: (0, i, 0)),
                   pl.BlockSpec((N_EXPERTS, tm), lambda i: (0, i))),
        compiler_params=_cparams(("parallel",)),
        name="out_proj",
    )(oa, ob, ga, gb, x2, mod3, wa, wb, wo, g1, b1, wrh, wrl)


def _first_max(x, rows):
    m = jnp.max(x, axis=0, keepdims=True)
    idx = jnp.min(jnp.where(x == m, rows, 1e9), axis=0, keepdims=True)
    return m, idx


def _router_kernel(lg_ref, rb_ref, idx_ref, w_ref, pos_ref, cnt_ref, carry_ref, *, tm):
    @pl.when(pl.program_id(0) == 0)
    def _():
        carry_ref[...] = jnp.zeros_like(carry_ref)

    per_g = N_EXPERTS // N_GROUPS
    scores = _sigmoid(lg_ref[...])
    choice = scores + rb_ref[...][:, 0:1]
    rows = lax.broadcasted_iota(I32, (N_EXPERTS, tm), 0).astype(F32)
    rows_g = lax.broadcasted_iota(I32, (per_g, tm), 0).astype(F32)
    ninf = -jnp.inf

    gs = []
    for g in range(N_GROUPS):
        x = choice[g * per_g:(g + 1) * per_g, :]
        m1, i1 = _first_max(x, rows_g)
        m2 = jnp.max(jnp.where(rows_g == i1, ninf, x), axis=0, keepdims=True)
        gs.append(m1 + m2)
    gscore = jnp.concatenate(gs, axis=0)
    rows8 = lax.broadcasted_iota(I32, (N_GROUPS, tm), 0).astype(F32)
    keep = jnp.zeros((N_GROUPS, tm), F32)
    for _ in range(TOPK_GROUPS):
        _, gi = _first_max(gscore, rows8)
        hit = rows8 == gi
        keep = jnp.where(hit, 1.0, keep)
        gscore = jnp.where(hit, ninf, gscore)
    keep_full = jnp.concatenate(
        [jnp.broadcast_to(keep[g:g + 1, :], (per_g, tm)) for g in range(N_GROUPS)], axis=0)
    masked = jnp.where(keep_full > 0.0, choice, NEG)

    idxs, ws = [], []
    onehot = jnp.zeros((N_EXPERTS, tm), F32)
    for _ in range(TOP_K):
        _, ei = _first_max(masked, rows)
        hit = rows == ei
        idxs.append(ei)
        ws.append(jnp.sum(jnp.where(hit, scores, 0.0), axis=0, keepdims=True))
        masked = jnp.where(hit, ninf, masked)
        onehot = jnp.where(hit, 1.0, onehot)
    idx = jnp.concatenate(idxs, axis=0)
    w = jnp.concatenate(ws, axis=0)
    idx_ref[...] = idx.astype(I32)
    w_ref[...] = w / jnp.sum(w, axis=0, keepdims=True) * ROUTED_SCALE

    tri = _tri_strict_upper(tm, BF16)
    base = _dot(onehot.astype(BF16), tri) + carry_ref[...][:, 0:1]
    pos = [jnp.sum(jnp.where(rows == idxs[k], base, 0.0), axis=0, keepdims=True) for k in range(TOP_K)]
    pos_ref[...] = jnp.concatenate(pos, axis=0).astype(I32)
    carry = carry_ref[...] + jnp.sum(onehot, axis=1, keepdims=True)
    carry_ref[...] = carry
    cnt_ref[...] = carry


def _router(lg, router_bias, tm):
    e, t = lg.shape
    rb = jnp.broadcast_to(router_bias.reshape(e, 1).astype(F32), (e, LANES))
    tok = lambda r: pl.BlockSpec((r, tm), lambda i: (0, i))
    return pl.pallas_call(
        functools.partial(_router_kernel, tm=tm),
        out_shape=(jax.ShapeDtypeStruct((TOP_K, t), I32), jax.ShapeDtypeStruct((TOP_K, t), F32),
                   jax.ShapeDtypeStruct((TOP_K, t), I32), jax.ShapeDtypeStruct((e, LANES), F32)),
        grid=(t // tm,),
        in_specs=[tok(e), pl.BlockSpec((e, LANES), lambda i: (0, 0))],
        out_specs=(tok(TOP_K), tok(TOP_K), tok(TOP_K), pl.BlockSpec((e, LANES), lambda i: (0, 0))),
        scratch_shapes=[pltpu.VMEM((e, LANES), F32)],
        compiler_params=_cparams(("arbitrary",)),
        name="router",
    )(lg, rb)


def _dest_kernel(idx_ref, pos_ref, st_ref, dest_ref, *, tm):
    rows = lax.broadcasted_iota(I32, (N_EXPERTS, tm), 0)
    starts = st_ref[...][:, 0:1]
    idx = idx_ref[...]
    out = []
    for k in range(TOP_K):
        out.append(jnp.sum(jnp.where(rows == idx[k:k + 1, :], starts, 0.0), axis=0, keepdims=True))
    dest_ref[...] = jnp.concatenate(out, axis=0).astype(I32) + pos_ref[...]


def _dest(idx, pos, starts, tm):
    k, t = idx.shape
    st = jnp.broadcast_to(starts.reshape(N_EXPERTS, 1).astype(F32), (N_EXPERTS, LANES))
    tok = pl.BlockSpec((k, tm), lambda i: (0, i))
    return pl.pallas_call(
        functools.partial(_dest_kernel, tm=tm),
        out_shape=jax.ShapeDtypeStruct((k, t), I32),
        grid=(t // tm,),
        in_specs=[tok, tok, pl.BlockSpec((N_EXPERTS, LANES), lambda i: (0, 0))],
        out_specs=tok,
        compiler_params=_cparams(("parallel",)),
        name="dest",
    )(idx, pos, st)


def _experts_kernel(blk_ref, used_ref, x_ref, wg_ref, wu_ref, wd_ref, y_ref):
    @pl.when(pl.program_id(0) < used_ref[0])
    def _():
        x = jnp.concatenate([x_ref[j] for j in range(PIECES)], axis=1)
        x = _unpack_pairs(x).astype(BF16)
        a = _dot(x, wg_ref[0].astype(BF16))
        b = _dot(x, wu_ref[0].astype(BF16))
        h = (a * _sigmoid(a) * b).astype(BF16)
        y = _pack_pairs(_dot(h, wd_ref[0].astype(BF16)))
        for j in range(PIECES):
            y_ref[j] = y[:, j * SC_ROW:(j + 1) * SC_ROW]


def _experts(xs, blk_e, n_used, wg, wu, wd, bm):
    _, cap, _ = xs.shape
    n_blocks = cap // bm
    d, f = wg.shape[1], wg.shape[2]
    rows = pl.BlockSpec((PIECES, bm, SC_ROW), lambda b, blk, used: (0, jnp.minimum(b, used[0] - 1), 0))
    return pl.pallas_call(
        _experts_kernel,
        out_shape=jax.ShapeDtypeStruct(xs.shape, I32),
        grid_spec=pltpu.PrefetchScalarGridSpec(
            num_scalar_prefetch=2,
            grid=(n_blocks,),
            in_specs=[rows,
                      pl.BlockSpec((1, d, f), lambda b, blk, used: (blk[b], 0, 0)),
                      pl.BlockSpec((1, d, f), lambda b, blk, used: (blk[b], 0, 0)),
                      pl.BlockSpec((1, f, d), lambda b, blk, used: (blk[b], 0, 0))],
            out_specs=rows),
        compiler_params=_cparams(("arbitrary",)),
        name="experts",
    )(blk_e, n_used, xs, wg, wu, wd)


def _final_kernel(x1_ref, u2_ref, yg_ref, w_ref, mod_ref, sg_ref, su_ref, sd_ref, g2_ref, b2_ref, o_ref):
    mod = mod_ref[0]
    w = w_ref[...]
    rows = lambda ref, *lead: jnp.concatenate([ref[(j,) + lead] for j in range(PIECES)], axis=1)
    routed = w[:, 0:1] * _unpack_pairs(rows(yg_ref, 0))
    for k in range(1, TOP_K):
        routed = routed + w[:, k:k + 1] * _unpack_pairs(rows(yg_ref, k))
    u = _unpack_pairs(rows(u2_ref)).astype(BF16)
    a = _dot(u, sg_ref[...])
    b = _dot(u, su_ref[...])
    shared = _dot((a * _sigmoid(a) * b).astype(BF16), sd_ref[...])
    y = DN_ALPHA * x1_ref[...] + mod[5:6, :] * (routed + shared)
    o_ref[...] = _layer_norm(y, g2_ref[...], b2_ref[...])


def _final(x1, u2p, yg, wtok, mod3, sg, su, sd, g2, b2, seq, tm):
    t, d = x1.shape
    per_b = seq // tm
    row = lambda w: pl.BlockSpec((tm, w), lambda i: (i, 0))
    cst = lambda a: pl.BlockSpec(a.shape, lambda i: (0,) * a.ndim)
    return pl.pallas_call(
        _final_kernel,
        out_shape=jax.ShapeDtypeStruct((t, d), F32),
        grid=(t // tm,),
        in_specs=[row(d), pl.BlockSpec((PIECES, tm, SC_ROW), lambda i: (0, i, 0)),
                  pl.BlockSpec((PIECES, TOP_K, tm, SC_ROW), lambda i: (0, 0, i, 0)),
                  row(LANES), pl.BlockSpec((1, 6, d), lambda i: (i // per_b, 0, 0)),
                  cst(sg), cst(su), cst(sd), cst(g2), cst(b2)],
        out_specs=row(d),
        compiler_params=_cparams(("parallel",)),
        name="final",
    )(x1, u2p, yg, wtok, mod3, sg, su, sd, g2, b2)


def _sc_mesh():
    return plsc.VectorSubcoreMesh(core_axis_name="core", subcore_axis_name="subcore")


def _sc_scatter_rows(src, idx, n_out, src_block):
    n_idx = idx.shape[0]

    @functools.partial(pl.kernel, out_type=jax.ShapeDtypeStruct((n_out, SC_ROW), src.dtype),
                       mesh=_sc_mesh(), scratch_types=[])
    def k(x_hbm, i_hbm, o_hbm):
        def body(x_vmem, i_vmem):
            pltpu.sync_copy(x_vmem, o_hbm.at[i_vmem.at[0]])

        pltpu.emit_pipeline(
            body, grid=(n_idx // SC_WINDOW,),
            in_specs=[pl.BlockSpec((SC_WINDOW, SC_ROW), lambda i: (src_block(i), 0)),
                      pl.BlockSpec((1, SC_WINDOW), lambda i: (0, i))],
            out_specs=[],
            core_axis_name=("core", "subcore"),
            dimension_semantics=(pltpu.PARALLEL,),
        )(x_hbm, i_hbm)

    return k(src, idx.reshape(1, n_idx))


def _sc_gather_rows(src, idx):
    n_idx = idx.shape[0]

    @functools.partial(pl.kernel, out_type=jax.ShapeDtypeStruct((n_idx, SC_ROW), src.dtype),
                       mesh=_sc_mesh(), scratch_types=[])
    def k(x_hbm, i_hbm, o_hbm):
        def body(i_vmem, o_vmem):
            pltpu.sync_copy(x_hbm.at[i_vmem.at[0]], o_vmem)

        pltpu.emit_pipeline(
            body, grid=(n_idx // SC_WINDOW,),
            in_specs=[pl.BlockSpec((1, SC_WINDOW), lambda i: (0, i))],
            out_specs=[pl.BlockSpec((SC_WINDOW, SC_ROW), lambda i: (i, 0))],
            core_axis_name=("core", "subcore"),
            dimension_semantics=(pltpu.PARALLEL,),
        )(i_hbm, o_hbm)

    return k(src, idx.reshape(1, n_idx))


def _moe_plan(counts, n_tok):
    bm = BM_EXPERT
    padded = (counts + bm - 1) // bm * bm
    p_ends = jnp.cumsum(padded)
    starts = p_ends - padded
    n_blocks = n_tok * TOP_K // bm + N_EXPERTS
    blk_e = jnp.minimum(jnp.searchsorted(p_ends, jnp.arange(n_blocks, dtype=I32) * bm, side="right"),
                        N_EXPERTS - 1).astype(I32)
    n_used = (p_ends[-1] // bm).astype(I32).reshape(1)
    return starts, blk_e, n_used, n_blocks


def _layer(x, mod, positions, w_in, w_br_a, w_br_b, w_out, cmp_pos_k, cmp_pos_v, cmp_k_w1, cmp_k_w2,
           cmp_v_w1, cmp_v_w2, ln1_g, ln1_b, w_router, router_bias, w_exp_gate, w_exp_up, w_exp_down,
           w_sh_gate, w_sh_up, w_sh_down, ln2_g, ln2_b):
    bsz, seq, d = x.shape
    t = bsz * seq
    assert seq // CMP_STRIDE == LANES and seq % TQ_ATTN == 0 and seq % TM_PROJ == 0 and KC_ATTN == TM_PROJ
    x2 = x.reshape(t, d)
    mod3 = mod.reshape(bsz, 6, d)

    w_pack, w_small = _pack_w_in(w_in)
    z = _in_proj(x2, mod3, w_pack, w_small, _rope_tables(positions), seq, TM_PROJ)
    per_b = lambda name: z[name].reshape(bsz, seq, z[name].shape[1])

    o_a = _dsa(per_b("qi"), z["wT"], per_b("ki"), per_b("qa"), per_b("ka"), z["vaT"],
               bsz, seq, TQ_ATTN, KC_ATTN)

    n_rows = seq // CMP_STRIDE
    kc_x, vcT = _compress(z["kcmp"].reshape(bsz, n_rows, CMP_STRIDE * LANES),
                          z["vcmp"].reshape(bsz, n_rows, CMP_STRIDE * LANES),
                          cmp_pos_k, cmp_pos_v, cmp_k_w1, cmp_k_w2, cmp_v_w1, cmp_v_w2)
    o_b = _nsa(per_b("qbraw"), per_b("qbrot"), kc_x, vcT, per_b("ksel"), z["vselT"], per_b("kwin"),
               z["vwinT"], z["gT"], bsz, seq, TQ_ATTN, KC_ATTN)

    wr_hi = w_router.T.astype(BF16)
    wr_lo = (w_router.T - wr_hi.astype(F32)).astype(BF16)
    x1, u2p, logits = _out_proj(
        o_a.reshape(t, -1), o_b.reshape(t, -1), z["ga"], z["gb"], x2, mod3,
        w_br_a.astype(BF16), w_br_b.astype(BF16), w_out.astype(BF16),
        ln1_g.reshape(1, d), ln1_b.reshape(1, d), wr_hi, wr_lo, seq, TM_PROJ)

    idx, wts, pos, counts = _router(logits, router_bias, TM_ROUTE)
    starts, blk_e, n_used, n_blocks = _moe_plan(counts[:, 0].astype(I32), t)
    dest = _dest(idx, pos, starts, TM_ROUTE)

    cap = n_blocks * BM_EXPERT
    dest_p = (dest[None] + (jnp.arange(PIECES, dtype=I32) * cap).reshape(PIECES, 1, 1)).reshape(-1)
    tb = t // SC_WINDOW
    xs = _sc_scatter_rows(u2p.reshape(PIECES * t, SC_ROW), dest_p, cap * PIECES,
                          lambda i: (i // (TOP_K * tb)) * tb + i % tb)
    ys = _experts(xs.reshape(PIECES, cap, SC_ROW), blk_e, n_used, w_exp_gate, w_exp_up, w_exp_down,
                  BM_EXPERT)
    yg = _sc_gather_rows(ys.reshape(cap * PIECES, SC_ROW), dest_p).reshape(PIECES, TOP_K, t, SC_ROW)

    wtok = jnp.pad(wts.T, ((0, 0), (0, LANES - TOP_K)))
    return _final(x1, u2p, yg, wtok, mod3, w_sh_gate.astype(BF16), w_sh_up.astype(BF16),
                  w_sh_down.astype(BF16), ln2_g.reshape(1, d), ln2_b.reshape(1, d), seq, TM_PROJ
                  ).reshape(bsz, seq, d)


def kernel(x, c, positions, w_ada, b_ada, w_in, w_br_a, w_br_b, w_out, cmp_pos_k, cmp_pos_v, cmp_k_w1,
           cmp_k_w2, cmp_v_w1, cmp_v_w2, ln1_g, ln1_b, w_router, router_bias, w_exp_gate, w_exp_up,
           w_exp_down, w_sh_gate, w_sh_up, w_sh_down, ln2_g, ln2_b):
    for l in range(w_ada.shape[0]):
        mod = _mod(c, w_ada[l], b_ada[l])
        x = _layer(x, mod, positions, w_in[l], w_br_a[l], w_br_b[l], w_out[l], cmp_pos_k[l], cmp_pos_v[l],
                   cmp_k_w1[l], cmp_k_w2[l], cmp_v_w1[l], cmp_v_w2[l], ln1_g[l], ln1_b[l], w_router[l],
                   router_bias[l], w_exp_gate[l], w_exp_up[l], w_exp_down[l], w_sh_gate[l], w_sh_up[l],
                   w_sh_down[l], ln2_g[l], ln2_b[l])
    return x
```

```python
import functools
import math

import jax
import jax.numpy as jnp
import numpy as np
from jax import lax
from jax.experimental import pallas as pl
from jax.experimental.pallas import tpu as pltpu
from jax.experimental.pallas import tpu_sc as plsc

F32 = jnp.float32
BF16 = jnp.bfloat16
I32 = jnp.int32

D_MODEL = 1024
HEAD_DIM = 64
ROPE_THETA = 500000.0
ROPE_FRACTION = 4
A_HEADS = 8
A_KV_HEADS = 2
IDX_HEADS = 8
IDX_DIM = 32
DSA_TOPK_MAX = 256
B_HEADS = 8
B_KV_HEADS = 2
REP = 4
CMP_BLOCK = 32
CMP_STRIDE = 16
CMP_HIDDEN = 256
SEL_BLOCK = 64
SEL_COUNT = 16
SEL_LOCAL = 2
WINDOW = 512
N_EXPERTS = 256
TOP_K = 8
N_GROUPS = 8
TOPK_GROUPS = 4
ROUTED_SCALE = 2.5
DEPTH = 1
DN_ALPHA = (2 * DEPTH) ** 0.25
LN_EPS = 1e-5
NEG = -1e30
FORCE = 1e9
INT_MIN = -2147483648
N_GATES = 3 * B_HEADS

LANES = 128
SUBLANES = 8
VMEM_LIMIT = 56 * 1024 * 1024
SC_WINDOW = 128
SC_ROW = 128
PIECES = (D_MODEL // 2) // SC_ROW

TM_PROJ = 512
TQ_ATTN = 256
KC_ATTN = 512
TM_ROUTE = 512
BM_EXPERT = 256

_IN_WIDTHS = (512, 128, 128, 256, 32, 8, 512, 128, 128, 128, 128, 128, 128, 24, 1024, 1024)
_IN_OFFS = np.concatenate([[0], np.cumsum(_IN_WIDTHS)]).tolist()

NT_DIMS = (((1,), (1,)), ((), ()))


def _cparams(sem):
    return pltpu.CompilerParams(dimension_semantics=sem, vmem_limit_bytes=VMEM_LIMIT)


def _sigmoid(x):
    return 1.0 / (1.0 + jnp.exp(-x))


def _dot(a, b):
    return jnp.dot(a, b, preferred_element_type=F32)


def _dot_nt(a, b):
    return lax.dot_general(a, b, NT_DIMS, preferred_element_type=F32)


def _sort_key(x):
    x = jnp.where(x == 0.0, 0.0, x)
    bits = pltpu.bitcast(x, I32)
    return jnp.where(bits < 0, bits ^ 0x7FFFFFFF, bits)


def _kth_largest_key(count_ge, shape, k):
    kf = float(k)
    t0 = jnp.where(count_ge(jnp.zeros(shape, I32)) >= kf, 0, INT_MIN).astype(I32)

    def body(it, t):
        cand = t + jnp.left_shift(jnp.int32(1), 30 - it)
        return jnp.where(count_ge(cand) >= kf, cand, t)

    return lax.fori_loop(0, 31, body, t0)


def _tri_strict_lower(n, dtype):
    r = lax.broadcasted_iota(I32, (n, n), 0)
    c = lax.broadcasted_iota(I32, (n, n), 1)
    return jnp.where(c < r, 1.0, 0.0).astype(dtype)


def _tri_strict_upper(n, dtype):
    r = lax.broadcasted_iota(I32, (n, n), 0)
    c = lax.broadcasted_iota(I32, (n, n), 1)
    return jnp.where(r < c, 1.0, 0.0).astype(dtype)


def _mod_kernel(c_ref, w_ref, b_ref, o_ref):
    c = c_ref[...]
    cond = (c * _sigmoid(c)).astype(BF16)
    o_ref[...] = _dot(cond, w_ref[...].astype(BF16)) + b_ref[...]


def _mod(c, w_ada, b_ada):
    bsz, d = c.shape
    n = w_ada.shape[1]
    tn = 1024
    return pl.pallas_call(
        _mod_kernel,
        out_shape=jax.ShapeDtypeStruct((bsz, n), F32),
        grid=(n // tn,),
        in_specs=[pl.BlockSpec((bsz, d), lambda j: (0, 0)),
                  pl.BlockSpec((d, tn), lambda j: (0, j)),
                  pl.BlockSpec((1, tn), lambda j: (0, j))],
        out_specs=pl.BlockSpec((bsz, tn), lambda j: (0, j)),
        compiler_params=_cparams(("parallel",)),
        name="mod",
    )(c, w_ada, b_ada.reshape(1, n))


def _rope(z, c_tab, s_tab, period, half):
    w = z.shape[1]
    reps = w // LANES
    c = jnp.concatenate([c_tab] * reps, axis=1) if reps > 1 else c_tab
    s = jnp.concatenate([s_tab] * reps, axis=1) if reps > 1 else s_tab
    lane = lax.broadcasted_iota(I32, z.shape, 1)
    first = (lane & (period - 1)) < half
    partner = jnp.where(first, pltpu.roll(z, w - half, axis=1), pltpu.roll(z, half, axis=1))
    return z * c + partner * s


def _in_proj_kernel(x_ref, mod_ref, w_ref, wsm_ref, c64_ref, s64_ref, c32_ref, s32_ref,
                    qa_ref, ka_ref, vaT_ref, qi_ref, ki_ref, wT_ref, qbraw_ref, qbrot_ref,
                    kcmp_ref, vcmp_ref, ksel_ref, vselT_ref, kwin_ref, vwinT_ref, gT_ref, ga_ref, gb_ref):
    mod = mod_ref[0]
    u = (x_ref[...] * (1.0 + mod[1:2, :]) + mod[0:1, :]).astype(BF16)
    tm = u.shape[0]
    c64, s64, c32, s32 = c64_ref[...], s64_ref[...], c32_ref[...], s32_ref[...]
    scale = HEAD_DIM ** -0.5
    lane = lax.broadcasted_iota(I32, (tm, LANES), 1)
    low = lane < HEAD_DIM

    def proj(a, b):
        return _dot(u, w_ref[:, a:b])

    rope64 = lambda z: _rope(z, c64, s64, HEAD_DIM, HEAD_DIM // ROPE_FRACTION // 2)
    rope32 = lambda z: _rope(z, c32, s32, IDX_DIM, IDX_DIM // ROPE_FRACTION // 2)

    def head_slots64(z):
        out = []
        for h in range(A_HEADS):
            pair = z[:, (h // 2) * LANES:(h // 2 + 1) * LANES]
            g = h // REP
            src = pair if h % 2 == g else pltpu.roll(pair, HEAD_DIM, axis=1)
            out.append(jnp.where(low, src, 0.0) if g == 0 else jnp.where(low, 0.0, src))
        return jnp.concatenate(out, axis=1).astype(BF16)

    def head_slots32(z):
        per = LANES // IDX_DIM
        out = []
        for h in range(IDX_HEADS):
            col = z[:, (h // per) * LANES:(h // per + 1) * LANES]
            shift = IDX_DIM * (h % per)
            src = col if shift == 0 else pltpu.roll(col, LANES - shift, axis=1)
            out.append(jnp.where(lane < IDX_DIM, src, 0.0))
        return jnp.concatenate(out, axis=1).astype(BF16)

    def store_vt(ref, z, chunk):
        zt = z.T
        ones = jnp.ones((HEAD_DIM, chunk), F32)
        for g in range(A_KV_HEADS):
            for j in range(tm // chunk):
                blk = zt[g * HEAD_DIM:(g + 1) * HEAD_DIM, j * chunk:(j + 1) * chunk]
                ref[g, j] = jnp.concatenate([blk, ones], axis=0).astype(BF16)

    qa_ref[...] = head_slots64(rope64(proj(0, 512)) * scale)
    ka_ref[...] = rope64(proj(512, 640)).astype(BF16)
    store_vt(vaT_ref, proj(640, 768), tm)
    qi_ref[...] = head_slots32(rope32(proj(768, 1024)))
    ki_ref[...] = rope32(proj(1024, 1152)).astype(BF16)
    qb = proj(1152, 1664)
    qbraw_ref[...] = head_slots64(qb * scale)
    qbrot_ref[...] = head_slots64(rope64(qb) * scale)
    kcmp_ref[...] = proj(1664, 1792)
    vcmp_ref[...] = proj(1792, 1920)
    ksel_ref[...] = rope64(proj(1920, 2048)).astype(BF16)
    store_vt(vselT_ref, proj(2048, 2176), tm)
    kwin_ref[...] = rope64(proj(2176, 2304)).astype(BF16)
    store_vt(vwinT_ref, proj(2304, 2432), LANES)
    ga_ref[...] = _sigmoid(proj(2432, 3456)).astype(BF16)
    gb_ref[...] = _sigmoid(proj(3456, 4480)).astype(BF16)
    small = _dot_nt(wsm_ref[...], u)
    wT_ref[...] = small[0:IDX_HEADS, :]
    gT_ref[...] = _sigmoid(small[IDX_HEADS:IDX_HEADS + N_GATES, :])


def _pack_w_in(w_in):
    d = w_in.shape[0]
    col = lambda i: w_in[:, _IN_OFFS[i]:_IN_OFFS[i + 1]]
    ki = jnp.concatenate([col(4), jnp.zeros((d, LANES - IDX_DIM), w_in.dtype)], axis=1)
    parts = [col(0), col(1), col(2), col(3), ki, col(6), col(7), col(8), col(9), col(10),
             col(11), col(12), col(14), col(15)]
    w_small = jnp.concatenate([col(5), col(13)], axis=1).T
    return jnp.concatenate(parts, axis=1).astype(BF16), w_small.astype(BF16)


def _rope_tables(positions):
    pos = positions.astype(F32).reshape(-1, 1)

    def tab(dim):
        rot = dim // ROPE_FRACTION
        half = rot // 2
        inv = ROPE_THETA ** (-(jnp.arange(half, dtype=F32) * 2.0) / rot)
        ang = pos * inv
        cos, sin = jnp.cos(ang), jnp.sin(ang)
        ones = jnp.ones((pos.shape[0], dim - rot), F32)
        c = jnp.concatenate([cos, cos, ones], axis=1)
        s = jnp.concatenate([-sin, sin, 0.0 * ones], axis=1)
        return jnp.tile(c, (1, LANES // dim)), jnp.tile(s, (1, LANES // dim))

    return tab(HEAD_DIM) + tab(IDX_DIM)


def _in_proj(x2, mod3, w_pack, w_small, tabs, seq, tm):
    t, d = x2.shape
    n = w_pack.shape[1]
    per_b = seq // tm
    g = A_KV_HEADS
    row = lambda w: pl.BlockSpec((tm, w), lambda i: (i, 0))
    tok = lambda r: pl.BlockSpec((r, tm), lambda i: (0, i))
    vt_chunk = pl.BlockSpec((g, 1, LANES, tm), lambda i: (0, i, 0, 0))
    vt_lane = pl.BlockSpec((g, tm // LANES, LANES, LANES), lambda i: (0, i, 0, 0))
    sds = jax.ShapeDtypeStruct
    vt_chunk_shape = sds((g, t // tm, LANES, tm), BF16)
    outs = (("qa", sds((t, 1024), BF16), row(1024)), ("ka", sds((t, LANES), BF16), row(LANES)),
            ("vaT", vt_chunk_shape, vt_chunk), ("qi", sds((t, 1024), BF16), row(1024)),
            ("ki", sds((t, LANES), BF16), row(LANES)), ("wT", sds((IDX_HEADS, t), F32), tok(IDX_HEADS)),
            ("qbraw", sds((t, 1024), BF16), row(1024)), ("qbrot", sds((t, 1024), BF16), row(1024)),
            ("kcmp", sds((t, LANES), F32), row(LANES)), ("vcmp", sds((t, LANES), F32), row(LANES)),
            ("ksel", sds((t, LANES), BF16), row(LANES)), ("vselT", vt_chunk_shape, vt_chunk),
            ("kwin", sds((t, LANES), BF16), row(LANES)),
            ("vwinT", sds((g, t // LANES, LANES, LANES), BF16), vt_lane),
            ("gT", sds((N_GATES, t), F32), tok(N_GATES)),
            ("ga", sds((t, d), BF16), row(d)), ("gb", sds((t, d), BF16), row(d)))
    res = pl.pallas_call(
        _in_proj_kernel,
        out_shape=tuple(o[1] for o in outs),
        grid=(t // tm,),
        in_specs=[row(d),
                  pl.BlockSpec((1, 6, d), lambda i: (i // per_b, 0, 0)),
                  pl.BlockSpec((d, n), lambda i: (0, 0)),
                  pl.BlockSpec(w_small.shape, lambda i: (0, 0)),
                  row(LANES), row(LANES), row(LANES), row(LANES)],
        out_specs=tuple(o[2] for o in outs),
        compiler_params=_cparams(("parallel",)),
        name="in_proj",
    )(x2, mod3, w_pack, w_small, *tabs)
    return {o[0]: r for o, r in zip(outs, res)}


def _fold_rows(x, op):
    n = x.shape[0]
    while n % (2 * SUBLANES) == 0:
        n //= 2
        x = op(x[:n], x[n:])
    slabs = [x[i * SUBLANES:(i + 1) * SUBLANES] for i in range(n // SUBLANES)]
    while len(slabs) > 1:
        nxt = [op(slabs[i], slabs[i + 1]) for i in range(0, len(slabs) - 1, 2)]
        slabs = nxt + ([slabs[-1]] if len(slabs) % 2 else [])
    return slabs[0]


def _col_max(x):
    return jnp.max(_fold_rows(x, jnp.maximum), axis=0, keepdims=True)


def _col_sum(x):
    return jnp.sum(_fold_rows(x, jnp.add), axis=0, keepdims=True)


def _count_rows(mask01):
    return _fold_rows(mask01, jnp.add)


def _stack_heads(q_ref, heads):
    return jnp.concatenate([q_ref[0, :, h * LANES:(h + 1) * LANES] for h in heads], axis=0)


def _flash_step(k, q_stack, v_t, bias4, m, acc):
    s = _dot_nt(k, q_stack) + bias4
    m_new = jnp.maximum(m, _col_max(s))
    e = jnp.exp(s - m_new).astype(BF16)
    return m_new, acc * jnp.exp(m - m_new) + _dot(v_t, e)


HEADS_PER_CHAIN = REP


def _head_chains():
    return [(h // REP, tuple(range(h, h + HEADS_PER_CHAIN))) for h in range(0, A_HEADS, HEADS_PER_CHAIN)]


def _flash_loop(q_ref, k_ref, vT_ref, bias_of, n_chunks, tq, kc):
    chains = _head_chains()
    q_stacks = [_stack_heads(q_ref, heads) for _, heads in chains]
    width = HEADS_PER_CHAIN * tq

    def body(c, carry):
        k = k_ref[0, pl.ds(pl.multiple_of(c * kc, kc), kc), :]
        out = []
        for (g, _), q_stack, (m, acc) in zip(chains, q_stacks, carry):
            bias = jnp.concatenate([bias_of(g, c)] * HEADS_PER_CHAIN, axis=1)
            out.append(_flash_step(k, q_stack, vT_ref[g, c], bias, m, acc))
        return tuple(out)

    init = tuple((jnp.full((1, width), NEG, F32), jnp.zeros((LANES, width), F32)) for _ in chains)
    res = lax.fori_loop(0, n_chunks, body, init)
    return [(heads, acc) for (_, heads), (_, acc) in zip(chains, res)]


def _normalise(acc):
    return acc / jnp.maximum(acc[HEAD_DIM:HEAD_DIM + 1, :], 1e-30)


def _store_heads(o_ref, o_t, heads, tq):
    o = o_t.T
    for r, h in enumerate(heads):
        o_ref[0, :, h * HEAD_DIM:(h + 1) * HEAD_DIM] = o[r * tq:(r + 1) * tq, 0:HEAD_DIM].astype(o_ref.dtype)


def _select_mask(keys, thr, need, offset, tri):
    gt = jnp.where(keys > thr, 1.0, 0.0)
    eq = jnp.where(keys == thr, 1.0, 0.0)
    prefix = _dot(tri, eq.astype(BF16)) + offset
    return gt + jnp.where(prefix < need, eq, 0.0), offset + _col_sum(eq)


def _dsa_kernel(qi_ref, wT_ref, ki_ref, qa_ref, ka_ref, vT_ref, o_ref, keys_ref, bias_ref,
                *, tq, kc, n_keep):
    t0 = pl.program_id(1) * tq
    n_chunks = (t0 + tq + kc - 1) // kc
    wT = wT_ref[...] * (IDX_HEADS ** -0.5 * IDX_DIM ** -0.5)
    qi_stack = _stack_heads(qi_ref, range(IDX_HEADS))
    keypos = lax.broadcasted_iota(I32, (kc, tq), 0)
    qpos = t0 + lax.broadcasted_iota(I32, (kc, tq), 1)

    def score_chunk(c, carry):
        k0 = pl.multiple_of(c * kc, kc)
        sc = _dot_nt(ki_ref[0, pl.ds(k0, kc), :], qi_stack)
        score = jnp.zeros((kc, tq), F32)
        for h in range(IDX_HEADS):
            score = score + wT[h:h + 1, :] * jnp.maximum(sc[:, h * tq:(h + 1) * tq], 0.0)
        keys_ref[c] = jnp.where(k0 + keypos <= qpos, _sort_key(score), INT_MIN)
        return carry

    lax.fori_loop(0, n_chunks, score_chunk, 0)

    def count_ge(cand):
        def body(c, acc):
            return acc + _count_rows(jnp.where(keys_ref[c] >= cand, 1.0, 0.0))
        acc = lax.fori_loop(0, n_chunks, body, jnp.zeros((SUBLANES, tq), F32))
        return jnp.sum(acc, axis=0, keepdims=True)

    thr = _kth_largest_key(count_ge, (1, tq), n_keep)
    need = float(n_keep) - count_ge(thr + 1)
    tri = _tri_strict_lower(LANES, BF16)
    sub = lax.broadcasted_iota(I32, (LANES, tq), 0)
    qsub = t0 + lax.broadcasted_iota(I32, (LANES, tq), 1)

    def bias_chunk(c, offset):
        for j in range(kc // LANES):
            rows = slice(j * LANES, (j + 1) * LANES)
            sel, offset = _select_mask(keys_ref[c, rows, :], thr, need, offset, tri)
            causal = c * kc + j * LANES + sub <= qsub
            bias_ref[c, rows, :] = jnp.where(causal, (sel - 1.0) * 1e30, NEG)
        return offset

    lax.fori_loop(0, n_chunks, bias_chunk, jnp.zeros((1, tq), F32))

    accs = _flash_loop(qa_ref, ka_ref, vT_ref, lambda g, c: bias_ref[c], n_chunks, tq, kc)
    for heads, acc in accs:
        _store_heads(o_ref, _normalise(acc), heads, tq)


def _dsa(qi, wT, ki, qa, ka, vaT, bsz, seq, tq, kc):
    n_keep = min(DSA_TOPK_MAX, seq // 4)
    nq = seq // tq
    qblk = lambda w: pl.BlockSpec((1, tq, w), lambda b, i: (b, i, 0))
    full = lambda w: pl.BlockSpec((1, seq, w), lambda b, i: (b, 0, 0))
    return pl.pallas_call(
        functools.partial(_dsa_kernel, tq=tq, kc=kc, n_keep=n_keep),
        out_shape=jax.ShapeDtypeStruct((bsz, seq, A_HEADS * HEAD_DIM), BF16),
        grid=(bsz, nq),
        in_specs=[qblk(1024), pl.BlockSpec((IDX_HEADS, tq), lambda b, i: (0, b * nq + i)), full(LANES),
                  qblk(1024), full(LANES),
                  pl.BlockSpec((A_KV_HEADS, seq // kc, LANES, kc), lambda b, i: (0, b, 0, 0))],
        out_specs=qblk(512),
        scratch_shapes=[pltpu.VMEM((seq // kc, kc, tq), I32), pltpu.VMEM((seq // kc, kc, tq), F32)],
        compiler_params=_cparams(("parallel", "parallel")),
        name="dsa",
    )(qi, wT, ki, qa, ka, vaT)


def _gelu_tanh(x):
    return 0.5 * x * (1.0 + jnp.tanh(math.sqrt(2.0 / math.pi) * (x + 0.044715 * (x * x * x))))


def _compress_kernel(k_ref, v_ref, pk_ref, pv_ref, w1k_ref, w2k_ref, w1v_ref, w2v_ref,
                     kc_ref, vcT_ref, *, n_rows):
    half = CMP_BLOCK // 2

    def one(x_ref, p_ref, w1_ref, w2_ref):
        outs = []
        for g in range(B_KV_HEADS):
            lo = jnp.zeros((n_rows, CMP_HIDDEN), F32)
            hi = jnp.zeros((n_rows, CMP_HIDDEN), F32)
            for l in range(half):
                xl = x_ref[0, :, l * LANES + g * HEAD_DIM:l * LANES + (g + 1) * HEAD_DIM]
                a = (xl + p_ref[l:l + 1, :]).astype(BF16)
                b = (xl + p_ref[half + l:half + l + 1, :]).astype(BF16)
                lo = lo + _dot(a, w1_ref[l * HEAD_DIM:(l + 1) * HEAD_DIM, :].astype(BF16))
                hi = hi + _dot(b, w1_ref[(half + l) * HEAD_DIM:(half + l + 1) * HEAD_DIM, :].astype(BF16))
            hid = lo + pltpu.roll(hi, n_rows - 1, axis=0)
            outs.append(_dot(_gelu_tanh(hid).astype(BF16), w2_ref[...].astype(BF16)))
        return outs

    k0, k1 = one(k_ref, pk_ref, w1k_ref, w2k_ref)
    kc_ref[0] = jnp.concatenate([k0, k1], axis=1).astype(kc_ref.dtype)
    for g, v in enumerate(one(v_ref, pv_ref, w1v_ref, w2v_ref)):
        vcT_ref[0, g] = jnp.concatenate([v, jnp.zeros_like(v)], axis=1).T.astype(vcT_ref.dtype)


def _compress(kcmp, vcmp, pos_k, pos_v, w1k, w2k, w1v, w2v):
    bsz, n_rows, width = kcmp.shape
    xspec = pl.BlockSpec((1, n_rows, width), lambda b: (b, 0, 0))
    cst = lambda a: pl.BlockSpec(a.shape, lambda b: (0,) * a.ndim)
    return pl.pallas_call(
        functools.partial(_compress_kernel, n_rows=n_rows),
        out_shape=(jax.ShapeDtypeStruct((bsz, n_rows, LANES), BF16),
                   jax.ShapeDtypeStruct((bsz, B_KV_HEADS, LANES, n_rows), BF16)),
        grid=(bsz,),
        in_specs=[xspec, xspec, cst(pos_k), cst(pos_v), cst(w1k), cst(w2k), cst(w1v), cst(w2v)],
        out_specs=(pl.BlockSpec((1, n_rows, LANES), lambda b: (b, 0, 0)),
                   pl.BlockSpec((1, B_KV_HEADS, LANES, n_rows), lambda b: (b, 0, 0, 0))),
        compiler_params=_cparams(("parallel",)),
        name="compress",
    )(kcmp, vcmp, pos_k, pos_v, w1k, w2k, w1v, w2v)


def _split3(x):
    a = x.astype(BF16)
    r = x - a.astype(F32)
    b = r.astype(BF16)
    c = (r - b.astype(F32)).astype(BF16)
    return a, b, c


def _nsa_kernel(qraw_ref, qrot_ref, kc_ref, vcT_ref, ksel_ref, vselT_ref, kwin_ref, vwinT_ref,
                gT_ref, exp_ref, o_ref, bias_ref, *, tq, kc, seq, n_c):
    t0 = pl.program_id(1) * tq
    n_chunks = (t0 + tq + kc - 1) // kc
    n_s = seq // SEL_BLOCK
    n_pick = min(SEL_COUNT, n_s)
    span = WINDOW + tq
    gT = gT_ref[...]

    ridx = lax.broadcasted_iota(I32, (LANES, tq), 0)
    tl = t0 + lax.broadcasted_iota(I32, (LANES, tq), 1)
    valid_c = jnp.where((ridx * CMP_STRIDE + CMP_BLOCK - 1 <= tl) & (ridx < n_c), 1.0, 0.0)
    valid4 = jnp.concatenate([valid_c] * REP, axis=1) > 0.0
    js = lax.broadcasted_iota(I32, (LANES, LANES), 0) * SEL_BLOCK
    cs = lax.broadcasted_iota(I32, (LANES, LANES), 1) * CMP_STRIDE
    overlap_t = jnp.where((cs <= js + SEL_BLOCK - 1) & (cs + CMP_BLOCK - 1 >= js), 1.0, 0.0).astype(BF16)
    cur = tl // SEL_BLOCK
    forced = (ridx == 0) | ((cur - ridx >= 0) & (cur - ridx < SEL_LOCAL))
    blk_causal = ridx * SEL_BLOCK <= tl
    tri = _tri_strict_lower(LANES, BF16)
    keypos = lax.broadcasted_iota(I32, (kc, tq), 0)
    qpos = t0 + lax.broadcasted_iota(I32, (kc, tq), 1)
    w0 = pl.multiple_of(jnp.clip(t0 - WINDOW, 0, seq - span), tq)
    wdiff = (t0 + lax.broadcasted_iota(I32, (span, tq), 1)) - (w0 + lax.broadcasted_iota(I32, (span, tq), 0))
    wbias = jnp.where((wdiff >= 0) & (wdiff < WINDOW), 0.0, NEG)
    wbias_c = jnp.concatenate([wbias] * HEADS_PER_CHAIN, axis=1)
    wblk = w0 // LANES

    o_cmp_g = []
    for g in range(B_KV_HEADS):
        heads = range(g * REP, (g + 1) * REP)
        s = jnp.where(valid4, _dot_nt(kc_ref[0], _stack_heads(qraw_ref, heads)), NEG)
        e = jnp.where(valid4, jnp.exp(s - _col_max(s)), 0.0)
        p = e / jnp.maximum(_col_sum(e), 1e-30)
        o_cmp_g.append(_dot(vcT_ref[0, g], p.astype(BF16)))
        p_sum = p[:, 0:tq]
        for r in range(1, REP):
            p_sum = p_sum + p[:, r * tq:(r + 1) * tq]
        pa, pb, pc = _split3(p_sum)
        imp = _dot(overlap_t, pa) + _dot(overlap_t, pb) + _dot(overlap_t, pc)
        imp = jnp.where(forced, FORCE, jnp.where(blk_causal, imp, NEG))
        keys = jnp.where(ridx < n_s, _sort_key(imp), INT_MIN)
        count_ge = lambda cand, keys=keys: _col_sum(jnp.where(keys >= cand, 1.0, 0.0))
        thr = _kth_largest_key(count_ge, (1, tq), n_pick)
        need = float(n_pick) - _col_sum(jnp.where(keys > thr, 1.0, 0.0))
        blk_sel, _ = _select_mask(keys, thr, need, jnp.zeros((1, tq), F32), tri)
        blk_sel = blk_sel.astype(BF16)

        def bias_chunk(c, carry, blk_sel=blk_sel, g=g):
            tok_sel = _dot(exp_ref[c], blk_sel)
            bias_ref[g, c] = jnp.where(c * kc + keypos <= qpos, (tok_sel - 1.0) * 1e30, NEG)
            return carry

        lax.fori_loop(0, n_chunks, bias_chunk, 0)

    sel = _flash_loop(qrot_ref, ksel_ref, vselT_ref, lambda g, c: bias_ref[g, c], n_chunks, tq, kc)

    kwin = kwin_ref[0, pl.ds(w0, span), :]
    for (g, heads), (_, acc_sel) in zip(_head_chains(), sel):
        sw = _dot_nt(kwin, _stack_heads(qrot_ref, heads)) + wbias_c
        ew = jnp.exp(sw - _col_max(sw)).astype(BF16)
        acc_win = _dot(vwinT_ref[g, wblk], ew[0:LANES, :])
        for j in range(1, span // LANES):
            acc_win = acc_win + _dot(vwinT_ref[g, wblk + j], ew[j * LANES:(j + 1) * LANES, :])

        o_sel = _normalise(acc_sel)
        o_win = _normalise(acc_win)
        cols = []
        for r, h in enumerate(heads):
            col = slice(r * tq, (r + 1) * tq)
            cmp_col = slice((h % REP) * tq, (h % REP + 1) * tq)
            cols.append(gT[3 * h:3 * h + 1, :] * o_cmp_g[g][:, cmp_col]
                        + gT[3 * h + 1:3 * h + 2, :] * o_sel[:, col]
                        + gT[3 * h + 2:3 * h + 3, :] * o_win[:, col])
        _store_heads(o_ref, jnp.concatenate(cols, axis=1), heads, tq)


def _nsa(qraw, qrot, kc_x, vcT, ksel, vselT, kwin, vwinT, gT, bsz, seq, tq, kc):
    n_c = (seq - CMP_BLOCK) // CMP_STRIDE + 1
    nq = seq // tq
    key_blk = (jnp.arange(seq, dtype=I32) // SEL_BLOCK).reshape(seq // kc, kc, 1)
    expand = (key_blk == jnp.arange(LANES, dtype=I32).reshape(1, 1, LANES)).astype(BF16)
    qblk = lambda w: pl.BlockSpec((1, tq, w), lambda b, i: (b, i, 0))
    full = lambda w: pl.BlockSpec((1, seq, w), lambda b, i: (b, 0, 0))
    g = B_KV_HEADS
    return pl.pallas_call(
        functools.partial(_nsa_kernel, tq=tq, kc=kc, seq=seq, n_c=n_c),
        out_shape=jax.ShapeDtypeStruct((bsz, seq, B_HEADS * HEAD_DIM), BF16),
        grid=(bsz, nq),
        in_specs=[qblk(1024), qblk(1024),
                  pl.BlockSpec((1,) + kc_x.shape[1:], lambda b, i: (b, 0, 0)),
                  pl.BlockSpec((1,) + vcT.shape[1:], lambda b, i: (b, 0, 0, 0)),
                  full(LANES), pl.BlockSpec((g, seq // kc, LANES, kc), lambda b, i: (0, b, 0, 0)),
                  full(LANES), pl.BlockSpec((g, seq // LANES, LANES, LANES), lambda b, i: (0, b, 0, 0)),
                  pl.BlockSpec((N_GATES, tq), lambda b, i: (0, b * nq + i)),
                  pl.BlockSpec(expand.shape, lambda b, i: (0, 0, 0))],
        out_specs=qblk(512),
        scratch_shapes=[pltpu.VMEM((g, seq // kc, kc, tq), F32)],
        compiler_params=_cparams(("parallel", "parallel")),
        name="nsa",
    )(qraw, qrot, kc_x, vcT, ksel, vselT, kwin, vwinT, gT, expand)


def _pack_pairs(x):
    n = x.shape[1] // 2
    lo = pltpu.bitcast(x[:, :n].astype(BF16).astype(F32), I32)
    hi = pltpu.bitcast(x[:, n:].astype(BF16).astype(F32), I32)
    return lax.shift_right_logical(lo, 16) | (hi & jnp.int32(-65536))


def _unpack_pairs(p):
    lo = pltpu.bitcast(lax.shift_left(p, 16), F32)
    hi = pltpu.bitcast(p & jnp.int32(-65536), F32)
    return jnp.concatenate([lo, hi], axis=1)


def _layer_norm(y, g, b):
    mu = jnp.mean(y, axis=1, keepdims=True)
    yc = y - mu
    var = jnp.mean(yc * yc, axis=1, keepdims=True)
    return yc * lax.rsqrt(var + LN_EPS) * g + b


def _out_proj_kernel(oa_ref, ob_ref, ga_ref, gb_ref, x_ref, mod_ref, wa_ref, wb_ref, wo_ref,
                     g1_ref, b1_ref, wrh_ref, wrl_ref, x1_ref, u2_ref, lg_ref):
    mod = mod_ref[0]
    merged = (ga_ref[...].astype(F32) * _dot(oa_ref[...], wa_ref[...])
              + gb_ref[...].astype(F32) * _dot(ob_ref[...], wb_ref[...]))
    mix = _dot(merged.astype(BF16), wo_ref[...])
    x1 = _layer_norm(DN_ALPHA * x_ref[...] + mod[2:3, :] * mix, g1_ref[...], b1_ref[...])
    x1_ref[...] = x1
    u2 = x1 * (1.0 + mod[4:5, :]) + mod[3:4, :]
    packed = _pack_pairs(u2)
    for j in range(PIECES):
        u2_ref[j] = packed[:, j * SC_ROW:(j + 1) * SC_ROW]
    uh = u2.astype(BF16)
    ul = (u2 - uh.astype(F32)).astype(BF16)
    lg_ref[...] = _dot_nt(wrh_ref[...], uh) + _dot_nt(wrh_ref[...], ul) + _dot_nt(wrl_ref[...], uh)


def _out_proj(oa, ob, ga, gb, x2, mod3, wa, wb, wo, g1, b1, wrh, wrl, seq, tm):
    t, d = x2.shape
    per_b = seq // tm
    row = lambda w: pl.BlockSpec((tm, w), lambda i: (i, 0))
    cst = lambda a: pl.BlockSpec(a.shape, lambda i: (0,) * a.ndim)
    return pl.pallas_call(
        _out_proj_kernel,
        out_shape=(jax.ShapeDtypeStruct((t, d), F32), jax.ShapeDtypeStruct((PIECES, t, SC_ROW), I32),
                   jax.ShapeDtypeStruct((N_EXPERTS, t), F32)),
        grid=(t // tm,),
        in_specs=[row(512), row(512), row(d), row(d), row(d),
                  pl.BlockSpec((1, 6, d), lambda i: (i // per_b, 0, 0)),
                  cst(wa), cst(wb), cst(wo), cst(g1), cst(b1), cst(wrh), cst(wrl)],
        out_specs=(row(d), pl.BlockSpec((PIECES, tm, SC_ROW), lambda i: (0, i, 0)),
                   pl.BlockSpec((N_EXPERTS, tm), lambda i: (0, i))),
        compiler_params=_cparams(("parallel",)),
        name="out_proj",
    )(oa, ob, ga, gb, x2, mod3, wa, wb, wo, g1, b1, wrh, wrl)


def _first_max(x, rows):
    m = jnp.max(x, axis=0, keepdims=True)
    idx = jnp.min(jnp.where(x == m, rows, 1e9), axis=0, keepdims=True)
    return m, idx


def _router_kernel(lg_ref, rb_ref, idx_ref, w_ref, pos_ref, cnt_ref, carry_ref, *, tm):
    @pl.when(pl.program_id(0) == 0)
    def _():
        carry_ref[...] = jnp.zeros_like(carry_ref)

    per_g = N_EXPERTS // N_GROUPS
    scores = _sigmoid(lg_ref[...])
    choice = scores + rb_ref[...][:, 0:1]
    rows = lax.broadcasted_iota(I32, (N_EXPERTS, tm), 0).astype(F32)
    rows_g = lax.broadcasted_iota(I32, (per_g, tm), 0).astype(F32)
    ninf = -jnp.inf

    gs = []
    for g in range(N_GROUPS):
        x = choice[g * per_g:(g + 1) * per_g, :]
        m1, i1 = _first_max(x, rows_g)
        m2 = jnp.max(jnp.where(rows_g == i1, ninf, x), axis=0, keepdims=True)
        gs.append(m1 + m2)
    gscore = jnp.concatenate(gs, axis=0)
    rows8 = lax.broadcasted_iota(I32, (N_GROUPS, tm), 0).astype(F32)
    keep = jnp.zeros((N_GROUPS, tm), F32)
    for _ in range(TOPK_GROUPS):
        _, gi = _first_max(gscore, rows8)
        hit = rows8 == gi
        keep = jnp.where(hit, 1.0, keep)
        gscore = jnp.where(hit, ninf, gscore)
    keep_full = jnp.concatenate(
        [jnp.broadcast_to(keep[g:g + 1, :], (per_g, tm)) for g in range(N_GROUPS)], axis=0)
    masked = jnp.where(keep_full > 0.0, choice, NEG)

    idxs, ws = [], []
    onehot = jnp.zeros((N_EXPERTS, tm), F32)
    for _ in range(TOP_K):
        _, ei = _first_max(masked, rows)
        hit = rows == ei
        idxs.append(ei)
        ws.append(jnp.sum(jnp.where(hit, scores, 0.0), axis=0, keepdims=True))
        masked = jnp.where(hit, ninf, masked)
        onehot = jnp.where(hit, 1.0, onehot)
    idx = jnp.concatenate(idxs, axis=0)
    w = jnp.concatenate(ws, axis=0)
    idx_ref[...] = idx.astype(I32)
    w_ref[...] = w / jnp.sum(w, axis=0, keepdims=True) * ROUTED_SCALE

    tri = _tri_strict_upper(tm, BF16)
    base = _dot(onehot.astype(BF16), tri) + carry_ref[...][:, 0:1]
    pos = [jnp.sum(jnp.where(rows == idxs[k], base, 0.0), axis=0, keepdims=True) for k in range(TOP_K)]
    pos_ref[...] = jnp.concatenate(pos, axis=0).astype(I32)
    carry = carry_ref[...] + jnp.sum(onehot, axis=1, keepdims=True)
    carry_ref[...] = carry
    cnt_ref[...] = carry


def _router(lg, router_bias, tm):
    e, t = lg.shape
    rb = jnp.broadcast_to(router_bias.reshape(e, 1).astype(F32), (e, LANES))
    tok = lambda r: pl.BlockSpec((r, tm), lambda i: (0, i))
    return pl.pallas_call(
        functools.partial(_router_kernel, tm=tm),
        out_shape=(jax.ShapeDtypeStruct((TOP_K, t), I32), jax.ShapeDtypeStruct((TOP_K, t), F32),
                   jax.ShapeDtypeStruct((TOP_K, t), I32), jax.ShapeDtypeStruct((e, LANES), F32)),
        grid=(t // tm,),
        in_specs=[tok(e), pl.BlockSpec((e, LANES), lambda i: (0, 0))],
        out_specs=(tok(TOP_K), tok(TOP_K), tok(TOP_K), pl.BlockSpec((e, LANES), lambda i: (0, 0))),
        scratch_shapes=[pltpu.VMEM((e, LANES), F32)],
        compiler_params=_cparams(("arbitrary",)),
        name="router",
    )(lg, rb)


def _dest_kernel(idx_ref, pos_ref, st_ref, dest_ref, *, tm):
    rows = lax.broadcasted_iota(I32, (N_EXPERTS, tm), 0)
    starts = st_ref[...][:, 0:1]
    idx = idx_ref[...]
    out = []
    for k in range(TOP_K):
        out.append(jnp.sum(jnp.where(rows == idx[k:k + 1, :], starts, 0.0), axis=0, keepdims=True))
    dest_ref[...] = jnp.concatenate(out, axis=0).astype(I32) + pos_ref[...]


def _dest(idx, pos, starts, tm):
    k, t = idx.shape
    st = jnp.broadcast_to(starts.reshape(N_EXPERTS, 1).astype(F32), (N_EXPERTS, LANES))
    tok = pl.BlockSpec((k, tm), lambda i: (0, i))
    return pl.pallas_call(
        functools.partial(_dest_kernel, tm=tm),
        out_shape=jax.ShapeDtypeStruct((k, t), I32),
        grid=(t // tm,),
        in_specs=[tok, tok, pl.BlockSpec((N_EXPERTS, LANES), lambda i: (0, 0))],
        out_specs=tok,
        compiler_params=_cparams(("parallel",)),
        name="dest",
    )(idx, pos, st)


def _experts_kernel(blk_ref, used_ref, first_ref, slot_ref, next_ref, x_ref, wg_hbm, wu_hbm, wd_hbm, y_ref,
                    wg_buf, wu_buf, wd_buf, wg_bf, wu_bf, wd_bf, sems):
    b = pl.program_id(0)
    active = b < used_ref[0]

    def weight_copies(e, slot):
        return [pltpu.make_async_copy(hbm.at[e], buf.at[slot], sems.at[slot, i])
                for i, (hbm, buf) in enumerate(((wg_hbm, wg_buf), (wu_hbm, wu_buf), (wd_hbm, wd_buf)))]

    @pl.when(b == 0)
    def _():
        for cp in weight_copies(blk_ref[0], 0):
            cp.start()

    @pl.when(active & (first_ref[b] == 1))
    def _():
        slot = slot_ref[b]
        for cp in weight_copies(blk_ref[b], slot):
            cp.wait()
        nxt = next_ref[b]

        @pl.when(nxt >= 0)
        def _():
            for cp in weight_copies(nxt, 1 - slot):
                cp.start()

        wg_bf[...] = wg_buf[slot].astype(BF16)
        wu_bf[...] = wu_buf[slot].astype(BF16)
        wd_bf[...] = wd_buf[slot].astype(BF16)

    @pl.when(active)
    def _():
        x = jnp.concatenate([x_ref[j] for j in range(PIECES)], axis=1)
        x = _unpack_pairs(x).astype(BF16)
        a = _dot(x, wg_bf[...])
        u = _dot(x, wu_bf[...])
        h = (a * _sigmoid(a) * u).astype(BF16)
        y = _pack_pairs(_dot(h, wd_bf[...]))
        for j in range(PIECES):
            y_ref[j] = y[:, j * SC_ROW:(j + 1) * SC_ROW]


def _experts(xs, plan, wg, wu, wd, bm):
    _, cap, _ = xs.shape
    n_blocks = cap // bm
    d, f = wg.shape[1], wg.shape[2]
    rows = pl.BlockSpec((PIECES, bm, SC_ROW), lambda b, blk, used, *_: (0, jnp.minimum(b, used[0] - 1), 0))
    hbm = pl.BlockSpec(memory_space=pl.ANY)
    return pl.pallas_call(
        _experts_kernel,
        out_shape=jax.ShapeDtypeStruct(xs.shape, I32),
        grid_spec=pltpu.PrefetchScalarGridSpec(
            num_scalar_prefetch=5,
            grid=(n_blocks,),
            in_specs=[rows, hbm, hbm, hbm],
            out_specs=rows,
            scratch_shapes=[pltpu.VMEM((2, d, f), F32), pltpu.VMEM((2, d, f), F32), pltpu.VMEM((2, f, d), F32),
                            pltpu.VMEM((d, f), BF16), pltpu.VMEM((d, f), BF16), pltpu.VMEM((f, d), BF16),
                            pltpu.SemaphoreType.DMA((2, 3))]),
        compiler_params=_cparams(("arbitrary",)),
        name="experts",
    )(plan["blk_e"], plan["n_used"], plan["first"], plan["slot"], plan["next_e"], xs, wg, wu, wd)


def _final_kernel(x1_ref, u2_ref, yg_ref, w_ref, mod_ref, sg_ref, su_ref, sd_ref, g2_ref, b2_ref, o_ref):
    mod = mod_ref[0]
    w = w_ref[...]
    rows = lambda ref, *lead: jnp.concatenate([ref[(j,) + lead] for j in range(PIECES)], axis=1)
    routed = w[:, 0:1] * _unpack_pairs(rows(yg_ref, 0))
    for k in range(1, TOP_K):
        routed = routed + w[:, k:k + 1] * _unpack_pairs(rows(yg_ref, k))
    u = _unpack_pairs(rows(u2_ref)).astype(BF16)
    a = _dot(u, sg_ref[...])
    b = _dot(u, su_ref[...])
    shared = _dot((a * _sigmoid(a) * b).astype(BF16), sd_ref[...])
    y = DN_ALPHA * x1_ref[...] + mod[5:6, :] * (routed + shared)
    o_ref[...] = _layer_norm(y, g2_ref[...], b2_ref[...])


def _final(x1, u2p, yg, wtok, mod3, sg, su, sd, g2, b2, seq, tm):
    t, d = x1.shape
    per_b = seq // tm
    row = lambda w: pl.BlockSpec((tm, w), lambda i: (i, 0))
    cst = lambda a: pl.BlockSpec(a.shape, lambda i: (0,) * a.ndim)
    return pl.pallas_call(
        _final_kernel,
        out_shape=jax.ShapeDtypeStruct((t, d), F32),
        grid=(t // tm,),
        in_specs=[row(d), pl.BlockSpec((PIECES, tm, SC_ROW), lambda i: (0, i, 0)),
                  pl.BlockSpec((PIECES, TOP_K, tm, SC_ROW), lambda i: (0, 0, i, 0)),
                  row(LANES), pl.BlockSpec((1, 6, d), lambda i: (i // per_b, 0, 0)),
                  cst(sg), cst(su), cst(sd), cst(g2), cst(b2)],
        out_specs=row(d),
        compiler_params=_cparams(("parallel",)),
        name="final",
    )(x1, u2p, yg, wtok, mod3, sg, su, sd, g2, b2)


def _sc_mesh():
    return plsc.VectorSubcoreMesh(core_axis_name="core", subcore_axis_name="subcore")


def _sc_scatter_rows(src, idx, n_out, src_block):
    n_idx = idx.shape[0]

    @functools.partial(pl.kernel, out_type=jax.ShapeDtypeStruct((n_out, SC_ROW), src.dtype),
                       mesh=_sc_mesh(), scratch_types=[])
    def k(x_hbm, i_hbm, o_hbm):
        def body(x_vmem, i_vmem):
            pltpu.sync_copy(x_vmem, o_hbm.at[i_vmem.at[0]])

        pltpu.emit_pipeline(
            body, grid=(n_idx // SC_WINDOW,),
            in_specs=[pl.BlockSpec((SC_WINDOW, SC_ROW), lambda i: (src_block(i), 0)),
                      pl.BlockSpec((1, SC_WINDOW), lambda i: (0, i))],
            out_specs=[],
            core_axis_name=("core", "subcore"),
            dimension_semantics=(pltpu.PARALLEL,),
        )(x_hbm, i_hbm)

    return k(src, idx.reshape(1, n_idx))


def _sc_gather_rows(src, idx):
    n_idx = idx.shape[0]

    @functools.partial(pl.kernel, out_type=jax.ShapeDtypeStruct((n_idx, SC_ROW), src.dtype),
                       mesh=_sc_mesh(), scratch_types=[])
    def k(x_hbm, i_hbm, o_hbm):
        def body(i_vmem, o_vmem):
            pltpu.sync_copy(x_hbm.at[i_vmem.at[0]], o_vmem)

        pltpu.emit_pipeline(
            body, grid=(n_idx // SC_WINDOW,),
            in_specs=[pl.BlockSpec((1, SC_WINDOW), lambda i: (0, i))],
            out_specs=[pl.BlockSpec((SC_WINDOW, SC_ROW), lambda i: (i, 0))],
            core_axis_name=("core", "subcore"),
            dimension_semantics=(pltpu.PARALLEL,),
        )(i_hbm, o_hbm)

    return k(src, idx.reshape(1, n_idx))


def _moe_plan(counts, n_tok):
    bm = BM_EXPERT
    padded = (counts + bm - 1) // bm * bm
    p_ends = jnp.cumsum(padded)
    starts = p_ends - padded
    n_blocks = n_tok * TOP_K // bm + N_EXPERTS
    blk = jnp.arange(n_blocks, dtype=I32)
    blk_e = jnp.minimum(jnp.sum(p_ends[None, :] <= (blk * bm)[:, None], axis=1), N_EXPERTS - 1).astype(I32)
    n_used = (p_ends[-1] // bm).astype(I32)
    prev = jnp.concatenate([jnp.full((1,), -1, I32), blk_e[:-1]])
    first = ((blk_e != prev) & (blk < n_used)).astype(I32)
    slot = (jnp.cumsum(first) - 1) % 2
    first_pos = jnp.where(first == 1, blk, n_blocks)
    next_first = lax.cummin(jnp.concatenate([first_pos[1:], jnp.full((1,), n_blocks, I32)]), reverse=True)
    next_e = jnp.where(next_first < n_blocks, blk_e[jnp.minimum(next_first, n_blocks - 1)], -1)
    plan = dict(blk_e=blk_e, n_used=n_used.reshape(1), first=first, slot=slot.astype(I32),
                next_e=next_e.astype(I32))
    return starts, plan, n_blocks


def _layer(x, mod, positions, w_in, w_br_a, w_br_b, w_out, cmp_pos_k, cmp_pos_v, cmp_k_w1, cmp_k_w2,
           cmp_v_w1, cmp_v_w2, ln1_g, ln1_b, w_router, router_bias, w_exp_gate, w_exp_up, w_exp_down,
           w_sh_gate, w_sh_up, w_sh_down, ln2_g, ln2_b):
    bsz, seq, d = x.shape
    t = bsz * seq
    assert seq // CMP_STRIDE == LANES and seq % TQ_ATTN == 0 and seq % TM_PROJ == 0 and KC_ATTN == TM_PROJ
    x2 = x.reshape(t, d)
    mod3 = mod.reshape(bsz, 6, d)

    w_pack, w_small = _pack_w_in(w_in)
    z = _in_proj(x2, mod3, w_pack, w_small, _rope_tables(positions), seq, TM_PROJ)
    per_b = lambda name: z[name].reshape(bsz, seq, z[name].shape[1])

    o_a = _dsa(per_b("qi"), z["wT"], per_b("ki"), per_b("qa"), per_b("ka"), z["vaT"],
               bsz, seq, TQ_ATTN, KC_ATTN)

    n_rows = seq // CMP_STRIDE
    kc_x, vcT = _compress(z["kcmp"].reshape(bsz, n_rows, CMP_STRIDE * LANES),
                          z["vcmp"].reshape(bsz, n_rows, CMP_STRIDE * LANES),
                          cmp_pos_k, cmp_pos_v, cmp_k_w1, cmp_k_w2, cmp_v_w1, cmp_v_w2)
    o_b = _nsa(per_b("qbraw"), per_b("qbrot"), kc_x, vcT, per_b("ksel"), z["vselT"], per_b("kwin"),
               z["vwinT"], z["gT"], bsz, seq, TQ_ATTN, KC_ATTN)

    wr_hi = w_router.T.astype(BF16)
    wr_lo = (w_router.T - wr_hi.astype(F32)).astype(BF16)
    x1, u2p, logits = _out_proj(
        o_a.reshape(t, -1), o_b.reshape(t, -1), z["ga"], z["gb"], x2, mod3,
        w_br_a.astype(BF16), w_br_b.astype(BF16), w_out.astype(BF16),
        ln1_g.reshape(1, d), ln1_b.reshape(1, d), wr_hi, wr_lo, seq, TM_PROJ)

    idx, wts, pos, counts = _router(logits, router_bias, TM_ROUTE)
    starts, plan, n_blocks = _moe_plan(counts[:, 0].astype(I32), t)
    dest = _dest(idx, pos, starts, TM_ROUTE)

    cap = n_blocks * BM_EXPERT
    dest_p = (dest[None] + (jnp.arange(PIECES, dtype=I32) * cap).reshape(PIECES, 1, 1)).reshape(-1)
    tb = t // SC_WINDOW
    xs = _sc_scatter_rows(u2p.reshape(PIECES * t, SC_ROW), dest_p, cap * PIECES,
                          lambda i: (i // (TOP_K * tb)) * tb + i % tb)
    ys = _experts(xs.reshape(PIECES, cap, SC_ROW), plan, w_exp_gate, w_exp_up, w_exp_down, BM_EXPERT)
    yg = _sc_gather_rows(ys.reshape(cap * PIECES, SC_ROW), dest_p).reshape(PIECES, TOP_K, t, SC_ROW)

    wtok = jnp.pad(wts.T, ((0, 0), (0, LANES - TOP_K)))
    return _final(x1, u2p, yg, wtok, mod3, w_sh_gate.astype(BF16), w_sh_up.astype(BF16),
                  w_sh_down.astype(BF16), ln2_g.reshape(1, d), ln2_b.reshape(1, d), seq, TM_PROJ
                  ).reshape(bsz, seq, d)


def kernel(x, c, positions, w_ada, b_ada, w_in, w_br_a, w_br_b, w_out, cmp_pos_k, cmp_pos_v, cmp_k_w1,
           cmp_k_w2, cmp_v_w1, cmp_v_w2, ln1_g, ln1_b, w_router, router_bias, w_exp_gate, w_exp_up,
           w_exp_down, w_sh_gate, w_sh_up, w_sh_down, ln2_g, ln2_b):
    for l in range(w_ada.shape[0]):
        mod = _mod(c, w_ada[l], b_ada[l])
        x = _layer(x, mod, positions, w_in[l], w_br_a[l], w_br_b[l], w_out[l], cmp_pos_k[l], cmp_pos_v[l],
                   cmp_k_w1[l], cmp_k_w2[l], cmp_v_w1[l], cmp_v_w2[l], ln1_g[l], ln1_b[l], w_router[l],
                   router_bias[l], w_exp_gate[l], w_exp_up[l], w_exp_down[l], w_sh_gate[l], w_sh_up[l],
                   w_sh_down[l], ln2_g[l], ln2_b[l])
    return x
```

```python
import functools
import math

import jax
import jax.numpy as jnp
import numpy as np
from jax import lax
from jax.experimental import pallas as pl
from jax.experimental.pallas import tpu as pltpu
from jax.experimental.pallas import tpu_sc as plsc

F32 = jnp.float32
BF16 = jnp.bfloat16
I32 = jnp.int32

D_MODEL = 1024
HEAD_DIM = 64
ROPE_THETA = 500000.0
ROPE_FRACTION = 4
A_HEADS = 8
A_KV_HEADS = 2
IDX_HEADS = 8
IDX_DIM = 32
DSA_TOPK_MAX = 256
B_HEADS = 8
B_KV_HEADS = 2
REP = 4
CMP_BLOCK = 32
CMP_STRIDE = 16
CMP_HIDDEN = 256
SEL_BLOCK = 64
SEL_COUNT = 16
SEL_LOCAL = 2
WINDOW = 512
N_EXPERTS = 256
TOP_K = 8
N_GROUPS = 8
TOPK_GROUPS = 4
ROUTED_SCALE = 2.5
DEPTH = 1
DN_ALPHA = (2 * DEPTH) ** 0.25
LN_EPS = 1e-5
NEG = -1e30
FORCE = 1e9
INT_MIN = -2147483648
N_GATES = 3 * B_HEADS

LANES = 128
SUBLANES = 8
VMEM_LIMIT = 56 * 1024 * 1024
SC_WINDOW = 128
SC_ROW = 256
PIECES = (D_MODEL // 2) // SC_ROW

TM_PROJ = 512
TQ_ATTN = 256
KC_ATTN = 512
TM_ROUTE = 512
BM_EXPERT = 256

_IN_WIDTHS = (512, 128, 128, 256, 32, 8, 512, 128, 128, 128, 128, 128, 128, 24, 1024, 1024)
_IN_OFFS = np.concatenate([[0], np.cumsum(_IN_WIDTHS)]).tolist()

NT_DIMS = (((1,), (1,)), ((), ()))


def _cparams(sem):
    return pltpu.CompilerParams(dimension_semantics=sem, vmem_limit_bytes=VMEM_LIMIT)


def _sigmoid(x):
    return 1.0 / (1.0 + jnp.exp(-x))


def _dot(a, b):
    return jnp.dot(a, b, preferred_element_type=F32)


def _dot_nt(a, b):
    return lax.dot_general(a, b, NT_DIMS, preferred_element_type=F32)


def _sort_key(x):
    x = jnp.where(x == 0.0, 0.0, x)
    bits = pltpu.bitcast(x, I32)
    return jnp.where(bits < 0, bits ^ 0x7FFFFFFF, bits)


def _kth_largest_key(count_ge, shape, k):
    kf = float(k)
    t0 = jnp.where(count_ge(jnp.zeros(shape, I32)) >= kf, 0, INT_MIN).astype(I32)

    def body(it, t):
        cand = t + jnp.left_shift(jnp.int32(1), 30 - it)
        return jnp.where(count_ge(cand) >= kf, cand, t)

    return lax.fori_loop(0, 31, body, t0)


def _tri_strict_lower(n, dtype):
    r = lax.broadcasted_iota(I32, (n, n), 0)
    c = lax.broadcasted_iota(I32, (n, n), 1)
    return jnp.where(c < r, 1.0, 0.0).astype(dtype)


def _tri_strict_upper(n, dtype):
    r = lax.broadcasted_iota(I32, (n, n), 0)
    c = lax.broadcasted_iota(I32, (n, n), 1)
    return jnp.where(r < c, 1.0, 0.0).astype(dtype)


def _mod_kernel(c_ref, w_ref, b_ref, o_ref):
    c = c_ref[...]
    cond = (c * _sigmoid(c)).astype(BF16)
    o_ref[...] = _dot(cond, w_ref[...].astype(BF16)) + b_ref[...]


def _mod(c, w_ada, b_ada):
    bsz, d = c.shape
    n = w_ada.shape[1]
    tn = 1024
    return pl.pallas_call(
        _mod_kernel,
        out_shape=jax.ShapeDtypeStruct((bsz, n), F32),
        grid=(n // tn,),
        in_specs=[pl.BlockSpec((bsz, d), lambda j: (0, 0)),
                  pl.BlockSpec((d, tn), lambda j: (0, j)),
                  pl.BlockSpec((1, tn), lambda j: (0, j))],
        out_specs=pl.BlockSpec((bsz, tn), lambda j: (0, j)),
        compiler_params=_cparams(("parallel",)),
        name="mod",
    )(c, w_ada, b_ada.reshape(1, n))


def _rope(z, c_tab, s_tab, period, half):
    w = z.shape[1]
    reps = w // LANES
    c = jnp.concatenate([c_tab] * reps, axis=1) if reps > 1 else c_tab
    s = jnp.concatenate([s_tab] * reps, axis=1) if reps > 1 else s_tab
    lane = lax.broadcasted_iota(I32, z.shape, 1)
    first = (lane & (period - 1)) < half
    partner = jnp.where(first, pltpu.roll(z, w - half, axis=1), pltpu.roll(z, half, axis=1))
    return z * c + partner * s


def _in_proj_kernel(x_ref, mod_ref, w_ref, wsm_ref, c64_ref, s64_ref, c32_ref, s32_ref,
                    qa_ref, ka_ref, vaT_ref, qi_ref, ki_ref, wT_ref, qbraw_ref, qbrot_ref,
                    kcmp_ref, vcmp_ref, ksel_ref, vselT_ref, kwin_ref, vwinT_ref, gT_ref, ga_ref, gb_ref):
    mod = mod_ref[0]
    u = (x_ref[...] * (1.0 + mod[1:2, :]) + mod[0:1, :]).astype(BF16)
    tm = u.shape[0]
    c64, s64, c32, s32 = c64_ref[...], s64_ref[...], c32_ref[...], s32_ref[...]
    scale = HEAD_DIM ** -0.5
    lane = lax.broadcasted_iota(I32, (tm, LANES), 1)
    low = lane < HEAD_DIM

    def proj(a, b):
        return _dot(u, w_ref[:, a:b])

    rope64 = lambda z: _rope(z, c64, s64, HEAD_DIM, HEAD_DIM // ROPE_FRACTION // 2)
    rope32 = lambda z: _rope(z, c32, s32, IDX_DIM, IDX_DIM // ROPE_FRACTION // 2)

    def head_slots64(z):
        out = []
        for h in range(A_HEADS):
            pair = z[:, (h // 2) * LANES:(h // 2 + 1) * LANES]
            g = h // REP
            src = pair if h % 2 == g else pltpu.roll(pair, HEAD_DIM, axis=1)
            out.append(jnp.where(low, src, 0.0) if g == 0 else jnp.where(low, 0.0, src))
        return jnp.concatenate(out, axis=1).astype(BF16)

    def head_slots32(z):
        per = LANES // IDX_DIM
        out = []
        for h in range(IDX_HEADS):
            col = z[:, (h // per) * LANES:(h // per + 1) * LANES]
            shift = IDX_DIM * (h % per)
            src = col if shift == 0 else pltpu.roll(col, LANES - shift, axis=1)
            out.append(jnp.where(lane < IDX_DIM, src, 0.0))
        return jnp.concatenate(out, axis=1).astype(BF16)

    def store_vt(ref, z, chunk):
        zt = z.T
        ones = jnp.ones((HEAD_DIM, chunk), F32)
        for g in range(A_KV_HEADS):
            for j in range(tm // chunk):
                blk = zt[g * HEAD_DIM:(g + 1) * HEAD_DIM, j * chunk:(j + 1) * chunk]
                ref[g, j] = jnp.concatenate([blk, ones], axis=0).astype(BF16)

    qa_ref[...] = head_slots64(rope64(proj(0, 512)) * scale)
    ka_ref[...] = rope64(proj(512, 640)).astype(BF16)
    store_vt(vaT_ref, proj(640, 768), tm)
    qi_ref[...] = head_slots32(rope32(proj(768, 1024)))
    ki_ref[...] = rope32(proj(1024, 1152)).astype(BF16)
    qb = proj(1152, 1664)
    qbraw_ref[...] = head_slots64(qb * scale)
    qbrot_ref[...] = head_slots64(rope64(qb) * scale)
    kcmp_ref[...] = proj(1664, 1792)
    vcmp_ref[...] = proj(1792, 1920)
    ksel_ref[...] = rope64(proj(1920, 2048)).astype(BF16)
    store_vt(vselT_ref, proj(2048, 2176), tm)
    kwin_ref[...] = rope64(proj(2176, 2304)).astype(BF16)
    store_vt(vwinT_ref, proj(2304, 2432), LANES)
    ga_ref[...] = _sigmoid(proj(2432, 3456)).astype(BF16)
    gb_ref[...] = _sigmoid(proj(3456, 4480)).astype(BF16)
    small = _dot_nt(wsm_ref[...], u)
    wT_ref[...] = small[0:IDX_HEADS, :]
    gT_ref[...] = _sigmoid(small[IDX_HEADS:IDX_HEADS + N_GATES, :])


def _pack_w_in(w_in):
    d = w_in.shape[0]
    col = lambda i: w_in[:, _IN_OFFS[i]:_IN_OFFS[i + 1]]
    ki = jnp.concatenate([col(4), jnp.zeros((d, LANES - IDX_DIM), w_in.dtype)], axis=1)
    parts = [col(0), col(1), col(2), col(3), ki, col(6), col(7), col(8), col(9), col(10),
             col(11), col(12), col(14), col(15)]
    w_small = jnp.concatenate([col(5), col(13)], axis=1).T
    return jnp.concatenate(parts, axis=1).astype(BF16), w_small.astype(BF16)


def _rope_tables(positions):
    pos = positions.astype(F32).reshape(-1, 1)

    def tab(dim):
        rot = dim // ROPE_FRACTION
        half = rot // 2
        inv = ROPE_THETA ** (-(jnp.arange(half, dtype=F32) * 2.0) / rot)
        ang = pos * inv
        cos, sin = jnp.cos(ang), jnp.sin(ang)
        ones = jnp.ones((pos.shape[0], dim - rot), F32)
        c = jnp.concatenate([cos, cos, ones], axis=1)
        s = jnp.concatenate([-sin, sin, 0.0 * ones], axis=1)
        return jnp.tile(c, (1, LANES // dim)), jnp.tile(s, (1, LANES // dim))

    return tab(HEAD_DIM) + tab(IDX_DIM)


def _in_proj(x2, mod3, w_pack, w_small, tabs, seq, tm):
    t, d = x2.shape
    n = w_pack.shape[1]
    per_b = seq // tm
    g = A_KV_HEADS
    row = lambda w: pl.BlockSpec((tm, w), lambda i: (i, 0))
    tok = lambda r: pl.BlockSpec((r, tm), lambda i: (0, i))
    vt_chunk = pl.BlockSpec((g, 1, LANES, tm), lambda i: (0, i, 0, 0))
    vt_lane = pl.BlockSpec((g, tm // LANES, LANES, LANES), lambda i: (0, i, 0, 0))
    sds = jax.ShapeDtypeStruct
    vt_chunk_shape = sds((g, t // tm, LANES, tm), BF16)
    outs = (("qa", sds((t, 1024), BF16), row(1024)), ("ka", sds((t, LANES), BF16), row(LANES)),
            ("vaT", vt_chunk_shape, vt_chunk), ("qi", sds((t, 1024), BF16), row(1024)),
            ("ki", sds((t, LANES), BF16), row(LANES)), ("wT", sds((IDX_HEADS, t), F32), tok(IDX_HEADS)),
            ("qbraw", sds((t, 1024), BF16), row(1024)), ("qbrot", sds((t, 1024), BF16), row(1024)),
            ("kcmp", sds((t, LANES), F32), row(LANES)), ("vcmp", sds((t, LANES), F32), row(LANES)),
            ("ksel", sds((t, LANES), BF16), row(LANES)), ("vselT", vt_chunk_shape, vt_chunk),
            ("kwin", sds((t, LANES), BF16), row(LANES)),
            ("vwinT", sds((g, t // LANES, LANES, LANES), BF16), vt_lane),
            ("gT", sds((N_GATES, t), F32), tok(N_GATES)),
            ("ga", sds((t, d), BF16), row(d)), ("gb", sds((t, d), BF16), row(d)))
    res = pl.pallas_call(
        _in_proj_kernel,
        out_shape=tuple(o[1] for o in outs),
        grid=(t // tm,),
        in_specs=[row(d),
                  pl.BlockSpec((1, 6, d), lambda i: (i // per_b, 0, 0)),
                  pl.BlockSpec((d, n), lambda i: (0, 0)),
                  pl.BlockSpec(w_small.shape, lambda i: (0, 0)),
                  row(LANES), row(LANES), row(LANES), row(LANES)],
        out_specs=tuple(o[2] for o in outs),
        compiler_params=_cparams(("parallel",)),
        name="in_proj",
    )(x2, mod3, w_pack, w_small, *tabs)
    return {o[0]: r for o, r in zip(outs, res)}


def _fold_rows(x, op):
    n = x.shape[0]
    while n % (2 * SUBLANES) == 0:
        n //= 2
        x = op(x[:n], x[n:])
    slabs = [x[i * SUBLANES:(i + 1) * SUBLANES] for i in range(n // SUBLANES)]
    while len(slabs) > 1:
        nxt = [op(slabs[i], slabs[i + 1]) for i in range(0, len(slabs) - 1, 2)]
        slabs = nxt + ([slabs[-1]] if len(slabs) % 2 else [])
    return slabs[0]


def _col_max(x):
    return jnp.max(_fold_rows(x, jnp.maximum), axis=0, keepdims=True)


def _col_sum(x):
    return jnp.sum(_fold_rows(x, jnp.add), axis=0, keepdims=True)


def _count_rows(mask01):
    return _fold_rows(mask01, jnp.add)


def _stack_heads(q_ref, heads):
    return jnp.concatenate([q_ref[0, :, h * LANES:(h + 1) * LANES] for h in heads], axis=0)


def _flash_step(k, q_stack, v_t, bias4, m, acc):
    s = _dot_nt(k, q_stack) + bias4
    m_new = jnp.maximum(m, _col_max(s))
    e = jnp.exp(s - m_new).astype(BF16)
    return m_new, acc * jnp.exp(m - m_new) + _dot(v_t, e)


HEADS_PER_CHAIN = REP


def _head_chains():
    return [(h // REP, tuple(range(h, h + HEADS_PER_CHAIN))) for h in range(0, A_HEADS, HEADS_PER_CHAIN)]


def _flash_loop(q_ref, k_ref, vT_ref, bias_of, n_chunks, tq, kc):
    chains = _head_chains()
    q_stacks = [_stack_heads(q_ref, heads) for _, heads in chains]
    width = HEADS_PER_CHAIN * tq

    def body(c, carry):
        k = k_ref[0, pl.ds(pl.multiple_of(c * kc, kc), kc), :]
        scores = [_dot_nt(k, q_stack) for q_stack in q_stacks]
        out = []
        for (g, _), s, (m, acc) in zip(chains, scores, carry):
            s = s + jnp.concatenate([bias_of(g, c)] * HEADS_PER_CHAIN, axis=1)
            m_new = jnp.maximum(m, _col_max(s))
            e = jnp.exp(s - m_new).astype(BF16)
            out.append((m_new, acc * jnp.exp(m - m_new) + _dot(vT_ref[g, c], e)))
        return tuple(out)

    init = tuple((jnp.full((1, width), NEG, F32), jnp.zeros((LANES, width), F32)) for _ in chains)
    res = lax.fori_loop(0, n_chunks, body, init)
    return [(heads, acc) for (_, heads), (_, acc) in zip(chains, res)]


def _normalise(acc):
    return acc / jnp.maximum(acc[HEAD_DIM:HEAD_DIM + 1, :], 1e-30)


def _store_heads(o_ref, o_t, heads, tq):
    o = o_t.T
    for r, h in enumerate(heads):
        o_ref[0, :, h * HEAD_DIM:(h + 1) * HEAD_DIM] = o[r * tq:(r + 1) * tq, 0:HEAD_DIM].astype(o_ref.dtype)


def _select_mask(keys, thr, need, offset, tri):
    gt = jnp.where(keys > thr, 1.0, 0.0)
    eq = jnp.where(keys == thr, 1.0, 0.0)
    prefix = _dot(tri, eq.astype(BF16)) + offset
    return gt + jnp.where(prefix < need, eq, 0.0), offset + _col_sum(eq)


def _dsa_kernel(qi_ref, wT_ref, ki_ref, qa_ref, ka_ref, vT_ref, o_ref, keys_ref, bias_ref,
                *, tq, kc, n_keep):
    t0 = pl.program_id(1) * tq
    n_chunks = (t0 + tq + kc - 1) // kc
    wT = wT_ref[...] * (IDX_HEADS ** -0.5 * IDX_DIM ** -0.5)
    qi_stack = _stack_heads(qi_ref, range(IDX_HEADS))
    keypos = lax.broadcasted_iota(I32, (kc, tq), 0)
    qpos = t0 + lax.broadcasted_iota(I32, (kc, tq), 1)

    def score_chunk(c, carry):
        k0 = pl.multiple_of(c * kc, kc)
        sc = _dot_nt(ki_ref[0, pl.ds(k0, kc), :], qi_stack)
        score = jnp.zeros((kc, tq), F32)
        for h in range(IDX_HEADS):
            score = score + wT[h:h + 1, :] * jnp.maximum(sc[:, h * tq:(h + 1) * tq], 0.0)
        keys_ref[c] = jnp.where(k0 + keypos <= qpos, _sort_key(score), INT_MIN)
        return carry

    lax.fori_loop(0, n_chunks, score_chunk, 0)

    def count_ge(cand):
        def body(c, acc):
            return acc + _count_rows(jnp.where(keys_ref[c] >= cand, 1.0, 0.0))
        acc = lax.fori_loop(0, n_chunks, body, jnp.zeros((SUBLANES, tq), F32))
        return jnp.sum(acc, axis=0, keepdims=True)

    thr = _kth_largest_key(count_ge, (1, tq), n_keep)
    need = float(n_keep) - count_ge(thr + 1)
    tri = _tri_strict_lower(LANES, BF16)
    sub = lax.broadcasted_iota(I32, (LANES, tq), 0)
    qsub = t0 + lax.broadcasted_iota(I32, (LANES, tq), 1)

    def bias_chunk(c, offset):
        for j in range(kc // LANES):
            rows = slice(j * LANES, (j + 1) * LANES)
            sel, offset = _select_mask(keys_ref[c, rows, :], thr, need, offset, tri)
            causal = c * kc + j * LANES + sub <= qsub
            bias_ref[c, rows, :] = jnp.where(causal, (sel - 1.0) * 1e30, NEG)
        return offset

    lax.fori_loop(0, n_chunks, bias_chunk, jnp.zeros((1, tq), F32))

    accs = _flash_loop(qa_ref, ka_ref, vT_ref, lambda g, c: bias_ref[c], n_chunks, tq, kc)
    for heads, acc in accs:
        _store_heads(o_ref, _normalise(acc), heads, tq)


def _dsa(qi, wT, ki, qa, ka, vaT, bsz, seq, tq, kc):
    n_keep = min(DSA_TOPK_MAX, seq // 4)
    nq = seq // tq
    qblk = lambda w: pl.BlockSpec((1, tq, w), lambda b, i: (b, i, 0))
    full = lambda w: pl.BlockSpec((1, seq, w), lambda b, i: (b, 0, 0))
    return pl.pallas_call(
        functools.partial(_dsa_kernel, tq=tq, kc=kc, n_keep=n_keep),
        out_shape=jax.ShapeDtypeStruct((bsz, seq, A_HEADS * HEAD_DIM), BF16),
        grid=(bsz, nq),
        in_specs=[qblk(1024), pl.BlockSpec((IDX_HEADS, tq), lambda b, i: (0, b * nq + i)), full(LANES),
                  qblk(1024), full(LANES),
                  pl.BlockSpec((A_KV_HEADS, seq // kc, LANES, kc), lambda b, i: (0, b, 0, 0))],
        out_specs=qblk(512),
        scratch_shapes=[pltpu.VMEM((seq // kc, kc, tq), I32), pltpu.VMEM((seq // kc, kc, tq), F32)],
        compiler_params=_cparams(("parallel", "parallel")),
        name="dsa",
    )(qi, wT, ki, qa, ka, vaT)


def _gelu_tanh(x):
    return 0.5 * x * (1.0 + jnp.tanh(math.sqrt(2.0 / math.pi) * (x + 0.044715 * (x * x * x))))


def _compress_kernel(k_ref, v_ref, pk_ref, pv_ref, w1k_ref, w2k_ref, w1v_ref, w2v_ref,
                     kc_ref, vcT_ref, *, n_rows):
    half = CMP_BLOCK // 2

    def one(x_ref, p_ref, w1_ref, w2_ref):
        outs = []
        for g in range(B_KV_HEADS):
            lo = jnp.zeros((n_rows, CMP_HIDDEN), F32)
            hi = jnp.zeros((n_rows, CMP_HIDDEN), F32)
            for l in range(half):
                xl = x_ref[0, :, l * LANES + g * HEAD_DIM:l * LANES + (g + 1) * HEAD_DIM]
                a = (xl + p_ref[l:l + 1, :]).astype(BF16)
                b = (xl + p_ref[half + l:half + l + 1, :]).astype(BF16)
                lo = lo + _dot(a, w1_ref[l * HEAD_DIM:(l + 1) * HEAD_DIM, :].astype(BF16))
                hi = hi + _dot(b, w1_ref[(half + l) * HEAD_DIM:(half + l + 1) * HEAD_DIM, :].astype(BF16))
            hid = lo + pltpu.roll(hi, n_rows - 1, axis=0)
            outs.append(_dot(_gelu_tanh(hid).astype(BF16), w2_ref[...].astype(BF16)))
        return outs

    k0, k1 = one(k_ref, pk_ref, w1k_ref, w2k_ref)
    kc_ref[0] = jnp.concatenate([k0, k1], axis=1).astype(kc_ref.dtype)
    for g, v in enumerate(one(v_ref, pv_ref, w1v_ref, w2v_ref)):
        vcT_ref[0, g] = jnp.concatenate([v, jnp.zeros_like(v)], axis=1).T.astype(vcT_ref.dtype)


def _compress(kcmp, vcmp, pos_k, pos_v, w1k, w2k, w1v, w2v):
    bsz, n_rows, width = kcmp.shape
    xspec = pl.BlockSpec((1, n_rows, width), lambda b: (b, 0, 0))
    cst = lambda a: pl.BlockSpec(a.shape, lambda b: (0,) * a.ndim)
    return pl.pallas_call(
        functools.partial(_compress_kernel, n_rows=n_rows),
        out_shape=(jax.ShapeDtypeStruct((bsz, n_rows, LANES), BF16),
                   jax.ShapeDtypeStruct((bsz, B_KV_HEADS, LANES, n_rows), BF16)),
        grid=(bsz,),
        in_specs=[xspec, xspec, cst(pos_k), cst(pos_v), cst(w1k), cst(w2k), cst(w1v), cst(w2v)],
        out_specs=(pl.BlockSpec((1, n_rows, LANES), lambda b: (b, 0, 0)),
                   pl.BlockSpec((1, B_KV_HEADS, LANES, n_rows), lambda b: (b, 0, 0, 0))),
        compiler_params=_cparams(("parallel",)),
        name="compress",
    )(kcmp, vcmp, pos_k, pos_v, w1k, w2k, w1v, w2v)


def _split3(x):
    a = x.astype(BF16)
    r = x - a.astype(F32)
    b = r.astype(BF16)
    c = (r - b.astype(F32)).astype(BF16)
    return a, b, c


def _nsa_kernel(qraw_ref, qrot_ref, kc_ref, vcT_ref, ksel_ref, vselT_ref, kwin_ref, vwinT_ref,
                gT_ref, exp_ref, o_ref, bias_ref, *, tq, kc, seq, n_c):
    t0 = pl.program_id(1) * tq
    n_chunks = (t0 + tq + kc - 1) // kc
    n_s = seq // SEL_BLOCK
    n_pick = min(SEL_COUNT, n_s)
    span = WINDOW + tq
    gT = gT_ref[...]

    ridx = lax.broadcasted_iota(I32, (LANES, tq), 0)
    tl = t0 + lax.broadcasted_iota(I32, (LANES, tq), 1)
    valid_c = jnp.where((ridx * CMP_STRIDE + CMP_BLOCK - 1 <= tl) & (ridx < n_c), 1.0, 0.0)
    valid4 = jnp.concatenate([valid_c] * REP, axis=1) > 0.0
    js = lax.broadcasted_iota(I32, (LANES, LANES), 0) * SEL_BLOCK
    cs = lax.broadcasted_iota(I32, (LANES, LANES), 1) * CMP_STRIDE
    overlap_t = jnp.where((cs <= js + SEL_BLOCK - 1) & (cs + CMP_BLOCK - 1 >= js), 1.0, 0.0).astype(BF16)
    cur = tl // SEL_BLOCK
    forced = (ridx == 0) | ((cur - ridx >= 0) & (cur - ridx < SEL_LOCAL))
    blk_causal = ridx * SEL_BLOCK <= tl
    tri = _tri_strict_lower(LANES, BF16)
    keypos = lax.broadcasted_iota(I32, (kc, tq), 0)
    qpos = t0 + lax.broadcasted_iota(I32, (kc, tq), 1)
    w0 = pl.multiple_of(jnp.clip(t0 - WINDOW, 0, seq - span), tq)
    wdiff = (t0 + lax.broadcasted_iota(I32, (span, tq), 1)) - (w0 + lax.broadcasted_iota(I32, (span, tq), 0))
    wbias = jnp.where((wdiff >= 0) & (wdiff < WINDOW), 0.0, NEG)
    wbias_c = jnp.concatenate([wbias] * HEADS_PER_CHAIN, axis=1)
    wblk = w0 // LANES

    o_cmp_g = []
    for g in range(B_KV_HEADS):
        heads = range(g * REP, (g + 1) * REP)
        s = jnp.where(valid4, _dot_nt(kc_ref[0], _stack_heads(qraw_ref, heads)), NEG)
        e = jnp.where(valid4, jnp.exp(s - _col_max(s)), 0.0)
        p = e / jnp.maximum(_col_sum(e), 1e-30)
        o_cmp_g.append(_dot(vcT_ref[0, g], p.astype(BF16)))
        p_sum = p[:, 0:tq]
        for r in range(1, REP):
            p_sum = p_sum + p[:, r * tq:(r + 1) * tq]
        pa, pb, pc = _split3(p_sum)
        imp = _dot(overlap_t, pa) + _dot(overlap_t, pb) + _dot(overlap_t, pc)
        imp = jnp.where(forced, FORCE, jnp.where(blk_causal, imp, NEG))
        keys = _sort_key(imp[0:n_s, :])
        count_ge = lambda cand, keys=keys: _col_sum(jnp.where(keys >= cand, 1.0, 0.0))
        thr = _kth_largest_key(count_ge, (1, tq), n_pick)
        need = float(n_pick) - _col_sum(jnp.where(keys > thr, 1.0, 0.0))
        blk_sel, _ = _select_mask(keys, thr, need, jnp.zeros((1, tq), F32), tri[0:n_s, 0:n_s])
        blk_sel = jnp.concatenate([blk_sel, jnp.zeros((LANES - n_s, tq), F32)], axis=0).astype(BF16)

        def bias_chunk(c, carry, blk_sel=blk_sel, g=g):
            tok_sel = _dot(exp_ref[c], blk_sel)
            bias_ref[g, c] = jnp.where(c * kc + keypos <= qpos, (tok_sel - 1.0) * 1e30, NEG)
            return carry

        lax.fori_loop(0, n_chunks, bias_chunk, 0)

    sel = _flash_loop(qrot_ref, ksel_ref, vselT_ref, lambda g, c: bias_ref[g, c], n_chunks, tq, kc)

    kwin = kwin_ref[0, pl.ds(w0, span), :]
    for (g, heads), (_, acc_sel) in zip(_head_chains(), sel):
        sw = _dot_nt(kwin, _stack_heads(qrot_ref, heads)) + wbias_c
        ew = jnp.exp(sw - _col_max(sw)).astype(BF16)
        acc_win = _dot(vwinT_ref[g, wblk], ew[0:LANES, :])
        for j in range(1, span // LANES):
            acc_win = acc_win + _dot(vwinT_ref[g, wblk + j], ew[j * LANES:(j + 1) * LANES, :])

        o_sel = _normalise(acc_sel)
        o_win = _normalise(acc_win)
        cols = []
        for r, h in enumerate(heads):
            col = slice(r * tq, (r + 1) * tq)
            cmp_col = slice((h % REP) * tq, (h % REP + 1) * tq)
            cols.append(gT[3 * h:3 * h + 1, :] * o_cmp_g[g][:, cmp_col]
                        + gT[3 * h + 1:3 * h + 2, :] * o_sel[:, col]
                        + gT[3 * h + 2:3 * h + 3, :] * o_win[:, col])
        _store_heads(o_ref, jnp.concatenate(cols, axis=1), heads, tq)


def _nsa(qraw, qrot, kc_x, vcT, ksel, vselT, kwin, vwinT, gT, bsz, seq, tq, kc):
    n_c = (seq - CMP_BLOCK) // CMP_STRIDE + 1
    nq = seq // tq
    key_blk = (jnp.arange(seq, dtype=I32) // SEL_BLOCK).reshape(seq // kc, kc, 1)
    expand = (key_blk == jnp.arange(LANES, dtype=I32).reshape(1, 1, LANES)).astype(BF16)
    qblk = lambda w: pl.BlockSpec((1, tq, w), lambda b, i: (b, i, 0))
    full = lambda w: pl.BlockSpec((1, seq, w), lambda b, i: (b, 0, 0))
    g = B_KV_HEADS
    return pl.pallas_call(
        functools.partial(_nsa_kernel, tq=tq, kc=kc, seq=seq, n_c=n_c),
        out_shape=jax.ShapeDtypeStruct((bsz, seq, B_HEADS * HEAD_DIM), BF16),
        grid=(bsz, nq),
        in_specs=[qblk(1024), qblk(1024),
                  pl.BlockSpec((1,) + kc_x.shape[1:], lambda b, i: (b, 0, 0)),
                  pl.BlockSpec((1,) + vcT.shape[1:], lambda b, i: (b, 0, 0, 0)),
                  full(LANES), pl.BlockSpec((g, seq // kc, LANES, kc), lambda b, i: (0, b, 0, 0)),
                  full(LANES), pl.BlockSpec((g, seq // LANES, LANES, LANES), lambda b, i: (0, b, 0, 0)),
                  pl.BlockSpec((N_GATES, tq), lambda b, i: (0, b * nq + i)),
                  pl.BlockSpec(expand.shape, lambda b, i: (0, 0, 0))],
        out_specs=qblk(512),
        scratch_shapes=[pltpu.VMEM((g, seq // kc, kc, tq), F32)],
        compiler_params=_cparams(("parallel", "parallel")),
        name="nsa",
    )(qraw, qrot, kc_x, vcT, ksel, vselT, kwin, vwinT, gT, expand)


def _pack_pairs(x):
    n = x.shape[1] // 2
    lo = pltpu.bitcast(x[:, :n].astype(BF16).astype(F32), I32)
    hi = pltpu.bitcast(x[:, n:].astype(BF16).astype(F32), I32)
    return lax.shift_right_logical(lo, 16) | (hi & jnp.int32(-65536))


def _unpack_pairs(p):
    lo = pltpu.bitcast(lax.shift_left(p, 16), F32)
    hi = pltpu.bitcast(p & jnp.int32(-65536), F32)
    return jnp.concatenate([lo, hi], axis=1)


def _layer_norm(y, g, b):
    mu = jnp.mean(y, axis=1, keepdims=True)
    yc = y - mu
    var = jnp.mean(yc * yc, axis=1, keepdims=True)
    return yc * lax.rsqrt(var + LN_EPS) * g + b


def _out_proj_kernel(oa_ref, ob_ref, ga_ref, gb_ref, x_ref, mod_ref, wa_ref, wb_ref, wo_ref,
                     g1_ref, b1_ref, wrh_ref, wrl_ref, x1_ref, u2_ref, lg_ref):
    mod = mod_ref[0]
    merged = (ga_ref[...].astype(F32) * _dot(oa_ref[...], wa_ref[...])
              + gb_ref[...].astype(F32) * _dot(ob_ref[...], wb_ref[...]))
    mix = _dot(merged.astype(BF16), wo_ref[...])
    x1 = _layer_norm(DN_ALPHA * x_ref[...] + mod[2:3, :] * mix, g1_ref[...], b1_ref[...])
    x1_ref[...] = x1
    u2 = x1 * (1.0 + mod[4:5, :]) + mod[3:4, :]
    packed = _pack_pairs(u2)
    for j in range(PIECES):
        u2_ref[j] = packed[:, j * SC_ROW:(j + 1) * SC_ROW]
    uh = u2.astype(BF16)
    ul = (u2 - uh.astype(F32)).astype(BF16)
    lg_ref[...] = _dot_nt(wrh_ref[...], uh) + _dot_nt(wrh_ref[...], ul) + _dot_nt(wrl_ref[...], uh)


def _out_proj(oa, ob, ga, gb, x2, mod3, wa, wb, wo, g1, b1, wrh, wrl, seq, tm):
    t, d = x2.shape
    per_b = seq // tm
    row = lambda w: pl.BlockSpec((tm, w), lambda i: (i, 0))
    cst = lambda a: pl.BlockSpec(a.shape, lambda i: (0,) * a.ndim)
    return pl.pallas_call(
        _out_proj_kernel,
        out_shape=(jax.ShapeDtypeStruct((t, d), F32), jax.ShapeDtypeStruct((PIECES, t, SC_ROW), I32),
                   jax.ShapeDtypeStruct((N_EXPERTS, t), F32)),
        grid=(t // tm,),
        in_specs=[row(512), row(512), row(d), row(d), row(d),
                  pl.BlockSpec((1, 6, d), lambda i: (i // per_b, 0, 0)),
                  cst(wa), cst(wb), cst(wo), cst(g1), cst(b1), cst(wrh), cst(wrl)],
        out_specs=(row(d), pl.BlockSpec((PIECES, tm, SC_ROW), lambda i: (0, i, 0)),
                   pl.BlockSpec((N_EXPERTS, tm), lambda i: (0, i))),
        compiler_params=_cparams(("parallel",)),
        name="out_proj",
    )(oa, ob, ga, gb, x2, mod3, wa, wb, wo, g1, b1, wrh, wrl)


def _first_max(x, rows):
    m = jnp.max(x, axis=0, keepdims=True)
    idx = jnp.min(jnp.where(x == m, rows, 1e9), axis=0, keepdims=True)
    return m, idx


def _router_kernel(lg_ref, rb_ref, idx_ref, w_ref, pos_ref, cnt_ref, carry_ref, *, tm):
    @pl.when(pl.program_id(0) == 0)
    def _():
        carry_ref[...] = jnp.zeros_like(carry_ref)

    per_g = N_EXPERTS // N_GROUPS
    scores = _sigmoid(lg_ref[...])
    choice = scores + rb_ref[...][:, 0:1]
    rows = lax.broadcasted_iota(I32, (N_EXPERTS, tm), 0).astype(F32)
    rows_g = lax.broadcasted_iota(I32, (per_g, tm), 0).astype(F32)
    ninf = -jnp.inf

    gs = []
    for g in range(N_GROUPS):
        x = choice[g * per_g:(g + 1) * per_g, :]
        m1, i1 = _first_max(x, rows_g)
        m2 = jnp.max(jnp.where(rows_g == i1, ninf, x), axis=0, keepdims=True)
        gs.append(m1 + m2)
    gscore = jnp.concatenate(gs, axis=0)
    rows8 = lax.broadcasted_iota(I32, (N_GROUPS, tm), 0).astype(F32)
    keep = jnp.zeros((N_GROUPS, tm), F32)
    for _ in range(TOPK_GROUPS):
        _, gi = _first_max(gscore, rows8)
        hit = rows8 == gi
        keep = jnp.where(hit, 1.0, keep)
        gscore = jnp.where(hit, ninf, gscore)
    keep_full = jnp.concatenate(
        [jnp.broadcast_to(keep[g:g + 1, :], (per_g, tm)) for g in range(N_GROUPS)], axis=0)
    masked = jnp.where(keep_full > 0.0, choice, NEG)

    idxs, ws = [], []
    onehot = jnp.zeros((N_EXPERTS, tm), F32)
    for _ in range(TOP_K):
        _, ei = _first_max(masked, rows)
        hit = rows == ei
        idxs.append(ei)
        ws.append(jnp.sum(jnp.where(hit, scores, 0.0), axis=0, keepdims=True))
        masked = jnp.where(hit, ninf, masked)
        onehot = jnp.where(hit, 1.0, onehot)
    idx = jnp.concatenate(idxs, axis=0)
    w = jnp.concatenate(ws, axis=0)
    idx_ref[...] = idx.astype(I32)
    w_ref[...] = w / jnp.sum(w, axis=0, keepdims=True) * ROUTED_SCALE

    tri = _tri_strict_upper(tm, BF16)
    base = _dot(onehot.astype(BF16), tri) + carry_ref[...][:, 0:1]
    pos = [jnp.sum(jnp.where(rows == idxs[k], base, 0.0), axis=0, keepdims=True) for k in range(TOP_K)]
    pos_ref[...] = jnp.concatenate(pos, axis=0).astype(I32)
    carry = carry_ref[...] + jnp.sum(onehot, axis=1, keepdims=True)
    carry_ref[...] = carry
    cnt_ref[...] = carry


def _router(lg, router_bias, tm):
    e, t = lg.shape
    rb = jnp.broadcast_to(router_bias.reshape(e, 1).astype(F32), (e, LANES))
    tok = lambda r: pl.BlockSpec((r, tm), lambda i: (0, i))
    return pl.pallas_call(
        functools.partial(_router_kernel, tm=tm),
        out_shape=(jax.ShapeDtypeStruct((TOP_K, t), I32), jax.ShapeDtypeStruct((TOP_K, t), F32),
                   jax.ShapeDtypeStruct((TOP_K, t), I32), jax.ShapeDtypeStruct((e, LANES), F32)),
        grid=(t // tm,),
        in_specs=[tok(e), pl.BlockSpec((e, LANES), lambda i: (0, 0))],
        out_specs=(tok(TOP_K), tok(TOP_K), tok(TOP_K), pl.BlockSpec((e, LANES), lambda i: (0, 0))),
        scratch_shapes=[pltpu.VMEM((e, LANES), F32)],
        compiler_params=_cparams(("arbitrary",)),
        name="router",
    )(lg, rb)


def _dest_kernel(idx_ref, pos_ref, st_ref, dest_ref, *, tm):
    rows = lax.broadcasted_iota(I32, (N_EXPERTS, tm), 0)
    starts = st_ref[...][:, 0:1]
    idx = idx_ref[...]
    out = []
    for k in range(TOP_K):
        out.append(jnp.sum(jnp.where(rows == idx[k:k + 1, :], starts, 0.0), axis=0, keepdims=True))
    dest_ref[...] = jnp.concatenate(out, axis=0).astype(I32) + pos_ref[...]


def _dest(idx, pos, starts, tm):
    k, t = idx.shape
    st = jnp.broadcast_to(starts.reshape(N_EXPERTS, 1).astype(F32), (N_EXPERTS, LANES))
    tok = pl.BlockSpec((k, tm), lambda i: (0, i))
    return pl.pallas_call(
        functools.partial(_dest_kernel, tm=tm),
        out_shape=jax.ShapeDtypeStruct((k, t), I32),
        grid=(t // tm,),
        in_specs=[tok, tok, pl.BlockSpec((N_EXPERTS, LANES), lambda i: (0, 0))],
        out_specs=tok,
        compiler_params=_cparams(("parallel",)),
        name="dest",
    )(idx, pos, st)


def _experts_kernel(blk_ref, used_ref, first_ref, slot_ref, next_ref, x_ref, wg_hbm, wu_hbm, wd_hbm, y_ref,
                    wg_buf, wu_buf, wd_buf, wg_bf, wu_bf, wd_bf, sems):
    b = pl.program_id(0)
    active = b < used_ref[0]

    def weight_copies(e, slot):
        return [pltpu.make_async_copy(hbm.at[e], buf.at[slot], sems.at[slot, i])
                for i, (hbm, buf) in enumerate(((wg_hbm, wg_buf), (wu_hbm, wu_buf), (wd_hbm, wd_buf)))]

    @pl.when(b == 0)
    def _():
        for cp in weight_copies(blk_ref[0], 0):
            cp.start()

    @pl.when(active & (first_ref[b] == 1))
    def _():
        slot = slot_ref[b]
        for cp in weight_copies(blk_ref[b], slot):
            cp.wait()
        nxt = next_ref[b]

        @pl.when(nxt >= 0)
        def _():
            for cp in weight_copies(nxt, 1 - slot):
                cp.start()

        wg_bf[...] = wg_buf[slot].astype(BF16)
        wu_bf[...] = wu_buf[slot].astype(BF16)
        wd_bf[...] = wd_buf[slot].astype(BF16)

    @pl.when(active)
    def _():
        x = jnp.concatenate([x_ref[j] for j in range(PIECES)], axis=1)
        x = _unpack_pairs(x).astype(BF16)
        a = _dot(x, wg_bf[...])
        u = _dot(x, wu_bf[...])
        h = (a * _sigmoid(a) * u).astype(BF16)
        y = _pack_pairs(_dot(h, wd_bf[...]))
        for j in range(PIECES):
            y_ref[j] = y[:, j * SC_ROW:(j + 1) * SC_ROW]


def _experts(xs, plan, wg, wu, wd, bm):
    _, cap, _ = xs.shape
    n_blocks = cap // bm
    d, f = wg.shape[1], wg.shape[2]
    rows = pl.BlockSpec((PIECES, bm, SC_ROW), lambda b, blk, used, *_: (0, jnp.minimum(b, used[0] - 1), 0))
    hbm = pl.BlockSpec(memory_space=pl.ANY)
    return pl.pallas_call(
        _experts_kernel,
        out_shape=jax.ShapeDtypeStruct(xs.shape, I32),
        grid_spec=pltpu.PrefetchScalarGridSpec(
            num_scalar_prefetch=5,
            grid=(n_blocks,),
            in_specs=[rows, hbm, hbm, hbm],
            out_specs=rows,
            scratch_shapes=[pltpu.VMEM((2, d, f), F32), pltpu.VMEM((2, d, f), F32), pltpu.VMEM((2, f, d), F32),
                            pltpu.VMEM((d, f), BF16), pltpu.VMEM((d, f), BF16), pltpu.VMEM((f, d), BF16),
                            pltpu.SemaphoreType.DMA((2, 3))]),
        compiler_params=_cparams(("arbitrary",)),
        name="experts",
    )(plan["blk_e"], plan["n_used"], plan["first"], plan["slot"], plan["next_e"], xs, wg, wu, wd)


def _final_kernel(x1_ref, u2_ref, yg_ref, w_ref, mod_ref, sg_ref, su_ref, sd_ref, g2_ref, b2_ref, o_ref):
    mod = mod_ref[0]
    w = w_ref[...]
    rows = lambda ref, *lead: jnp.concatenate([ref[(j,) + lead] for j in range(PIECES)], axis=1)
    routed = w[:, 0:1] * _unpack_pairs(rows(yg_ref, 0))
    for k in range(1, TOP_K):
        routed = routed + w[:, k:k + 1] * _unpack_pairs(rows(yg_ref, k))
    u = _unpack_pairs(rows(u2_ref)).astype(BF16)
    a = _dot(u, sg_ref[...])
    b = _dot(u, su_ref[...])
    shared = _dot((a * _sigmoid(a) * b).astype(BF16), sd_ref[...])
    y = DN_ALPHA * x1_ref[...] + mod[5:6, :] * (routed + shared)
    o_ref[...] = _layer_norm(y, g2_ref[...], b2_ref[...])


def _final(x1, u2p, yg, wtok, mod3, sg, su, sd, g2, b2, seq, tm):
    t, d = x1.shape
    per_b = seq // tm
    row = lambda w: pl.BlockSpec((tm, w), lambda i: (i, 0))
    cst = lambda a: pl.BlockSpec(a.shape, lambda i: (0,) * a.ndim)
    return pl.pallas_call(
        _final_kernel,
        out_shape=jax.ShapeDtypeStruct((t, d), F32),
        grid=(t // tm,),
        in_specs=[row(d), pl.BlockSpec((PIECES, tm, SC_ROW), lambda i: (0, i, 0)),
                  pl.BlockSpec((PIECES, TOP_K, tm, SC_ROW), lambda i: (0, 0, i, 0)),
                  row(LANES), pl.BlockSpec((1, 6, d), lambda i: (i // per_b, 0, 0)),
                  cst(sg), cst(su), cst(sd), cst(g2), cst(b2)],
        out_specs=row(d),
        compiler_params=_cparams(("parallel",)),
        name="final",
    )(x1, u2p, yg, wtok, mod3, sg, su, sd, g2, b2)


def _sc_mesh():
    return plsc.VectorSubcoreMesh(core_axis_name="core", subcore_axis_name="subcore")


def _sc_scatter_rows(src, idx, n_out, src_block):
    n_idx = idx.shape[0]

    @functools.partial(pl.kernel, out_type=jax.ShapeDtypeStruct((n_out, SC_ROW), src.dtype),
                       mesh=_sc_mesh(), scratch_types=[])
    def k(x_hbm, i_hbm, o_hbm):
        def body(x_vmem, i_vmem):
            pltpu.sync_copy(x_vmem, o_hbm.at[i_vmem.at[0]])

        pltpu.emit_pipeline(
            body, grid=(n_idx // SC_WINDOW,),
            in_specs=[pl.BlockSpec((SC_WINDOW, SC_ROW), lambda i: (src_block(i), 0)),
                      pl.BlockSpec((1, SC_WINDOW), lambda i: (0, i))],
            out_specs=[],
            core_axis_name=("core", "subcore"),
            dimension_semantics=(pltpu.PARALLEL,),
        )(x_hbm, i_hbm)

    return k(src, idx.reshape(1, n_idx))


def _sc_gather_rows(src, idx):
    n_idx = idx.shape[0]

    @functools.partial(pl.kernel, out_type=jax.ShapeDtypeStruct((n_idx, SC_ROW), src.dtype),
                       mesh=_sc_mesh(), scratch_types=[])
    def k(x_hbm, i_hbm, o_hbm):
        def body(i_vmem, o_vmem):
            pltpu.sync_copy(x_hbm.at[i_vmem.at[0]], o_vmem)

        pltpu.emit_pipeline(
            body, grid=(n_idx // SC_WINDOW,),
            in_specs=[pl.BlockSpec((1, SC_WINDOW), lambda i: (0, i))],
            out_specs=[pl.BlockSpec((SC_WINDOW, SC_ROW), lambda i: (i, 0))],
            core_axis_name=("core", "subcore"),
            dimension_semantics=(pltpu.PARALLEL,),
        )(i_hbm, o_hbm)

    return k(src, idx.reshape(1, n_idx))


def _moe_plan(counts, n_tok):
    bm = BM_EXPERT
    padded = (counts + bm - 1) // bm * bm
    p_ends = jnp.cumsum(padded)
    starts = p_ends - padded
    n_blocks = n_tok * TOP_K // bm + N_EXPERTS
    blk = jnp.arange(n_blocks, dtype=I32)
    blk_e = jnp.minimum(jnp.sum(p_ends[None, :] <= (blk * bm)[:, None], axis=1), N_EXPERTS - 1).astype(I32)
    n_used = (p_ends[-1] // bm).astype(I32)
    prev = jnp.concatenate([jnp.full((1,), -1, I32), blk_e[:-1]])
    first = ((blk_e != prev) & (blk < n_used)).astype(I32)
    slot = (jnp.cumsum(first) - 1) % 2
    first_pos = jnp.where(first == 1, blk, n_blocks)
    next_first = lax.cummin(jnp.concatenate([first_pos[1:], jnp.full((1,), n_blocks, I32)]), reverse=True)
    next_e = jnp.where(next_first < n_blocks, blk_e[jnp.minimum(next_first, n_blocks - 1)], -1)
    plan = dict(blk_e=blk_e, n_used=n_used.reshape(1), first=first, slot=slot.astype(I32),
                next_e=next_e.astype(I32))
    return starts, plan, n_blocks


def _layer(x, mod, positions, w_in, w_br_a, w_br_b, w_out, cmp_pos_k, cmp_pos_v, cmp_k_w1, cmp_k_w2,
           cmp_v_w1, cmp_v_w2, ln1_g, ln1_b, w_router, router_bias, w_exp_gate, w_exp_up, w_exp_down,
           w_sh_gate, w_sh_up, w_sh_down, ln2_g, ln2_b):
    bsz, seq, d = x.shape
    t = bsz * seq
    assert seq // CMP_STRIDE == LANES and seq % TQ_ATTN == 0 and seq % TM_PROJ == 0 and KC_ATTN == TM_PROJ
    x2 = x.reshape(t, d)
    mod3 = mod.reshape(bsz, 6, d)

    w_pack, w_small = _pack_w_in(w_in)
    z = _in_proj(x2, mod3, w_pack, w_small, _rope_tables(positions), seq, TM_PROJ)
    per_b = lambda name: z[name].reshape(bsz, seq, z[name].shape[1])

    o_a = _dsa(per_b("qi"), z["wT"], per_b("ki"), per_b("qa"), per_b("ka"), z["vaT"],
               bsz, seq, TQ_ATTN, KC_ATTN)

    n_rows = seq // CMP_STRIDE
    kc_x, vcT = _compress(z["kcmp"].reshape(bsz, n_rows, CMP_STRIDE * LANES),
                          z["vcmp"].reshape(bsz, n_rows, CMP_STRIDE * LANES),
                          cmp_pos_k, cmp_pos_v, cmp_k_w1, cmp_k_w2, cmp_v_w1, cmp_v_w2)
    o_b = _nsa(per_b("qbraw"), per_b("qbrot"), kc_x, vcT, per_b("ksel"), z["vselT"], per_b("kwin"),
               z["vwinT"], z["gT"], bsz, seq, TQ_ATTN, KC_ATTN)

    wr_hi = w_router.T.astype(BF16)
    wr_lo = (w_router.T - wr_hi.astype(F32)).astype(BF16)
    x1, u2p, logits = _out_proj(
        o_a.reshape(t, -1), o_b.reshape(t, -1), z["ga"], z["gb"], x2, mod3,
        w_br_a.astype(BF16), w_br_b.astype(BF16), w_out.astype(BF16),
        ln1_g.reshape(1, d), ln1_b.reshape(1, d), wr_hi, wr_lo, seq, TM_PROJ)

    idx, wts, pos, counts = _router(logits, router_bias, TM_ROUTE)
    starts, plan, n_blocks = _moe_plan(counts[:, 0].astype(I32), t)
    dest = _dest(idx, pos, starts, TM_ROUTE)

    cap = n_blocks * BM_EXPERT
    dest_p = (dest[None] + (jnp.arange(PIECES, dtype=I32) * cap).reshape(PIECES, 1, 1)).reshape(-1)
    tb = t // SC_WINDOW
    xs = _sc_scatter_rows(u2p.reshape(PIECES * t, SC_ROW), dest_p, cap * PIECES,
                          lambda i: (i // (TOP_K * tb)) * tb + i % tb)
    ys = _experts(xs.reshape(PIECES, cap, SC_ROW), plan, w_exp_gate, w_exp_up, w_exp_down, BM_EXPERT)
    yg = _sc_gather_rows(ys.reshape(cap * PIECES, SC_ROW), dest_p).reshape(PIECES, TOP_K, t, SC_ROW)

    wtok = jnp.pad(wts.T, ((0, 0), (0, LANES - TOP_K)))
    return _final(x1, u2p, yg, wtok, mod3, w_sh_gate.astype(BF16), w_sh_up.astype(BF16),
                  w_sh_down.astype(BF16), ln2_g.reshape(1, d), ln2_b.reshape(1, d), seq, TM_PROJ
                  ).reshape(bsz, seq, d)


def kernel(x, c, positions, w_ada, b_ada, w_in, w_br_a, w_br_b, w_out, cmp_pos_k, cmp_pos_v, cmp_k_w1,
           cmp_k_w2, cmp_v_w1, cmp_v_w2, ln1_g, ln1_b, w_router, router_bias, w_exp_gate, w_exp_up,
           w_exp_down, w_sh_gate, w_sh_up, w_sh_down, ln2_g, ln2_b):
    for l in range(w_ada.shape[0]):
        mod = _mod(c, w_ada[l], b_ada[l])
        x = _layer(x, mod, positions, w_in[l], w_br_a[l], w_br_b[l], w_out[l], cmp_pos_k[l], cmp_pos_v[l],
                   cmp_k_w1[l], cmp_k_w2[l], cmp_v_w1[l], cmp_v_w2[l], ln1_g[l], ln1_b[l], w_router[l],
                   router_bias[l], w_exp_gate[l], w_exp_up[l], w_exp_down[l], w_sh_gate[l], w_sh_up[l],
                   w_sh_down[l], ln2_g[l], ln2_b[l])
    return x
```

```python
import functools
import math

import jax
import jax.numpy as jnp
import numpy as np
from jax import lax
from jax.experimental import pallas as pl
from jax.experimental.pallas import tpu as pltpu
from jax.experimental.pallas import tpu_sc as plsc

F32 = jnp.float32
BF16 = jnp.bfloat16
I32 = jnp.int32

D_MODEL = 1024
HEAD_DIM = 64
ROPE_THETA = 500000.0
ROPE_FRACTION = 4
A_HEADS = 8
A_KV_HEADS = 2
IDX_HEADS = 8
IDX_DIM = 32
DSA_TOPK_MAX = 256
B_HEADS = 8
B_KV_HEADS = 2
REP = 4
CMP_BLOCK = 32
CMP_STRIDE = 16
CMP_HIDDEN = 256
SEL_BLOCK = 64
SEL_COUNT = 16
SEL_LOCAL = 2
WINDOW = 512
N_EXPERTS = 256
TOP_K = 8
N_GROUPS = 8
TOPK_GROUPS = 4
ROUTED_SCALE = 2.5
DEPTH = 1
DN_ALPHA = (2 * DEPTH) ** 0.25
LN_EPS = 1e-5
NEG = -1e30
FORCE = 1e9
INT_MIN = -2147483648
N_GATES = 3 * B_HEADS

LANES = 128
SUBLANES = 8
VMEM_LIMIT = 56 * 1024 * 1024
SC_WINDOW = 128
SC_ROW = 256
PIECES = (D_MODEL // 2) // SC_ROW

TM_PROJ = 512
TQ_ATTN = 256
KC_ATTN = 512
TM_ROUTE = 512
BM_EXPERT = 512

_IN_WIDTHS = (512, 128, 128, 256, 32, 8, 512, 128, 128, 128, 128, 128, 128, 24, 1024, 1024)
_IN_OFFS = np.concatenate([[0], np.cumsum(_IN_WIDTHS)]).tolist()

NT_DIMS = (((1,), (1,)), ((), ()))


def _cparams(sem):
    return pltpu.CompilerParams(dimension_semantics=sem, vmem_limit_bytes=VMEM_LIMIT)


def _sigmoid(x):
    return 1.0 / (1.0 + jnp.exp(-x))


def _dot(a, b):
    return jnp.dot(a, b, preferred_element_type=F32)


def _dot_nt(a, b):
    return lax.dot_general(a, b, NT_DIMS, preferred_element_type=F32)


def _sort_key(x):
    x = jnp.where(x == 0.0, 0.0, x)
    bits = pltpu.bitcast(x, I32)
    return jnp.where(bits < 0, bits ^ 0x7FFFFFFF, bits)


def _kth_largest_key(count_ge, shape, k):
    kf = float(k)
    t0 = jnp.where(count_ge(jnp.zeros(shape, I32)) >= kf, 0, INT_MIN).astype(I32)

    def body(it, t):
        cand = t + jnp.left_shift(jnp.int32(1), 30 - it)
        return jnp.where(count_ge(cand) >= kf, cand, t)

    return lax.fori_loop(0, 31, body, t0)


def _tri_strict_lower(n, dtype):
    r = lax.broadcasted_iota(I32, (n, n), 0)
    c = lax.broadcasted_iota(I32, (n, n), 1)
    return jnp.where(c < r, 1.0, 0.0).astype(dtype)


def _tri_strict_upper(n, dtype):
    r = lax.broadcasted_iota(I32, (n, n), 0)
    c = lax.broadcasted_iota(I32, (n, n), 1)
    return jnp.where(r < c, 1.0, 0.0).astype(dtype)


def _mod_kernel(c_ref, w_ref, b_ref, o_ref):
    c = c_ref[...]
    cond = (c * _sigmoid(c)).astype(BF16)
    o_ref[...] = _dot(cond, w_ref[...].astype(BF16)) + b_ref[...]


def _mod(c, w_ada, b_ada):
    bsz, d = c.shape
    n = w_ada.shape[1]
    tn = 1024
    return pl.pallas_call(
        _mod_kernel,
        out_shape=jax.ShapeDtypeStruct((bsz, n), F32),
        grid=(n // tn,),
        in_specs=[pl.BlockSpec((bsz, d), lambda j: (0, 0)),
                  pl.BlockSpec((d, tn), lambda j: (0, j)),
                  pl.BlockSpec((1, tn), lambda j: (0, j))],
        out_specs=pl.BlockSpec((bsz, tn), lambda j: (0, j)),
        compiler_params=_cparams(("parallel",)),
        name="mod",
    )(c, w_ada, b_ada.reshape(1, n))


def _rope(z, c_tab, s_tab, period, half):
    w = z.shape[1]
    reps = w // LANES
    c = jnp.concatenate([c_tab] * reps, axis=1) if reps > 1 else c_tab
    s = jnp.concatenate([s_tab] * reps, axis=1) if reps > 1 else s_tab
    lane = lax.broadcasted_iota(I32, z.shape, 1)
    first = (lane & (period - 1)) < half
    partner = jnp.where(first, pltpu.roll(z, w - half, axis=1), pltpu.roll(z, half, axis=1))
    return z * c + partner * s


def _in_proj_kernel(x_ref, mod_ref, w_ref, wsm_ref, c64_ref, s64_ref, c32_ref, s32_ref,
                    qa_ref, ka_ref, vaT_ref, qi_ref, ki_ref, wT_ref, qbraw_ref, qbrot_ref,
                    kcmp_ref, vcmp_ref, ksel_ref, vselT_ref, kwin_ref, vwinT_ref, gT_ref, ga_ref, gb_ref):
    mod = mod_ref[0]
    u = (x_ref[...] * (1.0 + mod[1:2, :]) + mod[0:1, :]).astype(BF16)
    tm = u.shape[0]
    c64, s64, c32, s32 = c64_ref[...], s64_ref[...], c32_ref[...], s32_ref[...]
    scale = HEAD_DIM ** -0.5
    lane = lax.broadcasted_iota(I32, (tm, LANES), 1)
    low = lane < HEAD_DIM

    def proj(a, b):
        return _dot(u, w_ref[:, a:b])

    rope64 = lambda z: _rope(z, c64, s64, HEAD_DIM, HEAD_DIM // ROPE_FRACTION // 2)
    rope32 = lambda z: _rope(z, c32, s32, IDX_DIM, IDX_DIM // ROPE_FRACTION // 2)

    def head_slots64(z):
        out = []
        for h in range(A_HEADS):
            pair = z[:, (h // 2) * LANES:(h // 2 + 1) * LANES]
            g = h // REP
            src = pair if h % 2 == g else pltpu.roll(pair, HEAD_DIM, axis=1)
            out.append(jnp.where(low, src, 0.0) if g == 0 else jnp.where(low, 0.0, src))
        return jnp.concatenate(out, axis=1).astype(BF16)

    def head_slots32(z):
        per = LANES // IDX_DIM
        out = []
        for h in range(IDX_HEADS):
            col = z[:, (h // per) * LANES:(h // per + 1) * LANES]
            shift = IDX_DIM * (h % per)
            src = col if shift == 0 else pltpu.roll(col, LANES - shift, axis=1)
            out.append(jnp.where(lane < IDX_DIM, src, 0.0))
        return jnp.concatenate(out, axis=1).astype(BF16)

    def store_vt(ref, z, chunk):
        zt = z.T
        ones = jnp.ones((HEAD_DIM, chunk), F32)
        for g in range(A_KV_HEADS):
            for j in range(tm // chunk):
                blk = zt[g * HEAD_DIM:(g + 1) * HEAD_DIM, j * chunk:(j + 1) * chunk]
                ref[g, j] = jnp.concatenate([blk, ones], axis=0).astype(BF16)

    qa_ref[...] = head_slots64(rope64(proj(0, 512)) * scale)
    ka_ref[...] = rope64(proj(512, 640)).astype(BF16)
    store_vt(vaT_ref, proj(640, 768), tm)
    qi_ref[...] = head_slots32(rope32(proj(768, 1024)))
    ki_ref[...] = rope32(proj(1024, 1152)).astype(BF16)
    qb = proj(1152, 1664)
    qbraw_ref[...] = head_slots64(qb * scale)
    qbrot_ref[...] = head_slots64(rope64(qb) * scale)
    kcmp_ref[...] = proj(1664, 1792)
    vcmp_ref[...] = proj(1792, 1920)
    ksel_ref[...] = rope64(proj(1920, 2048)).astype(BF16)
    store_vt(vselT_ref, proj(2048, 2176), tm)
    kwin_ref[...] = rope64(proj(2176, 2304)).astype(BF16)
    store_vt(vwinT_ref, proj(2304, 2432), LANES)
    ga_ref[...] = _sigmoid(proj(2432, 3456)).astype(BF16)
    gb_ref[...] = _sigmoid(proj(3456, 4480)).astype(BF16)
    small = _dot_nt(wsm_ref[...], u)
    wT_ref[...] = small[0:IDX_HEADS, :]
    gT_ref[...] = _sigmoid(small[IDX_HEADS:IDX_HEADS + N_GATES, :])


def _pack_w_in(w_in):
    d = w_in.shape[0]
    col = lambda i: w_in[:, _IN_OFFS[i]:_IN_OFFS[i + 1]]
    ki = jnp.concatenate([col(4), jnp.zeros((d, LANES - IDX_DIM), w_in.dtype)], axis=1)
    parts = [col(0), col(1), col(2), col(3), ki, col(6), col(7), col(8), col(9), col(10),
             col(11), col(12), col(14), col(15)]
    w_small = jnp.concatenate([col(5), col(13)], axis=1).T
    return jnp.concatenate(parts, axis=1).astype(BF16), w_small.astype(BF16)


def _rope_tables(positions):
    pos = positions.astype(F32).reshape(-1, 1)

    def tab(dim):
        rot = dim // ROPE_FRACTION
        half = rot // 2
        inv = ROPE_THETA ** (-(jnp.arange(half, dtype=F32) * 2.0) / rot)
        ang = pos * inv
        cos, sin = jnp.cos(ang), jnp.sin(ang)
        ones = jnp.ones((pos.shape[0], dim - rot), F32)
        c = jnp.concatenate([cos, cos, ones], axis=1)
        s = jnp.concatenate([-sin, sin, 0.0 * ones], axis=1)
        return jnp.tile(c, (1, LANES // dim)), jnp.tile(s, (1, LANES // dim))

    return tab(HEAD_DIM) + tab(IDX_DIM)


def _in_proj(x2, mod3, w_pack, w_small, tabs, seq, tm):
    t, d = x2.shape
    n = w_pack.shape[1]
    per_b = seq // tm
    g = A_KV_HEADS
    row = lambda w: pl.BlockSpec((tm, w), lambda i: (i, 0))
    tok = lambda r: pl.BlockSpec((r, tm), lambda i: (0, i))
    vt_chunk = pl.BlockSpec((g, 1, LANES, tm), lambda i: (0, i, 0, 0))
    vt_lane = pl.BlockSpec((g, tm // LANES, LANES, LANES), lambda i: (0, i, 0, 0))
    sds = jax.ShapeDtypeStruct
    vt_chunk_shape = sds((g, t // tm, LANES, tm), BF16)
    outs = (("qa", sds((t, 1024), BF16), row(1024)), ("ka", sds((t, LANES), BF16), row(LANES)),
            ("vaT", vt_chunk_shape, vt_chunk), ("qi", sds((t, 1024), BF16), row(1024)),
            ("ki", sds((t, LANES), BF16), row(LANES)), ("wT", sds((IDX_HEADS, t), F32), tok(IDX_HEADS)),
            ("qbraw", sds((t, 1024), BF16), row(1024)), ("qbrot", sds((t, 1024), BF16), row(1024)),
            ("kcmp", sds((t, LANES), F32), row(LANES)), ("vcmp", sds((t, LANES), F32), row(LANES)),
            ("ksel", sds((t, LANES), BF16), row(LANES)), ("vselT", vt_chunk_shape, vt_chunk),
            ("kwin", sds((t, LANES), BF16), row(LANES)),
            ("vwinT", sds((g, t // LANES, LANES, LANES), BF16), vt_lane),
            ("gT", sds((N_GATES, t), F32), tok(N_GATES)),
            ("ga", sds((t, d), BF16), row(d)), ("gb", sds((t, d), BF16), row(d)))
    res = pl.pallas_call(
        _in_proj_kernel,
        out_shape=tuple(o[1] for o in outs),
        grid=(t // tm,),
        in_specs=[row(d),
                  pl.BlockSpec((1, 6, d), lambda i: (i // per_b, 0, 0)),
                  pl.BlockSpec((d, n), lambda i: (0, 0)),
                  pl.BlockSpec(w_small.shape, lambda i: (0, 0)),
                  row(LANES), row(LANES), row(LANES), row(LANES)],
        out_specs=tuple(o[2] for o in outs),
        compiler_params=_cparams(("parallel",)),
        name="in_proj",
    )(x2, mod3, w_pack, w_small, *tabs)
    return {o[0]: r for o, r in zip(outs, res)}


def _fold_rows(x, op):
    n = x.shape[0]
    while n % (2 * SUBLANES) == 0:
        n //= 2
        x = op(x[:n], x[n:])
    slabs = [x[i * SUBLANES:(i + 1) * SUBLANES] for i in range(n // SUBLANES)]
    while len(slabs) > 1:
        nxt = [op(slabs[i], slabs[i + 1]) for i in range(0, len(slabs) - 1, 2)]
        slabs = nxt + ([slabs[-1]] if len(slabs) % 2 else [])
    return slabs[0]


def _col_max(x):
    return jnp.max(_fold_rows(x, jnp.maximum), axis=0, keepdims=True)


def _col_sum(x):
    return jnp.sum(_fold_rows(x, jnp.add), axis=0, keepdims=True)


def _count_rows(mask01):
    return _fold_rows(mask01, jnp.add)


def _stack_heads(q_ref, heads):
    return jnp.concatenate([q_ref[0, :, h * LANES:(h + 1) * LANES] for h in heads], axis=0)


def _flash_step(k, q_stack, v_t, bias4, m, acc):
    s = _dot_nt(k, q_stack) + bias4
    m_new = jnp.maximum(m, _col_max(s))
    e = jnp.exp(s - m_new).astype(BF16)
    return m_new, acc * jnp.exp(m - m_new) + _dot(v_t, e)


HEADS_PER_CHAIN = REP


def _head_chains():
    return [(h // REP, tuple(range(h, h + HEADS_PER_CHAIN))) for h in range(0, A_HEADS, HEADS_PER_CHAIN)]


def _flash_loop(q_ref, k_ref, vT_ref, bias_of, n_chunks, tq, kc):
    chains = _head_chains()
    q_stacks = [_stack_heads(q_ref, heads) for _, heads in chains]
    width = HEADS_PER_CHAIN * tq

    def body(c, carry):
        k = k_ref[0, pl.ds(pl.multiple_of(c * kc, kc), kc), :]
        scores = [_dot_nt(k, q_stack) for q_stack in q_stacks]
        out = []
        for (g, _), s, (m, acc) in zip(chains, scores, carry):
            s = s + jnp.concatenate([bias_of(g, c)] * HEADS_PER_CHAIN, axis=1)
            m_new = jnp.maximum(m, _col_max(s))
            e = jnp.exp(s - m_new).astype(BF16)
            out.append((m_new, acc * jnp.exp(m - m_new) + _dot(vT_ref[g, c], e)))
        return tuple(out)

    init = tuple((jnp.full((1, width), NEG, F32), jnp.zeros((LANES, width), F32)) for _ in chains)
    res = lax.fori_loop(0, n_chunks, body, init)
    return [(heads, acc) for (_, heads), (_, acc) in zip(chains, res)]


def _normalise(acc):
    return acc / jnp.maximum(acc[HEAD_DIM:HEAD_DIM + 1, :], 1e-30)


def _store_heads(o_ref, o_t, heads, tq):
    o = o_t.T
    for r, h in enumerate(heads):
        o_ref[0, :, h * HEAD_DIM:(h + 1) * HEAD_DIM] = o[r * tq:(r + 1) * tq, 0:HEAD_DIM].astype(o_ref.dtype)


def _select_mask(keys, thr, need, offset, tri):
    gt = jnp.where(keys > thr, 1.0, 0.0)
    eq = jnp.where(keys == thr, 1.0, 0.0)
    prefix = _dot(tri, eq.astype(BF16)) + offset
    return gt + jnp.where(prefix < need, eq, 0.0), offset + _col_sum(eq)


def _dsa_kernel(qi_ref, wT_ref, ki_ref, qa_ref, ka_ref, vT_ref, o_ref, keys_ref, bias_ref,
                *, tq, kc, n_keep):
    t0 = pl.program_id(1) * tq
    n_chunks = (t0 + tq + kc - 1) // kc
    wT = wT_ref[...] * (IDX_HEADS ** -0.5 * IDX_DIM ** -0.5)
    qi_stack = _stack_heads(qi_ref, range(IDX_HEADS))
    keypos = lax.broadcasted_iota(I32, (kc, tq), 0)
    qpos = t0 + lax.broadcasted_iota(I32, (kc, tq), 1)

    def score_chunk(c, carry):
        k0 = pl.multiple_of(c * kc, kc)
        sc = _dot_nt(ki_ref[0, pl.ds(k0, kc), :], qi_stack)
        score = jnp.zeros((kc, tq), F32)
        for h in range(IDX_HEADS):
            score = score + wT[h:h + 1, :] * jnp.maximum(sc[:, h * tq:(h + 1) * tq], 0.0)
        keys_ref[c] = jnp.where(k0 + keypos <= qpos, _sort_key(score), INT_MIN)
        return carry

    lax.fori_loop(0, n_chunks, score_chunk, 0)

    def count_ge(cand):
        def body(c, acc):
            return acc + _count_rows(jnp.where(keys_ref[c] >= cand, 1.0, 0.0))
        acc = lax.fori_loop(0, n_chunks, body, jnp.zeros((SUBLANES, tq), F32))
        return jnp.sum(acc, axis=0, keepdims=True)

    thr = _kth_largest_key(count_ge, (1, tq), n_keep)
    need = float(n_keep) - count_ge(thr + 1)
    tri = _tri_strict_lower(LANES, BF16)
    sub = lax.broadcasted_iota(I32, (LANES, tq), 0)
    qsub = t0 + lax.broadcasted_iota(I32, (LANES, tq), 1)

    def bias_chunk(c, offset):
        for j in range(kc // LANES):
            rows = slice(j * LANES, (j + 1) * LANES)
            sel, offset = _select_mask(keys_ref[c, rows, :], thr, need, offset, tri)
            causal = c * kc + j * LANES + sub <= qsub
            bias_ref[c, rows, :] = jnp.where(causal, (sel - 1.0) * 1e30, NEG)
        return offset

    lax.fori_loop(0, n_chunks, bias_chunk, jnp.zeros((1, tq), F32))

    accs = _flash_loop(qa_ref, ka_ref, vT_ref, lambda g, c: bias_ref[c], n_chunks, tq, kc)
    for heads, acc in accs:
        _store_heads(o_ref, _normalise(acc), heads, tq)


def _dsa(qi, wT, ki, qa, ka, vaT, bsz, seq, tq, kc):
    n_keep = min(DSA_TOPK_MAX, seq // 4)
    nq = seq // tq
    qblk = lambda w: pl.BlockSpec((1, tq, w), lambda b, i: (b, i, 0))
    full = lambda w: pl.BlockSpec((1, seq, w), lambda b, i: (b, 0, 0))
    return pl.pallas_call(
        functools.partial(_dsa_kernel, tq=tq, kc=kc, n_keep=n_keep),
        out_shape=jax.ShapeDtypeStruct((bsz, seq, A_HEADS * HEAD_DIM), BF16),
        grid=(bsz, nq),
        in_specs=[qblk(1024), pl.BlockSpec((IDX_HEADS, tq), lambda b, i: (0, b * nq + i)), full(LANES),
                  qblk(1024), full(LANES),
                  pl.BlockSpec((A_KV_HEADS, seq // kc, LANES, kc), lambda b, i: (0, b, 0, 0))],
        out_specs=qblk(512),
        scratch_shapes=[pltpu.VMEM((seq // kc, kc, tq), I32), pltpu.VMEM((seq // kc, kc, tq), F32)],
        compiler_params=_cparams(("parallel", "parallel")),
        name="dsa",
    )(qi, wT, ki, qa, ka, vaT)


def _gelu_tanh(x):
    return 0.5 * x * (1.0 + jnp.tanh(math.sqrt(2.0 / math.pi) * (x + 0.044715 * (x * x * x))))


def _compress_kernel(k_ref, v_ref, pk_ref, pv_ref, w1k_ref, w2k_ref, w1v_ref, w2v_ref,
                     kc_ref, vcT_ref, *, n_rows):
    half = CMP_BLOCK // 2

    def one(x_ref, p_ref, w1_ref, w2_ref):
        outs = []
        for g in range(B_KV_HEADS):
            lo = jnp.zeros((n_rows, CMP_HIDDEN), F32)
            hi = jnp.zeros((n_rows, CMP_HIDDEN), F32)
            for l in range(half):
                xl = x_ref[0, :, l * LANES + g * HEAD_DIM:l * LANES + (g + 1) * HEAD_DIM]
                a = (xl + p_ref[l:l + 1, :]).astype(BF16)
                b = (xl + p_ref[half + l:half + l + 1, :]).astype(BF16)
                lo = lo + _dot(a, w1_ref[l * HEAD_DIM:(l + 1) * HEAD_DIM, :].astype(BF16))
                hi = hi + _dot(b, w1_ref[(half + l) * HEAD_DIM:(half + l + 1) * HEAD_DIM, :].astype(BF16))
            hid = lo + pltpu.roll(hi, n_rows - 1, axis=0)
            outs.append(_dot(_gelu_tanh(hid).astype(BF16), w2_ref[...].astype(BF16)))
        return outs

    k0, k1 = one(k_ref, pk_ref, w1k_ref, w2k_ref)
    kc_ref[0] = jnp.concatenate([k0, k1], axis=1).astype(kc_ref.dtype)
    for g, v in enumerate(one(v_ref, pv_ref, w1v_ref, w2v_ref)):
        vcT_ref[0, g] = jnp.concatenate([v, jnp.zeros_like(v)], axis=1).T.astype(vcT_ref.dtype)


def _compress(kcmp, vcmp, pos_k, pos_v, w1k, w2k, w1v, w2v):
    bsz, n_rows, width = kcmp.shape
    xspec = pl.BlockSpec((1, n_rows, width), lambda b: (b, 0, 0))
    cst = lambda a: pl.BlockSpec(a.shape, lambda b: (0,) * a.ndim)
    return pl.pallas_call(
        functools.partial(_compress_kernel, n_rows=n_rows),
        out_shape=(jax.ShapeDtypeStruct((bsz, n_rows, LANES), BF16),
                   jax.ShapeDtypeStruct((bsz, B_KV_HEADS, LANES, n_rows), BF16)),
        grid=(bsz,),
        in_specs=[xspec, xspec, cst(pos_k), cst(pos_v), cst(w1k), cst(w2k), cst(w1v), cst(w2v)],
        out_specs=(pl.BlockSpec((1, n_rows, LANES), lambda b: (b, 0, 0)),
                   pl.BlockSpec((1, B_KV_HEADS, LANES, n_rows), lambda b: (b, 0, 0, 0))),
        compiler_params=_cparams(("parallel",)),
        name="compress",
    )(kcmp, vcmp, pos_k, pos_v, w1k, w2k, w1v, w2v)


def _split3(x):
    a = x.astype(BF16)
    r = x - a.astype(F32)
    b = r.astype(BF16)
    c = (r - b.astype(F32)).astype(BF16)
    return a, b, c


def _nsa_kernel(qraw_ref, qrot_ref, kc_ref, vcT_ref, ksel_ref, vselT_ref, kwin_ref, vwinT_ref,
                gT_ref, exp_ref, o_ref, bias_ref, *, tq, kc, seq, n_c):
    t0 = pl.program_id(1) * tq
    n_chunks = (t0 + tq + kc - 1) // kc
    n_s = seq // SEL_BLOCK
    n_pick = min(SEL_COUNT, n_s)
    span = WINDOW + tq
    gT = gT_ref[...]

    ridx = lax.broadcasted_iota(I32, (LANES, tq), 0)
    tl = t0 + lax.broadcasted_iota(I32, (LANES, tq), 1)
    valid_c = jnp.where((ridx * CMP_STRIDE + CMP_BLOCK - 1 <= tl) & (ridx < n_c), 1.0, 0.0)
    valid4 = jnp.concatenate([valid_c] * REP, axis=1) > 0.0
    js = lax.broadcasted_iota(I32, (LANES, LANES), 0) * SEL_BLOCK
    cs = lax.broadcasted_iota(I32, (LANES, LANES), 1) * CMP_STRIDE
    overlap_t = jnp.where((cs <= js + SEL_BLOCK - 1) & (cs + CMP_BLOCK - 1 >= js), 1.0, 0.0).astype(BF16)
    cur = tl // SEL_BLOCK
    forced = (ridx == 0) | ((cur - ridx >= 0) & (cur - ridx < SEL_LOCAL))
    blk_causal = ridx * SEL_BLOCK <= tl
    tri = _tri_strict_lower(LANES, BF16)
    keypos = lax.broadcasted_iota(I32, (kc, tq), 0)
    qpos = t0 + lax.broadcasted_iota(I32, (kc, tq), 1)
    w0 = pl.multiple_of(jnp.clip(t0 - WINDOW, 0, seq - span), tq)
    wdiff = (t0 + lax.broadcasted_iota(I32, (span, tq), 1)) - (w0 + lax.broadcasted_iota(I32, (span, tq), 0))
    wbias = jnp.where((wdiff >= 0) & (wdiff < WINDOW), 0.0, NEG)
    wbias_c = jnp.concatenate([wbias] * HEADS_PER_CHAIN, axis=1)
    wblk = w0 // LANES

    o_cmp_g = []
    for g in range(B_KV_HEADS):
        heads = range(g * REP, (g + 1) * REP)
        s = jnp.where(valid4, _dot_nt(kc_ref[0], _stack_heads(qraw_ref, heads)), NEG)
        e = jnp.where(valid4, jnp.exp(s - _col_max(s)), 0.0)
        p = e / jnp.maximum(_col_sum(e), 1e-30)
        o_cmp_g.append(_dot(vcT_ref[0, g], p.astype(BF16)))
        p_sum = p[:, 0:tq]
        for r in range(1, REP):
            p_sum = p_sum + p[:, r * tq:(r + 1) * tq]
        pa, pb, pc = _split3(p_sum)
        imp = _dot(overlap_t, pa) + _dot(overlap_t, pb) + _dot(overlap_t, pc)
        imp = jnp.where(forced, FORCE, jnp.where(blk_causal, imp, NEG))
        keys = _sort_key(imp[0:n_s, :])
        count_ge = lambda cand, keys=keys: _col_sum(jnp.where(keys >= cand, 1.0, 0.0))
        thr = _kth_largest_key(count_ge, (1, tq), n_pick)
        need = float(n_pick) - _col_sum(jnp.where(keys > thr, 1.0, 0.0))
        blk_sel, _ = _select_mask(keys, thr, need, jnp.zeros((1, tq), F32), tri[0:n_s, 0:n_s])
        blk_sel = jnp.concatenate([blk_sel, jnp.zeros((LANES - n_s, tq), F32)], axis=0).astype(BF16)

        def bias_chunk(c, carry, blk_sel=blk_sel, g=g):
            tok_sel = _dot(exp_ref[c], blk_sel)
            bias_ref[g, c] = jnp.where(c * kc + keypos <= qpos, (tok_sel - 1.0) * 1e30, NEG)
            return carry

        lax.fori_loop(0, n_chunks, bias_chunk, 0)

    sel = _flash_loop(qrot_ref, ksel_ref, vselT_ref, lambda g, c: bias_ref[g, c], n_chunks, tq, kc)

    kwin = kwin_ref[0, pl.ds(w0, span), :]
    for (g, heads), (_, acc_sel) in zip(_head_chains(), sel):
        sw = _dot_nt(kwin, _stack_heads(qrot_ref, heads)) + wbias_c
        ew = jnp.exp(sw - _col_max(sw)).astype(BF16)
        acc_win = _dot(vwinT_ref[g, wblk], ew[0:LANES, :])
        for j in range(1, span // LANES):
            acc_win = acc_win + _dot(vwinT_ref[g, wblk + j], ew[j * LANES:(j + 1) * LANES, :])

        o_sel = _normalise(acc_sel)
        o_win = _normalise(acc_win)
        cols = []
        for r, h in enumerate(heads):
            col = slice(r * tq, (r + 1) * tq)
            cmp_col = slice((h % REP) * tq, (h % REP + 1) * tq)
            cols.append(gT[3 * h:3 * h + 1, :] * o_cmp_g[g][:, cmp_col]
                        + gT[3 * h + 1:3 * h + 2, :] * o_sel[:, col]
                        + gT[3 * h + 2:3 * h + 3, :] * o_win[:, col])
        _store_heads(o_ref, jnp.concatenate(cols, axis=1), heads, tq)


def _nsa(qraw, qrot, kc_x, vcT, ksel, vselT, kwin, vwinT, gT, bsz, seq, tq, kc):
    n_c = (seq - CMP_BLOCK) // CMP_STRIDE + 1
    nq = seq // tq
    key_blk = (jnp.arange(seq, dtype=I32) // SEL_BLOCK).reshape(seq // kc, kc, 1)
    expand = (key_blk == jnp.arange(LANES, dtype=I32).reshape(1, 1, LANES)).astype(BF16)
    qblk = lambda w: pl.BlockSpec((1, tq, w), lambda b, i: (b, i, 0))
    full = lambda w: pl.BlockSpec((1, seq, w), lambda b, i: (b, 0, 0))
    g = B_KV_HEADS
    return pl.pallas_call(
        functools.partial(_nsa_kernel, tq=tq, kc=kc, seq=seq, n_c=n_c),
        out_shape=jax.ShapeDtypeStruct((bsz, seq, B_HEADS * HEAD_DIM), BF16),
        grid=(bsz, nq),
        in_specs=[qblk(1024), qblk(1024),
                  pl.BlockSpec((1,) + kc_x.shape[1:], lambda b, i: (b, 0, 0)),
                  pl.BlockSpec((1,) + vcT.shape[1:], lambda b, i: (b, 0, 0, 0)),
                  full(LANES), pl.BlockSpec((g, seq // kc, LANES, kc), lambda b, i: (0, b, 0, 0)),
                  full(LANES), pl.BlockSpec((g, seq // LANES, LANES, LANES), lambda b, i: (0, b, 0, 0)),
                  pl.BlockSpec((N_GATES, tq), lambda b, i: (0, b * nq + i)),
                  pl.BlockSpec(expand.shape, lambda b, i: (0, 0, 0))],
        out_specs=qblk(512),
        scratch_shapes=[pltpu.VMEM((g, seq // kc, kc, tq), F32)],
        compiler_params=_cparams(("parallel", "parallel")),
        name="nsa",
    )(qraw, qrot, kc_x, vcT, ksel, vselT, kwin, vwinT, gT, expand)


def _pack_pairs(x):
    n = x.shape[1] // 2
    lo = pltpu.bitcast(x[:, :n].astype(BF16).astype(F32), I32)
    hi = pltpu.bitcast(x[:, n:].astype(BF16).astype(F32), I32)
    return lax.shift_right_logical(lo, 16) | (hi & jnp.int32(-65536))


def _unpack_pairs(p):
    lo = pltpu.bitcast(lax.shift_left(p, 16), F32)
    hi = pltpu.bitcast(p & jnp.int32(-65536), F32)
    return jnp.concatenate([lo, hi], axis=1)


def _layer_norm(y, g, b):
    mu = jnp.mean(y, axis=1, keepdims=True)
    yc = y - mu
    var = jnp.mean(yc * yc, axis=1, keepdims=True)
    return yc * lax.rsqrt(var + LN_EPS) * g + b


def _out_proj_kernel(oa_ref, ob_ref, ga_ref, gb_ref, x_ref, mod_ref, wa_ref, wb_ref, wo_ref,
                     g1_ref, b1_ref, wrh_ref, wrl_ref, x1_ref, u2_ref, lg_ref):
    mod = mod_ref[0]
    merged = (ga_ref[...].astype(F32) * _dot(oa_ref[...], wa_ref[...])
              + gb_ref[...].astype(F32) * _dot(ob_ref[...], wb_ref[...]))
    mix = _dot(merged.astype(BF16), wo_ref[...])
    x1 = _layer_norm(DN_ALPHA * x_ref[...] + mod[2:3, :] * mix, g1_ref[...], b1_ref[...])
    x1_ref[...] = x1
    u2 = x1 * (1.0 + mod[4:5, :]) + mod[3:4, :]
    packed = _pack_pairs(u2)
    for j in range(PIECES):
        u2_ref[j] = packed[:, j * SC_ROW:(j + 1) * SC_ROW]
    uh = u2.astype(BF16)
    ul = (u2 - uh.astype(F32)).astype(BF16)
    lg_ref[...] = _dot_nt(wrh_ref[...], uh) + _dot_nt(wrh_ref[...], ul) + _dot_nt(wrl_ref[...], uh)


def _out_proj(oa, ob, ga, gb, x2, mod3, wa, wb, wo, g1, b1, wrh, wrl, seq, tm):
    t, d = x2.shape
    per_b = seq // tm
    row = lambda w: pl.BlockSpec((tm, w), lambda i: (i, 0))
    cst = lambda a: pl.BlockSpec(a.shape, lambda i: (0,) * a.ndim)
    return pl.pallas_call(
        _out_proj_kernel,
        out_shape=(jax.ShapeDtypeStruct((t, d), F32), jax.ShapeDtypeStruct((PIECES, t, SC_ROW), I32),
                   jax.ShapeDtypeStruct((N_EXPERTS, t), F32)),
        grid=(t // tm,),
        in_specs=[row(512), row(512), row(d), row(d), row(d),
                  pl.BlockSpec((1, 6, d), lambda i: (i // per_b, 0, 0)),
                  cst(wa), cst(wb), cst(wo), cst(g1), cst(b1), cst(wrh), cst(wrl)],
        out_specs=(row(d), pl.BlockSpec((PIECES, tm, SC_ROW), lambda i: (0, i, 0)),
                   pl.BlockSpec((N_EXPERTS, tm), lambda i: (0, i))),
        compiler_params=_cparams(("parallel",)),
        name="out_proj",
    )(oa, ob, ga, gb, x2, mod3, wa, wb, wo, g1, b1, wrh, wrl)


def _first_max(x, rows):
    m = jnp.max(x, axis=0, keepdims=True)
    idx = jnp.min(jnp.where(x == m, rows, 1e9), axis=0, keepdims=True)
    return m, idx


def _router_kernel(lg_ref, rb_ref, idx_ref, w_ref, pos_ref, cnt_ref, carry_ref, *, tm):
    @pl.when(pl.program_id(0) == 0)
    def _():
        carry_ref[...] = jnp.zeros_like(carry_ref)

    per_g = N_EXPERTS // N_GROUPS
    scores = _sigmoid(lg_ref[...])
    choice = scores + rb_ref[...][:, 0:1]
    rows = lax.broadcasted_iota(I32, (N_EXPERTS, tm), 0).astype(F32)
    rows_g = lax.broadcasted_iota(I32, (per_g, tm), 0).astype(F32)
    ninf = -jnp.inf

    gs = []
    for g in range(N_GROUPS):
        x = choice[g * per_g:(g + 1) * per_g, :]
        m1, i1 = _first_max(x, rows_g)
        m2 = jnp.max(jnp.where(rows_g == i1, ninf, x), axis=0, keepdims=True)
        gs.append(m1 + m2)
    gscore = jnp.concatenate(gs, axis=0)
    rows8 = lax.broadcasted_iota(I32, (N_GROUPS, tm), 0).astype(F32)
    keep = jnp.zeros((N_GROUPS, tm), F32)
    for _ in range(TOPK_GROUPS):
        _, gi = _first_max(gscore, rows8)
        hit = rows8 == gi
        keep = jnp.where(hit, 1.0, keep)
        gscore = jnp.where(hit, ninf, gscore)
    keep_full = jnp.concatenate(
        [jnp.broadcast_to(keep[g:g + 1, :], (per_g, tm)) for g in range(N_GROUPS)], axis=0)
    masked = jnp.where(keep_full > 0.0, choice, NEG)

    idxs, ws = [], []
    onehot = jnp.zeros((N_EXPERTS, tm), F32)
    for _ in range(TOP_K):
        _, ei = _first_max(masked, rows)
        hit = rows == ei
        idxs.append(ei)
        ws.append(jnp.sum(jnp.where(hit, scores, 0.0), axis=0, keepdims=True))
        masked = jnp.where(hit, ninf, masked)
        onehot = jnp.where(hit, 1.0, onehot)
    idx = jnp.concatenate(idxs, axis=0)
    w = jnp.concatenate(ws, axis=0)
    idx_ref[...] = idx.astype(I32)
    w_ref[...] = w / jnp.sum(w, axis=0, keepdims=True) * ROUTED_SCALE

    tri = _tri_strict_upper(tm, BF16)
    base = _dot(onehot.astype(BF16), tri) + carry_ref[...][:, 0:1]
    pos = [jnp.sum(jnp.where(rows == idxs[k], base, 0.0), axis=0, keepdims=True) for k in range(TOP_K)]
    pos_ref[...] = jnp.concatenate(pos, axis=0).astype(I32)
    carry = carry_ref[...] + jnp.sum(onehot, axis=1, keepdims=True)
    carry_ref[...] = carry
    cnt_ref[...] = carry


def _router(lg, router_bias, tm):
    e, t = lg.shape
    rb = jnp.broadcast_to(router_bias.reshape(e, 1).astype(F32), (e, LANES))
    tok = lambda r: pl.BlockSpec((r, tm), lambda i: (0, i))
    return pl.pallas_call(
        functools.partial(_router_kernel, tm=tm),
        out_shape=(jax.ShapeDtypeStruct((TOP_K, t), I32), jax.ShapeDtypeStruct((TOP_K, t), F32),
                   jax.ShapeDtypeStruct((TOP_K, t), I32), jax.ShapeDtypeStruct((e, LANES), F32)),
        grid=(t // tm,),
        in_specs=[tok(e), pl.BlockSpec((e, LANES), lambda i: (0, 0))],
        out_specs=(tok(TOP_K), tok(TOP_K), tok(TOP_K), pl.BlockSpec((e, LANES), lambda i: (0, 0))),
        scratch_shapes=[pltpu.VMEM((e, LANES), F32)],
        compiler_params=_cparams(("arbitrary",)),
        name="router",
    )(lg, rb)


def _dest_kernel(idx_ref, pos_ref, st_ref, dest_ref, *, tm):
    rows = lax.broadcasted_iota(I32, (N_EXPERTS, tm), 0)
    starts = st_ref[...][:, 0:1]
    idx = idx_ref[...]
    out = []
    for k in range(TOP_K):
        out.append(jnp.sum(jnp.where(rows == idx[k:k + 1, :], starts, 0.0), axis=0, keepdims=True))
    dest_ref[...] = jnp.concatenate(out, axis=0).astype(I32) + pos_ref[...]


def _dest(idx, pos, starts, tm):
    k, t = idx.shape
    st = jnp.broadcast_to(starts.reshape(N_EXPERTS, 1).astype(F32), (N_EXPERTS, LANES))
    tok = pl.BlockSpec((k, tm), lambda i: (0, i))
    return pl.pallas_call(
        functools.partial(_dest_kernel, tm=tm),
        out_shape=jax.ShapeDtypeStruct((k, t), I32),
        grid=(t // tm,),
        in_specs=[tok, tok, pl.BlockSpec((N_EXPERTS, LANES), lambda i: (0, 0))],
        out_specs=tok,
        compiler_params=_cparams(("parallel",)),
        name="dest",
    )(idx, pos, st)


def _experts_kernel(blk_ref, used_ref, first_ref, slot_ref, next_ref, x_ref, wg_hbm, wu_hbm, wd_hbm, y_ref,
                    wg_buf, wu_buf, wd_buf, wg_bf, wu_bf, wd_bf, sems):
    b = pl.program_id(0)
    active = b < used_ref[0]

    def weight_copies(e, slot):
        return [pltpu.make_async_copy(hbm.at[e], buf.at[slot], sems.at[slot, i])
                for i, (hbm, buf) in enumerate(((wg_hbm, wg_buf), (wu_hbm, wu_buf), (wd_hbm, wd_buf)))]

    @pl.when(b == 0)
    def _():
        for cp in weight_copies(blk_ref[0], 0):
            cp.start()

    @pl.when(active & (first_ref[b] == 1))
    def _():
        slot = slot_ref[b]
        for cp in weight_copies(blk_ref[b], slot):
            cp.wait()
        nxt = next_ref[b]

        @pl.when(nxt >= 0)
        def _():
            for cp in weight_copies(nxt, 1 - slot):
                cp.start()

        wg_bf[...] = wg_buf[slot].astype(BF16)
        wu_bf[...] = wu_buf[slot].astype(BF16)
        wd_bf[...] = wd_buf[slot].astype(BF16)

    @pl.when(active)
    def _():
        x = jnp.concatenate([x_ref[j] for j in range(PIECES)], axis=1)
        x = _unpack_pairs(x).astype(BF16)
        a = _dot(x, wg_bf[...])
        u = _dot(x, wu_bf[...])
        h = (a * _sigmoid(a) * u).astype(BF16)
        y = _pack_pairs(_dot(h, wd_bf[...]))
        for j in range(PIECES):
            y_ref[j] = y[:, j * SC_ROW:(j + 1) * SC_ROW]


def _experts(xs, plan, wg, wu, wd, bm):
    _, cap, _ = xs.shape
    n_blocks = cap // bm
    d, f = wg.shape[1], wg.shape[2]
    rows = pl.BlockSpec((PIECES, bm, SC_ROW), lambda b, blk, used, *_: (0, jnp.minimum(b, used[0] - 1), 0))
    hbm = pl.BlockSpec(memory_space=pl.ANY)
    return pl.pallas_call(
        _experts_kernel,
        out_shape=jax.ShapeDtypeStruct(xs.shape, I32),
        grid_spec=pltpu.PrefetchScalarGridSpec(
            num_scalar_prefetch=5,
            grid=(n_blocks,),
            in_specs=[rows, hbm, hbm, hbm],
            out_specs=rows,
            scratch_shapes=[pltpu.VMEM((2, d, f), F32), pltpu.VMEM((2, d, f), F32), pltpu.VMEM((2, f, d), F32),
                            pltpu.VMEM((d, f), BF16), pltpu.VMEM((d, f), BF16), pltpu.VMEM((f, d), BF16),
                            pltpu.SemaphoreType.DMA((2, 3))]),
        compiler_params=_cparams(("arbitrary",)),
        name="experts",
    )(plan["blk_e"], plan["n_used"], plan["first"], plan["slot"], plan["next_e"], xs, wg, wu, wd)


def _final_kernel(x1_ref, u2_ref, yg_ref, w_ref, mod_ref, sg_ref, su_ref, sd_ref, g2_ref, b2_ref, o_ref):
    mod = mod_ref[0]
    w = w_ref[...]
    rows = lambda ref, *lead: jnp.concatenate([ref[(j,) + lead] for j in range(PIECES)], axis=1)
    routed = w[:, 0:1] * _unpack_pairs(rows(yg_ref, 0))
    for k in range(1, TOP_K):
        routed = routed + w[:, k:k + 1] * _unpack_pairs(rows(yg_ref, k))
    u = _unpack_pairs(rows(u2_ref)).astype(BF16)
    a = _dot(u, sg_ref[...])
    b = _dot(u, su_ref[...])
    shared = _dot((a * _sigmoid(a) * b).astype(BF16), sd_ref[...])
    y = DN_ALPHA * x1_ref[...] + mod[5:6, :] * (routed + shared)
    o_ref[...] = _layer_norm(y, g2_ref[...], b2_ref[...])


def _final(x1, u2p, yg, wtok, mod3, sg, su, sd, g2, b2, seq, tm):
    t, d = x1.shape
    per_b = seq // tm
    row = lambda w: pl.BlockSpec((tm, w), lambda i: (i, 0))
    cst = lambda a: pl.BlockSpec(a.shape, lambda i: (0,) * a.ndim)
    return pl.pallas_call(
        _final_kernel,
        out_shape=jax.ShapeDtypeStruct((t, d), F32),
        grid=(t // tm,),
        in_specs=[row(d), pl.BlockSpec((PIECES, tm, SC_ROW), lambda i: (0, i, 0)),
                  pl.BlockSpec((PIECES, TOP_K, tm, SC_ROW), lambda i: (0, 0, i, 0)),
                  row(LANES), pl.BlockSpec((1, 6, d), lambda i: (i // per_b, 0, 0)),
                  cst(sg), cst(su), cst(sd), cst(g2), cst(b2)],
        out_specs=row(d),
        compiler_params=_cparams(("parallel",)),
        name="final",
    )(x1, u2p, yg, wtok, mod3, sg, su, sd, g2, b2)


def _sc_mesh():
    return plsc.VectorSubcoreMesh(core_axis_name="core", subcore_axis_name="subcore")


def _sc_scatter_rows(src, idx, n_out, src_block):
    n_idx = idx.shape[0]

    @functools.partial(pl.kernel, out_type=jax.ShapeDtypeStruct((n_out, SC_ROW), src.dtype),
                       mesh=_sc_mesh(), scratch_types=[])
    def k(x_hbm, i_hbm, o_hbm):
        def body(x_vmem, i_vmem):
            pltpu.sync_copy(x_vmem, o_hbm.at[i_vmem.at[0]])

        pltpu.emit_pipeline(
            body, grid=(n_idx // SC_WINDOW,),
            in_specs=[pl.BlockSpec((SC_WINDOW, SC_ROW), lambda i: (src_block(i), 0)),
                      pl.BlockSpec((1, SC_WINDOW), lambda i: (0, i))],
            out_specs=[],
            core_axis_name=("core", "subcore"),
            dimension_semantics=(pltpu.PARALLEL,),
        )(x_hbm, i_hbm)

    return k(src, idx.reshape(1, n_idx))


def _sc_gather_rows(src, idx):
    n_idx = idx.shape[0]

    @functools.partial(pl.kernel, out_type=jax.ShapeDtypeStruct((n_idx, SC_ROW), src.dtype),
                       mesh=_sc_mesh(), scratch_types=[])
    def k(x_hbm, i_hbm, o_hbm):
        def body(i_vmem, o_vmem):
            pltpu.sync_copy(x_hbm.at[i_vmem.at[0]], o_vmem)

        pltpu.emit_pipeline(
            body, grid=(n_idx // SC_WINDOW,),
            in_specs=[pl.BlockSpec((1, SC_WINDOW), lambda i: (0, i))],
            out_specs=[pl.BlockSpec((SC_WINDOW, SC_ROW), lambda i: (i, 0))],
            core_axis_name=("core", "subcore"),
            dimension_semantics=(pltpu.PARALLEL,),
        )(i_hbm, o_hbm)

    return k(src, idx.reshape(1, n_idx))


def _moe_plan(counts, n_tok):
    bm = BM_EXPERT
    padded = (counts + bm - 1) // bm * bm
    p_ends = jnp.cumsum(padded)
    starts = p_ends - padded
    n_blocks = n_tok * TOP_K // bm + N_EXPERTS
    blk = jnp.arange(n_blocks, dtype=I32)
    blk_e = jnp.minimum(jnp.sum(p_ends[None, :] <= (blk * bm)[:, None], axis=1), N_EXPERTS - 1).astype(I32)
    n_used = (p_ends[-1] // bm).astype(I32)
    prev = jnp.concatenate([jnp.full((1,), -1, I32), blk_e[:-1]])
    first = ((blk_e != prev) & (blk < n_used)).astype(I32)
    slot = (jnp.cumsum(first) - 1) % 2
    first_pos = jnp.where(first == 1, blk, n_blocks)
    next_first = lax.cummin(jnp.concatenate([first_pos[1:], jnp.full((1,), n_blocks, I32)]), reverse=True)
    next_e = jnp.where(next_first < n_blocks, blk_e[jnp.minimum(next_first, n_blocks - 1)], -1)
    plan = dict(blk_e=blk_e, n_used=n_used.reshape(1), first=first, slot=slot.astype(I32),
                next_e=next_e.astype(I32))
    return starts, plan, n_blocks


def _layer(x, mod, positions, w_in, w_br_a, w_br_b, w_out, cmp_pos_k, cmp_pos_v, cmp_k_w1, cmp_k_w2,
           cmp_v_w1, cmp_v_w2, ln1_g, ln1_b, w_router, router_bias, w_exp_gate, w_exp_up, w_exp_down,
           w_sh_gate, w_sh_up, w_sh_down, ln2_g, ln2_b):
    bsz, seq, d = x.shape
    t = bsz * seq
    assert seq // CMP_STRIDE == LANES and seq % TQ_ATTN == 0 and seq % TM_PROJ == 0 and KC_ATTN == TM_PROJ
    x2 = x.reshape(t, d)
    mod3 = mod.reshape(bsz, 6, d)

    w_pack, w_small = _pack_w_in(w_in)
    z = _in_proj(x2, mod3, w_pack, w_small, _rope_tables(positions), seq, TM_PROJ)
    per_b = lambda name: z[name].reshape(bsz, seq, z[name].shape[1])

    o_a = _dsa(per_b("qi"), z["wT"], per_b("ki"), per_b("qa"), per_b("ka"), z["vaT"],
               bsz, seq, TQ_ATTN, KC_ATTN)

    n_rows = seq // CMP_STRIDE
    kc_x, vcT = _compress(z["kcmp"].reshape(bsz, n_rows, CMP_STRIDE * LANES),
                          z["vcmp"].reshape(bsz, n_rows, CMP_STRIDE * LANES),
                          cmp_pos_k, cmp_pos_v, cmp_k_w1, cmp_k_w2, cmp_v_w1, cmp_v_w2)
    o_b = _nsa(per_b("qbraw"), per_b("qbrot"), kc_x, vcT, per_b("ksel"), z["vselT"], per_b("kwin"),
               z["vwinT"], z["gT"], bsz, seq, TQ_ATTN, KC_ATTN)

    wr_hi = w_router.T.astype(BF16)
    wr_lo = (w_router.T - wr_hi.astype(F32)).astype(BF16)
    x1, u2p, logits = _out_proj(
        o_a.reshape(t, -1), o_b.reshape(t, -1), z["ga"], z["gb"], x2, mod3,
        w_br_a.astype(BF16), w_br_b.astype(BF16), w_out.astype(BF16),
        ln1_g.reshape(1, d), ln1_b.reshape(1, d), wr_hi, wr_lo, seq, TM_PROJ)

    idx, wts, pos, counts = _router(logits, router_bias, TM_ROUTE)
    starts, plan, n_blocks = _moe_plan(counts[:, 0].astype(I32), t)
    dest = _dest(idx, pos, starts, TM_ROUTE)

    cap = n_blocks * BM_EXPERT
    dest_p = (dest[None] + (jnp.arange(PIECES, dtype=I32) * cap).reshape(PIECES, 1, 1)).reshape(-1)
    tb = t // SC_WINDOW
    xs = _sc_scatter_rows(u2p.reshape(PIECES * t, SC_ROW), dest_p, cap * PIECES,
                          lambda i: (i // (TOP_K * tb)) * tb + i % tb)
    ys = _experts(xs.reshape(PIECES, cap, SC_ROW), plan, w_exp_gate, w_exp_up, w_exp_down, BM_EXPERT)
    yg = _sc_gather_rows(ys.reshape(cap * PIECES, SC_ROW), dest_p).reshape(PIECES, TOP_K, t, SC_ROW)

    wtok = jnp.pad(wts.T, ((0, 0), (0, LANES - TOP_K)))
    return _final(x1, u2p, yg, wtok, mod3, w_sh_gate.astype(BF16), w_sh_up.astype(BF16),
                  w_sh_down.astype(BF16), ln2_g.reshape(1, d), ln2_b.reshape(1, d), seq, TM_PROJ
                  ).reshape(bsz, seq, d)


def kernel(x, c, positions, w_ada, b_ada, w_in, w_br_a, w_br_b, w_out, cmp_pos_k, cmp_pos_v, cmp_k_w1,
           cmp_k_w2, cmp_v_w1, cmp_v_w2, ln1_g, ln1_b, w_router, router_bias, w_exp_gate, w_exp_up,
           w_exp_down, w_sh_gate, w_sh_up, w_sh_down, ln2_g, ln2_b):
    for l in range(w_ada.shape[0]):
        mod = _mod(c, w_ada[l], b_ada[l])
        x = _layer(x, mod, positions, w_in[l], w_br_a[l], w_br_b[l], w_out[l], cmp_pos_k[l], cmp_pos_v[l],
                   cmp_k_w1[l], cmp_k_w2[l], cmp_v_w1[l], cmp_v_w2[l], ln1_g[l], ln1_b[l], w_router[l],
                   router_bias[l], w_exp_gate[l], w_exp_up[l], w_exp_down[l], w_sh_gate[l], w_sh_up[l],
                   w_sh_down[l], ln2_g[l], ln2_b[l])
    return x
```

```python
import functools
import math

import jax
import jax.numpy as jnp
import numpy as np
from jax import lax
from jax.experimental import pallas as pl
from jax.experimental.pallas import tpu as pltpu
from jax.experimental.pallas import tpu_sc as plsc

F32 = jnp.float32
BF16 = jnp.bfloat16
I32 = jnp.int32

D_MODEL = 1024
HEAD_DIM = 64
ROPE_THETA = 500000.0
ROPE_FRACTION = 4
A_HEADS = 8
A_KV_HEADS = 2
IDX_HEADS = 8
IDX_DIM = 32
DSA_TOPK_MAX = 256
B_HEADS = 8
B_KV_HEADS = 2
REP = 4
CMP_BLOCK = 32
CMP_STRIDE = 16
CMP_HIDDEN = 256
SEL_BLOCK = 64
SEL_COUNT = 16
SEL_LOCAL = 2
WINDOW = 512
N_EXPERTS = 256
TOP_K = 8
N_GROUPS = 8
TOPK_GROUPS = 4
ROUTED_SCALE = 2.5
DEPTH = 1
DN_ALPHA = (2 * DEPTH) ** 0.25
LN_EPS = 1e-5
NEG = -1e30
FORCE = 1e9
INT_MIN = -2147483648
N_GATES = 3 * B_HEADS

LANES = 128
SUBLANES = 8
VMEM_LIMIT = 56 * 1024 * 1024
SC_WINDOW = 128
SC_ROW = 256
PIECES = (D_MODEL // 2) // SC_ROW

TM_PROJ = 512
TQ_DSA = 256
TQ_NSA = 512
KC_ATTN = 512
TM_ROUTE = 512
BM_EXPERT = 512

_IN_WIDTHS = (512, 128, 128, 256, 32, 8, 512, 128, 128, 128, 128, 128, 128, 24, 1024, 1024)
_IN_OFFS = np.concatenate([[0], np.cumsum(_IN_WIDTHS)]).tolist()

NT_DIMS = (((1,), (1,)), ((), ()))


def _cparams(sem):
    return pltpu.CompilerParams(dimension_semantics=sem, vmem_limit_bytes=VMEM_LIMIT)


def _sigmoid(x):
    return 1.0 / (1.0 + jnp.exp(-x))


def _dot(a, b):
    return jnp.dot(a, b, preferred_element_type=F32)


def _dot_nt(a, b):
    return lax.dot_general(a, b, NT_DIMS, preferred_element_type=F32)


def _sort_key(x):
    x = jnp.where(x == 0.0, 0.0, x)
    bits = pltpu.bitcast(x, I32)
    return jnp.where(bits < 0, bits ^ 0x7FFFFFFF, bits)


def _kth_largest_key(count_ge, shape, k):
    kf = float(k)
    t0 = jnp.where(count_ge(jnp.zeros(shape, I32)) >= kf, 0, INT_MIN).astype(I32)

    def body(it, t):
        cand = t + jnp.left_shift(jnp.int32(1), 30 - it)
        return jnp.where(count_ge(cand) >= kf, cand, t)

    return lax.fori_loop(0, 31, body, t0)


def _tri_strict_lower(n, dtype):
    r = lax.broadcasted_iota(I32, (n, n), 0)
    c = lax.broadcasted_iota(I32, (n, n), 1)
    return jnp.where(c < r, 1.0, 0.0).astype(dtype)


def _tri_strict_upper(n, dtype):
    r = lax.broadcasted_iota(I32, (n, n), 0)
    c = lax.broadcasted_iota(I32, (n, n), 1)
    return jnp.where(r < c, 1.0, 0.0).astype(dtype)


def _mod_kernel(c_ref, w_ref, b_ref, o_ref):
    c = c_ref[...]
    cond = (c * _sigmoid(c)).astype(BF16)
    o_ref[...] = _dot(cond, w_ref[...].astype(BF16)) + b_ref[...]


def _mod(c, w_ada, b_ada):
    bsz, d = c.shape
    n = w_ada.shape[1]
    tn = 1024
    return pl.pallas_call(
        _mod_kernel,
        out_shape=jax.ShapeDtypeStruct((bsz, n), F32),
        grid=(n // tn,),
        in_specs=[pl.BlockSpec((bsz, d), lambda j: (0, 0)),
                  pl.BlockSpec((d, tn), lambda j: (0, j)),
                  pl.BlockSpec((1, tn), lambda j: (0, j))],
        out_specs=pl.BlockSpec((bsz, tn), lambda j: (0, j)),
        compiler_params=_cparams(("parallel",)),
        name="mod",
    )(c, w_ada, b_ada.reshape(1, n))


def _rope(z, c_tab, s_tab, period, half):
    w = z.shape[1]
    reps = w // LANES
    c = jnp.concatenate([c_tab] * reps, axis=1) if reps > 1 else c_tab
    s = jnp.concatenate([s_tab] * reps, axis=1) if reps > 1 else s_tab
    lane = lax.broadcasted_iota(I32, z.shape, 1)
    first = (lane & (period - 1)) < half
    partner = jnp.where(first, pltpu.roll(z, w - half, axis=1), pltpu.roll(z, half, axis=1))
    return z * c + partner * s


def _in_proj_kernel(x_ref, mod_ref, w_ref, wsm_ref, c64_ref, s64_ref, c32_ref, s32_ref,
                    qa_ref, ka_ref, vaT_ref, qi_ref, ki_ref, wT_ref, qbraw_ref, qbrot_ref,
                    kcmp_ref, vcmp_ref, ksel_ref, vselT_ref, kwin_ref, vwinT_ref, gT_ref, ga_ref, gb_ref):
    mod = mod_ref[0]
    u = (x_ref[...] * (1.0 + mod[1:2, :]) + mod[0:1, :]).astype(BF16)
    tm = u.shape[0]
    c64, s64, c32, s32 = c64_ref[...], s64_ref[...], c32_ref[...], s32_ref[...]
    scale = HEAD_DIM ** -0.5
    lane = lax.broadcasted_iota(I32, (tm, LANES), 1)
    low = lane < HEAD_DIM

    def proj(a, b):
        return _dot(u, w_ref[:, a:b])

    rope64 = lambda z: _rope(z, c64, s64, HEAD_DIM, HEAD_DIM // ROPE_FRACTION // 2)
    rope32 = lambda z: _rope(z, c32, s32, IDX_DIM, IDX_DIM // ROPE_FRACTION // 2)

    def head_slots64(z):
        out = []
        for h in range(A_HEADS):
            pair = z[:, (h // 2) * LANES:(h // 2 + 1) * LANES]
            g = h // REP
            src = pair if h % 2 == g else pltpu.roll(pair, HEAD_DIM, axis=1)
            out.append(jnp.where(low, src, 0.0) if g == 0 else jnp.where(low, 0.0, src))
        return jnp.concatenate(out, axis=1).astype(BF16)

    def head_slots32(z):
        per = LANES // IDX_DIM
        out = []
        for h in range(IDX_HEADS):
            col = z[:, (h // per) * LANES:(h // per + 1) * LANES]
            shift = IDX_DIM * (h % per)
            src = col if shift == 0 else pltpu.roll(col, LANES - shift, axis=1)
            out.append(jnp.where(lane < IDX_DIM, src, 0.0))
        return jnp.concatenate(out, axis=1).astype(BF16)

    def store_vt(ref, z, chunk):
        zt = z.T
        ones = jnp.ones((HEAD_DIM, chunk), F32)
        for g in range(A_KV_HEADS):
            for j in range(tm // chunk):
                blk = zt[g * HEAD_DIM:(g + 1) * HEAD_DIM, j * chunk:(j + 1) * chunk]
                ref[g, j] = jnp.concatenate([blk, ones], axis=0).astype(BF16)

    qa_ref[...] = head_slots64(rope64(proj(0, 512)) * scale)
    ka_ref[...] = rope64(proj(512, 640)).astype(BF16)
    store_vt(vaT_ref, proj(640, 768), tm)
    qi_ref[...] = head_slots32(rope32(proj(768, 1024)))
    ki_ref[...] = rope32(proj(1024, 1152)).astype(BF16)
    qb = proj(1152, 1664)
    qbraw_ref[...] = head_slots64(qb * scale)
    qbrot_ref[...] = head_slots64(rope64(qb) * scale)
    kcmp_ref[...] = proj(1664, 1792)
    vcmp_ref[...] = proj(1792, 1920)
    ksel_ref[...] = rope64(proj(1920, 2048)).astype(BF16)
    store_vt(vselT_ref, proj(2048, 2176), tm)
    kwin_ref[...] = rope64(proj(2176, 2304)).astype(BF16)
    store_vt(vwinT_ref, proj(2304, 2432), LANES)
    ga_ref[...] = _sigmoid(proj(2432, 3456)).astype(BF16)
    gb_ref[...] = _sigmoid(proj(3456, 4480)).astype(BF16)
    small = _dot_nt(wsm_ref[...], u)
    wT_ref[...] = small[0:IDX_HEADS, :]
    gT_ref[...] = _sigmoid(small[IDX_HEADS:IDX_HEADS + N_GATES, :])


def _pack_w_in(w_in):
    d = w_in.shape[0]
    col = lambda i: w_in[:, _IN_OFFS[i]:_IN_OFFS[i + 1]]
    ki = jnp.concatenate([col(4), jnp.zeros((d, LANES - IDX_DIM), w_in.dtype)], axis=1)
    parts = [col(0), col(1), col(2), col(3), ki, col(6), col(7), col(8), col(9), col(10),
             col(11), col(12), col(14), col(15)]
    w_small = jnp.concatenate([col(5), col(13)], axis=1).T
    return jnp.concatenate(parts, axis=1).astype(BF16), w_small.astype(BF16)


def _rope_tables(positions):
    pos = positions.astype(F32).reshape(-1, 1)

    def tab(dim):
        rot = dim // ROPE_FRACTION
        half = rot // 2
        inv = ROPE_THETA ** (-(jnp.arange(half, dtype=F32) * 2.0) / rot)
        ang = pos * inv
        cos, sin = jnp.cos(ang), jnp.sin(ang)
        ones = jnp.ones((pos.shape[0], dim - rot), F32)
        c = jnp.concatenate([cos, cos, ones], axis=1)
        s = jnp.concatenate([-sin, sin, 0.0 * ones], axis=1)
        return jnp.tile(c, (1, LANES // dim)), jnp.tile(s, (1, LANES // dim))

    return tab(HEAD_DIM) + tab(IDX_DIM)


def _in_proj(x2, mod3, w_pack, w_small, tabs, seq, tm):
    t, d = x2.shape
    n = w_pack.shape[1]
    per_b = seq // tm
    g = A_KV_HEADS
    row = lambda w: pl.BlockSpec((tm, w), lambda i: (i, 0))
    tok = lambda r: pl.BlockSpec((r, tm), lambda i: (0, i))
    vt_chunk = pl.BlockSpec((g, 1, LANES, tm), lambda i: (0, i, 0, 0))
    vt_lane = pl.BlockSpec((g, tm // LANES, LANES, LANES), lambda i: (0, i, 0, 0))
    sds = jax.ShapeDtypeStruct
    vt_chunk_shape = sds((g, t // tm, LANES, tm), BF16)
    outs = (("qa", sds((t, 1024), BF16), row(1024)), ("ka", sds((t, LANES), BF16), row(LANES)),
            ("vaT", vt_chunk_shape, vt_chunk), ("qi", sds((t, 1024), BF16), row(1024)),
            ("ki", sds((t, LANES), BF16), row(LANES)), ("wT", sds((IDX_HEADS, t), F32), tok(IDX_HEADS)),
            ("qbraw", sds((t, 1024), BF16), row(1024)), ("qbrot", sds((t, 1024), BF16), row(1024)),
            ("kcmp", sds((t, LANES), F32), row(LANES)), ("vcmp", sds((t, LANES), F32), row(LANES)),
            ("ksel", sds((t, LANES), BF16), row(LANES)), ("vselT", vt_chunk_shape, vt_chunk),
            ("kwin", sds((t, LANES), BF16), row(LANES)),
            ("vwinT", sds((g, t // LANES, LANES, LANES), BF16), vt_lane),
            ("gT", sds((N_GATES, t), F32), tok(N_GATES)),
            ("ga", sds((t, d), BF16), row(d)), ("gb", sds((t, d), BF16), row(d)))
    res = pl.pallas_call(
        _in_proj_kernel,
        out_shape=tuple(o[1] for o in outs),
        grid=(t // tm,),
        in_specs=[row(d),
                  pl.BlockSpec((1, 6, d), lambda i: (i // per_b, 0, 0)),
                  pl.BlockSpec((d, n), lambda i: (0, 0)),
                  pl.BlockSpec(w_small.shape, lambda i: (0, 0)),
                  row(LANES), row(LANES), row(LANES), row(LANES)],
        out_specs=tuple(o[2] for o in outs),
        compiler_params=_cparams(("parallel",)),
        name="in_proj",
    )(x2, mod3, w_pack, w_small, *tabs)
    return {o[0]: r for o, r in zip(outs, res)}


def _fold_rows(x, op):
    n = x.shape[0]
    while n % (2 * SUBLANES) == 0:
        n //= 2
        x = op(x[:n], x[n:])
    slabs = [x[i * SUBLANES:(i + 1) * SUBLANES] for i in range(n // SUBLANES)]
    while len(slabs) > 1:
        nxt = [op(slabs[i], slabs[i + 1]) for i in range(0, len(slabs) - 1, 2)]
        slabs = nxt + ([slabs[-1]] if len(slabs) % 2 else [])
    return slabs[0]


def _col_max(x):
    return jnp.max(_fold_rows(x, jnp.maximum), axis=0, keepdims=True)


def _col_sum(x):
    return jnp.sum(_fold_rows(x, jnp.add), axis=0, keepdims=True)


def _count_rows(mask01):
    return _fold_rows(mask01, jnp.add)


def _stack_heads(q_ref, heads):
    return jnp.concatenate([q_ref[0, :, h * LANES:(h + 1) * LANES] for h in heads], axis=0)


def _flash_step(k, q_stack, v_t, bias4, m, acc):
    s = _dot_nt(k, q_stack) + bias4
    m_new = jnp.maximum(m, _col_max(s))
    e = jnp.exp(s - m_new).astype(BF16)
    return m_new, acc * jnp.exp(m - m_new) + _dot(v_t, e)


HEADS_PER_CHAIN = REP


def _head_chains():
    return [(h // REP, tuple(range(h, h + HEADS_PER_CHAIN))) for h in range(0, A_HEADS, HEADS_PER_CHAIN)]


def _flash_loop(q_ref, k_ref, vT_ref, bias_of, n_chunks, tq, kc):
    chains = _head_chains()
    q_stacks = [_stack_heads(q_ref, heads) for _, heads in chains]
    width = HEADS_PER_CHAIN * tq

    def body(c, carry):
        k = k_ref[0, pl.ds(pl.multiple_of(c * kc, kc), kc), :]
        out = []
        for (g, _), q_stack, (m, acc) in zip(chains, q_stacks, carry):
            bias = jnp.concatenate([bias_of(g, c)] * HEADS_PER_CHAIN, axis=1)
            out.append(_flash_step(k, q_stack, vT_ref[g, c], bias, m, acc))
        return tuple(out)

    init = tuple((jnp.full((1, width), NEG, F32), jnp.zeros((LANES, width), F32)) for _ in chains)
    res = lax.fori_loop(0, n_chunks, body, init)
    return [(heads, acc) for (_, heads), (_, acc) in zip(chains, res)]


def _normalise(acc):
    return acc / jnp.maximum(acc[HEAD_DIM:HEAD_DIM + 1, :], 1e-30)


def _store_heads(o_ref, o_t, heads, tq):
    o = o_t.T
    for r, h in enumerate(heads):
        o_ref[0, :, h * HEAD_DIM:(h + 1) * HEAD_DIM] = o[r * tq:(r + 1) * tq, 0:HEAD_DIM].astype(o_ref.dtype)


def _select_mask(keys, thr, need, offset, tri):
    gt = jnp.where(keys > thr, 1.0, 0.0)
    eq = jnp.where(keys == thr, 1.0, 0.0)
    prefix = _dot(tri, eq.astype(BF16)) + offset
    return gt + jnp.where(prefix < need, eq, 0.0), offset + _col_sum(eq)


def _dsa_kernel(qi_ref, wT_ref, ki_ref, qa_ref, ka_ref, vT_ref, o_ref, keys_ref, bias_ref,
                *, tq, kc, n_keep):
    t0 = pl.program_id(1) * tq
    n_chunks = (t0 + tq + kc - 1) // kc
    wT = wT_ref[...] * (IDX_HEADS ** -0.5 * IDX_DIM ** -0.5)
    qi_stack = _stack_heads(qi_ref, range(IDX_HEADS))
    keypos = lax.broadcasted_iota(I32, (kc, tq), 0)
    qpos = t0 + lax.broadcasted_iota(I32, (kc, tq), 1)

    def score_chunk(c, carry):
        k0 = pl.multiple_of(c * kc, kc)
        sc = _dot_nt(ki_ref[0, pl.ds(k0, kc), :], qi_stack)
        score = jnp.zeros((kc, tq), F32)
        for h in range(IDX_HEADS):
            score = score + wT[h:h + 1, :] * jnp.maximum(sc[:, h * tq:(h + 1) * tq], 0.0)
        keys_ref[c] = jnp.where(k0 + keypos <= qpos, _sort_key(score), INT_MIN)
        return carry

    lax.fori_loop(0, n_chunks, score_chunk, 0)

    def count_ge(cand):
        def body(c, acc):
            return acc + _count_rows(jnp.where(keys_ref[c] >= cand, 1.0, 0.0))
        acc = lax.fori_loop(0, n_chunks, body, jnp.zeros((SUBLANES, tq), F32))
        return jnp.sum(acc, axis=0, keepdims=True)

    thr = _kth_largest_key(count_ge, (1, tq), n_keep)
    need = float(n_keep) - count_ge(thr + 1)
    tri = _tri_strict_lower(LANES, BF16)
    sub = lax.broadcasted_iota(I32, (LANES, tq), 0)
    qsub = t0 + lax.broadcasted_iota(I32, (LANES, tq), 1)

    def bias_chunk(c, offset):
        for j in range(kc // LANES):
            rows = slice(j * LANES, (j + 1) * LANES)
            sel, offset = _select_mask(keys_ref[c, rows, :], thr, need, offset, tri)
            causal = c * kc + j * LANES + sub <= qsub
            bias_ref[c, rows, :] = jnp.where(causal, (sel - 1.0) * 1e30, NEG)
        return offset

    lax.fori_loop(0, n_chunks, bias_chunk, jnp.zeros((1, tq), F32))

    accs = _flash_loop(qa_ref, ka_ref, vT_ref, lambda g, c: bias_ref[c], n_chunks, tq, kc)
    for heads, acc in accs:
        _store_heads(o_ref, _normalise(acc), heads, tq)


def _dsa(qi, wT, ki, qa, ka, vaT, bsz, seq, tq, kc):
    n_keep = min(DSA_TOPK_MAX, seq // 4)
    nq = seq // tq
    qblk = lambda w: pl.BlockSpec((1, tq, w), lambda b, i: (b, i, 0))
    full = lambda w: pl.BlockSpec((1, seq, w), lambda b, i: (b, 0, 0))
    return pl.pallas_call(
        functools.partial(_dsa_kernel, tq=tq, kc=kc, n_keep=n_keep),
        out_shape=jax.ShapeDtypeStruct((bsz, seq, A_HEADS * HEAD_DIM), BF16),
        grid=(bsz, nq),
        in_specs=[qblk(1024), pl.BlockSpec((IDX_HEADS, tq), lambda b, i: (0, b * nq + i)), full(LANES),
                  qblk(1024), full(LANES),
                  pl.BlockSpec((A_KV_HEADS, seq // kc, LANES, kc), lambda b, i: (0, b, 0, 0))],
        out_specs=qblk(512),
        scratch_shapes=[pltpu.VMEM((seq // kc, kc, tq), I32), pltpu.VMEM((seq // kc, kc, tq), F32)],
        compiler_params=_cparams(("parallel", "parallel")),
        name="dsa",
    )(qi, wT, ki, qa, ka, vaT)


def _gelu_tanh(x):
    return 0.5 * x * (1.0 + jnp.tanh(math.sqrt(2.0 / math.pi) * (x + 0.044715 * (x * x * x))))


def _compress_kernel(k_ref, v_ref, pk_ref, pv_ref, w1k_ref, w2k_ref, w1v_ref, w2v_ref,
                     kc_ref, vcT_ref, *, n_rows):
    half = CMP_BLOCK // 2

    def one(x_ref, p_ref, w1_ref, w2_ref):
        outs = []
        for g in range(B_KV_HEADS):
            lo = jnp.zeros((n_rows, CMP_HIDDEN), F32)
            hi = jnp.zeros((n_rows, CMP_HIDDEN), F32)
            for l in range(half):
                xl = x_ref[0, :, l * LANES + g * HEAD_DIM:l * LANES + (g + 1) * HEAD_DIM]
                a = (xl + p_ref[l:l + 1, :]).astype(BF16)
                b = (xl + p_ref[half + l:half + l + 1, :]).astype(BF16)
                lo = lo + _dot(a, w1_ref[l * HEAD_DIM:(l + 1) * HEAD_DIM, :].astype(BF16))
                hi = hi + _dot(b, w1_ref[(half + l) * HEAD_DIM:(half + l + 1) * HEAD_DIM, :].astype(BF16))
            hid = lo + pltpu.roll(hi, n_rows - 1, axis=0)
            outs.append(_dot(_gelu_tanh(hid).astype(BF16), w2_ref[...].astype(BF16)))
        return outs

    k0, k1 = one(k_ref, pk_ref, w1k_ref, w2k_ref)
    kc_ref[0] = jnp.concatenate([k0, k1], axis=1).astype(kc_ref.dtype)
    for g, v in enumerate(one(v_ref, pv_ref, w1v_ref, w2v_ref)):
        vcT_ref[0, g] = jnp.concatenate([v, jnp.zeros_like(v)], axis=1).T.astype(vcT_ref.dtype)


def _compress(kcmp, vcmp, pos_k, pos_v, w1k, w2k, w1v, w2v):
    bsz, n_rows, width = kcmp.shape
    xspec = pl.BlockSpec((1, n_rows, width), lambda b: (b, 0, 0))
    cst = lambda a: pl.BlockSpec(a.shape, lambda b: (0,) * a.ndim)
    return pl.pallas_call(
        functools.partial(_compress_kernel, n_rows=n_rows),
        out_shape=(jax.ShapeDtypeStruct((bsz, n_rows, LANES), BF16),
                   jax.ShapeDtypeStruct((bsz, B_KV_HEADS, LANES, n_rows), BF16)),
        grid=(bsz,),
        in_specs=[xspec, xspec, cst(pos_k), cst(pos_v), cst(w1k), cst(w2k), cst(w1v), cst(w2v)],
        out_specs=(pl.BlockSpec((1, n_rows, LANES), lambda b: (b, 0, 0)),
                   pl.BlockSpec((1, B_KV_HEADS, LANES, n_rows), lambda b: (b, 0, 0, 0))),
        compiler_params=_cparams(("parallel",)),
        name="compress",
    )(kcmp, vcmp, pos_k, pos_v, w1k, w2k, w1v, w2v)


def _split3(x):
    a = x.astype(BF16)
    r = x - a.astype(F32)
    b = r.astype(BF16)
    c = (r - b.astype(F32)).astype(BF16)
    return a, b, c


def _nsa_kernel(qraw_ref, qrot_ref, kc_ref, vcT_ref, ksel_ref, vselT_ref, kwin_ref, vwinT_ref,
                gT_ref, exp_ref, o_ref, bias_ref, *, tq, kc, seq, n_c):
    t0 = pl.program_id(1) * tq
    n_chunks = (t0 + tq + kc - 1) // kc
    n_s = seq // SEL_BLOCK
    n_pick = min(SEL_COUNT, n_s)
    span = WINDOW + tq
    gT = gT_ref[...]

    ridx = lax.broadcasted_iota(I32, (LANES, tq), 0)
    tl = t0 + lax.broadcasted_iota(I32, (LANES, tq), 1)
    valid_c = jnp.where((ridx * CMP_STRIDE + CMP_BLOCK - 1 <= tl) & (ridx < n_c), 1.0, 0.0)
    valid4 = jnp.concatenate([valid_c] * REP, axis=1) > 0.0
    js = lax.broadcasted_iota(I32, (LANES, LANES), 0) * SEL_BLOCK
    cs = lax.broadcasted_iota(I32, (LANES, LANES), 1) * CMP_STRIDE
    overlap_t = jnp.where((cs <= js + SEL_BLOCK - 1) & (cs + CMP_BLOCK - 1 >= js), 1.0, 0.0).astype(BF16)
    cur = tl // SEL_BLOCK
    forced = (ridx == 0) | ((cur - ridx >= 0) & (cur - ridx < SEL_LOCAL))
    blk_causal = ridx * SEL_BLOCK <= tl
    tri = _tri_strict_lower(LANES, BF16)
    keypos = lax.broadcasted_iota(I32, (kc, tq), 0)
    qpos = t0 + lax.broadcasted_iota(I32, (kc, tq), 1)
    w0 = pl.multiple_of(jnp.clip(t0 - WINDOW, 0, seq - span), tq)
    wdiff = (t0 + lax.broadcasted_iota(I32, (span, tq), 1)) - (w0 + lax.broadcasted_iota(I32, (span, tq), 0))
    wbias = jnp.where((wdiff >= 0) & (wdiff < WINDOW), 0.0, NEG)
    wbias_c = jnp.concatenate([wbias] * HEADS_PER_CHAIN, axis=1)
    wblk = w0 // LANES

    o_cmp_g = []
    for g in range(B_KV_HEADS):
        heads = range(g * REP, (g + 1) * REP)
        s = jnp.where(valid4, _dot_nt(kc_ref[0], _stack_heads(qraw_ref, heads)), NEG)
        e = jnp.where(valid4, jnp.exp(s - _col_max(s)), 0.0)
        p = e / jnp.maximum(_col_sum(e), 1e-30)
        o_cmp_g.append(_dot(vcT_ref[0, g], p.astype(BF16)))
        p_sum = p[:, 0:tq]
        for r in range(1, REP):
            p_sum = p_sum + p[:, r * tq:(r + 1) * tq]
        pa, pb, pc = _split3(p_sum)
        imp = _dot(overlap_t, pa) + _dot(overlap_t, pb) + _dot(overlap_t, pc)
        imp = jnp.where(forced, FORCE, jnp.where(blk_causal, imp, NEG))
        keys = _sort_key(imp[0:n_s, :])
        count_ge = lambda cand, keys=keys: _col_sum(jnp.where(keys >= cand, 1.0, 0.0))
        thr = _kth_largest_key(count_ge, (1, tq), n_pick)
        need = float(n_pick) - _col_sum(jnp.where(keys > thr, 1.0, 0.0))
        blk_sel, _ = _select_mask(keys, thr, need, jnp.zeros((1, tq), F32), tri[0:n_s, 0:n_s])
        blk_sel = jnp.concatenate([blk_sel, jnp.zeros((LANES - n_s, tq), F32)], axis=0).astype(BF16)

        def bias_chunk(c, carry, blk_sel=blk_sel, g=g):
            tok_sel = _dot(exp_ref[c], blk_sel)
            bias_ref[g, c] = jnp.where(c * kc + keypos <= qpos, (tok_sel - 1.0) * 1e30, NEG)
            return carry

        lax.fori_loop(0, n_chunks, bias_chunk, 0)

    sel = _flash_loop(qrot_ref, ksel_ref, vselT_ref, lambda g, c: bias_ref[g, c], n_chunks, tq, kc)

    kwin = kwin_ref[0, pl.ds(w0, span), :]
    for (g, heads), (_, acc_sel) in zip(_head_chains(), sel):
        sw = _dot_nt(kwin, _stack_heads(qrot_ref, heads)) + wbias_c
        ew = jnp.exp(sw - _col_max(sw)).astype(BF16)
        acc_win = _dot(vwinT_ref[g, wblk], ew[0:LANES, :])
        for j in range(1, span // LANES):
            acc_win = acc_win + _dot(vwinT_ref[g, wblk + j], ew[j * LANES:(j + 1) * LANES, :])

        o_sel = _normalise(acc_sel)
        o_win = _normalise(acc_win)
        cols = []
        for r, h in enumerate(heads):
            col = slice(r * tq, (r + 1) * tq)
            cmp_col = slice((h % REP) * tq, (h % REP + 1) * tq)
            cols.append(gT[3 * h:3 * h + 1, :] * o_cmp_g[g][:, cmp_col]
                        + gT[3 * h + 1:3 * h + 2, :] * o_sel[:, col]
                        + gT[3 * h + 2:3 * h + 3, :] * o_win[:, col])
        _store_heads(o_ref, jnp.concatenate(cols, axis=1), heads, tq)


def _nsa(qraw, qrot, kc_x, vcT, ksel, vselT, kwin, vwinT, gT, bsz, seq, tq, kc):
    n_c = (seq - CMP_BLOCK) // CMP_STRIDE + 1
    nq = seq // tq
    key_blk = (jnp.arange(seq, dtype=I32) // SEL_BLOCK).reshape(seq // kc, kc, 1)
    expand = (key_blk == jnp.arange(LANES, dtype=I32).reshape(1, 1, LANES)).astype(BF16)
    qblk = lambda w: pl.BlockSpec((1, tq, w), lambda b, i: (b, i, 0))
    full = lambda w: pl.BlockSpec((1, seq, w), lambda b, i: (b, 0, 0))
    g = B_KV_HEADS
    return pl.pallas_call(
        functools.partial(_nsa_kernel, tq=tq, kc=kc, seq=seq, n_c=n_c),
        out_shape=jax.ShapeDtypeStruct((bsz, seq, B_HEADS * HEAD_DIM), BF16),
        grid=(bsz, nq),
        in_specs=[qblk(1024), qblk(1024),
                  pl.BlockSpec((1,) + kc_x.shape[1:], lambda b, i: (b, 0, 0)),
                  pl.BlockSpec((1,) + vcT.shape[1:], lambda b, i: (b, 0, 0, 0)),
                  full(LANES), pl.BlockSpec((g, seq // kc, LANES, kc), lambda b, i: (0, b, 0, 0)),
                  full(LANES), pl.BlockSpec((g, seq // LANES, LANES, LANES), lambda b, i: (0, b, 0, 0)),
                  pl.BlockSpec((N_GATES, tq), lambda b, i: (0, b * nq + i)),
                  pl.BlockSpec(expand.shape, lambda b, i: (0, 0, 0))],
        out_specs=qblk(512),
        scratch_shapes=[pltpu.VMEM((g, seq // kc, kc, tq), F32)],
        compiler_params=_cparams(("parallel", "parallel")),
        name="nsa",
    )(qraw, qrot, kc_x, vcT, ksel, vselT, kwin, vwinT, gT, expand)


def _pack_pairs(x):
    n = x.shape[1] // 2
    lo = pltpu.bitcast(x[:, :n].astype(BF16).astype(F32), I32)
    hi = pltpu.bitcast(x[:, n:].astype(BF16).astype(F32), I32)
    return lax.shift_right_logical(lo, 16) | (hi & jnp.int32(-65536))


def _unpack_pairs(p):
    lo = pltpu.bitcast(lax.shift_left(p, 16), F32)
    hi = pltpu.bitcast(p & jnp.int32(-65536), F32)
    return jnp.concatenate([lo, hi], axis=1)


def _layer_norm(y, g, b):
    mu = jnp.mean(y, axis=1, keepdims=True)
    yc = y - mu
    var = jnp.mean(yc * yc, axis=1, keepdims=True)
    return yc * lax.rsqrt(var + LN_EPS) * g + b


def _out_proj_kernel(oa_ref, ob_ref, ga_ref, gb_ref, x_ref, mod_ref, wa_ref, wb_ref, wo_ref,
                     g1_ref, b1_ref, wrh_ref, wrl_ref, x1_ref, u2_ref, lg_ref):
    mod = mod_ref[0]
    merged = (ga_ref[...].astype(F32) * _dot(oa_ref[...], wa_ref[...])
              + gb_ref[...].astype(F32) * _dot(ob_ref[...], wb_ref[...]))
    mix = _dot(merged.astype(BF16), wo_ref[...])
    x1 = _layer_norm(DN_ALPHA * x_ref[...] + mod[2:3, :] * mix, g1_ref[...], b1_ref[...])
    x1_ref[...] = x1
    u2 = x1 * (1.0 + mod[4:5, :]) + mod[3:4, :]
    packed = _pack_pairs(u2)
    for j in range(PIECES):
        u2_ref[j] = packed[:, j * SC_ROW:(j + 1) * SC_ROW]
    uh = u2.astype(BF16)
    ul = (u2 - uh.astype(F32)).astype(BF16)
    lg_ref[...] = _dot_nt(wrh_ref[...], uh) + _dot_nt(wrh_ref[...], ul) + _dot_nt(wrl_ref[...], uh)


def _out_proj(oa, ob, ga, gb, x2, mod3, wa, wb, wo, g1, b1, wrh, wrl, seq, tm):
    t, d = x2.shape
    per_b = seq // tm
    row = lambda w: pl.BlockSpec((tm, w), lambda i: (i, 0))
    cst = lambda a: pl.BlockSpec(a.shape, lambda i: (0,) * a.ndim)
    return pl.pallas_call(
        _out_proj_kernel,
        out_shape=(jax.ShapeDtypeStruct((t, d), F32), jax.ShapeDtypeStruct((PIECES, t, SC_ROW), I32),
                   jax.ShapeDtypeStruct((N_EXPERTS, t), F32)),
        grid=(t // tm,),
        in_specs=[row(512), row(512), row(d), row(d), row(d),
                  pl.BlockSpec((1, 6, d), lambda i: (i // per_b, 0, 0)),
                  cst(wa), cst(wb), cst(wo), cst(g1), cst(b1), cst(wrh), cst(wrl)],
        out_specs=(row(d), pl.BlockSpec((PIECES, tm, SC_ROW), lambda i: (0, i, 0)),
                   pl.BlockSpec((N_EXPERTS, tm), lambda i: (0, i))),
        compiler_params=_cparams(("parallel",)),
        name="out_proj",
    )(oa, ob, ga, gb, x2, mod3, wa, wb, wo, g1, b1, wrh, wrl)


def _first_max(x, rows):
    m = jnp.max(x, axis=0, keepdims=True)
    idx = jnp.min(jnp.where(x == m, rows, 1e9), axis=0, keepdims=True)
    return m, idx


def _router_kernel(lg_ref, rb_ref, idx_ref, w_ref, pos_ref, cnt_ref, carry_ref, *, tm):
    @pl.when(pl.program_id(0) == 0)
    def _():
        carry_ref[...] = jnp.zeros_like(carry_ref)

    per_g = N_EXPERTS // N_GROUPS
    scores = _sigmoid(lg_ref[...])
    choice = scores + rb_ref[...][:, 0:1]
    rows = lax.broadcasted_iota(I32, (N_EXPERTS, tm), 0).astype(F32)
    rows_g = lax.broadcasted_iota(I32, (per_g, tm), 0).astype(F32)
    ninf = -jnp.inf

    gs = []
    for g in range(N_GROUPS):
        x = choice[g * per_g:(g + 1) * per_g, :]
        m1, i1 = _first_max(x, rows_g)
        m2 = jnp.max(jnp.where(rows_g == i1, ninf, x), axis=0, keepdims=True)
        gs.append(m1 + m2)
    gscore = jnp.concatenate(gs, axis=0)
    rows8 = lax.broadcasted_iota(I32, (N_GROUPS, tm), 0).astype(F32)
    keep = jnp.zeros((N_GROUPS, tm), F32)
    for _ in range(TOPK_GROUPS):
        _, gi = _first_max(gscore, rows8)
        hit = rows8 == gi
        keep = jnp.where(hit, 1.0, keep)
        gscore = jnp.where(hit, ninf, gscore)
    keep_full = jnp.concatenate(
        [jnp.broadcast_to(keep[g:g + 1, :], (per_g, tm)) for g in range(N_GROUPS)], axis=0)
    masked = jnp.where(keep_full > 0.0, choice, NEG)

    idxs, ws = [], []
    onehot = jnp.zeros((N_EXPERTS, tm), F32)
    for _ in range(TOP_K):
        _, ei = _first_max(masked, rows)
        hit = rows == ei
        idxs.append(ei)
        ws.append(jnp.sum(jnp.where(hit, scores, 0.0), axis=0, keepdims=True))
        masked = jnp.where(hit, ninf, masked)
        onehot = jnp.where(hit, 1.0, onehot)
    idx = jnp.concatenate(idxs, axis=0)
    w = jnp.concatenate(ws, axis=0)
    idx_ref[...] = idx.astype(I32)
    w = w / jnp.sum(w, axis=0, keepdims=True) * ROUTED_SCALE
    w_ref[...] = jnp.concatenate([w, jnp.zeros((LANES - TOP_K, tm), F32)], axis=0).T

    tri = _tri_strict_upper(tm, BF16)
    base = _dot(onehot.astype(BF16), tri) + carry_ref[...][:, 0:1]
    pos = [jnp.sum(jnp.where(rows == idxs[k], base, 0.0), axis=0, keepdims=True) for k in range(TOP_K)]
    pos_ref[...] = jnp.concatenate(pos, axis=0).astype(I32)
    carry = carry_ref[...] + jnp.sum(onehot, axis=1, keepdims=True)
    carry_ref[...] = carry
    cnt_ref[...] = carry


def _router(lg, router_bias, tm):
    e, t = lg.shape
    rb = jnp.broadcast_to(router_bias.reshape(e, 1).astype(F32), (e, LANES))
    tok = lambda r: pl.BlockSpec((r, tm), lambda i: (0, i))
    return pl.pallas_call(
        functools.partial(_router_kernel, tm=tm),
        out_shape=(jax.ShapeDtypeStruct((TOP_K, t), I32), jax.ShapeDtypeStruct((t, LANES), F32),
                   jax.ShapeDtypeStruct((TOP_K, t), I32), jax.ShapeDtypeStruct((e, LANES), F32)),
        grid=(t // tm,),
        in_specs=[tok(e), pl.BlockSpec((e, LANES), lambda i: (0, 0))],
        out_specs=(tok(TOP_K), pl.BlockSpec((tm, LANES), lambda i: (i, 0)), tok(TOP_K),
                   pl.BlockSpec((e, LANES), lambda i: (0, 0))),
        scratch_shapes=[pltpu.VMEM((e, LANES), F32)],
        compiler_params=_cparams(("arbitrary",)),
        name="router",
    )(lg, rb)


def _dest_kernel(idx_ref, pos_ref, st_ref, dest_ref, *, tm):
    rows = lax.broadcasted_iota(I32, (N_EXPERTS, tm), 0)
    starts = st_ref[...][:, 0:1]
    idx = idx_ref[...]
    out = []
    for k in range(TOP_K):
        out.append(jnp.sum(jnp.where(rows == idx[k:k + 1, :], starts, 0.0), axis=0, keepdims=True))
    dest_ref[...] = jnp.concatenate(out, axis=0).astype(I32) + pos_ref[...]


def _dest(idx, pos, starts, tm):
    k, t = idx.shape
    st = jnp.broadcast_to(starts.reshape(N_EXPERTS, 1).astype(F32), (N_EXPERTS, LANES))
    tok = pl.BlockSpec((k, tm), lambda i: (0, i))
    return pl.pallas_call(
        functools.partial(_dest_kernel, tm=tm),
        out_shape=jax.ShapeDtypeStruct((k, t), I32),
        grid=(t // tm,),
        in_specs=[tok, tok, pl.BlockSpec((N_EXPERTS, LANES), lambda i: (0, 0))],
        out_specs=tok,
        compiler_params=_cparams(("parallel",)),
        name="dest",
    )(idx, pos, st)


def _experts_kernel(blk_ref, used_ref, first_ref, slot_ref, next_ref, x_ref, wg_hbm, wu_hbm, wd_hbm, y_ref,
                    wg_buf, wu_buf, wd_buf, wg_bf, wu_bf, wd_bf, sems):
    b = pl.program_id(0)
    active = b < used_ref[0]

    def weight_copies(e, slot):
        return [pltpu.make_async_copy(hbm.at[e], buf.at[slot], sems.at[slot, i])
                for i, (hbm, buf) in enumerate(((wg_hbm, wg_buf), (wu_hbm, wu_buf), (wd_hbm, wd_buf)))]

    @pl.when(b == 0)
    def _():
        for cp in weight_copies(blk_ref[0], 0):
            cp.start()

    @pl.when(active & (first_ref[b] == 1))
    def _():
        slot = slot_ref[b]
        for cp in weight_copies(blk_ref[b], slot):
            cp.wait()
        nxt = next_ref[b]

        @pl.when(nxt >= 0)
        def _():
            for cp in weight_copies(nxt, 1 - slot):
                cp.start()

        wg_bf[...] = wg_buf[slot].astype(BF16)
        wu_bf[...] = wu_buf[slot].astype(BF16)
        wd_bf[...] = wd_buf[slot].astype(BF16)

    @pl.when(active)
    def _():
        x = jnp.concatenate([x_ref[j] for j in range(PIECES)], axis=1)
        x = _unpack_pairs(x).astype(BF16)
        a = _dot(x, wg_bf[...])
        u = _dot(x, wu_bf[...])
        h = (a * _sigmoid(a) * u).astype(BF16)
        y = _pack_pairs(_dot(h, wd_bf[...]))
        for j in range(PIECES):
            y_ref[j] = y[:, j * SC_ROW:(j + 1) * SC_ROW]


def _experts(xs, plan, wg, wu, wd, bm):
    _, cap, _ = xs.shape
    n_blocks = cap // bm
    d, f = wg.shape[1], wg.shape[2]
    rows = pl.BlockSpec((PIECES, bm, SC_ROW), lambda b, blk, used, *_: (0, jnp.minimum(b, used[0] - 1), 0))
    hbm = pl.BlockSpec(memory_space=pl.ANY)
    return pl.pallas_call(
        _experts_kernel,
        out_shape=jax.ShapeDtypeStruct(xs.shape, I32),
        grid_spec=pltpu.PrefetchScalarGridSpec(
            num_scalar_prefetch=5,
            grid=(n_blocks,),
            in_specs=[rows, hbm, hbm, hbm],
            out_specs=rows,
            scratch_shapes=[pltpu.VMEM((2, d, f), F32), pltpu.VMEM((2, d, f), F32), pltpu.VMEM((2, f, d), F32),
                            pltpu.VMEM((d, f), BF16), pltpu.VMEM((d, f), BF16), pltpu.VMEM((f, d), BF16),
                            pltpu.SemaphoreType.DMA((2, 3))]),
        compiler_params=_cparams(("arbitrary",)),
        name="experts",
    )(plan["blk_e"], plan["n_used"], plan["first"], plan["slot"], plan["next_e"], xs, wg, wu, wd)


def _final_kernel(x1_ref, u2_ref, yg_ref, w_ref, mod_ref, sg_ref, su_ref, sd_ref, g2_ref, b2_ref, o_ref):
    mod = mod_ref[0]
    w = w_ref[...]
    rows = lambda ref, *lead: jnp.concatenate([ref[(j,) + lead] for j in range(PIECES)], axis=1)
    routed = w[:, 0:1] * _unpack_pairs(rows(yg_ref, 0))
    for k in range(1, TOP_K):
        routed = routed + w[:, k:k + 1] * _unpack_pairs(rows(yg_ref, k))
    u = _unpack_pairs(rows(u2_ref)).astype(BF16)
    a = _dot(u, sg_ref[...])
    b = _dot(u, su_ref[...])
    shared = _dot((a * _sigmoid(a) * b).astype(BF16), sd_ref[...])
    y = DN_ALPHA * x1_ref[...] + mod[5:6, :] * (routed + shared)
    o_ref[...] = _layer_norm(y, g2_ref[...], b2_ref[...])


def _final(x1, u2p, yg, wtok, mod3, sg, su, sd, g2, b2, seq, tm):
    t, d = x1.shape
    per_b = seq // tm
    row = lambda w: pl.BlockSpec((tm, w), lambda i: (i, 0))
    cst = lambda a: pl.BlockSpec(a.shape, lambda i: (0,) * a.ndim)
    return pl.pallas_call(
        _final_kernel,
        out_shape=jax.ShapeDtypeStruct((t, d), F32),
        grid=(t // tm,),
        in_specs=[row(d), pl.BlockSpec((PIECES, tm, SC_ROW), lambda i: (0, i, 0)),
                  pl.BlockSpec((PIECES, TOP_K, tm, SC_ROW), lambda i: (0, 0, i, 0)),
                  row(LANES), pl.BlockSpec((1, 6, d), lambda i: (i // per_b, 0, 0)),
                  cst(sg), cst(su), cst(sd), cst(g2), cst(b2)],
        out_specs=row(d),
        compiler_params=_cparams(("parallel",)),
        name="final",
    )(x1, u2p, yg, wtok, mod3, sg, su, sd, g2, b2)


def _sc_mesh():
    return plsc.VectorSubcoreMesh(core_axis_name="core", subcore_axis_name="subcore")


def _sc_scatter_rows(src, idx, n_out, src_block):
    n_idx = idx.shape[0]

    @functools.partial(pl.kernel, out_type=jax.ShapeDtypeStruct((n_out, SC_ROW), src.dtype),
                       mesh=_sc_mesh(), scratch_types=[])
    def k(x_hbm, i_hbm, o_hbm):
        def body(x_vmem, i_vmem):
            pltpu.sync_copy(x_vmem, o_hbm.at[i_vmem.at[0]])

        pltpu.emit_pipeline(
            body, grid=(n_idx // SC_WINDOW,),
            in_specs=[pl.BlockSpec((SC_WINDOW, SC_ROW), lambda i: (src_block(i), 0)),
                      pl.BlockSpec((1, SC_WINDOW), lambda i: (0, i))],
            out_specs=[],
            core_axis_name=("core", "subcore"),
            dimension_semantics=(pltpu.PARALLEL,),
        )(x_hbm, i_hbm)

    return k(src, idx.reshape(1, n_idx))


def _sc_gather_rows(src, idx):
    n_idx = idx.shape[0]

    @functools.partial(pl.kernel, out_type=jax.ShapeDtypeStruct((n_idx, SC_ROW), src.dtype),
                       mesh=_sc_mesh(), scratch_types=[])
    def k(x_hbm, i_hbm, o_hbm):
        def body(i_vmem, o_vmem):
            pltpu.sync_copy(x_hbm.at[i_vmem.at[0]], o_vmem)

        pltpu.emit_pipeline(
            body, grid=(n_idx // SC_WINDOW,),
            in_specs=[pl.BlockSpec((1, SC_WINDOW), lambda i: (0, i))],
            out_specs=[pl.BlockSpec((SC_WINDOW, SC_ROW), lambda i: (i, 0))],
            core_axis_name=("core", "subcore"),
            dimension_semantics=(pltpu.PARALLEL,),
        )(i_hbm, o_hbm)

    return k(src, idx.reshape(1, n_idx))


def _moe_plan(counts, n_tok):
    bm = BM_EXPERT
    padded = (counts + bm - 1) // bm * bm
    p_ends = jnp.cumsum(padded)
    starts = p_ends - padded
    n_blocks = n_tok * TOP_K // bm + N_EXPERTS
    blk = jnp.arange(n_blocks, dtype=I32)
    blk_e = jnp.minimum(jnp.sum(p_ends[None, :] <= (blk * bm)[:, None], axis=1), N_EXPERTS - 1).astype(I32)
    n_used = (p_ends[-1] // bm).astype(I32)
    prev = jnp.concatenate([jnp.full((1,), -1, I32), blk_e[:-1]])
    first = ((blk_e != prev) & (blk < n_used)).astype(I32)
    slot = (jnp.cumsum(first) - 1) % 2
    first_pos = jnp.where(first == 1, blk, n_blocks)
    next_first = lax.cummin(jnp.concatenate([first_pos[1:], jnp.full((1,), n_blocks, I32)]), reverse=True)
    next_e = jnp.where(next_first < n_blocks, blk_e[jnp.minimum(next_first, n_blocks - 1)], -1)
    plan = dict(blk_e=blk_e, n_used=n_used.reshape(1), first=first, slot=slot.astype(I32),
                next_e=next_e.astype(I32))
    return starts, plan, n_blocks


def _layer(x, mod, positions, w_in, w_br_a, w_br_b, w_out, cmp_pos_k, cmp_pos_v, cmp_k_w1, cmp_k_w2,
           cmp_v_w1, cmp_v_w2, ln1_g, ln1_b, w_router, router_bias, w_exp_gate, w_exp_up, w_exp_down,
           w_sh_gate, w_sh_up, w_sh_down, ln2_g, ln2_b):
    bsz, seq, d = x.shape
    t = bsz * seq
    assert seq // CMP_STRIDE == LANES and seq % TQ_NSA == 0 and seq % TQ_DSA == 0
    assert seq % TM_PROJ == 0 and KC_ATTN == TM_PROJ
    x2 = x.reshape(t, d)
    mod3 = mod.reshape(bsz, 6, d)

    w_pack, w_small = _pack_w_in(w_in)
    z = _in_proj(x2, mod3, w_pack, w_small, _rope_tables(positions), seq, TM_PROJ)
    per_b = lambda name: z[name].reshape(bsz, seq, z[name].shape[1])

    o_a = _dsa(per_b("qi"), z["wT"], per_b("ki"), per_b("qa"), per_b("ka"), z["vaT"],
               bsz, seq, TQ_DSA, KC_ATTN)

    n_rows = seq // CMP_STRIDE
    kc_x, vcT = _compress(z["kcmp"].reshape(bsz, n_rows, CMP_STRIDE * LANES),
                          z["vcmp"].reshape(bsz, n_rows, CMP_STRIDE * LANES),
                          cmp_pos_k, cmp_pos_v, cmp_k_w1, cmp_k_w2, cmp_v_w1, cmp_v_w2)
    o_b = _nsa(per_b("qbraw"), per_b("qbrot"), kc_x, vcT, per_b("ksel"), z["vselT"], per_b("kwin"),
               z["vwinT"], z["gT"], bsz, seq, TQ_NSA, KC_ATTN)

    wr_hi = w_router.T.astype(BF16)
    wr_lo = (w_router.T - wr_hi.astype(F32)).astype(BF16)
    x1, u2p, logits = _out_proj(
        o_a.reshape(t, -1), o_b.reshape(t, -1), z["ga"], z["gb"], x2, mod3,
        w_br_a.astype(BF16), w_br_b.astype(BF16), w_out.astype(BF16),
        ln1_g.reshape(1, d), ln1_b.reshape(1, d), wr_hi, wr_lo, seq, TM_PROJ)

    idx, wtok, pos, counts = _router(logits, router_bias, TM_ROUTE)
    starts, plan, n_blocks = _moe_plan(counts[:, 0].astype(I32), t)
    dest = _dest(idx, pos, starts, TM_ROUTE)

    cap = n_blocks * BM_EXPERT
    dest_p = (dest[None] + (jnp.arange(PIECES, dtype=I32) * cap).reshape(PIECES, 1, 1)).reshape(-1)
    tb = t // SC_WINDOW
    xs = _sc_scatter_rows(u2p.reshape(PIECES * t, SC_ROW), dest_p, cap * PIECES,
                          lambda i: (i // (TOP_K * tb)) * tb + i % tb)
    ys = _experts(xs.reshape(PIECES, cap, SC_ROW), plan, w_exp_gate, w_exp_up, w_exp_down, BM_EXPERT)
    yg = _sc_gather_rows(ys.reshape(cap * PIECES, SC_ROW), dest_p).reshape(PIECES, TOP_K, t, SC_ROW)

    return _final(x1, u2p, yg, wtok, mod3, w_sh_gate.astype(BF16), w_sh_up.astype(BF16),
                  w_sh_down.astype(BF16), ln2_g.reshape(1, d), ln2_b.reshape(1, d), seq, TM_PROJ
                  ).reshape(bsz, seq, d)


def kernel(x, c, positions, w_ada, b_ada, w_in, w_br_a, w_br_b, w_out, cmp_pos_k, cmp_pos_v, cmp_k_w1,
           cmp_k_w2, cmp_v_w1, cmp_v_w2, ln1_g, ln1_b, w_router, router_bias, w_exp_gate, w_exp_up,
           w_exp_down, w_sh_gate, w_sh_up, w_sh_down, ln2_g, ln2_b):
    for l in range(w_ada.shape[0]):
        mod = _mod(c, w_ada[l], b_ada[l])
        x = _layer(x, mod, positions, w_in[l], w_br_a[l], w_br_b[l], w_out[l], cmp_pos_k[l], cmp_pos_v[l],
                   cmp_k_w1[l], cmp_k_w2[l], cmp_v_w1[l], cmp_v_w2[l], ln1_g[l], ln1_b[l], w_router[l],
                   router_bias[l], w_exp_gate[l], w_exp_up[l], w_exp_down[l], w_sh_gate[l], w_sh_up[l],
                   w_sh_down[l], ln2_g[l], ln2_b[l])
    return x
```

```python
import functools
import math

import jax
import jax.numpy as jnp
import numpy as np
from jax import lax
from jax.experimental import pallas as pl
from jax.experimental.pallas import tpu as pltpu
from jax.experimental.pallas import tpu_sc as plsc

F32 = jnp.float32
BF16 = jnp.bfloat16
I32 = jnp.int32

D_MODEL = 1024
HEAD_DIM = 64
ROPE_THETA = 500000.0
ROPE_FRACTION = 4
A_HEADS = 8
A_KV_HEADS = 2
IDX_HEADS = 8
IDX_DIM = 32
DSA_TOPK_MAX = 256
B_HEADS = 8
B_KV_HEADS = 2
REP = 4
CMP_BLOCK = 32
CMP_STRIDE = 16
CMP_HIDDEN = 256
SEL_BLOCK = 64
SEL_COUNT = 16
SEL_LOCAL = 2
WINDOW = 512
N_EXPERTS = 256
TOP_K = 8
N_GROUPS = 8
TOPK_GROUPS = 4
ROUTED_SCALE = 2.5
DEPTH = 1
DN_ALPHA = (2 * DEPTH) ** 0.25
LN_EPS = 1e-5
NEG = -1e30
FORCE = 1e9
INT_MIN = -2147483648
N_GATES = 3 * B_HEADS

LANES = 128
SUBLANES = 8
VMEM_LIMIT = 56 * 1024 * 1024
SC_WINDOW = 128
SC_ROW = 256
PIECES = (D_MODEL // 2) // SC_ROW

TM_PROJ = 512
TQ_DSA = 256
TQ_NSA = 512
KC_ATTN = 512
TM_ROUTE = 512
BM_EXPERT = 512

_IN_WIDTHS = (512, 128, 128, 256, 32, 8, 512, 128, 128, 128, 128, 128, 128, 24, 1024, 1024)
_IN_OFFS = np.concatenate([[0], np.cumsum(_IN_WIDTHS)]).tolist()

NT_DIMS = (((1,), (1,)), ((), ()))


def _cparams(sem):
    return pltpu.CompilerParams(dimension_semantics=sem, vmem_limit_bytes=VMEM_LIMIT)


def _sigmoid(x):
    return 1.0 / (1.0 + jnp.exp(-x))


def _dot(a, b):
    return jnp.dot(a, b, preferred_element_type=F32)


def _dot_nt(a, b):
    return lax.dot_general(a, b, NT_DIMS, preferred_element_type=F32)


def _sort_key(x):
    x = jnp.where(x == 0.0, 0.0, x)
    bits = pltpu.bitcast(x, I32)
    return jnp.where(bits < 0, bits ^ 0x7FFFFFFF, bits)


def _kth_largest_key(count_ge, shape, k):
    kf = float(k)
    t0 = jnp.where(count_ge(jnp.zeros(shape, I32)) >= kf, 0, INT_MIN).astype(I32)

    def body(it, t):
        cand = t + jnp.left_shift(jnp.int32(1), 30 - it)
        return jnp.where(count_ge(cand) >= kf, cand, t)

    return lax.fori_loop(0, 31, body, t0)


def _tri_strict_lower(n, dtype):
    r = lax.broadcasted_iota(I32, (n, n), 0)
    c = lax.broadcasted_iota(I32, (n, n), 1)
    return jnp.where(c < r, 1.0, 0.0).astype(dtype)


def _tri_strict_upper(n, dtype):
    r = lax.broadcasted_iota(I32, (n, n), 0)
    c = lax.broadcasted_iota(I32, (n, n), 1)
    return jnp.where(r < c, 1.0, 0.0).astype(dtype)


def _mod_kernel(c_ref, w_ref, b_ref, o_ref):
    c = c_ref[...]
    cond = (c * _sigmoid(c)).astype(BF16)
    o_ref[...] = _dot(cond, w_ref[...].astype(BF16)) + b_ref[...]


def _mod(c, w_ada, b_ada):
    bsz, d = c.shape
    n = w_ada.shape[1]
    tn = 1024
    return pl.pallas_call(
        _mod_kernel,
        out_shape=jax.ShapeDtypeStruct((bsz, n), F32),
        grid=(n // tn,),
        in_specs=[pl.BlockSpec((bsz, d), lambda j: (0, 0)),
                  pl.BlockSpec((d, tn), lambda j: (0, j)),
                  pl.BlockSpec((1, tn), lambda j: (0, j))],
        out_specs=pl.BlockSpec((bsz, tn), lambda j: (0, j)),
        compiler_params=_cparams(("parallel",)),
        name="mod",
    )(c, w_ada, b_ada.reshape(1, n))


def _rope(z, c_tab, s_tab, period, half):
    w = z.shape[1]
    reps = w // LANES
    c = jnp.concatenate([c_tab] * reps, axis=1) if reps > 1 else c_tab
    s = jnp.concatenate([s_tab] * reps, axis=1) if reps > 1 else s_tab
    lane = lax.broadcasted_iota(I32, z.shape, 1)
    first = (lane & (period - 1)) < half
    partner = jnp.where(first, pltpu.roll(z, w - half, axis=1), pltpu.roll(z, half, axis=1))
    return z * c + partner * s


def _in_proj_kernel(x_ref, mod_ref, w_ref, wsm_ref, c64_ref, s64_ref, c32_ref, s32_ref,
                    qa_ref, ka_ref, vaT_ref, qi_ref, ki_ref, wT_ref, qbraw_ref, qbrot_ref,
                    kcmp_ref, vcmp_ref, ksel_ref, vselT_ref, kwin_ref, vwinT_ref, gT_ref, ga_ref, gb_ref):
    mod = mod_ref[0]
    u = (x_ref[...] * (1.0 + mod[1:2, :]) + mod[0:1, :]).astype(BF16)
    tm = u.shape[0]
    c64, s64, c32, s32 = c64_ref[...], s64_ref[...], c32_ref[...], s32_ref[...]
    scale = HEAD_DIM ** -0.5
    lane = lax.broadcasted_iota(I32, (tm, LANES), 1)
    low = lane < HEAD_DIM

    def proj(a, b):
        return _dot(u, w_ref[:, a:b])

    rope64 = lambda z: _rope(z, c64, s64, HEAD_DIM, HEAD_DIM // ROPE_FRACTION // 2)
    rope32 = lambda z: _rope(z, c32, s32, IDX_DIM, IDX_DIM // ROPE_FRACTION // 2)

    def head_slots64(z):
        out = []
        for h in range(A_HEADS):
            pair = z[:, (h // 2) * LANES:(h // 2 + 1) * LANES]
            g = h // REP
            src = pair if h % 2 == g else pltpu.roll(pair, HEAD_DIM, axis=1)
            out.append(jnp.where(low, src, 0.0) if g == 0 else jnp.where(low, 0.0, src))
        return jnp.concatenate(out, axis=1).astype(BF16)

    def head_slots32(z):
        per = LANES // IDX_DIM
        out = []
        for h in range(IDX_HEADS):
            col = z[:, (h // per) * LANES:(h // per + 1) * LANES]
            shift = IDX_DIM * (h % per)
            src = col if shift == 0 else pltpu.roll(col, LANES - shift, axis=1)
            out.append(jnp.where(lane < IDX_DIM, src, 0.0))
        return jnp.concatenate(out, axis=1).astype(BF16)

    def store_vt(ref, z, chunk):
        zt = z.T
        ones = jnp.ones((HEAD_DIM, chunk), F32)
        for g in range(A_KV_HEADS):
            for j in range(tm // chunk):
                blk = zt[g * HEAD_DIM:(g + 1) * HEAD_DIM, j * chunk:(j + 1) * chunk]
                ref[g, j] = jnp.concatenate([blk, ones], axis=0).astype(BF16)

    qa_ref[...] = head_slots64(rope64(proj(0, 512)) * scale)
    ka_ref[...] = rope64(proj(512, 640)).astype(BF16)
    store_vt(vaT_ref, proj(640, 768), tm)
    qi_ref[...] = head_slots32(rope32(proj(768, 1024)))
    ki_ref[...] = rope32(proj(1024, 1152)).astype(BF16)
    qb = proj(1152, 1664)
    qbraw_ref[...] = head_slots64(qb * scale)
    qbrot_ref[...] = head_slots64(rope64(qb) * scale)
    kcmp_ref[...] = proj(1664, 1792)
    vcmp_ref[...] = proj(1792, 1920)
    ksel_ref[...] = rope64(proj(1920, 2048)).astype(BF16)
    store_vt(vselT_ref, proj(2048, 2176), tm)
    kwin_ref[...] = rope64(proj(2176, 2304)).astype(BF16)
    store_vt(vwinT_ref, proj(2304, 2432), LANES)
    ga_ref[...] = _sigmoid(proj(2432, 3456)).astype(BF16)
    gb_ref[...] = _sigmoid(proj(3456, 4480)).astype(BF16)
    small = _dot_nt(wsm_ref[...], u)
    wT_ref[...] = small[0:IDX_HEADS, :]
    gT_ref[...] = _sigmoid(small[IDX_HEADS:IDX_HEADS + N_GATES, :])


def _pack_w_in(w_in):
    d = w_in.shape[0]
    col = lambda i: w_in[:, _IN_OFFS[i]:_IN_OFFS[i + 1]]
    ki = jnp.concatenate([col(4), jnp.zeros((d, LANES - IDX_DIM), w_in.dtype)], axis=1)
    parts = [col(0), col(1), col(2), col(3), ki, col(6), col(7), col(8), col(9), col(10),
             col(11), col(12), col(14), col(15)]
    w_small = jnp.concatenate([col(5), col(13)], axis=1).T
    return jnp.concatenate(parts, axis=1).astype(BF16), w_small.astype(BF16)


def _rope_tables(positions):
    pos = positions.astype(F32).reshape(-1, 1)

    def tab(dim):
        rot = dim // ROPE_FRACTION
        half = rot // 2
        inv = ROPE_THETA ** (-(jnp.arange(half, dtype=F32) * 2.0) / rot)
        ang = pos * inv
        cos, sin = jnp.cos(ang), jnp.sin(ang)
        ones = jnp.ones((pos.shape[0], dim - rot), F32)
        c = jnp.concatenate([cos, cos, ones], axis=1)
        s = jnp.concatenate([-sin, sin, 0.0 * ones], axis=1)
        return jnp.tile(c, (1, LANES // dim)), jnp.tile(s, (1, LANES // dim))

    return tab(HEAD_DIM) + tab(IDX_DIM)


def _in_proj(x2, mod3, w_pack, w_small, tabs, seq, tm):
    t, d = x2.shape
    n = w_pack.shape[1]
    per_b = seq // tm
    g = A_KV_HEADS
    row = lambda w: pl.BlockSpec((tm, w), lambda i: (i, 0))
    tok = lambda r: pl.BlockSpec((r, tm), lambda i: (0, i))
    vt_chunk = pl.BlockSpec((g, 1, LANES, tm), lambda i: (0, i, 0, 0))
    vt_lane = pl.BlockSpec((g, tm // LANES, LANES, LANES), lambda i: (0, i, 0, 0))
    sds = jax.ShapeDtypeStruct
    vt_chunk_shape = sds((g, t // tm, LANES, tm), BF16)
    outs = (("qa", sds((t, 1024), BF16), row(1024)), ("ka", sds((t, LANES), BF16), row(LANES)),
            ("vaT", vt_chunk_shape, vt_chunk), ("qi", sds((t, 1024), BF16), row(1024)),
            ("ki", sds((t, LANES), BF16), row(LANES)), ("wT", sds((IDX_HEADS, t), F32), tok(IDX_HEADS)),
            ("qbraw", sds((t, 1024), BF16), row(1024)), ("qbrot", sds((t, 1024), BF16), row(1024)),
            ("kcmp", sds((t, LANES), F32), row(LANES)), ("vcmp", sds((t, LANES), F32), row(LANES)),
            ("ksel", sds((t, LANES), BF16), row(LANES)), ("vselT", vt_chunk_shape, vt_chunk),
            ("kwin", sds((t, LANES), BF16), row(LANES)),
            ("vwinT", sds((g, t // LANES, LANES, LANES), BF16), vt_lane),
            ("gT", sds((N_GATES, t), F32), tok(N_GATES)),
            ("ga", sds((t, d), BF16), row(d)), ("gb", sds((t, d), BF16), row(d)))
    res = pl.pallas_call(
        _in_proj_kernel,
        out_shape=tuple(o[1] for o in outs),
        grid=(t // tm,),
        in_specs=[row(d),
                  pl.BlockSpec((1, 6, d), lambda i: (i // per_b, 0, 0)),
                  pl.BlockSpec((d, n), lambda i: (0, 0)),
                  pl.BlockSpec(w_small.shape, lambda i: (0, 0)),
                  row(LANES), row(LANES), row(LANES), row(LANES)],
        out_specs=tuple(o[2] for o in outs),
        compiler_params=_cparams(("parallel",)),
        name="in_proj",
    )(x2, mod3, w_pack, w_small, *tabs)
    return {o[0]: r for o, r in zip(outs, res)}


def _fold_rows(x, op):
    n = x.shape[0]
    while n % (2 * SUBLANES) == 0:
        n //= 2
        x = op(x[:n], x[n:])
    slabs = [x[i * SUBLANES:(i + 1) * SUBLANES] for i in range(n // SUBLANES)]
    while len(slabs) > 1:
        nxt = [op(slabs[i], slabs[i + 1]) for i in range(0, len(slabs) - 1, 2)]
        slabs = nxt + ([slabs[-1]] if len(slabs) % 2 else [])
    return slabs[0]


def _col_max(x):
    return jnp.max(_fold_rows(x, jnp.maximum), axis=0, keepdims=True)


def _col_sum(x):
    return jnp.sum(_fold_rows(x, jnp.add), axis=0, keepdims=True)


PACKED_ROWS = 16


def _fold_rows_packed(x):
    n = x.shape[0]
    assert n % PACKED_ROWS == 0 and n // PACKED_ROWS <= 256
    while n > PACKED_ROWS:
        n //= 2
        x = x[:n] + x[n:]
    return x


def _stack_heads(q_ref, heads):
    return jnp.concatenate([q_ref[0, :, h * LANES:(h + 1) * LANES] for h in heads], axis=0)


def _flash_step(k, q_stack, v_t, bias4, m, acc):
    s = _dot_nt(k, q_stack) + bias4
    m_new = jnp.maximum(m, _col_max(s))
    e = jnp.exp(s - m_new).astype(BF16)
    return m_new, acc * jnp.exp(m - m_new) + _dot(v_t, e)


HEADS_PER_CHAIN = REP


def _head_chains():
    return [(h // REP, tuple(range(h, h + HEADS_PER_CHAIN))) for h in range(0, A_HEADS, HEADS_PER_CHAIN)]


def _flash_loop(q_ref, k_ref, vT_ref, bias_of, n_chunks, tq, kc):
    chains = _head_chains()
    q_stacks = [_stack_heads(q_ref, heads) for _, heads in chains]
    width = HEADS_PER_CHAIN * tq

    def body(c, carry):
        k = k_ref[0, pl.ds(pl.multiple_of(c * kc, kc), kc), :]
        out = []
        for (g, _), q_stack, (m, acc) in zip(chains, q_stacks, carry):
            bias = jnp.concatenate([bias_of(g, c)] * HEADS_PER_CHAIN, axis=1)
            out.append(_flash_step(k, q_stack, vT_ref[g, c], bias, m, acc))
        return tuple(out)

    init = tuple((jnp.full((1, width), NEG, F32), jnp.zeros((LANES, width), F32)) for _ in chains)
    res = lax.fori_loop(0, n_chunks, body, init)
    return [(heads, acc) for (_, heads), (_, acc) in zip(chains, res)]


def _normalise(acc):
    return acc / jnp.maximum(acc[HEAD_DIM:HEAD_DIM + 1, :], 1e-30)


def _store_heads(o_ref, o_t, heads, tq):
    o = o_t.T
    for r, h in enumerate(heads):
        o_ref[0, :, h * HEAD_DIM:(h + 1) * HEAD_DIM] = o[r * tq:(r + 1) * tq, 0:HEAD_DIM].astype(o_ref.dtype)


def _select_mask(keys, thr, need, offset, tri):
    gt = jnp.where(keys > thr, 1.0, 0.0)
    eq = jnp.where(keys == thr, 1.0, 0.0)
    prefix = _dot(tri, eq.astype(BF16)) + offset
    return gt + jnp.where(prefix < need, eq, 0.0), offset + _col_sum(eq)


def _dsa_kernel(qi_ref, wT_ref, ki_ref, qa_ref, ka_ref, vT_ref, o_ref, keys_ref, bias_ref, planes_ref,
                live_ref, *, tq, kc, n_keep):
    t0 = pl.program_id(1) * tq
    n_chunks = (t0 + tq + kc - 1) // kc
    wT = wT_ref[...] * (IDX_HEADS ** -0.5 * IDX_DIM ** -0.5)
    qi_stack = _stack_heads(qi_ref, range(IDX_HEADS))
    keypos = lax.broadcasted_iota(I32, (kc, tq), 0)
    qpos = t0 + lax.broadcasted_iota(I32, (kc, tq), 1)

    def score_chunk(c, carry):
        k0 = pl.multiple_of(c * kc, kc)
        sc = _dot_nt(ki_ref[0, pl.ds(k0, kc), :], qi_stack)
        score = jnp.zeros((kc, tq), F32)
        for h in range(IDX_HEADS):
            score = score + wT[h:h + 1, :] * jnp.maximum(sc[:, h * tq:(h + 1) * tq], 0.0)
        key = jnp.where(k0 + keypos <= qpos, _sort_key(score), INT_MIN)
        keys_ref[c] = key
        flipped = key ^ INT_MIN
        for lvl in range(4):
            byte = lax.shift_right_logical(flipped, 8 * lvl) & 0xFF
            planes_ref[lvl, c] = byte.astype(F32).astype(BF16)
        live_ref[c] = jnp.ones((kc, tq), BF16)
        return carry

    lax.fori_loop(0, n_chunks, score_chunk, 0)

    def masked_count(lvl, pick):
        def body(c, acc):
            hit = jnp.where(pick(planes_ref[lvl, c]), live_ref[c], jnp.zeros((kc, tq), BF16))
            return acc + _fold_rows_packed(hit).astype(F32)
        acc = lax.fori_loop(0, n_chunks, body, jnp.zeros((PACKED_ROWS, tq), F32))
        return jnp.sum(acc, axis=0, keepdims=True)

    above = jnp.zeros((1, tq), F32)
    thr_u = jnp.zeros((1, tq), I32)
    for lvl in (3, 2, 1, 0):
        def bit_step(it, t, lvl=lvl, above=above):
            cand = t + jnp.left_shift(jnp.int32(1), 7 - it).astype(F32)
            cnt = above + masked_count(lvl, lambda p: p >= cand.astype(BF16))
            return jnp.where(cnt >= float(n_keep), cand, t)

        t = lax.fori_loop(0, 8, bit_step, jnp.zeros((1, tq), F32))
        tb = t.astype(BF16)
        above = above + masked_count(lvl, lambda p: p > tb)

        def narrow(c, carry, lvl=lvl, tb=tb):
            live_ref[c] = jnp.where(planes_ref[lvl, c] == tb, live_ref[c], jnp.zeros((kc, tq), BF16))
            return carry

        lax.fori_loop(0, n_chunks, narrow, 0)
        thr_u = thr_u | jnp.left_shift(t.astype(I32), 8 * lvl)
    thr = thr_u ^ INT_MIN
    need = float(n_keep) - above
    tri = _tri_strict_lower(LANES, BF16)
    sub = lax.broadcasted_iota(I32, (LANES, tq), 0)
    qsub = t0 + lax.broadcasted_iota(I32, (LANES, tq), 1)

    def bias_chunk(c, offset):
        for j in range(kc // LANES):
            rows = slice(j * LANES, (j + 1) * LANES)
            sel, offset = _select_mask(keys_ref[c, rows, :], thr, need, offset, tri)
            causal = c * kc + j * LANES + sub <= qsub
            bias_ref[c, rows, :] = jnp.where(causal, (sel - 1.0) * 1e30, NEG)
        return offset

    lax.fori_loop(0, n_chunks, bias_chunk, jnp.zeros((1, tq), F32))

    accs = _flash_loop(qa_ref, ka_ref, vT_ref, lambda g, c: bias_ref[c], n_chunks, tq, kc)
    for heads, acc in accs:
        _store_heads(o_ref, _normalise(acc), heads, tq)


def _dsa(qi, wT, ki, qa, ka, vaT, bsz, seq, tq, kc):
    n_keep = min(DSA_TOPK_MAX, seq // 4)
    nq = seq // tq
    qblk = lambda w: pl.BlockSpec((1, tq, w), lambda b, i: (b, i, 0))
    full = lambda w: pl.BlockSpec((1, seq, w), lambda b, i: (b, 0, 0))
    return pl.pallas_call(
        functools.partial(_dsa_kernel, tq=tq, kc=kc, n_keep=n_keep),
        out_shape=jax.ShapeDtypeStruct((bsz, seq, A_HEADS * HEAD_DIM), BF16),
        grid=(bsz, nq),
        in_specs=[qblk(1024), pl.BlockSpec((IDX_HEADS, tq), lambda b, i: (0, b * nq + i)), full(LANES),
                  qblk(1024), full(LANES),
                  pl.BlockSpec((A_KV_HEADS, seq // kc, LANES, kc), lambda b, i: (0, b, 0, 0))],
        out_specs=qblk(512),
        scratch_shapes=[pltpu.VMEM((seq // kc, kc, tq), I32), pltpu.VMEM((seq // kc, kc, tq), F32),
                        pltpu.VMEM((4, seq // kc, kc, tq), BF16), pltpu.VMEM((seq // kc, kc, tq), BF16)],
        compiler_params=_cparams(("parallel", "parallel")),
        name="dsa",
    )(qi, wT, ki, qa, ka, vaT)


def _gelu_tanh(x):
    return 0.5 * x * (1.0 + jnp.tanh(math.sqrt(2.0 / math.pi) * (x + 0.044715 * (x * x * x))))


def _compress_kernel(k_ref, v_ref, pk_ref, pv_ref, w1k_ref, w2k_ref, w1v_ref, w2v_ref,
                     kc_ref, vcT_ref, *, n_rows):
    half = CMP_BLOCK // 2

    def one(x_ref, p_ref, w1_ref, w2_ref):
        outs = []
        for g in range(B_KV_HEADS):
            lo = jnp.zeros((n_rows, CMP_HIDDEN), F32)
            hi = jnp.zeros((n_rows, CMP_HIDDEN), F32)
            for l in range(half):
                xl = x_ref[0, :, l * LANES + g * HEAD_DIM:l * LANES + (g + 1) * HEAD_DIM]
                a = (xl + p_ref[l:l + 1, :]).astype(BF16)
                b = (xl + p_ref[half + l:half + l + 1, :]).astype(BF16)
                lo = lo + _dot(a, w1_ref[l * HEAD_DIM:(l + 1) * HEAD_DIM, :].astype(BF16))
                hi = hi + _dot(b, w1_ref[(half + l) * HEAD_DIM:(half + l + 1) * HEAD_DIM, :].astype(BF16))
            hid = lo + pltpu.roll(hi, n_rows - 1, axis=0)
            outs.append(_dot(_gelu_tanh(hid).astype(BF16), w2_ref[...].astype(BF16)))
        return outs

    k0, k1 = one(k_ref, pk_ref, w1k_ref, w2k_ref)
    kc_ref[0] = jnp.concatenate([k0, k1], axis=1).astype(kc_ref.dtype)
    for g, v in enumerate(one(v_ref, pv_ref, w1v_ref, w2v_ref)):
        vcT_ref[0, g] = jnp.concatenate([v, jnp.zeros_like(v)], axis=1).T.astype(vcT_ref.dtype)


def _compress(kcmp, vcmp, pos_k, pos_v, w1k, w2k, w1v, w2v):
    bsz, n_rows, width = kcmp.shape
    xspec = pl.BlockSpec((1, n_rows, width), lambda b: (b, 0, 0))
    cst = lambda a: pl.BlockSpec(a.shape, lambda b: (0,) * a.ndim)
    return pl.pallas_call(
        functools.partial(_compress_kernel, n_rows=n_rows),
        out_shape=(jax.ShapeDtypeStruct((bsz, n_rows, LANES), BF16),
                   jax.ShapeDtypeStruct((bsz, B_KV_HEADS, LANES, n_rows), BF16)),
        grid=(bsz,),
        in_specs=[xspec, xspec, cst(pos_k), cst(pos_v), cst(w1k), cst(w2k), cst(w1v), cst(w2v)],
        out_specs=(pl.BlockSpec((1, n_rows, LANES), lambda b: (b, 0, 0)),
                   pl.BlockSpec((1, B_KV_HEADS, LANES, n_rows), lambda b: (b, 0, 0, 0))),
        compiler_params=_cparams(("parallel",)),
        name="compress",
    )(kcmp, vcmp, pos_k, pos_v, w1k, w2k, w1v, w2v)


def _split3(x):
    a = x.astype(BF16)
    r = x - a.astype(F32)
    b = r.astype(BF16)
    c = (r - b.astype(F32)).astype(BF16)
    return a, b, c


def _nsa_kernel(qraw_ref, qrot_ref, kc_ref, vcT_ref, ksel_ref, vselT_ref, kwin_ref, vwinT_ref,
                gT_ref, exp_ref, o_ref, bias_ref, *, tq, kc, seq, n_c):
    t0 = pl.program_id(1) * tq
    n_chunks = (t0 + tq + kc - 1) // kc
    n_s = seq // SEL_BLOCK
    n_pick = min(SEL_COUNT, n_s)
    span = WINDOW + tq
    gT = gT_ref[...]

    ridx = lax.broadcasted_iota(I32, (LANES, tq), 0)
    tl = t0 + lax.broadcasted_iota(I32, (LANES, tq), 1)
    valid_c = jnp.where((ridx * CMP_STRIDE + CMP_BLOCK - 1 <= tl) & (ridx < n_c), 1.0, 0.0)
    valid4 = jnp.concatenate([valid_c] * REP, axis=1) > 0.0
    js = lax.broadcasted_iota(I32, (LANES, LANES), 0) * SEL_BLOCK
    cs = lax.broadcasted_iota(I32, (LANES, LANES), 1) * CMP_STRIDE
    overlap_t = jnp.where((cs <= js + SEL_BLOCK - 1) & (cs + CMP_BLOCK - 1 >= js), 1.0, 0.0).astype(BF16)
    cur = tl // SEL_BLOCK
    forced = (ridx == 0) | ((cur - ridx >= 0) & (cur - ridx < SEL_LOCAL))
    blk_causal = ridx * SEL_BLOCK <= tl
    tri = _tri_strict_lower(LANES, BF16)
    keypos = lax.broadcasted_iota(I32, (kc, tq), 0)
    qpos = t0 + lax.broadcasted_iota(I32, (kc, tq), 1)
    w0 = pl.multiple_of(jnp.clip(t0 - WINDOW, 0, seq - span), tq)
    wdiff = (t0 + lax.broadcasted_iota(I32, (span, tq), 1)) - (w0 + lax.broadcasted_iota(I32, (span, tq), 0))
    wbias = jnp.where((wdiff >= 0) & (wdiff < WINDOW), 0.0, NEG)
    wbias_c = jnp.concatenate([wbias] * HEADS_PER_CHAIN, axis=1)
    wblk = w0 // LANES

    o_cmp_g = []
    for g in range(B_KV_HEADS):
        heads = range(g * REP, (g + 1) * REP)
        s = jnp.where(valid4, _dot_nt(kc_ref[0], _stack_heads(qraw_ref, heads)), NEG)
        e = jnp.where(valid4, jnp.exp(s - _col_max(s)), 0.0)
        p = e / jnp.maximum(_col_sum(e), 1e-30)
        o_cmp_g.append(_dot(vcT_ref[0, g], p.astype(BF16)))
        p_sum = p[:, 0:tq]
        for r in range(1, REP):
            p_sum = p_sum + p[:, r * tq:(r + 1) * tq]
        pa, pb, pc = _split3(p_sum)
        imp = _dot(overlap_t, pa) + _dot(overlap_t, pb) + _dot(overlap_t, pc)
        imp = jnp.where(forced, FORCE, jnp.where(blk_causal, imp, NEG))
        keys = _sort_key(imp[0:n_s, :])
        count_ge = lambda cand, keys=keys: _col_sum(jnp.where(keys >= cand, 1.0, 0.0))
        thr = _kth_largest_key(count_ge, (1, tq), n_pick)
        need = float(n_pick) - _col_sum(jnp.where(keys > thr, 1.0, 0.0))
        blk_sel, _ = _select_mask(keys, thr, need, jnp.zeros((1, tq), F32), tri[0:n_s, 0:n_s])
        blk_sel = jnp.concatenate([blk_sel, jnp.zeros((LANES - n_s, tq), F32)], axis=0).astype(BF16)

        def bias_chunk(c, carry, blk_sel=blk_sel, g=g):
            tok_sel = _dot(exp_ref[c], blk_sel)
            bias_ref[g, c] = jnp.where(c * kc + keypos <= qpos, (tok_sel - 1.0) * 1e30, NEG)
            return carry

        lax.fori_loop(0, n_chunks, bias_chunk, 0)

    sel = _flash_loop(qrot_ref, ksel_ref, vselT_ref, lambda g, c: bias_ref[g, c], n_chunks, tq, kc)

    kwin = kwin_ref[0, pl.ds(w0, span), :]
    for (g, heads), (_, acc_sel) in zip(_head_chains(), sel):
        sw = _dot_nt(kwin, _stack_heads(qrot_ref, heads)) + wbias_c
        ew = jnp.exp(sw - _col_max(sw)).astype(BF16)
        acc_win = _dot(vwinT_ref[g, wblk], ew[0:LANES, :])
        for j in range(1, span // LANES):
            acc_win = acc_win + _dot(vwinT_ref[g, wblk + j], ew[j * LANES:(j + 1) * LANES, :])

        o_sel = _normalise(acc_sel)
        o_win = _normalise(acc_win)
        cols = []
        for r, h in enumerate(heads):
            col = slice(r * tq, (r + 1) * tq)
            cmp_col = slice((h % REP) * tq, (h % REP + 1) * tq)
            cols.append(gT[3 * h:3 * h + 1, :] * o_cmp_g[g][:, cmp_col]
                        + gT[3 * h + 1:3 * h + 2, :] * o_sel[:, col]
                        + gT[3 * h + 2:3 * h + 3, :] * o_win[:, col])
        _store_heads(o_ref, jnp.concatenate(cols, axis=1), heads, tq)


def _nsa(qraw, qrot, kc_x, vcT, ksel, vselT, kwin, vwinT, gT, bsz, seq, tq, kc):
    n_c = (seq - CMP_BLOCK) // CMP_STRIDE + 1
    nq = seq // tq
    key_blk = (jnp.arange(seq, dtype=I32) // SEL_BLOCK).reshape(seq // kc, kc, 1)
    expand = (key_blk == jnp.arange(LANES, dtype=I32).reshape(1, 1, LANES)).astype(BF16)
    qblk = lambda w: pl.BlockSpec((1, tq, w), lambda b, i: (b, i, 0))
    full = lambda w: pl.BlockSpec((1, seq, w), lambda b, i: (b, 0, 0))
    g = B_KV_HEADS
    return pl.pallas_call(
        functools.partial(_nsa_kernel, tq=tq, kc=kc, seq=seq, n_c=n_c),
        out_shape=jax.ShapeDtypeStruct((bsz, seq, B_HEADS * HEAD_DIM), BF16),
        grid=(bsz, nq),
        in_specs=[qblk(1024), qblk(1024),
                  pl.BlockSpec((1,) + kc_x.shape[1:], lambda b, i: (b, 0, 0)),
                  pl.BlockSpec((1,) + vcT.shape[1:], lambda b, i: (b, 0, 0, 0)),
                  full(LANES), pl.BlockSpec((g, seq // kc, LANES, kc), lambda b, i: (0, b, 0, 0)),
                  full(LANES), pl.BlockSpec((g, seq // LANES, LANES, LANES), lambda b, i: (0, b, 0, 0)),
                  pl.BlockSpec((N_GATES, tq), lambda b, i: (0, b * nq + i)),
                  pl.BlockSpec(expand.shape, lambda b, i: (0, 0, 0))],
        out_specs=qblk(512),
        scratch_shapes=[pltpu.VMEM((g, seq // kc, kc, tq), F32)],
        compiler_params=_cparams(("parallel", "parallel")),
        name="nsa",
    )(qraw, qrot, kc_x, vcT, ksel, vselT, kwin, vwinT, gT, expand)


def _pack_pairs(x):
    n = x.shape[1] // 2
    lo = pltpu.bitcast(x[:, :n].astype(BF16).astype(F32), I32)
    hi = pltpu.bitcast(x[:, n:].astype(BF16).astype(F32), I32)
    return lax.shift_right_logical(lo, 16) | (hi & jnp.int32(-65536))


def _unpack_pairs(p):
    lo = pltpu.bitcast(lax.shift_left(p, 16), F32)
    hi = pltpu.bitcast(p & jnp.int32(-65536), F32)
    return jnp.concatenate([lo, hi], axis=1)


def _layer_norm(y, g, b):
    mu = jnp.mean(y, axis=1, keepdims=True)
    yc = y - mu
    var = jnp.mean(yc * yc, axis=1, keepdims=True)
    return yc * lax.rsqrt(var + LN_EPS) * g + b


def _out_proj_kernel(oa_ref, ob_ref, ga_ref, gb_ref, x_ref, mod_ref, wa_ref, wb_ref, wo_ref,
                     g1_ref, b1_ref, wrh_ref, wrl_ref, x1_ref, u2_ref, lg_ref):
    mod = mod_ref[0]
    merged = (ga_ref[...].astype(F32) * _dot(oa_ref[...], wa_ref[...])
              + gb_ref[...].astype(F32) * _dot(ob_ref[...], wb_ref[...]))
    mix = _dot(merged.astype(BF16), wo_ref[...])
    x1 = _layer_norm(DN_ALPHA * x_ref[...] + mod[2:3, :] * mix, g1_ref[...], b1_ref[...])
    x1_ref[...] = x1
    u2 = x1 * (1.0 + mod[4:5, :]) + mod[3:4, :]
    packed = _pack_pairs(u2)
    for j in range(PIECES):
        u2_ref[j] = packed[:, j * SC_ROW:(j + 1) * SC_ROW]
    uh = u2.astype(BF16)
    ul = (u2 - uh.astype(F32)).astype(BF16)
    lg_ref[...] = _dot_nt(wrh_ref[...], uh) + _dot_nt(wrh_ref[...], ul) + _dot_nt(wrl_ref[...], uh)


def _out_proj(oa, ob, ga, gb, x2, mod3, wa, wb, wo, g1, b1, wrh, wrl, seq, tm):
    t, d = x2.shape
    per_b = seq // tm
    row = lambda w: pl.BlockSpec((tm, w), lambda i: (i, 0))
    cst = lambda a: pl.BlockSpec(a.shape, lambda i: (0,) * a.ndim)
    return pl.pallas_call(
        _out_proj_kernel,
        out_shape=(jax.ShapeDtypeStruct((t, d), F32), jax.ShapeDtypeStruct((PIECES, t, SC_ROW), I32),
                   jax.ShapeDtypeStruct((N_EXPERTS, t), F32)),
        grid=(t // tm,),
        in_specs=[row(512), row(512), row(d), row(d), row(d),
                  pl.BlockSpec((1, 6, d), lambda i: (i // per_b, 0, 0)),
                  cst(wa), cst(wb), cst(wo), cst(g1), cst(b1), cst(wrh), cst(wrl)],
        out_specs=(row(d), pl.BlockSpec((PIECES, tm, SC_ROW), lambda i: (0, i, 0)),
                   pl.BlockSpec((N_EXPERTS, tm), lambda i: (0, i))),
        compiler_params=_cparams(("parallel",)),
        name="out_proj",
    )(oa, ob, ga, gb, x2, mod3, wa, wb, wo, g1, b1, wrh, wrl)


def _first_max(x, rows):
    m = jnp.max(x, axis=0, keepdims=True)
    idx = jnp.min(jnp.where(x == m, rows, 1e9), axis=0, keepdims=True)
    return m, idx


def _router_kernel(lg_ref, rb_ref, idx_ref, w_ref, pos_ref, cnt_ref, carry_ref, *, tm):
    @pl.when(pl.program_id(0) == 0)
    def _():
        carry_ref[...] = jnp.zeros_like(carry_ref)

    per_g = N_EXPERTS // N_GROUPS
    scores = _sigmoid(lg_ref[...])
    choice = scores + rb_ref[...][:, 0:1]
    rows = lax.broadcasted_iota(I32, (N_EXPERTS, tm), 0).astype(F32)
    rows_g = lax.broadcasted_iota(I32, (per_g, tm), 0).astype(F32)
    ninf = -jnp.inf

    gs = []
    for g in range(N_GROUPS):
        x = choice[g * per_g:(g + 1) * per_g, :]
        m1, i1 = _first_max(x, rows_g)
        m2 = jnp.max(jnp.where(rows_g == i1, ninf, x), axis=0, keepdims=True)
        gs.append(m1 + m2)
    gscore = jnp.concatenate(gs, axis=0)
    rows8 = lax.broadcasted_iota(I32, (N_GROUPS, tm), 0).astype(F32)
    keep = jnp.zeros((N_GROUPS, tm), F32)
    for _ in range(TOPK_GROUPS):
        _, gi = _first_max(gscore, rows8)
        hit = rows8 == gi
        keep = jnp.where(hit, 1.0, keep)
        gscore = jnp.where(hit, ninf, gscore)
    keep_full = jnp.concatenate(
        [jnp.broadcast_to(keep[g:g + 1, :], (per_g, tm)) for g in range(N_GROUPS)], axis=0)
    masked = jnp.where(keep_full > 0.0, choice, NEG)

    idxs, ws = [], []
    onehot = jnp.zeros((N_EXPERTS, tm), F32)
    for _ in range(TOP_K):
        _, ei = _first_max(masked, rows)
        hit = rows == ei
        idxs.append(ei)
        ws.append(jnp.sum(jnp.where(hit, scores, 0.0), axis=0, keepdims=True))
        masked = jnp.where(hit, ninf, masked)
        onehot = jnp.where(hit, 1.0, onehot)
    idx = jnp.concatenate(idxs, axis=0)
    w = jnp.concatenate(ws, axis=0)
    idx_ref[...] = idx.astype(I32)
    w = w / jnp.sum(w, axis=0, keepdims=True) * ROUTED_SCALE
    w_ref[...] = jnp.concatenate([w, jnp.zeros((LANES - TOP_K, tm), F32)], axis=0).T

    tri = _tri_strict_upper(tm, BF16)
    base = _dot(onehot.astype(BF16), tri) + carry_ref[...][:, 0:1]
    pos = [jnp.sum(jnp.where(rows == idxs[k], base, 0.0), axis=0, keepdims=True) for k in range(TOP_K)]
    pos_ref[...] = jnp.concatenate(pos, axis=0).astype(I32)
    carry = carry_ref[...] + jnp.sum(onehot, axis=1, keepdims=True)
    carry_ref[...] = carry
    cnt_ref[...] = carry


def _router(lg, router_bias, tm):
    e, t = lg.shape
    rb = jnp.broadcast_to(router_bias.reshape(e, 1).astype(F32), (e, LANES))
    tok = lambda r: pl.BlockSpec((r, tm), lambda i: (0, i))
    return pl.pallas_call(
        functools.partial(_router_kernel, tm=tm),
        out_shape=(jax.ShapeDtypeStruct((TOP_K, t), I32), jax.ShapeDtypeStruct((t, LANES), F32),
                   jax.ShapeDtypeStruct((TOP_K, t), I32), jax.ShapeDtypeStruct((e, LANES), F32)),
        grid=(t // tm,),
        in_specs=[tok(e), pl.BlockSpec((e, LANES), lambda i: (0, 0))],
        out_specs=(tok(TOP_K), pl.BlockSpec((tm, LANES), lambda i: (i, 0)), tok(TOP_K),
                   pl.BlockSpec((e, LANES), lambda i: (0, 0))),
        scratch_shapes=[pltpu.VMEM((e, LANES), F32)],
        compiler_params=_cparams(("arbitrary",)),
        name="router",
    )(lg, rb)


def _dest_kernel(idx_ref, pos_ref, st_ref, dest_ref, *, tm):
    rows = lax.broadcasted_iota(I32, (N_EXPERTS, tm), 0)
    starts = st_ref[...][:, 0:1]
    idx = idx_ref[...]
    out = []
    for k in range(TOP_K):
        out.append(jnp.sum(jnp.where(rows == idx[k:k + 1, :], starts, 0.0), axis=0, keepdims=True))
    dest_ref[...] = jnp.concatenate(out, axis=0).astype(I32) + pos_ref[...]


def _dest(idx, pos, starts, tm):
    k, t = idx.shape
    st = jnp.broadcast_to(starts.reshape(N_EXPERTS, 1).astype(F32), (N_EXPERTS, LANES))
    tok = pl.BlockSpec((k, tm), lambda i: (0, i))
    return pl.pallas_call(
        functools.partial(_dest_kernel, tm=tm),
        out_shape=jax.ShapeDtypeStruct((k, t), I32),
        grid=(t // tm,),
        in_specs=[tok, tok, pl.BlockSpec((N_EXPERTS, LANES), lambda i: (0, 0))],
        out_specs=tok,
        compiler_params=_cparams(("parallel",)),
        name="dest",
    )(idx, pos, st)


def _experts_kernel(blk_ref, used_ref, first_ref, slot_ref, next_ref, x_ref, wg_hbm, wu_hbm, wd_hbm, y_ref,
                    wg_buf, wu_buf, wd_buf, wg_bf, wu_bf, wd_bf, sems):
    b = pl.program_id(0)
    active = b < used_ref[0]

    def weight_copies(e, slot):
        return [pltpu.make_async_copy(hbm.at[e], buf.at[slot], sems.at[slot, i])
                for i, (hbm, buf) in enumerate(((wg_hbm, wg_buf), (wu_hbm, wu_buf), (wd_hbm, wd_buf)))]

    @pl.when(b == 0)
    def _():
        for cp in weight_copies(blk_ref[0], 0):
            cp.start()

    @pl.when(active & (first_ref[b] == 1))
    def _():
        slot = slot_ref[b]
        for cp in weight_copies(blk_ref[b], slot):
            cp.wait()
        nxt = next_ref[b]

        @pl.when(nxt >= 0)
        def _():
            for cp in weight_copies(nxt, 1 - slot):
                cp.start()

        wg_bf[...] = wg_buf[slot].astype(BF16)
        wu_bf[...] = wu_buf[slot].astype(BF16)
        wd_bf[...] = wd_buf[slot].astype(BF16)

    @pl.when(active)
    def _():
        x = jnp.concatenate([x_ref[j] for j in range(PIECES)], axis=1)
        x = _unpack_pairs(x).astype(BF16)
        a = _dot(x, wg_bf[...])
        u = _dot(x, wu_bf[...])
        h = (a * _sigmoid(a) * u).astype(BF16)
        y = _pack_pairs(_dot(h, wd_bf[...]))
        for j in range(PIECES):
            y_ref[j] = y[:, j * SC_ROW:(j + 1) * SC_ROW]


def _experts(xs, plan, wg, wu, wd, bm):
    _, cap, _ = xs.shape
    n_blocks = cap // bm
    d, f = wg.shape[1], wg.shape[2]
    rows = pl.BlockSpec((PIECES, bm, SC_ROW), lambda b, blk, used, *_: (0, jnp.minimum(b, used[0] - 1), 0))
    hbm = pl.BlockSpec(memory_space=pl.ANY)
    return pl.pallas_call(
        _experts_kernel,
        out_shape=jax.ShapeDtypeStruct(xs.shape, I32),
        grid_spec=pltpu.PrefetchScalarGridSpec(
            num_scalar_prefetch=5,
            grid=(n_blocks,),
            in_specs=[rows, hbm, hbm, hbm],
            out_specs=rows,
            scratch_shapes=[pltpu.VMEM((2, d, f), F32), pltpu.VMEM((2, d, f), F32), pltpu.VMEM((2, f, d), F32),
                            pltpu.VMEM((d, f), BF16), pltpu.VMEM((d, f), BF16), pltpu.VMEM((f, d), BF16),
                            pltpu.SemaphoreType.DMA((2, 3))]),
        compiler_params=_cparams(("arbitrary",)),
        name="experts",
    )(plan["blk_e"], plan["n_used"], plan["first"], plan["slot"], plan["next_e"], xs, wg, wu, wd)


def _final_kernel(x1_ref, u2_ref, yg_ref, w_ref, mod_ref, sg_ref, su_ref, sd_ref, g2_ref, b2_ref, o_ref):
    mod = mod_ref[0]
    w = w_ref[...]
    rows = lambda ref, *lead: jnp.concatenate([ref[(j,) + lead] for j in range(PIECES)], axis=1)
    routed = w[:, 0:1] * _unpack_pairs(rows(yg_ref, 0))
    for k in range(1, TOP_K):
        routed = routed + w[:, k:k + 1] * _unpack_pairs(rows(yg_ref, k))
    u = _unpack_pairs(rows(u2_ref)).astype(BF16)
    a = _dot(u, sg_ref[...])
    b = _dot(u, su_ref[...])
    shared = _dot((a * _sigmoid(a) * b).astype(BF16), sd_ref[...])
    y = DN_ALPHA * x1_ref[...] + mod[5:6, :] * (routed + shared)
    o_ref[...] = _layer_norm(y, g2_ref[...], b2_ref[...])


def _final(x1, u2p, yg, wtok, mod3, sg, su, sd, g2, b2, seq, tm):
    t, d = x1.shape
    per_b = seq // tm
    row = lambda w: pl.BlockSpec((tm, w), lambda i: (i, 0))
    cst = lambda a: pl.BlockSpec(a.shape, lambda i: (0,) * a.ndim)
    return pl.pallas_call(
        _final_kernel,
        out_shape=jax.ShapeDtypeStruct((t, d), F32),
        grid=(t // tm,),
        in_specs=[row(d), pl.BlockSpec((PIECES, tm, SC_ROW), lambda i: (0, i, 0)),
                  pl.BlockSpec((PIECES, TOP_K, tm, SC_ROW), lambda i: (0, 0, i, 0)),
                  row(LANES), pl.BlockSpec((1, 6, d), lambda i: (i // per_b, 0, 0)),
                  cst(sg), cst(su), cst(sd), cst(g2), cst(b2)],
        out_specs=row(d),
        compiler_params=_cparams(("parallel",)),
        name="final",
    )(x1, u2p, yg, wtok, mod3, sg, su, sd, g2, b2)


def _sc_mesh():
    return plsc.VectorSubcoreMesh(core_axis_name="core", subcore_axis_name="subcore")


def _sc_scatter_rows(src, idx, n_out, src_block):
    n_idx = idx.shape[0]

    @functools.partial(pl.kernel, out_type=jax.ShapeDtypeStruct((n_out, SC_ROW), src.dtype),
                       mesh=_sc_mesh(), scratch_types=[])
    def k(x_hbm, i_hbm, o_hbm):
        def body(x_vmem, i_vmem):
            pltpu.sync_copy(x_vmem, o_hbm.at[i_vmem.at[0]])

        pltpu.emit_pipeline(
            body, grid=(n_idx // SC_WINDOW,),
            in_specs=[pl.BlockSpec((SC_WINDOW, SC_ROW), lambda i: (src_block(i), 0)),
                      pl.BlockSpec((1, SC_WINDOW), lambda i: (0, i))],
            out_specs=[],
            core_axis_name=("core", "subcore"),
            dimension_semantics=(pltpu.PARALLEL,),
        )(x_hbm, i_hbm)

    return k(src, idx.reshape(1, n_idx))


def _sc_gather_rows(src, idx):
    n_idx = idx.shape[0]

    @functools.partial(pl.kernel, out_type=jax.ShapeDtypeStruct((n_idx, SC_ROW), src.dtype),
                       mesh=_sc_mesh(), scratch_types=[])
    def k(x_hbm, i_hbm, o_hbm):
        def body(i_vmem, o_vmem):
            pltpu.sync_copy(x_hbm.at[i_vmem.at[0]], o_vmem)

        pltpu.emit_pipeline(
            body, grid=(n_idx // SC_WINDOW,),
            in_specs=[pl.BlockSpec((1, SC_WINDOW), lambda i: (0, i))],
            out_specs=[pl.BlockSpec((SC_WINDOW, SC_ROW), lambda i: (i, 0))],
            core_axis_name=("core", "subcore"),
            dimension_semantics=(pltpu.PARALLEL,),
        )(i_hbm, o_hbm)

    return k(src, idx.reshape(1, n_idx))


def _moe_plan(counts, n_tok):
    bm = BM_EXPERT
    padded = (counts + bm - 1) // bm * bm
    p_ends = jnp.cumsum(padded)
    starts = p_ends - padded
    n_blocks = n_tok * TOP_K // bm + N_EXPERTS
    blk = jnp.arange(n_blocks, dtype=I32)
    blk_e = jnp.minimum(jnp.sum(p_ends[None, :] <= (blk * bm)[:, None], axis=1), N_EXPERTS - 1).astype(I32)
    n_used = (p_ends[-1] // bm).astype(I32)
    prev = jnp.concatenate([jnp.full((1,), -1, I32), blk_e[:-1]])
    first = ((blk_e != prev) & (blk < n_used)).astype(I32)
    slot = (jnp.cumsum(first) - 1) % 2
    first_pos = jnp.where(first == 1, blk, n_blocks)
    next_first = lax.cummin(jnp.concatenate([first_pos[1:], jnp.full((1,), n_blocks, I32)]), reverse=True)
    next_e = jnp.where(next_first < n_blocks, blk_e[jnp.minimum(next_first, n_blocks - 1)], -1)
    plan = dict(blk_e=blk_e, n_used=n_used.reshape(1), first=first, slot=slot.astype(I32),
                next_e=next_e.astype(I32))
    return starts, plan, n_blocks


def _layer(x, mod, positions, w_in, w_br_a, w_br_b, w_out, cmp_pos_k, cmp_pos_v, cmp_k_w1, cmp_k_w2,
           cmp_v_w1, cmp_v_w2, ln1_g, ln1_b, w_router, router_bias, w_exp_gate, w_exp_up, w_exp_down,
           w_sh_gate, w_sh_up, w_sh_down, ln2_g, ln2_b):
    bsz, seq, d = x.shape
    t = bsz * seq
    assert seq // CMP_STRIDE == LANES and seq % TQ_NSA == 0 and seq % TQ_DSA == 0
    assert seq % TM_PROJ == 0 and KC_ATTN == TM_PROJ
    x2 = x.reshape(t, d)
    mod3 = mod.reshape(bsz, 6, d)

    w_pack, w_small = _pack_w_in(w_in)
    z = _in_proj(x2, mod3, w_pack, w_small, _rope_tables(positions), seq, TM_PROJ)
    per_b = lambda name: z[name].reshape(bsz, seq, z[name].shape[1])

    o_a = _dsa(per_b("qi"), z["wT"], per_b("ki"), per_b("qa"), per_b("ka"), z["vaT"],
               bsz, seq, TQ_DSA, KC_ATTN)

    n_rows = seq // CMP_STRIDE
    kc_x, vcT = _compress(z["kcmp"].reshape(bsz, n_rows, CMP_STRIDE * LANES),
                          z["vcmp"].reshape(bsz, n_rows, CMP_STRIDE * LANES),
                          cmp_pos_k, cmp_pos_v, cmp_k_w1, cmp_k_w2, cmp_v_w1, cmp_v_w2)
    o_b = _nsa(per_b("qbraw"), per_b("qbrot"), kc_x, vcT, per_b("ksel"), z["vselT"], per_b("kwin"),
               z["vwinT"], z["gT"], bsz, seq, TQ_NSA, KC_ATTN)

    wr_hi = w_router.T.astype(BF16)
    wr_lo = (w_router.T - wr_hi.astype(F32)).astype(BF16)
    x1, u2p, logits = _out_proj(
        o_a.reshape(t, -1), o_b.reshape(t, -1), z["ga"], z["gb"], x2, mod3,
        w_br_a.astype(BF16), w_br_b.astype(BF16), w_out.astype(BF16),
        ln1_g.reshape(1, d), ln1_b.reshape(1, d), wr_hi, wr_lo, seq, TM_PROJ)

    idx, wtok, pos, counts = _router(logits, router_bias, TM_ROUTE)
    starts, plan, n_blocks = _moe_plan(counts[:, 0].astype(I32), t)
    dest = _dest(idx, pos, starts, TM_ROUTE)

    cap = n_blocks * BM_EXPERT
    dest_p = (dest[None] + (jnp.arange(PIECES, dtype=I32) * cap).reshape(PIECES, 1, 1)).reshape(-1)
    tb = t // SC_WINDOW
    xs = _sc_scatter_rows(u2p.reshape(PIECES * t, SC_ROW), dest_p, cap * PIECES,
                          lambda i: (i // (TOP_K * tb)) * tb + i % tb)
    ys = _experts(xs.reshape(PIECES, cap, SC_ROW), plan, w_exp_gate, w_exp_up, w_exp_down, BM_EXPERT)
    yg = _sc_gather_rows(ys.reshape(cap * PIECES, SC_ROW), dest_p).reshape(PIECES, TOP_K, t, SC_ROW)

    return _final(x1, u2p, yg, wtok, mod3, w_sh_gate.astype(BF16), w_sh_up.astype(BF16),
                  w_sh_down.astype(BF16), ln2_g.reshape(1, d), ln2_b.reshape(1, d), seq, TM_PROJ
                  ).reshape(bsz, seq, d)


def kernel(x, c, positions, w_ada, b_ada, w_in, w_br_a, w_br_b, w_out, cmp_pos_k, cmp_pos_v, cmp_k_w1,
           cmp_k_w2, cmp_v_w1, cmp_v_w2, ln1_g, ln1_b, w_router, router_bias, w_exp_gate, w_exp_up,
           w_exp_down, w_sh_gate, w_sh_up, w_sh_down, ln2_g, ln2_b):
    for l in range(w_ada.shape[0]):
        mod = _mod(c, w_ada[l], b_ada[l])
        x = _layer(x, mod, positions, w_in[l], w_br_a[l], w_br_b[l], w_out[l], cmp_pos_k[l], cmp_pos_v[l],
                   cmp_k_w1[l], cmp_k_w2[l], cmp_v_w1[l], cmp_v_w2[l], ln1_g[l], ln1_b[l], w_router[l],
                   router_bias[l], w_exp_gate[l], w_exp_up[l], w_exp_down[l], w_sh_gate[l], w_sh_up[l],
                   w_sh_down[l], ln2_g[l], ln2_b[l])
    return x
```

```python
import functools
import math

import jax
import jax.numpy as jnp
import numpy as np
from jax import lax
from jax.experimental import pallas as pl
from jax.experimental.pallas import tpu as pltpu
from jax.experimental.pallas import tpu_sc as plsc

F32 = jnp.float32
BF16 = jnp.bfloat16
I32 = jnp.int32

D_MODEL = 1024
HEAD_DIM = 64
ROPE_THETA = 500000.0
ROPE_FRACTION = 4
A_HEADS = 8
A_KV_HEADS = 2
IDX_HEADS = 8
IDX_DIM = 32
DSA_TOPK_MAX = 256
B_HEADS = 8
B_KV_HEADS = 2
REP = 4
CMP_BLOCK = 32
CMP_STRIDE = 16
CMP_HIDDEN = 256
SEL_BLOCK = 64
SEL_COUNT = 16
SEL_LOCAL = 2
WINDOW = 512
N_EXPERTS = 256
TOP_K = 8
N_GROUPS = 8
TOPK_GROUPS = 4
ROUTED_SCALE = 2.5
DEPTH = 1
DN_ALPHA = (2 * DEPTH) ** 0.25
LN_EPS = 1e-5
NEG = -1e30
FORCE = 1e9
INT_MIN = -2147483648
N_GATES = 3 * B_HEADS

LANES = 128
SUBLANES = 8
VMEM_LIMIT = 56 * 1024 * 1024
SC_WINDOW = 128
SC_ROW = 256
PIECES = (D_MODEL // 2) // SC_ROW

TM_PROJ = 512
TQ_ATTN = 512
WIN_TQ = 256
KC_ATTN = 512
TM_ROUTE = 512
BM_EXPERT = 512

_IN_WIDTHS = (512, 128, 128, 256, 32, 8, 512, 128, 128, 128, 128, 128, 128, 24, 1024, 1024)
_IN_OFFS = np.concatenate([[0], np.cumsum(_IN_WIDTHS)]).tolist()

NT_DIMS = (((1,), (1,)), ((), ()))


def _cparams(sem):
    return pltpu.CompilerParams(dimension_semantics=sem, vmem_limit_bytes=VMEM_LIMIT)


def _sigmoid(x):
    return 1.0 / (1.0 + jnp.exp(-x))


def _dot(a, b):
    return jnp.dot(a, b, preferred_element_type=F32)


def _dot_nt(a, b):
    return lax.dot_general(a, b, NT_DIMS, preferred_element_type=F32)


def _sort_key(x):
    x = jnp.where(x == 0.0, 0.0, x)
    bits = pltpu.bitcast(x, I32)
    return jnp.where(bits < 0, bits ^ 0x7FFFFFFF, bits)


def _kth_largest_key(count_ge, shape, k):
    kf = float(k)
    t0 = jnp.where(count_ge(jnp.zeros(shape, I32)) >= kf, 0, INT_MIN).astype(I32)

    def body(it, t):
        cand = t + jnp.left_shift(jnp.int32(1), 30 - it)
        return jnp.where(count_ge(cand) >= kf, cand, t)

    return lax.fori_loop(0, 31, body, t0)


def _tri_strict_lower(n, dtype):
    r = lax.broadcasted_iota(I32, (n, n), 0)
    c = lax.broadcasted_iota(I32, (n, n), 1)
    return jnp.where(c < r, 1.0, 0.0).astype(dtype)


def _tri_strict_upper(n, dtype):
    r = lax.broadcasted_iota(I32, (n, n), 0)
    c = lax.broadcasted_iota(I32, (n, n), 1)
    return jnp.where(r < c, 1.0, 0.0).astype(dtype)


def _mod_kernel(c_ref, w_ref, b_ref, o_ref):
    c = c_ref[...]
    cond = (c * _sigmoid(c)).astype(BF16)
    o_ref[...] = _dot(cond, w_ref[...].astype(BF16)) + b_ref[...]


def _mod(c, w_ada, b_ada):
    bsz, d = c.shape
    n = w_ada.shape[1]
    tn = 1024
    return pl.pallas_call(
        _mod_kernel,
        out_shape=jax.ShapeDtypeStruct((bsz, n), F32),
        grid=(n // tn,),
        in_specs=[pl.BlockSpec((bsz, d), lambda j: (0, 0)),
                  pl.BlockSpec((d, tn), lambda j: (0, j)),
                  pl.BlockSpec((1, tn), lambda j: (0, j))],
        out_specs=pl.BlockSpec((bsz, tn), lambda j: (0, j)),
        compiler_params=_cparams(("parallel",)),
        name="mod",
    )(c, w_ada, b_ada.reshape(1, n))


def _rope(z, c_tab, s_tab, period, half):
    w = z.shape[1]
    reps = w // LANES
    c = jnp.concatenate([c_tab] * reps, axis=1) if reps > 1 else c_tab
    s = jnp.concatenate([s_tab] * reps, axis=1) if reps > 1 else s_tab
    lane = lax.broadcasted_iota(I32, z.shape, 1)
    first = (lane & (period - 1)) < half
    partner = jnp.where(first, pltpu.roll(z, w - half, axis=1), pltpu.roll(z, half, axis=1))
    return z * c + partner * s


def _in_proj_kernel(x_ref, mod_ref, w_ref, wsm_ref, c64_ref, s64_ref, c32_ref, s32_ref,
                    qa_ref, ka_ref, vaT_ref, qi_ref, ki_ref, wT_ref, qbraw_ref, qbrot_ref,
                    kcmp_ref, vcmp_ref, ksel_ref, vselT_ref, kwin_ref, vwinT_ref, gT_ref, ga_ref, gb_ref):
    mod = mod_ref[0]
    u = (x_ref[...] * (1.0 + mod[1:2, :]) + mod[0:1, :]).astype(BF16)
    tm = u.shape[0]
    c64, s64, c32, s32 = c64_ref[...], s64_ref[...], c32_ref[...], s32_ref[...]
    scale = HEAD_DIM ** -0.5
    lane = lax.broadcasted_iota(I32, (tm, LANES), 1)
    low = lane < HEAD_DIM

    def proj(a, b):
        return _dot(u, w_ref[:, a:b])

    rope64 = lambda z: _rope(z, c64, s64, HEAD_DIM, HEAD_DIM // ROPE_FRACTION // 2)
    rope32 = lambda z: _rope(z, c32, s32, IDX_DIM, IDX_DIM // ROPE_FRACTION // 2)

    def head_slots64(z):
        out = []
        for h in range(A_HEADS):
            pair = z[:, (h // 2) * LANES:(h // 2 + 1) * LANES]
            g = h // REP
            src = pair if h % 2 == g else pltpu.roll(pair, HEAD_DIM, axis=1)
            out.append(jnp.where(low, src, 0.0) if g == 0 else jnp.where(low, 0.0, src))
        return jnp.concatenate(out, axis=1).astype(BF16)

    def head_slots32(z):
        per = LANES // IDX_DIM
        out = []
        for h in range(IDX_HEADS):
            col = z[:, (h // per) * LANES:(h // per + 1) * LANES]
            shift = IDX_DIM * (h % per)
            src = col if shift == 0 else pltpu.roll(col, LANES - shift, axis=1)
            out.append(jnp.where(lane < IDX_DIM, src, 0.0))
        return jnp.concatenate(out, axis=1).astype(BF16)

    def store_vt(ref, z, chunk):
        zt = z.T
        ones = jnp.ones((HEAD_DIM, chunk), F32)
        for g in range(A_KV_HEADS):
            for j in range(tm // chunk):
                blk = zt[g * HEAD_DIM:(g + 1) * HEAD_DIM, j * chunk:(j + 1) * chunk]
                ref[g, j] = jnp.concatenate([blk, ones], axis=0).astype(BF16)

    qa_ref[...] = head_slots64(rope64(proj(0, 512)) * scale)
    ka_ref[...] = rope64(proj(512, 640)).astype(BF16)
    store_vt(vaT_ref, proj(640, 768), tm)
    qi_ref[...] = head_slots32(rope32(proj(768, 1024)))
    ki_ref[...] = rope32(proj(1024, 1152)).astype(BF16)
    qb = proj(1152, 1664)
    qbraw_ref[...] = head_slots64(qb * scale)
    qbrot_ref[...] = head_slots64(rope64(qb) * scale)
    kcmp_ref[...] = proj(1664, 1792)
    vcmp_ref[...] = proj(1792, 1920)
    ksel_ref[...] = rope64(proj(1920, 2048)).astype(BF16)
    store_vt(vselT_ref, proj(2048, 2176), tm)
    kwin_ref[...] = rope64(proj(2176, 2304)).astype(BF16)
    store_vt(vwinT_ref, proj(2304, 2432), LANES)
    ga_ref[...] = _sigmoid(proj(2432, 3456)).astype(BF16)
    gb_ref[...] = _sigmoid(proj(3456, 4480)).astype(BF16)
    small = _dot_nt(wsm_ref[...], u)
    wT_ref[...] = small[0:IDX_HEADS, :]
    gT_ref[...] = _sigmoid(small[IDX_HEADS:IDX_HEADS + N_GATES, :])


def _pack_w_in(w_in):
    d = w_in.shape[0]
    col = lambda i: w_in[:, _IN_OFFS[i]:_IN_OFFS[i + 1]]
    ki = jnp.concatenate([col(4), jnp.zeros((d, LANES - IDX_DIM), w_in.dtype)], axis=1)
    parts = [col(0), col(1), col(2), col(3), ki, col(6), col(7), col(8), col(9), col(10),
             col(11), col(12), col(14), col(15)]
    w_small = jnp.concatenate([col(5), col(13)], axis=1).T
    return jnp.concatenate(parts, axis=1).astype(BF16), w_small.astype(BF16)


def _rope_tables(positions):
    pos = positions.astype(F32).reshape(-1, 1)

    def tab(dim):
        rot = dim // ROPE_FRACTION
        half = rot // 2
        inv = ROPE_THETA ** (-(jnp.arange(half, dtype=F32) * 2.0) / rot)
        ang = pos * inv
        cos, sin = jnp.cos(ang), jnp.sin(ang)
        ones = jnp.ones((pos.shape[0], dim - rot), F32)
        c = jnp.concatenate([cos, cos, ones], axis=1)
        s = jnp.concatenate([-sin, sin, 0.0 * ones], axis=1)
        return jnp.tile(c, (1, LANES // dim)), jnp.tile(s, (1, LANES // dim))

    return tab(HEAD_DIM) + tab(IDX_DIM)


def _in_proj(x2, mod3, w_pack, w_small, tabs, seq, tm):
    t, d = x2.shape
    n = w_pack.shape[1]
    per_b = seq // tm
    g = A_KV_HEADS
    row = lambda w: pl.BlockSpec((tm, w), lambda i: (i, 0))
    tok = lambda r: pl.BlockSpec((r, tm), lambda i: (0, i))
    vt_chunk = pl.BlockSpec((g, 1, LANES, tm), lambda i: (0, i, 0, 0))
    vt_lane = pl.BlockSpec((g, tm // LANES, LANES, LANES), lambda i: (0, i, 0, 0))
    sds = jax.ShapeDtypeStruct
    vt_chunk_shape = sds((g, t // tm, LANES, tm), BF16)
    outs = (("qa", sds((t, 1024), BF16), row(1024)), ("ka", sds((t, LANES), BF16), row(LANES)),
            ("vaT", vt_chunk_shape, vt_chunk), ("qi", sds((t, 1024), BF16), row(1024)),
            ("ki", sds((t, LANES), BF16), row(LANES)), ("wT", sds((IDX_HEADS, t), F32), tok(IDX_HEADS)),
            ("qbraw", sds((t, 1024), BF16), row(1024)), ("qbrot", sds((t, 1024), BF16), row(1024)),
            ("kcmp", sds((t, LANES), F32), row(LANES)), ("vcmp", sds((t, LANES), F32), row(LANES)),
            ("ksel", sds((t, LANES), BF16), row(LANES)), ("vselT", vt_chunk_shape, vt_chunk),
            ("kwin", sds((t, LANES), BF16), row(LANES)),
            ("vwinT", sds((g, t // LANES, LANES, LANES), BF16), vt_lane),
            ("gT", sds((N_GATES, t), F32), tok(N_GATES)),
            ("ga", sds((t, d), BF16), row(d)), ("gb", sds((t, d), BF16), row(d)))
    res = pl.pallas_call(
        _in_proj_kernel,
        out_shape=tuple(o[1] for o in outs),
        grid=(t // tm,),
        in_specs=[row(d),
                  pl.BlockSpec((1, 6, d), lambda i: (i // per_b, 0, 0)),
                  pl.BlockSpec((d, n), lambda i: (0, 0)),
                  pl.BlockSpec(w_small.shape, lambda i: (0, 0)),
                  row(LANES), row(LANES), row(LANES), row(LANES)],
        out_specs=tuple(o[2] for o in outs),
        compiler_params=_cparams(("parallel",)),
        name="in_proj",
    )(x2, mod3, w_pack, w_small, *tabs)
    return {o[0]: r for o, r in zip(outs, res)}


def _fold_rows(x, op):
    n = x.shape[0]
    while n % (2 * SUBLANES) == 0:
        n //= 2
        x = op(x[:n], x[n:])
    slabs = [x[i * SUBLANES:(i + 1) * SUBLANES] for i in range(n // SUBLANES)]
    while len(slabs) > 1:
        nxt = [op(slabs[i], slabs[i + 1]) for i in range(0, len(slabs) - 1, 2)]
        slabs = nxt + ([slabs[-1]] if len(slabs) % 2 else [])
    return slabs[0]


def _col_max(x):
    return jnp.max(_fold_rows(x, jnp.maximum), axis=0, keepdims=True)


def _col_sum(x):
    return jnp.sum(_fold_rows(x, jnp.add), axis=0, keepdims=True)


PACKED_ROWS = 16


def _fold_rows_packed(x):
    n = x.shape[0]
    assert n % PACKED_ROWS == 0 and n // PACKED_ROWS <= 256
    while n > PACKED_ROWS:
        n //= 2
        x = x[:n] + x[n:]
    return x


def _stack_heads(q_ref, heads):
    return jnp.concatenate([q_ref[0, :, h * LANES:(h + 1) * LANES] for h in heads], axis=0)


def _flash_step(k, q_stack, v_t, bias4, m, acc):
    s = _dot_nt(k, q_stack) + bias4
    m_new = jnp.maximum(m, _col_max(s))
    e = jnp.exp(s - m_new).astype(BF16)
    return m_new, acc * jnp.exp(m - m_new) + _dot(v_t, e)


HEADS_PER_CHAIN = REP


def _head_chains():
    return [(h // REP, tuple(range(h, h + HEADS_PER_CHAIN))) for h in range(0, A_HEADS, HEADS_PER_CHAIN)]


def _flash_loop(q_ref, k_ref, vT_ref, bias_of, n_chunks, tq, kc):
    chains = _head_chains()
    q_stacks = [_stack_heads(q_ref, heads) for _, heads in chains]
    width = HEADS_PER_CHAIN * tq

    def body(c, carry):
        k = k_ref[0, pl.ds(pl.multiple_of(c * kc, kc), kc), :]
        out = []
        for (g, _), q_stack, (m, acc) in zip(chains, q_stacks, carry):
            bias = jnp.concatenate([bias_of(g, c)] * HEADS_PER_CHAIN, axis=1)
            out.append(_flash_step(k, q_stack, vT_ref[g, c], bias, m, acc))
        return tuple(out)

    init = tuple((jnp.full((1, width), NEG, F32), jnp.zeros((LANES, width), F32)) for _ in chains)
    res = lax.fori_loop(0, n_chunks, body, init)
    return [(heads, acc) for (_, heads), (_, acc) in zip(chains, res)]


def _normalise(acc):
    return acc / jnp.maximum(acc[HEAD_DIM:HEAD_DIM + 1, :], 1e-30)


def _store_heads(o_ref, o_t, heads, tq):
    o = o_t.T
    for r, h in enumerate(heads):
        o_ref[0, :, h * HEAD_DIM:(h + 1) * HEAD_DIM] = o[r * tq:(r + 1) * tq, 0:HEAD_DIM].astype(o_ref.dtype)


def _select_mask(keys, thr, need, offset, tri):
    gt = jnp.where(keys > thr, 1.0, 0.0)
    eq = jnp.where(keys == thr, 1.0, 0.0)
    prefix = _dot(tri, eq.astype(BF16)) + offset
    return gt + jnp.where(prefix < need, eq, 0.0), offset + _col_sum(eq)


def _dsa_kernel(qi_ref, wT_ref, ki_ref, qa_ref, ka_ref, vT_ref, o_ref, keys_ref, bias_ref, planes_ref,
                live_ref, *, tq, kc, n_keep):
    t0 = pl.program_id(1) * tq
    n_chunks = (t0 + tq + kc - 1) // kc
    wT = wT_ref[...] * (IDX_HEADS ** -0.5 * IDX_DIM ** -0.5)
    qi_stack = _stack_heads(qi_ref, range(IDX_HEADS))
    keypos = lax.broadcasted_iota(I32, (kc, tq), 0)
    qpos = t0 + lax.broadcasted_iota(I32, (kc, tq), 1)

    def score_chunk(c, carry):
        k0 = pl.multiple_of(c * kc, kc)
        sc = _dot_nt(ki_ref[0, pl.ds(k0, kc), :], qi_stack)
        score = jnp.zeros((kc, tq), F32)
        for h in range(IDX_HEADS):
            score = score + wT[h:h + 1, :] * jnp.maximum(sc[:, h * tq:(h + 1) * tq], 0.0)
        key = jnp.where(k0 + keypos <= qpos, _sort_key(score), INT_MIN)
        keys_ref[c] = key
        flipped = key ^ INT_MIN
        for lvl in range(4):
            byte = lax.shift_right_logical(flipped, 8 * lvl) & 0xFF
            planes_ref[lvl, c] = byte.astype(F32).astype(BF16)
        live_ref[c] = jnp.ones((kc, tq), BF16)
        return carry

    lax.fori_loop(0, n_chunks, score_chunk, 0)

    def masked_count(lvl, pick):
        def body(c, acc):
            hit = jnp.where(pick(planes_ref[lvl, c]), live_ref[c], jnp.zeros((kc, tq), BF16))
            return acc + _fold_rows_packed(hit).astype(F32)
        acc = lax.fori_loop(0, n_chunks, body, jnp.zeros((PACKED_ROWS, tq), F32))
        return jnp.sum(acc, axis=0, keepdims=True)

    above = jnp.zeros((1, tq), F32)
    thr_u = jnp.zeros((1, tq), I32)
    for lvl in (3, 2, 1, 0):
        def bit_step(it, t, lvl=lvl, above=above):
            cand = t + jnp.left_shift(jnp.int32(1), 7 - it).astype(F32)
            cnt = above + masked_count(lvl, lambda p: p >= cand.astype(BF16))
            return jnp.where(cnt >= float(n_keep), cand, t)

        t = lax.fori_loop(0, 8, bit_step, jnp.zeros((1, tq), F32))
        tb = t.astype(BF16)
        above = above + masked_count(lvl, lambda p: p > tb)

        def narrow(c, carry, lvl=lvl, tb=tb):
            live_ref[c] = jnp.where(planes_ref[lvl, c] == tb, live_ref[c], jnp.zeros((kc, tq), BF16))
            return carry

        lax.fori_loop(0, n_chunks, narrow, 0)
        thr_u = thr_u | jnp.left_shift(t.astype(I32), 8 * lvl)
    thr = thr_u ^ INT_MIN
    need = float(n_keep) - above
    tri = _tri_strict_lower(LANES, BF16)
    sub = lax.broadcasted_iota(I32, (LANES, tq), 0)
    qsub = t0 + lax.broadcasted_iota(I32, (LANES, tq), 1)

    def bias_chunk(c, offset):
        for j in range(kc // LANES):
            rows = slice(j * LANES, (j + 1) * LANES)
            sel, offset = _select_mask(keys_ref[c, rows, :], thr, need, offset, tri)
            causal = c * kc + j * LANES + sub <= qsub
            bias_ref[c, rows, :] = jnp.where(causal, (sel - 1.0) * 1e30, NEG)
        return offset

    lax.fori_loop(0, n_chunks, bias_chunk, jnp.zeros((1, tq), F32))

    accs = _flash_loop(qa_ref, ka_ref, vT_ref, lambda g, c: bias_ref[c], n_chunks, tq, kc)
    for heads, acc in accs:
        _store_heads(o_ref, _normalise(acc), heads, tq)


def _dsa(qi, wT, ki, qa, ka, vaT, bsz, seq, tq, kc):
    n_keep = min(DSA_TOPK_MAX, seq // 4)
    nq = seq // tq
    qblk = lambda w: pl.BlockSpec((1, tq, w), lambda b, i: (b, i, 0))
    full = lambda w: pl.BlockSpec((1, seq, w), lambda b, i: (b, 0, 0))
    return pl.pallas_call(
        functools.partial(_dsa_kernel, tq=tq, kc=kc, n_keep=n_keep),
        out_shape=jax.ShapeDtypeStruct((bsz, seq, A_HEADS * HEAD_DIM), BF16),
        grid=(bsz, nq),
        in_specs=[qblk(1024), pl.BlockSpec((IDX_HEADS, tq), lambda b, i: (0, b * nq + i)), full(LANES),
                  qblk(1024), full(LANES),
                  pl.BlockSpec((A_KV_HEADS, seq // kc, LANES, kc), lambda b, i: (0, b, 0, 0))],
        out_specs=qblk(512),
        scratch_shapes=[pltpu.VMEM((seq // kc, kc, tq), I32), pltpu.VMEM((seq // kc, kc, tq), F32),
                        pltpu.VMEM((4, seq // kc, kc, tq), BF16), pltpu.VMEM((seq // kc, kc, tq), BF16)],
        compiler_params=_cparams(("parallel", "parallel")),
        name="dsa",
    )(qi, wT, ki, qa, ka, vaT)


def _gelu_tanh(x):
    return 0.5 * x * (1.0 + jnp.tanh(math.sqrt(2.0 / math.pi) * (x + 0.044715 * (x * x * x))))


def _compress_kernel(k_ref, v_ref, pk_ref, pv_ref, w1k_ref, w2k_ref, w1v_ref, w2v_ref,
                     kc_ref, vcT_ref, *, n_rows):
    half = CMP_BLOCK // 2

    def one(x_ref, p_ref, w1_ref, w2_ref):
        outs = []
        for g in range(B_KV_HEADS):
            lo = jnp.zeros((n_rows, CMP_HIDDEN), F32)
            hi = jnp.zeros((n_rows, CMP_HIDDEN), F32)
            for l in range(half):
                xl = x_ref[0, :, l * LANES + g * HEAD_DIM:l * LANES + (g + 1) * HEAD_DIM]
                a = (xl + p_ref[l:l + 1, :]).astype(BF16)
                b = (xl + p_ref[half + l:half + l + 1, :]).astype(BF16)
                lo = lo + _dot(a, w1_ref[l * HEAD_DIM:(l + 1) * HEAD_DIM, :].astype(BF16))
                hi = hi + _dot(b, w1_ref[(half + l) * HEAD_DIM:(half + l + 1) * HEAD_DIM, :].astype(BF16))
            hid = lo + pltpu.roll(hi, n_rows - 1, axis=0)
            outs.append(_dot(_gelu_tanh(hid).astype(BF16), w2_ref[...].astype(BF16)))
        return outs

    k0, k1 = one(k_ref, pk_ref, w1k_ref, w2k_ref)
    kc_ref[0] = jnp.concatenate([k0, k1], axis=1).astype(kc_ref.dtype)
    for g, v in enumerate(one(v_ref, pv_ref, w1v_ref, w2v_ref)):
        vcT_ref[0, g] = jnp.concatenate([v, jnp.zeros_like(v)], axis=1).T.astype(vcT_ref.dtype)


def _compress(kcmp, vcmp, pos_k, pos_v, w1k, w2k, w1v, w2v):
    bsz, n_rows, width = kcmp.shape
    xspec = pl.BlockSpec((1, n_rows, width), lambda b: (b, 0, 0))
    cst = lambda a: pl.BlockSpec(a.shape, lambda b: (0,) * a.ndim)
    return pl.pallas_call(
        functools.partial(_compress_kernel, n_rows=n_rows),
        out_shape=(jax.ShapeDtypeStruct((bsz, n_rows, LANES), BF16),
                   jax.ShapeDtypeStruct((bsz, B_KV_HEADS, LANES, n_rows), BF16)),
        grid=(bsz,),
        in_specs=[xspec, xspec, cst(pos_k), cst(pos_v), cst(w1k), cst(w2k), cst(w1v), cst(w2v)],
        out_specs=(pl.BlockSpec((1, n_rows, LANES), lambda b: (b, 0, 0)),
                   pl.BlockSpec((1, B_KV_HEADS, LANES, n_rows), lambda b: (b, 0, 0, 0))),
        compiler_params=_cparams(("parallel",)),
        name="compress",
    )(kcmp, vcmp, pos_k, pos_v, w1k, w2k, w1v, w2v)


def _split3(x):
    a = x.astype(BF16)
    r = x - a.astype(F32)
    b = r.astype(BF16)
    c = (r - b.astype(F32)).astype(BF16)
    return a, b, c


def _nsa_kernel(qraw_ref, qrot_ref, kc_ref, vcT_ref, ksel_ref, vselT_ref, kwin_ref, vwinT_ref,
                gT_ref, exp_ref, o_ref, bias_ref, *, tq, kc, seq, n_c):
    t0 = pl.program_id(1) * tq
    n_chunks = (t0 + tq + kc - 1) // kc
    n_s = seq // SEL_BLOCK
    n_pick = min(SEL_COUNT, n_s)
    wt = min(tq, WIN_TQ)
    span = WINDOW + wt
    gT = gT_ref[...]

    ridx = lax.broadcasted_iota(I32, (LANES, tq), 0)
    tl = t0 + lax.broadcasted_iota(I32, (LANES, tq), 1)
    valid_c = jnp.where((ridx * CMP_STRIDE + CMP_BLOCK - 1 <= tl) & (ridx < n_c), 1.0, 0.0)
    valid4 = jnp.concatenate([valid_c] * REP, axis=1) > 0.0
    js = lax.broadcasted_iota(I32, (LANES, LANES), 0) * SEL_BLOCK
    cs = lax.broadcasted_iota(I32, (LANES, LANES), 1) * CMP_STRIDE
    overlap_t = jnp.where((cs <= js + SEL_BLOCK - 1) & (cs + CMP_BLOCK - 1 >= js), 1.0, 0.0).astype(BF16)
    cur = tl // SEL_BLOCK
    forced = (ridx == 0) | ((cur - ridx >= 0) & (cur - ridx < SEL_LOCAL))
    blk_causal = ridx * SEL_BLOCK <= tl
    tri = _tri_strict_lower(LANES, BF16)
    keypos = lax.broadcasted_iota(I32, (kc, tq), 0)
    qpos = t0 + lax.broadcasted_iota(I32, (kc, tq), 1)

    o_cmp_g = []
    for g in range(B_KV_HEADS):
        heads = range(g * REP, (g + 1) * REP)
        s = jnp.where(valid4, _dot_nt(kc_ref[0], _stack_heads(qraw_ref, heads)), NEG)
        e = jnp.where(valid4, jnp.exp(s - _col_max(s)), 0.0)
        p = e / jnp.maximum(_col_sum(e), 1e-30)
        o_cmp_g.append(_dot(vcT_ref[0, g], p.astype(BF16)))
        p_sum = p[:, 0:tq]
        for r in range(1, REP):
            p_sum = p_sum + p[:, r * tq:(r + 1) * tq]
        pa, pb, pc = _split3(p_sum)
        imp = _dot(overlap_t, pa) + _dot(overlap_t, pb) + _dot(overlap_t, pc)
        imp = jnp.where(forced, FORCE, jnp.where(blk_causal, imp, NEG))
        keys = _sort_key(imp[0:n_s, :])
        count_ge = lambda cand, keys=keys: _col_sum(jnp.where(keys >= cand, 1.0, 0.0))
        thr = _kth_largest_key(count_ge, (1, tq), n_pick)
        need = float(n_pick) - _col_sum(jnp.where(keys > thr, 1.0, 0.0))
        blk_sel, _ = _select_mask(keys, thr, need, jnp.zeros((1, tq), F32), tri[0:n_s, 0:n_s])
        blk_sel = jnp.concatenate([blk_sel, jnp.zeros((LANES - n_s, tq), F32)], axis=0).astype(BF16)

        def bias_chunk(c, carry, blk_sel=blk_sel, g=g):
            tok_sel = _dot(exp_ref[c], blk_sel)
            bias_ref[g, c] = jnp.where(c * kc + keypos <= qpos, (tok_sel - 1.0) * 1e30, NEG)
            return carry

        lax.fori_loop(0, n_chunks, bias_chunk, 0)

    sel = _flash_loop(qrot_ref, ksel_ref, vselT_ref, lambda g, c: bias_ref[g, c], n_chunks, tq, kc)

    for (g, heads), (_, acc_sel) in zip(_head_chains(), sel):
        o_win_sub = []
        for sub in range(tq // wt):
            t_sub = t0 + sub * wt
            w0 = pl.multiple_of(jnp.clip(t_sub - WINDOW, 0, seq - span), wt)
            wdiff = ((t_sub + lax.broadcasted_iota(I32, (span, wt), 1))
                     - (w0 + lax.broadcasted_iota(I32, (span, wt), 0)))
            wbias = jnp.where((wdiff >= 0) & (wdiff < WINDOW), 0.0, NEG)
            q_sub = jnp.concatenate([qrot_ref[0, sub * wt:(sub + 1) * wt, h * LANES:(h + 1) * LANES]
                                     for h in heads], axis=0)
            sw = (_dot_nt(kwin_ref[0, pl.ds(w0, span), :], q_sub)
                  + jnp.concatenate([wbias] * HEADS_PER_CHAIN, axis=1))
            ew = jnp.exp(sw - _col_max(sw)).astype(BF16)
            wblk = w0 // LANES
            acc_win = _dot(vwinT_ref[g, wblk], ew[0:LANES, :])
            for j in range(1, span // LANES):
                acc_win = acc_win + _dot(vwinT_ref[g, wblk + j], ew[j * LANES:(j + 1) * LANES, :])
            o_win_sub.append(_normalise(acc_win))

        o_sel = _normalise(acc_sel)
        cols = []
        for r, h in enumerate(heads):
            col = slice(r * tq, (r + 1) * tq)
            cmp_col = slice((h % REP) * tq, (h % REP + 1) * tq)
            o_win = jnp.concatenate([o[:, r * wt:(r + 1) * wt] for o in o_win_sub], axis=1)
            cols.append(gT[3 * h:3 * h + 1, :] * o_cmp_g[g][:, cmp_col]
                        + gT[3 * h + 1:3 * h + 2, :] * o_sel[:, col]
                        + gT[3 * h + 2:3 * h + 3, :] * o_win)
        _store_heads(o_ref, jnp.concatenate(cols, axis=1), heads, tq)


def _nsa(qraw, qrot, kc_x, vcT, ksel, vselT, kwin, vwinT, gT, bsz, seq, tq, kc):
    n_c = (seq - CMP_BLOCK) // CMP_STRIDE + 1
    nq = seq // tq
    key_blk = (jnp.arange(seq, dtype=I32) // SEL_BLOCK).reshape(seq // kc, kc, 1)
    expand = (key_blk == jnp.arange(LANES, dtype=I32).reshape(1, 1, LANES)).astype(BF16)
    qblk = lambda w: pl.BlockSpec((1, tq, w), lambda b, i: (b, i, 0))
    full = lambda w: pl.BlockSpec((1, seq, w), lambda b, i: (b, 0, 0))
    g = B_KV_HEADS
    return pl.pallas_call(
        functools.partial(_nsa_kernel, tq=tq, kc=kc, seq=seq, n_c=n_c),
        out_shape=jax.ShapeDtypeStruct((bsz, seq, B_HEADS * HEAD_DIM), BF16),
        grid=(bsz, nq),
        in_specs=[qblk(1024), qblk(1024),
                  pl.BlockSpec((1,) + kc_x.shape[1:], lambda b, i: (b, 0, 0)),
                  pl.BlockSpec((1,) + vcT.shape[1:], lambda b, i: (b, 0, 0, 0)),
                  full(LANES), pl.BlockSpec((g, seq // kc, LANES, kc), lambda b, i: (0, b, 0, 0)),
                  full(LANES), pl.BlockSpec((g, seq // LANES, LANES, LANES), lambda b, i: (0, b, 0, 0)),
                  pl.BlockSpec((N_GATES, tq), lambda b, i: (0, b * nq + i)),
                  pl.BlockSpec(expand.shape, lambda b, i: (0, 0, 0))],
        out_specs=qblk(512),
        scratch_shapes=[pltpu.VMEM((g, seq // kc, kc, tq), F32)],
        compiler_params=_cparams(("parallel", "parallel")),
        name="nsa",
    )(qraw, qrot, kc_x, vcT, ksel, vselT, kwin, vwinT, gT, expand)


def _pack_pairs(x):
    n = x.shape[1] // 2
    lo = pltpu.bitcast(x[:, :n].astype(BF16).astype(F32), I32)
    hi = pltpu.bitcast(x[:, n:].astype(BF16).astype(F32), I32)
    return lax.shift_right_logical(lo, 16) | (hi & jnp.int32(-65536))


def _unpack_pairs(p):
    lo = pltpu.bitcast(lax.shift_left(p, 16), F32)
    hi = pltpu.bitcast(p & jnp.int32(-65536), F32)
    return jnp.concatenate([lo, hi], axis=1)


def _layer_norm(y, g, b):
    mu = jnp.mean(y, axis=1, keepdims=True)
    yc = y - mu
    var = jnp.mean(yc * yc, axis=1, keepdims=True)
    return yc * lax.rsqrt(var + LN_EPS) * g + b


def _out_proj_kernel(oa_ref, ob_ref, ga_ref, gb_ref, x_ref, mod_ref, wa_ref, wb_ref, wo_ref,
                     g1_ref, b1_ref, wrh_ref, wrl_ref, x1_ref, u2_ref, lg_ref):
    mod = mod_ref[0]
    merged = (ga_ref[...].astype(F32) * _dot(oa_ref[...], wa_ref[...])
              + gb_ref[...].astype(F32) * _dot(ob_ref[...], wb_ref[...]))
    mix = _dot(merged.astype(BF16), wo_ref[...])
    x1 = _layer_norm(DN_ALPHA * x_ref[...] + mod[2:3, :] * mix, g1_ref[...], b1_ref[...])
    x1_ref[...] = x1
    u2 = x1 * (1.0 + mod[4:5, :]) + mod[3:4, :]
    packed = _pack_pairs(u2)
    for j in range(PIECES):
        u2_ref[j] = packed[:, j * SC_ROW:(j + 1) * SC_ROW]
    uh = u2.astype(BF16)
    ul = (u2 - uh.astype(F32)).astype(BF16)
    lg_ref[...] = _dot_nt(wrh_ref[...], uh) + _dot_nt(wrh_ref[...], ul) + _dot_nt(wrl_ref[...], uh)


def _out_proj(oa, ob, ga, gb, x2, mod3, wa, wb, wo, g1, b1, wrh, wrl, seq, tm):
    t, d = x2.shape
    per_b = seq // tm
    row = lambda w: pl.BlockSpec((tm, w), lambda i: (i, 0))
    cst = lambda a: pl.BlockSpec(a.shape, lambda i: (0,) * a.ndim)
    return pl.pallas_call(
        _out_proj_kernel,
        out_shape=(jax.ShapeDtypeStruct((t, d), F32), jax.ShapeDtypeStruct((PIECES, t, SC_ROW), I32),
                   jax.ShapeDtypeStruct((N_EXPERTS, t), F32)),
        grid=(t // tm,),
        in_specs=[row(512), row(512), row(d), row(d), row(d),
                  pl.BlockSpec((1, 6, d), lambda i: (i // per_b, 0, 0)),
                  cst(wa), cst(wb), cst(wo), cst(g1), cst(b1), cst(wrh), cst(wrl)],
        out_specs=(row(d), pl.BlockSpec((PIECES, tm, SC_ROW), lambda i: (0, i, 0)),
                   pl.BlockSpec((N_EXPERTS, tm), lambda i: (0, i))),
        compiler_params=_cparams(("parallel",)),
        name="out_proj",
    )(oa, ob, ga, gb, x2, mod3, wa, wb, wo, g1, b1, wrh, wrl)


def _first_max(x, rows):
    m = jnp.max(x, axis=0, keepdims=True)
    idx = jnp.min(jnp.where(x == m, rows, 1e9), axis=0, keepdims=True)
    return m, idx


def _router_kernel(lg_ref, rb_ref, idx_ref, w_ref, pos_ref, cnt_ref, carry_ref, *, tm):
    @pl.when(pl.program_id(0) == 0)
    def _():
        carry_ref[...] = jnp.zeros_like(carry_ref)

    per_g = N_EXPERTS // N_GROUPS
    scores = _sigmoid(lg_ref[...])
    choice = scores + rb_ref[...][:, 0:1]
    rows = lax.broadcasted_iota(I32, (N_EXPERTS, tm), 0).astype(F32)
    rows_g = lax.broadcasted_iota(I32, (per_g, tm), 0).astype(F32)
    ninf = -jnp.inf

    gs = []
    for g in range(N_GROUPS):
        x = choice[g * per_g:(g + 1) * per_g, :]
        m1, i1 = _first_max(x, rows_g)
        m2 = jnp.max(jnp.where(rows_g == i1, ninf, x), axis=0, keepdims=True)
        gs.append(m1 + m2)
    gscore = jnp.concatenate(gs, axis=0)
    rows8 = lax.broadcasted_iota(I32, (N_GROUPS, tm), 0).astype(F32)
    keep = jnp.zeros((N_GROUPS, tm), F32)
    for _ in range(TOPK_GROUPS):
        _, gi = _first_max(gscore, rows8)
        hit = rows8 == gi
        keep = jnp.where(hit, 1.0, keep)
        gscore = jnp.where(hit, ninf, gscore)
    keep_full = jnp.concatenate(
        [jnp.broadcast_to(keep[g:g + 1, :], (per_g, tm)) for g in range(N_GROUPS)], axis=0)
    masked = jnp.where(keep_full > 0.0, choice, NEG)

    idxs, ws = [], []
    onehot = jnp.zeros((N_EXPERTS, tm), F32)
    for _ in range(TOP_K):
        _, ei = _first_max(masked, rows)
        hit = rows == ei
        idxs.append(ei)
        ws.append(jnp.sum(jnp.where(hit, scores, 0.0), axis=0, keepdims=True))
        masked = jnp.where(hit, ninf, masked)
        onehot = jnp.where(hit, 1.0, onehot)
    idx = jnp.concatenate(idxs, axis=0)
    w = jnp.concatenate(ws, axis=0)
    idx_ref[...] = idx.astype(I32)
    w = w / jnp.sum(w, axis=0, keepdims=True) * ROUTED_SCALE
    w_ref[...] = jnp.concatenate([w, jnp.zeros((LANES - TOP_K, tm), F32)], axis=0).T

    tri = _tri_strict_upper(tm, BF16)
    base = _dot(onehot.astype(BF16), tri) + carry_ref[...][:, 0:1]
    pos = [jnp.sum(jnp.where(rows == idxs[k], base, 0.0), axis=0, keepdims=True) for k in range(TOP_K)]
    pos_ref[...] = jnp.concatenate(pos, axis=0).astype(I32)
    carry = carry_ref[...] + jnp.sum(onehot, axis=1, keepdims=True)
    carry_ref[...] = carry
    cnt_ref[...] = carry


def _router(lg, router_bias, tm):
    e, t = lg.shape
    rb = jnp.broadcast_to(router_bias.reshape(e, 1).astype(F32), (e, LANES))
    tok = lambda r: pl.BlockSpec((r, tm), lambda i: (0, i))
    return pl.pallas_call(
        functools.partial(_router_kernel, tm=tm),
        out_shape=(jax.ShapeDtypeStruct((TOP_K, t), I32), jax.ShapeDtypeStruct((t, LANES), F32),
                   jax.ShapeDtypeStruct((TOP_K, t), I32), jax.ShapeDtypeStruct((e, LANES), F32)),
        grid=(t // tm,),
        in_specs=[tok(e), pl.BlockSpec((e, LANES), lambda i: (0, 0))],
        out_specs=(tok(TOP_K), pl.BlockSpec((tm, LANES), lambda i: (i, 0)), tok(TOP_K),
                   pl.BlockSpec((e, LANES), lambda i: (0, 0))),
        scratch_shapes=[pltpu.VMEM((e, LANES), F32)],
        compiler_params=_cparams(("arbitrary",)),
        name="router",
    )(lg, rb)


def _dest_kernel(idx_ref, pos_ref, st_ref, dest_ref, *, tm):
    rows = lax.broadcasted_iota(I32, (N_EXPERTS, tm), 0)
    starts = st_ref[...][:, 0:1]
    idx = idx_ref[...]
    out = []
    for k in range(TOP_K):
        out.append(jnp.sum(jnp.where(rows == idx[k:k + 1, :], starts, 0.0), axis=0, keepdims=True))
    dest_ref[...] = jnp.concatenate(out, axis=0).astype(I32) + pos_ref[...]


def _dest(idx, pos, starts, tm):
    k, t = idx.shape
    st = jnp.broadcast_to(starts.reshape(N_EXPERTS, 1).astype(F32), (N_EXPERTS, LANES))
    tok = pl.BlockSpec((k, tm), lambda i: (0, i))
    return pl.pallas_call(
        functools.partial(_dest_kernel, tm=tm),
        out_shape=jax.ShapeDtypeStruct((k, t), I32),
        grid=(t // tm,),
        in_specs=[tok, tok, pl.BlockSpec((N_EXPERTS, LANES), lambda i: (0, 0))],
        out_specs=tok,
        compiler_params=_cparams(("parallel",)),
        name="dest",
    )(idx, pos, st)


def _experts_kernel(blk_ref, used_ref, first_ref, slot_ref, next_ref, x_ref, wg_hbm, wu_hbm, wd_hbm, y_ref,
                    wg_buf, wu_buf, wd_buf, wg_bf, wu_bf, wd_bf, sems):
    b = pl.program_id(0)
    active = b < used_ref[0]

    def weight_copies(e, slot):
        return [pltpu.make_async_copy(hbm.at[e], buf.at[slot], sems.at[slot, i])
                for i, (hbm, buf) in enumerate(((wg_hbm, wg_buf), (wu_hbm, wu_buf), (wd_hbm, wd_buf)))]

    @pl.when(b == 0)
    def _():
        for cp in weight_copies(blk_ref[0], 0):
            cp.start()

    @pl.when(active & (first_ref[b] == 1))
    def _():
        slot = slot_ref[b]
        for cp in weight_copies(blk_ref[b], slot):
            cp.wait()
        nxt = next_ref[b]

        @pl.when(nxt >= 0)
        def _():
            for cp in weight_copies(nxt, 1 - slot):
                cp.start()

        wg_bf[...] = wg_buf[slot].astype(BF16)
        wu_bf[...] = wu_buf[slot].astype(BF16)
        wd_bf[...] = wd_buf[slot].astype(BF16)

    @pl.when(active)
    def _():
        x = jnp.concatenate([x_ref[j] for j in range(PIECES)], axis=1)
        x = _unpack_pairs(x).astype(BF16)
        a = _dot(x, wg_bf[...])
        u = _dot(x, wu_bf[...])
        h = (a * _sigmoid(a) * u).astype(BF16)
        y = _pack_pairs(_dot(h, wd_bf[...]))
        for j in range(PIECES):
            y_ref[j] = y[:, j * SC_ROW:(j + 1) * SC_ROW]


def _experts(xs, plan, wg, wu, wd, bm):
    _, cap, _ = xs.shape
    n_blocks = cap // bm
    d, f = wg.shape[1], wg.shape[2]
    rows = pl.BlockSpec((PIECES, bm, SC_ROW), lambda b, blk, used, *_: (0, jnp.minimum(b, used[0] - 1), 0))
    hbm = pl.BlockSpec(memory_space=pl.ANY)
    return pl.pallas_call(
        _experts_kernel,
        out_shape=jax.ShapeDtypeStruct(xs.shape, I32),
        grid_spec=pltpu.PrefetchScalarGridSpec(
            num_scalar_prefetch=5,
            grid=(n_blocks,),
            in_specs=[rows, hbm, hbm, hbm],
            out_specs=rows,
            scratch_shapes=[pltpu.VMEM((2, d, f), F32), pltpu.VMEM((2, d, f), F32), pltpu.VMEM((2, f, d), F32),
                            pltpu.VMEM((d, f), BF16), pltpu.VMEM((d, f), BF16), pltpu.VMEM((f, d), BF16),
                            pltpu.SemaphoreType.DMA((2, 3))]),
        compiler_params=_cparams(("arbitrary",)),
        name="experts",
    )(plan["blk_e"], plan["n_used"], plan["first"], plan["slot"], plan["next_e"], xs, wg, wu, wd)


def _final_kernel(x1_ref, u2_ref, yg_ref, w_ref, mod_ref, sg_ref, su_ref, sd_ref, g2_ref, b2_ref, o_ref):
    mod = mod_ref[0]
    w = w_ref[...]
    rows = lambda ref, *lead: jnp.concatenate([ref[(j,) + lead] for j in range(PIECES)], axis=1)
    routed = w[:, 0:1] * _unpack_pairs(rows(yg_ref, 0))
    for k in range(1, TOP_K):
        routed = routed + w[:, k:k + 1] * _unpack_pairs(rows(yg_ref, k))
    u = _unpack_pairs(rows(u2_ref)).astype(BF16)
    a = _dot(u, sg_ref[...])
    b = _dot(u, su_ref[...])
    shared = _dot((a * _sigmoid(a) * b).astype(BF16), sd_ref[...])
    y = DN_ALPHA * x1_ref[...] + mod[5:6, :] * (routed + shared)
    o_ref[...] = _layer_norm(y, g2_ref[...], b2_ref[...])


def _final(x1, u2p, yg, wtok, mod3, sg, su, sd, g2, b2, seq, tm):
    t, d = x1.shape
    per_b = seq // tm
    row = lambda w: pl.BlockSpec((tm, w), lambda i: (i, 0))
    cst = lambda a: pl.BlockSpec(a.shape, lambda i: (0,) * a.ndim)
    return pl.pallas_call(
        _final_kernel,
        out_shape=jax.ShapeDtypeStruct((t, d), F32),
        grid=(t // tm,),
        in_specs=[row(d), pl.BlockSpec((PIECES, tm, SC_ROW), lambda i: (0, i, 0)),
                  pl.BlockSpec((PIECES, TOP_K, tm, SC_ROW), lambda i: (0, 0, i, 0)),
                  row(LANES), pl.BlockSpec((1, 6, d), lambda i: (i // per_b, 0, 0)),
                  cst(sg), cst(su), cst(sd), cst(g2), cst(b2)],
        out_specs=row(d),
        compiler_params=_cparams(("parallel",)),
        name="final",
    )(x1, u2p, yg, wtok, mod3, sg, su, sd, g2, b2)


def _sc_mesh():
    return plsc.VectorSubcoreMesh(core_axis_name="core", subcore_axis_name="subcore")


def _sc_scatter_rows(src, idx, n_out, src_block):
    n_idx = idx.shape[0]

    @functools.partial(pl.kernel, out_type=jax.ShapeDtypeStruct((n_out, SC_ROW), src.dtype),
                       mesh=_sc_mesh(), scratch_types=[])
    def k(x_hbm, i_hbm, o_hbm):
        def body(x_vmem, i_vmem):
            pltpu.sync_copy(x_vmem, o_hbm.at[i_vmem.at[0]])

        pltpu.emit_pipeline(
            body, grid=(n_idx // SC_WINDOW,),
            in_specs=[pl.BlockSpec((SC_WINDOW, SC_ROW), lambda i: (src_block(i), 0)),
                      pl.BlockSpec((1, SC_WINDOW), lambda i: (0, i))],
            out_specs=[],
            core_axis_name=("core", "subcore"),
            dimension_semantics=(pltpu.PARALLEL,),
        )(x_hbm, i_hbm)

    return k(src, idx.reshape(1, n_idx))


def _sc_gather_rows(src, idx):
    n_idx = idx.shape[0]

    @functools.partial(pl.kernel, out_type=jax.ShapeDtypeStruct((n_idx, SC_ROW), src.dtype),
                       mesh=_sc_mesh(), scratch_types=[])
    def k(x_hbm, i_hbm, o_hbm):
        def body(i_vmem, o_vmem):
            pltpu.sync_copy(x_hbm.at[i_vmem.at[0]], o_vmem)

        pltpu.emit_pipeline(
            body, grid=(n_idx // SC_WINDOW,),
            in_specs=[pl.BlockSpec((1, SC_WINDOW), lambda i: (0, i))],
            out_specs=[pl.BlockSpec((SC_WINDOW, SC_ROW), lambda i: (i, 0))],
            core_axis_name=("core", "subcore"),
            dimension_semantics=(pltpu.PARALLEL,),
        )(i_hbm, o_hbm)

    return k(src, idx.reshape(1, n_idx))


def _moe_plan(counts, n_tok):
    bm = BM_EXPERT
    padded = (counts + bm - 1) // bm * bm
    p_ends = jnp.cumsum(padded)
    starts = p_ends - padded
    n_blocks = n_tok * TOP_K // bm + N_EXPERTS
    blk = jnp.arange(n_blocks, dtype=I32)
    blk_e = jnp.minimum(jnp.sum(p_ends[None, :] <= (blk * bm)[:, None], axis=1), N_EXPERTS - 1).astype(I32)
    n_used = (p_ends[-1] // bm).astype(I32)
    prev = jnp.concatenate([jnp.full((1,), -1, I32), blk_e[:-1]])
    first = ((blk_e != prev) & (blk < n_used)).astype(I32)
    slot = (jnp.cumsum(first) - 1) % 2
    first_pos = jnp.where(first == 1, blk, n_blocks)
    next_first = lax.cummin(jnp.concatenate([first_pos[1:], jnp.full((1,), n_blocks, I32)]), reverse=True)
    next_e = jnp.where(next_first < n_blocks, blk_e[jnp.minimum(next_first, n_blocks - 1)], -1)
    plan = dict(blk_e=blk_e, n_used=n_used.reshape(1), first=first, slot=slot.astype(I32),
                next_e=next_e.astype(I32))
    return starts, plan, n_blocks


def _layer(x, mod, positions, w_in, w_br_a, w_br_b, w_out, cmp_pos_k, cmp_pos_v, cmp_k_w1, cmp_k_w2,
           cmp_v_w1, cmp_v_w2, ln1_g, ln1_b, w_router, router_bias, w_exp_gate, w_exp_up, w_exp_down,
           w_sh_gate, w_sh_up, w_sh_down, ln2_g, ln2_b):
    bsz, seq, d = x.shape
    t = bsz * seq
    assert seq // CMP_STRIDE == LANES and seq % TQ_ATTN == 0
    assert seq % TM_PROJ == 0 and KC_ATTN == TM_PROJ
    x2 = x.reshape(t, d)
    mod3 = mod.reshape(bsz, 6, d)

    w_pack, w_small = _pack_w_in(w_in)
    z = _in_proj(x2, mod3, w_pack, w_small, _rope_tables(positions), seq, TM_PROJ)
    per_b = lambda name: z[name].reshape(bsz, seq, z[name].shape[1])

    o_a = _dsa(per_b("qi"), z["wT"], per_b("ki"), per_b("qa"), per_b("ka"), z["vaT"],
               bsz, seq, TQ_ATTN, KC_ATTN)

    n_rows = seq // CMP_STRIDE
    kc_x, vcT = _compress(z["kcmp"].reshape(bsz, n_rows, CMP_STRIDE * LANES),
                          z["vcmp"].reshape(bsz, n_rows, CMP_STRIDE * LANES),
                          cmp_pos_k, cmp_pos_v, cmp_k_w1, cmp_k_w2, cmp_v_w1, cmp_v_w2)
    o_b = _nsa(per_b("qbraw"), per_b("qbrot"), kc_x, vcT, per_b("ksel"), z["vselT"], per_b("kwin"),
               z["vwinT"], z["gT"], bsz, seq, TQ_ATTN, KC_ATTN)

    wr_hi = w_router.T.astype(BF16)
    wr_lo = (w_router.T - wr_hi.astype(F32)).astype(BF16)
    x1, u2p, logits = _out_proj(
        o_a.reshape(t, -1), o_b.reshape(t, -1), z["ga"], z["gb"], x2, mod3,
        w_br_a.astype(BF16), w_br_b.astype(BF16), w_out.astype(BF16),
        ln1_g.reshape(1, d), ln1_b.reshape(1, d), wr_hi, wr_lo, seq, TM_PROJ)

    idx, wtok, pos, counts = _router(logits, router_bias, TM_ROUTE)
    starts, plan, n_blocks = _moe_plan(counts[:, 0].astype(I32), t)
    dest = _dest(idx, pos, starts, TM_ROUTE)

    cap = n_blocks * BM_EXPERT
    dest_p = (dest[None] + (jnp.arange(PIECES, dtype=I32) * cap).reshape(PIECES, 1, 1)).reshape(-1)
    tb = t // SC_WINDOW
    xs = _sc_scatter_rows(u2p.reshape(PIECES * t, SC_ROW), dest_p, cap * PIECES,
                          lambda i: (i // (TOP_K * tb)) * tb + i % tb)
    ys = _experts(xs.reshape(PIECES, cap, SC_ROW), plan, w_exp_gate, w_exp_up, w_exp_down, BM_EXPERT)
    yg = _sc_gather_rows(ys.reshape(cap * PIECES, SC_ROW), dest_p).reshape(PIECES, TOP_K, t, SC_ROW)

    return _final(x1, u2p, yg, wtok, mod3, w_sh_gate.astype(BF16), w_sh_up.astype(BF16),
                  w_sh_down.astype(BF16), ln2_g.reshape(1, d), ln2_b.reshape(1, d), seq, TM_PROJ
                  ).reshape(bsz, seq, d)


def kernel(x, c, positions, w_ada, b_ada, w_in, w_br_a, w_br_b, w_out, cmp_pos_k, cmp_pos_v, cmp_k_w1,
           cmp_k_w2, cmp_v_w1, cmp_v_w2, ln1_g, ln1_b, w_router, router_bias, w_exp_gate, w_exp_up,
           w_exp_down, w_sh_gate, w_sh_up, w_sh_down, ln2_g, ln2_b):
    for l in range(w_ada.shape[0]):
        mod = _mod(c, w_ada[l], b_ada[l])
        x = _layer(x, mod, positions, w_in[l], w_br_a[l], w_br_b[l], w_out[l], cmp_pos_k[l], cmp_pos_v[l],
                   cmp_k_w1[l], cmp_k_w2[l], cmp_v_w1[l], cmp_v_w2[l], ln1_g[l], ln1_b[l], w_router[l],
                   router_bias[l], w_exp_gate[l], w_exp_up[l], w_exp_down[l], w_sh_gate[l], w_sh_up[l],
                   w_sh_down[l], ln2_g[l], ln2_b[l])
    return x
```

```python
import functools
import math

import jax
import jax.numpy as jnp
import numpy as np
from jax import lax
from jax.experimental import pallas as pl
from jax.experimental.pallas import tpu as pltpu
from jax.experimental.pallas import tpu_sc as plsc

F32 = jnp.float32
BF16 = jnp.bfloat16
I32 = jnp.int32

D_MODEL = 1024
HEAD_DIM = 64
ROPE_THETA = 500000.0
ROPE_FRACTION = 4
A_HEADS = 8
A_KV_HEADS = 2
IDX_HEADS = 8
IDX_DIM = 32
DSA_TOPK_MAX = 256
B_HEADS = 8
B_KV_HEADS = 2
REP = 4
CMP_BLOCK = 32
CMP_STRIDE = 16
CMP_HIDDEN = 256
SEL_BLOCK = 64
SEL_COUNT = 16
SEL_LOCAL = 2
WINDOW = 512
N_EXPERTS = 256
TOP_K = 8
N_GROUPS = 8
TOPK_GROUPS = 4
ROUTED_SCALE = 2.5
DEPTH = 1
DN_ALPHA = (2 * DEPTH) ** 0.25
LN_EPS = 1e-5
NEG = -1e30
FORCE = 1e9
INT_MIN = -2147483648
N_GATES = 3 * B_HEADS

LANES = 128
SUBLANES = 8
VMEM_LIMIT = 56 * 1024 * 1024
SC_WINDOW = 128
SC_ROW = 256
PIECES = (D_MODEL // 2) // SC_ROW

TM_PROJ = 512
TQ_ATTN = 512
WIN_TQ = 256
KC_ATTN = 512
TM_ROUTE = 512
BM_EXPERT = 512

_IN_WIDTHS = (512, 128, 128, 256, 32, 8, 512, 128, 128, 128, 128, 128, 128, 24, 1024, 1024)
_IN_OFFS = np.concatenate([[0], np.cumsum(_IN_WIDTHS)]).tolist()

NT_DIMS = (((1,), (1,)), ((), ()))


def _cparams(sem):
    return pltpu.CompilerParams(dimension_semantics=sem, vmem_limit_bytes=VMEM_LIMIT)


def _sigmoid(x):
    return 1.0 / (1.0 + jnp.exp(-x))


def _dot(a, b):
    return jnp.dot(a, b, preferred_element_type=F32)


def _dot_nt(a, b):
    return lax.dot_general(a, b, NT_DIMS, preferred_element_type=F32)


def _sort_key(x):
    x = jnp.where(x == 0.0, 0.0, x)
    bits = pltpu.bitcast(x, I32)
    return jnp.where(bits < 0, bits ^ 0x7FFFFFFF, bits)


def _kth_largest_key(count_ge, shape, k):
    kf = float(k)
    t0 = jnp.where(count_ge(jnp.zeros(shape, I32)) >= kf, 0, INT_MIN).astype(I32)

    def body(it, t):
        cand = t + jnp.left_shift(jnp.int32(1), 30 - it)
        return jnp.where(count_ge(cand) >= kf, cand, t)

    return lax.fori_loop(0, 31, body, t0)


def _tri_strict_lower(n, dtype):
    r = lax.broadcasted_iota(I32, (n, n), 0)
    c = lax.broadcasted_iota(I32, (n, n), 1)
    return jnp.where(c < r, 1.0, 0.0).astype(dtype)


def _tri_strict_upper(n, dtype):
    r = lax.broadcasted_iota(I32, (n, n), 0)
    c = lax.broadcasted_iota(I32, (n, n), 1)
    return jnp.where(r < c, 1.0, 0.0).astype(dtype)


def _mod_kernel(c_ref, w_ref, b_ref, o_ref):
    c = c_ref[...]
    cond = (c * _sigmoid(c)).astype(BF16)
    o_ref[...] = _dot(cond, w_ref[...].astype(BF16)) + b_ref[...]


def _mod(c, w_ada, b_ada):
    bsz, d = c.shape
    n = w_ada.shape[1]
    tn = 1024
    return pl.pallas_call(
        _mod_kernel,
        out_shape=jax.ShapeDtypeStruct((bsz, n), F32),
        grid=(n // tn,),
        in_specs=[pl.BlockSpec((bsz, d), lambda j: (0, 0)),
                  pl.BlockSpec((d, tn), lambda j: (0, j)),
                  pl.BlockSpec((1, tn), lambda j: (0, j))],
        out_specs=pl.BlockSpec((bsz, tn), lambda j: (0, j)),
        compiler_params=_cparams(("parallel",)),
        name="mod",
    )(c, w_ada, b_ada.reshape(1, n))


def _rope(z, c_tab, s_tab, period, half):
    w = z.shape[1]
    reps = w // LANES
    c = jnp.concatenate([c_tab] * reps, axis=1) if reps > 1 else c_tab
    s = jnp.concatenate([s_tab] * reps, axis=1) if reps > 1 else s_tab
    lane = lax.broadcasted_iota(I32, z.shape, 1)
    first = (lane & (period - 1)) < half
    partner = jnp.where(first, pltpu.roll(z, w - half, axis=1), pltpu.roll(z, half, axis=1))
    return z * c + partner * s


def _in_proj_kernel(x_ref, mod_ref, w_ref, wsm_ref, c64_ref, s64_ref, c32_ref, s32_ref,
                    qa_ref, ka_ref, vaT_ref, qi_ref, ki_ref, wT_ref, qbraw_ref, qbrot_ref,
                    kcmp_ref, vcmp_ref, ksel_ref, vselT_ref, kwin_ref, vwinT_ref, gT_ref, ga_ref, gb_ref):
    mod = mod_ref[0]
    u = (x_ref[...] * (1.0 + mod[1:2, :]) + mod[0:1, :]).astype(BF16)
    tm = u.shape[0]
    c64, s64, c32, s32 = c64_ref[...], s64_ref[...], c32_ref[...], s32_ref[...]
    scale = HEAD_DIM ** -0.5
    lane = lax.broadcasted_iota(I32, (tm, LANES), 1)
    low = lane < HEAD_DIM

    def proj(a, b):
        return _dot(u, w_ref[:, a:b])

    rope64 = lambda z: _rope(z, c64, s64, HEAD_DIM, HEAD_DIM // ROPE_FRACTION // 2)
    rope32 = lambda z: _rope(z, c32, s32, IDX_DIM, IDX_DIM // ROPE_FRACTION // 2)

    def head_slots64(z):
        out = []
        for h in range(A_HEADS):
            pair = z[:, (h // 2) * LANES:(h // 2 + 1) * LANES]
            g = h // REP
            src = pair if h % 2 == g else pltpu.roll(pair, HEAD_DIM, axis=1)
            out.append(jnp.where(low, src, 0.0) if g == 0 else jnp.where(low, 0.0, src))
        return jnp.concatenate(out, axis=1).astype(BF16)

    def head_slots32(z):
        per = LANES // IDX_DIM
        out = []
        for h in range(IDX_HEADS):
            col = z[:, (h // per) * LANES:(h // per + 1) * LANES]
            shift = IDX_DIM * (h % per)
            src = col if shift == 0 else pltpu.roll(col, LANES - shift, axis=1)
            out.append(jnp.where(lane < IDX_DIM, src, 0.0))
        return jnp.concatenate(out, axis=1).astype(BF16)

    def store_vt(ref, z, chunk):
        zt = z.T
        ones = jnp.ones((HEAD_DIM, chunk), F32)
        for g in range(A_KV_HEADS):
            for j in range(tm // chunk):
                blk = zt[g * HEAD_DIM:(g + 1) * HEAD_DIM, j * chunk:(j + 1) * chunk]
                ref[g, j] = jnp.concatenate([blk, ones], axis=0).astype(BF16)

    qa_ref[...] = head_slots64(rope64(proj(0, 512)) * scale)
    ka_ref[...] = rope64(proj(512, 640)).astype(BF16)
    store_vt(vaT_ref, proj(640, 768), tm)
    qi_ref[...] = head_slots32(rope32(proj(768, 1024)))
    ki_ref[...] = rope32(proj(1024, 1152)).astype(BF16)
    qb = proj(1152, 1664)
    qbraw_ref[...] = head_slots64(qb * scale)
    qbrot_ref[...] = head_slots64(rope64(qb) * scale)
    kcmp_ref[...] = proj(1664, 1792)
    vcmp_ref[...] = proj(1792, 1920)
    ksel_ref[...] = rope64(proj(1920, 2048)).astype(BF16)
    store_vt(vselT_ref, proj(2048, 2176), tm)
    kwin_ref[...] = rope64(proj(2176, 2304)).astype(BF16)
    store_vt(vwinT_ref, proj(2304, 2432), LANES)
    ga_ref[...] = _sigmoid(proj(2432, 3456)).astype(BF16)
    gb_ref[...] = _sigmoid(proj(3456, 4480)).astype(BF16)
    small = _dot_nt(wsm_ref[...], u)
    wT_ref[...] = small[0:IDX_HEADS, :]
    gT_ref[...] = _sigmoid(small[IDX_HEADS:IDX_HEADS + N_GATES, :])


def _pack_w_in(w_in):
    d = w_in.shape[0]
    col = lambda i: w_in[:, _IN_OFFS[i]:_IN_OFFS[i + 1]]
    ki = jnp.concatenate([col(4), jnp.zeros((d, LANES - IDX_DIM), w_in.dtype)], axis=1)
    parts = [col(0), col(1), col(2), col(3), ki, col(6), col(7), col(8), col(9), col(10),
             col(11), col(12), col(14), col(15)]
    w_small = jnp.concatenate([col(5), col(13)], axis=1).T
    return jnp.concatenate(parts, axis=1).astype(BF16), w_small.astype(BF16)


def _rope_tables(positions):
    pos = positions.astype(F32).reshape(-1, 1)

    def tab(dim):
        rot = dim // ROPE_FRACTION
        half = rot // 2
        inv = ROPE_THETA ** (-(jnp.arange(half, dtype=F32) * 2.0) / rot)
        ang = pos * inv
        cos, sin = jnp.cos(ang), jnp.sin(ang)
        ones = jnp.ones((pos.shape[0], dim - rot), F32)
        c = jnp.concatenate([cos, cos, ones], axis=1)
        s = jnp.concatenate([-sin, sin, 0.0 * ones], axis=1)
        return jnp.tile(c, (1, LANES // dim)), jnp.tile(s, (1, LANES // dim))

    return tab(HEAD_DIM) + tab(IDX_DIM)


def _in_proj(x2, mod3, w_pack, w_small, tabs, seq, tm):
    t, d = x2.shape
    n = w_pack.shape[1]
    per_b = seq // tm
    g = A_KV_HEADS
    row = lambda w: pl.BlockSpec((tm, w), lambda i: (i, 0))
    tok = lambda r: pl.BlockSpec((r, tm), lambda i: (0, i))
    vt_chunk = pl.BlockSpec((g, 1, LANES, tm), lambda i: (0, i, 0, 0))
    vt_lane = pl.BlockSpec((g, tm // LANES, LANES, LANES), lambda i: (0, i, 0, 0))
    sds = jax.ShapeDtypeStruct
    vt_chunk_shape = sds((g, t // tm, LANES, tm), BF16)
    outs = (("qa", sds((t, 1024), BF16), row(1024)), ("ka", sds((t, LANES), BF16), row(LANES)),
            ("vaT", vt_chunk_shape, vt_chunk), ("qi", sds((t, 1024), BF16), row(1024)),
            ("ki", sds((t, LANES), BF16), row(LANES)), ("wT", sds((IDX_HEADS, t), F32), tok(IDX_HEADS)),
            ("qbraw", sds((t, 1024), BF16), row(1024)), ("qbrot", sds((t, 1024), BF16), row(1024)),
            ("kcmp", sds((t, LANES), F32), row(LANES)), ("vcmp", sds((t, LANES), F32), row(LANES)),
            ("ksel", sds((t, LANES), BF16), row(LANES)), ("vselT", vt_chunk_shape, vt_chunk),
            ("kwin", sds((t, LANES), BF16), row(LANES)),
            ("vwinT", sds((g, t // LANES, LANES, LANES), BF16), vt_lane),
            ("gT", sds((N_GATES, t), F32), tok(N_GATES)),
            ("ga", sds((t, d), BF16), row(d)), ("gb", sds((t, d), BF16), row(d)))
    res = pl.pallas_call(
        _in_proj_kernel,
        out_shape=tuple(o[1] for o in outs),
        grid=(t // tm,),
        in_specs=[row(d),
                  pl.BlockSpec((1, 6, d), lambda i: (i // per_b, 0, 0)),
                  pl.BlockSpec((d, n), lambda i: (0, 0)),
                  pl.BlockSpec(w_small.shape, lambda i: (0, 0)),
                  row(LANES), row(LANES), row(LANES), row(LANES)],
        out_specs=tuple(o[2] for o in outs),
        compiler_params=_cparams(("parallel",)),
        name="in_proj",
    )(x2, mod3, w_pack, w_small, *tabs)
    return {o[0]: r for o, r in zip(outs, res)}


def _fold_rows(x, op):
    n = x.shape[0]
    while n % (2 * SUBLANES) == 0:
        n //= 2
        x = op(x[:n], x[n:])
    slabs = [x[i * SUBLANES:(i + 1) * SUBLANES] for i in range(n // SUBLANES)]
    while len(slabs) > 1:
        nxt = [op(slabs[i], slabs[i + 1]) for i in range(0, len(slabs) - 1, 2)]
        slabs = nxt + ([slabs[-1]] if len(slabs) % 2 else [])
    return slabs[0]


def _col_max(x):
    return jnp.max(_fold_rows(x, jnp.maximum), axis=0, keepdims=True)


def _col_sum(x):
    return jnp.sum(_fold_rows(x, jnp.add), axis=0, keepdims=True)


PACKED_ROWS = 16


def _fold_rows_packed(x):
    n = x.shape[0]
    assert n % PACKED_ROWS == 0 and n // PACKED_ROWS <= 256
    while n > PACKED_ROWS:
        n //= 2
        x = x[:n] + x[n:]
    return x


def _stack_heads(q_ref, heads):
    return jnp.concatenate([q_ref[0, :, h * LANES:(h + 1) * LANES] for h in heads], axis=0)


def _flash_step(k, q_stack, v_t, bias4, m, acc):
    s = _dot_nt(k, q_stack) + bias4
    m_new = jnp.maximum(m, _col_max(s))
    e = jnp.exp(s - m_new).astype(BF16)
    return m_new, acc * jnp.exp(m - m_new) + _dot(v_t, e)


HEADS_PER_CHAIN = REP


def _head_chains():
    return [(h // REP, tuple(range(h, h + HEADS_PER_CHAIN))) for h in range(0, A_HEADS, HEADS_PER_CHAIN)]


def _flash_loop(q_ref, k_ref, vT_ref, bias_of, n_chunks, tq, kc):
    chains = _head_chains()
    q_stacks = [_stack_heads(q_ref, heads) for _, heads in chains]
    width = HEADS_PER_CHAIN * tq

    def body(c, carry):
        k = k_ref[0, pl.ds(pl.multiple_of(c * kc, kc), kc), :]
        out = []
        for (g, _), q_stack, (m, acc) in zip(chains, q_stacks, carry):
            bias = jnp.concatenate([bias_of(g, c)] * HEADS_PER_CHAIN, axis=1)
            out.append(_flash_step(k, q_stack, vT_ref[g, c], bias, m, acc))
        return tuple(out)

    init = tuple((jnp.full((1, width), NEG, F32), jnp.zeros((LANES, width), F32)) for _ in chains)
    res = lax.fori_loop(0, n_chunks, body, init)
    return [(heads, acc) for (_, heads), (_, acc) in zip(chains, res)]


def _normalise(acc):
    return acc / jnp.maximum(acc[HEAD_DIM:HEAD_DIM + 1, :], 1e-30)


def _store_heads(o_ref, o_t, heads, tq):
    o = o_t.T
    for r, h in enumerate(heads):
        o_ref[0, :, h * HEAD_DIM:(h + 1) * HEAD_DIM] = o[r * tq:(r + 1) * tq, 0:HEAD_DIM].astype(o_ref.dtype)


def _select_mask(keys, thr, need, offset, tri):
    gt = jnp.where(keys > thr, 1.0, 0.0)
    eq = jnp.where(keys == thr, 1.0, 0.0)
    prefix = _dot(tri, eq.astype(BF16)) + offset
    return gt + jnp.where(prefix < need, eq, 0.0), offset + _col_sum(eq)


def _dsa_kernel(qi_ref, wT_ref, ki_ref, qa_ref, ka_ref, vT_ref, o_ref, keys_ref, bias_ref, planes_ref,
                *, tq, kc, n_keep):
    t0 = pl.program_id(1) * tq
    n_chunks = (t0 + tq + kc - 1) // kc
    wT = wT_ref[...] * (IDX_HEADS ** -0.5 * IDX_DIM ** -0.5)
    qi_stack = _stack_heads(qi_ref, range(IDX_HEADS))
    keypos = lax.broadcasted_iota(I32, (kc, tq), 0)
    qpos = t0 + lax.broadcasted_iota(I32, (kc, tq), 1)

    def score_chunk(c, carry):
        k0 = pl.multiple_of(c * kc, kc)
        sc = _dot_nt(ki_ref[0, pl.ds(k0, kc), :], qi_stack)
        score = jnp.zeros((kc, tq), F32)
        for h in range(IDX_HEADS):
            score = score + wT[h:h + 1, :] * jnp.maximum(sc[:, h * tq:(h + 1) * tq], 0.0)
        key = jnp.where(k0 + keypos <= qpos, _sort_key(score), INT_MIN)
        keys_ref[c] = key
        flipped = key ^ INT_MIN
        for lvl in range(4):
            byte = lax.shift_right_logical(flipped, 8 * lvl) & 0xFF
            planes_ref[lvl, c] = byte.astype(F32).astype(BF16)
        return carry

    lax.fori_loop(0, n_chunks, score_chunk, 0)

    one = jnp.ones((kc, tq), BF16)
    zero = jnp.zeros((kc, tq), BF16)
    dead = jnp.full((kc, tq), -1.0, BF16)

    def count_where(lvl, pick, also=None):
        def body(c, acc):
            p = planes_ref[lvl, c]
            if also is not None:
                also(c, p)
            return acc + _fold_rows_packed(jnp.where(pick(p), one, zero)).astype(F32)
        acc = lax.fori_loop(0, n_chunks, body, jnp.zeros((PACKED_ROWS, tq), F32))
        return jnp.sum(acc, axis=0, keepdims=True)

    above = jnp.zeros((1, tq), F32)
    thr_u = jnp.zeros((1, tq), I32)
    for lvl in (3, 2, 1, 0):
        def bit_step(it, t, lvl=lvl, above=above):
            cand = t + jnp.left_shift(jnp.int32(1), 7 - it).astype(F32)
            cnt = above + count_where(lvl, lambda p: p >= cand.astype(BF16))
            return jnp.where(cnt >= float(n_keep), cand, t)

        t = lax.fori_loop(0, 8, bit_step, jnp.zeros((1, tq), F32))
        tb = t.astype(BF16)

        def narrow(c, p, lvl=lvl, tb=tb):
            planes_ref[lvl - 1, c] = jnp.where(p == tb, planes_ref[lvl - 1, c], dead)

        above = above + count_where(lvl, lambda p: p > tb, narrow if lvl > 0 else None)
        thr_u = thr_u | jnp.left_shift(t.astype(I32), 8 * lvl)
    thr = thr_u ^ INT_MIN
    need = float(n_keep) - above
    tri = _tri_strict_lower(LANES, BF16)
    sub = lax.broadcasted_iota(I32, (LANES, tq), 0)
    qsub = t0 + lax.broadcasted_iota(I32, (LANES, tq), 1)

    def bias_chunk(c, offset):
        for j in range(kc // LANES):
            rows = slice(j * LANES, (j + 1) * LANES)
            sel, offset = _select_mask(keys_ref[c, rows, :], thr, need, offset, tri)
            causal = c * kc + j * LANES + sub <= qsub
            bias_ref[c, rows, :] = jnp.where(causal, (sel - 1.0) * 1e30, NEG)
        return offset

    lax.fori_loop(0, n_chunks, bias_chunk, jnp.zeros((1, tq), F32))

    accs = _flash_loop(qa_ref, ka_ref, vT_ref, lambda g, c: bias_ref[c], n_chunks, tq, kc)
    for heads, acc in accs:
        _store_heads(o_ref, _normalise(acc), heads, tq)


def _dsa(qi, wT, ki, qa, ka, vaT, bsz, seq, tq, kc):
    n_keep = min(DSA_TOPK_MAX, seq // 4)
    nq = seq // tq
    qblk = lambda w: pl.BlockSpec((1, tq, w), lambda b, i: (b, i, 0))
    full = lambda w: pl.BlockSpec((1, seq, w), lambda b, i: (b, 0, 0))
    return pl.pallas_call(
        functools.partial(_dsa_kernel, tq=tq, kc=kc, n_keep=n_keep),
        out_shape=jax.ShapeDtypeStruct((bsz, seq, A_HEADS * HEAD_DIM), BF16),
        grid=(bsz, nq),
        in_specs=[qblk(1024), pl.BlockSpec((IDX_HEADS, tq), lambda b, i: (0, b * nq + i)), full(LANES),
                  qblk(1024), full(LANES),
                  pl.BlockSpec((A_KV_HEADS, seq // kc, LANES, kc), lambda b, i: (0, b, 0, 0))],
        out_specs=qblk(512),
        scratch_shapes=[pltpu.VMEM((seq // kc, kc, tq), I32), pltpu.VMEM((seq // kc, kc, tq), F32),
                        pltpu.VMEM((4, seq // kc, kc, tq), BF16)],
        compiler_params=_cparams(("parallel", "parallel")),
        name="dsa",
    )(qi, wT, ki, qa, ka, vaT)


def _gelu_tanh(x):
    return 0.5 * x * (1.0 + jnp.tanh(math.sqrt(2.0 / math.pi) * (x + 0.044715 * (x * x * x))))


def _compress_kernel(k_ref, v_ref, pk_ref, pv_ref, w1k_ref, w2k_ref, w1v_ref, w2v_ref,
                     kc_ref, vcT_ref, *, n_rows):
    half = CMP_BLOCK // 2

    def one(x_ref, p_ref, w1_ref, w2_ref):
        outs = []
        for g in range(B_KV_HEADS):
            lo = jnp.zeros((n_rows, CMP_HIDDEN), F32)
            hi = jnp.zeros((n_rows, CMP_HIDDEN), F32)
            for l in range(half):
                xl = x_ref[0, :, l * LANES + g * HEAD_DIM:l * LANES + (g + 1) * HEAD_DIM]
                a = (xl + p_ref[l:l + 1, :]).astype(BF16)
                b = (xl + p_ref[half + l:half + l + 1, :]).astype(BF16)
                lo = lo + _dot(a, w1_ref[l * HEAD_DIM:(l + 1) * HEAD_DIM, :].astype(BF16))
                hi = hi + _dot(b, w1_ref[(half + l) * HEAD_DIM:(half + l + 1) * HEAD_DIM, :].astype(BF16))
            hid = lo + pltpu.roll(hi, n_rows - 1, axis=0)
            outs.append(_dot(_gelu_tanh(hid).astype(BF16), w2_ref[...].astype(BF16)))
        return outs

    k0, k1 = one(k_ref, pk_ref, w1k_ref, w2k_ref)
    kc_ref[0] = jnp.concatenate([k0, k1], axis=1).astype(kc_ref.dtype)
    for g, v in enumerate(one(v_ref, pv_ref, w1v_ref, w2v_ref)):
        vcT_ref[0, g] = jnp.concatenate([v, jnp.zeros_like(v)], axis=1).T.astype(vcT_ref.dtype)


def _compress(kcmp, vcmp, pos_k, pos_v, w1k, w2k, w1v, w2v):
    bsz, n_rows, width = kcmp.shape
    xspec = pl.BlockSpec((1, n_rows, width), lambda b: (b, 0, 0))
    cst = lambda a: pl.BlockSpec(a.shape, lambda b: (0,) * a.ndim)
    return pl.pallas_call(
        functools.partial(_compress_kernel, n_rows=n_rows),
        out_shape=(jax.ShapeDtypeStruct((bsz, n_rows, LANES), BF16),
                   jax.ShapeDtypeStruct((bsz, B_KV_HEADS, LANES, n_rows), BF16)),
        grid=(bsz,),
        in_specs=[xspec, xspec, cst(pos_k), cst(pos_v), cst(w1k), cst(w2k), cst(w1v), cst(w2v)],
        out_specs=(pl.BlockSpec((1, n_rows, LANES), lambda b: (b, 0, 0)),
                   pl.BlockSpec((1, B_KV_HEADS, LANES, n_rows), lambda b: (b, 0, 0, 0))),
        compiler_params=_cparams(("parallel",)),
        name="compress",
    )(kcmp, vcmp, pos_k, pos_v, w1k, w2k, w1v, w2v)


def _split3(x):
    a = x.astype(BF16)
    r = x - a.astype(F32)
    b = r.astype(BF16)
    c = (r - b.astype(F32)).astype(BF16)
    return a, b, c


def _nsa_kernel(qraw_ref, qrot_ref, kc_ref, vcT_ref, ksel_ref, vselT_ref, kwin_ref, vwinT_ref,
                gT_ref, exp_ref, o_ref, bias_ref, *, tq, kc, seq, n_c):
    t0 = pl.program_id(1) * tq
    n_chunks = (t0 + tq + kc - 1) // kc
    n_s = seq // SEL_BLOCK
    n_pick = min(SEL_COUNT, n_s)
    wt = min(tq, WIN_TQ)
    span = WINDOW + wt
    gT = gT_ref[...]

    ridx = lax.broadcasted_iota(I32, (LANES, tq), 0)
    tl = t0 + lax.broadcasted_iota(I32, (LANES, tq), 1)
    valid_c = jnp.where((ridx * CMP_STRIDE + CMP_BLOCK - 1 <= tl) & (ridx < n_c), 1.0, 0.0)
    valid4 = jnp.concatenate([valid_c] * REP, axis=1) > 0.0
    js = lax.broadcasted_iota(I32, (LANES, LANES), 0) * SEL_BLOCK
    cs = lax.broadcasted_iota(I32, (LANES, LANES), 1) * CMP_STRIDE
    overlap_t = jnp.where((cs <= js + SEL_BLOCK - 1) & (cs + CMP_BLOCK - 1 >= js), 1.0, 0.0).astype(BF16)
    cur = tl // SEL_BLOCK
    forced = (ridx == 0) | ((cur - ridx >= 0) & (cur - ridx < SEL_LOCAL))
    blk_causal = ridx * SEL_BLOCK <= tl
    tri = _tri_strict_lower(LANES, BF16)
    keypos = lax.broadcasted_iota(I32, (kc, tq), 0)
    qpos = t0 + lax.broadcasted_iota(I32, (kc, tq), 1)

    o_cmp_g = []
    for g in range(B_KV_HEADS):
        heads = range(g * REP, (g + 1) * REP)
        s = jnp.where(valid4, _dot_nt(kc_ref[0], _stack_heads(qraw_ref, heads)), NEG)
        e = jnp.where(valid4, jnp.exp(s - _col_max(s)), 0.0)
        p = e / jnp.maximum(_col_sum(e), 1e-30)
        o_cmp_g.append(_dot(vcT_ref[0, g], p.astype(BF16)))
        p_sum = p[:, 0:tq]
        for r in range(1, REP):
            p_sum = p_sum + p[:, r * tq:(r + 1) * tq]
        pa, pb, pc = _split3(p_sum)
        imp = _dot(overlap_t, pa) + _dot(overlap_t, pb) + _dot(overlap_t, pc)
        imp = jnp.where(forced, FORCE, jnp.where(blk_causal, imp, NEG))
        keys = _sort_key(imp[0:n_s, :])
        count_ge = lambda cand, keys=keys: _col_sum(jnp.where(keys >= cand, 1.0, 0.0))
        thr = _kth_largest_key(count_ge, (1, tq), n_pick)
        need = float(n_pick) - _col_sum(jnp.where(keys > thr, 1.0, 0.0))
        blk_sel, _ = _select_mask(keys, thr, need, jnp.zeros((1, tq), F32), tri[0:n_s, 0:n_s])
        blk_sel = jnp.concatenate([blk_sel, jnp.zeros((LANES - n_s, tq), F32)], axis=0).astype(BF16)

        def bias_chunk(c, carry, blk_sel=blk_sel, g=g):
            tok_sel = _dot(exp_ref[c], blk_sel)
            bias_ref[g, c] = jnp.where(c * kc + keypos <= qpos, (tok_sel - 1.0) * 1e30, NEG)
            return carry

        lax.fori_loop(0, n_chunks, bias_chunk, 0)

    sel = _flash_loop(qrot_ref, ksel_ref, vselT_ref, lambda g, c: bias_ref[g, c], n_chunks, tq, kc)

    for (g, heads), (_, acc_sel) in zip(_head_chains(), sel):
        o_win_sub = []
        for sub in range(tq // wt):
            t_sub = t0 + sub * wt
            w0 = pl.multiple_of(jnp.clip(t_sub - WINDOW, 0, seq - span), wt)
            wdiff = ((t_sub + lax.broadcasted_iota(I32, (span, wt), 1))
                     - (w0 + lax.broadcasted_iota(I32, (span, wt), 0)))
            wbias = jnp.where((wdiff >= 0) & (wdiff < WINDOW), 0.0, NEG)
            q_sub = jnp.concatenate([qrot_ref[0, sub * wt:(sub + 1) * wt, h * LANES:(h + 1) * LANES]
                                     for h in heads], axis=0)
            sw = (_dot_nt(kwin_ref[0, pl.ds(w0, span), :], q_sub)
                  + jnp.concatenate([wbias] * HEADS_PER_CHAIN, axis=1))
            ew = jnp.exp(sw - _col_max(sw)).astype(BF16)
            wblk = w0 // LANES
            acc_win = _dot(vwinT_ref[g, wblk], ew[0:LANES, :])
            for j in range(1, span // LANES):
                acc_win = acc_win + _dot(vwinT_ref[g, wblk + j], ew[j * LANES:(j + 1) * LANES, :])
            o_win_sub.append(_normalise(acc_win))

        o_sel = _normalise(acc_sel)
        cols = []
        for r, h in enumerate(heads):
            col = slice(r * tq, (r + 1) * tq)
            cmp_col = slice((h % REP) * tq, (h % REP + 1) * tq)
            o_win = jnp.concatenate([o[:, r * wt:(r + 1) * wt] for o in o_win_sub], axis=1)
            cols.append(gT[3 * h:3 * h + 1, :] * o_cmp_g[g][:, cmp_col]
                        + gT[3 * h + 1:3 * h + 2, :] * o_sel[:, col]
                        + gT[3 * h + 2:3 * h + 3, :] * o_win)
        _store_heads(o_ref, jnp.concatenate(cols, axis=1), heads, tq)


def _nsa(qraw, qrot, kc_x, vcT, ksel, vselT, kwin, vwinT, gT, bsz, seq, tq, kc):
    n_c = (seq - CMP_BLOCK) // CMP_STRIDE + 1
    nq = seq // tq
    key_blk = (jnp.arange(seq, dtype=I32) // SEL_BLOCK).reshape(seq // kc, kc, 1)
    expand = (key_blk == jnp.arange(LANES, dtype=I32).reshape(1, 1, LANES)).astype(BF16)
    qblk = lambda w: pl.BlockSpec((1, tq, w), lambda b, i: (b, i, 0))
    full = lambda w: pl.BlockSpec((1, seq, w), lambda b, i: (b, 0, 0))
    g = B_KV_HEADS
    return pl.pallas_call(
        functools.partial(_nsa_kernel, tq=tq, kc=kc, seq=seq, n_c=n_c),
        out_shape=jax.ShapeDtypeStruct((bsz, seq, B_HEADS * HEAD_DIM), BF16),
        grid=(bsz, nq),
        in_specs=[qblk(1024), qblk(1024),
                  pl.BlockSpec((1,) + kc_x.shape[1:], lambda b, i: (b, 0, 0)),
                  pl.BlockSpec((1,) + vcT.shape[1:], lambda b, i: (b, 0, 0, 0)),
                  full(LANES), pl.BlockSpec((g, seq // kc, LANES, kc), lambda b, i: (0, b, 0, 0)),
                  full(LANES), pl.BlockSpec((g, seq // LANES, LANES, LANES), lambda b, i: (0, b, 0, 0)),
                  pl.BlockSpec((N_GATES, tq), lambda b, i: (0, b * nq + i)),
                  pl.BlockSpec(expand.shape, lambda b, i: (0, 0, 0))],
        out_specs=qblk(512),
        scratch_shapes=[pltpu.VMEM((g, seq // kc, kc, tq), F32)],
        compiler_params=_cparams(("parallel", "parallel")),
        name="nsa",
    )(qraw, qrot, kc_x, vcT, ksel, vselT, kwin, vwinT, gT, expand)


def _pack_pairs(x):
    n = x.shape[1] // 2
    lo = pltpu.bitcast(x[:, :n].astype(BF16).astype(F32), I32)
    hi = pltpu.bitcast(x[:, n:].astype(BF16).astype(F32), I32)
    return lax.shift_right_logical(lo, 16) | (hi & jnp.int32(-65536))


def _unpack_pairs(p):
    lo = pltpu.bitcast(lax.shift_left(p, 16), F32)
    hi = pltpu.bitcast(p & jnp.int32(-65536), F32)
    return jnp.concatenate([lo, hi], axis=1)


def _layer_norm(y, g, b):
    mu = jnp.mean(y, axis=1, keepdims=True)
    yc = y - mu
    var = jnp.mean(yc * yc, axis=1, keepdims=True)
    return yc * lax.rsqrt(var + LN_EPS) * g + b


def _out_proj_kernel(oa_ref, ob_ref, ga_ref, gb_ref, x_ref, mod_ref, wa_ref, wb_ref, wo_ref,
                     g1_ref, b1_ref, wrh_ref, wrl_ref, x1_ref, u2_ref, lg_ref):
    mod = mod_ref[0]
    merged = (ga_ref[...].astype(F32) * _dot(oa_ref[...], wa_ref[...])
              + gb_ref[...].astype(F32) * _dot(ob_ref[...], wb_ref[...]))
    mix = _dot(merged.astype(BF16), wo_ref[...])
    x1 = _layer_norm(DN_ALPHA * x_ref[...] + mod[2:3, :] * mix, g1_ref[...], b1_ref[...])
    x1_ref[...] = x1
    u2 = x1 * (1.0 + mod[4:5, :]) + mod[3:4, :]
    packed = _pack_pairs(u2)
    for j in range(PIECES):
        u2_ref[j] = packed[:, j * SC_ROW:(j + 1) * SC_ROW]
    uh = u2.astype(BF16)
    ul = (u2 - uh.astype(F32)).astype(BF16)
    lg_ref[...] = _dot_nt(wrh_ref[...], uh) + _dot_nt(wrh_ref[...], ul) + _dot_nt(wrl_ref[...], uh)


def _out_proj(oa, ob, ga, gb, x2, mod3, wa, wb, wo, g1, b1, wrh, wrl, seq, tm):
    t, d = x2.shape
    per_b = seq // tm
    row = lambda w: pl.BlockSpec((tm, w), lambda i: (i, 0))
    cst = lambda a: pl.BlockSpec(a.shape, lambda i: (0,) * a.ndim)
    return pl.pallas_call(
        _out_proj_kernel,
        out_shape=(jax.ShapeDtypeStruct((t, d), F32), jax.ShapeDtypeStruct((PIECES, t, SC_ROW), I32),
                   jax.ShapeDtypeStruct((N_EXPERTS, t), F32)),
        grid=(t // tm,),
        in_specs=[row(512), row(512), row(d), row(d), row(d),
                  pl.BlockSpec((1, 6, d), lambda i: (i // per_b, 0, 0)),
                  cst(wa), cst(wb), cst(wo), cst(g1), cst(b1), cst(wrh), cst(wrl)],
        out_specs=(row(d), pl.BlockSpec((PIECES, tm, SC_ROW), lambda i: (0, i, 0)),
                   pl.BlockSpec((N_EXPERTS, tm), lambda i: (0, i))),
        compiler_params=_cparams(("parallel",)),
        name="out_proj",
    )(oa, ob, ga, gb, x2, mod3, wa, wb, wo, g1, b1, wrh, wrl)


def _first_max(x, rows):
    m = jnp.max(x, axis=0, keepdims=True)
    idx = jnp.min(jnp.where(x == m, rows, 1e9), axis=0, keepdims=True)
    return m, idx


def _router_kernel(lg_ref, rb_ref, idx_ref, w_ref, pos_ref, cnt_ref, carry_ref, *, tm):
    @pl.when(pl.program_id(0) == 0)
    def _():
        carry_ref[...] = jnp.zeros_like(carry_ref)

    per_g = N_EXPERTS // N_GROUPS
    scores = _sigmoid(lg_ref[...])
    choice = scores + rb_ref[...][:, 0:1]
    rows = lax.broadcasted_iota(I32, (N_EXPERTS, tm), 0).astype(F32)
    rows_g = lax.broadcasted_iota(I32, (per_g, tm), 0).astype(F32)
    ninf = -jnp.inf

    gs = []
    for g in range(N_GROUPS):
        x = choice[g * per_g:(g + 1) * per_g, :]
        m1, i1 = _first_max(x, rows_g)
        m2 = jnp.max(jnp.where(rows_g == i1, ninf, x), axis=0, keepdims=True)
        gs.append(m1 + m2)
    gscore = jnp.concatenate(gs, axis=0)
    rows8 = lax.broadcasted_iota(I32, (N_GROUPS, tm), 0).astype(F32)
    keep = jnp.zeros((N_GROUPS, tm), F32)
    for _ in range(TOPK_GROUPS):
        _, gi = _first_max(gscore, rows8)
        hit = rows8 == gi
        keep = jnp.where(hit, 1.0, keep)
        gscore = jnp.where(hit, ninf, gscore)
    keep_full = jnp.concatenate(
        [jnp.broadcast_to(keep[g:g + 1, :], (per_g, tm)) for g in range(N_GROUPS)], axis=0)
    masked = jnp.where(keep_full > 0.0, choice, NEG)

    idxs, ws = [], []
    onehot = jnp.zeros((N_EXPERTS, tm), F32)
    for _ in range(TOP_K):
        _, ei = _first_max(masked, rows)
        hit = rows == ei
        idxs.append(ei)
        ws.append(jnp.sum(jnp.where(hit, scores, 0.0), axis=0, keepdims=True))
        masked = jnp.where(hit, ninf, masked)
        onehot = jnp.where(hit, 1.0, onehot)
    idx = jnp.concatenate(idxs, axis=0)
    w = jnp.concatenate(ws, axis=0)
    idx_ref[...] = idx.astype(I32)
    w = w / jnp.sum(w, axis=0, keepdims=True) * ROUTED_SCALE
    w_ref[...] = jnp.concatenate([w, jnp.zeros((LANES - TOP_K, tm), F32)], axis=0).T

    tri = _tri_strict_upper(tm, BF16)
    base = _dot(onehot.astype(BF16), tri) + carry_ref[...][:, 0:1]
    pos = [jnp.sum(jnp.where(rows == idxs[k], base, 0.0), axis=0, keepdims=True) for k in range(TOP_K)]
    pos_ref[...] = jnp.concatenate(pos, axis=0).astype(I32)
    carry = carry_ref[...] + jnp.sum(onehot, axis=1, keepdims=True)
    carry_ref[...] = carry
    cnt_ref[...] = carry


def _router(lg, router_bias, tm):
    e, t = lg.shape
    rb = jnp.broadcast_to(router_bias.reshape(e, 1).astype(F32), (e, LANES))
    tok = lambda r: pl.BlockSpec((r, tm), lambda i: (0, i))
    return pl.pallas_call(
        functools.partial(_router_kernel, tm=tm),
        out_shape=(jax.ShapeDtypeStruct((TOP_K, t), I32), jax.ShapeDtypeStruct((t, LANES), F32),
                   jax.ShapeDtypeStruct((TOP_K, t), I32), jax.ShapeDtypeStruct((e, LANES), F32)),
        grid=(t // tm,),
        in_specs=[tok(e), pl.BlockSpec((e, LANES), lambda i: (0, 0))],
        out_specs=(tok(TOP_K), pl.BlockSpec((tm, LANES), lambda i: (i, 0)), tok(TOP_K),
                   pl.BlockSpec((e, LANES), lambda i: (0, 0))),
        scratch_shapes=[pltpu.VMEM((e, LANES), F32)],
        compiler_params=_cparams(("arbitrary",)),
        name="router",
    )(lg, rb)


def _dest_kernel(idx_ref, pos_ref, st_ref, dest_ref, *, tm):
    rows = lax.broadcasted_iota(I32, (N_EXPERTS, tm), 0)
    starts = st_ref[...][:, 0:1]
    idx = idx_ref[...]
    out = []
    for k in range(TOP_K):
        out.append(jnp.sum(jnp.where(rows == idx[k:k + 1, :], starts, 0.0), axis=0, keepdims=True))
    dest_ref[...] = jnp.concatenate(out, axis=0).astype(I32) + pos_ref[...]


def _dest(idx, pos, starts, tm):
    k, t = idx.shape
    st = jnp.broadcast_to(starts.reshape(N_EXPERTS, 1).astype(F32), (N_EXPERTS, LANES))
    tok = pl.BlockSpec((k, tm), lambda i: (0, i))
    return pl.pallas_call(
        functools.partial(_dest_kernel, tm=tm),
        out_shape=jax.ShapeDtypeStruct((k, t), I32),
        grid=(t // tm,),
        in_specs=[tok, tok, pl.BlockSpec((N_EXPERTS, LANES), lambda i: (0, 0))],
        out_specs=tok,
        compiler_params=_cparams(("parallel",)),
        name="dest",
    )(idx, pos, st)


def _experts_kernel(blk_ref, used_ref, first_ref, slot_ref, next_ref, x_ref, wg_hbm, wu_hbm, wd_hbm, y_ref,
                    wg_buf, wu_buf, wd_buf, wg_bf, wu_bf, wd_bf, sems):
    b = pl.program_id(0)
    active = b < used_ref[0]

    def weight_copies(e, slot):
        return [pltpu.make_async_copy(hbm.at[e], buf.at[slot], sems.at[slot, i])
                for i, (hbm, buf) in enumerate(((wg_hbm, wg_buf), (wu_hbm, wu_buf), (wd_hbm, wd_buf)))]

    @pl.when(b == 0)
    def _():
        for cp in weight_copies(blk_ref[0], 0):
            cp.start()

    @pl.when(active & (first_ref[b] == 1))
    def _():
        slot = slot_ref[b]
        for cp in weight_copies(blk_ref[b], slot):
            cp.wait()
        nxt = next_ref[b]

        @pl.when(nxt >= 0)
        def _():
            for cp in weight_copies(nxt, 1 - slot):
                cp.start()

        wg_bf[...] = wg_buf[slot].astype(BF16)
        wu_bf[...] = wu_buf[slot].astype(BF16)
        wd_bf[...] = wd_buf[slot].astype(BF16)

    @pl.when(active)
    def _():
        x = jnp.concatenate([x_ref[j] for j in range(PIECES)], axis=1)
        x = _unpack_pairs(x).astype(BF16)
        a = _dot(x, wg_bf[...])
        u = _dot(x, wu_bf[...])
        h = (a * _sigmoid(a) * u).astype(BF16)
        y = _pack_pairs(_dot(h, wd_bf[...]))
        for j in range(PIECES):
            y_ref[j] = y[:, j * SC_ROW:(j + 1) * SC_ROW]


def _experts(xs, plan, wg, wu, wd, bm):
    _, cap, _ = xs.shape
    n_blocks = cap // bm
    d, f = wg.shape[1], wg.shape[2]
    rows = pl.BlockSpec((PIECES, bm, SC_ROW), lambda b, blk, used, *_: (0, jnp.minimum(b, used[0] - 1), 0))
    hbm = pl.BlockSpec(memory_space=pl.ANY)
    return pl.pallas_call(
        _experts_kernel,
        out_shape=jax.ShapeDtypeStruct(xs.shape, I32),
        grid_spec=pltpu.PrefetchScalarGridSpec(
            num_scalar_prefetch=5,
            grid=(n_blocks,),
            in_specs=[rows, hbm, hbm, hbm],
            out_specs=rows,
            scratch_shapes=[pltpu.VMEM((2, d, f), F32), pltpu.VMEM((2, d, f), F32), pltpu.VMEM((2, f, d), F32),
                            pltpu.VMEM((d, f), BF16), pltpu.VMEM((d, f), BF16), pltpu.VMEM((f, d), BF16),
                            pltpu.SemaphoreType.DMA((2, 3))]),
        compiler_params=_cparams(("arbitrary",)),
        name="experts",
    )(plan["blk_e"], plan["n_used"], plan["first"], plan["slot"], plan["next_e"], xs, wg, wu, wd)


def _final_kernel(x1_ref, u2_ref, yg_ref, w_ref, mod_ref, sg_ref, su_ref, sd_ref, g2_ref, b2_ref, o_ref):
    mod = mod_ref[0]
    w = w_ref[...]
    rows = lambda ref, *lead: jnp.concatenate([ref[(j,) + lead] for j in range(PIECES)], axis=1)
    routed = w[:, 0:1] * _unpack_pairs(rows(yg_ref, 0))
    for k in range(1, TOP_K):
        routed = routed + w[:, k:k + 1] * _unpack_pairs(rows(yg_ref, k))
    u = _unpack_pairs(rows(u2_ref)).astype(BF16)
    a = _dot(u, sg_ref[...])
    b = _dot(u, su_ref[...])
    shared = _dot((a * _sigmoid(a) * b).astype(BF16), sd_ref[...])
    y = DN_ALPHA * x1_ref[...] + mod[5:6, :] * (routed + shared)
    o_ref[...] = _layer_norm(y, g2_ref[...], b2_ref[...])


def _final(x1, u2p, yg, wtok, mod3, sg, su, sd, g2, b2, seq, tm):
    t, d = x1.shape
    per_b = seq // tm
    row = lambda w: pl.BlockSpec((tm, w), lambda i: (i, 0))
    cst = lambda a: pl.BlockSpec(a.shape, lambda i: (0,) * a.ndim)
    return pl.pallas_call(
        _final_kernel,
        out_shape=jax.ShapeDtypeStruct((t, d), F32),
        grid=(t // tm,),
        in_specs=[row(d), pl.BlockSpec((PIECES, tm, SC_ROW), lambda i: (0, i, 0)),
                  pl.BlockSpec((PIECES, TOP_K, tm, SC_ROW), lambda i: (0, 0, i, 0)),
                  row(LANES), pl.BlockSpec((1, 6, d), lambda i: (i // per_b, 0, 0)),
                  cst(sg), cst(su), cst(sd), cst(g2), cst(b2)],
        out_specs=row(d),
        compiler_params=_cparams(("parallel",)),
        name="final",
    )(x1, u2p, yg, wtok, mod3, sg, su, sd, g2, b2)


def _sc_mesh():
    return plsc.VectorSubcoreMesh(core_axis_name="core", subcore_axis_name="subcore")


def _sc_scatter_rows(src, idx, n_out, src_block):
    n_idx = idx.shape[0]

    @functools.partial(pl.kernel, out_type=jax.ShapeDtypeStruct((n_out, SC_ROW), src.dtype),
                       mesh=_sc_mesh(), scratch_types=[])
    def k(x_hbm, i_hbm, o_hbm):
        def body(x_vmem, i_vmem):
            pltpu.sync_copy(x_vmem, o_hbm.at[i_vmem.at[0]])

        pltpu.emit_pipeline(
            body, grid=(n_idx // SC_WINDOW,),
            in_specs=[pl.BlockSpec((SC_WINDOW, SC_ROW), lambda i: (src_block(i), 0)),
                      pl.BlockSpec((1, SC_WINDOW), lambda i: (0, i))],
            out_specs=[],
            core_axis_name=("core", "subcore"),
            dimension_semantics=(pltpu.PARALLEL,),
        )(x_hbm, i_hbm)

    return k(src, idx.reshape(1, n_idx))


def _sc_gather_rows(src, idx):
    n_idx = idx.shape[0]

    @functools.partial(pl.kernel, out_type=jax.ShapeDtypeStruct((n_idx, SC_ROW), src.dtype),
                       mesh=_sc_mesh(), scratch_types=[])
    def k(x_hbm, i_hbm, o_hbm):
        def body(i_vmem, o_vmem):
            pltpu.sync_copy(x_hbm.at[i_vmem.at[0]], o_vmem)

        pltpu.emit_pipeline(
            body, grid=(n_idx // SC_WINDOW,),
            in_specs=[pl.BlockSpec((1, SC_WINDOW), lambda i: (0, i))],
            out_specs=[pl.BlockSpec((SC_WINDOW, SC_ROW), lambda i: (i, 0))],
            core_axis_name=("core", "subcore"),
            dimension_semantics=(pltpu.PARALLEL,),
        )(i_hbm, o_hbm)

    return k(src, idx.reshape(1, n_idx))


def _moe_plan(counts, n_tok):
    bm = BM_EXPERT
    padded = (counts + bm - 1) // bm * bm
    p_ends = jnp.cumsum(padded)
    starts = p_ends - padded
    n_blocks = n_tok * TOP_K // bm + N_EXPERTS
    blk = jnp.arange(n_blocks, dtype=I32)
    blk_e = jnp.minimum(jnp.sum(p_ends[None, :] <= (blk * bm)[:, None], axis=1), N_EXPERTS - 1).astype(I32)
    n_used = (p_ends[-1] // bm).astype(I32)
    prev = jnp.concatenate([jnp.full((1,), -1, I32), blk_e[:-1]])
    first = ((blk_e != prev) & (blk < n_used)).astype(I32)
    slot = (jnp.cumsum(first) - 1) % 2
    first_pos = jnp.where(first == 1, blk, n_blocks)
    next_first = lax.cummin(jnp.concatenate([first_pos[1:], jnp.full((1,), n_blocks, I32)]), reverse=True)
    next_e = jnp.where(next_first < n_blocks, blk_e[jnp.minimum(next_first, n_blocks - 1)], -1)
    plan = dict(blk_e=blk_e, n_used=n_used.reshape(1), first=first, slot=slot.astype(I32),
                next_e=next_e.astype(I32))
    return starts, plan, n_blocks


def _layer(x, mod, positions, w_in, w_br_a, w_br_b, w_out, cmp_pos_k, cmp_pos_v, cmp_k_w1, cmp_k_w2,
           cmp_v_w1, cmp_v_w2, ln1_g, ln1_b, w_router, router_bias, w_exp_gate, w_exp_up, w_exp_down,
           w_sh_gate, w_sh_up, w_sh_down, ln2_g, ln2_b):
    bsz, seq, d = x.shape
    t = bsz * seq
    assert seq // CMP_STRIDE == LANES and seq % TQ_ATTN == 0
    assert seq % TM_PROJ == 0 and KC_ATTN == TM_PROJ
    x2 = x.reshape(t, d)
    mod3 = mod.reshape(bsz, 6, d)

    w_pack, w_small = _pack_w_in(w_in)
    z = _in_proj(x2, mod3, w_pack, w_small, _rope_tables(positions), seq, TM_PROJ)
    per_b = lambda name: z[name].reshape(bsz, seq, z[name].shape[1])

    o_a = _dsa(per_b("qi"), z["wT"], per_b("ki"), per_b("qa"), per_b("ka"), z["vaT"],
               bsz, seq, TQ_ATTN, KC_ATTN)

    n_rows = seq // CMP_STRIDE
    kc_x, vcT = _compress(z["kcmp"].reshape(bsz, n_rows, CMP_STRIDE * LANES),
                          z["vcmp"].reshape(bsz, n_rows, CMP_STRIDE * LANES),
                          cmp_pos_k, cmp_pos_v, cmp_k_w1, cmp_k_w2, cmp_v_w1, cmp_v_w2)
    o_b = _nsa(per_b("qbraw"), per_b("qbrot"), kc_x, vcT, per_b("ksel"), z["vselT"], per_b("kwin"),
               z["vwinT"], z["gT"], bsz, seq, TQ_ATTN, KC_ATTN)

    wr_hi = w_router.T.astype(BF16)
    wr_lo = (w_router.T - wr_hi.astype(F32)).astype(BF16)
    x1, u2p, logits = _out_proj(
        o_a.reshape(t, -1), o_b.reshape(t, -1), z["ga"], z["gb"], x2, mod3,
        w_br_a.astype(BF16), w_br_b.astype(BF16), w_out.astype(BF16),
        ln1_g.reshape(1, d), ln1_b.reshape(1, d), wr_hi, wr_lo, seq, TM_PROJ)

    idx, wtok, pos, counts = _router(logits, router_bias, TM_ROUTE)
    starts, plan, n_blocks = _moe_plan(counts[:, 0].astype(I32), t)
    dest = _dest(idx, pos, starts, TM_ROUTE)

    cap = n_blocks * BM_EXPERT
    dest_p = (dest[None] + (jnp.arange(PIECES, dtype=I32) * cap).reshape(PIECES, 1, 1)).reshape(-1)
    tb = t // SC_WINDOW
    xs = _sc_scatter_rows(u2p.reshape(PIECES * t, SC_ROW), dest_p, cap * PIECES,
                          lambda i: (i // (TOP_K * tb)) * tb + i % tb)
    ys = _experts(xs.reshape(PIECES, cap, SC_ROW), plan, w_exp_gate, w_exp_up, w_exp_down, BM_EXPERT)
    yg = _sc_gather_rows(ys.reshape(cap * PIECES, SC_ROW), dest_p).reshape(PIECES, TOP_K, t, SC_ROW)

    return _final(x1, u2p, yg, wtok, mod3, w_sh_gate.astype(BF16), w_sh_up.astype(BF16),
                  w_sh_down.astype(BF16), ln2_g.reshape(1, d), ln2_b.reshape(1, d), seq, TM_PROJ
                  ).reshape(bsz, seq, d)


def kernel(x, c, positions, w_ada, b_ada, w_in, w_br_a, w_br_b, w_out, cmp_pos_k, cmp_pos_v, cmp_k_w1,
           cmp_k_w2, cmp_v_w1, cmp_v_w2, ln1_g, ln1_b, w_router, router_bias, w_exp_gate, w_exp_up,
           w_exp_down, w_sh_gate, w_sh_up, w_sh_down, ln2_g, ln2_b):
    for l in range(w_ada.shape[0]):
        mod = _mod(c, w_ada[l], b_ada[l])
        x = _layer(x, mod, positions, w_in[l], w_br_a[l], w_br_b[l], w_out[l], cmp_pos_k[l], cmp_pos_v[l],
                   cmp_k_w1[l], cmp_k_w2[l], cmp_v_w1[l], cmp_v_w2[l], ln1_g[l], ln1_b[l], w_router[l],
                   router_bias[l], w_exp_gate[l], w_exp_up[l], w_exp_down[l], w_sh_gate[l], w_sh_up[l],
                   w_sh_down[l], ln2_g[l], ln2_b[l])
    return x
```

```python
import functools
import math

import jax
import jax.numpy as jnp
import numpy as np
from jax import lax
from jax.experimental import pallas as pl
from jax.experimental.pallas import tpu as pltpu
from jax.experimental.pallas import tpu_sc as plsc

F32 = jnp.float32
BF16 = jnp.bfloat16
I32 = jnp.int32

D_MODEL = 1024
HEAD_DIM = 64
ROPE_THETA = 500000.0
ROPE_FRACTION = 4
A_HEADS = 8
A_KV_HEADS = 2
IDX_HEADS = 8
IDX_DIM = 32
DSA_TOPK_MAX = 256
B_HEADS = 8
B_KV_HEADS = 2
REP = 4
CMP_BLOCK = 32
CMP_STRIDE = 16
CMP_HIDDEN = 256
SEL_BLOCK = 64
SEL_COUNT = 16
SEL_LOCAL = 2
WINDOW = 512
N_EXPERTS = 256
TOP_K = 8
N_GROUPS = 8
TOPK_GROUPS = 4
ROUTED_SCALE = 2.5
DEPTH = 1
DN_ALPHA = (2 * DEPTH) ** 0.25
LN_EPS = 1e-5
NEG = -1e30
FORCE = 1e9
INT_MIN = -2147483648
N_GATES = 3 * B_HEADS

LANES = 128
SUBLANES = 8
VMEM_LIMIT = 56 * 1024 * 1024
SC_WINDOW = 128
SC_ROW = 256
PIECES = (D_MODEL // 2) // SC_ROW

TM_PROJ = 512
TQ_ATTN = 512
WIN_TQ = 256
KC_ATTN = 512
TM_ROUTE = 512
BM_EXPERT = 512

_IN_WIDTHS = (512, 128, 128, 256, 32, 8, 512, 128, 128, 128, 128, 128, 128, 24, 1024, 1024)
_IN_OFFS = np.concatenate([[0], np.cumsum(_IN_WIDTHS)]).tolist()

NT_DIMS = (((1,), (1,)), ((), ()))


def _cparams(sem):
    return pltpu.CompilerParams(dimension_semantics=sem, vmem_limit_bytes=VMEM_LIMIT)


def _sigmoid(x):
    return 1.0 / (1.0 + jnp.exp(-x))


def _dot(a, b):
    return jnp.dot(a, b, preferred_element_type=F32)


def _dot_nt(a, b):
    return lax.dot_general(a, b, NT_DIMS, preferred_element_type=F32)


def _sort_key(x):
    x = jnp.where(x == 0.0, 0.0, x)
    bits = pltpu.bitcast(x, I32)
    return jnp.where(bits < 0, bits ^ 0x7FFFFFFF, bits)


def _kth_largest_key(count_ge, shape, k):
    kf = float(k)
    t0 = jnp.where(count_ge(jnp.zeros(shape, I32)) >= kf, 0, INT_MIN).astype(I32)

    def body(it, t):
        cand = t + jnp.left_shift(jnp.int32(1), 30 - it)
        return jnp.where(count_ge(cand) >= kf, cand, t)

    return lax.fori_loop(0, 31, body, t0)


def _tri_strict_lower(n, dtype):
    r = lax.broadcasted_iota(I32, (n, n), 0)
    c = lax.broadcasted_iota(I32, (n, n), 1)
    return jnp.where(c < r, 1.0, 0.0).astype(dtype)


def _tri_strict_upper(n, dtype):
    r = lax.broadcasted_iota(I32, (n, n), 0)
    c = lax.broadcasted_iota(I32, (n, n), 1)
    return jnp.where(r < c, 1.0, 0.0).astype(dtype)


def _mod_kernel(c_ref, w_ref, b_ref, o_ref):
    c = c_ref[...]
    cond = (c * _sigmoid(c)).astype(BF16)
    o_ref[...] = _dot(cond, w_ref[...].astype(BF16)) + b_ref[...]


def _mod(c, w_ada, b_ada):
    bsz, d = c.shape
    n = w_ada.shape[1]
    tn = 1024
    return pl.pallas_call(
        _mod_kernel,
        out_shape=jax.ShapeDtypeStruct((bsz, n), F32),
        grid=(n // tn,),
        in_specs=[pl.BlockSpec((bsz, d), lambda j: (0, 0)),
                  pl.BlockSpec((d, tn), lambda j: (0, j)),
                  pl.BlockSpec((1, tn), lambda j: (0, j))],
        out_specs=pl.BlockSpec((bsz, tn), lambda j: (0, j)),
        compiler_params=_cparams(("parallel",)),
        name="mod",
    )(c, w_ada, b_ada.reshape(1, n))


def _rope(z, c_tab, s_tab, period, half):
    w = z.shape[1]
    reps = w // LANES
    c = jnp.concatenate([c_tab] * reps, axis=1) if reps > 1 else c_tab
    s = jnp.concatenate([s_tab] * reps, axis=1) if reps > 1 else s_tab
    lane = lax.broadcasted_iota(I32, z.shape, 1)
    first = (lane & (period - 1)) < half
    partner = jnp.where(first, pltpu.roll(z, w - half, axis=1), pltpu.roll(z, half, axis=1))
    return z * c + partner * s


def _in_proj_kernel(x_ref, mod_ref, w_ref, wsm_ref, c64_ref, s64_ref, c32_ref, s32_ref,
                    qa_ref, ka_ref, vaT_ref, qi_ref, ki_ref, wT_ref, qbraw_ref, qbrot_ref,
                    kcmp_ref, vcmp_ref, ksel_ref, vselT_ref, kwin_ref, vwinT_ref, gT_ref, ga_ref, gb_ref):
    mod = mod_ref[0]
    u = (x_ref[...] * (1.0 + mod[1:2, :]) + mod[0:1, :]).astype(BF16)
    tm = u.shape[0]
    c64, s64, c32, s32 = c64_ref[...], s64_ref[...], c32_ref[...], s32_ref[...]
    scale = HEAD_DIM ** -0.5
    lane = lax.broadcasted_iota(I32, (tm, LANES), 1)
    low = lane < HEAD_DIM

    def proj(a, b):
        return _dot(u, w_ref[:, a:b])

    rope64 = lambda z: _rope(z, c64, s64, HEAD_DIM, HEAD_DIM // ROPE_FRACTION // 2)
    rope32 = lambda z: _rope(z, c32, s32, IDX_DIM, IDX_DIM // ROPE_FRACTION // 2)

    def head_slots64(z):
        out = []
        for h in range(A_HEADS):
            pair = z[:, (h // 2) * LANES:(h // 2 + 1) * LANES]
            g = h // REP
            src = pair if h % 2 == g else pltpu.roll(pair, HEAD_DIM, axis=1)
            out.append(jnp.where(low, src, 0.0) if g == 0 else jnp.where(low, 0.0, src))
        return jnp.concatenate(out, axis=1).astype(BF16)

    def head_slots32(z):
        per = LANES // IDX_DIM
        out = []
        for h in range(IDX_HEADS):
            col = z[:, (h // per) * LANES:(h // per + 1) * LANES]
            shift = IDX_DIM * (h % per)
            src = col if shift == 0 else pltpu.roll(col, LANES - shift, axis=1)
            out.append(jnp.where(lane < IDX_DIM, src, 0.0))
        return jnp.concatenate(out, axis=1).astype(BF16)

    def store_vt(ref, z, chunk):
        zt = z.T
        ones = jnp.ones((HEAD_DIM, chunk), F32)
        for g in range(A_KV_HEADS):
            for j in range(tm // chunk):
                blk = zt[g * HEAD_DIM:(g + 1) * HEAD_DIM, j * chunk:(j + 1) * chunk]
                ref[g, j] = jnp.concatenate([blk, ones], axis=0).astype(BF16)

    qa_ref[...] = head_slots64(rope64(proj(0, 512)) * scale)
    ka_ref[...] = rope64(proj(512, 640)).astype(BF16)
    store_vt(vaT_ref, proj(640, 768), tm)
    qi_ref[...] = head_slots32(rope32(proj(768, 1024)))
    ki_ref[...] = rope32(proj(1024, 1152)).astype(BF16)
    qb = proj(1152, 1664)
    qbraw_ref[...] = head_slots64(qb * scale)
    qbrot_ref[...] = head_slots64(rope64(qb) * scale)
    kcmp_ref[...] = proj(1664, 1792)
    vcmp_ref[...] = proj(1792, 1920)
    ksel_ref[...] = rope64(proj(1920, 2048)).astype(BF16)
    store_vt(vselT_ref, proj(2048, 2176), tm)
    kwin_ref[...] = rope64(proj(2176, 2304)).astype(BF16)
    store_vt(vwinT_ref, proj(2304, 2432), LANES)
    ga_ref[...] = _sigmoid(proj(2432, 3456)).astype(BF16)
    gb_ref[...] = _sigmoid(proj(3456, 4480)).astype(BF16)
    small = _dot_nt(wsm_ref[...], u)
    wT_ref[...] = small[0:IDX_HEADS, :]
    gT_ref[...] = _sigmoid(small[IDX_HEADS:IDX_HEADS + N_GATES, :])


def _pack_w_in(w_in):
    d = w_in.shape[0]
    col = lambda i: w_in[:, _IN_OFFS[i]:_IN_OFFS[i + 1]]
    ki = jnp.concatenate([col(4), jnp.zeros((d, LANES - IDX_DIM), w_in.dtype)], axis=1)
    parts = [col(0), col(1), col(2), col(3), ki, col(6), col(7), col(8), col(9), col(10),
             col(11), col(12), col(14), col(15)]
    w_small = jnp.concatenate([col(5), col(13)], axis=1).T
    return jnp.concatenate(parts, axis=1).astype(BF16), w_small.astype(BF16)


def _rope_tables(positions):
    pos = positions.astype(F32).reshape(-1, 1)

    def tab(dim):
        rot = dim // ROPE_FRACTION
        half = rot // 2
        inv = ROPE_THETA ** (-(jnp.arange(half, dtype=F32) * 2.0) / rot)
        ang = pos * inv
        cos, sin = jnp.cos(ang), jnp.sin(ang)
        ones = jnp.ones((pos.shape[0], dim - rot), F32)
        c = jnp.concatenate([cos, cos, ones], axis=1)
        s = jnp.concatenate([-sin, sin, 0.0 * ones], axis=1)
        return jnp.tile(c, (1, LANES // dim)), jnp.tile(s, (1, LANES // dim))

    return tab(HEAD_DIM) + tab(IDX_DIM)


def _in_proj(x2, mod3, w_pack, w_small, tabs, seq, tm):
    t, d = x2.shape
    n = w_pack.shape[1]
    per_b = seq // tm
    g = A_KV_HEADS
    row = lambda w: pl.BlockSpec((tm, w), lambda i: (i, 0))
    tok = lambda r: pl.BlockSpec((r, tm), lambda i: (0, i))
    vt_chunk = pl.BlockSpec((g, 1, LANES, tm), lambda i: (0, i, 0, 0))
    vt_lane = pl.BlockSpec((g, tm // LANES, LANES, LANES), lambda i: (0, i, 0, 0))
    sds = jax.ShapeDtypeStruct
    vt_chunk_shape = sds((g, t // tm, LANES, tm), BF16)
    outs = (("qa", sds((t, 1024), BF16), row(1024)), ("ka", sds((t, LANES), BF16), row(LANES)),
            ("vaT", vt_chunk_shape, vt_chunk), ("qi", sds((t, 1024), BF16), row(1024)),
            ("ki", sds((t, LANES), BF16), row(LANES)), ("wT", sds((IDX_HEADS, t), F32), tok(IDX_HEADS)),
            ("qbraw", sds((t, 1024), BF16), row(1024)), ("qbrot", sds((t, 1024), BF16), row(1024)),
            ("kcmp", sds((t, LANES), F32), row(LANES)), ("vcmp", sds((t, LANES), F32), row(LANES)),
            ("ksel", sds((t, LANES), BF16), row(LANES)), ("vselT", vt_chunk_shape, vt_chunk),
            ("kwin", sds((t, LANES), BF16), row(LANES)),
            ("vwinT", sds((g, t // LANES, LANES, LANES), BF16), vt_lane),
            ("gT", sds((N_GATES, t), F32), tok(N_GATES)),
            ("ga", sds((t, d), BF16), row(d)), ("gb", sds((t, d), BF16), row(d)))
    res = pl.pallas_call(
        _in_proj_kernel,
        out_shape=tuple(o[1] for o in outs),
        grid=(t // tm,),
        in_specs=[row(d),
                  pl.BlockSpec((1, 6, d), lambda i: (i // per_b, 0, 0)),
                  pl.BlockSpec((d, n), lambda i: (0, 0)),
                  pl.BlockSpec(w_small.shape, lambda i: (0, 0)),
                  row(LANES), row(LANES), row(LANES), row(LANES)],
        out_specs=tuple(o[2] for o in outs),
        compiler_params=_cparams(("parallel",)),
        name="in_proj",
    )(x2, mod3, w_pack, w_small, *tabs)
    return {o[0]: r for o, r in zip(outs, res)}


def _fold_rows(x, op):
    n = x.shape[0]
    while n % (2 * SUBLANES) == 0:
        n //= 2
        x = op(x[:n], x[n:])
    slabs = [x[i * SUBLANES:(i + 1) * SUBLANES] for i in range(n // SUBLANES)]
    while len(slabs) > 1:
        nxt = [op(slabs[i], slabs[i + 1]) for i in range(0, len(slabs) - 1, 2)]
        slabs = nxt + ([slabs[-1]] if len(slabs) % 2 else [])
    return slabs[0]


def _col_max(x):
    return jnp.max(_fold_rows(x, jnp.maximum), axis=0, keepdims=True)


def _col_sum(x):
    return jnp.sum(_fold_rows(x, jnp.add), axis=0, keepdims=True)


PACKED_ROWS = 16


def _fold_rows_packed(x):
    n = x.shape[0]
    assert n % PACKED_ROWS == 0 and n // PACKED_ROWS <= 256
    while n > PACKED_ROWS:
        n //= 2
        x = x[:n] + x[n:]
    return x


def _stack_heads(q_ref, heads):
    return jnp.concatenate([q_ref[0, :, h * LANES:(h + 1) * LANES] for h in heads], axis=0)


def _flash_step(k, q_stack, v_t, bias4, m, acc):
    s = _dot_nt(k, q_stack) + bias4
    m_new = jnp.maximum(m, _col_max(s))
    e = jnp.exp(s - m_new).astype(BF16)
    return m_new, acc * jnp.exp(m - m_new) + _dot(v_t, e)


HEADS_PER_CHAIN = REP


def _head_chains():
    return [(h // REP, tuple(range(h, h + HEADS_PER_CHAIN))) for h in range(0, A_HEADS, HEADS_PER_CHAIN)]


def _flash_loop(q_ref, k_ref, vT_ref, bias_of, n_chunks, tq, kc):
    chains = _head_chains()
    q_stacks = [_stack_heads(q_ref, heads) for _, heads in chains]
    width = HEADS_PER_CHAIN * tq

    def body(c, carry):
        k = k_ref[0, pl.ds(pl.multiple_of(c * kc, kc), kc), :]
        out = []
        for (g, _), q_stack, (m, acc) in zip(chains, q_stacks, carry):
            bias = jnp.concatenate([bias_of(g, c)] * HEADS_PER_CHAIN, axis=1)
            out.append(_flash_step(k, q_stack, vT_ref[g, c], bias, m, acc))
        return tuple(out)

    init = tuple((jnp.full((1, width), NEG, F32), jnp.zeros((LANES, width), F32)) for _ in chains)
    res = lax.fori_loop(0, n_chunks, body, init)
    return [(heads, acc) for (_, heads), (_, acc) in zip(chains, res)]


def _normalise(acc):
    return acc / jnp.maximum(acc[HEAD_DIM:HEAD_DIM + 1, :], 1e-30)


def _store_heads(o_ref, o_t, heads, tq):
    o = o_t.T
    for r, h in enumerate(heads):
        o_ref[0, :, h * HEAD_DIM:(h + 1) * HEAD_DIM] = o[r * tq:(r + 1) * tq, 0:HEAD_DIM].astype(o_ref.dtype)


def _select_mask(keys, thr, need, offset, tri):
    gt = jnp.where(keys > thr, 1.0, 0.0)
    eq = jnp.where(keys == thr, 1.0, 0.0)
    prefix = _dot(tri, eq.astype(BF16)) + offset
    return gt + jnp.where(prefix < need, eq, 0.0), offset + _col_sum(eq)


def _dsa_kernel(qi_ref, wT_ref, ki_ref, qa_ref, ka_ref, vT_ref, o_ref, keys_ref, bias_ref, planes_ref,
                *, tq, kc, n_keep):
    t0 = pl.program_id(1) * tq
    n_chunks = (t0 + tq + kc - 1) // kc
    wT = wT_ref[...] * (IDX_HEADS ** -0.5 * IDX_DIM ** -0.5)
    qi_stack = _stack_heads(qi_ref, range(IDX_HEADS))
    keypos = lax.broadcasted_iota(I32, (kc, tq), 0)
    qpos = t0 + lax.broadcasted_iota(I32, (kc, tq), 1)

    def score_chunk(c, carry):
        k0 = pl.multiple_of(c * kc, kc)
        sc = _dot_nt(ki_ref[0, pl.ds(k0, kc), :], qi_stack)
        score = jnp.zeros((kc, tq), F32)
        for h in range(IDX_HEADS):
            score = score + wT[h:h + 1, :] * jnp.maximum(sc[:, h * tq:(h + 1) * tq], 0.0)
        key = jnp.where(k0 + keypos <= qpos, _sort_key(score), INT_MIN)
        keys_ref[c] = key
        flipped = key ^ INT_MIN
        for lvl in range(4):
            byte = lax.shift_right_logical(flipped, 8 * lvl) & 0xFF
            planes_ref[lvl, c] = byte.astype(F32).astype(BF16)
        return carry

    lax.fori_loop(0, n_chunks, score_chunk, 0)

    one = jnp.ones((kc, tq), BF16)
    zero = jnp.zeros((kc, tq), BF16)
    dead = jnp.full((kc, tq), -1.0, BF16)

    def count_where(lvl, pick, also=None):
        def body(c, acc):
            p = planes_ref[lvl, c]
            if also is not None:
                also(c, p)
            return acc + _fold_rows_packed(jnp.where(pick(p), one, zero)).astype(F32)
        acc = lax.fori_loop(0, n_chunks, body, jnp.zeros((PACKED_ROWS, tq), F32))
        return jnp.sum(acc, axis=0, keepdims=True)

    above = jnp.zeros((1, tq), F32)
    thr_u = jnp.zeros((1, tq), I32)
    for lvl in (3, 2, 1, 0):
        def bit_step(it, t, lvl=lvl, above=above):
            cand = t + jnp.left_shift(jnp.int32(1), 7 - it).astype(F32)
            cnt = above + count_where(lvl, lambda p: p >= cand.astype(BF16))
            return jnp.where(cnt >= float(n_keep), cand, t)

        t = lax.fori_loop(0, 8, bit_step, jnp.zeros((1, tq), F32))
        tb = t.astype(BF16)

        def narrow(c, p, lvl=lvl, tb=tb):
            planes_ref[lvl - 1, c] = jnp.where(p == tb, planes_ref[lvl - 1, c], dead)

        above = above + count_where(lvl, lambda p: p > tb, narrow if lvl > 0 else None)
        thr_u = thr_u | jnp.left_shift(t.astype(I32), 8 * lvl)
    thr = thr_u ^ INT_MIN
    need = float(n_keep) - above
    tri = _tri_strict_lower(LANES, BF16)
    sub = lax.broadcasted_iota(I32, (LANES, tq), 0)
    qsub = t0 + lax.broadcasted_iota(I32, (LANES, tq), 1)

    def bias_chunk(c, offset):
        for j in range(kc // LANES):
            rows = slice(j * LANES, (j + 1) * LANES)
            sel, offset = _select_mask(keys_ref[c, rows, :], thr, need, offset, tri)
            causal = c * kc + j * LANES + sub <= qsub
            bias_ref[c, rows, :] = jnp.where(causal, (sel - 1.0) * 1e30, NEG)
        return offset

    lax.fori_loop(0, n_chunks, bias_chunk, jnp.zeros((1, tq), F32))

    accs = _flash_loop(qa_ref, ka_ref, vT_ref, lambda g, c: bias_ref[c], n_chunks, tq, kc)
    for heads, acc in accs:
        _store_heads(o_ref, _normalise(acc), heads, tq)


def _dsa(qi, wT, ki, qa, ka, vaT, bsz, seq, tq, kc):
    n_keep = min(DSA_TOPK_MAX, seq // 4)
    nq = seq // tq
    qblk = lambda w: pl.BlockSpec((1, tq, w), lambda b, i: (b, i, 0))
    full = lambda w: pl.BlockSpec((1, seq, w), lambda b, i: (b, 0, 0))
    return pl.pallas_call(
        functools.partial(_dsa_kernel, tq=tq, kc=kc, n_keep=n_keep),
        out_shape=jax.ShapeDtypeStruct((bsz, seq, A_HEADS * HEAD_DIM), BF16),
        grid=(bsz, nq),
        in_specs=[qblk(1024), pl.BlockSpec((IDX_HEADS, tq), lambda b, i: (0, b * nq + i)), full(LANES),
                  qblk(1024), full(LANES),
                  pl.BlockSpec((A_KV_HEADS, seq // kc, LANES, kc), lambda b, i: (0, b, 0, 0))],
        out_specs=qblk(512),
        scratch_shapes=[pltpu.VMEM((seq // kc, kc, tq), I32), pltpu.VMEM((seq // kc, kc, tq), F32),
                        pltpu.VMEM((4, seq // kc, kc, tq), BF16)],
        compiler_params=_cparams(("parallel", "parallel")),
        name="dsa",
    )(qi, wT, ki, qa, ka, vaT)


def _gelu_tanh(x):
    return 0.5 * x * (1.0 + jnp.tanh(math.sqrt(2.0 / math.pi) * (x + 0.044715 * (x * x * x))))


def _compress_kernel(k_ref, v_ref, pk_ref, pv_ref, w1k_ref, w2k_ref, w1v_ref, w2v_ref,
                     kc_ref, vcT_ref, *, n_rows):
    half = CMP_BLOCK // 2

    def one(x_ref, p_ref, w1_ref, w2_ref):
        outs = []
        for g in range(B_KV_HEADS):
            lo = jnp.zeros((n_rows, CMP_HIDDEN), F32)
            hi = jnp.zeros((n_rows, CMP_HIDDEN), F32)
            for l in range(half):
                xl = x_ref[0, :, l * LANES + g * HEAD_DIM:l * LANES + (g + 1) * HEAD_DIM]
                a = (xl + p_ref[l:l + 1, :]).astype(BF16)
                b = (xl + p_ref[half + l:half + l + 1, :]).astype(BF16)
                lo = lo + _dot(a, w1_ref[l * HEAD_DIM:(l + 1) * HEAD_DIM, :].astype(BF16))
                hi = hi + _dot(b, w1_ref[(half + l) * HEAD_DIM:(half + l + 1) * HEAD_DIM, :].astype(BF16))
            hid = lo + pltpu.roll(hi, n_rows - 1, axis=0)
            outs.append(_dot(_gelu_tanh(hid).astype(BF16), w2_ref[...].astype(BF16)))
        return outs

    k0, k1 = one(k_ref, pk_ref, w1k_ref, w2k_ref)
    kc_ref[0] = jnp.concatenate([k0, k1], axis=1).astype(kc_ref.dtype)
    for g, v in enumerate(one(v_ref, pv_ref, w1v_ref, w2v_ref)):
        vcT_ref[0, g] = jnp.concatenate([v, jnp.zeros_like(v)], axis=1).T.astype(vcT_ref.dtype)


def _compress(kcmp, vcmp, pos_k, pos_v, w1k, w2k, w1v, w2v):
    bsz, n_rows, width = kcmp.shape
    xspec = pl.BlockSpec((1, n_rows, width), lambda b: (b, 0, 0))
    cst = lambda a: pl.BlockSpec(a.shape, lambda b: (0,) * a.ndim)
    return pl.pallas_call(
        functools.partial(_compress_kernel, n_rows=n_rows),
        out_shape=(jax.ShapeDtypeStruct((bsz, n_rows, LANES), BF16),
                   jax.ShapeDtypeStruct((bsz, B_KV_HEADS, LANES, n_rows), BF16)),
        grid=(bsz,),
        in_specs=[xspec, xspec, cst(pos_k), cst(pos_v), cst(w1k), cst(w2k), cst(w1v), cst(w2v)],
        out_specs=(pl.BlockSpec((1, n_rows, LANES), lambda b: (b, 0, 0)),
                   pl.BlockSpec((1, B_KV_HEADS, LANES, n_rows), lambda b: (b, 0, 0, 0))),
        compiler_params=_cparams(("parallel",)),
        name="compress",
    )(kcmp, vcmp, pos_k, pos_v, w1k, w2k, w1v, w2v)


def _split3(x):
    a = x.astype(BF16)
    r = x - a.astype(F32)
    b = r.astype(BF16)
    c = (r - b.astype(F32)).astype(BF16)
    return a, b, c


def _nsa_kernel(qraw_ref, qrot_ref, kc_ref, vcT_ref, ksel_ref, vselT_ref, kwin_ref, vwinT_ref,
                gT_ref, exp_ref, o_ref, bias_ref, *, tq, kc, seq, n_c):
    t0 = pl.program_id(1) * tq
    n_chunks = (t0 + tq + kc - 1) // kc
    n_s = seq // SEL_BLOCK
    n_pick = min(SEL_COUNT, n_s)
    wt = min(tq, WIN_TQ)
    span = WINDOW + wt
    gT = gT_ref[...]

    ridx = lax.broadcasted_iota(I32, (LANES, tq), 0)
    tl = t0 + lax.broadcasted_iota(I32, (LANES, tq), 1)
    valid_c = jnp.where((ridx * CMP_STRIDE + CMP_BLOCK - 1 <= tl) & (ridx < n_c), 1.0, 0.0)
    valid4 = jnp.concatenate([valid_c] * REP, axis=1) > 0.0
    js = lax.broadcasted_iota(I32, (LANES, LANES), 0) * SEL_BLOCK
    cs = lax.broadcasted_iota(I32, (LANES, LANES), 1) * CMP_STRIDE
    overlap_t = jnp.where((cs <= js + SEL_BLOCK - 1) & (cs + CMP_BLOCK - 1 >= js), 1.0, 0.0).astype(BF16)
    cur = tl // SEL_BLOCK
    forced = (ridx == 0) | ((cur - ridx >= 0) & (cur - ridx < SEL_LOCAL))
    blk_causal = ridx * SEL_BLOCK <= tl
    tri = _tri_strict_lower(LANES, BF16)
    keypos = lax.broadcasted_iota(I32, (kc, tq), 0)
    qpos = t0 + lax.broadcasted_iota(I32, (kc, tq), 1)

    o_cmp_g = []
    for g in range(B_KV_HEADS):
        heads = range(g * REP, (g + 1) * REP)
        s = jnp.where(valid4, _dot_nt(kc_ref[0], _stack_heads(qraw_ref, heads)), NEG)
        e = jnp.where(valid4, jnp.exp(s - _col_max(s)), 0.0)
        p = e / jnp.maximum(_col_sum(e), 1e-30)
        o_cmp_g.append(_dot(vcT_ref[0, g], p.astype(BF16)))
        p_sum = p[:, 0:tq]
        for r in range(1, REP):
            p_sum = p_sum + p[:, r * tq:(r + 1) * tq]
        pa, pb, pc = _split3(p_sum)
        imp = _dot(overlap_t, pa) + _dot(overlap_t, pb) + _dot(overlap_t, pc)
        imp = jnp.where(forced, FORCE, jnp.where(blk_causal, imp, NEG))
        keys = _sort_key(imp[0:n_s, :])
        count_ge = lambda cand, keys=keys: _col_sum(jnp.where(keys >= cand, 1.0, 0.0))
        thr = _kth_largest_key(count_ge, (1, tq), n_pick)
        need = float(n_pick) - _col_sum(jnp.where(keys > thr, 1.0, 0.0))
        blk_sel, _ = _select_mask(keys, thr, need, jnp.zeros((1, tq), F32), tri[0:n_s, 0:n_s])
        blk_sel = jnp.concatenate([blk_sel, jnp.zeros((LANES - n_s, tq), F32)], axis=0).astype(BF16)

        def bias_chunk(c, carry, blk_sel=blk_sel, g=g):
            tok_sel = _dot(exp_ref[c], blk_sel)
            bias_ref[g, c] = jnp.where(c * kc + keypos <= qpos, (tok_sel - 1.0) * 1e30, NEG)
            return carry

        lax.fori_loop(0, n_chunks, bias_chunk, 0)

    sel = _flash_loop(qrot_ref, ksel_ref, vselT_ref, lambda g, c: bias_ref[g, c], n_chunks, tq, kc)

    for (g, heads), (_, acc_sel) in zip(_head_chains(), sel):
        o_win_sub = []
        for sub in range(tq // wt):
            t_sub = t0 + sub * wt
            w0 = pl.multiple_of(jnp.clip(t_sub - WINDOW, 0, seq - span), wt)
            wdiff = ((t_sub + lax.broadcasted_iota(I32, (span, wt), 1))
                     - (w0 + lax.broadcasted_iota(I32, (span, wt), 0)))
            wbias = jnp.where((wdiff >= 0) & (wdiff < WINDOW), 0.0, NEG)
            q_sub = jnp.concatenate([qrot_ref[0, sub * wt:(sub + 1) * wt, h * LANES:(h + 1) * LANES]
                                     for h in heads], axis=0)
            sw = (_dot_nt(kwin_ref[0, pl.ds(w0, span), :], q_sub)
                  + jnp.concatenate([wbias] * HEADS_PER_CHAIN, axis=1))
            ew = jnp.exp(sw - _col_max(sw)).astype(BF16)
            wblk = w0 // LANES
            acc_win = _dot(vwinT_ref[g, wblk], ew[0:LANES, :])
            for j in range(1, span // LANES):
                acc_win = acc_win + _dot(vwinT_ref[g, wblk + j], ew[j * LANES:(j + 1) * LANES, :])
            o_win_sub.append(_normalise(acc_win))

        o_sel = _normalise(acc_sel)
        cols = []
        for r, h in enumerate(heads):
            col = slice(r * tq, (r + 1) * tq)
            cmp_col = slice((h % REP) * tq, (h % REP + 1) * tq)
            o_win = jnp.concatenate([o[:, r * wt:(r + 1) * wt] for o in o_win_sub], axis=1)
            cols.append(gT[3 * h:3 * h + 1, :] * o_cmp_g[g][:, cmp_col]
                        + gT[3 * h + 1:3 * h + 2, :] * o_sel[:, col]
                        + gT[3 * h + 2:3 * h + 3, :] * o_win)
        _store_heads(o_ref, jnp.concatenate(cols, axis=1), heads, tq)


def _nsa(qraw, qrot, kc_x, vcT, ksel, vselT, kwin, vwinT, gT, bsz, seq, tq, kc):
    n_c = (seq - CMP_BLOCK) // CMP_STRIDE + 1
    nq = seq // tq
    key_blk = (jnp.arange(seq, dtype=I32) // SEL_BLOCK).reshape(seq // kc, kc, 1)
    expand = (key_blk == jnp.arange(LANES, dtype=I32).reshape(1, 1, LANES)).astype(BF16)
    qblk = lambda w: pl.BlockSpec((1, tq, w), lambda b, i: (b, i, 0))
    full = lambda w: pl.BlockSpec((1, seq, w), lambda b, i: (b, 0, 0))
    g = B_KV_HEADS
    return pl.pallas_call(
        functools.partial(_nsa_kernel, tq=tq, kc=kc, seq=seq, n_c=n_c),
        out_shape=jax.ShapeDtypeStruct((bsz, seq, B_HEADS * HEAD_DIM), BF16),
        grid=(bsz, nq),
        in_specs=[qblk(1024), qblk(1024),
                  pl.BlockSpec((1,) + kc_x.shape[1:], lambda b, i: (b, 0, 0)),
                  pl.BlockSpec((1,) + vcT.shape[1:], lambda b, i: (b, 0, 0, 0)),
                  full(LANES), pl.BlockSpec((g, seq // kc, LANES, kc), lambda b, i: (0, b, 0, 0)),
                  full(LANES), pl.BlockSpec((g, seq // LANES, LANES, LANES), lambda b, i: (0, b, 0, 0)),
                  pl.BlockSpec((N_GATES, tq), lambda b, i: (0, b * nq + i)),
                  pl.BlockSpec(expand.shape, lambda b, i: (0, 0, 0))],
        out_specs=qblk(512),
        scratch_shapes=[pltpu.VMEM((g, seq // kc, kc, tq), F32)],
        compiler_params=_cparams(("parallel", "parallel")),
        name="nsa",
    )(qraw, qrot, kc_x, vcT, ksel, vselT, kwin, vwinT, gT, expand)


def _pack_pairs(x):
    n = x.shape[1] // 2
    lo = pltpu.bitcast(x[:, :n].astype(BF16).astype(F32), I32)
    hi = pltpu.bitcast(x[:, n:].astype(BF16).astype(F32), I32)
    return lax.shift_right_logical(lo, 16) | (hi & jnp.int32(-65536))


def _unpack_pairs(p):
    lo = pltpu.bitcast(lax.shift_left(p, 16), F32)
    hi = pltpu.bitcast(p & jnp.int32(-65536), F32)
    return jnp.concatenate([lo, hi], axis=1)


def _layer_norm(y, g, b):
    mu = jnp.mean(y, axis=1, keepdims=True)
    yc = y - mu
    var = jnp.mean(yc * yc, axis=1, keepdims=True)
    return yc * lax.rsqrt(var + LN_EPS) * g + b


def _out_proj_kernel(oa_ref, ob_ref, ga_ref, gb_ref, x_ref, mod_ref, wa_ref, wb_ref, wo_ref,
                     g1_ref, b1_ref, wrh_ref, wrl_ref, x1_ref, u2_ref, lg_ref):
    mod = mod_ref[0]
    merged = (ga_ref[...].astype(F32) * _dot(oa_ref[...], wa_ref[...])
              + gb_ref[...].astype(F32) * _dot(ob_ref[...], wb_ref[...]))
    mix = _dot(merged.astype(BF16), wo_ref[...])
    x1 = _layer_norm(DN_ALPHA * x_ref[...] + mod[2:3, :] * mix, g1_ref[...], b1_ref[...])
    x1_ref[...] = x1
    u2 = x1 * (1.0 + mod[4:5, :]) + mod[3:4, :]
    packed = _pack_pairs(u2)
    for j in range(PIECES):
        u2_ref[j] = packed[:, j * SC_ROW:(j + 1) * SC_ROW]
    uh = u2.astype(BF16)
    ul = (u2 - uh.astype(F32)).astype(BF16)
    lg_ref[...] = _dot_nt(wrh_ref[...], uh) + _dot_nt(wrh_ref[...], ul) + _dot_nt(wrl_ref[...], uh)


def _out_proj(oa, ob, ga, gb, x2, mod3, wa, wb, wo, g1, b1, wrh, wrl, seq, tm):
    t, d = x2.shape
    per_b = seq // tm
    row = lambda w: pl.BlockSpec((tm, w), lambda i: (i, 0))
    cst = lambda a: pl.BlockSpec(a.shape, lambda i: (0,) * a.ndim)
    return pl.pallas_call(
        _out_proj_kernel,
        out_shape=(jax.ShapeDtypeStruct((t, d), F32), jax.ShapeDtypeStruct((PIECES, t, SC_ROW), I32),
                   jax.ShapeDtypeStruct((N_EXPERTS, t), F32)),
        grid=(t // tm,),
        in_specs=[row(512), row(512), row(d), row(d), row(d),
                  pl.BlockSpec((1, 6, d), lambda i: (i // per_b, 0, 0)),
                  cst(wa), cst(wb), cst(wo), cst(g1), cst(b1), cst(wrh), cst(wrl)],
        out_specs=(row(d), pl.BlockSpec((PIECES, tm, SC_ROW), lambda i: (0, i, 0)),
                   pl.BlockSpec((N_EXPERTS, tm), lambda i: (0, i))),
        compiler_params=_cparams(("parallel",)),
        name="out_proj",
    )(oa, ob, ga, gb, x2, mod3, wa, wb, wo, g1, b1, wrh, wrl)


def _first_max(x, rows):
    m = jnp.max(x, axis=0, keepdims=True)
    idx = jnp.min(jnp.where(x == m, rows, 1e9), axis=0, keepdims=True)
    return m, idx


def _router_kernel(lg_ref, rb_ref, idx_ref, w_ref, pos_ref, cnt_ref, carry_ref, *, tm):
    @pl.when(pl.program_id(0) == 0)
    def _():
        carry_ref[...] = jnp.zeros_like(carry_ref)

    per_g = N_EXPERTS // N_GROUPS
    scores = _sigmoid(lg_ref[...])
    choice = scores + rb_ref[...][:, 0:1]
    rows = lax.broadcasted_iota(I32, (N_EXPERTS, tm), 0).astype(F32)
    rows_g = lax.broadcasted_iota(I32, (per_g, tm), 0).astype(F32)
    ninf = -jnp.inf

    gs = []
    for g in range(N_GROUPS):
        x = choice[g * per_g:(g + 1) * per_g, :]
        m1, i1 = _first_max(x, rows_g)
        m2 = jnp.max(jnp.where(rows_g == i1, ninf, x), axis=0, keepdims=True)
        gs.append(m1 + m2)
    gscore = jnp.concatenate(gs, axis=0)
    rows8 = lax.broadcasted_iota(I32, (N_GROUPS, tm), 0).astype(F32)
    keep = jnp.zeros((N_GROUPS, tm), F32)
    for _ in range(TOPK_GROUPS):
        _, gi = _first_max(gscore, rows8)
        hit = rows8 == gi
        keep = jnp.where(hit, 1.0, keep)
        gscore = jnp.where(hit, ninf, gscore)
    keep_full = jnp.concatenate(
        [jnp.broadcast_to(keep[g:g + 1, :], (per_g, tm)) for g in range(N_GROUPS)], axis=0)
    masked = jnp.where(keep_full > 0.0, choice, NEG)

    idxs, ws = [], []
    onehot = jnp.zeros((N_EXPERTS, tm), F32)
    for _ in range(TOP_K):
        _, ei = _first_max(masked, rows)
        hit = rows == ei
        idxs.append(ei)
        ws.append(jnp.sum(jnp.where(hit, scores, 0.0), axis=0, keepdims=True))
        masked = jnp.where(hit, ninf, masked)
        onehot = jnp.where(hit, 1.0, onehot)
    idx = jnp.concatenate(idxs, axis=0)
    w = jnp.concatenate(ws, axis=0)
    idx_ref[...] = idx.astype(I32)
    w = w / jnp.sum(w, axis=0, keepdims=True) * ROUTED_SCALE
    w_ref[...] = jnp.concatenate([w, jnp.zeros((LANES - TOP_K, tm), F32)], axis=0).T

    tri = _tri_strict_upper(tm, BF16)
    base = _dot(onehot.astype(BF16), tri) + carry_ref[...][:, 0:1]
    pos = [jnp.sum(jnp.where(rows == idxs[k], base, 0.0), axis=0, keepdims=True) for k in range(TOP_K)]
    pos_ref[...] = jnp.concatenate(pos, axis=0).astype(I32)
    carry = carry_ref[...] + jnp.sum(onehot, axis=1, keepdims=True)
    carry_ref[...] = carry
    cnt_ref[...] = carry


def _router(lg, router_bias, tm):
    e, t = lg.shape
    rb = jnp.broadcast_to(router_bias.reshape(e, 1).astype(F32), (e, LANES))
    tok = lambda r: pl.BlockSpec((r, tm), lambda i: (0, i))
    return pl.pallas_call(
        functools.partial(_router_kernel, tm=tm),
        out_shape=(jax.ShapeDtypeStruct((TOP_K, t), I32), jax.ShapeDtypeStruct((t, LANES), F32),
                   jax.ShapeDtypeStruct((TOP_K, t), I32), jax.ShapeDtypeStruct((e, LANES), F32)),
        grid=(t // tm,),
        in_specs=[tok(e), pl.BlockSpec((e, LANES), lambda i: (0, 0))],
        out_specs=(tok(TOP_K), pl.BlockSpec((tm, LANES), lambda i: (i, 0)), tok(TOP_K),
                   pl.BlockSpec((e, LANES), lambda i: (0, 0))),
        scratch_shapes=[pltpu.VMEM((e, LANES), F32)],
        compiler_params=_cparams(("arbitrary",)),
        name="router",
    )(lg, rb)


def _dest_kernel(idx_ref, pos_ref, st_ref, dest_ref, *, tm):
    rows = lax.broadcasted_iota(I32, (N_EXPERTS, tm), 0)
    starts = st_ref[...][:, 0:1]
    idx = idx_ref[...]
    out = []
    for k in range(TOP_K):
        out.append(jnp.sum(jnp.where(rows == idx[k:k + 1, :], starts, 0.0), axis=0, keepdims=True))
    dest_ref[...] = jnp.concatenate(out, axis=0).astype(I32) + pos_ref[...]


def _dest(idx, pos, starts, tm):
    k, t = idx.shape
    st = jnp.broadcast_to(starts.reshape(N_EXPERTS, 1).astype(F32), (N_EXPERTS, LANES))
    tok = pl.BlockSpec((k, tm), lambda i: (0, i))
    return pl.pallas_call(
        functools.partial(_dest_kernel, tm=tm),
        out_shape=jax.ShapeDtypeStruct((k, t), I32),
        grid=(t // tm,),
        in_specs=[tok, tok, pl.BlockSpec((N_EXPERTS, LANES), lambda i: (0, 0))],
        out_specs=tok,
        compiler_params=_cparams(("parallel",)),
        name="dest",
    )(idx, pos, st)


def _experts_kernel(blk_ref, used_ref, first_ref, slot_ref, next_ref, x_ref, wg_hbm, wu_hbm, wd_hbm, y_ref,
                    wg_buf, wu_buf, wd_buf, wg_bf, wu_bf, wd_bf, sems):
    b = pl.program_id(0)
    active = b < used_ref[0]

    def weight_copies(e, slot):
        return [pltpu.make_async_copy(hbm.at[e], buf.at[slot], sems.at[slot, i])
                for i, (hbm, buf) in enumerate(((wg_hbm, wg_buf), (wu_hbm, wu_buf), (wd_hbm, wd_buf)))]

    @pl.when(b == 0)
    def _():
        for cp in weight_copies(blk_ref[0], 0):
            cp.start()

    @pl.when(active & (first_ref[b] == 1))
    def _():
        slot = slot_ref[b]
        for cp in weight_copies(blk_ref[b], slot):
            cp.wait()
        nxt = next_ref[b]

        @pl.when(nxt >= 0)
        def _():
            for cp in weight_copies(nxt, 1 - slot):
                cp.start()

        wg_bf[...] = wg_buf[slot].astype(BF16)
        wu_bf[...] = wu_buf[slot].astype(BF16)
        wd_bf[...] = wd_buf[slot].astype(BF16)

    @pl.when(active)
    def _():
        x = jnp.concatenate([x_ref[j] for j in range(PIECES)], axis=1)
        x = _unpack_pairs(x).astype(BF16)
        a = _dot(x, wg_bf[...])
        u = _dot(x, wu_bf[...])
        h = (a * _sigmoid(a) * u).astype(BF16)
        y = _pack_pairs(_dot(h, wd_bf[...]))
        for j in range(PIECES):
            y_ref[j] = y[:, j * SC_ROW:(j + 1) * SC_ROW]


def _experts(xs, plan, wg, wu, wd, bm):
    _, cap, _ = xs.shape
    n_blocks = cap // bm
    d, f = wg.shape[1], wg.shape[2]
    rows = pl.BlockSpec((PIECES, bm, SC_ROW), lambda b, blk, used, *_: (0, jnp.minimum(b, used[0] - 1), 0))
    hbm = pl.BlockSpec(memory_space=pl.ANY)
    return pl.pallas_call(
        _experts_kernel,
        out_shape=jax.ShapeDtypeStruct(xs.shape, I32),
        grid_spec=pltpu.PrefetchScalarGridSpec(
            num_scalar_prefetch=5,
            grid=(n_blocks,),
            in_specs=[rows, hbm, hbm, hbm],
            out_specs=rows,
            scratch_shapes=[pltpu.VMEM((2, d, f), F32), pltpu.VMEM((2, d, f), F32), pltpu.VMEM((2, f, d), F32),
                            pltpu.VMEM((d, f), BF16), pltpu.VMEM((d, f), BF16), pltpu.VMEM((f, d), BF16),
                            pltpu.SemaphoreType.DMA((2, 3))]),
        compiler_params=_cparams(("arbitrary",)),
        name="experts",
    )(plan["blk_e"], plan["n_used"], plan["first"], plan["slot"], plan["next_e"], xs, wg, wu, wd)


def _final_kernel(x1_ref, u2_ref, yg_ref, w_ref, mod_ref, sg_ref, su_ref, sd_ref, g2_ref, b2_ref, o_ref):
    mod = mod_ref[0]
    w = w_ref[...]
    rows = lambda ref, *lead: jnp.concatenate([ref[(j,) + lead] for j in range(PIECES)], axis=1)
    routed = w[:, 0:1] * _unpack_pairs(rows(yg_ref, 0))
    for k in range(1, TOP_K):
        routed = routed + w[:, k:k + 1] * _unpack_pairs(rows(yg_ref, k))
    u = _unpack_pairs(rows(u2_ref)).astype(BF16)
    a = _dot(u, sg_ref[...])
    b = _dot(u, su_ref[...])
    shared = _dot((a * _sigmoid(a) * b).astype(BF16), sd_ref[...])
    y = DN_ALPHA * x1_ref[...] + mod[5:6, :] * (routed + shared)
    o_ref[...] = _layer_norm(y, g2_ref[...], b2_ref[...])


def _final(x1, u2p, yg, wtok, mod3, sg, su, sd, g2, b2, seq, tm):
    t, d = x1.shape
    per_b = seq // tm
    row = lambda w: pl.BlockSpec((tm, w), lambda i: (i, 0))
    cst = lambda a: pl.BlockSpec(a.shape, lambda i: (0,) * a.ndim)
    return pl.pallas_call(
        _final_kernel,
        out_shape=jax.ShapeDtypeStruct((t, d), F32),
        grid=(t // tm,),
        in_specs=[row(d), pl.BlockSpec((PIECES, tm, SC_ROW), lambda i: (0, i, 0)),
                  pl.BlockSpec((PIECES, TOP_K, tm, SC_ROW), lambda i: (0, 0, i, 0)),
                  row(LANES), pl.BlockSpec((1, 6, d), lambda i: (i // per_b, 0, 0)),
                  cst(sg), cst(su), cst(sd), cst(g2), cst(b2)],
        out_specs=row(d),
        compiler_params=_cparams(("parallel",)),
        name="final",
    )(x1, u2p, yg, wtok, mod3, sg, su, sd, g2, b2)


def _sc_mesh():
    return plsc.VectorSubcoreMesh(core_axis_name="core", subcore_axis_name="subcore")


def _sc_scatter_rows(src, idx, n_out):
    n_copies, n = idx.shape

    @functools.partial(pl.kernel, out_type=jax.ShapeDtypeStruct((n_out, SC_ROW), src.dtype),
                       mesh=_sc_mesh(), scratch_types=[])
    def k(x_hbm, i_hbm, o_hbm):
        def body(x_vmem, i_vmem):
            for c in range(n_copies):
                pltpu.sync_copy(x_vmem, o_hbm.at[i_vmem.at[c]])

        pltpu.emit_pipeline(
            body, grid=(n // SC_WINDOW,),
            in_specs=[pl.BlockSpec((SC_WINDOW, SC_ROW), lambda i: (i, 0)),
                      pl.BlockSpec((n_copies, SC_WINDOW), lambda i: (0, i))],
            out_specs=[],
            core_axis_name=("core", "subcore"),
            dimension_semantics=(pltpu.PARALLEL,),
        )(x_hbm, i_hbm)

    return k(src, idx)


def _sc_gather_rows(src, idx):
    n_idx = idx.shape[0]

    @functools.partial(pl.kernel, out_type=jax.ShapeDtypeStruct((n_idx, SC_ROW), src.dtype),
                       mesh=_sc_mesh(), scratch_types=[])
    def k(x_hbm, i_hbm, o_hbm):
        def body(i_vmem, o_vmem):
            pltpu.sync_copy(x_hbm.at[i_vmem.at[0]], o_vmem)

        pltpu.emit_pipeline(
            body, grid=(n_idx // SC_WINDOW,),
            in_specs=[pl.BlockSpec((1, SC_WINDOW), lambda i: (0, i))],
            out_specs=[pl.BlockSpec((SC_WINDOW, SC_ROW), lambda i: (i, 0))],
            core_axis_name=("core", "subcore"),
            dimension_semantics=(pltpu.PARALLEL,),
        )(i_hbm, o_hbm)

    return k(src, idx.reshape(1, n_idx))


def _moe_plan(counts, n_tok):
    bm = BM_EXPERT
    padded = (counts + bm - 1) // bm * bm
    p_ends = jnp.cumsum(padded)
    starts = p_ends - padded
    n_blocks = n_tok * TOP_K // bm + N_EXPERTS
    blk = jnp.arange(n_blocks, dtype=I32)
    blk_e = jnp.minimum(jnp.sum(p_ends[None, :] <= (blk * bm)[:, None], axis=1), N_EXPERTS - 1).astype(I32)
    n_used = (p_ends[-1] // bm).astype(I32)
    prev = jnp.concatenate([jnp.full((1,), -1, I32), blk_e[:-1]])
    first = ((blk_e != prev) & (blk < n_used)).astype(I32)
    slot = (jnp.cumsum(first) - 1) % 2
    first_pos = jnp.where(first == 1, blk, n_blocks)
    next_first = lax.cummin(jnp.concatenate([first_pos[1:], jnp.full((1,), n_blocks, I32)]), reverse=True)
    next_e = jnp.where(next_first < n_blocks, blk_e[jnp.minimum(next_first, n_blocks - 1)], -1)
    plan = dict(blk_e=blk_e, n_used=n_used.reshape(1), first=first, slot=slot.astype(I32),
                next_e=next_e.astype(I32))
    return starts, plan, n_blocks


def _layer(x, mod, positions, w_in, w_br_a, w_br_b, w_out, cmp_pos_k, cmp_pos_v, cmp_k_w1, cmp_k_w2,
           cmp_v_w1, cmp_v_w2, ln1_g, ln1_b, w_router, router_bias, w_exp_gate, w_exp_up, w_exp_down,
           w_sh_gate, w_sh_up, w_sh_down, ln2_g, ln2_b):
    bsz, seq, d = x.shape
    t = bsz * seq
    assert seq // CMP_STRIDE == LANES and seq % TQ_ATTN == 0
    assert seq % TM_PROJ == 0 and KC_ATTN == TM_PROJ
    x2 = x.reshape(t, d)
    mod3 = mod.reshape(bsz, 6, d)

    w_pack, w_small = _pack_w_in(w_in)
    z = _in_proj(x2, mod3, w_pack, w_small, _rope_tables(positions), seq, TM_PROJ)
    per_b = lambda name: z[name].reshape(bsz, seq, z[name].shape[1])

    o_a = _dsa(per_b("qi"), z["wT"], per_b("ki"), per_b("qa"), per_b("ka"), z["vaT"],
               bsz, seq, TQ_ATTN, KC_ATTN)

    n_rows = seq // CMP_STRIDE
    kc_x, vcT = _compress(z["kcmp"].reshape(bsz, n_rows, CMP_STRIDE * LANES),
                          z["vcmp"].reshape(bsz, n_rows, CMP_STRIDE * LANES),
                          cmp_pos_k, cmp_pos_v, cmp_k_w1, cmp_k_w2, cmp_v_w1, cmp_v_w2)
    o_b = _nsa(per_b("qbraw"), per_b("qbrot"), kc_x, vcT, per_b("ksel"), z["vselT"], per_b("kwin"),
               z["vwinT"], z["gT"], bsz, seq, TQ_ATTN, KC_ATTN)

    wr_hi = w_router.T.astype(BF16)
    wr_lo = (w_router.T - wr_hi.astype(F32)).astype(BF16)
    x1, u2p, logits = _out_proj(
        o_a.reshape(t, -1), o_b.reshape(t, -1), z["ga"], z["gb"], x2, mod3,
        w_br_a.astype(BF16), w_br_b.astype(BF16), w_out.astype(BF16),
        ln1_g.reshape(1, d), ln1_b.reshape(1, d), wr_hi, wr_lo, seq, TM_PROJ)

    idx, wtok, pos, counts = _router(logits, router_bias, TM_ROUTE)
    starts, plan, n_blocks = _moe_plan(counts[:, 0].astype(I32), t)
    dest = _dest(idx, pos, starts, TM_ROUTE)

    cap = n_blocks * BM_EXPERT
    dest_p = dest[None] + (jnp.arange(PIECES, dtype=I32) * cap).reshape(PIECES, 1, 1)
    xs = _sc_scatter_rows(u2p.reshape(PIECES * t, SC_ROW),
                          jnp.swapaxes(dest_p, 0, 1).reshape(TOP_K, PIECES * t), cap * PIECES)
    ys = _experts(xs.reshape(PIECES, cap, SC_ROW), plan, w_exp_gate, w_exp_up, w_exp_down, BM_EXPERT)
    yg = _sc_gather_rows(ys.reshape(cap * PIECES, SC_ROW), dest_p.reshape(-1)
                         ).reshape(PIECES, TOP_K, t, SC_ROW)

    return _final(x1, u2p, yg, wtok, mod3, w_sh_gate.astype(BF16), w_sh_up.astype(BF16),
                  w_sh_down.astype(BF16), ln2_g.reshape(1, d), ln2_b.reshape(1, d), seq, TM_PROJ
                  ).reshape(bsz, seq, d)


def kernel(x, c, positions, w_ada, b_ada, w_in, w_br_a, w_br_b, w_out, cmp_pos_k, cmp_pos_v, cmp_k_w1,
           cmp_k_w2, cmp_v_w1, cmp_v_w2, ln1_g, ln1_b, w_router, router_bias, w_exp_gate, w_exp_up,
           w_exp_down, w_sh_gate, w_sh_up, w_sh_down, ln2_g, ln2_b):
    for l in range(w_ada.shape[0]):
        mod = _mod(c, w_ada[l], b_ada[l])
        x = _layer(x, mod, positions, w_in[l], w_br_a[l], w_br_b[l], w_out[l], cmp_pos_k[l], cmp_pos_v[l],
                   cmp_k_w1[l], cmp_k_w2[l], cmp_v_w1[l], cmp_v_w2[l], ln1_g[l], ln1_b[l], w_router[l],
                   router_bias[l], w_exp_gate[l], w_exp_up[l], w_exp_down[l], w_sh_gate[l], w_sh_up[l],
                   w_sh_down[l], ln2_g[l], ln2_b[l])
    return x
```

```python
import functools
import math

import jax
import jax.numpy as jnp
from jax import lax
from jax.experimental import pallas as pl
from jax.experimental.pallas import tpu as pltpu
from jax.experimental.pallas import tpu_sc as plsc

F32 = jnp.float32
BF16 = jnp.bfloat16
I32 = jnp.int32

D_MODEL = 1024
HEAD_DIM = 64
ROPE_THETA = 500000.0
ROPE_FRACTION = 4
A_HEADS = 8
A_KV_HEADS = 2
IDX_HEADS = 8
IDX_DIM = 32
DSA_TOPK_MAX = 256
B_HEADS = 8
B_KV_HEADS = 2
REP = 4
CMP_BLOCK = 32
CMP_STRIDE = 16
CMP_HIDDEN = 256
SEL_BLOCK = 64
SEL_COUNT = 16
SEL_LOCAL = 2
WINDOW = 512
N_EXPERTS = 256
TOP_K = 8
N_GROUPS = 8
TOPK_GROUPS = 4
ROUTED_SCALE = 2.5
DEPTH = 1
DN_ALPHA = (2 * DEPTH) ** 0.25
LN_EPS = 1e-5
NEG = -1e30
FORCE = 1e9
INT_MIN = -2147483648
N_GATES = 3 * B_HEADS

LANES = 128
SUBLANES = 8
VMEM_LIMIT = 56 * 1024 * 1024
SC_WINDOW = 128
SC_ROW = 256
PIECES = (D_MODEL // 2) // SC_ROW

TM_PROJ = 512
TQ_ATTN = 512
WIN_TQ = 256
KC_ATTN = 512
TM_ROUTE = 512
BM_EXPERT = 512

_IN_SLOTS = (("qa", A_HEADS * HEAD_DIM), ("ka", A_KV_HEADS * HEAD_DIM), ("va", A_KV_HEADS * HEAD_DIM),
             ("qi", IDX_HEADS * IDX_DIM), ("ki", IDX_DIM), ("wi", IDX_HEADS), ("qb", B_HEADS * HEAD_DIM),
             ("kcmp", B_KV_HEADS * HEAD_DIM), ("vcmp", B_KV_HEADS * HEAD_DIM), ("ksel", B_KV_HEADS * HEAD_DIM),
             ("vsel", B_KV_HEADS * HEAD_DIM), ("kwin", B_KV_HEADS * HEAD_DIM), ("vwin", B_KV_HEADS * HEAD_DIM),
             ("gnsa", N_GATES), ("ga", D_MODEL), ("gb", D_MODEL))


def _col_ranges(slots, align):
    out, start = {}, 0
    for name, width in slots:
        out[name] = (start, start + -(-width // align) * align)
        start = out[name][1]
    return out


_IN_COLS = _col_ranges(_IN_SLOTS, 1)
_PACK_COLS = _col_ranges([s for s in _IN_SLOTS if s[0] not in ("wi", "gnsa")], LANES)

NT_DIMS = (((1,), (1,)), ((), ()))


def _cparams(sem):
    return pltpu.CompilerParams(dimension_semantics=sem, vmem_limit_bytes=VMEM_LIMIT)


def _sigmoid(x):
    return 1.0 / (1.0 + jnp.exp(-x))


def _dot(a, b):
    return jnp.dot(a, b, preferred_element_type=F32)


def _dot_nt(a, b):
    return lax.dot_general(a, b, NT_DIMS, preferred_element_type=F32)


def _sort_key(x):
    x = jnp.where(x == 0.0, 0.0, x)
    bits = pltpu.bitcast(x, I32)
    return jnp.where(bits < 0, bits ^ 0x7FFFFFFF, bits)


def _kth_largest_key(count_ge, shape, k):
    kf = float(k)
    t0 = jnp.where(count_ge(jnp.zeros(shape, I32)) >= kf, 0, INT_MIN).astype(I32)

    def body(it, t):
        cand = t + jnp.left_shift(jnp.int32(1), 30 - it)
        return jnp.where(count_ge(cand) >= kf, cand, t)

    return lax.fori_loop(0, 31, body, t0)


def _tri_strict_lower(n, dtype):
    r = lax.broadcasted_iota(I32, (n, n), 0)
    c = lax.broadcasted_iota(I32, (n, n), 1)
    return jnp.where(c < r, 1.0, 0.0).astype(dtype)


def _tri_strict_upper(n, dtype):
    r = lax.broadcasted_iota(I32, (n, n), 0)
    c = lax.broadcasted_iota(I32, (n, n), 1)
    return jnp.where(r < c, 1.0, 0.0).astype(dtype)


def _mod_kernel(c_ref, w_ref, b_ref, o_ref):
    c = c_ref[...]
    cond = (c * _sigmoid(c)).astype(BF16)
    o_ref[...] = _dot(cond, w_ref[...].astype(BF16)) + b_ref[...]


def _mod(c, w_ada, b_ada):
    bsz, d = c.shape
    n = w_ada.shape[1]
    tn = 1024
    return pl.pallas_call(
        _mod_kernel,
        out_shape=jax.ShapeDtypeStruct((bsz, n), F32),
        grid=(n // tn,),
        in_specs=[pl.BlockSpec((bsz, d), lambda j: (0, 0)),
                  pl.BlockSpec((d, tn), lambda j: (0, j)),
                  pl.BlockSpec((1, tn), lambda j: (0, j))],
        out_specs=pl.BlockSpec((bsz, tn), lambda j: (0, j)),
        compiler_params=_cparams(("parallel",)),
        name="mod",
    )(c, w_ada, b_ada.reshape(1, n))


def _rope(z, c_tab, s_tab, period, half):
    w = z.shape[1]
    reps = w // LANES
    c = jnp.concatenate([c_tab] * reps, axis=1) if reps > 1 else c_tab
    s = jnp.concatenate([s_tab] * reps, axis=1) if reps > 1 else s_tab
    lane = lax.broadcasted_iota(I32, z.shape, 1)
    first = (lane & (period - 1)) < half
    partner = jnp.where(first, pltpu.roll(z, w - half, axis=1), pltpu.roll(z, half, axis=1))
    return z * c + partner * s


def _in_proj_kernel(x_ref, mod_ref, w_ref, wsm_ref, c64_ref, s64_ref, c32_ref, s32_ref,
                    qa_ref, ka_ref, vaT_ref, qi_ref, ki_ref, wT_ref, qbraw_ref, qbrot_ref,
                    kcmp_ref, vcmp_ref, ksel_ref, vselT_ref, kwin_ref, vwinT_ref, gT_ref, ga_ref, gb_ref):
    mod = mod_ref[0]
    u = (x_ref[...] * (1.0 + mod[1:2, :]) + mod[0:1, :]).astype(BF16)
    tm = u.shape[0]
    c64, s64, c32, s32 = c64_ref[...], s64_ref[...], c32_ref[...], s32_ref[...]
    scale = HEAD_DIM ** -0.5
    lane = lax.broadcasted_iota(I32, (tm, LANES), 1)
    low = lane < HEAD_DIM

    def proj(name):
        a, b = _PACK_COLS[name]
        return _dot(u, w_ref[:, a:b])

    rope64 = lambda z: _rope(z, c64, s64, HEAD_DIM, HEAD_DIM // ROPE_FRACTION // 2)
    rope32 = lambda z: _rope(z, c32, s32, IDX_DIM, IDX_DIM // ROPE_FRACTION // 2)

    def head_slots64(z):
        out = []
        for h in range(A_HEADS):
            pair = z[:, (h // 2) * LANES:(h // 2 + 1) * LANES]
            g = h // REP
            src = pair if h % 2 == g else pltpu.roll(pair, HEAD_DIM, axis=1)
            out.append(jnp.where(low, src, 0.0) if g == 0 else jnp.where(low, 0.0, src))
        return jnp.concatenate(out, axis=1).astype(BF16)

    def head_slots32(z):
        per = LANES // IDX_DIM
        out = []
        for h in range(IDX_HEADS):
            col = z[:, (h // per) * LANES:(h // per + 1) * LANES]
            shift = IDX_DIM * (h % per)
            src = col if shift == 0 else pltpu.roll(col, LANES - shift, axis=1)
            out.append(jnp.where(lane < IDX_DIM, src, 0.0))
        return jnp.concatenate(out, axis=1).astype(BF16)

    def store_vt(ref, z, chunk):
        zt = z.T
        ones = jnp.ones((HEAD_DIM, chunk), F32)
        for g in range(A_KV_HEADS):
            for j in range(tm // chunk):
                blk = zt[g * HEAD_DIM:(g + 1) * HEAD_DIM, j * chunk:(j + 1) * chunk]
                ref[g, j] = jnp.concatenate([blk, ones], axis=0).astype(BF16)

    qa_ref[...] = head_slots64(rope64(proj("qa")) * scale)
    ka_ref[...] = rope64(proj("ka")).astype(BF16)
    store_vt(vaT_ref, proj("va"), tm)
    qi_ref[...] = head_slots32(rope32(proj("qi")))
    ki_ref[...] = rope32(proj("ki")).astype(BF16)
    qb = proj("qb")
    qbraw_ref[...] = head_slots64(qb * scale)
    qbrot_ref[...] = head_slots64(rope64(qb) * scale)
    kcmp_ref[...] = proj("kcmp")
    vcmp_ref[...] = proj("vcmp")
    ksel_ref[...] = rope64(proj("ksel")).astype(BF16)
    store_vt(vselT_ref, proj("vsel"), tm)
    kwin_ref[...] = rope64(proj("kwin")).astype(BF16)
    store_vt(vwinT_ref, proj("vwin"), LANES)
    ga_ref[...] = _sigmoid(proj("ga")).astype(BF16)
    gb_ref[...] = _sigmoid(proj("gb")).astype(BF16)
    small = _dot_nt(wsm_ref[...], u)
    wT_ref[...] = small[0:IDX_HEADS, :]
    gT_ref[...] = _sigmoid(small[IDX_HEADS:IDX_HEADS + N_GATES, :])


def _pack_w_in(w_in):
    d = w_in.shape[0]
    col = lambda name: w_in[:, _IN_COLS[name][0]:_IN_COLS[name][1]]
    parts = []
    for name, (a, b) in _PACK_COLS.items():
        c = col(name)
        parts.append(jnp.pad(c, ((0, 0), (0, b - a - c.shape[1]))))
    w_small = jnp.concatenate([col("wi"), col("gnsa")], axis=1).T
    return jnp.concatenate(parts, axis=1).astype(BF16), w_small.astype(BF16)


def _rope_tables(positions):
    pos = positions.astype(F32).reshape(-1, 1)

    def tab(dim):
        rot = dim // ROPE_FRACTION
        half = rot // 2
        inv = ROPE_THETA ** (-(jnp.arange(half, dtype=F32) * 2.0) / rot)
        ang = pos * inv
        cos, sin = jnp.cos(ang), jnp.sin(ang)
        ones = jnp.ones((pos.shape[0], dim - rot), F32)
        c = jnp.concatenate([cos, cos, ones], axis=1)
        s = jnp.concatenate([-sin, sin, 0.0 * ones], axis=1)
        return jnp.tile(c, (1, LANES // dim)), jnp.tile(s, (1, LANES // dim))

    return tab(HEAD_DIM) + tab(IDX_DIM)


def _in_proj(x2, mod3, w_pack, w_small, tabs, seq, tm):
    t, d = x2.shape
    n = w_pack.shape[1]
    per_b = seq // tm
    g = A_KV_HEADS
    row = lambda w: pl.BlockSpec((tm, w), lambda i: (i, 0))
    tok = lambda r: pl.BlockSpec((r, tm), lambda i: (0, i))
    vt_chunk = pl.BlockSpec((g, 1, LANES, tm), lambda i: (0, i, 0, 0))
    vt_lane = pl.BlockSpec((g, tm // LANES, LANES, LANES), lambda i: (0, i, 0, 0))
    sds = jax.ShapeDtypeStruct
    vt_chunk_shape = sds((g, t // tm, LANES, tm), BF16)
    outs = (("qa", sds((t, 1024), BF16), row(1024)), ("ka", sds((t, LANES), BF16), row(LANES)),
            ("vaT", vt_chunk_shape, vt_chunk), ("qi", sds((t, 1024), BF16), row(1024)),
            ("ki", sds((t, LANES), BF16), row(LANES)), ("wT", sds((IDX_HEADS, t), F32), tok(IDX_HEADS)),
            ("qbraw", sds((t, 1024), BF16), row(1024)), ("qbrot", sds((t, 1024), BF16), row(1024)),
            ("kcmp", sds((t, LANES), F32), row(LANES)), ("vcmp", sds((t, LANES), F32), row(LANES)),
            ("ksel", sds((t, LANES), BF16), row(LANES)), ("vselT", vt_chunk_shape, vt_chunk),
            ("kwin", sds((t, LANES), BF16), row(LANES)),
            ("vwinT", sds((g, t // LANES, LANES, LANES), BF16), vt_lane),
            ("gT", sds((N_GATES, t), F32), tok(N_GATES)),
            ("ga", sds((t, d), BF16), row(d)), ("gb", sds((t, d), BF16), row(d)))
    res = pl.pallas_call(
        _in_proj_kernel,
        out_shape=tuple(o[1] for o in outs),
        grid=(t // tm,),
        in_specs=[row(d),
                  pl.BlockSpec((1, 6, d), lambda i: (i // per_b, 0, 0)),
                  pl.BlockSpec((d, n), lambda i: (0, 0)),
                  pl.BlockSpec(w_small.shape, lambda i: (0, 0)),
                  row(LANES), row(LANES), row(LANES), row(LANES)],
        out_specs=tuple(o[2] for o in outs),
        compiler_params=_cparams(("parallel",)),
        name="in_proj",
    )(x2, mod3, w_pack, w_small, *tabs)
    return {o[0]: r for o, r in zip(outs, res)}


def _fold_rows(x, op):
    n = x.shape[0]
    while n % (2 * SUBLANES) == 0:
        n //= 2
        x = op(x[:n], x[n:])
    slabs = [x[i * SUBLANES:(i + 1) * SUBLANES] for i in range(n // SUBLANES)]
    while len(slabs) > 1:
        nxt = [op(slabs[i], slabs[i + 1]) for i in range(0, len(slabs) - 1, 2)]
        slabs = nxt + ([slabs[-1]] if len(slabs) % 2 else [])
    return slabs[0]


def _col_max(x):
    return jnp.max(_fold_rows(x, jnp.maximum), axis=0, keepdims=True)


def _col_sum(x):
    return jnp.sum(_fold_rows(x, jnp.add), axis=0, keepdims=True)


PACKED_ROWS = 16


def _fold_rows_packed(x):
    n = x.shape[0]
    assert n % PACKED_ROWS == 0 and n // PACKED_ROWS <= 256
    while n > PACKED_ROWS:
        n //= 2
        x = x[:n] + x[n:]
    return x


def _stack_heads(q_ref, heads):
    return jnp.concatenate([q_ref[0, :, h * LANES:(h + 1) * LANES] for h in heads], axis=0)


def _flash_step(k, q_stack, v_t, bias4, m, acc):
    s = _dot_nt(k, q_stack) + bias4
    m_new = jnp.maximum(m, _col_max(s))
    e = jnp.exp(s - m_new).astype(BF16)
    return m_new, acc * jnp.exp(m - m_new) + _dot(v_t, e)


HEADS_PER_CHAIN = REP


def _head_chains():
    return [(h // REP, tuple(range(h, h + HEADS_PER_CHAIN))) for h in range(0, A_HEADS, HEADS_PER_CHAIN)]


def _flash_loop(q_ref, k_ref, vT_ref, bias_of, n_chunks, tq, kc):
    chains = _head_chains()
    q_stacks = [_stack_heads(q_ref, heads) for _, heads in chains]
    width = HEADS_PER_CHAIN * tq

    def body(c, carry):
        k = k_ref[0, pl.ds(pl.multiple_of(c * kc, kc), kc), :]
        out = []
        for (g, _), q_stack, (m, acc) in zip(chains, q_stacks, carry):
            bias = jnp.concatenate([bias_of(g, c)] * HEADS_PER_CHAIN, axis=1)
            out.append(_flash_step(k, q_stack, vT_ref[g, c], bias, m, acc))
        return tuple(out)

    init = tuple((jnp.full((1, width), NEG, F32), jnp.zeros((LANES, width), F32)) for _ in chains)
    res = lax.fori_loop(0, n_chunks, body, init)
    return [(heads, acc) for (_, heads), (_, acc) in zip(chains, res)]


def _normalise(acc):
    return acc * (1.0 / jnp.maximum(acc[HEAD_DIM:HEAD_DIM + 1, :], 1e-30))


def _store_heads(o_ref, o_t, heads, tq):
    o = o_t.T
    for r, h in enumerate(heads):
        o_ref[0, :, h * HEAD_DIM:(h + 1) * HEAD_DIM] = o[r * tq:(r + 1) * tq, 0:HEAD_DIM].astype(o_ref.dtype)


def _select_mask(keys, thr, need, offset, tri):
    gt = jnp.where(keys > thr, 1.0, 0.0)
    eq = jnp.where(keys == thr, 1.0, 0.0)
    prefix = _dot(tri, eq.astype(BF16)) + offset
    return gt + jnp.where(prefix < need, eq, 0.0), offset + _col_sum(eq)


def _dsa_kernel(qi_ref, wT_ref, ki_ref, qa_ref, ka_ref, vT_ref, o_ref, keys_ref, bias_ref, planes_ref,
                *, tq, kc, n_keep):
    t0 = pl.program_id(1) * tq
    n_chunks = (t0 + tq + kc - 1) // kc
    wT = wT_ref[...] * (IDX_HEADS ** -0.5 * IDX_DIM ** -0.5)
    qi_stack = _stack_heads(qi_ref, range(IDX_HEADS))
    keypos = lax.broadcasted_iota(I32, (kc, tq), 0)
    qpos = t0 + lax.broadcasted_iota(I32, (kc, tq), 1)

    def score_chunk(c, carry):
        k0 = pl.multiple_of(c * kc, kc)
        sc = _dot_nt(ki_ref[0, pl.ds(k0, kc), :], qi_stack)
        score = jnp.zeros((kc, tq), F32)
        for h in range(IDX_HEADS):
            score = score + wT[h:h + 1, :] * jnp.maximum(sc[:, h * tq:(h + 1) * tq], 0.0)
        key = jnp.where(k0 + keypos <= qpos, _sort_key(score), INT_MIN)
        keys_ref[c] = key
        flipped = key ^ INT_MIN
        for lvl in range(4):
            byte = lax.shift_right_logical(flipped, 8 * lvl) & 0xFF
            planes_ref[lvl, c] = byte.astype(F32).astype(BF16)
        return carry

    lax.fori_loop(0, n_chunks, score_chunk, 0)

    one = jnp.ones((kc, tq), BF16)
    zero = jnp.zeros((kc, tq), BF16)
    dead = jnp.full((kc, tq), -1.0, BF16)

    def count_where(lvl, pick, also=None):
        def body(c, acc):
            p = planes_ref[lvl, c]
            if also is not None:
                also(c, p)
            return acc + _fold_rows_packed(jnp.where(pick(p), one, zero)).astype(F32)
        acc = lax.fori_loop(0, n_chunks, body, jnp.zeros((PACKED_ROWS, tq), F32))
        return jnp.sum(acc, axis=0, keepdims=True)

    above = jnp.zeros((1, tq), F32)
    thr_u = jnp.zeros((1, tq), I32)
    for lvl in (3, 2, 1, 0):
        def bit_step(it, t, lvl=lvl, above=above):
            cand = t + jnp.left_shift(jnp.int32(1), 7 - it).astype(F32)
            cnt = above + count_where(lvl, lambda p: p >= cand.astype(BF16))
            return jnp.where(cnt >= float(n_keep), cand, t)

        t = lax.fori_loop(0, 8, bit_step, jnp.zeros((1, tq), F32))
        tb = t.astype(BF16)

        def narrow(c, p, lvl=lvl, tb=tb):
            planes_ref[lvl - 1, c] = jnp.where(p == tb, planes_ref[lvl - 1, c], dead)

        above = above + count_where(lvl, lambda p: p > tb, narrow if lvl > 0 else None)
        thr_u = thr_u | jnp.left_shift(t.astype(I32), 8 * lvl)
    thr = thr_u ^ INT_MIN
    need = float(n_keep) - above
    tri = _tri_strict_lower(LANES, BF16)
    sub = lax.broadcasted_iota(I32, (LANES, tq), 0)
    qsub = t0 + lax.broadcasted_iota(I32, (LANES, tq), 1)

    def bias_chunk(c, offset):
        for j in range(kc // LANES):
            rows = slice(j * LANES, (j + 1) * LANES)
            sel, offset = _select_mask(keys_ref[c, rows, :], thr, need, offset, tri)
            causal = c * kc + j * LANES + sub <= qsub
            bias_ref[c, rows, :] = jnp.where(causal, (sel - 1.0) * 1e30, NEG)
        return offset

    lax.fori_loop(0, n_chunks, bias_chunk, jnp.zeros((1, tq), F32))

    accs = _flash_loop(qa_ref, ka_ref, vT_ref, lambda g, c: bias_ref[c], n_chunks, tq, kc)
    for heads, acc in accs:
        _store_heads(o_ref, _normalise(acc), heads, tq)


def _dsa(qi, wT, ki, qa, ka, vaT, bsz, seq, tq, kc):
    n_keep = min(DSA_TOPK_MAX, seq // 4)
    nq = seq // tq
    qblk = lambda w: pl.BlockSpec((1, tq, w), lambda b, i: (b, i, 0))
    full = lambda w: pl.BlockSpec((1, seq, w), lambda b, i: (b, 0, 0))
    return pl.pallas_call(
        functools.partial(_dsa_kernel, tq=tq, kc=kc, n_keep=n_keep),
        out_shape=jax.ShapeDtypeStruct((bsz, seq, A_HEADS * HEAD_DIM), BF16),
        grid=(bsz, nq),
        in_specs=[qblk(1024), pl.BlockSpec((IDX_HEADS, tq), lambda b, i: (0, b * nq + i)), full(LANES),
                  qblk(1024), full(LANES),
                  pl.BlockSpec((A_KV_HEADS, seq // kc, LANES, kc), lambda b, i: (0, b, 0, 0))],
        out_specs=qblk(512),
        scratch_shapes=[pltpu.VMEM((seq // kc, kc, tq), I32), pltpu.VMEM((seq // kc, kc, tq), F32),
                        pltpu.VMEM((4, seq // kc, kc, tq), BF16)],
        compiler_params=_cparams(("parallel", "parallel")),
        name="dsa",
    )(qi, wT, ki, qa, ka, vaT)


def _gelu_tanh(x):
    return 0.5 * x * (1.0 + jnp.tanh(math.sqrt(2.0 / math.pi) * (x + 0.044715 * (x * x * x))))


def _compress_kernel(k_ref, v_ref, pk_ref, pv_ref, w1k_ref, w2k_ref, w1v_ref, w2v_ref,
                     kc_ref, vcT_ref, *, n_rows):
    half = CMP_BLOCK // 2

    def one(x_ref, p_ref, w1_ref, w2_ref):
        outs = []
        for g in range(B_KV_HEADS):
            lo = jnp.zeros((n_rows, CMP_HIDDEN), F32)
            hi = jnp.zeros((n_rows, CMP_HIDDEN), F32)
            for l in range(half):
                xl = x_ref[0, :, l * LANES + g * HEAD_DIM:l * LANES + (g + 1) * HEAD_DIM]
                a = (xl + p_ref[l:l + 1, :]).astype(BF16)
                b = (xl + p_ref[half + l:half + l + 1, :]).astype(BF16)
                lo = lo + _dot(a, w1_ref[l * HEAD_DIM:(l + 1) * HEAD_DIM, :].astype(BF16))
                hi = hi + _dot(b, w1_ref[(half + l) * HEAD_DIM:(half + l + 1) * HEAD_DIM, :].astype(BF16))
            hid = lo + pltpu.roll(hi, n_rows - 1, axis=0)
            outs.append(_dot(_gelu_tanh(hid).astype(BF16), w2_ref[...].astype(BF16)))
        return outs

    k0, k1 = one(k_ref, pk_ref, w1k_ref, w2k_ref)
    kc_ref[0] = jnp.concatenate([k0, k1], axis=1).astype(kc_ref.dtype)
    for g, v in enumerate(one(v_ref, pv_ref, w1v_ref, w2v_ref)):
        vcT_ref[0, g] = jnp.concatenate([v, jnp.zeros_like(v)], axis=1).T.astype(vcT_ref.dtype)


def _compress(kcmp, vcmp, pos_k, pos_v, w1k, w2k, w1v, w2v):
    bsz, n_rows, width = kcmp.shape
    xspec = pl.BlockSpec((1, n_rows, width), lambda b: (b, 0, 0))
    cst = lambda a: pl.BlockSpec(a.shape, lambda b: (0,) * a.ndim)
    return pl.pallas_call(
        functools.partial(_compress_kernel, n_rows=n_rows),
        out_shape=(jax.ShapeDtypeStruct((bsz, n_rows, LANES), BF16),
                   jax.ShapeDtypeStruct((bsz, B_KV_HEADS, LANES, n_rows), BF16)),
        grid=(bsz,),
        in_specs=[xspec, xspec, cst(pos_k), cst(pos_v), cst(w1k), cst(w2k), cst(w1v), cst(w2v)],
        out_specs=(pl.BlockSpec((1, n_rows, LANES), lambda b: (b, 0, 0)),
                   pl.BlockSpec((1, B_KV_HEADS, LANES, n_rows), lambda b: (b, 0, 0, 0))),
        compiler_params=_cparams(("parallel",)),
        name="compress",
    )(kcmp, vcmp, pos_k, pos_v, w1k, w2k, w1v, w2v)


def _split3(x):
    a = x.astype(BF16)
    r = x - a.astype(F32)
    b = r.astype(BF16)
    c = (r - b.astype(F32)).astype(BF16)
    return a, b, c


def _nsa_kernel(qraw_ref, qrot_ref, kc_ref, vcT_ref, ksel_ref, vselT_ref, kwin_ref, vwinT_ref,
                gT_ref, exp_ref, o_ref, bias_ref, *, tq, kc, seq, n_c):
    t0 = pl.program_id(1) * tq
    n_chunks = (t0 + tq + kc - 1) // kc
    n_s = seq // SEL_BLOCK
    n_pick = min(SEL_COUNT, n_s)
    wt = min(tq, WIN_TQ)
    span = WINDOW + wt
    gT = gT_ref[...]

    ridx = lax.broadcasted_iota(I32, (LANES, tq), 0)
    tl = t0 + lax.broadcasted_iota(I32, (LANES, tq), 1)
    valid_c = jnp.where((ridx * CMP_STRIDE + CMP_BLOCK - 1 <= tl) & (ridx < n_c), 1.0, 0.0)
    valid4 = jnp.concatenate([valid_c] * REP, axis=1) > 0.0
    js = lax.broadcasted_iota(I32, (LANES, LANES), 0) * SEL_BLOCK
    cs = lax.broadcasted_iota(I32, (LANES, LANES), 1) * CMP_STRIDE
    overlap_t = jnp.where((cs <= js + SEL_BLOCK - 1) & (cs + CMP_BLOCK - 1 >= js), 1.0, 0.0).astype(BF16)
    cur = tl // SEL_BLOCK
    forced = (ridx == 0) | ((cur - ridx >= 0) & (cur - ridx < SEL_LOCAL))
    blk_causal = ridx * SEL_BLOCK <= tl
    tri = _tri_strict_lower(LANES, BF16)
    keypos = lax.broadcasted_iota(I32, (kc, tq), 0)
    qpos = t0 + lax.broadcasted_iota(I32, (kc, tq), 1)

    o_cmp_g = []
    for g in range(B_KV_HEADS):
        heads = range(g * REP, (g + 1) * REP)
        s = jnp.where(valid4, _dot_nt(kc_ref[0], _stack_heads(qraw_ref, heads)), NEG)
        e = jnp.where(valid4, jnp.exp(s - _col_max(s)), 0.0)
        p = e * (1.0 / jnp.maximum(_col_sum(e), 1e-30))
        o_cmp_g.append(_dot(vcT_ref[0, g], p.astype(BF16)))
        p_sum = p[:, 0:tq]
        for r in range(1, REP):
            p_sum = p_sum + p[:, r * tq:(r + 1) * tq]
        pa, pb, pc = _split3(p_sum)
        imp = _dot(overlap_t, pa) + _dot(overlap_t, pb) + _dot(overlap_t, pc)
        imp = jnp.where(forced, FORCE, jnp.where(blk_causal, imp, NEG))
        keys = _sort_key(imp[0:n_s, :])
        count_ge = lambda cand, keys=keys: _col_sum(jnp.where(keys >= cand, 1.0, 0.0))
        thr = _kth_largest_key(count_ge, (1, tq), n_pick)
        need = float(n_pick) - _col_sum(jnp.where(keys > thr, 1.0, 0.0))
        blk_sel, _ = _select_mask(keys, thr, need, jnp.zeros((1, tq), F32), tri[0:n_s, 0:n_s])
        blk_sel = jnp.concatenate([blk_sel, jnp.zeros((LANES - n_s, tq), F32)], axis=0).astype(BF16)

        def bias_chunk(c, carry, blk_sel=blk_sel, g=g):
            tok_sel = _dot(exp_ref[c], blk_sel)
            bias_ref[g, c] = jnp.where(c * kc + keypos <= qpos, (tok_sel - 1.0) * 1e30, NEG)
            return carry

        lax.fori_loop(0, n_chunks, bias_chunk, 0)

    sel = _flash_loop(qrot_ref, ksel_ref, vselT_ref, lambda g, c: bias_ref[g, c], n_chunks, tq, kc)

    for (g, heads), (_, acc_sel) in zip(_head_chains(), sel):
        o_win_sub = []
        for sub in range(tq // wt):
            t_sub = t0 + sub * wt
            w0 = pl.multiple_of(jnp.clip(t_sub - WINDOW, 0, seq - span), wt)
            wdiff = ((t_sub + lax.broadcasted_iota(I32, (span, wt), 1))
                     - (w0 + lax.broadcasted_iota(I32, (span, wt), 0)))
            wbias = jnp.where((wdiff >= 0) & (wdiff < WINDOW), 0.0, NEG)
            q_sub = jnp.concatenate([qrot_ref[0, sub * wt:(sub + 1) * wt, h * LANES:(h + 1) * LANES]
                                     for h in heads], axis=0)
            sw = (_dot_nt(kwin_ref[0, pl.ds(w0, span), :], q_sub)
                  + jnp.concatenate([wbias] * HEADS_PER_CHAIN, axis=1))
            ew = jnp.exp(sw - _col_max(sw)).astype(BF16)
            wblk = w0 // LANES
            acc_win = _dot(vwinT_ref[g, wblk], ew[0:LANES, :])
            for j in range(1, span // LANES):
                acc_win = acc_win + _dot(vwinT_ref[g, wblk + j], ew[j * LANES:(j + 1) * LANES, :])
            o_win_sub.append(_normalise(acc_win))

        o_sel = _normalise(acc_sel)
        cols = []
        for r, h in enumerate(heads):
            col = slice(r * tq, (r + 1) * tq)
            cmp_col = slice((h % REP) * tq, (h % REP + 1) * tq)
            o_win = jnp.concatenate([o[:, r * wt:(r + 1) * wt] for o in o_win_sub], axis=1)
            cols.append(gT[3 * h:3 * h + 1, :] * o_cmp_g[g][:, cmp_col]
                        + gT[3 * h + 1:3 * h + 2, :] * o_sel[:, col]
                        + gT[3 * h + 2:3 * h + 3, :] * o_win)
        _store_heads(o_ref, jnp.concatenate(cols, axis=1), heads, tq)


def _nsa(qraw, qrot, kc_x, vcT, ksel, vselT, kwin, vwinT, gT, bsz, seq, tq, kc):
    n_c = (seq - CMP_BLOCK) // CMP_STRIDE + 1
    nq = seq // tq
    key_blk = (jnp.arange(seq, dtype=I32) // SEL_BLOCK).reshape(seq // kc, kc, 1)
    expand = (key_blk == jnp.arange(LANES, dtype=I32).reshape(1, 1, LANES)).astype(BF16)
    qblk = lambda w: pl.BlockSpec((1, tq, w), lambda b, i: (b, i, 0))
    full = lambda w: pl.BlockSpec((1, seq, w), lambda b, i: (b, 0, 0))
    g = B_KV_HEADS
    return pl.pallas_call(
        functools.partial(_nsa_kernel, tq=tq, kc=kc, seq=seq, n_c=n_c),
        out_shape=jax.ShapeDtypeStruct((bsz, seq, B_HEADS * HEAD_DIM), BF16),
        grid=(bsz, nq),
        in_specs=[qblk(1024), qblk(1024),
                  pl.BlockSpec((1,) + kc_x.shape[1:], lambda b, i: (b, 0, 0)),
                  pl.BlockSpec((1,) + vcT.shape[1:], lambda b, i: (b, 0, 0, 0)),
                  full(LANES), pl.BlockSpec((g, seq // kc, LANES, kc), lambda b, i: (0, b, 0, 0)),
                  full(LANES), pl.BlockSpec((g, seq // LANES, LANES, LANES), lambda b, i: (0, b, 0, 0)),
                  pl.BlockSpec((N_GATES, tq), lambda b, i: (0, b * nq + i)),
                  pl.BlockSpec(expand.shape, lambda b, i: (0, 0, 0))],
        out_specs=qblk(512),
        scratch_shapes=[pltpu.VMEM((g, seq // kc, kc, tq), F32)],
        compiler_params=_cparams(("parallel", "parallel")),
        name="nsa",
    )(qraw, qrot, kc_x, vcT, ksel, vselT, kwin, vwinT, gT, expand)


def _pack_pairs(x):
    n = x.shape[1] // 2
    lo = pltpu.bitcast(x[:, :n].astype(BF16).astype(F32), I32)
    hi = pltpu.bitcast(x[:, n:].astype(BF16).astype(F32), I32)
    return lax.shift_right_logical(lo, 16) | (hi & jnp.int32(-65536))


def _unpack_pairs(p):
    lo = pltpu.bitcast(lax.shift_left(p, 16), F32)
    hi = pltpu.bitcast(p & jnp.int32(-65536), F32)
    return jnp.concatenate([lo, hi], axis=1)


def _layer_norm(y, g, b):
    mu = jnp.mean(y, axis=1, keepdims=True)
    yc = y - mu
    var = jnp.mean(yc * yc, axis=1, keepdims=True)
    return yc * lax.rsqrt(var + LN_EPS) * g + b


def _out_proj_kernel(oa_ref, ob_ref, ga_ref, gb_ref, x_ref, mod_ref, wa_ref, wb_ref, wo_ref,
                     g1_ref, b1_ref, wrh_ref, wrl_ref, x1_ref, u2_ref, lg_ref):
    mod = mod_ref[0]
    merged = (ga_ref[...].astype(F32) * _dot(oa_ref[...], wa_ref[...])
              + gb_ref[...].astype(F32) * _dot(ob_ref[...], wb_ref[...]))
    mix = _dot(merged.astype(BF16), wo_ref[...])
    x1 = _layer_norm(DN_ALPHA * x_ref[...] + mod[2:3, :] * mix, g1_ref[...], b1_ref[...])
    x1_ref[...] = x1
    u2 = x1 * (1.0 + mod[4:5, :]) + mod[3:4, :]
    packed = _pack_pairs(u2)
    for j in range(PIECES):
        u2_ref[j] = packed[:, j * SC_ROW:(j + 1) * SC_ROW]
    uh = u2.astype(BF16)
    ul = (u2 - uh.astype(F32)).astype(BF16)
    lg_ref[...] = _dot_nt(wrh_ref[...], uh) + _dot_nt(wrh_ref[...], ul) + _dot_nt(wrl_ref[...], uh)


def _out_proj(oa, ob, ga, gb, x2, mod3, wa, wb, wo, g1, b1, wrh, wrl, seq, tm):
    t, d = x2.shape
    per_b = seq // tm
    row = lambda w: pl.BlockSpec((tm, w), lambda i: (i, 0))
    cst = lambda a: pl.BlockSpec(a.shape, lambda i: (0,) * a.ndim)
    return pl.pallas_call(
        _out_proj_kernel,
        out_shape=(jax.ShapeDtypeStruct((t, d), F32), jax.ShapeDtypeStruct((PIECES, t, SC_ROW), I32),
                   jax.ShapeDtypeStruct((N_EXPERTS, t), F32)),
        grid=(t // tm,),
        in_specs=[row(512), row(512), row(d), row(d), row(d),
                  pl.BlockSpec((1, 6, d), lambda i: (i // per_b, 0, 0)),
                  cst(wa), cst(wb), cst(wo), cst(g1), cst(b1), cst(wrh), cst(wrl)],
        out_specs=(row(d), pl.BlockSpec((PIECES, tm, SC_ROW), lambda i: (0, i, 0)),
                   pl.BlockSpec((N_EXPERTS, tm), lambda i: (0, i))),
        compiler_params=_cparams(("parallel",)),
        name="out_proj",
    )(oa, ob, ga, gb, x2, mod3, wa, wb, wo, g1, b1, wrh, wrl)


def _first_max(x, rows):
    m = jnp.max(x, axis=0, keepdims=True)
    idx = jnp.min(jnp.where(x == m, rows, 1e9), axis=0, keepdims=True)
    return m, idx


def _router_kernel(lg_ref, rb_ref, idx_ref, w_ref, pos_ref, cnt_ref, carry_ref, *, tm):
    @pl.when(pl.program_id(0) == 0)
    def _():
        carry_ref[...] = jnp.zeros_like(carry_ref)

    per_g = N_EXPERTS // N_GROUPS
    scores = _sigmoid(lg_ref[...])
    choice = scores + rb_ref[...][:, 0:1]
    rows = lax.broadcasted_iota(I32, (N_EXPERTS, tm), 0).astype(F32)
    rows_g = lax.broadcasted_iota(I32, (per_g, tm), 0).astype(F32)
    ninf = -jnp.inf

    gs = []
    for g in range(N_GROUPS):
        x = choice[g * per_g:(g + 1) * per_g, :]
        m1, i1 = _first_max(x, rows_g)
        m2 = jnp.max(jnp.where(rows_g == i1, ninf, x), axis=0, keepdims=True)
        gs.append(m1 + m2)
    gscore = jnp.concatenate(gs, axis=0)
    rows8 = lax.broadcasted_iota(I32, (N_GROUPS, tm), 0).astype(F32)
    keep = jnp.zeros((N_GROUPS, tm), F32)
    for _ in range(TOPK_GROUPS):
        _, gi = _first_max(gscore, rows8)
        hit = rows8 == gi
        keep = jnp.where(hit, 1.0, keep)
        gscore = jnp.where(hit, ninf, gscore)
    keep_full = jnp.concatenate(
        [jnp.broadcast_to(keep[g:g + 1, :], (per_g, tm)) for g in range(N_GROUPS)], axis=0)
    masked = jnp.where(keep_full > 0.0, choice, NEG)

    idxs, ws = [], []
    onehot = jnp.zeros((N_EXPERTS, tm), F32)
    for _ in range(TOP_K):
        _, ei = _first_max(masked, rows)
        hit = rows == ei
        idxs.append(ei)
        ws.append(jnp.sum(jnp.where(hit, scores, 0.0), axis=0, keepdims=True))
        masked = jnp.where(hit, ninf, masked)
        onehot = jnp.where(hit, 1.0, onehot)
    idx = jnp.concatenate(idxs, axis=0)
    w = jnp.concatenate(ws, axis=0)
    idx_ref[...] = idx.astype(I32)
    w = w / jnp.sum(w, axis=0, keepdims=True) * ROUTED_SCALE
    w_ref[...] = jnp.concatenate([w, jnp.zeros((LANES - TOP_K, tm), F32)], axis=0).T

    tri = _tri_strict_upper(tm, BF16)
    base = _dot(onehot.astype(BF16), tri) + carry_ref[...][:, 0:1]
    pos = [jnp.sum(jnp.where(rows == idxs[k], base, 0.0), axis=0, keepdims=True) for k in range(TOP_K)]
    pos_ref[...] = jnp.concatenate(pos, axis=0).astype(I32)
    carry = carry_ref[...] + jnp.sum(onehot, axis=1, keepdims=True)
    carry_ref[...] = carry
    cnt_ref[...] = carry


def _router(lg, router_bias, tm):
    e, t = lg.shape
    rb = jnp.broadcast_to(router_bias.reshape(e, 1).astype(F32), (e, LANES))
    tok = lambda r: pl.BlockSpec((r, tm), lambda i: (0, i))
    return pl.pallas_call(
        functools.partial(_router_kernel, tm=tm),
        out_shape=(jax.ShapeDtypeStruct((TOP_K, t), I32), jax.ShapeDtypeStruct((t, LANES), F32),
                   jax.ShapeDtypeStruct((TOP_K, t), I32), jax.ShapeDtypeStruct((e, LANES), F32)),
        grid=(t // tm,),
        in_specs=[tok(e), pl.BlockSpec((e, LANES), lambda i: (0, 0))],
        out_specs=(tok(TOP_K), pl.BlockSpec((tm, LANES), lambda i: (i, 0)), tok(TOP_K),
                   pl.BlockSpec((e, LANES), lambda i: (0, 0))),
        scratch_shapes=[pltpu.VMEM((e, LANES), F32)],
        compiler_params=_cparams(("arbitrary",)),
        name="router",
    )(lg, rb)


def _dest_kernel(idx_ref, pos_ref, st_ref, dest_ref, *, tm):
    rows = lax.broadcasted_iota(I32, (N_EXPERTS, tm), 0)
    starts = st_ref[...][:, 0:1]
    idx = idx_ref[...]
    out = []
    for k in range(TOP_K):
        out.append(jnp.sum(jnp.where(rows == idx[k:k + 1, :], starts, 0.0), axis=0, keepdims=True))
    dest_ref[...] = jnp.concatenate(out, axis=0).astype(I32) + pos_ref[...]


def _dest(idx, pos, starts, tm):
    k, t = idx.shape
    st = jnp.broadcast_to(starts.reshape(N_EXPERTS, 1).astype(F32), (N_EXPERTS, LANES))
    tok = pl.BlockSpec((k, tm), lambda i: (0, i))
    return pl.pallas_call(
        functools.partial(_dest_kernel, tm=tm),
        out_shape=jax.ShapeDtypeStruct((k, t), I32),
        grid=(t // tm,),
        in_specs=[tok, tok, pl.BlockSpec((N_EXPERTS, LANES), lambda i: (0, 0))],
        out_specs=tok,
        compiler_params=_cparams(("parallel",)),
        name="dest",
    )(idx, pos, st)


def _experts_kernel(blk_ref, used_ref, first_ref, slot_ref, next_ref, x_ref, wg_hbm, wu_hbm, wd_hbm, y_ref,
                    wg_buf, wu_buf, wd_buf, wg_bf, wu_bf, wd_bf, sems):
    b = pl.program_id(0)
    active = b < used_ref[0]

    def weight_copies(e, slot):
        return [pltpu.make_async_copy(hbm.at[e], buf.at[slot], sems.at[slot, i])
                for i, (hbm, buf) in enumerate(((wg_hbm, wg_buf), (wu_hbm, wu_buf), (wd_hbm, wd_buf)))]

    @pl.when(b == 0)
    def _():
        for cp in weight_copies(blk_ref[0], 0):
            cp.start()

    @pl.when(active & (first_ref[b] == 1))
    def _():
        slot = slot_ref[b]
        for cp in weight_copies(blk_ref[b], slot):
            cp.wait()
        nxt = next_ref[b]

        @pl.when(nxt >= 0)
        def _():
            for cp in weight_copies(nxt, 1 - slot):
                cp.start()

        wg_bf[...] = wg_buf[slot].astype(BF16)
        wu_bf[...] = wu_buf[slot].astype(BF16)
        wd_bf[...] = wd_buf[slot].astype(BF16)

    @pl.when(active)
    def _():
        x = jnp.concatenate([x_ref[j] for j in range(PIECES)], axis=1)
        x = _unpack_pairs(x).astype(BF16)
        a = _dot(x, wg_bf[...])
        u = _dot(x, wu_bf[...])
        h = (a * _sigmoid(a) * u).astype(BF16)
        y = _pack_pairs(_dot(h, wd_bf[...]))
        for j in range(PIECES):
            y_ref[j] = y[:, j * SC_ROW:(j + 1) * SC_ROW]


def _experts(xs, plan, wg, wu, wd, bm):
    _, cap, _ = xs.shape
    n_blocks = cap // bm
    d, f = wg.shape[1], wg.shape[2]
    rows = pl.BlockSpec((PIECES, bm, SC_ROW), lambda b, blk, used, *_: (0, jnp.minimum(b, used[0] - 1), 0))
    hbm = pl.BlockSpec(memory_space=pl.ANY)
    return pl.pallas_call(
        _experts_kernel,
        out_shape=jax.ShapeDtypeStruct(xs.shape, I32),
        grid_spec=pltpu.PrefetchScalarGridSpec(
            num_scalar_prefetch=5,
            grid=(n_blocks,),
            in_specs=[rows, hbm, hbm, hbm],
            out_specs=rows,
            scratch_shapes=[pltpu.VMEM((2, d, f), F32), pltpu.VMEM((2, d, f), F32), pltpu.VMEM((2, f, d), F32),
                            pltpu.VMEM((d, f), BF16), pltpu.VMEM((d, f), BF16), pltpu.VMEM((f, d), BF16),
                            pltpu.SemaphoreType.DMA((2, 3))]),
        compiler_params=_cparams(("arbitrary",)),
        name="experts",
    )(plan["blk_e"], plan["n_used"], plan["first"], plan["slot"], plan["next_e"], xs, wg, wu, wd)


def _final_kernel(x1_ref, u2_ref, yg_ref, w_ref, mod_ref, sg_ref, su_ref, sd_ref, g2_ref, b2_ref, o_ref):
    mod = mod_ref[0]
    w = w_ref[...]
    rows = lambda ref, *lead: jnp.concatenate([ref[(j,) + lead] for j in range(PIECES)], axis=1)
    routed = w[:, 0:1] * _unpack_pairs(rows(yg_ref, 0))
    for k in range(1, TOP_K):
        routed = routed + w[:, k:k + 1] * _unpack_pairs(rows(yg_ref, k))
    u = _unpack_pairs(rows(u2_ref)).astype(BF16)
    a = _dot(u, sg_ref[...])
    b = _dot(u, su_ref[...])
    shared = _dot((a * _sigmoid(a) * b).astype(BF16), sd_ref[...])
    y = DN_ALPHA * x1_ref[...] + mod[5:6, :] * (routed + shared)
    o_ref[...] = _layer_norm(y, g2_ref[...], b2_ref[...])


def _final(x1, u2p, yg, wtok, mod3, sg, su, sd, g2, b2, seq, tm):
    t, d = x1.shape
    per_b = seq // tm
    row = lambda w: pl.BlockSpec((tm, w), lambda i: (i, 0))
    cst = lambda a: pl.BlockSpec(a.shape, lambda i: (0,) * a.ndim)
    return pl.pallas_call(
        _final_kernel,
        out_shape=jax.ShapeDtypeStruct((t, d), F32),
        grid=(t // tm,),
        in_specs=[row(d), pl.BlockSpec((PIECES, tm, SC_ROW), lambda i: (0, i, 0)),
                  pl.BlockSpec((PIECES, TOP_K, tm, SC_ROW), lambda i: (0, 0, i, 0)),
                  row(LANES), pl.BlockSpec((1, 6, d), lambda i: (i // per_b, 0, 0)),
                  cst(sg), cst(su), cst(sd), cst(g2), cst(b2)],
        out_specs=row(d),
        compiler_params=_cparams(("parallel",)),
        name="final",
    )(x1, u2p, yg, wtok, mod3, sg, su, sd, g2, b2)


def _sc_mesh():
    return plsc.VectorSubcoreMesh(core_axis_name="core", subcore_axis_name="subcore")


def _sc_scatter_rows(src, idx, n_out):
    n_copies, n = idx.shape

    @functools.partial(pl.kernel, out_type=jax.ShapeDtypeStruct((n_out, SC_ROW), src.dtype),
                       mesh=_sc_mesh(), scratch_types=[])
    def k(x_hbm, i_hbm, o_hbm):
        def body(x_vmem, i_vmem):
            for c in range(n_copies):
                pltpu.sync_copy(x_vmem, o_hbm.at[i_vmem.at[c]])

        pltpu.emit_pipeline(
            body, grid=(n // SC_WINDOW,),
            in_specs=[pl.BlockSpec((SC_WINDOW, SC_ROW), lambda i: (i, 0)),
                      pl.BlockSpec((n_copies, SC_WINDOW), lambda i: (0, i))],
            out_specs=[],
            core_axis_name=("core", "subcore"),
            dimension_semantics=(pltpu.PARALLEL,),
        )(x_hbm, i_hbm)

    return k(src, idx)


def _sc_gather_rows(src, idx):
    n_idx = idx.shape[0]

    @functools.partial(pl.kernel, out_type=jax.ShapeDtypeStruct((n_idx, SC_ROW), src.dtype),
                       mesh=_sc_mesh(), scratch_types=[])
    def k(x_hbm, i_hbm, o_hbm):
        def body(i_vmem, o_vmem):
            pltpu.sync_copy(x_hbm.at[i_vmem.at[0]], o_vmem)

        pltpu.emit_pipeline(
            body, grid=(n_idx // SC_WINDOW,),
            in_specs=[pl.BlockSpec((1, SC_WINDOW), lambda i: (0, i))],
            out_specs=[pl.BlockSpec((SC_WINDOW, SC_ROW), lambda i: (i, 0))],
            core_axis_name=("core", "subcore"),
            dimension_semantics=(pltpu.PARALLEL,),
        )(i_hbm, o_hbm)

    return k(src, idx.reshape(1, n_idx))


def _moe_plan(counts, n_tok):
    bm = BM_EXPERT
    padded = (counts + bm - 1) // bm * bm
    p_ends = jnp.cumsum(padded)
    starts = p_ends - padded
    n_blocks = n_tok * TOP_K // bm + N_EXPERTS
    blk = jnp.arange(n_blocks, dtype=I32)
    blk_e = jnp.minimum(jnp.sum(p_ends[None, :] <= (blk * bm)[:, None], axis=1), N_EXPERTS - 1).astype(I32)
    n_used = (p_ends[-1] // bm).astype(I32)
    prev = jnp.concatenate([jnp.full((1,), -1, I32), blk_e[:-1]])
    first = ((blk_e != prev) & (blk < n_used)).astype(I32)
    slot = (jnp.cumsum(first) - 1) % 2
    first_pos = jnp.where(first == 1, blk, n_blocks)
    next_first = lax.cummin(jnp.concatenate([first_pos[1:], jnp.full((1,), n_blocks, I32)]), reverse=True)
    next_e = jnp.where(next_first < n_blocks, blk_e[jnp.minimum(next_first, n_blocks - 1)], -1)
    plan = dict(blk_e=blk_e, n_used=n_used.reshape(1), first=first, slot=slot.astype(I32),
                next_e=next_e.astype(I32))
    return starts, plan, n_blocks


def _layer(x, mod, positions, w_in, w_br_a, w_br_b, w_out, cmp_pos_k, cmp_pos_v, cmp_k_w1, cmp_k_w2,
           cmp_v_w1, cmp_v_w2, ln1_g, ln1_b, w_router, router_bias, w_exp_gate, w_exp_up, w_exp_down,
           w_sh_gate, w_sh_up, w_sh_down, ln2_g, ln2_b):
    bsz, seq, d = x.shape
    t = bsz * seq
    assert seq // CMP_STRIDE == LANES and seq % TQ_ATTN == 0
    assert seq % TM_PROJ == 0 and KC_ATTN == TM_PROJ
    x2 = x.reshape(t, d)
    mod3 = mod.reshape(bsz, 6, d)

    w_pack, w_small = _pack_w_in(w_in)
    z = _in_proj(x2, mod3, w_pack, w_small, _rope_tables(positions), seq, TM_PROJ)
    per_b = lambda name: z[name].reshape(bsz, seq, z[name].shape[1])

    o_a = _dsa(per_b("qi"), z["wT"], per_b("ki"), per_b("qa"), per_b("ka"), z["vaT"],
               bsz, seq, TQ_ATTN, KC_ATTN)

    n_rows = seq // CMP_STRIDE
    kc_x, vcT = _compress(z["kcmp"].reshape(bsz, n_rows, CMP_STRIDE * LANES),
                          z["vcmp"].reshape(bsz, n_rows, CMP_STRIDE * LANES),
                          cmp_pos_k, cmp_pos_v, cmp_k_w1, cmp_k_w2, cmp_v_w1, cmp_v_w2)
    o_b = _nsa(per_b("qbraw"), per_b("qbrot"), kc_x, vcT, per_b("ksel"), z["vselT"], per_b("kwin"),
               z["vwinT"], z["gT"], bsz, seq, TQ_ATTN, KC_ATTN)

    wr_hi = w_router.T.astype(BF16)
    wr_lo = (w_router.T - wr_hi.astype(F32)).astype(BF16)
    x1, u2p, logits = _out_proj(
        o_a.reshape(t, -1), o_b.reshape(t, -1), z["ga"], z["gb"], x2, mod3,
        w_br_a.astype(BF16), w_br_b.astype(BF16), w_out.astype(BF16),
        ln1_g.reshape(1, d), ln1_b.reshape(1, d), wr_hi, wr_lo, seq, TM_PROJ)

    idx, wtok, pos, counts = _router(logits, router_bias, TM_ROUTE)
    starts, plan, n_blocks = _moe_plan(counts[:, 0].astype(I32), t)
    dest = _dest(idx, pos, starts, TM_ROUTE)

    cap = n_blocks * BM_EXPERT
    dest_p = dest[None] + (jnp.arange(PIECES, dtype=I32) * cap).reshape(PIECES, 1, 1)
    xs = _sc_scatter_rows(u2p.reshape(PIECES * t, SC_ROW),
                          jnp.swapaxes(dest_p, 0, 1).reshape(TOP_K, PIECES * t), cap * PIECES)
    ys = _experts(xs.reshape(PIECES, cap, SC_ROW), plan, w_exp_gate, w_exp_up, w_exp_down, BM_EXPERT)
    yg = _sc_gather_rows(ys.reshape(cap * PIECES, SC_ROW), dest_p.reshape(-1)
                         ).reshape(PIECES, TOP_K, t, SC_ROW)

    return _final(x1, u2p, yg, wtok, mod3, w_sh_gate.astype(BF16), w_sh_up.astype(BF16),
                  w_sh_down.astype(BF16), ln2_g.reshape(1, d), ln2_b.reshape(1, d), seq, TM_PROJ
                  ).reshape(bsz, seq, d)


def kernel(x, c, positions, w_ada, b_ada, w_in, w_br_a, w_br_b, w_out, cmp_pos_k, cmp_pos_v, cmp_k_w1,
           cmp_k_w2, cmp_v_w1, cmp_v_w2, ln1_g, ln1_b, w_router, router_bias, w_exp_gate, w_exp_up,
           w_exp_down, w_sh_gate, w_sh_up, w_sh_down, ln2_g, ln2_b):
    for l in range(w_ada.shape[0]):
        mod = _mod(c, w_ada[l], b_ada[l])
        x = _layer(x, mod, positions, w_in[l], w_br_a[l], w_br_b[l], w_out[l], cmp_pos_k[l], cmp_pos_v[l],
                   cmp_k_w1[l], cmp_k_w2[l], cmp_v_w1[l], cmp_v_w2[l], ln1_g[l], ln1_b[l], w_router[l],
                   router_bias[l], w_exp_gate[l], w_exp_up[l], w_exp_down[l], w_sh_gate[l], w_sh_up[l],
                   w_sh_down[l], ln2_g[l], ln2_b[l])
    return x
```

```python
import functools
import math

import jax
import jax.numpy as jnp
from jax import lax
from jax.experimental import pallas as pl
from jax.experimental.pallas import tpu as pltpu
from jax.experimental.pallas import tpu_sc as plsc

F32 = jnp.float32
BF16 = jnp.bfloat16
I32 = jnp.int32

D_MODEL = 1024
HEAD_DIM = 64
ROPE_THETA = 500000.0
ROPE_FRACTION = 4
A_HEADS = 8
A_KV_HEADS = 2
IDX_HEADS = 8
IDX_DIM = 32
DSA_TOPK_MAX = 256
B_HEADS = 8
B_KV_HEADS = 2
REP = 4
CMP_BLOCK = 32
CMP_STRIDE = 16
CMP_HIDDEN = 256
SEL_BLOCK = 64
SEL_COUNT = 16
SEL_LOCAL = 2
WINDOW = 512
N_EXPERTS = 256
TOP_K = 8
N_GROUPS = 8
TOPK_GROUPS = 4
ROUTED_SCALE = 2.5
DEPTH = 1
DN_ALPHA = (2 * DEPTH) ** 0.25
LN_EPS = 1e-5
NEG = -1e30
FORCE = 1e9
INT_MIN = -2147483648
N_GATES = 3 * B_HEADS

LANES = 128
SUBLANES = 8
VMEM_LIMIT = 56 * 1024 * 1024
SC_WINDOW = 128
SC_ROW = 256
PIECES = (D_MODEL // 2) // SC_ROW

TM_PROJ = 512
TQ_ATTN = 512
WIN_TQ = 256
KC_ATTN = 512
TM_ROUTE = 512
BM_EXPERT = 512

_IN_SLOTS = (("qa", A_HEADS * HEAD_DIM), ("ka", A_KV_HEADS * HEAD_DIM), ("va", A_KV_HEADS * HEAD_DIM),
             ("qi", IDX_HEADS * IDX_DIM), ("ki", IDX_DIM), ("wi", IDX_HEADS), ("qb", B_HEADS * HEAD_DIM),
             ("kcmp", B_KV_HEADS * HEAD_DIM), ("vcmp", B_KV_HEADS * HEAD_DIM), ("ksel", B_KV_HEADS * HEAD_DIM),
             ("vsel", B_KV_HEADS * HEAD_DIM), ("kwin", B_KV_HEADS * HEAD_DIM), ("vwin", B_KV_HEADS * HEAD_DIM),
             ("gnsa", N_GATES), ("ga", D_MODEL), ("gb", D_MODEL))


def _col_ranges(slots, align):
    out, start = {}, 0
    for name, width in slots:
        out[name] = (start, start + -(-width // align) * align)
        start = out[name][1]
    return out


_IN_COLS = _col_ranges(_IN_SLOTS, 1)
_PACK_COLS = _col_ranges([s for s in _IN_SLOTS if s[0] not in ("wi", "gnsa")], LANES)

NT_DIMS = (((1,), (1,)), ((), ()))


def _cparams(sem):
    return pltpu.CompilerParams(dimension_semantics=sem, vmem_limit_bytes=VMEM_LIMIT)


def _sigmoid(x):
    return 1.0 / (1.0 + jnp.exp(-x))


def _dot(a, b):
    return jnp.dot(a, b, preferred_element_type=F32)


def _dot_nt(a, b):
    return lax.dot_general(a, b, NT_DIMS, preferred_element_type=F32)


def _sort_key(x):
    x = jnp.where(x == 0.0, 0.0, x)
    bits = pltpu.bitcast(x, I32)
    return jnp.where(bits < 0, bits ^ 0x7FFFFFFF, bits)


def _kth_largest_key(count_ge, shape, k):
    kf = float(k)
    t0 = jnp.where(count_ge(jnp.zeros(shape, I32)) >= kf, 0, INT_MIN).astype(I32)

    def body(it, t):
        cand = t + jnp.left_shift(jnp.int32(1), 30 - it)
        return jnp.where(count_ge(cand) >= kf, cand, t)

    return lax.fori_loop(0, 31, body, t0)


def _tri_strict_lower(n, dtype):
    r = lax.broadcasted_iota(I32, (n, n), 0)
    c = lax.broadcasted_iota(I32, (n, n), 1)
    return jnp.where(c < r, 1.0, 0.0).astype(dtype)


def _tri_strict_upper(n, dtype):
    r = lax.broadcasted_iota(I32, (n, n), 0)
    c = lax.broadcasted_iota(I32, (n, n), 1)
    return jnp.where(r < c, 1.0, 0.0).astype(dtype)


def _mod_kernel(c_ref, w_ref, b_ref, o_ref):
    c = c_ref[...]
    cond = (c * _sigmoid(c)).astype(BF16)
    o_ref[...] = _dot(cond, w_ref[...].astype(BF16)) + b_ref[...]


def _mod(c, w_ada, b_ada):
    bsz, d = c.shape
    n = w_ada.shape[1]
    tn = 1024
    return pl.pallas_call(
        _mod_kernel,
        out_shape=jax.ShapeDtypeStruct((bsz, n), F32),
        grid=(n // tn,),
        in_specs=[pl.BlockSpec((bsz, d), lambda j: (0, 0)),
                  pl.BlockSpec((d, tn), lambda j: (0, j)),
                  pl.BlockSpec((1, tn), lambda j: (0, j))],
        out_specs=pl.BlockSpec((bsz, tn), lambda j: (0, j)),
        compiler_params=_cparams(("parallel",)),
        name="mod",
    )(c, w_ada, b_ada.reshape(1, n))


def _rope(z, c_tab, s_tab, period, half):
    w = z.shape[1]
    reps = w // LANES
    c = jnp.concatenate([c_tab] * reps, axis=1) if reps > 1 else c_tab
    s = jnp.concatenate([s_tab] * reps, axis=1) if reps > 1 else s_tab
    lane = lax.broadcasted_iota(I32, z.shape, 1)
    first = (lane & (period - 1)) < half
    partner = jnp.where(first, pltpu.roll(z, w - half, axis=1), pltpu.roll(z, half, axis=1))
    return z * c + partner * s


def _in_proj_kernel(x_ref, mod_ref, w_ref, wsm_ref, c64_ref, s64_ref, c32_ref, s32_ref,
                    qa_ref, ka_ref, vaT_ref, qi_ref, ki_ref, wT_ref, qbraw_ref, qbrot_ref,
                    kcmp_ref, vcmp_ref, ksel_ref, vselT_ref, kwin_ref, vwinT_ref, gT_ref, ga_ref, gb_ref):
    mod = mod_ref[0]
    u = (x_ref[...] * (1.0 + mod[1:2, :]) + mod[0:1, :]).astype(BF16)
    tm = u.shape[0]
    c64, s64, c32, s32 = c64_ref[...], s64_ref[...], c32_ref[...], s32_ref[...]
    scale = HEAD_DIM ** -0.5
    lane = lax.broadcasted_iota(I32, (tm, LANES), 1)
    low = lane < HEAD_DIM

    def proj(name):
        a, b = _PACK_COLS[name]
        return _dot(u, w_ref[:, a:b])

    rope64 = lambda z: _rope(z, c64, s64, HEAD_DIM, HEAD_DIM // ROPE_FRACTION // 2)
    rope32 = lambda z: _rope(z, c32, s32, IDX_DIM, IDX_DIM // ROPE_FRACTION // 2)

    def head_slots64(z):
        out = []
        for h in range(A_HEADS):
            pair = z[:, (h // 2) * LANES:(h // 2 + 1) * LANES]
            g = h // REP
            src = pair if h % 2 == g else pltpu.roll(pair, HEAD_DIM, axis=1)
            out.append(jnp.where(low, src, 0.0) if g == 0 else jnp.where(low, 0.0, src))
        return jnp.concatenate(out, axis=1).astype(BF16)

    def head_slots32(z):
        per = LANES // IDX_DIM
        out = []
        for h in range(IDX_HEADS):
            col = z[:, (h // per) * LANES:(h // per + 1) * LANES]
            shift = IDX_DIM * (h % per)
            src = col if shift == 0 else pltpu.roll(col, LANES - shift, axis=1)
            out.append(jnp.where(lane < IDX_DIM, src, 0.0))
        return jnp.concatenate(out, axis=1).astype(BF16)

    def store_vt(ref, z, chunk):
        zt = z.T
        ones = jnp.ones((HEAD_DIM, chunk), F32)
        for g in range(A_KV_HEADS):
            for j in range(tm // chunk):
                blk = zt[g * HEAD_DIM:(g + 1) * HEAD_DIM, j * chunk:(j + 1) * chunk]
                ref[g, j] = jnp.concatenate([blk, ones], axis=0).astype(BF16)

    qa_ref[...] = head_slots64(rope64(proj("qa")) * scale)
    ka_ref[...] = rope64(proj("ka")).astype(BF16)
    store_vt(vaT_ref, proj("va"), tm)
    qi_ref[...] = head_slots32(rope32(proj("qi")))
    ki_ref[...] = rope32(proj("ki")).astype(BF16)
    qb = proj("qb")
    qbraw_ref[...] = head_slots64(qb * scale)
    qbrot_ref[...] = head_slots64(rope64(qb) * scale)
    kcmp_ref[...] = proj("kcmp")
    vcmp_ref[...] = proj("vcmp")
    ksel_ref[...] = rope64(proj("ksel")).astype(BF16)
    store_vt(vselT_ref, proj("vsel"), tm)
    kwin_ref[...] = rope64(proj("kwin")).astype(BF16)
    store_vt(vwinT_ref, proj("vwin"), LANES)
    ga_ref[...] = _sigmoid(proj("ga")).astype(BF16)
    gb_ref[...] = _sigmoid(proj("gb")).astype(BF16)
    small = _dot_nt(wsm_ref[...], u)
    wT_ref[...] = small[0:IDX_HEADS, :]
    gT_ref[...] = _sigmoid(small[IDX_HEADS:IDX_HEADS + N_GATES, :])


def _pack_w_in(w_in):
    d = w_in.shape[0]
    col = lambda name: w_in[:, _IN_COLS[name][0]:_IN_COLS[name][1]]
    parts = []
    for name, (a, b) in _PACK_COLS.items():
        c = col(name)
        parts.append(jnp.pad(c, ((0, 0), (0, b - a - c.shape[1]))))
    w_small = jnp.concatenate([col("wi"), col("gnsa")], axis=1).T
    return jnp.concatenate(parts, axis=1).astype(BF16), w_small.astype(BF16)


def _rope_tables(positions):
    pos = positions.astype(F32).reshape(-1, 1)

    def tab(dim):
        rot = dim // ROPE_FRACTION
        half = rot // 2
        inv = ROPE_THETA ** (-(jnp.arange(half, dtype=F32) * 2.0) / rot)
        ang = pos * inv
        cos, sin = jnp.cos(ang), jnp.sin(ang)
        ones = jnp.ones((pos.shape[0], dim - rot), F32)
        c = jnp.concatenate([cos, cos, ones], axis=1)
        s = jnp.concatenate([-sin, sin, 0.0 * ones], axis=1)
        return jnp.tile(c, (1, LANES // dim)), jnp.tile(s, (1, LANES // dim))

    return tab(HEAD_DIM) + tab(IDX_DIM)


def _in_proj(x2, mod3, w_pack, w_small, tabs, seq, tm):
    t, d = x2.shape
    n = w_pack.shape[1]
    per_b = seq // tm
    g = A_KV_HEADS
    row = lambda w: pl.BlockSpec((tm, w), lambda i: (i, 0))
    tok = lambda r: pl.BlockSpec((r, tm), lambda i: (0, i))
    vt_chunk = pl.BlockSpec((g, 1, LANES, tm), lambda i: (0, i, 0, 0))
    vt_lane = pl.BlockSpec((g, tm // LANES, LANES, LANES), lambda i: (0, i, 0, 0))
    sds = jax.ShapeDtypeStruct
    vt_chunk_shape = sds((g, t // tm, LANES, tm), BF16)
    outs = (("qa", sds((t, 1024), BF16), row(1024)), ("ka", sds((t, LANES), BF16), row(LANES)),
            ("vaT", vt_chunk_shape, vt_chunk), ("qi", sds((t, 1024), BF16), row(1024)),
            ("ki", sds((t, LANES), BF16), row(LANES)), ("wT", sds((IDX_HEADS, t), F32), tok(IDX_HEADS)),
            ("qbraw", sds((t, 1024), BF16), row(1024)), ("qbrot", sds((t, 1024), BF16), row(1024)),
            ("kcmp", sds((t, LANES), F32), row(LANES)), ("vcmp", sds((t, LANES), F32), row(LANES)),
            ("ksel", sds((t, LANES), BF16), row(LANES)), ("vselT", vt_chunk_shape, vt_chunk),
            ("kwin", sds((t, LANES), BF16), row(LANES)),
            ("vwinT", sds((g, t // LANES, LANES, LANES), BF16), vt_lane),
            ("gT", sds((N_GATES, t), F32), tok(N_GATES)),
            ("ga", sds((t, d), BF16), row(d)), ("gb", sds((t, d), BF16), row(d)))
    res = pl.pallas_call(
        _in_proj_kernel,
        out_shape=tuple(o[1] for o in outs),
        grid=(t // tm,),
        in_specs=[row(d),
                  pl.BlockSpec((1, 6, d), lambda i: (i // per_b, 0, 0)),
                  pl.BlockSpec((d, n), lambda i: (0, 0)),
                  pl.BlockSpec(w_small.shape, lambda i: (0, 0)),
                  row(LANES), row(LANES), row(LANES), row(LANES)],
        out_specs=tuple(o[2] for o in outs),
        compiler_params=_cparams(("parallel",)),
        name="in_proj",
    )(x2, mod3, w_pack, w_small, *tabs)
    return {o[0]: r for o, r in zip(outs, res)}


def _fold_rows(x, op):
    n = x.shape[0]
    while n % (2 * SUBLANES) == 0:
        n //= 2
        x = op(x[:n], x[n:])
    slabs = [x[i * SUBLANES:(i + 1) * SUBLANES] for i in range(n // SUBLANES)]
    while len(slabs) > 1:
        nxt = [op(slabs[i], slabs[i + 1]) for i in range(0, len(slabs) - 1, 2)]
        slabs = nxt + ([slabs[-1]] if len(slabs) % 2 else [])
    return slabs[0]


def _col_max(x):
    return jnp.max(_fold_rows(x, jnp.maximum), axis=0, keepdims=True)


def _col_sum(x):
    return jnp.sum(_fold_rows(x, jnp.add), axis=0, keepdims=True)


PACKED_ROWS = 16


def _fold_rows_packed(x):
    n = x.shape[0]
    assert n % PACKED_ROWS == 0 and n // PACKED_ROWS <= 256
    while n > PACKED_ROWS:
        n //= 2
        x = x[:n] + x[n:]
    return x


def _stack_heads(q_ref, heads):
    return jnp.concatenate([q_ref[0, :, h * LANES:(h + 1) * LANES] for h in heads], axis=0)


def _flash_step(k, q_stack, v_t, bias4, m, acc):
    s = _dot_nt(k, q_stack) + bias4
    m_new = jnp.maximum(m, _col_max(s))
    e = jnp.exp(s - m_new).astype(BF16)
    return m_new, acc * jnp.exp(m - m_new) + _dot(v_t, e)


HEADS_PER_CHAIN = REP


def _head_chains():
    return [(h // REP, tuple(range(h, h + HEADS_PER_CHAIN))) for h in range(0, A_HEADS, HEADS_PER_CHAIN)]


def _flash_loop(q_ref, k_ref, vT_ref, bias_of, n_chunks, tq, kc):
    chains = _head_chains()
    q_stacks = [_stack_heads(q_ref, heads) for _, heads in chains]
    width = HEADS_PER_CHAIN * tq

    def body(c, carry):
        k = k_ref[0, pl.ds(pl.multiple_of(c * kc, kc), kc), :]
        out = []
        for (g, _), q_stack, (m, acc) in zip(chains, q_stacks, carry):
            bias = jnp.concatenate([bias_of(g, c)] * HEADS_PER_CHAIN, axis=1)
            out.append(_flash_step(k, q_stack, vT_ref[g, c], bias, m, acc))
        return tuple(out)

    init = tuple((jnp.full((1, width), NEG, F32), jnp.zeros((LANES, width), F32)) for _ in chains)
    res = lax.fori_loop(0, n_chunks, body, init)
    return [(heads, acc) for (_, heads), (_, acc) in zip(chains, res)]


def _normalise(acc):
    return acc * (1.0 / jnp.maximum(acc[HEAD_DIM:HEAD_DIM + 1, :], 1e-30))


def _store_heads(o_ref, o_t, heads, tq):
    o = o_t.T
    for r, h in enumerate(heads):
        o_ref[0, :, h * HEAD_DIM:(h + 1) * HEAD_DIM] = o[r * tq:(r + 1) * tq, 0:HEAD_DIM].astype(o_ref.dtype)


def _select_mask(keys, thr, need, offset, tri):
    gt = jnp.where(keys > thr, 1.0, 0.0)
    eq = jnp.where(keys == thr, 1.0, 0.0)
    prefix = _dot(tri, eq.astype(BF16)) + offset
    return gt + jnp.where(prefix < need, eq, 0.0), offset + _col_sum(eq)


def _dsa_kernel(qi_ref, wT_ref, ki_ref, qa_ref, ka_ref, vT_ref, o_ref, keys_ref, bias_ref, planes_ref,
                *, tq, kc, n_keep):
    t0 = pl.program_id(1) * tq
    n_chunks = (t0 + tq + kc - 1) // kc
    wT = wT_ref[...] * (IDX_HEADS ** -0.5 * IDX_DIM ** -0.5)
    qi_stack = _stack_heads(qi_ref, range(IDX_HEADS))
    keypos = lax.broadcasted_iota(I32, (kc, tq), 0)
    qpos = t0 + lax.broadcasted_iota(I32, (kc, tq), 1)

    def score_chunk(c, carry):
        k0 = pl.multiple_of(c * kc, kc)
        sc = _dot_nt(ki_ref[0, pl.ds(k0, kc), :], qi_stack)
        score = jnp.zeros((kc, tq), F32)
        for h in range(IDX_HEADS):
            score = score + wT[h:h + 1, :] * jnp.maximum(sc[:, h * tq:(h + 1) * tq], 0.0)
        key = jnp.where(k0 + keypos <= qpos, _sort_key(score), INT_MIN)
        keys_ref[c] = key
        flipped = key ^ INT_MIN
        for lvl in range(4):
            byte = lax.shift_right_logical(flipped, 8 * lvl) & 0xFF
            planes_ref[lvl, c] = byte.astype(F32).astype(BF16)
        return carry

    lax.fori_loop(0, n_chunks, score_chunk, 0)

    one = jnp.ones((kc, tq), BF16)
    zero = jnp.zeros((kc, tq), BF16)
    dead = jnp.full((kc, tq), -1.0, BF16)

    def count_where(lvl, pick, also=None):
        def body(c, acc):
            p = planes_ref[lvl, c]
            if also is not None:
                also(c, p)
            return acc + _fold_rows_packed(jnp.where(pick(p), one, zero)).astype(F32)
        acc = lax.fori_loop(0, n_chunks, body, jnp.zeros((PACKED_ROWS, tq), F32))
        return jnp.sum(acc, axis=0, keepdims=True)

    above = jnp.zeros((1, tq), F32)
    thr_u = jnp.zeros((1, tq), I32)
    for lvl in (3, 2, 1, 0):
        def bit_step(it, t, lvl=lvl, above=above):
            cand = t + jnp.left_shift(jnp.int32(1), 7 - it).astype(F32)
            cnt = above + count_where(lvl, lambda p: p >= cand.astype(BF16))
            return jnp.where(cnt >= float(n_keep), cand, t)

        t = lax.fori_loop(0, 8, bit_step, jnp.zeros((1, tq), F32))
        tb = t.astype(BF16)

        def narrow(c, p, lvl=lvl, tb=tb):
            planes_ref[lvl - 1, c] = jnp.where(p == tb, planes_ref[lvl - 1, c], dead)

        above = above + count_where(lvl, lambda p: p > tb, narrow if lvl > 0 else None)
        thr_u = thr_u | jnp.left_shift(t.astype(I32), 8 * lvl)
    thr = thr_u ^ INT_MIN
    need = float(n_keep) - above
    tri = _tri_strict_lower(LANES, BF16)
    sub = lax.broadcasted_iota(I32, (LANES, tq), 0)
    qsub = t0 + lax.broadcasted_iota(I32, (LANES, tq), 1)

    def bias_chunk(c, offset):
        for j in range(kc // LANES):
            rows = slice(j * LANES, (j + 1) * LANES)
            sel, offset = _select_mask(keys_ref[c, rows, :], thr, need, offset, tri)
            causal = c * kc + j * LANES + sub <= qsub
            bias_ref[c, rows, :] = jnp.where(causal, (sel - 1.0) * 1e30, NEG)
        return offset

    lax.fori_loop(0, n_chunks, bias_chunk, jnp.zeros((1, tq), F32))

    accs = _flash_loop(qa_ref, ka_ref, vT_ref, lambda g, c: bias_ref[c], n_chunks, tq, kc)
    for heads, acc in accs:
        _store_heads(o_ref, _normalise(acc), heads, tq)


def _dsa(qi, wT, ki, qa, ka, vaT, bsz, seq, tq, kc):
    n_keep = min(DSA_TOPK_MAX, seq // 4)
    nq = seq // tq
    qblk = lambda w: pl.BlockSpec((1, tq, w), lambda b, i: (b, i, 0))
    full = lambda w: pl.BlockSpec((1, seq, w), lambda b, i: (b, 0, 0))
    return pl.pallas_call(
        functools.partial(_dsa_kernel, tq=tq, kc=kc, n_keep=n_keep),
        out_shape=jax.ShapeDtypeStruct((bsz, seq, A_HEADS * HEAD_DIM), BF16),
        grid=(bsz, nq),
        in_specs=[qblk(1024), pl.BlockSpec((IDX_HEADS, tq), lambda b, i: (0, b * nq + i)), full(LANES),
                  qblk(1024), full(LANES),
                  pl.BlockSpec((A_KV_HEADS, seq // kc, LANES, kc), lambda b, i: (0, b, 0, 0))],
        out_specs=qblk(512),
        scratch_shapes=[pltpu.VMEM((seq // kc, kc, tq), I32), pltpu.VMEM((seq // kc, kc, tq), F32),
                        pltpu.VMEM((4, seq // kc, kc, tq), BF16)],
        compiler_params=_cparams(("parallel", "parallel")),
        name="dsa",
    )(qi, wT, ki, qa, ka, vaT)


def _gelu_tanh(x):
    return 0.5 * x * (1.0 + jnp.tanh(math.sqrt(2.0 / math.pi) * (x + 0.044715 * (x * x * x))))


def _compress_kernel(k_ref, v_ref, pk_ref, pv_ref, w1k_ref, w2k_ref, w1v_ref, w2v_ref,
                     kc_ref, vcT_ref, *, n_rows):
    half = CMP_BLOCK // 2

    def one(x_ref, p_ref, w1_ref, w2_ref):
        outs = []
        for g in range(B_KV_HEADS):
            lo = jnp.zeros((n_rows, CMP_HIDDEN), F32)
            hi = jnp.zeros((n_rows, CMP_HIDDEN), F32)
            for l in range(half):
                xl = x_ref[0, pl.ds(l, n_rows, stride=CMP_STRIDE), :][:, g * HEAD_DIM:(g + 1) * HEAD_DIM]
                a = (xl + p_ref[l:l + 1, :]).astype(BF16)
                b = (xl + p_ref[half + l:half + l + 1, :]).astype(BF16)
                lo = lo + _dot(a, w1_ref[l * HEAD_DIM:(l + 1) * HEAD_DIM, :].astype(BF16))
                hi = hi + _dot(b, w1_ref[(half + l) * HEAD_DIM:(half + l + 1) * HEAD_DIM, :].astype(BF16))
            hid = lo + pltpu.roll(hi, n_rows - 1, axis=0)
            outs.append(_dot(_gelu_tanh(hid).astype(BF16), w2_ref[...].astype(BF16)))
        return outs

    k0, k1 = one(k_ref, pk_ref, w1k_ref, w2k_ref)
    kc_ref[0] = jnp.concatenate([k0, k1], axis=1).astype(kc_ref.dtype)
    for g, v in enumerate(one(v_ref, pv_ref, w1v_ref, w2v_ref)):
        vcT_ref[0, g] = jnp.concatenate([v, jnp.zeros_like(v)], axis=1).T.astype(vcT_ref.dtype)


def _compress(kcmp, vcmp, pos_k, pos_v, w1k, w2k, w1v, w2v):
    bsz, seq, width = kcmp.shape
    n_rows = seq // CMP_STRIDE
    xspec = pl.BlockSpec((1, seq, width), lambda b: (b, 0, 0))
    cst = lambda a: pl.BlockSpec(a.shape, lambda b: (0,) * a.ndim)
    return pl.pallas_call(
        functools.partial(_compress_kernel, n_rows=n_rows),
        out_shape=(jax.ShapeDtypeStruct((bsz, n_rows, LANES), BF16),
                   jax.ShapeDtypeStruct((bsz, B_KV_HEADS, LANES, n_rows), BF16)),
        grid=(bsz,),
        in_specs=[xspec, xspec, cst(pos_k), cst(pos_v), cst(w1k), cst(w2k), cst(w1v), cst(w2v)],
        out_specs=(pl.BlockSpec((1, n_rows, LANES), lambda b: (b, 0, 0)),
                   pl.BlockSpec((1, B_KV_HEADS, LANES, n_rows), lambda b: (b, 0, 0, 0))),
        compiler_params=_cparams(("parallel",)),
        name="compress",
    )(kcmp, vcmp, pos_k, pos_v, w1k, w2k, w1v, w2v)


def _split3(x):
    a = x.astype(BF16)
    r = x - a.astype(F32)
    b = r.astype(BF16)
    c = (r - b.astype(F32)).astype(BF16)
    return a, b, c


def _nsa_kernel(qraw_ref, qrot_ref, kc_ref, vcT_ref, ksel_ref, vselT_ref, kwin_ref, vwinT_ref,
                gT_ref, exp_ref, o_ref, bias_ref, *, tq, kc, seq, n_c):
    t0 = pl.program_id(1) * tq
    n_chunks = (t0 + tq + kc - 1) // kc
    n_s = seq // SEL_BLOCK
    n_pick = min(SEL_COUNT, n_s)
    wt = min(tq, WIN_TQ)
    span = WINDOW + wt
    gT = gT_ref[...]

    ridx = lax.broadcasted_iota(I32, (LANES, tq), 0)
    tl = t0 + lax.broadcasted_iota(I32, (LANES, tq), 1)
    valid_c = jnp.where((ridx * CMP_STRIDE + CMP_BLOCK - 1 <= tl) & (ridx < n_c), 1.0, 0.0)
    valid4 = jnp.concatenate([valid_c] * REP, axis=1) > 0.0
    js = lax.broadcasted_iota(I32, (LANES, LANES), 0) * SEL_BLOCK
    cs = lax.broadcasted_iota(I32, (LANES, LANES), 1) * CMP_STRIDE
    overlap_t = jnp.where((cs <= js + SEL_BLOCK - 1) & (cs + CMP_BLOCK - 1 >= js), 1.0, 0.0).astype(BF16)
    cur = tl // SEL_BLOCK
    forced = (ridx == 0) | ((cur - ridx >= 0) & (cur - ridx < SEL_LOCAL))
    blk_causal = ridx * SEL_BLOCK <= tl
    tri = _tri_strict_lower(LANES, BF16)
    keypos = lax.broadcasted_iota(I32, (kc, tq), 0)
    qpos = t0 + lax.broadcasted_iota(I32, (kc, tq), 1)

    o_cmp_g = []
    for g in range(B_KV_HEADS):
        heads = range(g * REP, (g + 1) * REP)
        s = jnp.where(valid4, _dot_nt(kc_ref[0], _stack_heads(qraw_ref, heads)), NEG)
        e = jnp.where(valid4, jnp.exp(s - _col_max(s)), 0.0)
        p = e * (1.0 / jnp.maximum(_col_sum(e), 1e-30))
        o_cmp_g.append(_dot(vcT_ref[0, g], p.astype(BF16)))
        p_sum = p[:, 0:tq]
        for r in range(1, REP):
            p_sum = p_sum + p[:, r * tq:(r + 1) * tq]
        pa, pb, pc = _split3(p_sum)
        imp = _dot(overlap_t, pa) + _dot(overlap_t, pb) + _dot(overlap_t, pc)
        imp = jnp.where(forced, FORCE, jnp.where(blk_causal, imp, NEG))
        keys = _sort_key(imp[0:n_s, :])
        count_ge = lambda cand, keys=keys: _col_sum(jnp.where(keys >= cand, 1.0, 0.0))
        thr = _kth_largest_key(count_ge, (1, tq), n_pick)
        need = float(n_pick) - _col_sum(jnp.where(keys > thr, 1.0, 0.0))
        blk_sel, _ = _select_mask(keys, thr, need, jnp.zeros((1, tq), F32), tri[0:n_s, 0:n_s])
        blk_sel = jnp.concatenate([blk_sel, jnp.zeros((LANES - n_s, tq), F32)], axis=0).astype(BF16)

        def bias_chunk(c, carry, blk_sel=blk_sel, g=g):
            tok_sel = _dot(exp_ref[c], blk_sel)
            bias_ref[g, c] = jnp.where(c * kc + keypos <= qpos, (tok_sel - 1.0) * 1e30, NEG)
            return carry

        lax.fori_loop(0, n_chunks, bias_chunk, 0)

    sel = _flash_loop(qrot_ref, ksel_ref, vselT_ref, lambda g, c: bias_ref[g, c], n_chunks, tq, kc)

    for (g, heads), (_, acc_sel) in zip(_head_chains(), sel):
        o_win_sub = []
        for sub in range(tq // wt):
            t_sub = t0 + sub * wt
            w0 = pl.multiple_of(jnp.clip(t_sub - WINDOW, 0, seq - span), wt)
            wdiff = ((t_sub + lax.broadcasted_iota(I32, (span, wt), 1))
                     - (w0 + lax.broadcasted_iota(I32, (span, wt), 0)))
            wbias = jnp.where((wdiff >= 0) & (wdiff < WINDOW), 0.0, NEG)
            q_sub = jnp.concatenate([qrot_ref[0, sub * wt:(sub + 1) * wt, h * LANES:(h + 1) * LANES]
                                     for h in heads], axis=0)
            sw = (_dot_nt(kwin_ref[0, pl.ds(w0, span), :], q_sub)
                  + jnp.concatenate([wbias] * HEADS_PER_CHAIN, axis=1))
            ew = jnp.exp(sw - _col_max(sw)).astype(BF16)
            wblk = w0 // LANES
            acc_win = _dot(vwinT_ref[g, wblk], ew[0:LANES, :])
            for j in range(1, span // LANES):
                acc_win = acc_win + _dot(vwinT_ref[g, wblk + j], ew[j * LANES:(j + 1) * LANES, :])
            o_win_sub.append(_normalise(acc_win))

        o_sel = _normalise(acc_sel)
        cols = []
        for r, h in enumerate(heads):
            col = slice(r * tq, (r + 1) * tq)
            cmp_col = slice((h % REP) * tq, (h % REP + 1) * tq)
            o_win = jnp.concatenate([o[:, r * wt:(r + 1) * wt] for o in o_win_sub], axis=1)
            cols.append(gT[3 * h:3 * h + 1, :] * o_cmp_g[g][:, cmp_col]
                        + gT[3 * h + 1:3 * h + 2, :] * o_sel[:, col]
                        + gT[3 * h + 2:3 * h + 3, :] * o_win)
        _store_heads(o_ref, jnp.concatenate(cols, axis=1), heads, tq)


def _nsa(qraw, qrot, kc_x, vcT, ksel, vselT, kwin, vwinT, gT, bsz, seq, tq, kc):
    n_c = (seq - CMP_BLOCK) // CMP_STRIDE + 1
    nq = seq // tq
    key_blk = (jnp.arange(seq, dtype=I32) // SEL_BLOCK).reshape(seq // kc, kc, 1)
    expand = (key_blk == jnp.arange(LANES, dtype=I32).reshape(1, 1, LANES)).astype(BF16)
    qblk = lambda w: pl.BlockSpec((1, tq, w), lambda b, i: (b, i, 0))
    full = lambda w: pl.BlockSpec((1, seq, w), lambda b, i: (b, 0, 0))
    g = B_KV_HEADS
    return pl.pallas_call(
        functools.partial(_nsa_kernel, tq=tq, kc=kc, seq=seq, n_c=n_c),
        out_shape=jax.ShapeDtypeStruct((bsz, seq, B_HEADS * HEAD_DIM), BF16),
        grid=(bsz, nq),
        in_specs=[qblk(1024), qblk(1024),
                  pl.BlockSpec((1,) + kc_x.shape[1:], lambda b, i: (b, 0, 0)),
                  pl.BlockSpec((1,) + vcT.shape[1:], lambda b, i: (b, 0, 0, 0)),
                  full(LANES), pl.BlockSpec((g, seq // kc, LANES, kc), lambda b, i: (0, b, 0, 0)),
                  full(LANES), pl.BlockSpec((g, seq // LANES, LANES, LANES), lambda b, i: (0, b, 0, 0)),
                  pl.BlockSpec((N_GATES, tq), lambda b, i: (0, b * nq + i)),
                  pl.BlockSpec(expand.shape, lambda b, i: (0, 0, 0))],
        out_specs=qblk(512),
        scratch_shapes=[pltpu.VMEM((g, seq // kc, kc, tq), F32)],
        compiler_params=_cparams(("parallel", "parallel")),
        name="nsa",
    )(qraw, qrot, kc_x, vcT, ksel, vselT, kwin, vwinT, gT, expand)


def _pack_pairs(x):
    n = x.shape[1] // 2
    lo = pltpu.bitcast(x[:, :n].astype(BF16).astype(F32), I32)
    hi = pltpu.bitcast(x[:, n:].astype(BF16).astype(F32), I32)
    return lax.shift_right_logical(lo, 16) | (hi & jnp.int32(-65536))


def _unpack_pairs(p):
    lo = pltpu.bitcast(lax.shift_left(p, 16), F32)
    hi = pltpu.bitcast(p & jnp.int32(-65536), F32)
    return jnp.concatenate([lo, hi], axis=1)


def _layer_norm(y, g, b):
    mu = jnp.mean(y, axis=1, keepdims=True)
    yc = y - mu
    var = jnp.mean(yc * yc, axis=1, keepdims=True)
    return yc * lax.rsqrt(var + LN_EPS) * g + b


def _out_proj_kernel(oa_ref, ob_ref, ga_ref, gb_ref, x_ref, mod_ref, wa_ref, wb_ref, wo_ref,
                     g1_ref, b1_ref, wrh_ref, wrl_ref, x1_ref, u2_ref, lg_ref):
    mod = mod_ref[0]
    merged = (ga_ref[...].astype(F32) * _dot(oa_ref[...], wa_ref[...])
              + gb_ref[...].astype(F32) * _dot(ob_ref[...], wb_ref[...]))
    mix = _dot(merged.astype(BF16), wo_ref[...])
    x1 = _layer_norm(DN_ALPHA * x_ref[...] + mod[2:3, :] * mix, g1_ref[...], b1_ref[...])
    x1_ref[...] = x1
    u2 = x1 * (1.0 + mod[4:5, :]) + mod[3:4, :]
    packed = _pack_pairs(u2)
    for j in range(PIECES):
        u2_ref[j] = packed[:, j * SC_ROW:(j + 1) * SC_ROW]
    uh = u2.astype(BF16)
    ul = (u2 - uh.astype(F32)).astype(BF16)
    lg_ref[...] = _dot_nt(wrh_ref[...], uh) + _dot_nt(wrh_ref[...], ul) + _dot_nt(wrl_ref[...], uh)


def _out_proj(oa, ob, ga, gb, x2, mod3, wa, wb, wo, g1, b1, wrh, wrl, seq, tm):
    t, d = x2.shape
    per_b = seq // tm
    row = lambda w: pl.BlockSpec((tm, w), lambda i: (i, 0))
    cst = lambda a: pl.BlockSpec(a.shape, lambda i: (0,) * a.ndim)
    return pl.pallas_call(
        _out_proj_kernel,
        out_shape=(jax.ShapeDtypeStruct((t, d), F32), jax.ShapeDtypeStruct((PIECES, t, SC_ROW), I32),
                   jax.ShapeDtypeStruct((N_EXPERTS, t), F32)),
        grid=(t // tm,),
        in_specs=[row(512), row(512), row(d), row(d), row(d),
                  pl.BlockSpec((1, 6, d), lambda i: (i // per_b, 0, 0)),
                  cst(wa), cst(wb), cst(wo), cst(g1), cst(b1), cst(wrh), cst(wrl)],
        out_specs=(row(d), pl.BlockSpec((PIECES, tm, SC_ROW), lambda i: (0, i, 0)),
                   pl.BlockSpec((N_EXPERTS, tm), lambda i: (0, i))),
        compiler_params=_cparams(("parallel",)),
        name="out_proj",
    )(oa, ob, ga, gb, x2, mod3, wa, wb, wo, g1, b1, wrh, wrl)


def _first_max(x, rows):
    m = jnp.max(x, axis=0, keepdims=True)
    idx = jnp.min(jnp.where(x == m, rows, 1e9), axis=0, keepdims=True)
    return m, idx


def _router_kernel(lg_ref, rb_ref, idx_ref, w_ref, pos_ref, cnt_ref, carry_ref, *, tm):
    @pl.when(pl.program_id(0) == 0)
    def _():
        carry_ref[...] = jnp.zeros_like(carry_ref)

    per_g = N_EXPERTS // N_GROUPS
    scores = _sigmoid(lg_ref[...])
    choice = scores + rb_ref[...][:, 0:1]
    rows = lax.broadcasted_iota(I32, (N_EXPERTS, tm), 0).astype(F32)
    rows_g = lax.broadcasted_iota(I32, (per_g, tm), 0).astype(F32)
    ninf = -jnp.inf

    gs = []
    for g in range(N_GROUPS):
        x = choice[g * per_g:(g + 1) * per_g, :]
        m1, i1 = _first_max(x, rows_g)
        m2 = jnp.max(jnp.where(rows_g == i1, ninf, x), axis=0, keepdims=True)
        gs.append(m1 + m2)
    gscore = jnp.concatenate(gs, axis=0)
    rows8 = lax.broadcasted_iota(I32, (N_GROUPS, tm), 0).astype(F32)
    keep = jnp.zeros((N_GROUPS, tm), F32)
    for _ in range(TOPK_GROUPS):
        _, gi = _first_max(gscore, rows8)
        hit = rows8 == gi
        keep = jnp.where(hit, 1.0, keep)
        gscore = jnp.where(hit, ninf, gscore)
    keep_full = jnp.concatenate(
        [jnp.broadcast_to(keep[g:g + 1, :], (per_g, tm)) for g in range(N_GROUPS)], axis=0)
    masked = jnp.where(keep_full > 0.0, choice, NEG)

    idxs, ws = [], []
    onehot = jnp.zeros((N_EXPERTS, tm), F32)
    for _ in range(TOP_K):
        _, ei = _first_max(masked, rows)
        hit = rows == ei
        idxs.append(ei)
        ws.append(jnp.sum(jnp.where(hit, scores, 0.0), axis=0, keepdims=True))
        masked = jnp.where(hit, ninf, masked)
        onehot = jnp.where(hit, 1.0, onehot)
    idx = jnp.concatenate(idxs, axis=0)
    w = jnp.concatenate(ws, axis=0)
    idx_ref[...] = idx.astype(I32)
    w = w / jnp.sum(w, axis=0, keepdims=True) * ROUTED_SCALE
    w_ref[...] = jnp.concatenate([w, jnp.zeros((LANES - TOP_K, tm), F32)], axis=0).T

    tri = _tri_strict_upper(tm, BF16)
    base = _dot(onehot.astype(BF16), tri) + carry_ref[...][:, 0:1]
    pos = [jnp.sum(jnp.where(rows == idxs[k], base, 0.0), axis=0, keepdims=True) for k in range(TOP_K)]
    pos_ref[...] = jnp.concatenate(pos, axis=0).astype(I32)
    carry = carry_ref[...] + jnp.sum(onehot, axis=1, keepdims=True)
    carry_ref[...] = carry
    cnt_ref[...] = carry


def _router(lg, router_bias, tm):
    e, t = lg.shape
    rb = jnp.broadcast_to(router_bias.reshape(e, 1).astype(F32), (e, LANES))
    tok = lambda r: pl.BlockSpec((r, tm), lambda i: (0, i))
    return pl.pallas_call(
        functools.partial(_router_kernel, tm=tm),
        out_shape=(jax.ShapeDtypeStruct((TOP_K, t), I32), jax.ShapeDtypeStruct((t, LANES), F32),
                   jax.ShapeDtypeStruct((TOP_K, t), I32), jax.ShapeDtypeStruct((e, LANES), F32)),
        grid=(t // tm,),
        in_specs=[tok(e), pl.BlockSpec((e, LANES), lambda i: (0, 0))],
        out_specs=(tok(TOP_K), pl.BlockSpec((tm, LANES), lambda i: (i, 0)), tok(TOP_K),
                   pl.BlockSpec((e, LANES), lambda i: (0, 0))),
        scratch_shapes=[pltpu.VMEM((e, LANES), F32)],
        compiler_params=_cparams(("arbitrary",)),
        name="router",
    )(lg, rb)


def _dest_kernel(idx_ref, pos_ref, st_ref, dest_ref, *, tm):
    rows = lax.broadcasted_iota(I32, (N_EXPERTS, tm), 0)
    starts = st_ref[...][:, 0:1]
    idx = idx_ref[...]
    out = []
    for k in range(TOP_K):
        out.append(jnp.sum(jnp.where(rows == idx[k:k + 1, :], starts, 0.0), axis=0, keepdims=True))
    dest_ref[...] = jnp.concatenate(out, axis=0).astype(I32) + pos_ref[...]


def _dest(idx, pos, starts, tm):
    k, t = idx.shape
    st = jnp.broadcast_to(starts.reshape(N_EXPERTS, 1).astype(F32), (N_EXPERTS, LANES))
    tok = pl.BlockSpec((k, tm), lambda i: (0, i))
    return pl.pallas_call(
        functools.partial(_dest_kernel, tm=tm),
        out_shape=jax.ShapeDtypeStruct((k, t), I32),
        grid=(t // tm,),
        in_specs=[tok, tok, pl.BlockSpec((N_EXPERTS, LANES), lambda i: (0, 0))],
        out_specs=tok,
        compiler_params=_cparams(("parallel",)),
        name="dest",
    )(idx, pos, st)


def _experts_kernel(blk_ref, used_ref, first_ref, slot_ref, next_ref, x_ref, wg_hbm, wu_hbm, wd_hbm, y_ref,
                    wg_buf, wu_buf, wd_buf, wg_bf, wu_bf, wd_bf, sems):
    b = pl.program_id(0)
    active = b < used_ref[0]

    def weight_copies(e, slot):
        return [pltpu.make_async_copy(hbm.at[e], buf.at[slot], sems.at[slot, i])
                for i, (hbm, buf) in enumerate(((wg_hbm, wg_buf), (wu_hbm, wu_buf), (wd_hbm, wd_buf)))]

    @pl.when(b == 0)
    def _():
        for cp in weight_copies(blk_ref[0], 0):
            cp.start()

    @pl.when(active & (first_ref[b] == 1))
    def _():
        slot = slot_ref[b]
        for cp in weight_copies(blk_ref[b], slot):
            cp.wait()
        nxt = next_ref[b]

        @pl.when(nxt >= 0)
        def _():
            for cp in weight_copies(nxt, 1 - slot):
                cp.start()

        wg_bf[...] = wg_buf[slot].astype(BF16)
        wu_bf[...] = wu_buf[slot].astype(BF16)
        wd_bf[...] = wd_buf[slot].astype(BF16)

    @pl.when(active)
    def _():
        x = jnp.concatenate([x_ref[j] for j in range(PIECES)], axis=1)
        x = _unpack_pairs(x).astype(BF16)
        a = _dot(x, wg_bf[...])
        u = _dot(x, wu_bf[...])
        h = (a * _sigmoid(a) * u).astype(BF16)
        y = _pack_pairs(_dot(h, wd_bf[...]))
        for j in range(PIECES):
            y_ref[j] = y[:, j * SC_ROW:(j + 1) * SC_ROW]


def _experts(xs, plan, wg, wu, wd, bm):
    _, cap, _ = xs.shape
    n_blocks = cap // bm
    d, f = wg.shape[1], wg.shape[2]
    rows = pl.BlockSpec((PIECES, bm, SC_ROW), lambda b, blk, used, *_: (0, jnp.minimum(b, used[0] - 1), 0))
    hbm = pl.BlockSpec(memory_space=pl.ANY)
    return pl.pallas_call(
        _experts_kernel,
        out_shape=jax.ShapeDtypeStruct(xs.shape, I32),
        grid_spec=pltpu.PrefetchScalarGridSpec(
            num_scalar_prefetch=5,
            grid=(n_blocks,),
            in_specs=[rows, hbm, hbm, hbm],
            out_specs=rows,
            scratch_shapes=[pltpu.VMEM((2, d, f), F32), pltpu.VMEM((2, d, f), F32), pltpu.VMEM((2, f, d), F32),
                            pltpu.VMEM((d, f), BF16), pltpu.VMEM((d, f), BF16), pltpu.VMEM((f, d), BF16),
                            pltpu.SemaphoreType.DMA((2, 3))]),
        compiler_params=_cparams(("arbitrary",)),
        name="experts",
    )(plan["blk_e"], plan["n_used"], plan["first"], plan["slot"], plan["next_e"], xs, wg, wu, wd)


def _final_kernel(x1_ref, u2_ref, yg_ref, w_ref, mod_ref, sg_ref, su_ref, sd_ref, g2_ref, b2_ref, o_ref):
    mod = mod_ref[0]
    w = w_ref[...]
    rows = lambda ref, *lead: jnp.concatenate([ref[(j,) + lead] for j in range(PIECES)], axis=1)
    routed = w[:, 0:1] * _unpack_pairs(rows(yg_ref, 0))
    for k in range(1, TOP_K):
        routed = routed + w[:, k:k + 1] * _unpack_pairs(rows(yg_ref, k))
    u = _unpack_pairs(rows(u2_ref)).astype(BF16)
    a = _dot(u, sg_ref[...])
    b = _dot(u, su_ref[...])
    shared = _dot((a * _sigmoid(a) * b).astype(BF16), sd_ref[...])
    y = DN_ALPHA * x1_ref[...] + mod[5:6, :] * (routed + shared)
    o_ref[...] = _layer_norm(y, g2_ref[...], b2_ref[...])


def _final(x1, u2p, yg, wtok, mod3, sg, su, sd, g2, b2, seq, tm):
    t, d = x1.shape
    per_b = seq // tm
    row = lambda w: pl.BlockSpec((tm, w), lambda i: (i, 0))
    cst = lambda a: pl.BlockSpec(a.shape, lambda i: (0,) * a.ndim)
    return pl.pallas_call(
        _final_kernel,
        out_shape=jax.ShapeDtypeStruct((t, d), F32),
        grid=(t // tm,),
        in_specs=[row(d), pl.BlockSpec((PIECES, tm, SC_ROW), lambda i: (0, i, 0)),
                  pl.BlockSpec((PIECES, TOP_K, tm, SC_ROW), lambda i: (0, 0, i, 0)),
                  row(LANES), pl.BlockSpec((1, 6, d), lambda i: (i // per_b, 0, 0)),
                  cst(sg), cst(su), cst(sd), cst(g2), cst(b2)],
        out_specs=row(d),
        compiler_params=_cparams(("parallel",)),
        name="final",
    )(x1, u2p, yg, wtok, mod3, sg, su, sd, g2, b2)


def _sc_mesh():
    return plsc.VectorSubcoreMesh(core_axis_name="core", subcore_axis_name="subcore")


def _sc_scatter_rows(src, idx, n_out):
    n_copies, n = idx.shape

    @functools.partial(pl.kernel, out_type=jax.ShapeDtypeStruct((n_out, SC_ROW), src.dtype),
                       mesh=_sc_mesh(), scratch_types=[])
    def k(x_hbm, i_hbm, o_hbm):
        def body(x_vmem, i_vmem):
            for c in range(n_copies):
                pltpu.sync_copy(x_vmem, o_hbm.at[i_vmem.at[c]])

        pltpu.emit_pipeline(
            body, grid=(n // SC_WINDOW,),
            in_specs=[pl.BlockSpec((SC_WINDOW, SC_ROW), lambda i: (i, 0)),
                      pl.BlockSpec((n_copies, SC_WINDOW), lambda i: (0, i))],
            out_specs=[],
            core_axis_name=("core", "subcore"),
            dimension_semantics=(pltpu.PARALLEL,),
        )(x_hbm, i_hbm)

    return k(src, idx)


def _sc_gather_rows(src, idx):
    n_idx = idx.shape[0]

    @functools.partial(pl.kernel, out_type=jax.ShapeDtypeStruct((n_idx, SC_ROW), src.dtype),
                       mesh=_sc_mesh(), scratch_types=[])
    def k(x_hbm, i_hbm, o_hbm):
        def body(i_vmem, o_vmem):
            pltpu.sync_copy(x_hbm.at[i_vmem.at[0]], o_vmem)

        pltpu.emit_pipeline(
            body, grid=(n_idx // SC_WINDOW,),
            in_specs=[pl.BlockSpec((1, SC_WINDOW), lambda i: (0, i))],
            out_specs=[pl.BlockSpec((SC_WINDOW, SC_ROW), lambda i: (i, 0))],
            core_axis_name=("core", "subcore"),
            dimension_semantics=(pltpu.PARALLEL,),
        )(i_hbm, o_hbm)

    return k(src, idx.reshape(1, n_idx))


def _moe_plan(counts, n_tok):
    bm = BM_EXPERT
    padded = (counts + bm - 1) // bm * bm
    p_ends = jnp.cumsum(padded)
    starts = p_ends - padded
    n_blocks = n_tok * TOP_K // bm + N_EXPERTS
    blk = jnp.arange(n_blocks, dtype=I32)
    blk_e = jnp.minimum(jnp.sum(p_ends[None, :] <= (blk * bm)[:, None], axis=1), N_EXPERTS - 1).astype(I32)
    n_used = (p_ends[-1] // bm).astype(I32)
    prev = jnp.concatenate([jnp.full((1,), -1, I32), blk_e[:-1]])
    first = ((blk_e != prev) & (blk < n_used)).astype(I32)
    slot = (jnp.cumsum(first) - 1) % 2
    first_pos = jnp.where(first == 1, blk, n_blocks)
    next_first = lax.cummin(jnp.concatenate([first_pos[1:], jnp.full((1,), n_blocks, I32)]), reverse=True)
    next_e = jnp.where(next_first < n_blocks, blk_e[jnp.minimum(next_first, n_blocks - 1)], -1)
    plan = dict(blk_e=blk_e, n_used=n_used.reshape(1), first=first, slot=slot.astype(I32),
                next_e=next_e.astype(I32))
    return starts, plan, n_blocks


def _layer(x, mod, positions, w_in, w_br_a, w_br_b, w_out, cmp_pos_k, cmp_pos_v, cmp_k_w1, cmp_k_w2,
           cmp_v_w1, cmp_v_w2, ln1_g, ln1_b, w_router, router_bias, w_exp_gate, w_exp_up, w_exp_down,
           w_sh_gate, w_sh_up, w_sh_down, ln2_g, ln2_b):
    bsz, seq, d = x.shape
    t = bsz * seq
    assert seq // CMP_STRIDE == LANES and seq % TQ_ATTN == 0
    assert seq % TM_PROJ == 0 and KC_ATTN == TM_PROJ
    x2 = x.reshape(t, d)
    mod3 = mod.reshape(bsz, 6, d)

    w_pack, w_small = _pack_w_in(w_in)
    z = _in_proj(x2, mod3, w_pack, w_small, _rope_tables(positions), seq, TM_PROJ)
    per_b = lambda name: z[name].reshape(bsz, seq, z[name].shape[1])

    o_a = _dsa(per_b("qi"), z["wT"], per_b("ki"), per_b("qa"), per_b("ka"), z["vaT"],
               bsz, seq, TQ_ATTN, KC_ATTN)

    kc_x, vcT = _compress(per_b("kcmp"), per_b("vcmp"),
                          cmp_pos_k, cmp_pos_v, cmp_k_w1, cmp_k_w2, cmp_v_w1, cmp_v_w2)
    o_b = _nsa(per_b("qbraw"), per_b("qbrot"), kc_x, vcT, per_b("ksel"), z["vselT"], per_b("kwin"),
               z["vwinT"], z["gT"], bsz, seq, TQ_ATTN, KC_ATTN)

    wr_hi = w_router.T.astype(BF16)
    wr_lo = (w_router.T - wr_hi.astype(F32)).astype(BF16)
    x1, u2p, logits = _out_proj(
        o_a.reshape(t, -1), o_b.reshape(t, -1), z["ga"], z["gb"], x2, mod3,
        w_br_a.astype(BF16), w_br_b.astype(BF16), w_out.astype(BF16),
        ln1_g.reshape(1, d), ln1_b.reshape(1, d), wr_hi, wr_lo, seq, TM_PROJ)

    idx, wtok, pos, counts = _router(logits, router_bias, TM_ROUTE)
    starts, plan, n_blocks = _moe_plan(counts[:, 0].astype(I32), t)
    dest = _dest(idx, pos, starts, TM_ROUTE)

    cap = n_blocks * BM_EXPERT
    dest_p = dest[None] + (jnp.arange(PIECES, dtype=I32) * cap).reshape(PIECES, 1, 1)
    xs = _sc_scatter_rows(u2p.reshape(PIECES * t, SC_ROW),
                          jnp.swapaxes(dest_p, 0, 1).reshape(TOP_K, PIECES * t), cap * PIECES)
    ys = _experts(xs.reshape(PIECES, cap, SC_ROW), plan, w_exp_gate, w_exp_up, w_exp_down, BM_EXPERT)
    yg = _sc_gather_rows(ys.reshape(cap * PIECES, SC_ROW), dest_p.reshape(-1)
                         ).reshape(PIECES, TOP_K, t, SC_ROW)

    return _final(x1, u2p, yg, wtok, mod3, w_sh_gate.astype(BF16), w_sh_up.astype(BF16),
                  w_sh_down.astype(BF16), ln2_g.reshape(1, d), ln2_b.reshape(1, d), seq, TM_PROJ
                  ).reshape(bsz, seq, d)


def kernel(x, c, positions, w_ada, b_ada, w_in, w_br_a, w_br_b, w_out, cmp_pos_k, cmp_pos_v, cmp_k_w1,
           cmp_k_w2, cmp_v_w1, cmp_v_w2, ln1_g, ln1_b, w_router, router_bias, w_exp_gate, w_exp_up,
           w_exp_down, w_sh_gate, w_sh_up, w_sh_down, ln2_g, ln2_b):
    for l in range(w_ada.shape[0]):
        mod = _mod(c, w_ada[l], b_ada[l])
        x = _layer(x, mod, positions, w_in[l], w_br_a[l], w_br_b[l], w_out[l], cmp_pos_k[l], cmp_pos_v[l],
                   cmp_k_w1[l], cmp_k_w2[l], cmp_v_w1[l], cmp_v_w2[l], ln1_g[l], ln1_b[l], w_router[l],
                   router_bias[l], w_exp_gate[l], w_exp_up[l], w_exp_down[l], w_sh_gate[l], w_sh_up[l],
                   w_sh_down[l], ln2_g[l], ln2_b[l])
    return x
```

```python
import functools
import math

import jax
import jax.numpy as jnp
from jax import lax
from jax.experimental import pallas as pl
from jax.experimental.pallas import tpu as pltpu
from jax.experimental.pallas import tpu_sc as plsc

F32 = jnp.float32
BF16 = jnp.bfloat16
I32 = jnp.int32

D_MODEL = 1024
HEAD_DIM = 64
ROPE_THETA = 500000.0
ROPE_FRACTION = 4
A_HEADS = 8
A_KV_HEADS = 2
IDX_HEADS = 8
IDX_DIM = 32
DSA_TOPK_MAX = 256
B_HEADS = 8
B_KV_HEADS = 2
REP = 4
CMP_BLOCK = 32
CMP_STRIDE = 16
CMP_HIDDEN = 256
SEL_BLOCK = 64
SEL_COUNT = 16
SEL_LOCAL = 2
WINDOW = 512
N_EXPERTS = 256
TOP_K = 8
N_GROUPS = 8
TOPK_GROUPS = 4
ROUTED_SCALE = 2.5
DEPTH = 1
DN_ALPHA = (2 * DEPTH) ** 0.25
LN_EPS = 1e-5
NEG = -1e30
FORCE = 1e9
INT_MIN = -2147483648
N_GATES = 3 * B_HEADS

LANES = 128
SUBLANES = 8
VMEM_LIMIT = 56 * 1024 * 1024
SC_WINDOW = 128
SC_ROW = 256
PIECES = (D_MODEL // 2) // SC_ROW

TM_PROJ = 512
TQ_ATTN = 512
WIN_TQ = 256
KC_ATTN = 512
TM_ROUTE = 512
BM_EXPERT = 512

_IN_SLOTS = (("qa", A_HEADS * HEAD_DIM), ("ka", A_KV_HEADS * HEAD_DIM), ("va", A_KV_HEADS * HEAD_DIM),
             ("qi", IDX_HEADS * IDX_DIM), ("ki", IDX_DIM), ("wi", IDX_HEADS), ("qb", B_HEADS * HEAD_DIM),
             ("kcmp", B_KV_HEADS * HEAD_DIM), ("vcmp", B_KV_HEADS * HEAD_DIM), ("ksel", B_KV_HEADS * HEAD_DIM),
             ("vsel", B_KV_HEADS * HEAD_DIM), ("kwin", B_KV_HEADS * HEAD_DIM), ("vwin", B_KV_HEADS * HEAD_DIM),
             ("gnsa", N_GATES), ("ga", D_MODEL), ("gb", D_MODEL))


def _col_ranges(slots, align):
    out, start = {}, 0
    for name, width in slots:
        out[name] = (start, start + -(-width // align) * align)
        start = out[name][1]
    return out


_IN_COLS = _col_ranges(_IN_SLOTS, 1)
_PACK_COLS = _col_ranges([s for s in _IN_SLOTS if s[0] not in ("wi", "gnsa")], LANES)

NT_DIMS = (((1,), (1,)), ((), ()))


def _cparams(sem):
    return pltpu.CompilerParams(dimension_semantics=sem, vmem_limit_bytes=VMEM_LIMIT)


def _sigmoid(x):
    return 1.0 / (1.0 + jnp.exp(-x))


def _dot(a, b):
    return jnp.dot(a, b, preferred_element_type=F32)


def _dot_nt(a, b):
    return lax.dot_general(a, b, NT_DIMS, preferred_element_type=F32)


def _sort_key(x):
    x = jnp.where(x == 0.0, 0.0, x)
    bits = pltpu.bitcast(x, I32)
    return jnp.where(bits < 0, bits ^ 0x7FFFFFFF, bits)


def _kth_largest_key(count_ge, shape, k):
    kf = float(k)
    t0 = jnp.where(count_ge(jnp.zeros(shape, I32)) >= kf, 0, INT_MIN).astype(I32)

    def body(it, t):
        cand = t + jnp.left_shift(jnp.int32(1), 30 - it)
        return jnp.where(count_ge(cand) >= kf, cand, t)

    return lax.fori_loop(0, 31, body, t0)


def _tri_strict_lower(n, dtype):
    r = lax.broadcasted_iota(I32, (n, n), 0)
    c = lax.broadcasted_iota(I32, (n, n), 1)
    return jnp.where(c < r, 1.0, 0.0).astype(dtype)


def _tri_strict_upper(n, dtype):
    r = lax.broadcasted_iota(I32, (n, n), 0)
    c = lax.broadcasted_iota(I32, (n, n), 1)
    return jnp.where(r < c, 1.0, 0.0).astype(dtype)


def _mod_kernel(c_ref, w_ref, b_ref, o_ref):
    c = c_ref[...]
    cond = (c * _sigmoid(c)).astype(BF16)
    o_ref[...] = _dot(cond, w_ref[...].astype(BF16)) + b_ref[...]


def _mod(c, w_ada, b_ada):
    bsz, d = c.shape
    n = w_ada.shape[1]
    tn = 1024
    return pl.pallas_call(
        _mod_kernel,
        out_shape=jax.ShapeDtypeStruct((bsz, n), F32),
        grid=(n // tn,),
        in_specs=[pl.BlockSpec((bsz, d), lambda j: (0, 0)),
                  pl.BlockSpec((d, tn), lambda j: (0, j)),
                  pl.BlockSpec((1, tn), lambda j: (0, j))],
        out_specs=pl.BlockSpec((bsz, tn), lambda j: (0, j)),
        compiler_params=_cparams(("parallel",)),
        name="mod",
    )(c, w_ada, b_ada.reshape(1, n))


def _rope(z, c_tab, s_tab, period, half):
    w = z.shape[1]
    reps = w // LANES
    c = jnp.concatenate([c_tab] * reps, axis=1) if reps > 1 else c_tab
    s = jnp.concatenate([s_tab] * reps, axis=1) if reps > 1 else s_tab
    lane = lax.broadcasted_iota(I32, z.shape, 1)
    first = (lane & (period - 1)) < half
    partner = jnp.where(first, pltpu.roll(z, w - half, axis=1), pltpu.roll(z, half, axis=1))
    return z * c + partner * s


def _in_proj_kernel(x_ref, mod_ref, w_ref, wsm_ref, rope_ref,
                    qa_ref, ka_ref, vaT_ref, qi_ref, ki_ref, wT_ref, qbraw_ref, qbrot_ref,
                    kcmp_ref, vcmp_ref, ksel_ref, vselT_ref, kwin_ref, vwinT_ref, gT_ref, ga_ref, gb_ref):
    mod = mod_ref[0]
    u = (x_ref[...] * (1.0 + mod[1:2, :]) + mod[0:1, :]).astype(BF16)
    tm = u.shape[0]
    c64, s64 = _rope_patterns(rope_ref[...], HEAD_DIM)
    c32, s32 = _rope_patterns(rope_ref[...], IDX_DIM)
    scale = HEAD_DIM ** -0.5
    lane = lax.broadcasted_iota(I32, (tm, LANES), 1)
    low = lane < HEAD_DIM

    def proj(name):
        a, b = _PACK_COLS[name]
        return _dot(u, w_ref[:, a:b])

    rope64 = lambda z: _rope(z, c64, s64, HEAD_DIM, HEAD_DIM // ROPE_FRACTION // 2)
    rope32 = lambda z: _rope(z, c32, s32, IDX_DIM, IDX_DIM // ROPE_FRACTION // 2)

    def head_slots64(z):
        out = []
        for h in range(A_HEADS):
            pair = z[:, (h // 2) * LANES:(h // 2 + 1) * LANES]
            g = h // REP
            src = pair if h % 2 == g else pltpu.roll(pair, HEAD_DIM, axis=1)
            out.append(jnp.where(low, src, 0.0) if g == 0 else jnp.where(low, 0.0, src))
        return jnp.concatenate(out, axis=1).astype(BF16)

    def head_slots32(z):
        per = LANES // IDX_DIM
        out = []
        for h in range(IDX_HEADS):
            col = z[:, (h // per) * LANES:(h // per + 1) * LANES]
            shift = IDX_DIM * (h % per)
            src = col if shift == 0 else pltpu.roll(col, LANES - shift, axis=1)
            out.append(jnp.where(lane < IDX_DIM, src, 0.0))
        return jnp.concatenate(out, axis=1).astype(BF16)

    def store_vt(ref, z, chunk):
        zt = z.T
        ones = jnp.ones((HEAD_DIM, chunk), F32)
        for g in range(A_KV_HEADS):
            for j in range(tm // chunk):
                blk = zt[g * HEAD_DIM:(g + 1) * HEAD_DIM, j * chunk:(j + 1) * chunk]
                ref[g, j] = jnp.concatenate([blk, ones], axis=0).astype(BF16)

    qa_ref[...] = head_slots64(rope64(proj("qa")) * scale)
    ka_ref[...] = rope64(proj("ka")).astype(BF16)
    store_vt(vaT_ref, proj("va"), tm)
    qi_ref[...] = head_slots32(rope32(proj("qi")))
    ki_ref[...] = rope32(proj("ki")).astype(BF16)
    qb = proj("qb")
    qbraw_ref[...] = head_slots64(qb * scale)
    qbrot_ref[...] = head_slots64(rope64(qb) * scale)
    kcmp_ref[...] = proj("kcmp")
    vcmp_ref[...] = proj("vcmp")
    ksel_ref[...] = rope64(proj("ksel")).astype(BF16)
    store_vt(vselT_ref, proj("vsel"), tm)
    kwin_ref[...] = rope64(proj("kwin")).astype(BF16)
    store_vt(vwinT_ref, proj("vwin"), LANES)
    ga_ref[...] = _sigmoid(proj("ga")).astype(BF16)
    gb_ref[...] = _sigmoid(proj("gb")).astype(BF16)
    small = _dot_nt(wsm_ref[...], u)
    wT_ref[...] = small[0:IDX_HEADS, :]
    gT_ref[...] = _sigmoid(small[IDX_HEADS:IDX_HEADS + N_GATES, :])


def _pack_w_in(w_in):
    d = w_in.shape[0]
    col = lambda name: w_in[:, _IN_COLS[name][0]:_IN_COLS[name][1]]
    parts = []
    for name, (a, b) in _PACK_COLS.items():
        c = col(name)
        parts.append(jnp.pad(c, ((0, 0), (0, b - a - c.shape[1]))))
    w_small = jnp.concatenate([col("wi"), col("gnsa")], axis=1).T
    return jnp.concatenate(parts, axis=1).astype(BF16), w_small.astype(BF16)


def _rope_halves(dim):
    return dim // ROPE_FRACTION // 2


_ROPE_LANES = {HEAD_DIM: 0, IDX_DIM: 2 * _rope_halves(HEAD_DIM)}


def _rope_table(positions):
    pos = positions.astype(F32).reshape(-1, 1)
    parts = []
    for dim in (HEAD_DIM, IDX_DIM):
        half = _rope_halves(dim)
        inv = ROPE_THETA ** (-(jnp.arange(half, dtype=F32) * 2.0) / (2 * half))
        ang = pos * inv
        parts += [jnp.cos(ang), jnp.sin(ang)]
    tab = jnp.concatenate(parts, axis=1)
    return jnp.pad(tab, ((0, 0), (0, LANES - tab.shape[1])))


def _rope_patterns(tab, dim):
    half = _rope_halves(dim)
    src = _ROPE_LANES[dim]
    lane = lax.broadcasted_iota(I32, tab.shape, 1)

    def spread(at, negate_first):
        v = jnp.where((lane >= at) & (lane < at + half), tab, 0.0)
        v = pltpu.roll(v, (LANES - at) % LANES, axis=1) if at else v
        rep = (-v if negate_first else v) + pltpu.roll(v, half, axis=1)
        period = dim
        while period < LANES:
            rep = rep + pltpu.roll(rep, period, axis=1)
            period *= 2
        return rep

    rotated = (lane & (dim - 1)) < 2 * half
    return jnp.where(rotated, spread(src, False), 1.0), spread(src + half, True)


def _in_proj(x2, mod3, w_pack, w_small, rope_tab, seq, tm):
    t, d = x2.shape
    n = w_pack.shape[1]
    per_b = seq // tm
    g = A_KV_HEADS
    row = lambda w: pl.BlockSpec((tm, w), lambda i: (i, 0))
    tok = lambda r: pl.BlockSpec((r, tm), lambda i: (0, i))
    vt_chunk = pl.BlockSpec((g, 1, LANES, tm), lambda i: (0, i, 0, 0))
    vt_lane = pl.BlockSpec((g, tm // LANES, LANES, LANES), lambda i: (0, i, 0, 0))
    sds = jax.ShapeDtypeStruct
    vt_chunk_shape = sds((g, t // tm, LANES, tm), BF16)
    outs = (("qa", sds((t, 1024), BF16), row(1024)), ("ka", sds((t, LANES), BF16), row(LANES)),
            ("vaT", vt_chunk_shape, vt_chunk), ("qi", sds((t, 1024), BF16), row(1024)),
            ("ki", sds((t, LANES), BF16), row(LANES)), ("wT", sds((IDX_HEADS, t), F32), tok(IDX_HEADS)),
            ("qbraw", sds((t, 1024), BF16), row(1024)), ("qbrot", sds((t, 1024), BF16), row(1024)),
            ("kcmp", sds((t, LANES), F32), row(LANES)), ("vcmp", sds((t, LANES), F32), row(LANES)),
            ("ksel", sds((t, LANES), BF16), row(LANES)), ("vselT", vt_chunk_shape, vt_chunk),
            ("kwin", sds((t, LANES), BF16), row(LANES)),
            ("vwinT", sds((g, t // LANES, LANES, LANES), BF16), vt_lane),
            ("gT", sds((N_GATES, t), F32), tok(N_GATES)),
            ("ga", sds((t, d), BF16), row(d)), ("gb", sds((t, d), BF16), row(d)))
    res = pl.pallas_call(
        _in_proj_kernel,
        out_shape=tuple(o[1] for o in outs),
        grid=(t // tm,),
        in_specs=[row(d),
                  pl.BlockSpec((1, 6, d), lambda i: (i // per_b, 0, 0)),
                  pl.BlockSpec((d, n), lambda i: (0, 0)),
                  pl.BlockSpec(w_small.shape, lambda i: (0, 0)),
                  row(LANES)],
        out_specs=tuple(o[2] for o in outs),
        compiler_params=_cparams(("parallel",)),
        name="in_proj",
    )(x2, mod3, w_pack, w_small, rope_tab)
    return {o[0]: r for o, r in zip(outs, res)}


def _fold_rows(x, op):
    n = x.shape[0]
    while n % (2 * SUBLANES) == 0:
        n //= 2
        x = op(x[:n], x[n:])
    slabs = [x[i * SUBLANES:(i + 1) * SUBLANES] for i in range(n // SUBLANES)]
    while len(slabs) > 1:
        nxt = [op(slabs[i], slabs[i + 1]) for i in range(0, len(slabs) - 1, 2)]
        slabs = nxt + ([slabs[-1]] if len(slabs) % 2 else [])
    return slabs[0]


def _col_max(x):
    return jnp.max(_fold_rows(x, jnp.maximum), axis=0, keepdims=True)


def _col_sum(x):
    return jnp.sum(_fold_rows(x, jnp.add), axis=0, keepdims=True)


PACKED_ROWS = 16


def _fold_rows_packed(x):
    n = x.shape[0]
    assert n % PACKED_ROWS == 0 and n // PACKED_ROWS <= 256
    while n > PACKED_ROWS:
        n //= 2
        x = x[:n] + x[n:]
    return x


def _stack_heads(q_ref, heads):
    return jnp.concatenate([q_ref[0, :, h * LANES:(h + 1) * LANES] for h in heads], axis=0)


def _flash_step(k, q_stack, v_t, bias4, m, acc):
    s = _dot_nt(k, q_stack) + bias4
    m_new = jnp.maximum(m, _col_max(s))
    e = jnp.exp(s - m_new).astype(BF16)
    return m_new, acc * jnp.exp(m - m_new) + _dot(v_t, e)


HEADS_PER_CHAIN = REP


def _head_chains():
    return [(h // REP, tuple(range(h, h + HEADS_PER_CHAIN))) for h in range(0, A_HEADS, HEADS_PER_CHAIN)]


def _flash_loop(q_ref, k_ref, vT_ref, bias_of, n_chunks, tq, kc):
    chains = _head_chains()
    q_stacks = [_stack_heads(q_ref, heads) for _, heads in chains]
    width = HEADS_PER_CHAIN * tq

    def body(c, carry):
        k = k_ref[0, pl.ds(pl.multiple_of(c * kc, kc), kc), :]
        out = []
        for (g, _), q_stack, (m, acc) in zip(chains, q_stacks, carry):
            bias = jnp.concatenate([bias_of(g, c)] * HEADS_PER_CHAIN, axis=1)
            out.append(_flash_step(k, q_stack, vT_ref[g, c], bias, m, acc))
        return tuple(out)

    init = tuple((jnp.full((1, width), NEG, F32), jnp.zeros((LANES, width), F32)) for _ in chains)
    res = lax.fori_loop(0, n_chunks, body, init)
    return [(heads, acc) for (_, heads), (_, acc) in zip(chains, res)]


def _normalise(acc):
    return acc * (1.0 / jnp.maximum(acc[HEAD_DIM:HEAD_DIM + 1, :], 1e-30))


def _store_heads(o_ref, o_t, heads, tq):
    o = o_t.T
    for r, h in enumerate(heads):
        o_ref[0, :, h * HEAD_DIM:(h + 1) * HEAD_DIM] = o[r * tq:(r + 1) * tq, 0:HEAD_DIM].astype(o_ref.dtype)


def _select_mask(keys, thr, need, offset, tri):
    gt = jnp.where(keys > thr, 1.0, 0.0)
    eq = jnp.where(keys == thr, 1.0, 0.0)
    prefix = _dot(tri, eq.astype(BF16)) + offset
    return gt + jnp.where(prefix < need, eq, 0.0), offset + _col_sum(eq)


def _dsa_kernel(qi_ref, wT_ref, ki_ref, qa_ref, ka_ref, vT_ref, o_ref, keys_ref, bias_ref, planes_ref,
                *, tq, kc, n_keep):
    t0 = pl.program_id(1) * tq
    n_chunks = (t0 + tq + kc - 1) // kc
    wT = wT_ref[...] * (IDX_HEADS ** -0.5 * IDX_DIM ** -0.5)
    qi_stack = _stack_heads(qi_ref, range(IDX_HEADS))
    keypos = lax.broadcasted_iota(I32, (kc, tq), 0)
    qpos = t0 + lax.broadcasted_iota(I32, (kc, tq), 1)

    def score_chunk(c, carry):
        k0 = pl.multiple_of(c * kc, kc)
        sc = _dot_nt(ki_ref[0, pl.ds(k0, kc), :], qi_stack)
        score = jnp.zeros((kc, tq), F32)
        for h in range(IDX_HEADS):
            score = score + wT[h:h + 1, :] * jnp.maximum(sc[:, h * tq:(h + 1) * tq], 0.0)
        key = jnp.where(k0 + keypos <= qpos, _sort_key(score), INT_MIN)
        keys_ref[c] = key
        flipped = key ^ INT_MIN
        for lvl in range(4):
            byte = lax.shift_right_logical(flipped, 8 * lvl) & 0xFF
            planes_ref[lvl, c] = byte.astype(F32).astype(BF16)
        return carry

    lax.fori_loop(0, n_chunks, score_chunk, 0)

    one = jnp.ones((kc, tq), BF16)
    zero = jnp.zeros((kc, tq), BF16)
    dead = jnp.full((kc, tq), -1.0, BF16)

    def count_where(lvl, pick, also=None):
        def body(c, acc):
            p = planes_ref[lvl, c]
            if also is not None:
                also(c, p)
            return acc + _fold_rows_packed(jnp.where(pick(p), one, zero)).astype(F32)
        acc = lax.fori_loop(0, n_chunks, body, jnp.zeros((PACKED_ROWS, tq), F32))
        return jnp.sum(acc, axis=0, keepdims=True)

    above = jnp.zeros((1, tq), F32)
    thr_u = jnp.zeros((1, tq), I32)
    for lvl in (3, 2, 1, 0):
        def bit_step(it, t, lvl=lvl, above=above):
            cand = t + jnp.left_shift(jnp.int32(1), 7 - it).astype(F32)
            cnt = above + count_where(lvl, lambda p: p >= cand.astype(BF16))
            return jnp.where(cnt >= float(n_keep), cand, t)

        t = lax.fori_loop(0, 8, bit_step, jnp.zeros((1, tq), F32))
        tb = t.astype(BF16)

        def narrow(c, p, lvl=lvl, tb=tb):
            planes_ref[lvl - 1, c] = jnp.where(p == tb, planes_ref[lvl - 1, c], dead)

        above = above + count_where(lvl, lambda p: p > tb, narrow if lvl > 0 else None)
        thr_u = thr_u | jnp.left_shift(t.astype(I32), 8 * lvl)
    thr = thr_u ^ INT_MIN
    need = float(n_keep) - above
    tri = _tri_strict_lower(LANES, BF16)
    sub = lax.broadcasted_iota(I32, (LANES, tq), 0)
    qsub = t0 + lax.broadcasted_iota(I32, (LANES, tq), 1)

    def bias_chunk(c, offset):
        for j in range(kc // LANES):
            rows = slice(j * LANES, (j + 1) * LANES)
            sel, offset = _select_mask(keys_ref[c, rows, :], thr, need, offset, tri)
            causal = c * kc + j * LANES + sub <= qsub
            bias_ref[c, rows, :] = jnp.where(causal, (sel - 1.0) * 1e30, NEG)
        return offset

    lax.fori_loop(0, n_chunks, bias_chunk, jnp.zeros((1, tq), F32))

    accs = _flash_loop(qa_ref, ka_ref, vT_ref, lambda g, c: bias_ref[c], n_chunks, tq, kc)
    for heads, acc in accs:
        _store_heads(o_ref, _normalise(acc), heads, tq)


def _dsa(qi, wT, ki, qa, ka, vaT, bsz, seq, tq, kc):
    n_keep = min(DSA_TOPK_MAX, seq // 4)
    nq = seq // tq
    qblk = lambda w: pl.BlockSpec((1, tq, w), lambda b, i: (b, i, 0))
    full = lambda w: pl.BlockSpec((1, seq, w), lambda b, i: (b, 0, 0))
    return pl.pallas_call(
        functools.partial(_dsa_kernel, tq=tq, kc=kc, n_keep=n_keep),
        out_shape=jax.ShapeDtypeStruct((bsz, seq, A_HEADS * HEAD_DIM), BF16),
        grid=(bsz, nq),
        in_specs=[qblk(1024), pl.BlockSpec((IDX_HEADS, tq), lambda b, i: (0, b * nq + i)), full(LANES),
                  qblk(1024), full(LANES),
                  pl.BlockSpec((A_KV_HEADS, seq // kc, LANES, kc), lambda b, i: (0, b, 0, 0))],
        out_specs=qblk(512),
        scratch_shapes=[pltpu.VMEM((seq // kc, kc, tq), I32), pltpu.VMEM((seq // kc, kc, tq), F32),
                        pltpu.VMEM((4, seq // kc, kc, tq), BF16)],
        compiler_params=_cparams(("parallel", "parallel")),
        name="dsa",
    )(qi, wT, ki, qa, ka, vaT)


def _gelu_tanh(x):
    return 0.5 * x * (1.0 + jnp.tanh(math.sqrt(2.0 / math.pi) * (x + 0.044715 * (x * x * x))))


def _compress_kernel(k_ref, v_ref, pk_ref, pv_ref, w1k_ref, w2k_ref, w1v_ref, w2v_ref,
                     kc_ref, vcT_ref, *, n_rows):
    half = CMP_BLOCK // 2

    def one(x_ref, p_ref, w1_ref, w2_ref):
        outs = []
        for g in range(B_KV_HEADS):
            lo = jnp.zeros((n_rows, CMP_HIDDEN), F32)
            hi = jnp.zeros((n_rows, CMP_HIDDEN), F32)
            for l in range(half):
                xl = x_ref[0, pl.ds(l, n_rows, stride=CMP_STRIDE), :][:, g * HEAD_DIM:(g + 1) * HEAD_DIM]
                a = (xl + p_ref[l:l + 1, :]).astype(BF16)
                b = (xl + p_ref[half + l:half + l + 1, :]).astype(BF16)
                lo = lo + _dot(a, w1_ref[l * HEAD_DIM:(l + 1) * HEAD_DIM, :].astype(BF16))
                hi = hi + _dot(b, w1_ref[(half + l) * HEAD_DIM:(half + l + 1) * HEAD_DIM, :].astype(BF16))
            hid = lo + pltpu.roll(hi, n_rows - 1, axis=0)
            outs.append(_dot(_gelu_tanh(hid).astype(BF16), w2_ref[...].astype(BF16)))
        return outs

    k0, k1 = one(k_ref, pk_ref, w1k_ref, w2k_ref)
    kc_ref[0] = jnp.concatenate([k0, k1], axis=1).astype(kc_ref.dtype)
    for g, v in enumerate(one(v_ref, pv_ref, w1v_ref, w2v_ref)):
        vcT_ref[0, g] = jnp.concatenate([v, jnp.zeros_like(v)], axis=1).T.astype(vcT_ref.dtype)


def _compress(kcmp, vcmp, pos_k, pos_v, w1k, w2k, w1v, w2v):
    bsz, seq, width = kcmp.shape
    n_rows = seq // CMP_STRIDE
    xspec = pl.BlockSpec((1, seq, width), lambda b: (b, 0, 0))
    cst = lambda a: pl.BlockSpec(a.shape, lambda b: (0,) * a.ndim)
    return pl.pallas_call(
        functools.partial(_compress_kernel, n_rows=n_rows),
        out_shape=(jax.ShapeDtypeStruct((bsz, n_rows, LANES), BF16),
                   jax.ShapeDtypeStruct((bsz, B_KV_HEADS, LANES, n_rows), BF16)),
        grid=(bsz,),
        in_specs=[xspec, xspec, cst(pos_k), cst(pos_v), cst(w1k), cst(w2k), cst(w1v), cst(w2v)],
        out_specs=(pl.BlockSpec((1, n_rows, LANES), lambda b: (b, 0, 0)),
                   pl.BlockSpec((1, B_KV_HEADS, LANES, n_rows), lambda b: (b, 0, 0, 0))),
        compiler_params=_cparams(("parallel",)),
        name="compress",
    )(kcmp, vcmp, pos_k, pos_v, w1k, w2k, w1v, w2v)


def _split3(x):
    a = x.astype(BF16)
    r = x - a.astype(F32)
    b = r.astype(BF16)
    c = (r - b.astype(F32)).astype(BF16)
    return a, b, c


def _nsa_kernel(qraw_ref, qrot_ref, kc_ref, vcT_ref, ksel_ref, vselT_ref, kwin_ref, vwinT_ref,
                gT_ref, exp_ref, o_ref, bias_ref, *, tq, kc, seq, n_c):
    t0 = pl.program_id(1) * tq
    n_chunks = (t0 + tq + kc - 1) // kc
    n_s = seq // SEL_BLOCK
    n_pick = min(SEL_COUNT, n_s)
    wt = min(tq, WIN_TQ)
    span = WINDOW + wt
    gT = gT_ref[...]

    ridx = lax.broadcasted_iota(I32, (LANES, tq), 0)
    tl = t0 + lax.broadcasted_iota(I32, (LANES, tq), 1)
    valid_c = jnp.where((ridx * CMP_STRIDE + CMP_BLOCK - 1 <= tl) & (ridx < n_c), 1.0, 0.0)
    valid4 = jnp.concatenate([valid_c] * REP, axis=1) > 0.0
    js = lax.broadcasted_iota(I32, (LANES, LANES), 0) * SEL_BLOCK
    cs = lax.broadcasted_iota(I32, (LANES, LANES), 1) * CMP_STRIDE
    overlap_t = jnp.where((cs <= js + SEL_BLOCK - 1) & (cs + CMP_BLOCK - 1 >= js), 1.0, 0.0).astype(BF16)
    cur = tl // SEL_BLOCK
    forced = (ridx == 0) | ((cur - ridx >= 0) & (cur - ridx < SEL_LOCAL))
    blk_causal = ridx * SEL_BLOCK <= tl
    tri = _tri_strict_lower(LANES, BF16)
    keypos = lax.broadcasted_iota(I32, (kc, tq), 0)
    qpos = t0 + lax.broadcasted_iota(I32, (kc, tq), 1)

    o_cmp_g = []
    for g in range(B_KV_HEADS):
        heads = range(g * REP, (g + 1) * REP)
        s = jnp.where(valid4, _dot_nt(kc_ref[0], _stack_heads(qraw_ref, heads)), NEG)
        e = jnp.where(valid4, jnp.exp(s - _col_max(s)), 0.0)
        p = e * (1.0 / jnp.maximum(_col_sum(e), 1e-30))
        o_cmp_g.append(_dot(vcT_ref[0, g], p.astype(BF16)))
        p_sum = p[:, 0:tq]
        for r in range(1, REP):
            p_sum = p_sum + p[:, r * tq:(r + 1) * tq]
        pa, pb, pc = _split3(p_sum)
        imp = _dot(overlap_t, pa) + _dot(overlap_t, pb) + _dot(overlap_t, pc)
        imp = jnp.where(forced, FORCE, jnp.where(blk_causal, imp, NEG))
        keys = _sort_key(imp[0:n_s, :])
        count_ge = lambda cand, keys=keys: _col_sum(jnp.where(keys >= cand, 1.0, 0.0))
        thr = _kth_largest_key(count_ge, (1, tq), n_pick)
        need = float(n_pick) - _col_sum(jnp.where(keys > thr, 1.0, 0.0))
        blk_sel, _ = _select_mask(keys, thr, need, jnp.zeros((1, tq), F32), tri[0:n_s, 0:n_s])
        blk_sel = jnp.concatenate([blk_sel, jnp.zeros((LANES - n_s, tq), F32)], axis=0).astype(BF16)

        def bias_chunk(c, carry, blk_sel=blk_sel, g=g):
            tok_sel = _dot(exp_ref[c], blk_sel)
            bias_ref[g, c] = jnp.where(c * kc + keypos <= qpos, (tok_sel - 1.0) * 1e30, NEG)
            return carry

        lax.fori_loop(0, n_chunks, bias_chunk, 0)

    sel = _flash_loop(qrot_ref, ksel_ref, vselT_ref, lambda g, c: bias_ref[g, c], n_chunks, tq, kc)

    for (g, heads), (_, acc_sel) in zip(_head_chains(), sel):
        o_win_sub = []
        for sub in range(tq // wt):
            t_sub = t0 + sub * wt
            w0 = pl.multiple_of(jnp.clip(t_sub - WINDOW, 0, seq - span), wt)
            wdiff = ((t_sub + lax.broadcasted_iota(I32, (span, wt), 1))
                     - (w0 + lax.broadcasted_iota(I32, (span, wt), 0)))
            wbias = jnp.where((wdiff >= 0) & (wdiff < WINDOW), 0.0, NEG)
            q_sub = jnp.concatenate([qrot_ref[0, sub * wt:(sub + 1) * wt, h * LANES:(h + 1) * LANES]
                                     for h in heads], axis=0)
            sw = (_dot_nt(kwin_ref[0, pl.ds(w0, span), :], q_sub)
                  + jnp.concatenate([wbias] * HEADS_PER_CHAIN, axis=1))
            ew = jnp.exp(sw - _col_max(sw)).astype(BF16)
            wblk = w0 // LANES
            acc_win = _dot(vwinT_ref[g, wblk], ew[0:LANES, :])
            for j in range(1, span // LANES):
                acc_win = acc_win + _dot(vwinT_ref[g, wblk + j], ew[j * LANES:(j + 1) * LANES, :])
            o_win_sub.append(_normalise(acc_win))

        o_sel = _normalise(acc_sel)
        cols = []
        for r, h in enumerate(heads):
            col = slice(r * tq, (r + 1) * tq)
            cmp_col = slice((h % REP) * tq, (h % REP + 1) * tq)
            o_win = jnp.concatenate([o[:, r * wt:(r + 1) * wt] for o in o_win_sub], axis=1)
            cols.append(gT[3 * h:3 * h + 1, :] * o_cmp_g[g][:, cmp_col]
                        + gT[3 * h + 1:3 * h + 2, :] * o_sel[:, col]
                        + gT[3 * h + 2:3 * h + 3, :] * o_win)
        _store_heads(o_ref, jnp.concatenate(cols, axis=1), heads, tq)


def _nsa(qraw, qrot, kc_x, vcT, ksel, vselT, kwin, vwinT, gT, bsz, seq, tq, kc):
    n_c = (seq - CMP_BLOCK) // CMP_STRIDE + 1
    nq = seq // tq
    key_blk = (jnp.arange(seq, dtype=I32) // SEL_BLOCK).reshape(seq // kc, kc, 1)
    expand = (key_blk == jnp.arange(LANES, dtype=I32).reshape(1, 1, LANES)).astype(BF16)
    qblk = lambda w: pl.BlockSpec((1, tq, w), lambda b, i: (b, i, 0))
    full = lambda w: pl.BlockSpec((1, seq, w), lambda b, i: (b, 0, 0))
    g = B_KV_HEADS
    return pl.pallas_call(
        functools.partial(_nsa_kernel, tq=tq, kc=kc, seq=seq, n_c=n_c),
        out_shape=jax.ShapeDtypeStruct((bsz, seq, B_HEADS * HEAD_DIM), BF16),
        grid=(bsz, nq),
        in_specs=[qblk(1024), qblk(1024),
                  pl.BlockSpec((1,) + kc_x.shape[1:], lambda b, i: (b, 0, 0)),
                  pl.BlockSpec((1,) + vcT.shape[1:], lambda b, i: (b, 0, 0, 0)),
                  full(LANES), pl.BlockSpec((g, seq // kc, LANES, kc), lambda b, i: (0, b, 0, 0)),
                  full(LANES), pl.BlockSpec((g, seq // LANES, LANES, LANES), lambda b, i: (0, b, 0, 0)),
                  pl.BlockSpec((N_GATES, tq), lambda b, i: (0, b * nq + i)),
                  pl.BlockSpec(expand.shape, lambda b, i: (0, 0, 0))],
        out_specs=qblk(512),
        scratch_shapes=[pltpu.VMEM((g, seq // kc, kc, tq), F32)],
        compiler_params=_cparams(("parallel", "parallel")),
        name="nsa",
    )(qraw, qrot, kc_x, vcT, ksel, vselT, kwin, vwinT, gT, expand)


def _pack_pairs(x):
    n = x.shape[1] // 2
    lo = pltpu.bitcast(x[:, :n].astype(BF16).astype(F32), I32)
    hi = pltpu.bitcast(x[:, n:].astype(BF16).astype(F32), I32)
    return lax.shift_right_logical(lo, 16) | (hi & jnp.int32(-65536))


def _unpack_pairs(p):
    lo = pltpu.bitcast(lax.shift_left(p, 16), F32)
    hi = pltpu.bitcast(p & jnp.int32(-65536), F32)
    return jnp.concatenate([lo, hi], axis=1)


def _layer_norm(y, g, b):
    mu = jnp.mean(y, axis=1, keepdims=True)
    yc = y - mu
    var = jnp.mean(yc * yc, axis=1, keepdims=True)
    return yc * lax.rsqrt(var + LN_EPS) * g + b


def _out_proj_kernel(oa_ref, ob_ref, ga_ref, gb_ref, x_ref, mod_ref, wa_ref, wb_ref, wo_ref,
                     g1_ref, b1_ref, wrh_ref, wrl_ref, x1_ref, u2_ref, lg_ref):
    mod = mod_ref[0]
    merged = (ga_ref[...].astype(F32) * _dot(oa_ref[...], wa_ref[...])
              + gb_ref[...].astype(F32) * _dot(ob_ref[...], wb_ref[...]))
    mix = _dot(merged.astype(BF16), wo_ref[...])
    x1 = _layer_norm(DN_ALPHA * x_ref[...] + mod[2:3, :] * mix, g1_ref[...], b1_ref[...])
    x1_ref[...] = x1
    u2 = x1 * (1.0 + mod[4:5, :]) + mod[3:4, :]
    packed = _pack_pairs(u2)
    for j in range(PIECES):
        u2_ref[j] = packed[:, j * SC_ROW:(j + 1) * SC_ROW]
    uh = u2.astype(BF16)
    ul = (u2 - uh.astype(F32)).astype(BF16)
    lg_ref[...] = _dot_nt(wrh_ref[...], uh) + _dot_nt(wrh_ref[...], ul) + _dot_nt(wrl_ref[...], uh)


def _out_proj(oa, ob, ga, gb, x2, mod3, wa, wb, wo, g1, b1, wrh, wrl, seq, tm):
    t, d = x2.shape
    per_b = seq // tm
    row = lambda w: pl.BlockSpec((tm, w), lambda i: (i, 0))
    cst = lambda a: pl.BlockSpec(a.shape, lambda i: (0,) * a.ndim)
    return pl.pallas_call(
        _out_proj_kernel,
        out_shape=(jax.ShapeDtypeStruct((t, d), F32), jax.ShapeDtypeStruct((PIECES, t, SC_ROW), I32),
                   jax.ShapeDtypeStruct((N_EXPERTS, t), F32)),
        grid=(t // tm,),
        in_specs=[row(512), row(512), row(d), row(d), row(d),
                  pl.BlockSpec((1, 6, d), lambda i: (i // per_b, 0, 0)),
                  cst(wa), cst(wb), cst(wo), cst(g1), cst(b1), cst(wrh), cst(wrl)],
        out_specs=(row(d), pl.BlockSpec((PIECES, tm, SC_ROW), lambda i: (0, i, 0)),
                   pl.BlockSpec((N_EXPERTS, tm), lambda i: (0, i))),
        compiler_params=_cparams(("parallel",)),
        name="out_proj",
    )(oa, ob, ga, gb, x2, mod3, wa, wb, wo, g1, b1, wrh, wrl)


def _first_max(x, rows):
    m = jnp.max(x, axis=0, keepdims=True)
    idx = jnp.min(jnp.where(x == m, rows, 1e9), axis=0, keepdims=True)
    return m, idx


def _router_kernel(lg_ref, rb_ref, idx_ref, w_ref, pos_ref, cnt_ref, carry_ref, *, tm):
    @pl.when(pl.program_id(0) == 0)
    def _():
        carry_ref[...] = jnp.zeros_like(carry_ref)

    per_g = N_EXPERTS // N_GROUPS
    scores = _sigmoid(lg_ref[...])
    choice = scores + rb_ref[...][:, 0:1]
    rows = lax.broadcasted_iota(I32, (N_EXPERTS, tm), 0).astype(F32)
    rows_g = lax.broadcasted_iota(I32, (per_g, tm), 0).astype(F32)
    ninf = -jnp.inf

    gs = []
    for g in range(N_GROUPS):
        x = choice[g * per_g:(g + 1) * per_g, :]
        m1, i1 = _first_max(x, rows_g)
        m2 = jnp.max(jnp.where(rows_g == i1, ninf, x), axis=0, keepdims=True)
        gs.append(m1 + m2)
    gscore = jnp.concatenate(gs, axis=0)
    rows8 = lax.broadcasted_iota(I32, (N_GROUPS, tm), 0).astype(F32)
    keep = jnp.zeros((N_GROUPS, tm), F32)
    for _ in range(TOPK_GROUPS):
        _, gi = _first_max(gscore, rows8)
        hit = rows8 == gi
        keep = jnp.where(hit, 1.0, keep)
        gscore = jnp.where(hit, ninf, gscore)
    keep_full = jnp.concatenate(
        [jnp.broadcast_to(keep[g:g + 1, :], (per_g, tm)) for g in range(N_GROUPS)], axis=0)
    masked = jnp.where(keep_full > 0.0, choice, NEG)

    idxs, ws = [], []
    onehot = jnp.zeros((N_EXPERTS, tm), F32)
    for _ in range(TOP_K):
        _, ei = _first_max(masked, rows)
        hit = rows == ei
        idxs.append(ei)
        ws.append(jnp.sum(jnp.where(hit, scores, 0.0), axis=0, keepdims=True))
        masked = jnp.where(hit, ninf, masked)
        onehot = jnp.where(hit, 1.0, onehot)
    idx = jnp.concatenate(idxs, axis=0)
    w = jnp.concatenate(ws, axis=0)
    idx_ref[...] = idx.astype(I32)
    w = w / jnp.sum(w, axis=0, keepdims=True) * ROUTED_SCALE
    w_ref[...] = jnp.concatenate([w, jnp.zeros((LANES - TOP_K, tm), F32)], axis=0).T

    tri = _tri_strict_upper(tm, BF16)
    base = _dot(onehot.astype(BF16), tri) + carry_ref[...][:, 0:1]
    pos = [jnp.sum(jnp.where(rows == idxs[k], base, 0.0), axis=0, keepdims=True) for k in range(TOP_K)]
    pos_ref[...] = jnp.concatenate(pos, axis=0).astype(I32)
    carry = carry_ref[...] + jnp.sum(onehot, axis=1, keepdims=True)
    carry_ref[...] = carry
    cnt_ref[...] = carry


def _router(lg, router_bias, tm):
    e, t = lg.shape
    rb = jnp.broadcast_to(router_bias.reshape(e, 1).astype(F32), (e, LANES))
    tok = lambda r: pl.BlockSpec((r, tm), lambda i: (0, i))
    return pl.pallas_call(
        functools.partial(_router_kernel, tm=tm),
        out_shape=(jax.ShapeDtypeStruct((TOP_K, t), I32), jax.ShapeDtypeStruct((t, LANES), F32),
                   jax.ShapeDtypeStruct((TOP_K, t), I32), jax.ShapeDtypeStruct((e, LANES), F32)),
        grid=(t // tm,),
        in_specs=[tok(e), pl.BlockSpec((e, LANES), lambda i: (0, 0))],
        out_specs=(tok(TOP_K), pl.BlockSpec((tm, LANES), lambda i: (i, 0)), tok(TOP_K),
                   pl.BlockSpec((e, LANES), lambda i: (0, 0))),
        scratch_shapes=[pltpu.VMEM((e, LANES), F32)],
        compiler_params=_cparams(("arbitrary",)),
        name="router",
    )(lg, rb)


def _dest_kernel(idx_ref, pos_ref, st_ref, dest_ref, *, tm):
    rows = lax.broadcasted_iota(I32, (N_EXPERTS, tm), 0)
    starts = st_ref[...][:, 0:1]
    idx = idx_ref[...]
    out = []
    for k in range(TOP_K):
        out.append(jnp.sum(jnp.where(rows == idx[k:k + 1, :], starts, 0.0), axis=0, keepdims=True))
    dest_ref[...] = jnp.concatenate(out, axis=0).astype(I32) + pos_ref[...]


def _dest(idx, pos, starts, tm):
    k, t = idx.shape
    st = jnp.broadcast_to(starts.reshape(N_EXPERTS, 1).astype(F32), (N_EXPERTS, LANES))
    tok = pl.BlockSpec((k, tm), lambda i: (0, i))
    return pl.pallas_call(
        functools.partial(_dest_kernel, tm=tm),
        out_shape=jax.ShapeDtypeStruct((k, t), I32),
        grid=(t // tm,),
        in_specs=[tok, tok, pl.BlockSpec((N_EXPERTS, LANES), lambda i: (0, 0))],
        out_specs=tok,
        compiler_params=_cparams(("parallel",)),
        name="dest",
    )(idx, pos, st)


def _experts_kernel(blk_ref, used_ref, first_ref, slot_ref, next_ref, x_ref, wg_hbm, wu_hbm, wd_hbm, y_ref,
                    wg_buf, wu_buf, wd_buf, wg_bf, wu_bf, wd_bf, sems):
    b = pl.program_id(0)
    active = b < used_ref[0]

    def weight_copies(e, slot):
        return [pltpu.make_async_copy(hbm.at[e], buf.at[slot], sems.at[slot, i])
                for i, (hbm, buf) in enumerate(((wg_hbm, wg_buf), (wu_hbm, wu_buf), (wd_hbm, wd_buf)))]

    @pl.when(b == 0)
    def _():
        for cp in weight_copies(blk_ref[0], 0):
            cp.start()

    @pl.when(active & (first_ref[b] == 1))
    def _():
        slot = slot_ref[b]
        for cp in weight_copies(blk_ref[b], slot):
            cp.wait()
        nxt = next_ref[b]

        @pl.when(nxt >= 0)
        def _():
            for cp in weight_copies(nxt, 1 - slot):
                cp.start()

        wg_bf[...] = wg_buf[slot].astype(BF16)
        wu_bf[...] = wu_buf[slot].astype(BF16)
        wd_bf[...] = wd_buf[slot].astype(BF16)

    @pl.when(active)
    def _():
        x = jnp.concatenate([x_ref[j] for j in range(PIECES)], axis=1)
        x = _unpack_pairs(x).astype(BF16)
        a = _dot(x, wg_bf[...])
        u = _dot(x, wu_bf[...])
        h = (a * _sigmoid(a) * u).astype(BF16)
        y = _pack_pairs(_dot(h, wd_bf[...]))
        for j in range(PIECES):
            y_ref[j] = y[:, j * SC_ROW:(j + 1) * SC_ROW]


def _experts(xs, plan, wg, wu, wd, bm):
    _, cap, _ = xs.shape
    n_blocks = cap // bm
    d, f = wg.shape[1], wg.shape[2]
    rows = pl.BlockSpec((PIECES, bm, SC_ROW), lambda b, blk, used, *_: (0, jnp.minimum(b, used[0] - 1), 0))
    hbm = pl.BlockSpec(memory_space=pl.ANY)
    return pl.pallas_call(
        _experts_kernel,
        out_shape=jax.ShapeDtypeStruct(xs.shape, I32),
        grid_spec=pltpu.PrefetchScalarGridSpec(
            num_scalar_prefetch=5,
            grid=(n_blocks,),
            in_specs=[rows, hbm, hbm, hbm],
            out_specs=rows,
            scratch_shapes=[pltpu.VMEM((2, d, f), F32), pltpu.VMEM((2, d, f), F32), pltpu.VMEM((2, f, d), F32),
                            pltpu.VMEM((d, f), BF16), pltpu.VMEM((d, f), BF16), pltpu.VMEM((f, d), BF16),
                            pltpu.SemaphoreType.DMA((2, 3))]),
        compiler_params=_cparams(("arbitrary",)),
        name="experts",
    )(plan["blk_e"], plan["n_used"], plan["first"], plan["slot"], plan["next_e"], xs, wg, wu, wd)


def _final_kernel(x1_ref, u2_ref, yg_ref, w_ref, mod_ref, sg_ref, su_ref, sd_ref, g2_ref, b2_ref, o_ref):
    mod = mod_ref[0]
    w = w_ref[...]
    rows = lambda ref, *lead: jnp.concatenate([ref[(j,) + lead] for j in range(PIECES)], axis=1)
    routed = w[:, 0:1] * _unpack_pairs(rows(yg_ref, 0))
    for k in range(1, TOP_K):
        routed = routed + w[:, k:k + 1] * _unpack_pairs(rows(yg_ref, k))
    u = _unpack_pairs(rows(u2_ref)).astype(BF16)
    a = _dot(u, sg_ref[...])
    b = _dot(u, su_ref[...])
    shared = _dot((a * _sigmoid(a) * b).astype(BF16), sd_ref[...])
    y = DN_ALPHA * x1_ref[...] + mod[5:6, :] * (routed + shared)
    o_ref[...] = _layer_norm(y, g2_ref[...], b2_ref[...])


def _final(x1, u2p, yg, wtok, mod3, sg, su, sd, g2, b2, seq, tm):
    t, d = x1.shape
    per_b = seq // tm
    row = lambda w: pl.BlockSpec((tm, w), lambda i: (i, 0))
    cst = lambda a: pl.BlockSpec(a.shape, lambda i: (0,) * a.ndim)
    return pl.pallas_call(
        _final_kernel,
        out_shape=jax.ShapeDtypeStruct((t, d), F32),
        grid=(t // tm,),
        in_specs=[row(d), pl.BlockSpec((PIECES, tm, SC_ROW), lambda i: (0, i, 0)),
                  pl.BlockSpec((PIECES, TOP_K, tm, SC_ROW), lambda i: (0, 0, i, 0)),
                  row(LANES), pl.BlockSpec((1, 6, d), lambda i: (i // per_b, 0, 0)),
                  cst(sg), cst(su), cst(sd), cst(g2), cst(b2)],
        out_specs=row(d),
        compiler_params=_cparams(("parallel",)),
        name="final",
    )(x1, u2p, yg, wtok, mod3, sg, su, sd, g2, b2)


def _sc_mesh():
    return plsc.VectorSubcoreMesh(core_axis_name="core", subcore_axis_name="subcore")


def _sc_scatter_rows(src, idx, n_out):
    n_copies, n = idx.shape

    @functools.partial(pl.kernel, out_type=jax.ShapeDtypeStruct((n_out, SC_ROW), src.dtype),
                       mesh=_sc_mesh(), scratch_types=[])
    def k(x_hbm, i_hbm, o_hbm):
        def body(x_vmem, i_vmem):
            for c in range(n_copies):
                pltpu.sync_copy(x_vmem, o_hbm.at[i_vmem.at[c]])

        pltpu.emit_pipeline(
            body, grid=(n // SC_WINDOW,),
            in_specs=[pl.BlockSpec((SC_WINDOW, SC_ROW), lambda i: (i, 0)),
                      pl.BlockSpec((n_copies, SC_WINDOW), lambda i: (0, i))],
            out_specs=[],
            core_axis_name=("core", "subcore"),
            dimension_semantics=(pltpu.PARALLEL,),
        )(x_hbm, i_hbm)

    return k(src, idx)


def _sc_gather_rows(src, idx):
    n_idx = idx.shape[0]

    @functools.partial(pl.kernel, out_type=jax.ShapeDtypeStruct((n_idx, SC_ROW), src.dtype),
                       mesh=_sc_mesh(), scratch_types=[])
    def k(x_hbm, i_hbm, o_hbm):
        def body(i_vmem, o_vmem):
            pltpu.sync_copy(x_hbm.at[i_vmem.at[0]], o_vmem)

        pltpu.emit_pipeline(
            body, grid=(n_idx // SC_WINDOW,),
            in_specs=[pl.BlockSpec((1, SC_WINDOW), lambda i: (0, i))],
            out_specs=[pl.BlockSpec((SC_WINDOW, SC_ROW), lambda i: (i, 0))],
            core_axis_name=("core", "subcore"),
            dimension_semantics=(pltpu.PARALLEL,),
        )(i_hbm, o_hbm)

    return k(src, idx.reshape(1, n_idx))


def _moe_plan(counts, n_tok):
    bm = BM_EXPERT
    padded = (counts + bm - 1) // bm * bm
    p_ends = jnp.cumsum(padded)
    starts = p_ends - padded
    n_blocks = n_tok * TOP_K // bm + N_EXPERTS
    blk = jnp.arange(n_blocks, dtype=I32)
    blk_e = jnp.minimum(jnp.sum(p_ends[None, :] <= (blk * bm)[:, None], axis=1), N_EXPERTS - 1).astype(I32)
    n_used = (p_ends[-1] // bm).astype(I32)
    prev = jnp.concatenate([jnp.full((1,), -1, I32), blk_e[:-1]])
    first = ((blk_e != prev) & (blk < n_used)).astype(I32)
    slot = (jnp.cumsum(first) - 1) % 2
    first_pos = jnp.where(first == 1, blk, n_blocks)
    next_first = lax.cummin(jnp.concatenate([first_pos[1:], jnp.full((1,), n_blocks, I32)]), reverse=True)
    next_e = jnp.where(next_first < n_blocks, blk_e[jnp.minimum(next_first, n_blocks - 1)], -1)
    plan = dict(blk_e=blk_e, n_used=n_used.reshape(1), first=first, slot=slot.astype(I32),
                next_e=next_e.astype(I32))
    return starts, plan, n_blocks


def _layer(x, mod, positions, w_in, w_br_a, w_br_b, w_out, cmp_pos_k, cmp_pos_v, cmp_k_w1, cmp_k_w2,
           cmp_v_w1, cmp_v_w2, ln1_g, ln1_b, w_router, router_bias, w_exp_gate, w_exp_up, w_exp_down,
           w_sh_gate, w_sh_up, w_sh_down, ln2_g, ln2_b):
    bsz, seq, d = x.shape
    t = bsz * seq
    assert seq // CMP_STRIDE == LANES and seq % TQ_ATTN == 0
    assert seq % TM_PROJ == 0 and KC_ATTN == TM_PROJ
    x2 = x.reshape(t, d)
    mod3 = mod.reshape(bsz, 6, d)

    w_pack, w_small = _pack_w_in(w_in)
    z = _in_proj(x2, mod3, w_pack, w_small, _rope_table(positions), seq, TM_PROJ)
    per_b = lambda name: z[name].reshape(bsz, seq, z[name].shape[1])

    o_a = _dsa(per_b("qi"), z["wT"], per_b("ki"), per_b("qa"), per_b("ka"), z["vaT"],
               bsz, seq, TQ_ATTN, KC_ATTN)

    kc_x, vcT = _compress(per_b("kcmp"), per_b("vcmp"),
                          cmp_pos_k, cmp_pos_v, cmp_k_w1, cmp_k_w2, cmp_v_w1, cmp_v_w2)
    o_b = _nsa(per_b("qbraw"), per_b("qbrot"), kc_x, vcT, per_b("ksel"), z["vselT"], per_b("kwin"),
               z["vwinT"], z["gT"], bsz, seq, TQ_ATTN, KC_ATTN)

    wr_hi = w_router.T.astype(BF16)
    wr_lo = (w_router.T - wr_hi.astype(F32)).astype(BF16)
    x1, u2p, logits = _out_proj(
        o_a.reshape(t, -1), o_b.reshape(t, -1), z["ga"], z["gb"], x2, mod3,
        w_br_a.astype(BF16), w_br_b.astype(BF16), w_out.astype(BF16),
        ln1_g.reshape(1, d), ln1_b.reshape(1, d), wr_hi, wr_lo, seq, TM_PROJ)

    idx, wtok, pos, counts = _router(logits, router_bias, TM_ROUTE)
    starts, plan, n_blocks = _moe_plan(counts[:, 0].astype(I32), t)
    dest = _dest(idx, pos, starts, TM_ROUTE)

    cap = n_blocks * BM_EXPERT
    dest_p = dest[None] + (jnp.arange(PIECES, dtype=I32) * cap).reshape(PIECES, 1, 1)
    xs = _sc_scatter_rows(u2p.reshape(PIECES * t, SC_ROW),
                          jnp.swapaxes(dest_p, 0, 1).reshape(TOP_K, PIECES * t), cap * PIECES)
    ys = _experts(xs.reshape(PIECES, cap, SC_ROW), plan, w_exp_gate, w_exp_up, w_exp_down, BM_EXPERT)
    yg = _sc_gather_rows(ys.reshape(cap * PIECES, SC_ROW), dest_p.reshape(-1)
                         ).reshape(PIECES, TOP_K, t, SC_ROW)

    return _final(x1, u2p, yg, wtok, mod3, w_sh_gate.astype(BF16), w_sh_up.astype(BF16),
                  w_sh_down.astype(BF16), ln2_g.reshape(1, d), ln2_b.reshape(1, d), seq, TM_PROJ
                  ).reshape(bsz, seq, d)


def kernel(x, c, positions, w_ada, b_ada, w_in, w_br_a, w_br_b, w_out, cmp_pos_k, cmp_pos_v, cmp_k_w1,
           cmp_k_w2, cmp_v_w1, cmp_v_w2, ln1_g, ln1_b, w_router, router_bias, w_exp_gate, w_exp_up,
           w_exp_down, w_sh_gate, w_sh_up, w_sh_down, ln2_g, ln2_b):
    for l in range(w_ada.shape[0]):
        mod = _mod(c, w_ada[l], b_ada[l])
        x = _layer(x, mod, positions, w_in[l], w_br_a[l], w_br_b[l], w_out[l], cmp_pos_k[l], cmp_pos_v[l],
                   cmp_k_w1[l], cmp_k_w2[l], cmp_v_w1[l], cmp_v_w2[l], ln1_g[l], ln1_b[l], w_router[l],
                   router_bias[l], w_exp_gate[l], w_exp_up[l], w_exp_down[l], w_sh_gate[l], w_sh_up[l],
                   w_sh_down[l], ln2_g[l], ln2_b[l])
    return x
```

```python
import functools
import math

import jax
import jax.numpy as jnp
from jax import lax
from jax.experimental import pallas as pl
from jax.experimental.pallas import tpu as pltpu
from jax.experimental.pallas import tpu_sc as plsc

F32 = jnp.float32
BF16 = jnp.bfloat16
I32 = jnp.int32

D_MODEL = 1024
HEAD_DIM = 64
ROPE_THETA = 500000.0
ROPE_FRACTION = 4
A_HEADS = 8
A_KV_HEADS = 2
IDX_HEADS = 8
IDX_DIM = 32
DSA_TOPK_MAX = 256
B_HEADS = 8
B_KV_HEADS = 2
REP = 4
CMP_BLOCK = 32
CMP_STRIDE = 16
CMP_HIDDEN = 256
SEL_BLOCK = 64
SEL_COUNT = 16
SEL_LOCAL = 2
WINDOW = 512
N_EXPERTS = 256
TOP_K = 8
N_GROUPS = 8
TOPK_GROUPS = 4
ROUTED_SCALE = 2.5
DEPTH = 1
DN_ALPHA = (2 * DEPTH) ** 0.25
LN_EPS = 1e-5
NEG = -1e30
FORCE = 1e9
INT_MIN = -2147483648
N_GATES = 3 * B_HEADS

LANES = 128
SUBLANES = 8
VMEM_LIMIT = 56 * 1024 * 1024
SC_WINDOW = 128
SC_ROW = 256
PIECES = (D_MODEL // 2) // SC_ROW

TM_PROJ = 512
TQ_ATTN = 512
WIN_TQ = 256
KC_ATTN = 512
TM_ROUTE = 512
BM_EXPERT = 512

_IN_SLOTS = (("qa", A_HEADS * HEAD_DIM), ("ka", A_KV_HEADS * HEAD_DIM), ("va", A_KV_HEADS * HEAD_DIM),
             ("qi", IDX_HEADS * IDX_DIM), ("ki", IDX_DIM), ("wi", IDX_HEADS), ("qb", B_HEADS * HEAD_DIM),
             ("kcmp", B_KV_HEADS * HEAD_DIM), ("vcmp", B_KV_HEADS * HEAD_DIM), ("ksel", B_KV_HEADS * HEAD_DIM),
             ("vsel", B_KV_HEADS * HEAD_DIM), ("kwin", B_KV_HEADS * HEAD_DIM), ("vwin", B_KV_HEADS * HEAD_DIM),
             ("gnsa", N_GATES), ("ga", D_MODEL), ("gb", D_MODEL))


def _col_ranges(slots, align):
    out, start = {}, 0
    for name, width in slots:
        out[name] = (start, start + -(-width // align) * align)
        start = out[name][1]
    return out


_IN_COLS = _col_ranges(_IN_SLOTS, 1)
_PACK_COLS = _col_ranges([s for s in _IN_SLOTS if s[0] not in ("wi", "gnsa")], LANES)

NT_DIMS = (((1,), (1,)), ((), ()))


def _cparams(sem):
    return pltpu.CompilerParams(dimension_semantics=sem, vmem_limit_bytes=VMEM_LIMIT)


def _sigmoid(x):
    return 1.0 / (1.0 + jnp.exp(-x))


def _dot(a, b):
    return jnp.dot(a, b, preferred_element_type=F32)


def _dot_nt(a, b):
    return lax.dot_general(a, b, NT_DIMS, preferred_element_type=F32)


def _sort_key(x):
    x = jnp.where(x == 0.0, 0.0, x)
    bits = pltpu.bitcast(x, I32)
    return jnp.where(bits < 0, bits ^ 0x7FFFFFFF, bits)


def _kth_largest_key(count_ge, shape, k):
    kf = float(k)
    t0 = jnp.where(count_ge(jnp.zeros(shape, I32)) >= kf, 0, INT_MIN).astype(I32)

    def body(it, t):
        cand = t + jnp.left_shift(jnp.int32(1), 30 - it)
        return jnp.where(count_ge(cand) >= kf, cand, t)

    return lax.fori_loop(0, 31, body, t0)


def _tri_strict_lower(n, dtype):
    r = lax.broadcasted_iota(I32, (n, n), 0)
    c = lax.broadcasted_iota(I32, (n, n), 1)
    return jnp.where(c < r, 1.0, 0.0).astype(dtype)


def _tri_strict_upper(n, dtype):
    r = lax.broadcasted_iota(I32, (n, n), 0)
    c = lax.broadcasted_iota(I32, (n, n), 1)
    return jnp.where(r < c, 1.0, 0.0).astype(dtype)


def _mod_kernel(c_ref, w_ref, b_ref, o_ref):
    c = c_ref[...]
    cond = (c * _sigmoid(c)).astype(BF16)
    o_ref[...] = _dot(cond, w_ref[...].astype(BF16)) + b_ref[...]


def _mod(c, w_ada, b_ada):
    bsz, d = c.shape
    n = w_ada.shape[1]
    tn = 1024
    return pl.pallas_call(
        _mod_kernel,
        out_shape=jax.ShapeDtypeStruct((bsz, n), F32),
        grid=(n // tn,),
        in_specs=[pl.BlockSpec((bsz, d), lambda j: (0, 0)),
                  pl.BlockSpec((d, tn), lambda j: (0, j)),
                  pl.BlockSpec((1, tn), lambda j: (0, j))],
        out_specs=pl.BlockSpec((bsz, tn), lambda j: (0, j)),
        compiler_params=_cparams(("parallel",)),
        name="mod",
    )(c, w_ada, b_ada.reshape(1, n))


def _rope(z, c_tab, s_tab, period, half):
    w = z.shape[1]
    reps = w // LANES
    c = jnp.concatenate([c_tab] * reps, axis=1) if reps > 1 else c_tab
    s = jnp.concatenate([s_tab] * reps, axis=1) if reps > 1 else s_tab
    lane = lax.broadcasted_iota(I32, z.shape, 1)
    first = (lane & (period - 1)) < half
    partner = jnp.where(first, pltpu.roll(z, w - half, axis=1), pltpu.roll(z, half, axis=1))
    return z * c + partner * s


def _in_proj_kernel(x_ref, mod_ref, w_ref, wsm_ref, rope_ref,
                    qa_ref, ka_ref, vaT_ref, qi_ref, ki_ref, wT_ref, qbraw_ref, qbrot_ref,
                    kcmp_ref, vcmp_ref, ksel_ref, vselT_ref, kwin_ref, vwinT_ref, gT_ref, ga_ref, gb_ref):
    mod = mod_ref[0]
    u = (x_ref[...] * (1.0 + mod[1:2, :]) + mod[0:1, :]).astype(BF16)
    tm = u.shape[0]
    c64, s64 = _rope_patterns(rope_ref[...], HEAD_DIM)
    c32, s32 = _rope_patterns(rope_ref[...], IDX_DIM)
    scale = HEAD_DIM ** -0.5
    lane = lax.broadcasted_iota(I32, (tm, LANES), 1)
    low = lane < HEAD_DIM

    def proj(name):
        a, b = _PACK_COLS[name]
        return _dot(u, w_ref[:, a:b])

    rope64 = lambda z: _rope(z, c64, s64, HEAD_DIM, HEAD_DIM // ROPE_FRACTION // 2)
    rope32 = lambda z: _rope(z, c32, s32, IDX_DIM, IDX_DIM // ROPE_FRACTION // 2)

    def head_slots64(z):
        out = []
        for h in range(A_HEADS):
            pair = z[:, (h // 2) * LANES:(h // 2 + 1) * LANES]
            g = h // REP
            src = pair if h % 2 == g else pltpu.roll(pair, HEAD_DIM, axis=1)
            out.append(jnp.where(low, src, 0.0) if g == 0 else jnp.where(low, 0.0, src))
        return jnp.concatenate(out, axis=1).astype(BF16)

    def head_slots32(z):
        per = LANES // IDX_DIM
        out = []
        for h in range(IDX_HEADS):
            col = z[:, (h // per) * LANES:(h // per + 1) * LANES]
            shift = IDX_DIM * (h % per)
            src = col if shift == 0 else pltpu.roll(col, LANES - shift, axis=1)
            out.append(jnp.where(lane < IDX_DIM, src, 0.0))
        return jnp.concatenate(out, axis=1).astype(BF16)

    def store_vt(ref, z, chunk):
        zt = z.T
        ones = jnp.ones((HEAD_DIM, chunk), F32)
        for g in range(A_KV_HEADS):
            for j in range(tm // chunk):
                blk = zt[g * HEAD_DIM:(g + 1) * HEAD_DIM, j * chunk:(j + 1) * chunk]
                ref[g, j] = jnp.concatenate([blk, ones], axis=0).astype(BF16)

    qa_ref[...] = head_slots64(rope64(proj("qa")) * scale)
    ka_ref[...] = rope64(proj("ka")).astype(BF16)
    store_vt(vaT_ref, proj("va"), tm)
    qi_ref[...] = head_slots32(rope32(proj("qi")))
    ki_ref[...] = rope32(proj("ki")).astype(BF16)
    qb = proj("qb")
    qbraw_ref[...] = head_slots64(qb * scale)
    qbrot_ref[...] = head_slots64(rope64(qb) * scale)
    kcmp_ref[...] = proj("kcmp")
    vcmp_ref[...] = proj("vcmp")
    ksel_ref[...] = rope64(proj("ksel")).astype(BF16)
    store_vt(vselT_ref, proj("vsel"), tm)
    kwin_ref[...] = rope64(proj("kwin")).astype(BF16)
    store_vt(vwinT_ref, proj("vwin"), LANES)
    ga_ref[...] = _sigmoid(proj("ga")).astype(BF16)
    gb_ref[...] = _sigmoid(proj("gb")).astype(BF16)
    small = _dot_nt(wsm_ref[...], u)
    wT_ref[...] = small[0:IDX_HEADS, :]
    gT_ref[...] = _sigmoid(small[IDX_HEADS:IDX_HEADS + N_GATES, :])


def _pack_w_in(w_in):
    d = w_in.shape[0]
    col = lambda name: w_in[:, _IN_COLS[name][0]:_IN_COLS[name][1]]
    parts = []
    for name, (a, b) in _PACK_COLS.items():
        c = col(name)
        parts.append(jnp.pad(c, ((0, 0), (0, b - a - c.shape[1]))))
    w_small = jnp.concatenate([col("wi"), col("gnsa")], axis=1).T
    return jnp.concatenate(parts, axis=1).astype(BF16), w_small.astype(BF16)


def _rope_halves(dim):
    return dim // ROPE_FRACTION // 2


_ROPE_LANES = {HEAD_DIM: (0, HEAD_DIM), IDX_DIM: (IDX_DIM, 3 * IDX_DIM)}


def _rope_table(positions):
    pos = positions.astype(F32).reshape(-1, 1)
    tab = jnp.zeros((pos.shape[0], LANES), F32)
    for dim in (HEAD_DIM, IDX_DIM):
        half = _rope_halves(dim)
        inv = ROPE_THETA ** (-(jnp.arange(half, dtype=F32) * 2.0) / (2 * half))
        cos, sin = jnp.cos(pos * inv), jnp.sin(pos * inv)
        c_at, s_at = _ROPE_LANES[dim]
        tab = tab.at[:, c_at:c_at + 2 * half].set(jnp.concatenate([cos, cos], axis=1))
        tab = tab.at[:, s_at:s_at + 2 * half].set(jnp.concatenate([-sin, sin], axis=1))
    return tab


def _rope_patterns(tab, dim):
    half = _rope_halves(dim)
    lane = lax.broadcasted_iota(I32, tab.shape, 1)

    def spread(at):
        rep = jnp.where((lane >= at) & (lane < at + 2 * half), tab, 0.0)
        period = dim
        while period < LANES:
            rep = rep + pltpu.roll(rep, period, axis=1)
            period *= 2
        return rep

    c_at, s_at = _ROPE_LANES[dim]
    rotated = (lane & (dim - 1)) < 2 * half
    return jnp.where(rotated, spread(c_at), 1.0), spread(s_at)


def _in_proj(x2, mod3, w_pack, w_small, rope_tab, seq, tm):
    t, d = x2.shape
    n = w_pack.shape[1]
    per_b = seq // tm
    g = A_KV_HEADS
    row = lambda w: pl.BlockSpec((tm, w), lambda i: (i, 0))
    tok = lambda r: pl.BlockSpec((r, tm), lambda i: (0, i))
    vt_chunk = pl.BlockSpec((g, 1, LANES, tm), lambda i: (0, i, 0, 0))
    vt_lane = pl.BlockSpec((g, tm // LANES, LANES, LANES), lambda i: (0, i, 0, 0))
    sds = jax.ShapeDtypeStruct
    vt_chunk_shape = sds((g, t // tm, LANES, tm), BF16)
    outs = (("qa", sds((t, 1024), BF16), row(1024)), ("ka", sds((t, LANES), BF16), row(LANES)),
            ("vaT", vt_chunk_shape, vt_chunk), ("qi", sds((t, 1024), BF16), row(1024)),
            ("ki", sds((t, LANES), BF16), row(LANES)), ("wT", sds((IDX_HEADS, t), F32), tok(IDX_HEADS)),
            ("qbraw", sds((t, 1024), BF16), row(1024)), ("qbrot", sds((t, 1024), BF16), row(1024)),
            ("kcmp", sds((t, LANES), F32), row(LANES)), ("vcmp", sds((t, LANES), F32), row(LANES)),
            ("ksel", sds((t, LANES), BF16), row(LANES)), ("vselT", vt_chunk_shape, vt_chunk),
            ("kwin", sds((t, LANES), BF16), row(LANES)),
            ("vwinT", sds((g, t // LANES, LANES, LANES), BF16), vt_lane),
            ("gT", sds((N_GATES, t), F32), tok(N_GATES)),
            ("ga", sds((t, d), BF16), row(d)), ("gb", sds((t, d), BF16), row(d)))
    res = pl.pallas_call(
        _in_proj_kernel,
        out_shape=tuple(o[1] for o in outs),
        grid=(t // tm,),
        in_specs=[row(d),
                  pl.BlockSpec((1, 6, d), lambda i: (i // per_b, 0, 0)),
                  pl.BlockSpec((d, n), lambda i: (0, 0)),
                  pl.BlockSpec(w_small.shape, lambda i: (0, 0)),
                  row(LANES)],
        out_specs=tuple(o[2] for o in outs),
        compiler_params=_cparams(("parallel",)),
        name="in_proj",
    )(x2, mod3, w_pack, w_small, rope_tab)
    return {o[0]: r for o, r in zip(outs, res)}


def _fold_rows(x, op):
    n = x.shape[0]
    while n % (2 * SUBLANES) == 0:
        n //= 2
        x = op(x[:n], x[n:])
    slabs = [x[i * SUBLANES:(i + 1) * SUBLANES] for i in range(n // SUBLANES)]
    while len(slabs) > 1:
        nxt = [op(slabs[i], slabs[i + 1]) for i in range(0, len(slabs) - 1, 2)]
        slabs = nxt + ([slabs[-1]] if len(slabs) % 2 else [])
    return slabs[0]


def _col_max(x):
    return jnp.max(_fold_rows(x, jnp.maximum), axis=0, keepdims=True)


def _col_sum(x):
    return jnp.sum(_fold_rows(x, jnp.add), axis=0, keepdims=True)


PACKED_ROWS = 16


def _fold_rows_packed(x):
    n = x.shape[0]
    assert n % PACKED_ROWS == 0 and n // PACKED_ROWS <= 256
    while n > PACKED_ROWS:
        n //= 2
        x = x[:n] + x[n:]
    return x


def _stack_heads(q_ref, heads):
    return jnp.concatenate([q_ref[0, :, h * LANES:(h + 1) * LANES] for h in heads], axis=0)


def _flash_step(k, q_stack, v_t, bias4, m, acc):
    s = _dot_nt(k, q_stack) + bias4
    m_new = jnp.maximum(m, _col_max(s))
    e = jnp.exp(s - m_new).astype(BF16)
    return m_new, acc * jnp.exp(m - m_new) + _dot(v_t, e)


HEADS_PER_CHAIN = REP


def _head_chains():
    return [(h // REP, tuple(range(h, h + HEADS_PER_CHAIN))) for h in range(0, A_HEADS, HEADS_PER_CHAIN)]


def _flash_loop(q_ref, k_ref, vT_ref, bias_of, n_chunks, tq, kc):
    chains = _head_chains()
    q_stacks = [_stack_heads(q_ref, heads) for _, heads in chains]
    width = HEADS_PER_CHAIN * tq

    def body(c, carry):
        k = k_ref[0, pl.ds(pl.multiple_of(c * kc, kc), kc), :]
        out = []
        for (g, _), q_stack, (m, acc) in zip(chains, q_stacks, carry):
            bias = jnp.concatenate([bias_of(g, c)] * HEADS_PER_CHAIN, axis=1)
            out.append(_flash_step(k, q_stack, vT_ref[g, c], bias, m, acc))
        return tuple(out)

    init = tuple((jnp.full((1, width), NEG, F32), jnp.zeros((LANES, width), F32)) for _ in chains)
    res = lax.fori_loop(0, n_chunks, body, init)
    return [(heads, acc) for (_, heads), (_, acc) in zip(chains, res)]


def _normalise(acc):
    return acc * (1.0 / jnp.maximum(acc[HEAD_DIM:HEAD_DIM + 1, :], 1e-30))


def _store_heads(o_ref, o_t, heads, tq):
    o = o_t.T
    for r, h in enumerate(heads):
        o_ref[0, :, h * HEAD_DIM:(h + 1) * HEAD_DIM] = o[r * tq:(r + 1) * tq, 0:HEAD_DIM].astype(o_ref.dtype)


def _select_mask(keys, thr, need, offset, tri):
    gt = jnp.where(keys > thr, 1.0, 0.0)
    eq = jnp.where(keys == thr, 1.0, 0.0)
    prefix = _dot(tri, eq.astype(BF16)) + offset
    return gt + jnp.where(prefix < need, eq, 0.0), offset + _col_sum(eq)


def _dsa_kernel(qi_ref, wT_ref, ki_ref, qa_ref, ka_ref, vT_ref, o_ref, keys_ref, bias_ref, planes_ref,
                *, tq, kc, n_keep):
    t0 = pl.program_id(1) * tq
    n_chunks = (t0 + tq + kc - 1) // kc
    wT = wT_ref[...] * (IDX_HEADS ** -0.5 * IDX_DIM ** -0.5)
    qi_stack = _stack_heads(qi_ref, range(IDX_HEADS))
    keypos = lax.broadcasted_iota(I32, (kc, tq), 0)
    qpos = t0 + lax.broadcasted_iota(I32, (kc, tq), 1)

    def score_chunk(c, carry):
        k0 = pl.multiple_of(c * kc, kc)
        sc = _dot_nt(ki_ref[0, pl.ds(k0, kc), :], qi_stack)
        score = jnp.zeros((kc, tq), F32)
        for h in range(IDX_HEADS):
            score = score + wT[h:h + 1, :] * jnp.maximum(sc[:, h * tq:(h + 1) * tq], 0.0)
        key = jnp.where(k0 + keypos <= qpos, _sort_key(score), INT_MIN)
        keys_ref[c] = key
        flipped = key ^ INT_MIN
        for lvl in range(4):
            byte = lax.shift_right_logical(flipped, 8 * lvl) & 0xFF
            planes_ref[lvl, c] = byte.astype(F32).astype(BF16)
        return carry

    lax.fori_loop(0, n_chunks, score_chunk, 0)

    one = jnp.ones((kc, tq), BF16)
    zero = jnp.zeros((kc, tq), BF16)
    dead = jnp.full((kc, tq), -1.0, BF16)

    def count_where(lvl, pick, also=None):
        def body(c, acc):
            p = planes_ref[lvl, c]
            if also is not None:
                also(c, p)
            return acc + _fold_rows_packed(jnp.where(pick(p), one, zero)).astype(F32)
        acc = lax.fori_loop(0, n_chunks, body, jnp.zeros((PACKED_ROWS, tq), F32))
        return jnp.sum(acc, axis=0, keepdims=True)

    above = jnp.zeros((1, tq), F32)
    thr_u = jnp.zeros((1, tq), I32)
    for lvl in (3, 2, 1, 0):
        def bit_step(it, t, lvl=lvl, above=above):
            cand = t + jnp.left_shift(jnp.int32(1), 7 - it).astype(F32)
            cnt = above + count_where(lvl, lambda p: p >= cand.astype(BF16))
            return jnp.where(cnt >= float(n_keep), cand, t)

        t = lax.fori_loop(0, 8, bit_step, jnp.zeros((1, tq), F32))
        tb = t.astype(BF16)

        def narrow(c, p, lvl=lvl, tb=tb):
            planes_ref[lvl - 1, c] = jnp.where(p == tb, planes_ref[lvl - 1, c], dead)

        above = above + count_where(lvl, lambda p: p > tb, narrow if lvl > 0 else None)
        thr_u = thr_u | jnp.left_shift(t.astype(I32), 8 * lvl)
    thr = thr_u ^ INT_MIN
    need = float(n_keep) - above
    tri = _tri_strict_lower(LANES, BF16)
    sub = lax.broadcasted_iota(I32, (LANES, tq), 0)
    qsub = t0 + lax.broadcasted_iota(I32, (LANES, tq), 1)

    def bias_chunk(c, offset):
        for j in range(kc // LANES):
            rows = slice(j * LANES, (j + 1) * LANES)
            sel, offset = _select_mask(keys_ref[c, rows, :], thr, need, offset, tri)
            causal = c * kc + j * LANES + sub <= qsub
            bias_ref[c, rows, :] = jnp.where(causal, (sel - 1.0) * 1e30, NEG)
        return offset

    lax.fori_loop(0, n_chunks, bias_chunk, jnp.zeros((1, tq), F32))

    accs = _flash_loop(qa_ref, ka_ref, vT_ref, lambda g, c: bias_ref[c], n_chunks, tq, kc)
    for heads, acc in accs:
        _store_heads(o_ref, _normalise(acc), heads, tq)


def _dsa(qi, wT, ki, qa, ka, vaT, bsz, seq, tq, kc):
    n_keep = min(DSA_TOPK_MAX, seq // 4)
    nq = seq // tq
    qblk = lambda w: pl.BlockSpec((1, tq, w), lambda b, i: (b, i, 0))
    full = lambda w: pl.BlockSpec((1, seq, w), lambda b, i: (b, 0, 0))
    return pl.pallas_call(
        functools.partial(_dsa_kernel, tq=tq, kc=kc, n_keep=n_keep),
        out_shape=jax.ShapeDtypeStruct((bsz, seq, A_HEADS * HEAD_DIM), BF16),
        grid=(bsz, nq),
        in_specs=[qblk(1024), pl.BlockSpec((IDX_HEADS, tq), lambda b, i: (0, b * nq + i)), full(LANES),
                  qblk(1024), full(LANES),
                  pl.BlockSpec((A_KV_HEADS, seq // kc, LANES, kc), lambda b, i: (0, b, 0, 0))],
        out_specs=qblk(512),
        scratch_shapes=[pltpu.VMEM((seq // kc, kc, tq), I32), pltpu.VMEM((seq // kc, kc, tq), F32),
                        pltpu.VMEM((4, seq // kc, kc, tq), BF16)],
        compiler_params=_cparams(("parallel", "parallel")),
        name="dsa",
    )(qi, wT, ki, qa, ka, vaT)


def _gelu_tanh(x):
    return 0.5 * x * (1.0 + jnp.tanh(math.sqrt(2.0 / math.pi) * (x + 0.044715 * (x * x * x))))


def _compress_kernel(k_ref, v_ref, pk_ref, pv_ref, w1k_ref, w2k_ref, w1v_ref, w2v_ref,
                     kc_ref, vcT_ref, *, n_rows):
    half = CMP_BLOCK // 2

    def one(x_ref, p_ref, w1_ref, w2_ref):
        outs = []
        for g in range(B_KV_HEADS):
            lo = jnp.zeros((n_rows, CMP_HIDDEN), F32)
            hi = jnp.zeros((n_rows, CMP_HIDDEN), F32)
            for l in range(half):
                xl = x_ref[0, pl.ds(l, n_rows, stride=CMP_STRIDE), :][:, g * HEAD_DIM:(g + 1) * HEAD_DIM]
                a = (xl + p_ref[l:l + 1, :]).astype(BF16)
                b = (xl + p_ref[half + l:half + l + 1, :]).astype(BF16)
                lo = lo + _dot(a, w1_ref[l * HEAD_DIM:(l + 1) * HEAD_DIM, :].astype(BF16))
                hi = hi + _dot(b, w1_ref[(half + l) * HEAD_DIM:(half + l + 1) * HEAD_DIM, :].astype(BF16))
            hid = lo + pltpu.roll(hi, n_rows - 1, axis=0)
            outs.append(_dot(_gelu_tanh(hid).astype(BF16), w2_ref[...].astype(BF16)))
        return outs

    k0, k1 = one(k_ref, pk_ref, w1k_ref, w2k_ref)
    kc_ref[0] = jnp.concatenate([k0, k1], axis=1).astype(kc_ref.dtype)
    for g, v in enumerate(one(v_ref, pv_ref, w1v_ref, w2v_ref)):
        vcT_ref[0, g] = jnp.concatenate([v, jnp.zeros_like(v)], axis=1).T.astype(vcT_ref.dtype)


def _compress(kcmp, vcmp, pos_k, pos_v, w1k, w2k, w1v, w2v):
    bsz, seq, width = kcmp.shape
    n_rows = seq // CMP_STRIDE
    xspec = pl.BlockSpec((1, seq, width), lambda b: (b, 0, 0))
    cst = lambda a: pl.BlockSpec(a.shape, lambda b: (0,) * a.ndim)
    return pl.pallas_call(
        functools.partial(_compress_kernel, n_rows=n_rows),
        out_shape=(jax.ShapeDtypeStruct((bsz, n_rows, LANES), BF16),
                   jax.ShapeDtypeStruct((bsz, B_KV_HEADS, LANES, n_rows), BF16)),
        grid=(bsz,),
        in_specs=[xspec, xspec, cst(pos_k), cst(pos_v), cst(w1k), cst(w2k), cst(w1v), cst(w2v)],
        out_specs=(pl.BlockSpec((1, n_rows, LANES), lambda b: (b, 0, 0)),
                   pl.BlockSpec((1, B_KV_HEADS, LANES, n_rows), lambda b: (b, 0, 0, 0))),
        compiler_params=_cparams(("parallel",)),
        name="compress",
    )(kcmp, vcmp, pos_k, pos_v, w1k, w2k, w1v, w2v)


def _split3(x):
    a = x.astype(BF16)
    r = x - a.astype(F32)
    b = r.astype(BF16)
    c = (r - b.astype(F32)).astype(BF16)
    return a, b, c


def _nsa_kernel(qraw_ref, qrot_ref, kc_ref, vcT_ref, ksel_ref, vselT_ref, kwin_ref, vwinT_ref,
                gT_ref, exp_ref, o_ref, bias_ref, *, tq, kc, seq, n_c):
    t0 = pl.program_id(1) * tq
    n_chunks = (t0 + tq + kc - 1) // kc
    n_s = seq // SEL_BLOCK
    n_pick = min(SEL_COUNT, n_s)
    wt = min(tq, WIN_TQ)
    span = WINDOW + wt
    gT = gT_ref[...]

    ridx = lax.broadcasted_iota(I32, (LANES, tq), 0)
    tl = t0 + lax.broadcasted_iota(I32, (LANES, tq), 1)
    valid_c = jnp.where((ridx * CMP_STRIDE + CMP_BLOCK - 1 <= tl) & (ridx < n_c), 1.0, 0.0)
    valid4 = jnp.concatenate([valid_c] * REP, axis=1) > 0.0
    js = lax.broadcasted_iota(I32, (LANES, LANES), 0) * SEL_BLOCK
    cs = lax.broadcasted_iota(I32, (LANES, LANES), 1) * CMP_STRIDE
    overlap_t = jnp.where((cs <= js + SEL_BLOCK - 1) & (cs + CMP_BLOCK - 1 >= js), 1.0, 0.0).astype(BF16)
    cur = tl // SEL_BLOCK
    forced = (ridx == 0) | ((cur - ridx >= 0) & (cur - ridx < SEL_LOCAL))
    blk_causal = ridx * SEL_BLOCK <= tl
    tri = _tri_strict_lower(LANES, BF16)
    keypos = lax.broadcasted_iota(I32, (kc, tq), 0)
    qpos = t0 + lax.broadcasted_iota(I32, (kc, tq), 1)

    o_cmp_g = []
    for g in range(B_KV_HEADS):
        heads = range(g * REP, (g + 1) * REP)
        s = jnp.where(valid4, _dot_nt(kc_ref[0], _stack_heads(qraw_ref, heads)), NEG)
        e = jnp.where(valid4, jnp.exp(s - _col_max(s)), 0.0)
        p = e * (1.0 / jnp.maximum(_col_sum(e), 1e-30))
        o_cmp_g.append(_dot(vcT_ref[0, g], p.astype(BF16)))
        p_sum = p[:, 0:tq]
        for r in range(1, REP):
            p_sum = p_sum + p[:, r * tq:(r + 1) * tq]
        pa, pb, pc = _split3(p_sum)
        imp = _dot(overlap_t, pa) + _dot(overlap_t, pb) + _dot(overlap_t, pc)
        imp = jnp.where(forced, FORCE, jnp.where(blk_causal, imp, NEG))
        keys = _sort_key(imp[0:n_s, :])
        count_ge = lambda cand, keys=keys: _col_sum(jnp.where(keys >= cand, 1.0, 0.0))
        thr = _kth_largest_key(count_ge, (1, tq), n_pick)
        need = float(n_pick) - _col_sum(jnp.where(keys > thr, 1.0, 0.0))
        blk_sel, _ = _select_mask(keys, thr, need, jnp.zeros((1, tq), F32), tri[0:n_s, 0:n_s])
        blk_sel = jnp.concatenate([blk_sel, jnp.zeros((LANES - n_s, tq), F32)], axis=0).astype(BF16)

        def bias_chunk(c, carry, blk_sel=blk_sel, g=g):
            tok_sel = _dot(exp_ref[c], blk_sel)
            bias_ref[g, c] = jnp.where(c * kc + keypos <= qpos, (tok_sel - 1.0) * 1e30, NEG)
            return carry

        lax.fori_loop(0, n_chunks, bias_chunk, 0)

    sel = _flash_loop(qrot_ref, ksel_ref, vselT_ref, lambda g, c: bias_ref[g, c], n_chunks, tq, kc)

    for (g, heads), (_, acc_sel) in zip(_head_chains(), sel):
        o_win_sub = []
        for sub in range(tq // wt):
            t_sub = t0 + sub * wt
            w0 = pl.multiple_of(jnp.clip(t_sub - WINDOW, 0, seq - span), wt)
            wdiff = ((t_sub + lax.broadcasted_iota(I32, (span, wt), 1))
                     - (w0 + lax.broadcasted_iota(I32, (span, wt), 0)))
            wbias = jnp.where((wdiff >= 0) & (wdiff < WINDOW), 0.0, NEG)
            q_sub = jnp.concatenate([qrot_ref[0, sub * wt:(sub + 1) * wt, h * LANES:(h + 1) * LANES]
                                     for h in heads], axis=0)
            sw = (_dot_nt(kwin_ref[0, pl.ds(w0, span), :], q_sub)
                  + jnp.concatenate([wbias] * HEADS_PER_CHAIN, axis=1))
            ew = jnp.exp(sw - _col_max(sw)).astype(BF16)
            wblk = w0 // LANES
            acc_win = _dot(vwinT_ref[g, wblk], ew[0:LANES, :])
            for j in range(1, span // LANES):
                acc_win = acc_win + _dot(vwinT_ref[g, wblk + j], ew[j * LANES:(j + 1) * LANES, :])
            o_win_sub.append(_normalise(acc_win))

        o_sel = _normalise(acc_sel)
        cols = []
        for r, h in enumerate(heads):
            col = slice(r * tq, (r + 1) * tq)
            cmp_col = slice((h % REP) * tq, (h % REP + 1) * tq)
            o_win = jnp.concatenate([o[:, r * wt:(r + 1) * wt] for o in o_win_sub], axis=1)
            cols.append(gT[3 * h:3 * h + 1, :] * o_cmp_g[g][:, cmp_col]
                        + gT[3 * h + 1:3 * h + 2, :] * o_sel[:, col]
                        + gT[3 * h + 2:3 * h + 3, :] * o_win)
        _store_heads(o_ref, jnp.concatenate(cols, axis=1), heads, tq)


def _nsa(qraw, qrot, kc_x, vcT, ksel, vselT, kwin, vwinT, gT, bsz, seq, tq, kc):
    n_c = (seq - CMP_BLOCK) // CMP_STRIDE + 1
    nq = seq // tq
    key_blk = (jnp.arange(seq, dtype=I32) // SEL_BLOCK).reshape(seq // kc, kc, 1)
    expand = (key_blk == jnp.arange(LANES, dtype=I32).reshape(1, 1, LANES)).astype(BF16)
    qblk = lambda w: pl.BlockSpec((1, tq, w), lambda b, i: (b, i, 0))
    full = lambda w: pl.BlockSpec((1, seq, w), lambda b, i: (b, 0, 0))
    g = B_KV_HEADS
    return pl.pallas_call(
        functools.partial(_nsa_kernel, tq=tq, kc=kc, seq=seq, n_c=n_c),
        out_shape=jax.ShapeDtypeStruct((bsz, seq, B_HEADS * HEAD_DIM), BF16),
        grid=(bsz, nq),
        in_specs=[qblk(1024), qblk(1024),
                  pl.BlockSpec((1,) + kc_x.shape[1:], lambda b, i: (b, 0, 0)),
                  pl.BlockSpec((1,) + vcT.shape[1:], lambda b, i: (b, 0, 0, 0)),
                  full(LANES), pl.BlockSpec((g, seq // kc, LANES, kc), lambda b, i: (0, b, 0, 0)),
                  full(LANES), pl.BlockSpec((g, seq // LANES, LANES, LANES), lambda b, i: (0, b, 0, 0)),
                  pl.BlockSpec((N_GATES, tq), lambda b, i: (0, b * nq + i)),
                  pl.BlockSpec(expand.shape, lambda b, i: (0, 0, 0))],
        out_specs=qblk(512),
        scratch_shapes=[pltpu.VMEM((g, seq // kc, kc, tq), F32)],
        compiler_params=_cparams(("parallel", "parallel")),
        name="nsa",
    )(qraw, qrot, kc_x, vcT, ksel, vselT, kwin, vwinT, gT, expand)


def _pack_pairs(x):
    n = x.shape[1] // 2
    lo = pltpu.bitcast(x[:, :n].astype(BF16).astype(F32), I32)
    hi = pltpu.bitcast(x[:, n:].astype(BF16).astype(F32), I32)
    return lax.shift_right_logical(lo, 16) | (hi & jnp.int32(-65536))


def _unpack_pairs(p):
    lo = pltpu.bitcast(lax.shift_left(p, 16), F32)
    hi = pltpu.bitcast(p & jnp.int32(-65536), F32)
    return jnp.concatenate([lo, hi], axis=1)


def _layer_norm(y, g, b):
    mu = jnp.mean(y, axis=1, keepdims=True)
    yc = y - mu
    var = jnp.mean(yc * yc, axis=1, keepdims=True)
    return yc * lax.rsqrt(var + LN_EPS) * g + b


def _out_proj_kernel(oa_ref, ob_ref, ga_ref, gb_ref, x_ref, mod_ref, wa_ref, wb_ref, wo_ref,
                     g1_ref, b1_ref, wrh_ref, wrl_ref, x1_ref, u2_ref, lg_ref):
    mod = mod_ref[0]
    merged = (ga_ref[...].astype(F32) * _dot(oa_ref[...], wa_ref[...])
              + gb_ref[...].astype(F32) * _dot(ob_ref[...], wb_ref[...]))
    mix = _dot(merged.astype(BF16), wo_ref[...])
    x1 = _layer_norm(DN_ALPHA * x_ref[...] + mod[2:3, :] * mix, g1_ref[...], b1_ref[...])
    x1_ref[...] = x1
    u2 = x1 * (1.0 + mod[4:5, :]) + mod[3:4, :]
    packed = _pack_pairs(u2)
    for j in range(PIECES):
        u2_ref[j] = packed[:, j * SC_ROW:(j + 1) * SC_ROW]
    uh = u2.astype(BF16)
    ul = (u2 - uh.astype(F32)).astype(BF16)
    lg_ref[...] = _dot_nt(wrh_ref[...], uh) + _dot_nt(wrh_ref[...], ul) + _dot_nt(wrl_ref[...], uh)


def _out_proj(oa, ob, ga, gb, x2, mod3, wa, wb, wo, g1, b1, wrh, wrl, seq, tm):
    t, d = x2.shape
    per_b = seq // tm
    row = lambda w: pl.BlockSpec((tm, w), lambda i: (i, 0))
    cst = lambda a: pl.BlockSpec(a.shape, lambda i: (0,) * a.ndim)
    return pl.pallas_call(
        _out_proj_kernel,
        out_shape=(jax.ShapeDtypeStruct((t, d), F32), jax.ShapeDtypeStruct((PIECES, t, SC_ROW), I32),
                   jax.ShapeDtypeStruct((N_EXPERTS, t), F32)),
        grid=(t // tm,),
        in_specs=[row(512), row(512), row(d), row(d), row(d),
                  pl.BlockSpec((1, 6, d), lambda i: (i // per_b, 0, 0)),
                  cst(wa), cst(wb), cst(wo), cst(g1), cst(b1), cst(wrh), cst(wrl)],
        out_specs=(row(d), pl.BlockSpec((PIECES, tm, SC_ROW), lambda i: (0, i, 0)),
                   pl.BlockSpec((N_EXPERTS, tm), lambda i: (0, i))),
        compiler_params=_cparams(("parallel",)),
        name="out_proj",
    )(oa, ob, ga, gb, x2, mod3, wa, wb, wo, g1, b1, wrh, wrl)


def _first_max(x, rows):
    m = jnp.max(x, axis=0, keepdims=True)
    idx = jnp.min(jnp.where(x == m, rows, 1e9), axis=0, keepdims=True)
    return m, idx


def _router_kernel(lg_ref, rb_ref, idx_ref, w_ref, pos_ref, cnt_ref, carry_ref, *, tm):
    @pl.when(pl.program_id(0) == 0)
    def _():
        carry_ref[...] = jnp.zeros_like(carry_ref)

    per_g = N_EXPERTS // N_GROUPS
    scores = _sigmoid(lg_ref[...])
    choice = scores + rb_ref[...][:, 0:1]
    rows = lax.broadcasted_iota(I32, (N_EXPERTS, tm), 0).astype(F32)
    rows_g = lax.broadcasted_iota(I32, (per_g, tm), 0).astype(F32)
    ninf = -jnp.inf

    gs = []
    for g in range(N_GROUPS):
        x = choice[g * per_g:(g + 1) * per_g, :]
        m1, i1 = _first_max(x, rows_g)
        m2 = jnp.max(jnp.where(rows_g == i1, ninf, x), axis=0, keepdims=True)
        gs.append(m1 + m2)
    gscore = jnp.concatenate(gs, axis=0)
    rows8 = lax.broadcasted_iota(I32, (N_GROUPS, tm), 0).astype(F32)
    keep = jnp.zeros((N_GROUPS, tm), F32)
    for _ in range(TOPK_GROUPS):
        _, gi = _first_max(gscore, rows8)
        hit = rows8 == gi
        keep = jnp.where(hit, 1.0, keep)
        gscore = jnp.where(hit, ninf, gscore)
    keep_full = jnp.concatenate(
        [jnp.broadcast_to(keep[g:g + 1, :], (per_g, tm)) for g in range(N_GROUPS)], axis=0)
    masked = jnp.where(keep_full > 0.0, choice, NEG)

    idxs, ws = [], []
    onehot = jnp.zeros((N_EXPERTS, tm), F32)
    for _ in range(TOP_K):
        _, ei = _first_max(masked, rows)
        hit = rows == ei
        idxs.append(ei)
        ws.append(jnp.sum(jnp.where(hit, scores, 0.0), axis=0, keepdims=True))
        masked = jnp.where(hit, ninf, masked)
        onehot = jnp.where(hit, 1.0, onehot)
    idx = jnp.concatenate(idxs, axis=0)
    w = jnp.concatenate(ws, axis=0)
    idx_ref[...] = idx.astype(I32)
    w = w / jnp.sum(w, axis=0, keepdims=True) * ROUTED_SCALE
    w_ref[...] = jnp.concatenate([w, jnp.zeros((LANES - TOP_K, tm), F32)], axis=0).T

    tri = _tri_strict_upper(tm, BF16)
    base = _dot(onehot.astype(BF16), tri) + carry_ref[...][:, 0:1]
    pos = [jnp.sum(jnp.where(rows == idxs[k], base, 0.0), axis=0, keepdims=True) for k in range(TOP_K)]
    pos_ref[...] = jnp.concatenate(pos, axis=0).astype(I32)
    carry = carry_ref[...] + jnp.sum(onehot, axis=1, keepdims=True)
    carry_ref[...] = carry
    cnt_ref[...] = carry


def _router(lg, router_bias, tm):
    e, t = lg.shape
    rb = jnp.broadcast_to(router_bias.reshape(e, 1).astype(F32), (e, LANES))
    tok = lambda r: pl.BlockSpec((r, tm), lambda i: (0, i))
    return pl.pallas_call(
        functools.partial(_router_kernel, tm=tm),
        out_shape=(jax.ShapeDtypeStruct((TOP_K, t), I32), jax.ShapeDtypeStruct((t, LANES), F32),
                   jax.ShapeDtypeStruct((TOP_K, t), I32), jax.ShapeDtypeStruct((e, LANES), F32)),
        grid=(t // tm,),
        in_specs=[tok(e), pl.BlockSpec((e, LANES), lambda i: (0, 0))],
        out_specs=(tok(TOP_K), pl.BlockSpec((tm, LANES), lambda i: (i, 0)), tok(TOP_K),
                   pl.BlockSpec((e, LANES), lambda i: (0, 0))),
        scratch_shapes=[pltpu.VMEM((e, LANES), F32)],
        compiler_params=_cparams(("arbitrary",)),
        name="router",
    )(lg, rb)


def _dest_kernel(idx_ref, pos_ref, st_ref, dest_ref, *, tm):
    rows = lax.broadcasted_iota(I32, (N_EXPERTS, tm), 0)
    starts = st_ref[...][:, 0:1]
    idx = idx_ref[...]
    out = []
    for k in range(TOP_K):
        out.append(jnp.sum(jnp.where(rows == idx[k:k + 1, :], starts, 0.0), axis=0, keepdims=True))
    dest_ref[...] = jnp.concatenate(out, axis=0).astype(I32) + pos_ref[...]


def _dest(idx, pos, starts, tm):
    k, t = idx.shape
    st = jnp.broadcast_to(starts.reshape(N_EXPERTS, 1).astype(F32), (N_EXPERTS, LANES))
    tok = pl.BlockSpec((k, tm), lambda i: (0, i))
    return pl.pallas_call(
        functools.partial(_dest_kernel, tm=tm),
        out_shape=jax.ShapeDtypeStruct((k, t), I32),
        grid=(t // tm,),
        in_specs=[tok, tok, pl.BlockSpec((N_EXPERTS, LANES), lambda i: (0, 0))],
        out_specs=tok,
        compiler_params=_cparams(("parallel",)),
        name="dest",
    )(idx, pos, st)


def _experts_kernel(blk_ref, used_ref, first_ref, slot_ref, next_ref, x_ref, wg_hbm, wu_hbm, wd_hbm, y_ref,
                    wg_buf, wu_buf, wd_buf, wg_bf, wu_bf, wd_bf, sems):
    b = pl.program_id(0)
    active = b < used_ref[0]

    def weight_copies(e, slot):
        return [pltpu.make_async_copy(hbm.at[e], buf.at[slot], sems.at[slot, i])
                for i, (hbm, buf) in enumerate(((wg_hbm, wg_buf), (wu_hbm, wu_buf), (wd_hbm, wd_buf)))]

    @pl.when(b == 0)
    def _():
        for cp in weight_copies(blk_ref[0], 0):
            cp.start()

    @pl.when(active & (first_ref[b] == 1))
    def _():
        slot = slot_ref[b]
        for cp in weight_copies(blk_ref[b], slot):
            cp.wait()
        nxt = next_ref[b]

        @pl.when(nxt >= 0)
        def _():
            for cp in weight_copies(nxt, 1 - slot):
                cp.start()

        wg_bf[...] = wg_buf[slot].astype(BF16)
        wu_bf[...] = wu_buf[slot].astype(BF16)
        wd_bf[...] = wd_buf[slot].astype(BF16)

    @pl.when(active)
    def _():
        x = jnp.concatenate([x_ref[j] for j in range(PIECES)], axis=1)
        x = _unpack_pairs(x).astype(BF16)
        a = _dot(x, wg_bf[...])
        u = _dot(x, wu_bf[...])
        h = (a * _sigmoid(a) * u).astype(BF16)
        y = _pack_pairs(_dot(h, wd_bf[...]))
        for j in range(PIECES):
            y_ref[j] = y[:, j * SC_ROW:(j + 1) * SC_ROW]


def _experts(xs, plan, wg, wu, wd, bm):
    _, cap, _ = xs.shape
    n_blocks = cap // bm
    d, f = wg.shape[1], wg.shape[2]
    rows = pl.BlockSpec((PIECES, bm, SC_ROW), lambda b, blk, used, *_: (0, jnp.minimum(b, used[0] - 1), 0))
    hbm = pl.BlockSpec(memory_space=pl.ANY)
    return pl.pallas_call(
        _experts_kernel,
        out_shape=jax.ShapeDtypeStruct(xs.shape, I32),
        grid_spec=pltpu.PrefetchScalarGridSpec(
            num_scalar_prefetch=5,
            grid=(n_blocks,),
            in_specs=[rows, hbm, hbm, hbm],
            out_specs=rows,
            scratch_shapes=[pltpu.VMEM((2, d, f), F32), pltpu.VMEM((2, d, f), F32), pltpu.VMEM((2, f, d), F32),
                            pltpu.VMEM((d, f), BF16), pltpu.VMEM((d, f), BF16), pltpu.VMEM((f, d), BF16),
                            pltpu.SemaphoreType.DMA((2, 3))]),
        compiler_params=_cparams(("arbitrary",)),
        name="experts",
    )(plan["blk_e"], plan["n_used"], plan["first"], plan["slot"], plan["next_e"], xs, wg, wu, wd)


def _final_kernel(x1_ref, u2_ref, yg_ref, w_ref, mod_ref, sg_ref, su_ref, sd_ref, g2_ref, b2_ref, o_ref):
    mod = mod_ref[0]
    w = w_ref[...]
    rows = lambda ref, *lead: jnp.concatenate([ref[(j,) + lead] for j in range(PIECES)], axis=1)
    routed = w[:, 0:1] * _unpack_pairs(rows(yg_ref, 0))
    for k in range(1, TOP_K):
        routed = routed + w[:, k:k + 1] * _unpack_pairs(rows(yg_ref, k))
    u = _unpack_pairs(rows(u2_ref)).astype(BF16)
    a = _dot(u, sg_ref[...])
    b = _dot(u, su_ref[...])
    shared = _dot((a * _sigmoid(a) * b).astype(BF16), sd_ref[...])
    y = DN_ALPHA * x1_ref[...] + mod[5:6, :] * (routed + shared)
    o_ref[...] = _layer_norm(y, g2_ref[...], b2_ref[...])


def _final(x1, u2p, yg, wtok, mod3, sg, su, sd, g2, b2, seq, tm):
    t, d = x1.shape
    per_b = seq // tm
    row = lambda w: pl.BlockSpec((tm, w), lambda i: (i, 0))
    cst = lambda a: pl.BlockSpec(a.shape, lambda i: (0,) * a.ndim)
    return pl.pallas_call(
        _final_kernel,
        out_shape=jax.ShapeDtypeStruct((t, d), F32),
        grid=(t // tm,),
        in_specs=[row(d), pl.BlockSpec((PIECES, tm, SC_ROW), lambda i: (0, i, 0)),
                  pl.BlockSpec((PIECES, TOP_K, tm, SC_ROW), lambda i: (0, 0, i, 0)),
                  row(LANES), pl.BlockSpec((1, 6, d), lambda i: (i // per_b, 0, 0)),
                  cst(sg), cst(su), cst(sd), cst(g2), cst(b2)],
        out_specs=row(d),
        compiler_params=_cparams(("parallel",)),
        name="final",
    )(x1, u2p, yg, wtok, mod3, sg, su, sd, g2, b2)


def _sc_mesh():
    return plsc.VectorSubcoreMesh(core_axis_name="core", subcore_axis_name="subcore")


def _sc_scatter_rows(src, idx, n_out):
    n_copies, n = idx.shape

    @functools.partial(pl.kernel, out_type=jax.ShapeDtypeStruct((n_out, SC_ROW), src.dtype),
                       mesh=_sc_mesh(), scratch_types=[])
    def k(x_hbm, i_hbm, o_hbm):
        def body(x_vmem, i_vmem):
            for c in range(n_copies):
                pltpu.sync_copy(x_vmem, o_hbm.at[i_vmem.at[c]])

        pltpu.emit_pipeline(
            body, grid=(n // SC_WINDOW,),
            in_specs=[pl.BlockSpec((SC_WINDOW, SC_ROW), lambda i: (i, 0)),
                      pl.BlockSpec((n_copies, SC_WINDOW), lambda i: (0, i))],
            out_specs=[],
            core_axis_name=("core", "subcore"),
            dimension_semantics=(pltpu.PARALLEL,),
        )(x_hbm, i_hbm)

    return k(src, idx)


def _sc_gather_rows(src, idx):
    n_idx = idx.shape[0]

    @functools.partial(pl.kernel, out_type=jax.ShapeDtypeStruct((n_idx, SC_ROW), src.dtype),
                       mesh=_sc_mesh(), scratch_types=[])
    def k(x_hbm, i_hbm, o_hbm):
        def body(i_vmem, o_vmem):
            pltpu.sync_copy(x_hbm.at[i_vmem.at[0]], o_vmem)

        pltpu.emit_pipeline(
            body, grid=(n_idx // SC_WINDOW,),
            in_specs=[pl.BlockSpec((1, SC_WINDOW), lambda i: (0, i))],
            out_specs=[pl.BlockSpec((SC_WINDOW, SC_ROW), lambda i: (i, 0))],
            core_axis_name=("core", "subcore"),
            dimension_semantics=(pltpu.PARALLEL,),
        )(i_hbm, o_hbm)

    return k(src, idx.reshape(1, n_idx))


def _moe_plan(counts, n_tok):
    bm = BM_EXPERT
    padded = (counts + bm - 1) // bm * bm
    p_ends = jnp.cumsum(padded)
    starts = p_ends - padded
    n_blocks = n_tok * TOP_K // bm + N_EXPERTS
    blk = jnp.arange(n_blocks, dtype=I32)
    blk_e = jnp.minimum(jnp.sum(p_ends[None, :] <= (blk * bm)[:, None], axis=1), N_EXPERTS - 1).astype(I32)
    n_used = (p_ends[-1] // bm).astype(I32)
    prev = jnp.concatenate([jnp.full((1,), -1, I32), blk_e[:-1]])
    first = ((blk_e != prev) & (blk < n_used)).astype(I32)
    slot = (jnp.cumsum(first) - 1) % 2
    first_pos = jnp.where(first == 1, blk, n_blocks)
    next_first = lax.cummin(jnp.concatenate([first_pos[1:], jnp.full((1,), n_blocks, I32)]), reverse=True)
    next_e = jnp.where(next_first < n_blocks, blk_e[jnp.minimum(next_first, n_blocks - 1)], -1)
    plan = dict(blk_e=blk_e, n_used=n_used.reshape(1), first=first, slot=slot.astype(I32),
                next_e=next_e.astype(I32))
    return starts, plan, n_blocks


def _layer(x, mod, positions, w_in, w_br_a, w_br_b, w_out, cmp_pos_k, cmp_pos_v, cmp_k_w1, cmp_k_w2,
           cmp_v_w1, cmp_v_w2, ln1_g, ln1_b, w_router, router_bias, w_exp_gate, w_exp_up, w_exp_down,
           w_sh_gate, w_sh_up, w_sh_down, ln2_g, ln2_b):
    bsz, seq, d = x.shape
    t = bsz * seq
    assert seq // CMP_STRIDE == LANES and seq % TQ_ATTN == 0
    assert seq % TM_PROJ == 0 and KC_ATTN == TM_PROJ
    x2 = x.reshape(t, d)
    mod3 = mod.reshape(bsz, 6, d)

    w_pack, w_small = _pack_w_in(w_in)
    z = _in_proj(x2, mod3, w_pack, w_small, _rope_table(positions), seq, TM_PROJ)
    per_b = lambda name: z[name].reshape(bsz, seq, z[name].shape[1])

    o_a = _dsa(per_b("qi"), z["wT"], per_b("ki"), per_b("qa"), per_b("ka"), z["vaT"],
               bsz, seq, TQ_ATTN, KC_ATTN)

    kc_x, vcT = _compress(per_b("kcmp"), per_b("vcmp"),
                          cmp_pos_k, cmp_pos_v, cmp_k_w1, cmp_k_w2, cmp_v_w1, cmp_v_w2)
    o_b = _nsa(per_b("qbraw"), per_b("qbrot"), kc_x, vcT, per_b("ksel"), z["vselT"], per_b("kwin"),
               z["vwinT"], z["gT"], bsz, seq, TQ_ATTN, KC_ATTN)

    wr_hi = w_router.T.astype(BF16)
    wr_lo = (w_router.T - wr_hi.astype(F32)).astype(BF16)
    x1, u2p, logits = _out_proj(
        o_a.reshape(t, -1), o_b.reshape(t, -1), z["ga"], z["gb"], x2, mod3,
        w_br_a.astype(BF16), w_br_b.astype(BF16), w_out.astype(BF16),
        ln1_g.reshape(1, d), ln1_b.reshape(1, d), wr_hi, wr_lo, seq, TM_PROJ)

    idx, wtok, pos, counts = _router(logits, router_bias, TM_ROUTE)
    starts, plan, n_blocks = _moe_plan(counts[:, 0].astype(I32), t)
    dest = _dest(idx, pos, starts, TM_ROUTE)

    cap = n_blocks * BM_EXPERT
    dest_p = dest[None] + (jnp.arange(PIECES, dtype=I32) * cap).reshape(PIECES, 1, 1)
    xs = _sc_scatter_rows(u2p.reshape(PIECES * t, SC_ROW),
                          jnp.swapaxes(dest_p, 0, 1).reshape(TOP_K, PIECES * t), cap * PIECES)
    ys = _experts(xs.reshape(PIECES, cap, SC_ROW), plan, w_exp_gate, w_exp_up, w_exp_down, BM_EXPERT)
    yg = _sc_gather_rows(ys.reshape(cap * PIECES, SC_ROW), dest_p.reshape(-1)
                         ).reshape(PIECES, TOP_K, t, SC_ROW)

    return _final(x1, u2p, yg, wtok, mod3, w_sh_gate.astype(BF16), w_sh_up.astype(BF16),
                  w_sh_down.astype(BF16), ln2_g.reshape(1, d), ln2_b.reshape(1, d), seq, TM_PROJ
                  ).reshape(bsz, seq, d)


def kernel(x, c, positions, w_ada, b_ada, w_in, w_br_a, w_br_b, w_out, cmp_pos_k, cmp_pos_v, cmp_k_w1,
           cmp_k_w2, cmp_v_w1, cmp_v_w2, ln1_g, ln1_b, w_router, router_bias, w_exp_gate, w_exp_up,
           w_exp_down, w_sh_gate, w_sh_up, w_sh_down, ln2_g, ln2_b):
    for l in range(w_ada.shape[0]):
        mod = _mod(c, w_ada[l], b_ada[l])
        x = _layer(x, mod, positions, w_in[l], w_br_a[l], w_br_b[l], w_out[l], cmp_pos_k[l], cmp_pos_v[l],
                   cmp_k_w1[l], cmp_k_w2[l], cmp_v_w1[l], cmp_v_w2[l], ln1_g[l], ln1_b[l], w_router[l],
                   router_bias[l], w_exp_gate[l], w_exp_up[l], w_exp_down[l], w_sh_gate[l], w_sh_up[l],
                   w_sh_down[l], ln2_g[l], ln2_b[l])
    return x
```

```python
import functools
import math

import jax
import jax.numpy as jnp
from jax import lax
from jax.experimental import pallas as pl
from jax.experimental.pallas import tpu as pltpu
from jax.experimental.pallas import tpu_sc as plsc

F32 = jnp.float32
BF16 = jnp.bfloat16
I32 = jnp.int32

D_MODEL = 1024
HEAD_DIM = 64
ROPE_THETA = 500000.0
ROPE_FRACTION = 4
A_HEADS = 8
A_KV_HEADS = 2
IDX_HEADS = 8
IDX_DIM = 32
DSA_TOPK_MAX = 256
B_HEADS = 8
B_KV_HEADS = 2
REP = 4
CMP_BLOCK = 32
CMP_STRIDE = 16
CMP_HIDDEN = 256
SEL_BLOCK = 64
SEL_COUNT = 16
SEL_LOCAL = 2
WINDOW = 512
N_EXPERTS = 256
TOP_K = 8
N_GROUPS = 8
TOPK_GROUPS = 4
ROUTED_SCALE = 2.5
DEPTH = 1
DN_ALPHA = (2 * DEPTH) ** 0.25
LN_EPS = 1e-5
NEG = -1e30
FORCE = 1e9
INT_MIN = -2147483648
N_GATES = 3 * B_HEADS

LANES = 128
SUBLANES = 8
VMEM_LIMIT = 56 * 1024 * 1024
SC_WINDOW = 128
SC_ROW = 256
PIECES = (D_MODEL // 2) // SC_ROW

TM_PROJ = 512
TQ_ATTN = 512
WIN_TQ = 256
KC_ATTN = 512
TM_ROUTE = 512
BM_EXPERT = 512

_IN_SLOTS = (("qa", A_HEADS * HEAD_DIM), ("ka", A_KV_HEADS * HEAD_DIM), ("va", A_KV_HEADS * HEAD_DIM),
             ("qi", IDX_HEADS * IDX_DIM), ("ki", IDX_DIM), ("wi", IDX_HEADS), ("qb", B_HEADS * HEAD_DIM),
             ("kcmp", B_KV_HEADS * HEAD_DIM), ("vcmp", B_KV_HEADS * HEAD_DIM), ("ksel", B_KV_HEADS * HEAD_DIM),
             ("vsel", B_KV_HEADS * HEAD_DIM), ("kwin", B_KV_HEADS * HEAD_DIM), ("vwin", B_KV_HEADS * HEAD_DIM),
             ("gnsa", N_GATES), ("ga", D_MODEL), ("gb", D_MODEL))


def _col_ranges(slots, align):
    out, start = {}, 0
    for name, width in slots:
        out[name] = (start, start + -(-width // align) * align)
        start = out[name][1]
    return out


_IN_COLS = _col_ranges(_IN_SLOTS, 1)
_PACK_COLS = _col_ranges([s for s in _IN_SLOTS if s[0] not in ("wi", "gnsa")], LANES)

NT_DIMS = (((1,), (1,)), ((), ()))


def _cparams(sem):
    return pltpu.CompilerParams(dimension_semantics=sem, vmem_limit_bytes=VMEM_LIMIT)


def _sigmoid(x):
    return 1.0 / (1.0 + jnp.exp(-x))


def _dot(a, b):
    return jnp.dot(a, b, preferred_element_type=F32)


def _dot_nt(a, b):
    return lax.dot_general(a, b, NT_DIMS, preferred_element_type=F32)


def _sort_key(x):
    x = jnp.where(x == 0.0, 0.0, x)
    bits = pltpu.bitcast(x, I32)
    return jnp.where(bits < 0, bits ^ 0x7FFFFFFF, bits)


def _kth_largest_key(count_ge, shape, k):
    kf = float(k)
    t0 = jnp.where(count_ge(jnp.zeros(shape, I32)) >= kf, 0, INT_MIN).astype(I32)

    def body(it, t):
        cand = t + jnp.left_shift(jnp.int32(1), 30 - it)
        return jnp.where(count_ge(cand) >= kf, cand, t)

    return lax.fori_loop(0, 31, body, t0)


def _tri_strict_lower(n, dtype):
    r = lax.broadcasted_iota(I32, (n, n), 0)
    c = lax.broadcasted_iota(I32, (n, n), 1)
    return jnp.where(c < r, 1.0, 0.0).astype(dtype)


def _tri_strict_upper(n, dtype):
    r = lax.broadcasted_iota(I32, (n, n), 0)
    c = lax.broadcasted_iota(I32, (n, n), 1)
    return jnp.where(r < c, 1.0, 0.0).astype(dtype)


def _mod_kernel(c_ref, w_ref, b_ref, o_ref):
    c = c_ref[...]
    cond = (c * _sigmoid(c)).astype(BF16)
    o_ref[...] = _dot(cond, w_ref[...].astype(BF16)) + b_ref[...]


def _mod(c, w_ada, b_ada):
    bsz, d = c.shape
    n = w_ada.shape[1]
    tn = 1024
    return pl.pallas_call(
        _mod_kernel,
        out_shape=jax.ShapeDtypeStruct((bsz, n), F32),
        grid=(n // tn,),
        in_specs=[pl.BlockSpec((bsz, d), lambda j: (0, 0)),
                  pl.BlockSpec((d, tn), lambda j: (0, j)),
                  pl.BlockSpec((1, tn), lambda j: (0, j))],
        out_specs=pl.BlockSpec((bsz, tn), lambda j: (0, j)),
        compiler_params=_cparams(("parallel",)),
        name="mod",
    )(c, w_ada, b_ada.reshape(1, n))


def _rope(z, c_tab, s_tab, period, half):
    w = z.shape[1]
    reps = w // LANES
    c = jnp.concatenate([c_tab] * reps, axis=1) if reps > 1 else c_tab
    s = jnp.concatenate([s_tab] * reps, axis=1) if reps > 1 else s_tab
    lane = lax.broadcasted_iota(I32, z.shape, 1)
    first = (lane & (period - 1)) < half
    partner = jnp.where(first, pltpu.roll(z, w - half, axis=1), pltpu.roll(z, half, axis=1))
    return z * c + partner * s


def _in_proj_kernel(x_ref, mod_ref, w_ref, wsm_ref, rope_ref,
                    qa_ref, ka_ref, vaT_ref, qi_ref, ki_ref, wT_ref, qbraw_ref, qbrot_ref,
                    kcmp_ref, vcmp_ref, ksel_ref, vselT_ref, kwin_ref, vwinT_ref, gT_ref, ga_ref, gb_ref):
    mod = mod_ref[0]
    u = (x_ref[...] * (1.0 + mod[1:2, :]) + mod[0:1, :]).astype(BF16)
    tm = u.shape[0]
    c64, s64 = _rope_patterns(rope_ref[...], HEAD_DIM)
    c32, s32 = _rope_patterns(rope_ref[...], IDX_DIM)
    scale = HEAD_DIM ** -0.5
    lane = lax.broadcasted_iota(I32, (tm, LANES), 1)
    low = lane < HEAD_DIM

    def proj(name):
        a, b = _PACK_COLS[name]
        return _dot(u, w_ref[:, a:b])

    rope64 = lambda z: _rope(z, c64, s64, HEAD_DIM, HEAD_DIM // ROPE_FRACTION // 2)
    rope32 = lambda z: _rope(z, c32, s32, IDX_DIM, IDX_DIM // ROPE_FRACTION // 2)

    def head_slots64(z):
        out = []
        for h in range(A_HEADS):
            pair = z[:, (h // 2) * LANES:(h // 2 + 1) * LANES]
            g = h // REP
            src = pair if h % 2 == g else pltpu.roll(pair, HEAD_DIM, axis=1)
            out.append(jnp.where(low, src, 0.0) if g == 0 else jnp.where(low, 0.0, src))
        return jnp.concatenate(out, axis=1).astype(BF16)

    def head_slots32(z):
        per = LANES // IDX_DIM
        out = []
        for h in range(IDX_HEADS):
            col = z[:, (h // per) * LANES:(h // per + 1) * LANES]
            shift = IDX_DIM * (h % per)
            src = col if shift == 0 else pltpu.roll(col, LANES - shift, axis=1)
            out.append(jnp.where(lane < IDX_DIM, src, 0.0))
        return jnp.concatenate(out, axis=1).astype(BF16)

    def store_vt(ref, z, chunk):
        zt = z.T
        ones = jnp.ones((HEAD_DIM, chunk), F32)
        for g in range(A_KV_HEADS):
            for j in range(tm // chunk):
                blk = zt[g * HEAD_DIM:(g + 1) * HEAD_DIM, j * chunk:(j + 1) * chunk]
                ref[g, j] = jnp.concatenate([blk, ones], axis=0).astype(BF16)

    qa_ref[...] = head_slots64(rope64(proj("qa")) * scale)
    ka_ref[...] = rope64(proj("ka")).astype(BF16)
    store_vt(vaT_ref, proj("va"), tm)
    qi_ref[...] = head_slots32(rope32(proj("qi")))
    ki_ref[...] = rope32(proj("ki")).astype(BF16)
    qb = proj("qb")
    qbraw_ref[...] = head_slots64(qb * scale)
    qbrot_ref[...] = head_slots64(rope64(qb) * scale)
    kcmp_ref[...] = proj("kcmp")
    vcmp_ref[...] = proj("vcmp")
    ksel_ref[...] = rope64(proj("ksel")).astype(BF16)
    store_vt(vselT_ref, proj("vsel"), tm)
    kwin_ref[...] = rope64(proj("kwin")).astype(BF16)
    store_vt(vwinT_ref, proj("vwin"), LANES)
    ga_ref[...] = _sigmoid(proj("ga")).astype(BF16)
    gb_ref[...] = _sigmoid(proj("gb")).astype(BF16)
    small = _dot_nt(wsm_ref[...], u)
    wT_ref[...] = small[0:IDX_HEADS, :]
    gT_ref[...] = _sigmoid(small[IDX_HEADS:IDX_HEADS + N_GATES, :])


def _pack_w_in(w_in):
    d = w_in.shape[0]
    col = lambda name: w_in[:, _IN_COLS[name][0]:_IN_COLS[name][1]]
    parts = []
    for name, (a, b) in _PACK_COLS.items():
        c = col(name)
        parts.append(jnp.pad(c, ((0, 0), (0, b - a - c.shape[1]))))
    w_small = jnp.concatenate([col("wi"), col("gnsa")], axis=1).T
    return jnp.concatenate(parts, axis=1).astype(BF16), w_small.astype(BF16)


def _rope_halves(dim):
    return dim // ROPE_FRACTION // 2


_ROPE_LANES = {HEAD_DIM: (0, HEAD_DIM), IDX_DIM: (IDX_DIM, 3 * IDX_DIM)}


def _rope_table(positions):
    pos = positions.astype(F32).reshape(-1, 1)
    blocks = {}
    for dim in (HEAD_DIM, IDX_DIM):
        half = _rope_halves(dim)
        inv = ROPE_THETA ** (-(jnp.arange(half, dtype=F32) * 2.0) / (2 * half))
        cos, sin = jnp.cos(pos * inv), jnp.sin(pos * inv)
        c_at, s_at = _ROPE_LANES[dim]
        blocks[c_at] = jnp.concatenate([cos, cos], axis=1)
        blocks[s_at] = jnp.concatenate([-sin, sin], axis=1)
    parts, lane = [], 0
    for at in sorted(blocks):
        parts += [jnp.zeros((pos.shape[0], at - lane), F32), blocks[at]]
        lane = at + blocks[at].shape[1]
    parts.append(jnp.zeros((pos.shape[0], LANES - lane), F32))
    return jnp.concatenate(parts, axis=1)


def _rope_patterns(tab, dim):
    half = _rope_halves(dim)
    lane = lax.broadcasted_iota(I32, tab.shape, 1)

    def spread(at):
        rep = jnp.where((lane >= at) & (lane < at + 2 * half), tab, 0.0)
        period = dim
        while period < LANES:
            rep = rep + pltpu.roll(rep, period, axis=1)
            period *= 2
        return rep

    c_at, s_at = _ROPE_LANES[dim]
    rotated = (lane & (dim - 1)) < 2 * half
    return jnp.where(rotated, spread(c_at), 1.0), spread(s_at)


def _in_proj(x2, mod3, w_pack, w_small, rope_tab, seq, tm):
    t, d = x2.shape
    n = w_pack.shape[1]
    per_b = seq // tm
    g = A_KV_HEADS
    row = lambda w: pl.BlockSpec((tm, w), lambda i: (i, 0))
    tok = lambda r: pl.BlockSpec((r, tm), lambda i: (0, i))
    vt_chunk = pl.BlockSpec((g, 1, LANES, tm), lambda i: (0, i, 0, 0))
    vt_lane = pl.BlockSpec((g, tm // LANES, LANES, LANES), lambda i: (0, i, 0, 0))
    sds = jax.ShapeDtypeStruct
    vt_chunk_shape = sds((g, t // tm, LANES, tm), BF16)
    outs = (("qa", sds((t, 1024), BF16), row(1024)), ("ka", sds((t, LANES), BF16), row(LANES)),
            ("vaT", vt_chunk_shape, vt_chunk), ("qi", sds((t, 1024), BF16), row(1024)),
            ("ki", sds((t, LANES), BF16), row(LANES)), ("wT", sds((IDX_HEADS, t), F32), tok(IDX_HEADS)),
            ("qbraw", sds((t, 1024), BF16), row(1024)), ("qbrot", sds((t, 1024), BF16), row(1024)),
            ("kcmp", sds((t, LANES), F32), row(LANES)), ("vcmp", sds((t, LANES), F32), row(LANES)),
            ("ksel", sds((t, LANES), BF16), row(LANES)), ("vselT", vt_chunk_shape, vt_chunk),
            ("kwin", sds((t, LANES), BF16), row(LANES)),
            ("vwinT", sds((g, t // LANES, LANES, LANES), BF16), vt_lane),
            ("gT", sds((N_GATES, t), F32), tok(N_GATES)),
            ("ga", sds((t, d), BF16), row(d)), ("gb", sds((t, d), BF16), row(d)))
    res = pl.pallas_call(
        _in_proj_kernel,
        out_shape=tuple(o[1] for o in outs),
        grid=(t // tm,),
        in_specs=[row(d),
                  pl.BlockSpec((1, 6, d), lambda i: (i // per_b, 0, 0)),
                  pl.BlockSpec((d, n), lambda i: (0, 0)),
                  pl.BlockSpec(w_small.shape, lambda i: (0, 0)),
                  row(LANES)],
        out_specs=tuple(o[2] for o in outs),
        compiler_params=_cparams(("parallel",)),
        name="in_proj",
    )(x2, mod3, w_pack, w_small, rope_tab)
    return {o[0]: r for o, r in zip(outs, res)}


def _fold_rows(x, op):
    n = x.shape[0]
    while n % (2 * SUBLANES) == 0:
        n //= 2
        x = op(x[:n], x[n:])
    slabs = [x[i * SUBLANES:(i + 1) * SUBLANES] for i in range(n // SUBLANES)]
    while len(slabs) > 1:
        nxt = [op(slabs[i], slabs[i + 1]) for i in range(0, len(slabs) - 1, 2)]
        slabs = nxt + ([slabs[-1]] if len(slabs) % 2 else [])
    return slabs[0]


def _col_max(x):
    return jnp.max(_fold_rows(x, jnp.maximum), axis=0, keepdims=True)


def _col_sum(x):
    return jnp.sum(_fold_rows(x, jnp.add), axis=0, keepdims=True)


PACKED_ROWS = 16


def _fold_rows_packed(x):
    n = x.shape[0]
    assert n % PACKED_ROWS == 0 and n // PACKED_ROWS <= 256
    while n > PACKED_ROWS:
        n //= 2
        x = x[:n] + x[n:]
    return x


def _stack_heads(q_ref, heads):
    return jnp.concatenate([q_ref[0, :, h * LANES:(h + 1) * LANES] for h in heads], axis=0)


def _flash_step(k, q_stack, v_t, bias4, m, acc):
    s = _dot_nt(k, q_stack) + bias4
    m_new = jnp.maximum(m, _col_max(s))
    e = jnp.exp(s - m_new).astype(BF16)
    return m_new, acc * jnp.exp(m - m_new) + _dot(v_t, e)


HEADS_PER_CHAIN = REP


def _head_chains():
    return [(h // REP, tuple(range(h, h + HEADS_PER_CHAIN))) for h in range(0, A_HEADS, HEADS_PER_CHAIN)]


def _flash_loop(q_ref, k_ref, vT_ref, bias_of, n_chunks, tq, kc):
    chains = _head_chains()
    q_stacks = [_stack_heads(q_ref, heads) for _, heads in chains]
    width = HEADS_PER_CHAIN * tq

    def body(c, carry):
        k = k_ref[0, pl.ds(pl.multiple_of(c * kc, kc), kc), :]
        out = []
        for (g, _), q_stack, (m, acc) in zip(chains, q_stacks, carry):
            bias = jnp.concatenate([bias_of(g, c)] * HEADS_PER_CHAIN, axis=1)
            out.append(_flash_step(k, q_stack, vT_ref[g, c], bias, m, acc))
        return tuple(out)

    init = tuple((jnp.full((1, width), NEG, F32), jnp.zeros((LANES, width), F32)) for _ in chains)
    res = lax.fori_loop(0, n_chunks, body, init)
    return [(heads, acc) for (_, heads), (_, acc) in zip(chains, res)]


def _normalise(acc):
    return acc * (1.0 / jnp.maximum(acc[HEAD_DIM:HEAD_DIM + 1, :], 1e-30))


def _store_heads(o_ref, o_t, heads, tq):
    o = o_t.T
    for r, h in enumerate(heads):
        o_ref[0, :, h * HEAD_DIM:(h + 1) * HEAD_DIM] = o[r * tq:(r + 1) * tq, 0:HEAD_DIM].astype(o_ref.dtype)


def _select_mask(keys, thr, need, offset, tri):
    gt = jnp.where(keys > thr, 1.0, 0.0)
    eq = jnp.where(keys == thr, 1.0, 0.0)
    prefix = _dot(tri, eq.astype(BF16)) + offset
    return gt + jnp.where(prefix < need, eq, 0.0), offset + _col_sum(eq)


def _dsa_kernel(qi_ref, wT_ref, ki_ref, qa_ref, ka_ref, vT_ref, o_ref, keys_ref, bias_ref, planes_ref,
                *, tq, kc, n_keep):
    t0 = pl.program_id(1) * tq
    n_chunks = (t0 + tq + kc - 1) // kc
    wT = wT_ref[...] * (IDX_HEADS ** -0.5 * IDX_DIM ** -0.5)
    qi_stack = _stack_heads(qi_ref, range(IDX_HEADS))
    keypos = lax.broadcasted_iota(I32, (kc, tq), 0)
    qpos = t0 + lax.broadcasted_iota(I32, (kc, tq), 1)

    def score_chunk(c, carry):
        k0 = pl.multiple_of(c * kc, kc)
        sc = _dot_nt(ki_ref[0, pl.ds(k0, kc), :], qi_stack)
        score = jnp.zeros((kc, tq), F32)
        for h in range(IDX_HEADS):
            score = score + wT[h:h + 1, :] * jnp.maximum(sc[:, h * tq:(h + 1) * tq], 0.0)
        key = jnp.where(k0 + keypos <= qpos, _sort_key(score), INT_MIN)
        keys_ref[c] = key
        flipped = key ^ INT_MIN
        for lvl in range(4):
            byte = lax.shift_right_logical(flipped, 8 * lvl) & 0xFF
            planes_ref[lvl, c] = byte.astype(F32).astype(BF16)
        return carry

    lax.fori_loop(0, n_chunks, score_chunk, 0)

    one = jnp.ones((kc, tq), BF16)
    zero = jnp.zeros((kc, tq), BF16)
    dead = jnp.full((kc, tq), -1.0, BF16)

    def count_where(lvl, pick, also=None):
        def body(c, acc):
            p = planes_ref[lvl, c]
            if also is not None:
                also(c, p)
            return acc + _fold_rows_packed(jnp.where(pick(p), one, zero)).astype(F32)
        acc = lax.fori_loop(0, n_chunks, body, jnp.zeros((PACKED_ROWS, tq), F32))
        return jnp.sum(acc, axis=0, keepdims=True)

    above = jnp.zeros((1, tq), F32)
    thr_u = jnp.zeros((1, tq), I32)
    for lvl in (3, 2, 1, 0):
        def bit_step(it, t, lvl=lvl, above=above):
            cand = t + jnp.left_shift(jnp.int32(1), 7 - it).astype(F32)
            cnt = above + count_where(lvl, lambda p: p >= cand.astype(BF16))
            return jnp.where(cnt >= float(n_keep), cand, t)

        t = lax.fori_loop(0, 8, bit_step, jnp.zeros((1, tq), F32))
        tb = t.astype(BF16)

        def narrow(c, p, lvl=lvl, tb=tb):
            planes_ref[lvl - 1, c] = jnp.where(p == tb, planes_ref[lvl - 1, c], dead)

        above = above + count_where(lvl, lambda p: p > tb, narrow if lvl > 0 else None)
        thr_u = thr_u | jnp.left_shift(t.astype(I32), 8 * lvl)
    thr = thr_u ^ INT_MIN
    need = float(n_keep) - above
    tri = _tri_strict_lower(LANES, BF16)
    sub = lax.broadcasted_iota(I32, (LANES, tq), 0)
    qsub = t0 + lax.broadcasted_iota(I32, (LANES, tq), 1)

    def bias_chunk(c, offset):
        for j in range(kc // LANES):
            rows = slice(j * LANES, (j + 1) * LANES)
            sel, offset = _select_mask(keys_ref[c, rows, :], thr, need, offset, tri)
            causal = c * kc + j * LANES + sub <= qsub
            bias_ref[c, rows, :] = jnp.where(causal, (sel - 1.0) * 1e30, NEG)
        return offset

    lax.fori_loop(0, n_chunks, bias_chunk, jnp.zeros((1, tq), F32))

    accs = _flash_loop(qa_ref, ka_ref, vT_ref, lambda g, c: bias_ref[c], n_chunks, tq, kc)
    for heads, acc in accs:
        _store_heads(o_ref, _normalise(acc), heads, tq)


def _dsa(qi, wT, ki, qa, ka, vaT, bsz, seq, tq, kc):
    n_keep = min(DSA_TOPK_MAX, seq // 4)
    nq = seq // tq
    qblk = lambda w: pl.BlockSpec((1, tq, w), lambda b, i: (b, i, 0))
    full = lambda w: pl.BlockSpec((1, seq, w), lambda b, i: (b, 0, 0))
    return pl.pallas_call(
        functools.partial(_dsa_kernel, tq=tq, kc=kc, n_keep=n_keep),
        out_shape=jax.ShapeDtypeStruct((bsz, seq, A_HEADS * HEAD_DIM), BF16),
        grid=(bsz, nq),
        in_specs=[qblk(1024), pl.BlockSpec((IDX_HEADS, tq), lambda b, i: (0, b * nq + i)), full(LANES),
                  qblk(1024), full(LANES),
                  pl.BlockSpec((A_KV_HEADS, seq // kc, LANES, kc), lambda b, i: (0, b, 0, 0))],
        out_specs=qblk(512),
        scratch_shapes=[pltpu.VMEM((seq // kc, kc, tq), I32), pltpu.VMEM((seq // kc, kc, tq), F32),
                        pltpu.VMEM((4, seq // kc, kc, tq), BF16)],
        compiler_params=_cparams(("parallel", "parallel")),
        name="dsa",
    )(qi, wT, ki, qa, ka, vaT)


def _gelu_tanh(x):
    return 0.5 * x * (1.0 + jnp.tanh(math.sqrt(2.0 / math.pi) * (x + 0.044715 * (x * x * x))))


def _compress_kernel(k_ref, v_ref, pk_ref, pv_ref, w1k_ref, w2k_ref, w1v_ref, w2v_ref,
                     kc_ref, vcT_ref, *, n_rows):
    half = CMP_BLOCK // 2

    def one(x_ref, p_ref, w1_ref, w2_ref):
        outs = []
        for g in range(B_KV_HEADS):
            lo = jnp.zeros((n_rows, CMP_HIDDEN), F32)
            hi = jnp.zeros((n_rows, CMP_HIDDEN), F32)
            for l in range(half):
                xl = x_ref[0, pl.ds(l, n_rows, stride=CMP_STRIDE), :][:, g * HEAD_DIM:(g + 1) * HEAD_DIM]
                a = (xl + p_ref[l:l + 1, :]).astype(BF16)
                b = (xl + p_ref[half + l:half + l + 1, :]).astype(BF16)
                lo = lo + _dot(a, w1_ref[l * HEAD_DIM:(l + 1) * HEAD_DIM, :].astype(BF16))
                hi = hi + _dot(b, w1_ref[(half + l) * HEAD_DIM:(half + l + 1) * HEAD_DIM, :].astype(BF16))
            hid = lo + pltpu.roll(hi, n_rows - 1, axis=0)
            outs.append(_dot(_gelu_tanh(hid).astype(BF16), w2_ref[...].astype(BF16)))
        return outs

    k0, k1 = one(k_ref, pk_ref, w1k_ref, w2k_ref)
    kc_ref[0] = jnp.concatenate([k0, k1], axis=1).astype(kc_ref.dtype)
    for g, v in enumerate(one(v_ref, pv_ref, w1v_ref, w2v_ref)):
        vcT_ref[0, g] = jnp.concatenate([v, jnp.zeros_like(v)], axis=1).T.astype(vcT_ref.dtype)


def _compress(kcmp, vcmp, pos_k, pos_v, w1k, w2k, w1v, w2v):
    bsz, seq, width = kcmp.shape
    n_rows = seq // CMP_STRIDE
    xspec = pl.BlockSpec((1, seq, width), lambda b: (b, 0, 0))
    cst = lambda a: pl.BlockSpec(a.shape, lambda b: (0,) * a.ndim)
    return pl.pallas_call(
        functools.partial(_compress_kernel, n_rows=n_rows),
        out_shape=(jax.ShapeDtypeStruct((bsz, n_rows, LANES), BF16),
                   jax.ShapeDtypeStruct((bsz, B_KV_HEADS, LANES, n_rows), BF16)),
        grid=(bsz,),
        in_specs=[xspec, xspec, cst(pos_k), cst(pos_v), cst(w1k), cst(w2k), cst(w1v), cst(w2v)],
        out_specs=(pl.BlockSpec((1, n_rows, LANES), lambda b: (b, 0, 0)),
                   pl.BlockSpec((1, B_KV_HEADS, LANES, n_rows), lambda b: (b, 0, 0, 0))),
        compiler_params=_cparams(("parallel",)),
        name="compress",
    )(kcmp, vcmp, pos_k, pos_v, w1k, w2k, w1v, w2v)


def _split3(x):
    a = x.astype(BF16)
    r = x - a.astype(F32)
    b = r.astype(BF16)
    c = (r - b.astype(F32)).astype(BF16)
    return a, b, c


def _nsa_kernel(qraw_ref, qrot_ref, kc_ref, vcT_ref, ksel_ref, vselT_ref, kwin_ref, vwinT_ref,
                gT_ref, exp_ref, o_ref, bias_ref, *, tq, kc, seq, n_c):
    t0 = pl.program_id(1) * tq
    n_chunks = (t0 + tq + kc - 1) // kc
    n_s = seq // SEL_BLOCK
    n_pick = min(SEL_COUNT, n_s)
    wt = min(tq, WIN_TQ)
    span = WINDOW + wt
    gT = gT_ref[...]

    ridx = lax.broadcasted_iota(I32, (LANES, tq), 0)
    tl = t0 + lax.broadcasted_iota(I32, (LANES, tq), 1)
    valid_c = jnp.where((ridx * CMP_STRIDE + CMP_BLOCK - 1 <= tl) & (ridx < n_c), 1.0, 0.0)
    valid4 = jnp.concatenate([valid_c] * REP, axis=1) > 0.0
    js = lax.broadcasted_iota(I32, (LANES, LANES), 0) * SEL_BLOCK
    cs = lax.broadcasted_iota(I32, (LANES, LANES), 1) * CMP_STRIDE
    overlap_t = jnp.where((cs <= js + SEL_BLOCK - 1) & (cs + CMP_BLOCK - 1 >= js), 1.0, 0.0).astype(BF16)
    cur = tl // SEL_BLOCK
    forced = (ridx == 0) | ((cur - ridx >= 0) & (cur - ridx < SEL_LOCAL))
    blk_causal = ridx * SEL_BLOCK <= tl
    tri = _tri_strict_lower(LANES, BF16)
    keypos = lax.broadcasted_iota(I32, (kc, tq), 0)
    qpos = t0 + lax.broadcasted_iota(I32, (kc, tq), 1)

    o_cmp_g = []
    for g in range(B_KV_HEADS):
        heads = range(g * REP, (g + 1) * REP)
        s = jnp.where(valid4, _dot_nt(kc_ref[0], _stack_heads(qraw_ref, heads)), NEG)
        e = jnp.where(valid4, jnp.exp(s - _col_max(s)), 0.0)
        p = e * (1.0 / jnp.maximum(_col_sum(e), 1e-30))
        o_cmp_g.append(_dot(vcT_ref[0, g], p.astype(BF16)))
        p_sum = p[:, 0:tq]
        for r in range(1, REP):
            p_sum = p_sum + p[:, r * tq:(r + 1) * tq]
        pa, pb, pc = _split3(p_sum)
        imp = _dot(overlap_t, pa) + _dot(overlap_t, pb) + _dot(overlap_t, pc)
        imp = jnp.where(forced, FORCE, jnp.where(blk_causal, imp, NEG))
        keys = _sort_key(imp[0:n_s, :])
        count_ge = lambda cand, keys=keys: _col_sum(jnp.where(keys >= cand, 1.0, 0.0))
        thr = _kth_largest_key(count_ge, (1, tq), n_pick)
        need = float(n_pick) - _col_sum(jnp.where(keys > thr, 1.0, 0.0))
        blk_sel, _ = _select_mask(keys, thr, need, jnp.zeros((1, tq), F32), tri[0:n_s, 0:n_s])
        blk_sel = jnp.concatenate([blk_sel, jnp.zeros((LANES - n_s, tq), F32)], axis=0).astype(BF16)

        def bias_chunk(c, carry, blk_sel=blk_sel, g=g):
            tok_sel = _dot(exp_ref[c], blk_sel)
            bias_ref[g, c] = jnp.where(c * kc + keypos <= qpos, (tok_sel - 1.0) * 1e30, NEG)
            return carry

        lax.fori_loop(0, n_chunks, bias_chunk, 0)

    sel = _flash_loop(qrot_ref, ksel_ref, vselT_ref, lambda g, c: bias_ref[g, c], n_chunks, tq, kc)

    for (g, heads), (_, acc_sel) in zip(_head_chains(), sel):
        o_win_sub = []
        for sub in range(tq // wt):
            t_sub = t0 + sub * wt
            w0 = pl.multiple_of(jnp.clip(t_sub - WINDOW, 0, seq - span), wt)
            wdiff = ((t_sub + lax.broadcasted_iota(I32, (span, wt), 1))
                     - (w0 + lax.broadcasted_iota(I32, (span, wt), 0)))
            wbias = jnp.where((wdiff >= 0) & (wdiff < WINDOW), 0.0, NEG)
            q_sub = jnp.concatenate([qrot_ref[0, sub * wt:(sub + 1) * wt, h * LANES:(h + 1) * LANES]
                                     for h in heads], axis=0)
            sw = (_dot_nt(kwin_ref[0, pl.ds(w0, span), :], q_sub)
                  + jnp.concatenate([wbias] * HEADS_PER_CHAIN, axis=1))
            ew = jnp.exp(sw - _col_max(sw)).astype(BF16)
            wblk = w0 // LANES
            acc_win = _dot(vwinT_ref[g, wblk], ew[0:LANES, :])
            for j in range(1, span // LANES):
                acc_win = acc_win + _dot(vwinT_ref[g, wblk + j], ew[j * LANES:(j + 1) * LANES, :])
            o_win_sub.append(_normalise(acc_win))

        o_sel = _normalise(acc_sel)
        cols = []
        for r, h in enumerate(heads):
            col = slice(r * tq, (r + 1) * tq)
            cmp_col = slice((h % REP) * tq, (h % REP + 1) * tq)
            o_win = jnp.concatenate([o[:, r * wt:(r + 1) * wt] for o in o_win_sub], axis=1)
            cols.append(gT[3 * h:3 * h + 1, :] * o_cmp_g[g][:, cmp_col]
                        + gT[3 * h + 1:3 * h + 2, :] * o_sel[:, col]
                        + gT[3 * h + 2:3 * h + 3, :] * o_win)
        _store_heads(o_ref, jnp.concatenate(cols, axis=1), heads, tq)


def _nsa(qraw, qrot, kc_x, vcT, ksel, vselT, kwin, vwinT, gT, bsz, seq, tq, kc):
    n_c = (seq - CMP_BLOCK) // CMP_STRIDE + 1
    nq = seq // tq
    key_blk = (jnp.arange(seq, dtype=I32) // SEL_BLOCK).reshape(seq // kc, kc, 1)
    expand = (key_blk == jnp.arange(LANES, dtype=I32).reshape(1, 1, LANES)).astype(BF16)
    qblk = lambda w: pl.BlockSpec((1, tq, w), lambda b, i: (b, i, 0))
    full = lambda w: pl.BlockSpec((1, seq, w), lambda b, i: (b, 0, 0))
    g = B_KV_HEADS
    return pl.pallas_call(
        functools.partial(_nsa_kernel, tq=tq, kc=kc, seq=seq, n_c=n_c),
        out_shape=jax.ShapeDtypeStruct((bsz, seq, B_HEADS * HEAD_DIM), BF16),
        grid=(bsz, nq),
        in_specs=[qblk(1024), qblk(1024),
                  pl.BlockSpec((1,) + kc_x.shape[1:], lambda b, i: (b, 0, 0)),
                  pl.BlockSpec((1,) + vcT.shape[1:], lambda b, i: (b, 0, 0, 0)),
                  full(LANES), pl.BlockSpec((g, seq // kc, LANES, kc), lambda b, i: (0, b, 0, 0)),
                  full(LANES), pl.BlockSpec((g, seq // LANES, LANES, LANES), lambda b, i: (0, b, 0, 0)),
                  pl.BlockSpec((N_GATES, tq), lambda b, i: (0, b * nq + i)),
                  pl.BlockSpec(expand.shape, lambda b, i: (0, 0, 0))],
        out_specs=qblk(512),
        scratch_shapes=[pltpu.VMEM((g, seq // kc, kc, tq), F32)],
        compiler_params=_cparams(("parallel", "parallel")),
        name="nsa",
    )(qraw, qrot, kc_x, vcT, ksel, vselT, kwin, vwinT, gT, expand)


def _pack_pairs(x):
    n = x.shape[1] // 2
    lo = pltpu.bitcast(x[:, :n].astype(BF16).astype(F32), I32)
    hi = pltpu.bitcast(x[:, n:].astype(BF16).astype(F32), I32)
    return lax.shift_right_logical(lo, 16) | (hi & jnp.int32(-65536))


def _unpack_pairs(p):
    lo = pltpu.bitcast(lax.shift_left(p, 16), F32)
    hi = pltpu.bitcast(p & jnp.int32(-65536), F32)
    return jnp.concatenate([lo, hi], axis=1)


def _layer_norm(y, g, b):
    mu = jnp.mean(y, axis=1, keepdims=True)
    yc = y - mu
    var = jnp.mean(yc * yc, axis=1, keepdims=True)
    return yc * lax.rsqrt(var + LN_EPS) * g + b


def _out_proj_kernel(oa_ref, ob_ref, ga_ref, gb_ref, x_ref, mod_ref, wa_ref, wb_ref, wo_ref,
                     g1_ref, b1_ref, wrh_ref, wrl_ref, x1_ref, u2_ref, lg_ref):
    mod = mod_ref[0]
    merged = (ga_ref[...].astype(F32) * _dot(oa_ref[...], wa_ref[...])
              + gb_ref[...].astype(F32) * _dot(ob_ref[...], wb_ref[...]))
    mix = _dot(merged.astype(BF16), wo_ref[...])
    x1 = _layer_norm(DN_ALPHA * x_ref[...] + mod[2:3, :] * mix, g1_ref[...], b1_ref[...])
    x1_ref[...] = x1
    u2 = x1 * (1.0 + mod[4:5, :]) + mod[3:4, :]
    packed = _pack_pairs(u2)
    for j in range(PIECES):
        u2_ref[j] = packed[:, j * SC_ROW:(j + 1) * SC_ROW]
    uh = u2.astype(BF16)
    ul = (u2 - uh.astype(F32)).astype(BF16)
    lg_ref[...] = _dot_nt(wrh_ref[...], uh) + _dot_nt(wrh_ref[...], ul) + _dot_nt(wrl_ref[...], uh)


def _out_proj(oa, ob, ga, gb, x2, mod3, wa, wb, wo, g1, b1, wrh, wrl, seq, tm):
    t, d = x2.shape
    per_b = seq // tm
    row = lambda w: pl.BlockSpec((tm, w), lambda i: (i, 0))
    cst = lambda a: pl.BlockSpec(a.shape, lambda i: (0,) * a.ndim)
    return pl.pallas_call(
        _out_proj_kernel,
        out_shape=(jax.ShapeDtypeStruct((t, d), F32), jax.ShapeDtypeStruct((PIECES, t, SC_ROW), I32),
                   jax.ShapeDtypeStruct((N_EXPERTS, t), F32)),
        grid=(t // tm,),
        in_specs=[row(512), row(512), row(d), row(d), row(d),
                  pl.BlockSpec((1, 6, d), lambda i: (i // per_b, 0, 0)),
                  cst(wa), cst(wb), cst(wo), cst(g1), cst(b1), cst(wrh), cst(wrl)],
        out_specs=(row(d), pl.BlockSpec((PIECES, tm, SC_ROW), lambda i: (0, i, 0)),
                   pl.BlockSpec((N_EXPERTS, tm), lambda i: (0, i))),
        compiler_params=_cparams(("parallel",)),
        name="out_proj",
    )(oa, ob, ga, gb, x2, mod3, wa, wb, wo, g1, b1, wrh, wrl)


def _first_max(x, rows):
    m = jnp.max(x, axis=0, keepdims=True)
    idx = jnp.min(jnp.where(x == m, rows, 1e9), axis=0, keepdims=True)
    return m, idx


def _router_kernel(lg_ref, rb_ref, idx_ref, w_ref, pos_ref, cnt_ref, carry_ref, *, tm):
    @pl.when(pl.program_id(0) == 0)
    def _():
        carry_ref[...] = jnp.zeros_like(carry_ref)

    per_g = N_EXPERTS // N_GROUPS
    scores = _sigmoid(lg_ref[...])
    choice = scores + rb_ref[...][:, 0:1]
    rows = lax.broadcasted_iota(I32, (N_EXPERTS, tm), 0).astype(F32)
    rows_g = lax.broadcasted_iota(I32, (per_g, tm), 0).astype(F32)
    ninf = -jnp.inf

    gs = []
    for g in range(N_GROUPS):
        x = choice[g * per_g:(g + 1) * per_g, :]
        m1, i1 = _first_max(x, rows_g)
        m2 = jnp.max(jnp.where(rows_g == i1, ninf, x), axis=0, keepdims=True)
        gs.append(m1 + m2)
    gscore = jnp.concatenate(gs, axis=0)
    rows8 = lax.broadcasted_iota(I32, (N_GROUPS, tm), 0).astype(F32)
    keep = jnp.zeros((N_GROUPS, tm), F32)
    for _ in range(TOPK_GROUPS):
        _, gi = _first_max(gscore, rows8)
        hit = rows8 == gi
        keep = jnp.where(hit, 1.0, keep)
        gscore = jnp.where(hit, ninf, gscore)
    keep_full = jnp.concatenate(
        [jnp.broadcast_to(keep[g:g + 1, :], (per_g, tm)) for g in range(N_GROUPS)], axis=0)
    masked = jnp.where(keep_full > 0.0, choice, NEG)

    idxs, ws = [], []
    onehot = jnp.zeros((N_EXPERTS, tm), F32)
    for _ in range(TOP_K):
        _, ei = _first_max(masked, rows)
        hit = rows == ei
        idxs.append(ei)
        ws.append(jnp.sum(jnp.where(hit, scores, 0.0), axis=0, keepdims=True))
        masked = jnp.where(hit, ninf, masked)
        onehot = jnp.where(hit, 1.0, onehot)
    idx = jnp.concatenate(idxs, axis=0)
    w = jnp.concatenate(ws, axis=0)
    idx_ref[...] = idx.astype(I32)
    w = w / jnp.sum(w, axis=0, keepdims=True) * ROUTED_SCALE
    w_ref[...] = jnp.concatenate([w, jnp.zeros((LANES - TOP_K, tm), F32)], axis=0).T

    tri = _tri_strict_upper(tm, BF16)
    base = _dot(onehot.astype(BF16), tri) + carry_ref[...][:, 0:1]
    pos = [jnp.sum(jnp.where(rows == idxs[k], base, 0.0), axis=0, keepdims=True) for k in range(TOP_K)]
    pos_ref[...] = jnp.concatenate(pos, axis=0).astype(I32)
    carry = carry_ref[...] + jnp.sum(onehot, axis=1, keepdims=True)
    carry_ref[...] = carry
    cnt_ref[...] = carry


def _router(lg, router_bias, tm):
    e, t = lg.shape
    rb = jnp.broadcast_to(router_bias.reshape(e, 1).astype(F32), (e, LANES))
    tok = lambda r: pl.BlockSpec((r, tm), lambda i: (0, i))
    return pl.pallas_call(
        functools.partial(_router_kernel, tm=tm),
        out_shape=(jax.ShapeDtypeStruct((TOP_K, t), I32), jax.ShapeDtypeStruct((t, LANES), F32),
                   jax.ShapeDtypeStruct((TOP_K, t), I32), jax.ShapeDtypeStruct((e, LANES), F32)),
        grid=(t // tm,),
        in_specs=[tok(e), pl.BlockSpec((e, LANES), lambda i: (0, 0))],
        out_specs=(tok(TOP_K), pl.BlockSpec((tm, LANES), lambda i: (i, 0)), tok(TOP_K),
                   pl.BlockSpec((e, LANES), lambda i: (0, 0))),
        scratch_shapes=[pltpu.VMEM((e, LANES), F32)],
        compiler_params=_cparams(("arbitrary",)),
        name="router",
    )(lg, rb)


def _dest_kernel(idx_ref, pos_ref, st_ref, dest_ref, *, tm):
    rows = lax.broadcasted_iota(I32, (N_EXPERTS, tm), 0)
    starts = st_ref[...][:, 0:1]
    idx = idx_ref[...]
    out = []
    for k in range(TOP_K):
        out.append(jnp.sum(jnp.where(rows == idx[k:k + 1, :], starts, 0.0), axis=0, keepdims=True))
    dest_ref[...] = jnp.concatenate(out, axis=0).astype(I32) + pos_ref[...]


def _dest(idx, pos, starts, tm):
    k, t = idx.shape
    st = jnp.broadcast_to(starts.reshape(N_EXPERTS, 1).astype(F32), (N_EXPERTS, LANES))
    tok = pl.BlockSpec((k, tm), lambda i: (0, i))
    return pl.pallas_call(
        functools.partial(_dest_kernel, tm=tm),
        out_shape=jax.ShapeDtypeStruct((k, t), I32),
        grid=(t // tm,),
        in_specs=[tok, tok, pl.BlockSpec((N_EXPERTS, LANES), lambda i: (0, 0))],
        out_specs=tok,
        compiler_params=_cparams(("parallel",)),
        name="dest",
    )(idx, pos, st)


def _experts_kernel(blk_ref, used_ref, first_ref, slot_ref, next_ref, x_ref, wg_hbm, wu_hbm, wd_hbm, y_ref,
                    wg_buf, wu_buf, wd_buf, wg_bf, wu_bf, wd_bf, sems):
    b = pl.program_id(0)
    active = b < used_ref[0]

    def weight_copies(e, slot):
        return [pltpu.make_async_copy(hbm.at[e], buf.at[slot], sems.at[slot, i])
                for i, (hbm, buf) in enumerate(((wg_hbm, wg_buf), (wu_hbm, wu_buf), (wd_hbm, wd_buf)))]

    @pl.when(b == 0)
    def _():
        for cp in weight_copies(blk_ref[0], 0):
            cp.start()

    @pl.when(active & (first_ref[b] == 1))
    def _():
        slot = slot_ref[b]
        for cp in weight_copies(blk_ref[b], slot):
            cp.wait()
        nxt = next_ref[b]

        @pl.when(nxt >= 0)
        def _():
            for cp in weight_copies(nxt, 1 - slot):
                cp.start()

        wg_bf[...] = wg_buf[slot].astype(BF16)
        wu_bf[...] = wu_buf[slot].astype(BF16)
        wd_bf[...] = wd_buf[slot].astype(BF16)

    @pl.when(active)
    def _():
        x = jnp.concatenate([x_ref[j] for j in range(PIECES)], axis=1)
        x = _unpack_pairs(x).astype(BF16)
        a = _dot(x, wg_bf[...])
        u = _dot(x, wu_bf[...])
        h = (a * _sigmoid(a) * u).astype(BF16)
        y = _pack_pairs(_dot(h, wd_bf[...]))
        for j in range(PIECES):
            y_ref[j] = y[:, j * SC_ROW:(j + 1) * SC_ROW]


def _experts(xs, plan, wg, wu, wd, bm):
    _, cap, _ = xs.shape
    n_blocks = cap // bm
    d, f = wg.shape[1], wg.shape[2]
    rows = pl.BlockSpec((PIECES, bm, SC_ROW), lambda b, blk, used, *_: (0, jnp.minimum(b, used[0] - 1), 0))
    hbm = pl.BlockSpec(memory_space=pl.ANY)
    return pl.pallas_call(
        _experts_kernel,
        out_shape=jax.ShapeDtypeStruct(xs.shape, I32),
        grid_spec=pltpu.PrefetchScalarGridSpec(
            num_scalar_prefetch=5,
            grid=(n_blocks,),
            in_specs=[rows, hbm, hbm, hbm],
            out_specs=rows,
            scratch_shapes=[pltpu.VMEM((2, d, f), F32), pltpu.VMEM((2, d, f), F32), pltpu.VMEM((2, f, d), F32),
                            pltpu.VMEM((d, f), BF16), pltpu.VMEM((d, f), BF16), pltpu.VMEM((f, d), BF16),
                            pltpu.SemaphoreType.DMA((2, 3))]),
        compiler_params=_cparams(("arbitrary",)),
        name="experts",
    )(plan["blk_e"], plan["n_used"], plan["first"], plan["slot"], plan["next_e"], xs, wg, wu, wd)


def _final_kernel(x1_ref, u2_ref, yg_ref, w_ref, mod_ref, sg_ref, su_ref, sd_ref, g2_ref, b2_ref, o_ref):
    mod = mod_ref[0]
    w = w_ref[...]
    rows = lambda ref, *lead: jnp.concatenate([ref[(j,) + lead] for j in range(PIECES)], axis=1)
    routed = w[:, 0:1] * _unpack_pairs(rows(yg_ref, 0))
    for k in range(1, TOP_K):
        routed = routed + w[:, k:k + 1] * _unpack_pairs(rows(yg_ref, k))
    u = _unpack_pairs(rows(u2_ref)).astype(BF16)
    a = _dot(u, sg_ref[...])
    b = _dot(u, su_ref[...])
    shared = _dot((a * _sigmoid(a) * b).astype(BF16), sd_ref[...])
    y = DN_ALPHA * x1_ref[...] + mod[5:6, :] * (routed + shared)
    o_ref[...] = _layer_norm(y, g2_ref[...], b2_ref[...])


def _final(x1, u2p, yg, wtok, mod3, sg, su, sd, g2, b2, seq, tm):
    t, d = x1.shape
    per_b = seq // tm
    row = lambda w: pl.BlockSpec((tm, w), lambda i: (i, 0))
    cst = lambda a: pl.BlockSpec(a.shape, lambda i: (0,) * a.ndim)
    return pl.pallas_call(
        _final_kernel,
        out_shape=jax.ShapeDtypeStruct((t, d), F32),
        grid=(t // tm,),
        in_specs=[row(d), pl.BlockSpec((PIECES, tm, SC_ROW), lambda i: (0, i, 0)),
                  pl.BlockSpec((PIECES, TOP_K, tm, SC_ROW), lambda i: (0, 0, i, 0)),
                  row(LANES), pl.BlockSpec((1, 6, d), lambda i: (i // per_b, 0, 0)),
                  cst(sg), cst(su), cst(sd), cst(g2), cst(b2)],
        out_specs=row(d),
        compiler_params=_cparams(("parallel",)),
        name="final",
    )(x1, u2p, yg, wtok, mod3, sg, su, sd, g2, b2)


def _sc_mesh():
    return plsc.VectorSubcoreMesh(core_axis_name="core", subcore_axis_name="subcore")


def _sc_scatter_rows(src, idx, n_out):
    n_copies, n = idx.shape

    @functools.partial(pl.kernel, out_type=jax.ShapeDtypeStruct((n_out, SC_ROW), src.dtype),
                       mesh=_sc_mesh(), scratch_types=[])
    def k(x_hbm, i_hbm, o_hbm):
        def body(x_vmem, i_vmem):
            for c in range(n_copies):
                pltpu.sync_copy(x_vmem, o_hbm.at[i_vmem.at[c]])

        pltpu.emit_pipeline(
            body, grid=(n // SC_WINDOW,),
            in_specs=[pl.BlockSpec((SC_WINDOW, SC_ROW), lambda i: (i, 0)),
                      pl.BlockSpec((n_copies, SC_WINDOW), lambda i: (0, i))],
            out_specs=[],
            core_axis_name=("core", "subcore"),
            dimension_semantics=(pltpu.PARALLEL,),
        )(x_hbm, i_hbm)

    return k(src, idx)


def _sc_gather_rows(src, idx):
    n_idx = idx.shape[0]

    @functools.partial(pl.kernel, out_type=jax.ShapeDtypeStruct((n_idx, SC_ROW), src.dtype),
                       mesh=_sc_mesh(), scratch_types=[])
    def k(x_hbm, i_hbm, o_hbm):
        def body(i_vmem, o_vmem):
            pltpu.sync_copy(x_hbm.at[i_vmem.at[0]], o_vmem)

        pltpu.emit_pipeline(
            body, grid=(n_idx // SC_WINDOW,),
            in_specs=[pl.BlockSpec((1, SC_WINDOW), lambda i: (0, i))],
            out_specs=[pl.BlockSpec((SC_WINDOW, SC_ROW), lambda i: (i, 0))],
            core_axis_name=("core", "subcore"),
            dimension_semantics=(pltpu.PARALLEL,),
        )(i_hbm, o_hbm)

    return k(src, idx.reshape(1, n_idx))


def _moe_plan(counts, n_tok):
    bm = BM_EXPERT
    padded = (counts + bm - 1) // bm * bm
    p_ends = jnp.cumsum(padded)
    starts = p_ends - padded
    n_blocks = n_tok * TOP_K // bm + N_EXPERTS
    blk = jnp.arange(n_blocks, dtype=I32)
    blk_e = jnp.minimum(jnp.sum(p_ends[None, :] <= (blk * bm)[:, None], axis=1), N_EXPERTS - 1).astype(I32)
    n_used = (p_ends[-1] // bm).astype(I32)
    prev = jnp.concatenate([jnp.full((1,), -1, I32), blk_e[:-1]])
    first = ((blk_e != prev) & (blk < n_used)).astype(I32)
    slot = (jnp.cumsum(first) - 1) % 2
    first_pos = jnp.where(first == 1, blk, n_blocks)
    next_first = lax.cummin(jnp.concatenate([first_pos[1:], jnp.full((1,), n_blocks, I32)]), reverse=True)
    next_e = jnp.where(next_first < n_blocks, blk_e[jnp.minimum(next_first, n_blocks - 1)], -1)
    plan = dict(blk_e=blk_e, n_used=n_used.reshape(1), first=first, slot=slot.astype(I32),
                next_e=next_e.astype(I32))
    return starts, plan, n_blocks


def _layer(x, mod, positions, w_in, w_br_a, w_br_b, w_out, cmp_pos_k, cmp_pos_v, cmp_k_w1, cmp_k_w2,
           cmp_v_w1, cmp_v_w2, ln1_g, ln1_b, w_router, router_bias, w_exp_gate, w_exp_up, w_exp_down,
           w_sh_gate, w_sh_up, w_sh_down, ln2_g, ln2_b):
    bsz, seq, d = x.shape
    t = bsz * seq
    assert seq // CMP_STRIDE == LANES and seq % TQ_ATTN == 0
    assert seq % TM_PROJ == 0 and KC_ATTN == TM_PROJ
    x2 = x.reshape(t, d)
    mod3 = mod.reshape(bsz, 6, d)

    w_pack, w_small = _pack_w_in(w_in)
    z = _in_proj(x2, mod3, w_pack, w_small, _rope_table(positions), seq, TM_PROJ)
    per_b = lambda name: z[name].reshape(bsz, seq, z[name].shape[1])

    o_a = _dsa(per_b("qi"), z["wT"], per_b("ki"), per_b("qa"), per_b("ka"), z["vaT"],
               bsz, seq, TQ_ATTN, KC_ATTN)

    kc_x, vcT = _compress(per_b("kcmp"), per_b("vcmp"),
                          cmp_pos_k, cmp_pos_v, cmp_k_w1, cmp_k_w2, cmp_v_w1, cmp_v_w2)
    o_b = _nsa(per_b("qbraw"), per_b("qbrot"), kc_x, vcT, per_b("ksel"), z["vselT"], per_b("kwin"),
               z["vwinT"], z["gT"], bsz, seq, TQ_ATTN, KC_ATTN)

    wr_hi = w_router.T.astype(BF16)
    wr_lo = (w_router.T - wr_hi.astype(F32)).astype(BF16)
    x1, u2p, logits = _out_proj(
        o_a.reshape(t, -1), o_b.reshape(t, -1), z["ga"], z["gb"], x2, mod3,
        w_br_a.astype(BF16), w_br_b.astype(BF16), w_out.astype(BF16),
        ln1_g.reshape(1, d), ln1_b.reshape(1, d), wr_hi, wr_lo, seq, TM_PROJ)

    idx, wtok, pos, counts = _router(logits, router_bias, TM_ROUTE)
    starts, plan, n_blocks = _moe_plan(counts[:, 0].astype(I32), t)
    dest = _dest(idx, pos, starts, TM_ROUTE)

    cap = n_blocks * BM_EXPERT
    dest_p = dest[None] + (jnp.arange(PIECES, dtype=I32) * cap).reshape(PIECES, 1, 1)
    xs = _sc_scatter_rows(u2p.reshape(PIECES * t, SC_ROW),
                          jnp.swapaxes(dest_p, 0, 1).reshape(TOP_K, PIECES * t), cap * PIECES)
    ys = _experts(xs.reshape(PIECES, cap, SC_ROW), plan, w_exp_gate, w_exp_up, w_exp_down, BM_EXPERT)
    yg = _sc_gather_rows(ys.reshape(cap * PIECES, SC_ROW), dest_p.reshape(-1)
                         ).reshape(PIECES, TOP_K, t, SC_ROW)

    return _final(x1, u2p, yg, wtok, mod3, w_sh_gate.astype(BF16), w_sh_up.astype(BF16),
                  w_sh_down.astype(BF16), ln2_g.reshape(1, d), ln2_b.reshape(1, d), seq, TM_PROJ
                  ).reshape(bsz, seq, d)


def kernel(x, c, positions, w_ada, b_ada, w_in, w_br_a, w_br_b, w_out, cmp_pos_k, cmp_pos_v, cmp_k_w1,
           cmp_k_w2, cmp_v_w1, cmp_v_w2, ln1_g, ln1_b, w_router, router_bias, w_exp_gate, w_exp_up,
           w_exp_down, w_sh_gate, w_sh_up, w_sh_down, ln2_g, ln2_b):
    for l in range(w_ada.shape[0]):
        mod = _mod(c, w_ada[l], b_ada[l])
        x = _layer(x, mod, positions, w_in[l], w_br_a[l], w_br_b[l], w_out[l], cmp_pos_k[l], cmp_pos_v[l],
                   cmp_k_w1[l], cmp_k_w2[l], cmp_v_w1[l], cmp_v_w2[l], ln1_g[l], ln1_b[l], w_router[l],
                   router_bias[l], w_exp_gate[l], w_exp_up[l], w_exp_down[l], w_sh_gate[l], w_sh_up[l],
                   w_sh_down[l], ln2_g[l], ln2_b[l])
    return x
```

```python
import functools
import math

import jax
import jax.numpy as jnp
from jax import lax
from jax.experimental import pallas as pl
from jax.experimental.pallas import tpu as pltpu
from jax.experimental.pallas import tpu_sc as plsc

F32 = jnp.float32
BF16 = jnp.bfloat16
I32 = jnp.int32

D_MODEL = 1024
HEAD_DIM = 64
ROPE_THETA = 500000.0
ROPE_FRACTION = 4
A_HEADS = 8
A_KV_HEADS = 2
IDX_HEADS = 8
IDX_DIM = 32
DSA_TOPK_MAX = 256
B_HEADS = 8
B_KV_HEADS = 2
REP = 4
CMP_BLOCK = 32
CMP_STRIDE = 16
CMP_HIDDEN = 256
SEL_BLOCK = 64
SEL_COUNT = 16
SEL_LOCAL = 2
WINDOW = 512
N_EXPERTS = 256
TOP_K = 8
N_GROUPS = 8
TOPK_GROUPS = 4
ROUTED_SCALE = 2.5
DEPTH = 1
DN_ALPHA = (2 * DEPTH) ** 0.25
LN_EPS = 1e-5
NEG = -1e30
FORCE = 1e9
INT_MIN = -2147483648
N_GATES = 3 * B_HEADS

LANES = 128
SUBLANES = 8
VMEM_LIMIT = 56 * 1024 * 1024
SC_WINDOW = 128
SC_ROW = 256
PIECES = (D_MODEL // 2) // SC_ROW

TM_PROJ = 512
TQ_ATTN = 512
WIN_TQ = 256
KC_ATTN = 512
TM_ROUTE = 512
BM_EXPERT = 512

_IN_SLOTS = (("qa", A_HEADS * HEAD_DIM), ("ka", A_KV_HEADS * HEAD_DIM), ("va", A_KV_HEADS * HEAD_DIM),
             ("qi", IDX_HEADS * IDX_DIM), ("ki", IDX_DIM), ("wi", IDX_HEADS), ("qb", B_HEADS * HEAD_DIM),
             ("kcmp", B_KV_HEADS * HEAD_DIM), ("vcmp", B_KV_HEADS * HEAD_DIM), ("ksel", B_KV_HEADS * HEAD_DIM),
             ("vsel", B_KV_HEADS * HEAD_DIM), ("kwin", B_KV_HEADS * HEAD_DIM), ("vwin", B_KV_HEADS * HEAD_DIM),
             ("gnsa", N_GATES), ("ga", D_MODEL), ("gb", D_MODEL))


def _col_ranges(slots, align):
    out, start = {}, 0
    for name, width in slots:
        out[name] = (start, start + -(-width // align) * align)
        start = out[name][1]
    return out


_IN_COLS = _col_ranges(_IN_SLOTS, 1)
_PACK_COLS = _col_ranges([s for s in _IN_SLOTS if s[0] not in ("wi", "gnsa")], LANES)

NT_DIMS = (((1,), (1,)), ((), ()))


def _cparams(sem):
    return pltpu.CompilerParams(dimension_semantics=sem, vmem_limit_bytes=VMEM_LIMIT)


def _sigmoid(x):
    return 1.0 / (1.0 + jnp.exp(-x))


def _dot(a, b):
    return jnp.dot(a, b, preferred_element_type=F32)


def _dot_nt(a, b):
    return lax.dot_general(a, b, NT_DIMS, preferred_element_type=F32)


def _sort_key(x):
    x = jnp.where(x == 0.0, 0.0, x)
    bits = pltpu.bitcast(x, I32)
    return jnp.where(bits < 0, bits ^ 0x7FFFFFFF, bits)


def _kth_largest_key(count_ge, shape, k):
    kf = float(k)
    t0 = jnp.where(count_ge(jnp.zeros(shape, I32)) >= kf, 0, INT_MIN).astype(I32)

    def body(it, t):
        cand = t + jnp.left_shift(jnp.int32(1), 30 - it)
        return jnp.where(count_ge(cand) >= kf, cand, t)

    return lax.fori_loop(0, 31, body, t0)


def _tri_strict_lower(n, dtype):
    r = lax.broadcasted_iota(I32, (n, n), 0)
    c = lax.broadcasted_iota(I32, (n, n), 1)
    return jnp.where(c < r, 1.0, 0.0).astype(dtype)


def _tri_strict_upper(n, dtype):
    r = lax.broadcasted_iota(I32, (n, n), 0)
    c = lax.broadcasted_iota(I32, (n, n), 1)
    return jnp.where(r < c, 1.0, 0.0).astype(dtype)


def _mod_kernel(c_ref, w_ref, b_ref, o_ref):
    c = c_ref[...]
    cond = (c * _sigmoid(c)).astype(BF16)
    o_ref[...] = _dot(cond, w_ref[...].astype(BF16)) + b_ref[...]


def _mod(c, w_ada, b_ada):
    bsz, d = c.shape
    n = w_ada.shape[1]
    tn = 1024
    return pl.pallas_call(
        _mod_kernel,
        out_shape=jax.ShapeDtypeStruct((bsz, n), F32),
        grid=(n // tn,),
        in_specs=[pl.BlockSpec((bsz, d), lambda j: (0, 0)),
                  pl.BlockSpec((d, tn), lambda j: (0, j)),
                  pl.BlockSpec((1, tn), lambda j: (0, j))],
        out_specs=pl.BlockSpec((bsz, tn), lambda j: (0, j)),
        compiler_params=_cparams(("parallel",)),
        name="mod",
    )(c, w_ada, b_ada.reshape(1, n))


def _rope(z, c_tab, s_tab, period, half):
    w = z.shape[1]
    reps = w // LANES
    c = jnp.concatenate([c_tab] * reps, axis=1) if reps > 1 else c_tab
    s = jnp.concatenate([s_tab] * reps, axis=1) if reps > 1 else s_tab
    lane = lax.broadcasted_iota(I32, z.shape, 1)
    first = (lane & (period - 1)) < half
    partner = jnp.where(first, pltpu.roll(z, w - half, axis=1), pltpu.roll(z, half, axis=1))
    return z * c + partner * s


def _in_proj_kernel(x_ref, mod_ref, w_ref, wsm_ref, rope_ref,
                    qa_ref, ka_ref, vaT_ref, qi_ref, ki_ref, wT_ref, qbraw_ref, qbrot_ref,
                    kcmp_ref, vcmp_ref, ksel_ref, vselT_ref, kwin_ref, vwinT_ref, gT_ref, ga_ref, gb_ref):
    mod = mod_ref[0]
    u = (x_ref[...] * (1.0 + mod[1:2, :]) + mod[0:1, :]).astype(BF16)
    tm = u.shape[0]
    c64, s64 = _rope_patterns(rope_ref[...], HEAD_DIM)
    c32, s32 = _rope_patterns(rope_ref[...], IDX_DIM)
    scale = HEAD_DIM ** -0.5
    lane = lax.broadcasted_iota(I32, (tm, LANES), 1)
    low = lane < HEAD_DIM

    def proj(name):
        a, b = _PACK_COLS[name]
        return _dot(u, w_ref[:, a:b])

    rope64 = lambda z: _rope(z, c64, s64, HEAD_DIM, HEAD_DIM // ROPE_FRACTION // 2)
    rope32 = lambda z: _rope(z, c32, s32, IDX_DIM, IDX_DIM // ROPE_FRACTION // 2)

    def head_slots64(z):
        out = []
        for h in range(A_HEADS):
            pair = z[:, (h // 2) * LANES:(h // 2 + 1) * LANES]
            g = h // REP
            src = pair if h % 2 == g else pltpu.roll(pair, HEAD_DIM, axis=1)
            out.append(jnp.where(low, src, 0.0) if g == 0 else jnp.where(low, 0.0, src))
        return jnp.concatenate(out, axis=1).astype(BF16)

    def head_slots32(z):
        per = LANES // IDX_DIM
        out = []
        for h in range(IDX_HEADS):
            col = z[:, (h // per) * LANES:(h // per + 1) * LANES]
            shift = IDX_DIM * (h % per)
            src = col if shift == 0 else pltpu.roll(col, LANES - shift, axis=1)
            out.append(jnp.where(lane < IDX_DIM, src, 0.0))
        return jnp.concatenate(out, axis=1).astype(BF16)

    def store_vt(ref, z, chunk):
        zt = z.T
        ones = jnp.ones((HEAD_DIM, chunk), F32)
        for g in range(A_KV_HEADS):
            for j in range(tm // chunk):
                blk = zt[g * HEAD_DIM:(g + 1) * HEAD_DIM, j * chunk:(j + 1) * chunk]
                ref[g, j] = jnp.concatenate([blk, ones], axis=0).astype(BF16)

    qa_ref[...] = head_slots64(rope64(proj("qa")) * scale)
    ka_ref[...] = rope64(proj("ka")).astype(BF16)
    store_vt(vaT_ref, proj("va"), tm)
    qi_ref[...] = head_slots32(rope32(proj("qi")))
    ki_ref[...] = rope32(proj("ki")).astype(BF16)
    qb = proj("qb")
    qbraw_ref[...] = head_slots64(qb * scale)
    qbrot_ref[...] = head_slots64(rope64(qb) * scale)
    kcmp_ref[...] = proj("kcmp")
    vcmp_ref[...] = proj("vcmp")
    ksel_ref[...] = rope64(proj("ksel")).astype(BF16)
    store_vt(vselT_ref, proj("vsel"), tm)
    kwin_ref[...] = rope64(proj("kwin")).astype(BF16)
    store_vt(vwinT_ref, proj("vwin"), LANES)
    ga_ref[...] = _sigmoid(proj("ga")).astype(BF16)
    gb_ref[...] = _sigmoid(proj("gb")).astype(BF16)
    small = _dot_nt(wsm_ref[...], u)
    wT_ref[...] = small[0:IDX_HEADS, :]
    gT_ref[...] = _sigmoid(small[IDX_HEADS:IDX_HEADS + N_GATES, :])


def _pack_w_in(w_in):
    d = w_in.shape[0]
    col = lambda name: w_in[:, _IN_COLS[name][0]:_IN_COLS[name][1]]
    parts = []
    for name, (a, b) in _PACK_COLS.items():
        c = col(name)
        parts.append(jnp.pad(c, ((0, 0), (0, b - a - c.shape[1]))))
    w_small = jnp.concatenate([col("wi"), col("gnsa")], axis=1).T
    return jnp.concatenate(parts, axis=1).astype(BF16), w_small.astype(BF16)


def _rope_halves(dim):
    return dim // ROPE_FRACTION // 2


_ROPE_LANES = {HEAD_DIM: 0, IDX_DIM: 2 * _rope_halves(HEAD_DIM)}


def _rope_table(positions):
    pos = positions.astype(F32).reshape(-1, 1)
    parts = []
    for dim in (HEAD_DIM, IDX_DIM):
        half = _rope_halves(dim)
        inv = ROPE_THETA ** (-(jnp.arange(half, dtype=F32) * 2.0) / (2 * half))
        ang = pos * inv
        parts += [jnp.cos(ang), jnp.sin(ang)]
    tab = jnp.concatenate(parts, axis=1)
    return jnp.pad(tab, ((0, 0), (0, LANES - tab.shape[1])))


def _rope_patterns(tab, dim):
    half = _rope_halves(dim)
    src = _ROPE_LANES[dim]
    lane = lax.broadcasted_iota(I32, tab.shape, 1)

    def spread(at, negate_first):
        v = jnp.where((lane >= at) & (lane < at + half), tab, 0.0)
        v = pltpu.roll(v, (LANES - at) % LANES, axis=1) if at else v
        rep = (-v if negate_first else v) + pltpu.roll(v, half, axis=1)
        period = dim
        while period < LANES:
            rep = rep + pltpu.roll(rep, period, axis=1)
            period *= 2
        return rep

    rotated = (lane & (dim - 1)) < 2 * half
    return jnp.where(rotated, spread(src, False), 1.0), spread(src + half, True)


def _in_proj(x2, mod3, w_pack, w_small, rope_tab, seq, tm):
    t, d = x2.shape
    n = w_pack.shape[1]
    per_b = seq // tm
    g = A_KV_HEADS
    row = lambda w: pl.BlockSpec((tm, w), lambda i: (i, 0))
    tok = lambda r: pl.BlockSpec((r, tm), lambda i: (0, i))
    vt_chunk = pl.BlockSpec((g, 1, LANES, tm), lambda i: (0, i, 0, 0))
    vt_lane = pl.BlockSpec((g, tm // LANES, LANES, LANES), lambda i: (0, i, 0, 0))
    sds = jax.ShapeDtypeStruct
    vt_chunk_shape = sds((g, t // tm, LANES, tm), BF16)
    outs = (("qa", sds((t, 1024), BF16), row(1024)), ("ka", sds((t, LANES), BF16), row(LANES)),
            ("vaT", vt_chunk_shape, vt_chunk), ("qi", sds((t, 1024), BF16), row(1024)),
            ("ki", sds((t, LANES), BF16), row(LANES)), ("wT", sds((IDX_HEADS, t), F32), tok(IDX_HEADS)),
            ("qbraw", sds((t, 1024), BF16), row(1024)), ("qbrot", sds((t, 1024), BF16), row(1024)),
            ("kcmp", sds((t, LANES), F32), row(LANES)), ("vcmp", sds((t, LANES), F32), row(LANES)),
            ("ksel", sds((t, LANES), BF16), row(LANES)), ("vselT", vt_chunk_shape, vt_chunk),
            ("kwin", sds((t, LANES), BF16), row(LANES)),
            ("vwinT", sds((g, t // LANES, LANES, LANES), BF16), vt_lane),
            ("gT", sds((N_GATES, t), F32), tok(N_GATES)),
            ("ga", sds((t, d), BF16), row(d)), ("gb", sds((t, d), BF16), row(d)))
    res = pl.pallas_call(
        _in_proj_kernel,
        out_shape=tuple(o[1] for o in outs),
        grid=(t // tm,),
        in_specs=[row(d),
                  pl.BlockSpec((1, 6, d), lambda i: (i // per_b, 0, 0)),
                  pl.BlockSpec((d, n), lambda i: (0, 0)),
                  pl.BlockSpec(w_small.shape, lambda i: (0, 0)),
                  row(LANES)],
        out_specs=tuple(o[2] for o in outs),
        compiler_params=_cparams(("parallel",)),
        name="in_proj",
    )(x2, mod3, w_pack, w_small, rope_tab)
    return {o[0]: r for o, r in zip(outs, res)}


def _fold_rows(x, op):
    n = x.shape[0]
    while n % (2 * SUBLANES) == 0:
        n //= 2
        x = op(x[:n], x[n:])
    slabs = [x[i * SUBLANES:(i + 1) * SUBLANES] for i in range(n // SUBLANES)]
    while len(slabs) > 1:
        nxt = [op(slabs[i], slabs[i + 1]) for i in range(0, len(slabs) - 1, 2)]
        slabs = nxt + ([slabs[-1]] if len(slabs) % 2 else [])
    return slabs[0]


def _col_max(x):
    return jnp.max(_fold_rows(x, jnp.maximum), axis=0, keepdims=True)


def _col_sum(x):
    return jnp.sum(_fold_rows(x, jnp.add), axis=0, keepdims=True)


PACKED_ROWS = 16


def _fold_rows_packed(x):
    n = x.shape[0]
    assert n % PACKED_ROWS == 0 and n // PACKED_ROWS <= 256
    while n > PACKED_ROWS:
        n //= 2
        x = x[:n] + x[n:]
    return x


def _stack_heads(q_ref, heads):
    return jnp.concatenate([q_ref[0, :, h * LANES:(h + 1) * LANES] for h in heads], axis=0)


def _flash_step(k, q_stack, v_t, bias4, m, acc):
    s = _dot_nt(k, q_stack) + bias4
    m_new = jnp.maximum(m, _col_max(s))
    e = jnp.exp(s - m_new).astype(BF16)
    return m_new, acc * jnp.exp(m - m_new) + _dot(v_t, e)


HEADS_PER_CHAIN = REP


def _head_chains():
    return [(h // REP, tuple(range(h, h + HEADS_PER_CHAIN))) for h in range(0, A_HEADS, HEADS_PER_CHAIN)]


def _flash_loop(q_ref, k_ref, vT_ref, bias_of, n_chunks, tq, kc):
    chains = _head_chains()
    q_stacks = [_stack_heads(q_ref, heads) for _, heads in chains]
    width = HEADS_PER_CHAIN * tq

    def body(c, carry):
        k = k_ref[0, pl.ds(pl.multiple_of(c * kc, kc), kc), :]
        out = []
        for (g, _), q_stack, (m, acc) in zip(chains, q_stacks, carry):
            bias = jnp.concatenate([bias_of(g, c)] * HEADS_PER_CHAIN, axis=1)
            out.append(_flash_step(k, q_stack, vT_ref[g, c], bias, m, acc))
        return tuple(out)

    init = tuple((jnp.full((1, width), NEG, F32), jnp.zeros((LANES, width), F32)) for _ in chains)
    res = lax.fori_loop(0, n_chunks, body, init)
    return [(heads, acc) for (_, heads), (_, acc) in zip(chains, res)]


def _normalise(acc):
    return acc * (1.0 / jnp.maximum(acc[HEAD_DIM:HEAD_DIM + 1, :], 1e-30))


def _store_heads(o_ref, o_t, heads, tq):
    o = o_t.T
    for r, h in enumerate(heads):
        o_ref[0, :, h * HEAD_DIM:(h + 1) * HEAD_DIM] = o[r * tq:(r + 1) * tq, 0:HEAD_DIM].astype(o_ref.dtype)


def _select_mask(keys, thr, need, offset, tri):
    gt = jnp.where(keys > thr, 1.0, 0.0)
    eq = jnp.where(keys == thr, 1.0, 0.0)
    prefix = _dot(tri, eq.astype(BF16)) + offset
    return gt + jnp.where(prefix < need, eq, 0.0), offset + _col_sum(eq)


def _dsa_kernel(qi_ref, wT_ref, ki_ref, qa_ref, ka_ref, vT_ref, o_ref, keys_ref, bias_ref, planes_ref,
                *, tq, kc, n_keep):
    t0 = pl.program_id(1) * tq
    n_chunks = (t0 + tq + kc - 1) // kc
    wT = wT_ref[...] * (IDX_HEADS ** -0.5 * IDX_DIM ** -0.5)
    qi_stack = _stack_heads(qi_ref, range(IDX_HEADS))
    keypos = lax.broadcasted_iota(I32, (kc, tq), 0)
    qpos = t0 + lax.broadcasted_iota(I32, (kc, tq), 1)

    def score_chunk(c, carry):
        k0 = pl.multiple_of(c * kc, kc)
        sc = _dot_nt(ki_ref[0, pl.ds(k0, kc), :], qi_stack)
        score = jnp.zeros((kc, tq), F32)
        for h in range(IDX_HEADS):
            score = score + wT[h:h + 1, :] * jnp.maximum(sc[:, h * tq:(h + 1) * tq], 0.0)
        key = jnp.where(k0 + keypos <= qpos, _sort_key(score), INT_MIN)
        keys_ref[c] = key
        flipped = key ^ INT_MIN
        for lvl in range(4):
            byte = lax.shift_right_logical(flipped, 8 * lvl) & 0xFF
            planes_ref[lvl, c] = byte.astype(F32).astype(BF16)
        return carry

    lax.fori_loop(0, n_chunks, score_chunk, 0)

    one = jnp.ones((kc, tq), BF16)
    zero = jnp.zeros((kc, tq), BF16)
    dead = jnp.full((kc, tq), -1.0, BF16)

    def count_where(lvl, pick, also=None):
        def body(c, acc):
            p = planes_ref[lvl, c]
            if also is not None:
                also(c, p)
            return acc + _fold_rows_packed(jnp.where(pick(p), one, zero)).astype(F32)
        acc = lax.fori_loop(0, n_chunks, body, jnp.zeros((PACKED_ROWS, tq), F32))
        return jnp.sum(acc, axis=0, keepdims=True)

    above = jnp.zeros((1, tq), F32)
    thr_u = jnp.zeros((1, tq), I32)
    for lvl in (3, 2, 1, 0):
        def bit_step(it, t, lvl=lvl, above=above):
            cand = t + jnp.left_shift(jnp.int32(1), 7 - it).astype(F32)
            cnt = above + count_where(lvl, lambda p: p >= cand.astype(BF16))
            return jnp.where(cnt >= float(n_keep), cand, t)

        t = lax.fori_loop(0, 8, bit_step, jnp.zeros((1, tq), F32))
        tb = t.astype(BF16)

        def narrow(c, p, lvl=lvl, tb=tb):
            planes_ref[lvl - 1, c] = jnp.where(p == tb, planes_ref[lvl - 1, c], dead)

        above = above + count_where(lvl, lambda p: p > tb, narrow if lvl > 0 else None)
        thr_u = thr_u | jnp.left_shift(t.astype(I32), 8 * lvl)
    thr = thr_u ^ INT_MIN
    need = float(n_keep) - above
    tri = _tri_strict_lower(LANES, BF16)
    sub = lax.broadcasted_iota(I32, (LANES, tq), 0)
    qsub = t0 + lax.broadcasted_iota(I32, (LANES, tq), 1)

    def bias_chunk(c, offset):
        for j in range(kc // LANES):
            rows = slice(j * LANES, (j + 1) * LANES)
            sel, offset = _select_mask(keys_ref[c, rows, :], thr, need, offset, tri)
            causal = c * kc + j * LANES + sub <= qsub
            bias_ref[c, rows, :] = jnp.where(causal, (sel - 1.0) * 1e30, NEG)
        return offset

    lax.fori_loop(0, n_chunks, bias_chunk, jnp.zeros((1, tq), F32))

    accs = _flash_loop(qa_ref, ka_ref, vT_ref, lambda g, c: bias_ref[c], n_chunks, tq, kc)
    for heads, acc in accs:
        _store_heads(o_ref, _normalise(acc), heads, tq)


def _dsa(qi, wT, ki, qa, ka, vaT, bsz, seq, tq, kc):
    n_keep = min(DSA_TOPK_MAX, seq // 4)
    nq = seq // tq
    qblk = lambda w: pl.BlockSpec((1, tq, w), lambda b, i: (b, i, 0))
    full = lambda w: pl.BlockSpec((1, seq, w), lambda b, i: (b, 0, 0))
    return pl.pallas_call(
        functools.partial(_dsa_kernel, tq=tq, kc=kc, n_keep=n_keep),
        out_shape=jax.ShapeDtypeStruct((bsz, seq, A_HEADS * HEAD_DIM), BF16),
        grid=(bsz, nq),
        in_specs=[qblk(1024), pl.BlockSpec((IDX_HEADS, tq), lambda b, i: (0, b * nq + i)), full(LANES),
                  qblk(1024), full(LANES),
                  pl.BlockSpec((A_KV_HEADS, seq // kc, LANES, kc), lambda b, i: (0, b, 0, 0))],
        out_specs=qblk(512),
        scratch_shapes=[pltpu.VMEM((seq // kc, kc, tq), I32), pltpu.VMEM((seq // kc, kc, tq), F32),
                        pltpu.VMEM((4, seq // kc, kc, tq), BF16)],
        compiler_params=_cparams(("parallel", "parallel")),
        name="dsa",
    )(qi, wT, ki, qa, ka, vaT)


def _gelu_tanh(x):
    return 0.5 * x * (1.0 + jnp.tanh(math.sqrt(2.0 / math.pi) * (x + 0.044715 * (x * x * x))))


def _compress_kernel(k_ref, v_ref, pk_ref, pv_ref, w1k_ref, w2k_ref, w1v_ref, w2v_ref,
                     kc_ref, vcT_ref, *, n_rows):
    half = CMP_BLOCK // 2

    def one(x_ref, p_ref, w1_ref, w2_ref):
        outs = []
        for g in range(B_KV_HEADS):
            lo = jnp.zeros((n_rows, CMP_HIDDEN), F32)
            hi = jnp.zeros((n_rows, CMP_HIDDEN), F32)
            for l in range(half):
                xl = x_ref[0, pl.ds(l, n_rows, stride=CMP_STRIDE), :][:, g * HEAD_DIM:(g + 1) * HEAD_DIM]
                a = (xl + p_ref[l:l + 1, :]).astype(BF16)
                b = (xl + p_ref[half + l:half + l + 1, :]).astype(BF16)
                lo = lo + _dot(a, w1_ref[l * HEAD_DIM:(l + 1) * HEAD_DIM, :].astype(BF16))
                hi = hi + _dot(b, w1_ref[(half + l) * HEAD_DIM:(half + l + 1) * HEAD_DIM, :].astype(BF16))
            hid = lo + pltpu.roll(hi, n_rows - 1, axis=0)
            outs.append(_dot(_gelu_tanh(hid).astype(BF16), w2_ref[...].astype(BF16)))
        return outs

    k0, k1 = one(k_ref, pk_ref, w1k_ref, w2k_ref)
    kc_ref[0] = jnp.concatenate([k0, k1], axis=1).astype(kc_ref.dtype)
    for g, v in enumerate(one(v_ref, pv_ref, w1v_ref, w2v_ref)):
        vcT_ref[0, g] = jnp.concatenate([v, jnp.zeros_like(v)], axis=1).T.astype(vcT_ref.dtype)


def _compress(kcmp, vcmp, pos_k, pos_v, w1k, w2k, w1v, w2v):
    bsz, seq, width = kcmp.shape
    n_rows = seq // CMP_STRIDE
    xspec = pl.BlockSpec((1, seq, width), lambda b: (b, 0, 0))
    cst = lambda a: pl.BlockSpec(a.shape, lambda b: (0,) * a.ndim)
    return pl.pallas_call(
        functools.partial(_compress_kernel, n_rows=n_rows),
        out_shape=(jax.ShapeDtypeStruct((bsz, n_rows, LANES), BF16),
                   jax.ShapeDtypeStruct((bsz, B_KV_HEADS, LANES, n_rows), BF16)),
        grid=(bsz,),
        in_specs=[xspec, xspec, cst(pos_k), cst(pos_v), cst(w1k), cst(w2k), cst(w1v), cst(w2v)],
        out_specs=(pl.BlockSpec((1, n_rows, LANES), lambda b: (b, 0, 0)),
                   pl.BlockSpec((1, B_KV_HEADS, LANES, n_rows), lambda b: (b, 0, 0, 0))),
        compiler_params=_cparams(("parallel",)),
        name="compress",
    )(kcmp, vcmp, pos_k, pos_v, w1k, w2k, w1v, w2v)


def _split3(x):
    a = x.astype(BF16)
    r = x - a.astype(F32)
    b = r.astype(BF16)
    c = (r - b.astype(F32)).astype(BF16)
    return a, b, c


def _nsa_kernel(qraw_ref, qrot_ref, kc_ref, vcT_ref, ksel_ref, vselT_ref, kwin_ref, vwinT_ref,
                gT_ref, exp_ref, o_ref, bias_ref, *, tq, kc, seq, n_c):
    t0 = pl.program_id(1) * tq
    n_chunks = (t0 + tq + kc - 1) // kc
    n_s = seq // SEL_BLOCK
    n_pick = min(SEL_COUNT, n_s)
    wt = min(tq, WIN_TQ)
    span = WINDOW + wt
    gT = gT_ref[...]

    ridx = lax.broadcasted_iota(I32, (LANES, tq), 0)
    tl = t0 + lax.broadcasted_iota(I32, (LANES, tq), 1)
    valid_c = jnp.where((ridx * CMP_STRIDE + CMP_BLOCK - 1 <= tl) & (ridx < n_c), 1.0, 0.0)
    valid4 = jnp.concatenate([valid_c] * REP, axis=1) > 0.0
    js = lax.broadcasted_iota(I32, (LANES, LANES), 0) * SEL_BLOCK
    cs = lax.broadcasted_iota(I32, (LANES, LANES), 1) * CMP_STRIDE
    overlap_t = jnp.where((cs <= js + SEL_BLOCK - 1) & (cs + CMP_BLOCK - 1 >= js), 1.0, 0.0).astype(BF16)
    cur = tl // SEL_BLOCK
    forced = (ridx == 0) | ((cur - ridx >= 0) & (cur - ridx < SEL_LOCAL))
    blk_causal = ridx * SEL_BLOCK <= tl
    tri = _tri_strict_lower(LANES, BF16)
    keypos = lax.broadcasted_iota(I32, (kc, tq), 0)
    qpos = t0 + lax.broadcasted_iota(I32, (kc, tq), 1)

    o_cmp_g = []
    for g in range(B_KV_HEADS):
        heads = range(g * REP, (g + 1) * REP)
        s = jnp.where(valid4, _dot_nt(kc_ref[0], _stack_heads(qraw_ref, heads)), NEG)
        e = jnp.where(valid4, jnp.exp(s - _col_max(s)), 0.0)
        p = e * (1.0 / jnp.maximum(_col_sum(e), 1e-30))
        o_cmp_g.append(_dot(vcT_ref[0, g], p.astype(BF16)))
        p_sum = p[:, 0:tq]
        for r in range(1, REP):
            p_sum = p_sum + p[:, r * tq:(r + 1) * tq]
        pa, pb, pc = _split3(p_sum)
        imp = _dot(overlap_t, pa) + _dot(overlap_t, pb) + _dot(overlap_t, pc)
        imp = jnp.where(forced, FORCE, jnp.where(blk_causal, imp, NEG))
        keys = _sort_key(imp[0:n_s, :])
        count_ge = lambda cand, keys=keys: _col_sum(jnp.where(keys >= cand, 1.0, 0.0))
        thr = _kth_largest_key(count_ge, (1, tq), n_pick)
        need = float(n_pick) - _col_sum(jnp.where(keys > thr, 1.0, 0.0))
        blk_sel, _ = _select_mask(keys, thr, need, jnp.zeros((1, tq), F32), tri[0:n_s, 0:n_s])
        blk_sel = jnp.concatenate([blk_sel, jnp.zeros((LANES - n_s, tq), F32)], axis=0).astype(BF16)

        def bias_chunk(c, carry, blk_sel=blk_sel, g=g):
            tok_sel = _dot(exp_ref[c], blk_sel)
            bias_ref[g, c] = jnp.where(c * kc + keypos <= qpos, (tok_sel - 1.0) * 1e30, NEG)
            return carry

        lax.fori_loop(0, n_chunks, bias_chunk, 0)

    sel = _flash_loop(qrot_ref, ksel_ref, vselT_ref, lambda g, c: bias_ref[g, c], n_chunks, tq, kc)

    for (g, heads), (_, acc_sel) in zip(_head_chains(), sel):
        o_win_sub = []
        for sub in range(tq // wt):
            t_sub = t0 + sub * wt
            w0 = pl.multiple_of(jnp.clip(t_sub - WINDOW, 0, seq - span), wt)
            wdiff = ((t_sub + lax.broadcasted_iota(I32, (span, wt), 1))
                     - (w0 + lax.broadcasted_iota(I32, (span, wt), 0)))
            wbias = jnp.where((wdiff >= 0) & (wdiff < WINDOW), 0.0, NEG)
            q_sub = jnp.concatenate([qrot_ref[0, sub * wt:(sub + 1) * wt, h * LANES:(h + 1) * LANES]
                                     for h in heads], axis=0)
            sw = (_dot_nt(kwin_ref[0, pl.ds(w0, span), :], q_sub)
                  + jnp.concatenate([wbias] * HEADS_PER_CHAIN, axis=1))
            ew = jnp.exp(sw - _col_max(sw)).astype(BF16)
            wblk = w0 // LANES
            acc_win = _dot(vwinT_ref[g, wblk], ew[0:LANES, :])
            for j in range(1, span // LANES):
                acc_win = acc_win + _dot(vwinT_ref[g, wblk + j], ew[j * LANES:(j + 1) * LANES, :])
            o_win_sub.append(_normalise(acc_win))

        o_sel = _normalise(acc_sel)
        cols = []
        for r, h in enumerate(heads):
            col = slice(r * tq, (r + 1) * tq)
            cmp_col = slice((h % REP) * tq, (h % REP + 1) * tq)
            o_win = jnp.concatenate([o[:, r * wt:(r + 1) * wt] for o in o_win_sub], axis=1)
            cols.append(gT[3 * h:3 * h + 1, :] * o_cmp_g[g][:, cmp_col]
                        + gT[3 * h + 1:3 * h + 2, :] * o_sel[:, col]
                        + gT[3 * h + 2:3 * h + 3, :] * o_win)
        _store_heads(o_ref, jnp.concatenate(cols, axis=1), heads, tq)


def _nsa(qraw, qrot, kc_x, vcT, ksel, vselT, kwin, vwinT, gT, bsz, seq, tq, kc):
    n_c = (seq - CMP_BLOCK) // CMP_STRIDE + 1
    nq = seq // tq
    key_blk = (jnp.arange(seq, dtype=I32) // SEL_BLOCK).reshape(seq // kc, kc, 1)
    expand = (key_blk == jnp.arange(LANES, dtype=I32).reshape(1, 1, LANES)).astype(BF16)
    qblk = lambda w: pl.BlockSpec((1, tq, w), lambda b, i: (b, i, 0))
    full = lambda w: pl.BlockSpec((1, seq, w), lambda b, i: (b, 0, 0))
    g = B_KV_HEADS
    return pl.pallas_call(
        functools.partial(_nsa_kernel, tq=tq, kc=kc, seq=seq, n_c=n_c),
        out_shape=jax.ShapeDtypeStruct((bsz, seq, B_HEADS * HEAD_DIM), BF16),
        grid=(bsz, nq),
        in_specs=[qblk(1024), qblk(1024),
                  pl.BlockSpec((1,) + kc_x.shape[1:], lambda b, i: (b, 0, 0)),
                  pl.BlockSpec((1,) + vcT.shape[1:], lambda b, i: (b, 0, 0, 0)),
                  full(LANES), pl.BlockSpec((g, seq // kc, LANES, kc), lambda b, i: (0, b, 0, 0)),
                  full(LANES), pl.BlockSpec((g, seq // LANES, LANES, LANES), lambda b, i: (0, b, 0, 0)),
                  pl.BlockSpec((N_GATES, tq), lambda b, i: (0, b * nq + i)),
                  pl.BlockSpec(expand.shape, lambda b, i: (0, 0, 0))],
        out_specs=qblk(512),
        scratch_shapes=[pltpu.VMEM((g, seq // kc, kc, tq), F32)],
        compiler_params=_cparams(("parallel", "parallel")),
        name="nsa",
    )(qraw, qrot, kc_x, vcT, ksel, vselT, kwin, vwinT, gT, expand)


def _pack_pairs(x):
    n = x.shape[1] // 2
    lo = pltpu.bitcast(x[:, :n].astype(BF16).astype(F32), I32)
    hi = pltpu.bitcast(x[:, n:].astype(BF16).astype(F32), I32)
    return lax.shift_right_logical(lo, 16) | (hi & jnp.int32(-65536))


def _unpack_pairs(p):
    lo = pltpu.bitcast(lax.shift_left(p, 16), F32)
    hi = pltpu.bitcast(p & jnp.int32(-65536), F32)
    return jnp.concatenate([lo, hi], axis=1)


def _layer_norm(y, g, b):
    mu = jnp.mean(y, axis=1, keepdims=True)
    yc = y - mu
    var = jnp.mean(yc * yc, axis=1, keepdims=True)
    return yc * lax.rsqrt(var + LN_EPS) * g + b


def _out_proj_kernel(oa_ref, ob_ref, ga_ref, gb_ref, x_ref, mod_ref, wa_ref, wb_ref, wo_ref,
                     g1_ref, b1_ref, wrh_ref, wrl_ref, x1_ref, u2_ref, lg_ref):
    mod = mod_ref[0]
    merged = (ga_ref[...].astype(F32) * _dot(oa_ref[...], wa_ref[...])
              + gb_ref[...].astype(F32) * _dot(ob_ref[...], wb_ref[...]))
    mix = _dot(merged.astype(BF16), wo_ref[...])
    x1 = _layer_norm(DN_ALPHA * x_ref[...] + mod[2:3, :] * mix, g1_ref[...], b1_ref[...])
    x1_ref[...] = x1
    u2 = x1 * (1.0 + mod[4:5, :]) + mod[3:4, :]
    packed = _pack_pairs(u2)
    for j in range(PIECES):
        u2_ref[j] = packed[:, j * SC_ROW:(j + 1) * SC_ROW]
    uh = u2.astype(BF16)
    ul = (u2 - uh.astype(F32)).astype(BF16)
    lg_ref[...] = _dot_nt(wrh_ref[...], uh) + _dot_nt(wrh_ref[...], ul) + _dot_nt(wrl_ref[...], uh)


def _out_proj(oa, ob, ga, gb, x2, mod3, wa, wb, wo, g1, b1, wrh, wrl, seq, tm):
    t, d = x2.shape
    per_b = seq // tm
    row = lambda w: pl.BlockSpec((tm, w), lambda i: (i, 0))
    cst = lambda a: pl.BlockSpec(a.shape, lambda i: (0,) * a.ndim)
    return pl.pallas_call(
        _out_proj_kernel,
        out_shape=(jax.ShapeDtypeStruct((t, d), F32), jax.ShapeDtypeStruct((PIECES, t, SC_ROW), I32),
                   jax.ShapeDtypeStruct((N_EXPERTS, t), F32)),
        grid=(t // tm,),
        in_specs=[row(512), row(512), row(d), row(d), row(d),
                  pl.BlockSpec((1, 6, d), lambda i: (i // per_b, 0, 0)),
                  cst(wa), cst(wb), cst(wo), cst(g1), cst(b1), cst(wrh), cst(wrl)],
        out_specs=(row(d), pl.BlockSpec((PIECES, tm, SC_ROW), lambda i: (0, i, 0)),
                   pl.BlockSpec((N_EXPERTS, tm), lambda i: (0, i))),
        compiler_params=_cparams(("parallel",)),
        name="out_proj",
    )(oa, ob, ga, gb, x2, mod3, wa, wb, wo, g1, b1, wrh, wrl)


def _first_max(x, rows):
    m = jnp.max(x, axis=0, keepdims=True)
    idx = jnp.min(jnp.where(x == m, rows, 1e9), axis=0, keepdims=True)
    return m, idx


def _router_kernel(lg_ref, rb_ref, idx_ref, w_ref, pos_ref, cnt_ref, carry_ref, *, tm):
    @pl.when(pl.program_id(0) == 0)
    def _():
        carry_ref[...] = jnp.zeros_like(carry_ref)

    per_g = N_EXPERTS // N_GROUPS
    scores = _sigmoid(lg_ref[...])
    choice = scores + rb_ref[...][:, 0:1]
    rows = lax.broadcasted_iota(I32, (N_EXPERTS, tm), 0).astype(F32)
    rows_g = lax.broadcasted_iota(I32, (per_g, tm), 0).astype(F32)
    ninf = -jnp.inf

    gs = []
    for g in range(N_GROUPS):
        x = choice[g * per_g:(g + 1) * per_g, :]
        m1, i1 = _first_max(x, rows_g)
        m2 = jnp.max(jnp.where(rows_g == i1, ninf, x), axis=0, keepdims=True)
        gs.append(m1 + m2)
    gscore = jnp.concatenate(gs, axis=0)
    rows8 = lax.broadcasted_iota(I32, (N_GROUPS, tm), 0).astype(F32)
    keep = jnp.zeros((N_GROUPS, tm), F32)
    for _ in range(TOPK_GROUPS):
        _, gi = _first_max(gscore, rows8)
        hit = rows8 == gi
        keep = jnp.where(hit, 1.0, keep)
        gscore = jnp.where(hit, ninf, gscore)
    keep_full = jnp.concatenate(
        [jnp.broadcast_to(keep[g:g + 1, :], (per_g, tm)) for g in range(N_GROUPS)], axis=0)
    masked = jnp.where(keep_full > 0.0, choice, NEG)

    idxs, ws = [], []
    onehot = jnp.zeros((N_EXPERTS, tm), F32)
    for _ in range(TOP_K):
        _, ei = _first_max(masked, rows)
        hit = rows == ei
        idxs.append(ei)
        ws.append(jnp.sum(jnp.where(hit, scores, 0.0), axis=0, keepdims=True))
        masked = jnp.where(hit, ninf, masked)
        onehot = jnp.where(hit, 1.0, onehot)
    idx = jnp.concatenate(idxs, axis=0)
    w = jnp.concatenate(ws, axis=0)
    idx_ref[...] = idx.astype(I32)
    w = w / jnp.sum(w, axis=0, keepdims=True) * ROUTED_SCALE
    w_ref[...] = jnp.concatenate([w, jnp.zeros((LANES - TOP_K, tm), F32)], axis=0).T

    tri = _tri_strict_upper(tm, BF16)
    base = _dot(onehot.astype(BF16), tri) + carry_ref[...][:, 0:1]
    pos = [jnp.sum(jnp.where(rows == idxs[k], base, 0.0), axis=0, keepdims=True) for k in range(TOP_K)]
    pos_ref[...] = jnp.concatenate(pos, axis=0).astype(I32)
    carry = carry_ref[...] + jnp.sum(onehot, axis=1, keepdims=True)
    carry_ref[...] = carry
    cnt_ref[...] = carry


def _router(lg, router_bias, tm):
    e, t = lg.shape
    rb = jnp.broadcast_to(router_bias.reshape(e, 1).astype(F32), (e, LANES))
    tok = lambda r: pl.BlockSpec((r, tm), lambda i: (0, i))
    return pl.pallas_call(
        functools.partial(_router_kernel, tm=tm),
        out_shape=(jax.ShapeDtypeStruct((TOP_K, t), I32), jax.ShapeDtypeStruct((t, LANES), F32),
                   jax.ShapeDtypeStruct((TOP_K, t), I32), jax.ShapeDtypeStruct((e, LANES), F32)),
        grid=(t // tm,),
        in_specs=[tok(e), pl.BlockSpec((e, LANES), lambda i: (0, 0))],
        out_specs=(tok(TOP_K), pl.BlockSpec((tm, LANES), lambda i: (i, 0)), tok(TOP_K),
                   pl.BlockSpec((e, LANES), lambda i: (0, 0))),
        scratch_shapes=[pltpu.VMEM((e, LANES), F32)],
        compiler_params=_cparams(("arbitrary",)),
        name="router",
    )(lg, rb)


def _dest_kernel(idx_ref, pos_ref, st_ref, dest_ref, *, tm):
    rows = lax.broadcasted_iota(I32, (N_EXPERTS, tm), 0)
    starts = st_ref[...][:, 0:1]
    idx = idx_ref[...]
    out = []
    for k in range(TOP_K):
        out.append(jnp.sum(jnp.where(rows == idx[k:k + 1, :], starts, 0.0), axis=0, keepdims=True))
    dest_ref[...] = jnp.concatenate(out, axis=0).astype(I32) + pos_ref[...]


def _dest(idx, pos, starts, tm):
    k, t = idx.shape
    st = jnp.broadcast_to(starts.reshape(N_EXPERTS, 1).astype(F32), (N_EXPERTS, LANES))
    tok = pl.BlockSpec((k, tm), lambda i: (0, i))
    return pl.pallas_call(
        functools.partial(_dest_kernel, tm=tm),
        out_shape=jax.ShapeDtypeStruct((k, t), I32),
        grid=(t // tm,),
        in_specs=[tok, tok, pl.BlockSpec((N_EXPERTS, LANES), lambda i: (0, 0))],
        out_specs=tok,
        compiler_params=_cparams(("parallel",)),
        name="dest",
    )(idx, pos, st)


def _experts_kernel(blk_ref, used_ref, first_ref, slot_ref, next_ref, x_ref, wg_hbm, wu_hbm, wd_hbm, y_ref,
                    wg_buf, wu_buf, wd_buf, wg_bf, wu_bf, wd_bf, sems):
    b = pl.program_id(0)
    active = b < used_ref[0]

    def weight_copies(e, slot):
        return [pltpu.make_async_copy(hbm.at[e], buf.at[slot], sems.at[slot, i])
                for i, (hbm, buf) in enumerate(((wg_hbm, wg_buf), (wu_hbm, wu_buf), (wd_hbm, wd_buf)))]

    @pl.when(b == 0)
    def _():
        for cp in weight_copies(blk_ref[0], 0):
            cp.start()

    @pl.when(active & (first_ref[b] == 1))
    def _():
        slot = slot_ref[b]
        for cp in weight_copies(blk_ref[b], slot):
            cp.wait()
        nxt = next_ref[b]

        @pl.when(nxt >= 0)
        def _():
            for cp in weight_copies(nxt, 1 - slot):
                cp.start()

        wg_bf[...] = wg_buf[slot].astype(BF16)
        wu_bf[...] = wu_buf[slot].astype(BF16)
        wd_bf[...] = wd_buf[slot].astype(BF16)

    @pl.when(active)
    def _():
        x = jnp.concatenate([x_ref[j] for j in range(PIECES)], axis=1)
        x = _unpack_pairs(x).astype(BF16)
        a = _dot(x, wg_bf[...])
        u = _dot(x, wu_bf[...])
        h = (a * _sigmoid(a) * u).astype(BF16)
        y = _pack_pairs(_dot(h, wd_bf[...]))
        for j in range(PIECES):
            y_ref[j] = y[:, j * SC_ROW:(j + 1) * SC_ROW]


def _experts(xs, plan, wg, wu, wd, bm):
    _, cap, _ = xs.shape
    n_blocks = cap // bm
    d, f = wg.shape[1], wg.shape[2]
    rows = pl.BlockSpec((PIECES, bm, SC_ROW), lambda b, blk, used, *_: (0, jnp.minimum(b, used[0] - 1), 0))
    hbm = pl.BlockSpec(memory_space=pl.ANY)
    return pl.pallas_call(
        _experts_kernel,
        out_shape=jax.ShapeDtypeStruct(xs.shape, I32),
        grid_spec=pltpu.PrefetchScalarGridSpec(
            num_scalar_prefetch=5,
            grid=(n_blocks,),
            in_specs=[rows, hbm, hbm, hbm],
            out_specs=rows,
            scratch_shapes=[pltpu.VMEM((2, d, f), F32), pltpu.VMEM((2, d, f), F32), pltpu.VMEM((2, f, d), F32),
                            pltpu.VMEM((d, f), BF16), pltpu.VMEM((d, f), BF16), pltpu.VMEM((f, d), BF16),
                            pltpu.SemaphoreType.DMA((2, 3))]),
        compiler_params=_cparams(("arbitrary",)),
        name="experts",
    )(plan["blk_e"], plan["n_used"], plan["first"], plan["slot"], plan["next_e"], xs, wg, wu, wd)


def _shared_kernel(u2_ref, sg_ref, su_ref, sd_ref, o_ref):
    u = _unpack_pairs(jnp.concatenate([u2_ref[j] for j in range(PIECES)], axis=1)).astype(BF16)
    a = _dot(u, sg_ref[...])
    b = _dot(u, su_ref[...])
    o_ref[...] = _dot((a * _sigmoid(a) * b).astype(BF16), sd_ref[...]).astype(o_ref.dtype)


def _shared(u2p, sg, su, sd, tm):
    _, t, _ = u2p.shape
    d = sd.shape[1]
    cst = lambda a: pl.BlockSpec(a.shape, lambda i: (0,) * a.ndim)
    return pl.pallas_call(
        _shared_kernel,
        out_shape=jax.ShapeDtypeStruct((t, d), BF16),
        grid=(t // tm,),
        in_specs=[pl.BlockSpec((PIECES, tm, SC_ROW), lambda i: (0, i, 0)), cst(sg), cst(su), cst(sd)],
        out_specs=pl.BlockSpec((tm, d), lambda i: (i, 0)),
        compiler_params=_cparams(("parallel",)),
        name="shared",
    )(u2p, sg, su, sd)


def _final_kernel(x1_ref, sh_ref, yg_ref, w_ref, mod_ref, g2_ref, b2_ref, o_ref):
    mod = mod_ref[0]
    w = w_ref[...]
    rows = lambda ref, *lead: jnp.concatenate([ref[(j,) + lead] for j in range(PIECES)], axis=1)
    routed = w[:, 0:1] * _unpack_pairs(rows(yg_ref, 0))
    for k in range(1, TOP_K):
        routed = routed + w[:, k:k + 1] * _unpack_pairs(rows(yg_ref, k))
    y = DN_ALPHA * x1_ref[...] + mod[5:6, :] * (routed + sh_ref[...].astype(F32))
    o_ref[...] = _layer_norm(y, g2_ref[...], b2_ref[...])


def _final(x1, shared, yg, wtok, mod3, g2, b2, seq, tm):
    t, d = x1.shape
    per_b = seq // tm
    row = lambda w: pl.BlockSpec((tm, w), lambda i: (i, 0))
    cst = lambda a: pl.BlockSpec(a.shape, lambda i: (0,) * a.ndim)
    return pl.pallas_call(
        _final_kernel,
        out_shape=jax.ShapeDtypeStruct((t, d), F32),
        grid=(t // tm,),
        in_specs=[row(d), row(d),
                  pl.BlockSpec((PIECES, TOP_K, tm, SC_ROW), lambda i: (0, 0, i, 0)),
                  row(LANES), pl.BlockSpec((1, 6, d), lambda i: (i // per_b, 0, 0)),
                  cst(g2), cst(b2)],
        out_specs=row(d),
        compiler_params=_cparams(("parallel",)),
        name="final",
    )(x1, shared, yg, wtok, mod3, g2, b2)


def _sc_mesh():
    return plsc.VectorSubcoreMesh(core_axis_name="core", subcore_axis_name="subcore")


def _sc_scatter_rows(src, idx, n_out):
    n_copies, n = idx.shape

    @functools.partial(pl.kernel, out_type=jax.ShapeDtypeStruct((n_out, SC_ROW), src.dtype),
                       mesh=_sc_mesh(), scratch_types=[])
    def k(x_hbm, i_hbm, o_hbm):
        def body(x_vmem, i_vmem):
            for c in range(n_copies):
                pltpu.sync_copy(x_vmem, o_hbm.at[i_vmem.at[c]])

        pltpu.emit_pipeline(
            body, grid=(n // SC_WINDOW,),
            in_specs=[pl.BlockSpec((SC_WINDOW, SC_ROW), lambda i: (i, 0)),
                      pl.BlockSpec((n_copies, SC_WINDOW), lambda i: (0, i))],
            out_specs=[],
            core_axis_name=("core", "subcore"),
            dimension_semantics=(pltpu.PARALLEL,),
        )(x_hbm, i_hbm)

    return k(src, idx)


def _sc_gather_rows(src, idx):
    n_idx = idx.shape[0]

    @functools.partial(pl.kernel, out_type=jax.ShapeDtypeStruct((n_idx, SC_ROW), src.dtype),
                       mesh=_sc_mesh(), scratch_types=[])
    def k(x_hbm, i_hbm, o_hbm):
        def body(i_vmem, o_vmem):
            pltpu.sync_copy(x_hbm.at[i_vmem.at[0]], o_vmem)

        pltpu.emit_pipeline(
            body, grid=(n_idx // SC_WINDOW,),
            in_specs=[pl.BlockSpec((1, SC_WINDOW), lambda i: (0, i))],
            out_specs=[pl.BlockSpec((SC_WINDOW, SC_ROW), lambda i: (i, 0))],
            core_axis_name=("core", "subcore"),
            dimension_semantics=(pltpu.PARALLEL,),
        )(i_hbm, o_hbm)

    return k(src, idx.reshape(1, n_idx))


def _moe_plan(counts, n_tok):
    bm = BM_EXPERT
    padded = (counts + bm - 1) // bm * bm
    p_ends = jnp.cumsum(padded)
    starts = p_ends - padded
    n_blocks = n_tok * TOP_K // bm + N_EXPERTS
    blk = jnp.arange(n_blocks, dtype=I32)
    blk_e = jnp.minimum(jnp.sum(p_ends[None, :] <= (blk * bm)[:, None], axis=1), N_EXPERTS - 1).astype(I32)
    n_used = (p_ends[-1] // bm).astype(I32)
    prev = jnp.concatenate([jnp.full((1,), -1, I32), blk_e[:-1]])
    first = ((blk_e != prev) & (blk < n_used)).astype(I32)
    slot = (jnp.cumsum(first) - 1) % 2
    first_pos = jnp.where(first == 1, blk, n_blocks)
    next_first = lax.cummin(jnp.concatenate([first_pos[1:], jnp.full((1,), n_blocks, I32)]), reverse=True)
    next_e = jnp.where(next_first < n_blocks, blk_e[jnp.minimum(next_first, n_blocks - 1)], -1)
    plan = dict(blk_e=blk_e, n_used=n_used.reshape(1), first=first, slot=slot.astype(I32),
                next_e=next_e.astype(I32))
    return starts, plan, n_blocks


def _layer(x, mod, positions, w_in, w_br_a, w_br_b, w_out, cmp_pos_k, cmp_pos_v, cmp_k_w1, cmp_k_w2,
           cmp_v_w1, cmp_v_w2, ln1_g, ln1_b, w_router, router_bias, w_exp_gate, w_exp_up, w_exp_down,
           w_sh_gate, w_sh_up, w_sh_down, ln2_g, ln2_b):
    bsz, seq, d = x.shape
    t = bsz * seq
    assert seq // CMP_STRIDE == LANES and seq % TQ_ATTN == 0
    assert seq % TM_PROJ == 0 and KC_ATTN == TM_PROJ
    x2 = x.reshape(t, d)
    mod3 = mod.reshape(bsz, 6, d)

    w_pack, w_small = _pack_w_in(w_in)
    z = _in_proj(x2, mod3, w_pack, w_small, _rope_table(positions), seq, TM_PROJ)
    per_b = lambda name: z[name].reshape(bsz, seq, z[name].shape[1])

    o_a = _dsa(per_b("qi"), z["wT"], per_b("ki"), per_b("qa"), per_b("ka"), z["vaT"],
               bsz, seq, TQ_ATTN, KC_ATTN)

    kc_x, vcT = _compress(per_b("kcmp"), per_b("vcmp"),
                          cmp_pos_k, cmp_pos_v, cmp_k_w1, cmp_k_w2, cmp_v_w1, cmp_v_w2)
    o_b = _nsa(per_b("qbraw"), per_b("qbrot"), kc_x, vcT, per_b("ksel"), z["vselT"], per_b("kwin"),
               z["vwinT"], z["gT"], bsz, seq, TQ_ATTN, KC_ATTN)

    wr_hi = w_router.T.astype(BF16)
    wr_lo = (w_router.T - wr_hi.astype(F32)).astype(BF16)
    x1, u2p, logits = _out_proj(
        o_a.reshape(t, -1), o_b.reshape(t, -1), z["ga"], z["gb"], x2, mod3,
        w_br_a.astype(BF16), w_br_b.astype(BF16), w_out.astype(BF16),
        ln1_g.reshape(1, d), ln1_b.reshape(1, d), wr_hi, wr_lo, seq, TM_PROJ)

    idx, wtok, pos, counts = _router(logits, router_bias, TM_ROUTE)
    starts, plan, n_blocks = _moe_plan(counts[:, 0].astype(I32), t)
    dest = _dest(idx, pos, starts, TM_ROUTE)

    cap = n_blocks * BM_EXPERT
    dest_p = dest[None] + (jnp.arange(PIECES, dtype=I32) * cap).reshape(PIECES, 1, 1)
    xs = _sc_scatter_rows(u2p.reshape(PIECES * t, SC_ROW),
                          jnp.swapaxes(dest_p, 0, 1).reshape(TOP_K, PIECES * t), cap * PIECES)
    ys = _experts(xs.reshape(PIECES, cap, SC_ROW), plan, w_exp_gate, w_exp_up, w_exp_down, BM_EXPERT)
    yg = _sc_gather_rows(ys.reshape(cap * PIECES, SC_ROW), dest_p.reshape(-1)
                         ).reshape(PIECES, TOP_K, t, SC_ROW)

    shared = _shared(u2p, w_sh_gate.astype(BF16), w_sh_up.astype(BF16), w_sh_down.astype(BF16), TM_PROJ)
    return _final(x1, shared, yg, wtok, mod3, ln2_g.reshape(1, d), ln2_b.reshape(1, d), seq, TM_PROJ
                  ).reshape(bsz, seq, d)


def kernel(x, c, positions, w_ada, b_ada, w_in, w_br_a, w_br_b, w_out, cmp_pos_k, cmp_pos_v, cmp_k_w1,
           cmp_k_w2, cmp_v_w1, cmp_v_w2, ln1_g, ln1_b, w_router, router_bias, w_exp_gate, w_exp_up,
           w_exp_down, w_sh_gate, w_sh_up, w_sh_down, ln2_g, ln2_b):
    for l in range(w_ada.shape[0]):
        mod = _mod(c, w_ada[l], b_ada[l])
        x = _layer(x, mod, positions, w_in[l], w_br_a[l], w_br_b[l], w_out[l], cmp_pos_k[l], cmp_pos_v[l],
                   cmp_k_w1[l], cmp_k_w2[l], cmp_v_w1[l], cmp_v_w2[l], ln1_g[l], ln1_b[l], w_router[l],
                   router_bias[l], w_exp_gate[l], w_exp_up[l], w_exp_down[l], w_sh_gate[l], w_sh_up[l],
                   w_sh_down[l], ln2_g[l], ln2_b[l])
    return x
```
